```python
import math
import jax, jax.numpy as jnp
from jax import lax
import numpy as np

D_MODEL = 1024
BATCH = 32
SEQ = 256
DEPTH = 1
DEC_BATCH = 8
DEC_SEQ = 4096
PAST_LEN = 256

GRID_W = 64
D_HYENA = 512
D_RET = 512
N_RET_HEADS = 4
RET_HEAD_DIM = D_RET // N_RET_HEADS
RET_CHUNK = 128
SHORT_CONV = 3
FILTER_EMB = 33
FILTER_HIDDEN = 64
N_EXPERTS = 16
EC_CAPACITY_FACTOR = 2
D_EXPERT = 1024
ROPE_BASE = 10000.0
EPS = 1e-6
IN_COLS = 3 * D_HYENA + 4 * D_RET + 2 * D_MODEL

kernel_name = "hybrid_hyena_retention_ec_dit_step"

F32 = jnp.float32


def rmsnorm(x, g):
    xf = x.astype(F32)
    y = xf * lax.rsqrt(jnp.mean(xf * xf, axis=-1, keepdims=True) + EPS)
    return (y * g.astype(F32)).astype(x.dtype)


def adaln(cvec, w_mod, b_mod):
    m = jax.nn.silu(cvec) @ w_mod + b_mod
    return jnp.split(m[..., None, :], 6, axis=-1)


def short_conv(x, w, b):
    L = x.shape[1]
    pad = SHORT_CONV // 2
    xp = jnp.pad(x, ((0, 0), (pad, pad), (0, 0)))
    y = b
    for j in range(SHORT_CONV):
        y = y + xp[:, j:j + L] * w[j]
    return y


def hyena_filters(L, w1, b1, freq, w2, b2, w3, decay):
    t = jnp.linspace(0.0, 1.0, L, dtype=F32)[:, None]
    bands = (FILTER_EMB - 1) // 2
    w = 2.0 * math.pi * jnp.arange(L, dtype=F32) / L
    f = jnp.linspace(1e-4, bands - 1, bands, dtype=F32)
    ang = w[:, None] * f[None, :]
    z = jnp.concatenate([t, jnp.cos(ang), -jnp.sin(ang)], axis=-1)
    fr = freq.astype(F32)
    h = jnp.sin(fr * (z @ w1.astype(F32) + b1.astype(F32)))
    h = jnp.sin(fr * (h @ w2.astype(F32) + b2.astype(F32)))
    h = (h @ w3.astype(F32)) * jnp.exp(-t * jnp.abs(decay.astype(F32)))
    return h[:, :D_HYENA], h[:, D_HYENA:]


def bidir_long_conv(u, h_fwd, h_bwd):
    L, C = h_fwd.shape
    k = jnp.concatenate([h_fwd, jnp.zeros((1, C), F32), h_bwd[:0:-1]], axis=0)
    uf = jnp.fft.rfft(u, n=2 * L, axis=1)
    kf = jnp.fft.rfft(k, n=2 * L, axis=0)
    return jnp.fft.irfft(uf * kf[None], n=2 * L, axis=1)[:, :L]


def hyena_branch(z, p):
    z = short_conv(z, p["hy_conv_w"], p["hy_conv_b"])
    v, x1, x2 = jnp.split(z, 3, axis=-1)
    L = z.shape[1]
    h_fwd, h_bwd = hyena_filters(L, p["hy_f_w1"], p["hy_f_b1"], p["hy_f_freq"], p["hy_f_w2"],
                                 p["hy_f_b2"], p["hy_f_w3"], p["hy_decay"])
    u = (v * x1).astype(F32)
    y = bidir_long_conv(u, h_fwd, h_bwd) + u * p["hy_bias"].astype(F32)
    return (y * x2.astype(F32)).astype(z.dtype)


def rope_2d(x):
    L = x.shape[1]
    rows = L // GRID_W
    rr, cc = jnp.meshgrid(jnp.arange(rows), jnp.arange(GRID_W), indexing="ij")
    row = rr.reshape(L).astype(F32)
    col = cc.reshape(L).astype(F32)
    half = RET_HEAD_DIM // 2
    nf = half // 2
    inv = ROPE_BASE ** (-jnp.arange(nf, dtype=F32) / nf)

    def rot(xp, pos):
        ang = pos[:, None] * inv[None, :]
        cs = jnp.cos(ang)[None, :, None, :]
        sn = jnp.sin(ang)[None, :, None, :]
        a, b = xp[..., :nf], xp[..., nf:]
        return jnp.concatenate([a * cs - b * sn, a * sn + b * cs], axis=-1)

    return jnp.concatenate([rot(x[..., :half], row), rot(x[..., half:], col)], axis=-1)


def retention_scan(q, k, v, log_gamma, s0):
    B, L, H, Dh = q.shape
    C = RET_CHUNK
    N = L // C
    qc = q.reshape(B, N, C, H, Dh)
    kc = k.reshape(B, N, C, H, Dh)
    vc = v.reshape(B, N, C, H, Dh)
    lg = log_gamma.astype(F32)
    pos = jnp.arange(C, dtype=F32)
    diff = pos[:, None] - pos[None, :]
    dmask = jnp.where(diff[None] >= 0, jnp.exp(lg[:, None, None] * jnp.maximum(diff, 0.0)[None]), 0.0)
    scores = jnp.einsum("bnihd,bnjhd->bnhij", qc, kc) * dmask
    inner = jnp.einsum("bnhij,bnjhe->bnihe", scores, vc)
    zeta = jnp.exp(lg[None, :] * (C - 1 - pos)[:, None])
    kv = jnp.einsum("bnjhd,jh,bnjhe->nbhde", kc, zeta, vc)
    chunk_decay = jnp.exp(lg * C)[None, :, None, None]

    def step(s, kv_n):
        return chunk_decay * s + kv_n, s

    s_final, s_prev = lax.scan(step, s0.astype(F32), kv)
    xi = jnp.exp(lg[None, :] * (pos + 1.0)[:, None])
    cross = jnp.einsum("bnihd,ih,nbhde->bnihe", qc, xi, s_prev)
    return (inner + cross).reshape(B, L, H, Dh), s_final


def retention_branch(q, k, v, g, p, s0_f, s0_b, use_rope):
    B, L, _ = q.shape
    sh = (B, L, N_RET_HEADS, RET_HEAD_DIM)
    qf = q.astype(F32).reshape(sh)
    kf = k.astype(F32).reshape(sh) * (RET_HEAD_DIM ** -0.5)
    vf = v.astype(F32).reshape(sh)
    if use_rope:
        qf, kf = rope_2d(qf), rope_2d(kf)
    lg_f = jax.nn.log_sigmoid(p["ret_decay_fwd"].astype(F32))
    lg_b = jax.nn.log_sigmoid(p["ret_decay_bwd"].astype(F32))
    o_f, s_f = retention_scan(qf, kf, vf, lg_f, s0_f)
    o_b, s_b = retention_scan(qf[:, ::-1], kf[:, ::-1], vf[:, ::-1], lg_b, s0_b)
    o = o_f + o_b[:, ::-1]
    mu = jnp.mean(o, axis=-1, keepdims=True)
    var = jnp.mean(jnp.square(o - mu), axis=-1, keepdims=True)
    o = ((o - mu) * lax.rsqrt(var + EPS)).reshape(B, L, D_RET) * p["ret_gn_g"].astype(F32)
    o = o * jax.nn.silu(g.astype(F32))
    return o.astype(q.dtype), s_f, s_b


def mixer(h, p, s0_f, s0_b, use_rope):
    proj = h @ p["w_in"]
    z_hy = proj[..., :3 * D_HYENA]
    q, k, v, g = jnp.split(proj[..., 3 * D_HYENA:3 * D_HYENA + 4 * D_RET], 4, axis=-1)
    gates = jax.nn.sigmoid(proj[..., 3 * D_HYENA + 4 * D_RET:])
    g_hy, g_ret = jnp.split(gates, 2, axis=-1)
    y_hy = hyena_branch(z_hy, p)
    y_ret, s_f, s_b = retention_branch(q, k, v, g, p, s0_f, s0_b, use_rope)
    merged = g_hy * (y_hy @ p["w_hy_o"]) + g_ret * (y_ret @ p["w_ret_o"])
    return merged @ p["w_out"], s_f, s_b


def ec_moe(h, p):
    B, L, D = h.shape
    T = B * L
    xt = h.reshape(T, D)
    aff = jax.nn.softmax((xt @ p["w_router"]).astype(F32), axis=-1)
    cap = (EC_CAPACITY_FACTOR * T) // N_EXPERTS
    vals, idx = lax.top_k(aff.T, cap)
    xs = xt[idx]
    hid = jax.nn.silu(jnp.einsum("ecd,edf->ecf", xs, p["w_e_gate"])) * jnp.einsum("ecd,edf->ecf", xs, p["w_e_up"])
    ye = jnp.einsum("ecf,efd->ecd", hid, p["w_e_down"]) * vals[..., None].astype(h.dtype)
    y = jnp.zeros((T, D), h.dtype).at[idx.reshape(-1)].add(ye.reshape(-1, D))
    return y.reshape(B, L, D)


def trunk_layer(x, mod, p, s0_f, s0_b, use_rope):
    sh1, sc1, g1, sh2, sc2, g2 = mod
    h = rmsnorm(x, p["norm1_g"]) * (1.0 + sc1) + sh1
    out, s_f, s_b = mixer(h, p, s0_f, s0_b, use_rope)
    x = x + g1 * out
    h = rmsnorm(x, p["norm2_g"]) * (1.0 + sc2) + sh2
    x = x + g2 * ec_moe(h, p)
    return x, s_f, s_b


def setup_inputs(seed: int = 0) -> dict:
    key = jax.random.key(seed)
    ks = iter(jax.random.split(key, 40))

    def nrm(shape, scale):
        return jax.random.normal(next(ks), shape, F32) * scale

    st_shape = (DEC_BATCH, DEPTH, N_RET_HEADS, RET_HEAD_DIM, RET_HEAD_DIM)
    gam = 1.0 - 2.0 ** (-5.0 - jnp.arange(N_RET_HEADS, dtype=F32))
    gam_logit = jnp.log(gam / (1.0 - gam))
    decay_base = jnp.linspace(3.07, 15.35, 2 * D_HYENA, dtype=F32)
    return {
        "x_prompt": nrm((BATCH, SEQ, D_MODEL), 1.0),
        "x_sample": nrm((DEC_BATCH, DEC_SEQ, D_MODEL), 1.0),
        "state_ret_fwd": nrm(st_shape, 0.5),
        "state_ret_bwd": nrm(st_shape, 0.5),
        "c": nrm((DEC_BATCH, D_MODEL), 1.0),
        "c_ctx": nrm((D_MODEL,), 1.0),
        "w_mod": nrm((DEPTH, D_MODEL, 6 * D_MODEL), 0.5 * D_MODEL ** -0.5),
        "b_mod": nrm((DEPTH, 6 * D_MODEL), 0.02),
        "norm1_g": 1.0 + nrm((DEPTH, D_MODEL), 0.02),
        "w_in": nrm((DEPTH, D_MODEL, IN_COLS), D_MODEL ** -0.5),
        "hy_conv_w": nrm((DEPTH, SHORT_CONV, 3 * D_HYENA), SHORT_CONV ** -0.5),
        "hy_conv_b": nrm((DEPTH, 3 * D_HYENA), 0.02),
        "hy_f_w1": nrm((DEPTH, FILTER_EMB, FILTER_HIDDEN), FILTER_EMB ** -0.5),
        "hy_f_b1": nrm((DEPTH, FILTER_HIDDEN), 0.1),
        "hy_f_freq": 1.0 + nrm((DEPTH, FILTER_HIDDEN), 0.05),
        "hy_f_w2": nrm((DEPTH, FILTER_HIDDEN, FILTER_HIDDEN), FILTER_HIDDEN ** -0.5),
        "hy_f_b2": nrm((DEPTH, FILTER_HIDDEN), 0.1),
        "hy_f_w3": nrm((DEPTH, FILTER_HIDDEN, 2 * D_HYENA), 0.1 * FILTER_HIDDEN ** -0.5),
        "hy_decay": decay_base[None] * (1.0 + nrm((DEPTH, 2 * D_HYENA), 0.05)),
        "hy_bias": nrm((DEPTH, D_HYENA), 0.5),
        "ret_decay_fwd": gam_logit[None] + nrm((DEPTH, N_RET_HEADS), 0.1),
        "ret_decay_bwd": gam_logit[None] + nrm((DEPTH, N_RET_HEADS), 0.1),
        "ret_gn_g": 1.0 + nrm((DEPTH, D_RET), 0.02),
        "w_hy_o": nrm((DEPTH, D_HYENA, D_MODEL), D_HYENA ** -0.5),
        "w_ret_o": nrm((DEPTH, D_RET, D_MODEL), D_RET ** -0.5),
        "w_out": nrm((DEPTH, D_MODEL, D_MODEL), D_MODEL ** -0.5),
        "norm2_g": 1.0 + nrm((DEPTH, D_MODEL), 0.02),
        "w_router": nrm((DEPTH, D_MODEL, N_EXPERTS), D_MODEL ** -0.5),
        "w_e_gate": nrm((DEPTH, N_EXPERTS, D_MODEL, D_EXPERT), D_MODEL ** -0.5),
        "w_e_up": nrm((DEPTH, N_EXPERTS, D_MODEL, D_EXPERT), D_MODEL ** -0.5),
        "w_e_down": nrm((DEPTH, N_EXPERTS, D_EXPERT, D_MODEL), D_EXPERT ** -0.5),
        "norm_f_g": 1.0 + nrm((D_MODEL,), 0.02),
    }


def reference(x_prompt, x_sample, state_ret_fwd, state_ret_bwd, c, c_ctx, w_mod, b_mod, norm1_g,
              w_in, hy_conv_w, hy_conv_b, hy_f_w1, hy_f_b1, hy_f_freq, hy_f_w2, hy_f_b2, hy_f_w3,
              hy_decay, hy_bias, ret_decay_fwd, ret_decay_bwd, ret_gn_g, w_hy_o, w_ret_o, w_out,
              norm2_g, w_router, w_e_gate, w_e_up, w_e_down, norm_f_g):
    xp = x_prompt
    xs = x_sample
    B = x_prompt.shape[0]
    zero_state = jnp.zeros((B, N_RET_HEADS, RET_HEAD_DIM, RET_HEAD_DIM), F32)
    new_f = []
    new_b = []
    for l in range(DEPTH):
        p = {
            "norm1_g": norm1_g[l], "w_in": w_in[l], "hy_conv_w": hy_conv_w[l], "hy_conv_b": hy_conv_b[l],
            "hy_f_w1": hy_f_w1[l], "hy_f_b1": hy_f_b1[l], "hy_f_freq": hy_f_freq[l],
            "hy_f_w2": hy_f_w2[l], "hy_f_b2": hy_f_b2[l], "hy_f_w3": hy_f_w3[l],
            "hy_decay": hy_decay[l], "hy_bias": hy_bias[l], "ret_decay_fwd": ret_decay_fwd[l],
            "ret_decay_bwd": ret_decay_bwd[l], "ret_gn_g": ret_gn_g[l], "w_hy_o": w_hy_o[l],
            "w_ret_o": w_ret_o[l], "w_out": w_out[l], "norm2_g": norm2_g[l],
            "w_router": w_router[l], "w_e_gate": w_e_gate[l], "w_e_up": w_e_up[l],
            "w_e_down": w_e_down[l],
        }
        mod_ctx = adaln(c_ctx, w_mod[l], b_mod[l])
        mod_lat = adaln(c, w_mod[l], b_mod[l])
        xp, s_f, s_b = trunk_layer(xp, mod_ctx, p, zero_state, zero_state, False)
        new_f.append(s_f)
        new_b.append(s_b)
        xs, _, _ = trunk_layer(xs, mod_lat, p, state_ret_fwd[:, l], state_ret_bwd[:, l], True)
    y_prompt = rmsnorm(xp, norm_f_g)
    y_sample = rmsnorm(xs, norm_f_g)
    new_state_fwd = jnp.stack(new_f, axis=1).astype(x_prompt.dtype)
    new_state_bwd = jnp.stack(new_b, axis=1).astype(x_prompt.dtype)
    return (y_prompt, y_sample, new_state_fwd, new_state_bwd)
```

```python
import functools
import math

import numpy as np
import jax
import jax.numpy as jnp
from jax import lax
from jax.experimental import pallas as pl
from jax.experimental.pallas import tpu as pltpu

F32 = jnp.float32
BF16 = jnp.bfloat16
HIGHEST = lax.Precision.HIGHEST

EPS = 1e-6
D_HYENA = 512
D_RET = 512
N_RET_HEADS = 4
RET_HEAD_DIM = 128
RET_CHUNK = 128
GRID_W = 64
FILTER_EMB = 33
ROPE_BASE = 10000.0
N_EXPERTS = 16
EC_CAPACITY_FACTOR = 2

TOKEN_TILE = 512
SLOT_ROWS = 128
ROW_ALIGN = 16
EXPERT_BLOCK = 512
FFT_MINOR = 64
FFT_GROUP = 4
VMEM_LIMIT = 56 * 1024 * 1024


def _cparams(sem):
    return pltpu.CompilerParams(dimension_semantics=sem, vmem_limit_bytes=VMEM_LIMIT)


def _silu(x):
    return x * jax.nn.sigmoid(x)


def _mod_kernel(c_ref, w_ref, b_ref, o_ref):
    s = _silu(c_ref[...])
    o_ref[...] = jnp.dot(s, w_ref[...], preferred_element_type=F32, precision=HIGHEST) + b_ref[...]


def _modulation(conds, w_mod, b_mod):
    nc, d = conds.shape
    return pl.pallas_call(
        _mod_kernel,
        grid=(6,),
        in_specs=[pl.BlockSpec((nc, d), lambda j: (0, 0)),
                  pl.BlockSpec((d, d), lambda j: (0, j)),
                  pl.BlockSpec((1, d), lambda j: (0, j))],
        out_specs=pl.BlockSpec((nc, d), lambda j: (0, j)),
        out_shape=jax.ShapeDtypeStruct((nc, 6 * d), F32),
        compiler_params=_cparams(("arbitrary",)),
        name="modulation",
    )(conds, w_mod, b_mod.reshape(1, -1))


def _inproj_kernel(x_ref, g_ref, sc_ref, sh_ref, w_ref, z_ref, qkvg_ref, gates_ref, *, splits):
    x = x_ref[...]
    ms = jnp.mean(x * x, axis=-1, keepdims=True)
    h = x * lax.rsqrt(ms + EPS) * g_ref[...]
    h = h * (1.0 + sc_ref[0]) + sh_ref[0]
    hb = h.astype(BF16)
    nz, nq, ng = splits
    cw = 512
    for c0 in range(0, nz + nq + ng, cw):
        acc = jnp.dot(hb, w_ref[:, c0:c0 + cw], preferred_element_type=F32)
        if c0 < nz:
            z_ref[:, c0:c0 + cw] = acc.astype(BF16)
        elif c0 < nz + nq:
            qkvg_ref[:, c0 - nz:c0 - nz + cw] = acc.astype(BF16)
        else:
            gates_ref[:, c0 - nz - nq:c0 - nz - nq + cw] = jax.nn.sigmoid(acc).astype(BF16)


def _in_projection(x, norm_g, sc, sh, w_in_bf, cond_of_tile):
    t, d = x.shape
    splits = (3 * D_HYENA, 4 * D_RET, 2 * d)
    tm = TOKEN_TILE
    row = lambda i: (i, 0)
    cond = lambda i: (cond_of_tile(i), 0, 0)
    return pl.pallas_call(
        functools.partial(_inproj_kernel, splits=splits),
        grid=(t // tm,),
        in_specs=[pl.BlockSpec((tm, d), row),
                  pl.BlockSpec((1, d), lambda i: (0, 0)),
                  pl.BlockSpec((1, 1, d), cond),
                  pl.BlockSpec((1, 1, d), cond),
                  pl.BlockSpec(w_in_bf.shape, lambda i: (0, 0))],
        out_specs=[pl.BlockSpec((tm, splits[0]), row),
                   pl.BlockSpec((tm, splits[1]), row),
                   pl.BlockSpec((tm, splits[2]), row)],
        out_shape=[jax.ShapeDtypeStruct((t, splits[0]), BF16),
                   jax.ShapeDtypeStruct((t, splits[1]), BF16),
                   jax.ShapeDtypeStruct((t, splits[2]), BF16)],
        compiler_params=_cparams(("parallel",)),
        name="in_projection",
    )(x, norm_g.reshape(1, d), sc, sh, w_in_bf)


def _filter_features(seq_len):
    t = np.linspace(0.0, 1.0, seq_len, dtype=np.float32)[:, None]
    bands = (FILTER_EMB - 1) // 2
    w = (np.float32(2.0 * math.pi) * np.arange(seq_len, dtype=np.float32)) / np.float32(seq_len)
    f = np.linspace(1e-4, bands - 1, bands, dtype=np.float32)
    ang = (w[:, None] * f[None, :]).astype(np.float64)
    z = np.concatenate([t, np.cos(ang), -np.sin(ang)], axis=-1).astype(np.float32)
    return np.pad(z, ((0, 0), (0, 128 - FILTER_EMB)))


def _filter_kernel(z_ref, w1_ref, b1_ref, fr_ref, w2_ref, b2_ref, w3_ref, dec_ref, o_ref):
    z = z_ref[...]
    fr = fr_ref[...]
    dot = functools.partial(jnp.dot, preferred_element_type=F32, precision=HIGHEST)
    h = jnp.sin(fr * (dot(z, w1_ref[...]) + b1_ref[...]))
    h = jnp.sin(fr * (dot(h, w2_ref[...]) + b2_ref[...]))
    h = dot(h, w3_ref[...]) * jnp.exp(-z[:, 0:1] * jnp.abs(dec_ref[...]))
    rows = h.shape[0]
    grow = pl.program_id(0) * rows + lax.broadcasted_iota(jnp.int32, (rows, 1), 0)
    o_ref[0] = h[:, :D_HYENA].astype(o_ref.dtype)
    o_ref[1] = jnp.zeros((rows, D_HYENA), o_ref.dtype)
    o_ref[2] = jnp.where(grow == 0, 0.0, h[:, D_HYENA:]).astype(o_ref.dtype)
    o_ref[3] = jnp.zeros((rows, D_HYENA), o_ref.dtype)


def _hyena_filters(seq_len, p, out_dtype):
    z = jnp.asarray(_filter_features(seq_len))
    w1 = jnp.pad(p["hy_f_w1"], ((0, 128 - FILTER_EMB), (0, 0)))
    rows = min(seq_len, 512)
    full = lambda a: pl.BlockSpec(a.shape, lambda i: (0,) * a.ndim)
    ops = [w1, p["hy_f_b1"].reshape(1, -1), p["hy_f_freq"].reshape(1, -1), p["hy_f_w2"],
           p["hy_f_b2"].reshape(1, -1), p["hy_f_w3"], p["hy_decay"].reshape(1, -1)]
    return pl.pallas_call(
        _filter_kernel,
        grid=(seq_len // rows,),
        in_specs=[pl.BlockSpec((rows, 128), lambda i: (i, 0))] + [full(a) for a in ops],
        out_specs=pl.BlockSpec((4, rows, D_HYENA), lambda i: (0, i, 0)),
        out_shape=jax.ShapeDtypeStruct((4, seq_len, D_HYENA), out_dtype),
        compiler_params=_cparams(("arbitrary",)),
        name="hyena_filter",
    )(z, *ops)


def _shortconv_kernel(v_ref, x1_ref, x2_ref, wv_ref, w1_ref, w2_ref, bv_ref, b1_ref, b2_ref, u_ref, x2c_ref):
    seq_len = v_ref.shape[1]
    row = lax.broadcasted_iota(jnp.int32, (seq_len, 1), 0)

    def conv(x_ref, w_ref, b_ref):
        x = x_ref[0].astype(F32)
        prev = jnp.where(row == 0, 0.0, pltpu.roll(x, 1, axis=0))
        nxt = jnp.where(row == seq_len - 1, 0.0, pltpu.roll(x, seq_len - 1, axis=0))
        return b_ref[...] + prev * w_ref[0:1] + x * w_ref[1:2] + nxt * w_ref[2:3]

    u_ref[0] = (conv(v_ref, wv_ref, bv_ref) * conv(x1_ref, w1_ref, b1_ref)).astype(u_ref.dtype)
    x2c_ref[0] = conv(x2_ref, w2_ref, b2_ref).astype(x2c_ref.dtype)


def _short_conv(z, conv_w, conv_b):
    b, seq_len, _ = z.shape
    cb = 128
    nb = D_HYENA // cb
    zspec = lambda off: pl.BlockSpec((1, seq_len, cb), lambda i, j: (i, 0, off + j))
    wspec = lambda off: pl.BlockSpec((3, cb), lambda i, j: (0, off + j))
    bspec = lambda off: pl.BlockSpec((1, cb), lambda i, j: (0, off + j))
    ospec = pl.BlockSpec((1, seq_len, cb), lambda i, j: (i, 0, j))
    cbias = conv_b.reshape(1, -1)
    return pl.pallas_call(
        _shortconv_kernel,
        grid=(b, nb),
        in_specs=[zspec(0), zspec(nb), zspec(2 * nb), wspec(0), wspec(nb), wspec(2 * nb),
                  bspec(0), bspec(nb), bspec(2 * nb)],
        out_specs=[ospec, ospec],
        out_shape=[jax.ShapeDtypeStruct((b, seq_len, D_HYENA), BF16)] * 2,
        compiler_params=_cparams(("parallel", "parallel")),
        name="hyena_short_conv",
    )(z, z, z, conv_w, conv_w, conv_w, cbias, cbias, cbias)


def _stacked_dft(n_out, n_in, modulus, scale=1.0):
    k = lax.broadcasted_iota(jnp.int32, (n_out, n_in), 0)
    n = lax.broadcasted_iota(jnp.int32, (n_out, n_in), 1)
    th = ((k * n) % modulus).astype(F32) * (2.0 * math.pi / modulus)
    c, s = jnp.cos(th), jnp.sin(th)
    fwd = jnp.block([[c, s], [-s, c]])
    inv = jnp.block([[c.T, -s.T], [s.T, c.T]]) * scale
    return fwd, inv


def _second_level_tables(n_total, n1, n2):
    shape = (n1, n2, n2)
    k1 = lax.broadcasted_iota(jnp.int32, shape, 0)
    k2 = lax.broadcasted_iota(jnp.int32, shape, 1)
    m = lax.broadcasted_iota(jnp.int32, shape, 2)
    th = ((m * (k1 + n1 * k2)) % n_total).astype(F32) * (2.0 * math.pi / n_total)
    c, s = jnp.cos(th), jnp.sin(th)
    g = jnp.concatenate([jnp.concatenate([c, s], axis=2), jnp.concatenate([-s, c], axis=2)], axis=1)
    return g, jnp.swapaxes(g, 1, 2)


def _fft1_core(u_sc, f_ref, a_ref, t_sc):
    n2 = FFT_MINOR
    _, seq_len, cb = u_sc.shape
    h1 = seq_len // n2

    def fwd(gi, carry):
        m0 = gi * FFT_GROUP
        cols = []
        for d in range(FFT_GROUP):
            xr = u_sc[0, pl.ds(m0 + d, h1, stride=n2), :]
            xi = u_sc[1, pl.ds(m0 + d, h1, stride=n2), :]
            cols.append(jnp.concatenate([xr, xi], axis=0))
        x = jnp.concatenate(cols, axis=1).astype(BF16)
        res = jnp.dot(f_ref[...], x, preferred_element_type=F32)
        for d in range(FFT_GROUP):
            t_sc[m0 + d] = res[:, d * cb:(d + 1) * cb]
        return carry

    lax.fori_loop(0, n2 // FFT_GROUP, fwd, 0)

    def transpose(k, carry):
        a_ref[0, k] = t_sc[:, k, :].astype(a_ref.dtype)
        return carry

    lax.fori_loop(0, t_sc.shape[1], transpose, 0, unroll=8)


def _fft1_kernel(h_ref, f_ref, a_ref, u_sc, t_sc):
    u_sc[...] = h_ref[...].astype(F32)
    _fft1_core(u_sc, f_ref, a_ref, t_sc)


def _conv_fft1_kernel(v_ref, x1_ref, x2_ref, wv_ref, w1_ref, w2_ref, bv_ref, b1_ref, b2_ref, f_ref,
                      u_ref, x2c_ref, a_ref, u_sc, t_sc):
    seq_len = v_ref.shape[1]
    row = lax.broadcasted_iota(jnp.int32, (seq_len, 1), 0)

    def conv(x_ref, b, w_ref, b_ref):
        x = x_ref[b].astype(F32)
        prev = jnp.where(row == 0, 0.0, pltpu.roll(x, 1, axis=0))
        nxt = jnp.where(row == seq_len - 1, 0.0, pltpu.roll(x, seq_len - 1, axis=0))
        return b_ref[...] + prev * w_ref[0:1] + x * w_ref[1:2] + nxt * w_ref[2:3]

    for b in range(2):
        u = conv(v_ref, b, wv_ref, bv_ref) * conv(x1_ref, b, w1_ref, b1_ref)
        u_sc[b] = u
        u_ref[b] = u.astype(u_ref.dtype)
        x2c_ref[b] = conv(x2_ref, b, w2_ref, b2_ref).astype(x2c_ref.dtype)
    _fft1_core(u_sc, f_ref, a_ref, t_sc)


def _fft1_scratch(seq_len, cb, n_rows):
    return [pltpu.VMEM((2, seq_len, cb), F32), pltpu.VMEM((FFT_MINOR, n_rows, cb), F32)]


def _fft_first_level(h, f1):
    b, seq_len, c = h.shape
    cb = 128
    return pl.pallas_call(
        _fft1_kernel,
        grid=(b // 2, c // cb),
        in_specs=[pl.BlockSpec((2, seq_len, cb), lambda i, j: (i, 0, j)),
                  pl.BlockSpec(f1.shape, lambda i, j: (0, 0))],
        out_specs=pl.BlockSpec((1, f1.shape[0], FFT_MINOR, cb), lambda i, j: (i, 0, 0, j)),
        out_shape=jax.ShapeDtypeStruct((b // 2, f1.shape[0], FFT_MINOR, c), BF16),
        scratch_shapes=_fft1_scratch(seq_len, cb, f1.shape[0]),
        compiler_params=_cparams(("parallel", "parallel")),
        name="hyena_dft_level1",
    )(h, f1)


def _conv_fft_first_level(z, conv_w, conv_b, f1):
    b, seq_len, _ = z.shape
    cb = 128
    nb = D_HYENA // cb
    zspec = lambda off: pl.BlockSpec((2, seq_len, cb), lambda i, j: (i, 0, off + j))
    wspec = lambda off: pl.BlockSpec((3, cb), lambda i, j: (0, off + j))
    bspec = lambda off: pl.BlockSpec((1, cb), lambda i, j: (0, off + j))
    ospec = pl.BlockSpec((2, seq_len, cb), lambda i, j: (i, 0, j))
    cbias = conv_b.reshape(1, -1)
    return pl.pallas_call(
        _conv_fft1_kernel,
        grid=(b // 2, nb),
        in_specs=[zspec(0), zspec(nb), zspec(2 * nb), wspec(0), wspec(nb), wspec(2 * nb),
                  bspec(0), bspec(nb), bspec(2 * nb), pl.BlockSpec(f1.shape, lambda i, j: (0, 0))],
        out_specs=[ospec, ospec,
                   pl.BlockSpec((1, f1.shape[0], FFT_MINOR, cb), lambda i, j: (i, 0, 0, j))],
        out_shape=[jax.ShapeDtypeStruct((b, seq_len, D_HYENA), BF16)] * 2
        + [jax.ShapeDtypeStruct((b // 2, f1.shape[0], FFT_MINOR, D_HYENA), BF16)],
        scratch_shapes=_fft1_scratch(seq_len, cb, f1.shape[0]),
        compiler_params=_cparams(("parallel", "parallel")),
        name="hyena_conv_dft_level1",
    )(z, z, z, conv_w, conv_w, conv_w, cbias, cbias, cbias, f1)


def _fft_s2f_kernel(a_ref, g_ref, kr_ref, ki_ref):
    _, _, kb, n2, c = a_ref.shape
    for kk in range(kb):
        g = g_ref[kk]
        hf = jnp.dot(g, a_ref[0, :, kk].reshape(2 * n2, c), preferred_element_type=F32)
        hb = jnp.dot(g, a_ref[1, :, kk].reshape(2 * n2, c), preferred_element_type=F32)
        kr_ref[kk] = hf[:n2] + hb[:n2]
        ki_ref[kk] = hf[n2:] - hb[n2:]


def _fft_filter_second_level(a, g, kb=8):
    _, _, n1, n2, c = a.shape
    spec = pl.BlockSpec((kb, n2, c), lambda i: (i, 0, 0))
    return pl.pallas_call(
        _fft_s2f_kernel,
        grid=(n1 // kb,),
        in_specs=[pl.BlockSpec((2, 2, kb, n2, c), lambda i: (0, 0, i, 0, 0)),
                  pl.BlockSpec((kb, 2 * n2, 2 * n2), lambda i: (i, 0, 0))],
        out_specs=[spec, spec],
        out_shape=[jax.ShapeDtypeStruct((n1, n2, c), F32)] * 2,
        compiler_params=_cparams(("parallel",)),
        name="hyena_filter_spectrum",
    )(a, g)


def _fft_s2_kernel(a_ref, g_ref, gt_ref, kr_ref, ki_ref, b_ref, r_sc):
    npairs, _, kb, n2, cb = a_ref.shape
    for p in range(npairs):
        for kk in range(kb):
            x = jnp.dot(g_ref[kk], a_ref[p, :, kk].reshape(2 * n2, cb), preferred_element_type=F32)
            xr, xi = x[:n2], x[n2:]
            kr, ki = kr_ref[kk], ki_ref[kk]
            y = jnp.concatenate([xr * kr - xi * ki, xr * ki + xi * kr], axis=0).astype(BF16)
            r_sc[kk] = jnp.dot(gt_ref[kk], y, preferred_element_type=F32)

        def transpose(m, carry):
            b_ref[p, 0, m] = r_sc[:, m, :].astype(b_ref.dtype)
            b_ref[p, 1, m] = r_sc[:, n2 + m, :].astype(b_ref.dtype)
            return carry

        lax.fori_loop(0, n2, transpose, 0, unroll=8)


def _fft_second_level(a, g, gt, kr, ki):
    p, _, n1, n2, c = a.shape
    kb, cb = 16, 256
    gspec = pl.BlockSpec((kb, 2 * n2, 2 * n2), lambda i, j: (i, 0, 0))
    kspec = pl.BlockSpec((kb, n2, cb), lambda i, j: (i, 0, j))
    return pl.pallas_call(
        _fft_s2_kernel,
        grid=(n1 // kb, c // cb),
        in_specs=[pl.BlockSpec((p, 2, kb, n2, cb), lambda i, j: (0, 0, i, 0, j)), gspec, gspec, kspec, kspec],
        out_specs=pl.BlockSpec((p, 2, n2, kb, cb), lambda i, j: (0, 0, 0, i, j)),
        out_shape=jax.ShapeDtypeStruct((p, 2, n2, n1, c), BF16),
        scratch_shapes=[pltpu.VMEM((kb, 2 * n2, cb), F32)],
        compiler_params=_cparams(("parallel", "parallel")),
        name="hyena_dft_level2",
    )(a, g, gt, kr, ki)


def _fft_s3_kernel(b_ref, f_ref, u_ref, x2_ref, bias_ref, o_ref, t_sc):
    _, _, n2, n1, cb = b_ref.shape
    h1 = t_sc.shape[1] // 2

    def inv(gi, carry):
        m0 = gi * FFT_GROUP
        x = jnp.concatenate([b_ref[0, :, m0 + d].reshape(2 * n1, cb) for d in range(FFT_GROUP)], axis=1)
        res = jnp.dot(f_ref[...], x, preferred_element_type=F32)
        for d in range(FFT_GROUP):
            t_sc[m0 + d] = res[:, d * cb:(d + 1) * cb]
        return carry

    lax.fori_loop(0, n2 // FFT_GROUP, inv, 0)

    bias = bias_ref[...]
    for b in range(2):
        def finish(q, carry):
            rows = pl.ds(pl.multiple_of(q * n2, n2), n2)
            conv = t_sc[:, b * h1 + q, :]
            u = u_ref[b, rows, :].astype(F32)
            o_ref[b, rows, :] = ((conv + u * bias) * x2_ref[b, rows, :].astype(F32)).astype(o_ref.dtype)
            return carry

        lax.fori_loop(0, h1, finish, 0, unroll=4)


def _fft_last_level(bt, f1inv, u, x2c, bias):
    p, _, n2, n1, c = bt.shape
    b, seq_len, _ = u.shape
    cb = 128
    uspec = pl.BlockSpec((2, seq_len, cb), lambda i, j: (i, 0, j))
    return pl.pallas_call(
        _fft_s3_kernel,
        grid=(p, c // cb),
        in_specs=[pl.BlockSpec((1, 2, n2, n1, cb), lambda i, j: (i, 0, 0, 0, j)),
                  pl.BlockSpec(f1inv.shape, lambda i, j: (0, 0)),
                  uspec, uspec,
                  pl.BlockSpec((1, cb), lambda i, j: (0, j))],
        out_specs=uspec,
        out_shape=jax.ShapeDtypeStruct(u.shape, BF16),
        scratch_shapes=[pltpu.VMEM((n2, f1inv.shape[0], cb), F32)],
        compiler_params=_cparams(("parallel", "parallel")),
        name="hyena_dft_inverse",
    )(bt, f1inv, u, x2c, bias)


def _kf_direct_kernel(h_ref, f_ref, kr_ref, ki_ref):
    n = kr_ref.shape[0]
    dot = functools.partial(jnp.dot, preferred_element_type=F32, precision=HIGHEST)
    hf = dot(f_ref[...], h_ref[0])
    hb = dot(f_ref[...], h_ref[2])
    kr_ref[...] = hf[:n] + hb[:n]
    ki_ref[...] = hf[n:] - hb[n:]


def _hyena_direct_kernel(u_ref, x2_ref, f_ref, fi_ref, kr_ref, ki_ref, bias_ref, o_ref):
    _, two, seq_len, cb = u_ref.shape
    n = kr_ref.shape[0]
    ub = u_ref[0].reshape(two * seq_len, cb)
    x = jnp.dot(f_ref[...], ub, preferred_element_type=F32)
    xr, xi = x[:n], x[n:]
    kr, ki = kr_ref[...], ki_ref[...]
    y = jnp.concatenate([xr * kr - xi * ki, xr * ki + xi * kr], axis=0).astype(BF16)
    conv = jnp.dot(fi_ref[...], y, preferred_element_type=F32)
    x2 = x2_ref[0].reshape(two * seq_len, cb).astype(F32)
    o = (conv + ub.astype(F32) * bias_ref[...]) * x2
    o_ref[0] = o.reshape(two, seq_len, cb).astype(o_ref.dtype)


def _hyena_branch(z, p):
    b, seq_len, _ = z.shape
    c = D_HYENA
    n = 2 * seq_len
    bias = p["hy_bias"].reshape(1, c)
    hh = _hyena_filters(seq_len, p, F32)
    if seq_len <= 512:
        u, x2c = _short_conv(z, p["hy_conv_w"], p["hy_conv_b"])
        fwd, inv = _stacked_dft(n, seq_len, n, scale=1.0 / n)
        kr, ki = pl.pallas_call(
            _kf_direct_kernel,
            out_shape=[jax.ShapeDtypeStruct((n, c), F32)] * 2,
            compiler_params=_cparams(None),
            name="hyena_filter_spectrum_direct",
        )(hh, fwd[:, :seq_len])
        cb = 256
        pair = lambda a: a.reshape(b // 2, 2, seq_len, c)
        uspec = pl.BlockSpec((1, 2, seq_len, cb), lambda i, j: (i, 0, 0, j))
        kspec = pl.BlockSpec((n, cb), lambda i, j: (0, j))
        out = pl.pallas_call(
            _hyena_direct_kernel,
            grid=(b // 2, c // cb),
            in_specs=[uspec, uspec,
                      pl.BlockSpec(fwd.shape, lambda i, j: (0, 0)),
                      pl.BlockSpec(inv.shape, lambda i, j: (0, 0)),
                      kspec, kspec,
                      pl.BlockSpec((1, cb), lambda i, j: (0, j))],
            out_specs=uspec,
            out_shape=jax.ShapeDtypeStruct((b // 2, 2, seq_len, c), BF16),
            compiler_params=_cparams(("parallel", "parallel")),
            name="hyena_dft_direct",
        )(pair(u), pair(x2c), fwd.astype(BF16), inv.astype(BF16), kr, ki, bias)
        return out.reshape(b, seq_len, c)

    n2 = FFT_MINOR
    n1 = n // n2
    h1 = seq_len // n2
    f1, f1inv = _stacked_dft(n1, h1, n1, scale=1.0 / n)
    f1, f1inv = f1.astype(BF16), f1inv.astype(BF16)
    g, gt = _second_level_tables(n, n1, n2)
    g, gt = g.astype(BF16), gt.astype(BF16)
    ha = _fft_first_level(hh, f1)
    kr, ki = _fft_filter_second_level(ha.reshape(2, 2, n1, n2, c), g)
    u, x2c, a = _conv_fft_first_level(z, p["hy_conv_w"], p["hy_conv_b"], f1)
    bt = _fft_second_level(a.reshape(b // 2, 2, n1, n2, c), g, gt, kr, ki)
    return _fft_last_level(bt, f1inv, u, x2c, bias)


def _rope_tables(seq_len):
    half = RET_HEAD_DIM // 2
    nf = half // 2
    t = jnp.arange(seq_len)
    row = (t // GRID_W).astype(F32)
    col = (t % GRID_W).astype(F32)
    inv = ROPE_BASE ** (-jnp.arange(nf, dtype=F32) / nf)
    ar = row[:, None] * inv[None, :]
    ac = col[:, None] * inv[None, :]
    cos = jnp.concatenate([jnp.cos(ar), jnp.cos(ar), jnp.cos(ac), jnp.cos(ac)], axis=-1)
    sin = jnp.concatenate([-jnp.sin(ar), jnp.sin(ar), -jnp.sin(ac), jnp.sin(ac)], axis=-1)
    return cos, sin


def _log_sigmoid(x):
    return jnp.minimum(x, 0.0) - jnp.log1p(jnp.exp(-jnp.abs(x)))


def _retention_kernel(*refs, rope, has_init, emit_state, cpb):
    refs = list(refs)
    q_ref, k_ref, v_ref, g_ref, dec_ref, gn_ref = refs[:6]
    del refs[:6]
    if rope:
        cos_ref, sin_ref = refs[:2]
        del refs[:2]
    if has_init:
        s0f_ref, s0b_ref = refs[:2]
        del refs[:2]
    o_ref = refs.pop(0)
    if emit_state:
        sf_out, sb_out = refs[:2]
        del refs[:2]
    sf_ref, sb_ref, sball_ref = refs

    c = RET_CHUNK
    dh = RET_HEAD_DIM
    nh = N_RET_HEADS
    phase = pl.program_id(1)
    j = pl.program_id(2)
    nb = pl.num_programs(2)

    ri = lax.broadcasted_iota(jnp.int32, (c, c), 0).astype(F32)
    ci = lax.broadcasted_iota(jnp.int32, (c, c), 1).astype(F32)
    diff = ri - ci
    lane = lax.broadcasted_iota(jnp.int32, (c, dh), 1)
    swap_hi = (lane % (dh // 2)) < (dh // 4)

    def head_consts(h):
        lgf = _log_sigmoid(dec_ref[0, h])[0:1, :]
        lgb = _log_sigmoid(dec_ref[1, h])[0:1, :]
        return lgf, lgb

    def load_rot(ref, r0, h, scale):
        x = ref[0, r0:r0 + c, h * dh:(h + 1) * dh].astype(F32)
        if scale != 1.0:
            x = x * scale
        if rope:
            cs = cos_ref[r0:r0 + c, :]
            sn = sin_ref[r0:r0 + c, :]
            rot = jnp.where(swap_hi, pltpu.roll(x, dh - dh // 4, axis=1), pltpu.roll(x, dh // 4, axis=1))
            x = x * cs + rot * sn
        return x

    kscale = dh ** -0.5
    dn_t = (((0,), (0,)), ((), ()))
    dn_nt = (((1,), (1,)), ((), ()))

    @pl.when(phase == 0)
    def _backward_sweep():
        @pl.when(j == 0)
        def _():
            for h in range(nh):
                sb_ref[h] = s0b_ref[0, h] if has_init else jnp.zeros((dh, dh), F32)

        blk = nb - 1 - j
        for h in range(nh):
            _, lgb = head_consts(h)
            zeta_b = jnp.exp(lgb * ri)
            cdec_b = jnp.exp(lgb * float(c))
            for cc in reversed(range(cpb)):
                r0 = cc * c
                n = blk * cpb + cc
                s = sb_ref[h]
                sball_ref[n, h] = s.astype(BF16)
                kz = (load_rot(k_ref, r0, h, kscale) * zeta_b).astype(BF16)
                vv = v_ref[0, r0:r0 + c, h * dh:(h + 1) * dh]
                sb_ref[h] = cdec_b * s + lax.dot_general(kz, vv, dn_t, preferred_element_type=F32)

        if emit_state:
            @pl.when(j == nb - 1)
            def _():
                for h in range(nh):
                    sb_out[0, h] = sb_ref[h]

    @pl.when(phase == 1)
    def _forward_sweep():
        @pl.when(j == 0)
        def _():
            for h in range(nh):
                sf_ref[h] = s0f_ref[0, h] if has_init else jnp.zeros((dh, dh), F32)

        for h in range(nh):
            lgf, lgb = head_consts(h)
            mask = (jnp.where(diff >= 0, jnp.exp(lgf * jnp.maximum(diff, 0.0)), 0.0)
                    + jnp.where(diff <= 0, jnp.exp(lgb * jnp.maximum(-diff, 0.0)), 0.0))
            xi_f = jnp.exp(lgf * (ri + 1.0))
            xi_b = jnp.exp(lgb * (float(c) - ri))
            zeta_f = jnp.exp(lgf * (float(c - 1) - ri))
            cdec_f = jnp.exp(lgf * float(c))
            gn = gn_ref[:, h * dh:(h + 1) * dh]
            for cc in range(cpb):
                r0 = cc * c
                n = j * cpb + cc
                q = load_rot(q_ref, r0, h, 1.0)
                k = load_rot(k_ref, r0, h, kscale)
                vv = v_ref[0, r0:r0 + c, h * dh:(h + 1) * dh]
                gate = g_ref[0, r0:r0 + c, h * dh:(h + 1) * dh].astype(F32)
                kb16 = k.astype(BF16)
                sc = lax.dot_general(q.astype(BF16), kb16, dn_nt, preferred_element_type=F32)
                inner = jnp.dot((sc * mask).astype(BF16), vv, preferred_element_type=F32)
                s = sf_ref[h]
                lhs = jnp.concatenate([q * xi_f, q * xi_b], axis=1).astype(BF16)
                rhs = jnp.concatenate([s.astype(BF16), sball_ref[n, h]], axis=0)
                o = inner + jnp.dot(lhs, rhs, preferred_element_type=F32)
                mu = jnp.mean(o, axis=-1, keepdims=True)
                d = o - mu
                var = jnp.mean(d * d, axis=-1, keepdims=True)
                y = d * lax.rsqrt(var + EPS) * gn * _silu(gate)
                o_ref[0, r0:r0 + c, h * dh:(h + 1) * dh] = y.astype(o_ref.dtype)
                kz = (k * zeta_f).astype(BF16)
                sf_ref[h] = cdec_f * s + lax.dot_general(kz, vv, dn_t, preferred_element_type=F32)

        if emit_state:
            @pl.when(j == nb - 1)
            def _():
                for h in range(nh):
                    sf_out[0, h] = sf_ref[h]


def _retention(qkvg, dec_f, dec_b, gn_g, s0_f, s0_b, rope, emit_state):
    b, seq_len, _ = qkvg.shape
    nh, dh, c = N_RET_HEADS, RET_HEAD_DIM, RET_CHUNK
    rb = min(seq_len, 512)
    nb = seq_len // rb
    cpb = rb // c
    has_init = s0_f is not None
    dec = jnp.broadcast_to(jnp.stack([dec_f, dec_b])[:, :, None, None], (2, nh, 8, 128)).astype(F32)
    kv_blk = lambda i, p, j: jnp.where(p == 0, nb - 1 - j, j)
    q_blk = lambda i, p, j: jnp.where(p == 0, 0, j)
    in_specs = [pl.BlockSpec((1, rb, D_RET), lambda i, p, j: (i, q_blk(i, p, j), 0)),
                pl.BlockSpec((1, rb, D_RET), lambda i, p, j: (i, kv_blk(i, p, j), 1)),
                pl.BlockSpec((1, rb, D_RET), lambda i, p, j: (i, kv_blk(i, p, j), 2)),
                pl.BlockSpec((1, rb, D_RET), lambda i, p, j: (i, q_blk(i, p, j), 3)),
                pl.BlockSpec((2, nh, 8, 128), lambda i, p, j: (0, 0, 0, 0)),
                pl.BlockSpec((1, D_RET), lambda i, p, j: (0, 0))]
    args = [qkvg, qkvg, qkvg, qkvg, dec, gn_g.reshape(1, -1)]
    if rope:
        cos, sin = _rope_tables(seq_len)
        tspec = pl.BlockSpec((rb, dh), lambda i, p, j: (kv_blk(i, p, j), 0))
        in_specs += [tspec, tspec]
        args += [cos, sin]
    sspec = pl.BlockSpec((1, nh, dh, dh), lambda i, p, j: (i, 0, 0, 0))
    if has_init:
        in_specs += [sspec, sspec]
        args += [s0_f, s0_b]
    out_specs = [pl.BlockSpec((1, rb, D_RET), lambda i, p, j: (i, q_blk(i, p, j), 0))]
    out_shape = [jax.ShapeDtypeStruct((b, seq_len, D_RET), BF16)]
    if emit_state:
        out_specs += [sspec, sspec]
        out_shape += [jax.ShapeDtypeStruct((b, nh, dh, dh), F32)] * 2
    return pl.pallas_call(
        functools.partial(_retention_kernel, rope=rope, has_init=has_init, emit_state=emit_state, cpb=cpb),
        grid=(b, 2, nb),
        in_specs=in_specs,
        out_specs=out_specs,
        out_shape=out_shape,
        scratch_shapes=[pltpu.VMEM((nh, dh, dh), F32), pltpu.VMEM((nh, dh, dh), F32),
                        pltpu.VMEM((nb * cpb, nh, dh, dh), BF16)],
        compiler_params=_cparams(("parallel", "arbitrary", "arbitrary")),
        name="retention",
    )(*args)


def _outproj_kernel(yhy_ref, yret_ref, gates_ref, x_ref, g1_ref, sc_ref, sh_ref, ng_ref,
                    why_ref, wret_ref, wout_ref, wr_ref, x1_ref, h2_ref, aff_ref):
    d = x_ref.shape[1]
    a = jnp.dot(yhy_ref[...], why_ref[...], preferred_element_type=F32)
    b = jnp.dot(yret_ref[...], wret_ref[...], preferred_element_type=F32)
    merged = gates_ref[:, :d].astype(F32) * a + gates_ref[:, d:].astype(F32) * b
    out = jnp.dot(merged.astype(BF16), wout_ref[...], preferred_element_type=F32)
    x1 = x_ref[...] + g1_ref[0] * out
    x1_ref[...] = x1
    ms = jnp.mean(x1 * x1, axis=-1, keepdims=True)
    h = x1 * lax.rsqrt(ms + EPS) * ng_ref[...]
    h = h * (1.0 + sc_ref[0]) + sh_ref[0]
    h_hi = h.astype(BF16)
    h2_ref[...] = h_hi
    h_lo = (h - h_hi.astype(F32)).astype(BF16)
    t = jnp.dot(h_hi, wr_ref[...], preferred_element_type=F32)
    logits = t[:, :128] + t[:, 128:] + jnp.dot(h_lo, wr_ref[:, :128], preferred_element_type=F32)
    lane = lax.broadcasted_iota(jnp.int32, logits.shape, 1)
    logits = jnp.where(lane < N_EXPERTS, logits, -jnp.inf)
    e = jnp.exp(logits - jnp.max(logits, axis=-1, keepdims=True))
    aff = e / jnp.sum(e, axis=-1, keepdims=True)
    aff_ref[0] = aff.T[:N_EXPERTS, :]


def _out_projection(y_hy, y_ret, gates, x, g1, sc2, sh2, norm2_g, w_hy_o, w_ret_o, w_out, w_router, cond_of_tile):
    t, d = x.shape
    tm = TOKEN_TILE
    row = lambda i: (i, 0)
    cond = lambda i: (cond_of_tile(i), 0, 0)
    full = lambda a: pl.BlockSpec(a.shape, lambda i: (0, 0))
    wr = jnp.pad(w_router, ((0, 0), (0, 128 - N_EXPERTS)))
    wr_hi = wr.astype(BF16)
    wr = jnp.concatenate([wr_hi, (wr - wr_hi.astype(F32)).astype(BF16)], axis=1)
    return pl.pallas_call(
        _outproj_kernel,
        grid=(t // tm,),
        in_specs=[pl.BlockSpec((tm, D_HYENA), row), pl.BlockSpec((tm, D_RET), row),
                  pl.BlockSpec((tm, 2 * d), row), pl.BlockSpec((tm, d), row),
                  pl.BlockSpec((1, 1, d), cond), pl.BlockSpec((1, 1, d), cond), pl.BlockSpec((1, 1, d), cond),
                  pl.BlockSpec((1, d), lambda i: (0, 0)),
                  full(w_hy_o), full(w_ret_o), full(w_out), full(wr)],
        out_specs=[pl.BlockSpec((tm, d), row), pl.BlockSpec((tm, d), row),
                   pl.BlockSpec((1, N_EXPERTS, tm), lambda i: (i, 0, 0))],
        out_shape=[jax.ShapeDtypeStruct((t, d), F32), jax.ShapeDtypeStruct((t, d), BF16),
                   jax.ShapeDtypeStruct((t // tm, N_EXPERTS, tm), F32)],
        compiler_params=_cparams(("parallel",)),
        name="out_projection_router",
    )(y_hy, y_ret, gates, x, g1, sc2, sh2, norm2_g.reshape(1, d), w_hy_o, w_ret_o, w_out, wr)


def _select_kernel(aff_ref, tri_ref, rank_ref, cnt_ref, start_ref, tot_ref, *, cap, idx_bits):
    nt, ne, tm = aff_ref.shape
    a = aff_ref[...]

    def count(m):
        return jnp.sum(jnp.sum(m, axis=0, keepdims=True), axis=2, keepdims=True)

    def thr_step(s, thr):
        cand = thr | (1 << (30 - s))
        cnt = count(jnp.where(a >= pltpu.bitcast(cand, F32), 1.0, 0.0))
        return jnp.where(cnt >= float(cap), cand, thr)

    thr = pltpu.bitcast(lax.fori_loop(0, 31, thr_step, jnp.zeros((1, ne, 1), jnp.int32)), F32)
    gt = a > thr
    eq = a == thr
    need = float(cap) - count(jnp.where(gt, 1.0, 0.0))
    idx = (lax.broadcasted_iota(jnp.int32, (nt, 1, tm), 0) * tm
           + lax.broadcasted_iota(jnp.int32, (nt, 1, tm), 2))

    def idx_step(s, lim):
        cand = lim | (1 << (idx_bits - 1 - s))
        cnt = count(jnp.where(eq, jnp.where(idx < cand, 1.0, 0.0), 0.0))
        return jnp.where(cnt < need, cand, lim)

    lim = lax.fori_loop(0, idx_bits, idx_step, jnp.zeros((1, ne, 1), jnp.int32))
    sel = jnp.where(gt, 1.0, jnp.where(eq, jnp.where(idx <= lim, 1.0, 0.0), 0.0))
    sel2 = sel.reshape(nt * ne, tm).astype(BF16)
    prefix = jnp.dot(sel2, tri_ref[...], preferred_element_type=F32)
    rank = jnp.where(sel2 > 0, prefix, -1.0).astype(jnp.int32)
    rank_ref[...] = rank.reshape(nt, ne, tm)
    ones = jnp.ones((tm, 128), BF16)
    cnt = jnp.dot(sel2, ones, preferred_element_type=F32).astype(jnp.int32).reshape(nt, ne, 128)
    cnt_ref[...] = cnt
    acc = jnp.zeros((ne, 128), jnp.int32)
    for t in range(nt):
        start_ref[t] = acc
        acc = acc + ((cnt[t] + (ROW_ALIGN - 1)) & (-ROW_ALIGN))
    tot_ref[...] = acc


def _select(aff, cap):
    nt, ne, tm = aff.shape
    idx_bits = max(1, int(math.ceil(math.log2(nt * tm))))
    r = lax.broadcasted_iota(jnp.int32, (tm, tm), 0)
    c = lax.broadcasted_iota(jnp.int32, (tm, tm), 1)
    tri = (r < c).astype(BF16)
    rank, cnt, start, tot = pl.pallas_call(
        functools.partial(_select_kernel, cap=cap, idx_bits=idx_bits),
        out_shape=[jax.ShapeDtypeStruct((nt, ne, tm), jnp.int32),
                   jax.ShapeDtypeStruct((nt, ne, 128), jnp.int32),
                   jax.ShapeDtypeStruct((nt, ne, 128), jnp.int32),
                   jax.ShapeDtypeStruct((ne, 128), jnp.int32)],
        compiler_params=_cparams(None),
        name="expert_choice_select",
    )(aff, tri)
    return rank, cnt[:, :, 0], start[:, :, 0], tot[:, 0]


def _list_rows(cap, nt):
    need = cap + ROW_ALIGN * nt + EXPERT_BLOCK + SLOT_ROWS
    return -(-need // EXPERT_BLOCK) * EXPERT_BLOCK


def _num_rounds(cnt_sm, i):
    m = cnt_sm[i, 0]
    for e in range(1, N_EXPERTS):
        m = jnp.maximum(m, cnt_sm[i, e])
    return jnp.maximum((m + SLOT_ROWS - 1) // SLOT_ROWS, 1)


def _gather_kernel(start_sm, cnt_sm, tot_sm, h_ref, rank_ref, xs_hbm, stage, zbuf, sem):
    i = pl.program_id(0)
    ne = N_EXPERTS
    slot = i % 2
    rank = rank_ref[0]
    sub = lax.broadcasted_iota(jnp.int32, (SLOT_ROWS, rank.shape[1]), 0)

    def copy(s, e, off):
        return pltpu.make_async_copy(stage.at[s, pl.ds(e * SLOT_ROWS, SLOT_ROWS)],
                                     xs_hbm.at[e, pl.ds(off, SLOT_ROWS)], sem.at[e])

    def fill(r):
        h = h_ref[...]
        for e in range(ne):
            onehot = jnp.where(rank[e:e + 1, :] == sub + r * SLOT_ROWS, 1.0, 0.0).astype(BF16)
            stage[slot, e * SLOT_ROWS:(e + 1) * SLOT_ROWS, :] = jnp.dot(
                onehot, h, preferred_element_type=F32).astype(BF16)

    def start_all(r):
        for e in range(ne):
            copy(slot, e, pl.multiple_of(start_sm[i, e] + r * SLOT_ROWS, ROW_ALIGN)).start()

    def wait_all(s):
        for e in range(ne):
            copy(s, e, 0).wait()

    fill(0)

    @pl.when(i > 0)
    def _():
        wait_all(1 - slot)

    start_all(0)

    def extra_round(r, carry):
        wait_all(slot)
        fill(r)
        start_all(r)
        return carry

    lax.fori_loop(1, _num_rounds(cnt_sm, i), extra_round, 0)

    @pl.when(i == pl.num_programs(0) - 1)
    def _zero_tail():
        wait_all(slot)
        zbuf[...] = jnp.zeros(zbuf.shape, zbuf.dtype)

        def zcopy(e):
            off = pl.multiple_of(tot_sm[e], ROW_ALIGN)
            return pltpu.make_async_copy(zbuf, xs_hbm.at[e, pl.ds(off, zbuf.shape[0])], sem.at[e])

        for e in range(ne):
            zcopy(e).start()
        for e in range(ne):
            zcopy(e).wait()


def _gather(h2, rank, start, cnt, tot, cap):
    t, d = h2.shape
    nt, ne, tm = rank.shape
    rl = _list_rows(cap, nt)
    return pl.pallas_call(
        _gather_kernel,
        grid_spec=pltpu.PrefetchScalarGridSpec(
            num_scalar_prefetch=3,
            grid=(nt,),
            in_specs=[pl.BlockSpec((tm, d), lambda i, *_: (i, 0)),
                      pl.BlockSpec((1, ne, tm), lambda i, *_: (i, 0, 0))],
            out_specs=pl.BlockSpec(memory_space=pl.ANY),
            scratch_shapes=[pltpu.VMEM((2, ne * SLOT_ROWS, d), BF16),
                            pltpu.VMEM((EXPERT_BLOCK + SLOT_ROWS, d), BF16),
                            pltpu.SemaphoreType.DMA((ne,))]),
        out_shape=jax.ShapeDtypeStruct((ne, rl, d), BF16),
        compiler_params=_cparams(("arbitrary",)),
        name="expert_gather",
    )(start, cnt, tot, h2, rank)


def _expert_kernel(tot_sm, xs_ref, wg_ref, wu_ref, wd_ref, ye_ref, wg_bf, wu_bf, wd_bf):
    e = pl.program_id(0)
    j = pl.program_id(1)

    @pl.when(j == 0)
    def _():
        wg_bf[...] = wg_ref[0].astype(BF16)
        wu_bf[...] = wu_ref[0].astype(BF16)
        wd_bf[...] = wd_ref[0].astype(BF16)

    @pl.when(j * EXPERT_BLOCK < tot_sm[e] + SLOT_ROWS)
    def _():
        x = xs_ref[0]
        g = jnp.dot(x, wg_bf[...], preferred_element_type=F32)
        u = jnp.dot(x, wu_bf[...], preferred_element_type=F32)
        hid = (_silu(g) * u).astype(BF16)
        ye_ref[0] = jnp.dot(hid, wd_bf[...], preferred_element_type=F32).astype(ye_ref.dtype)


def _experts(xs, tot, w_gate, w_up, w_down):
    ne, rl, d = xs.shape
    f = w_gate.shape[2]
    last = lambda e, tot: (tot[e] + SLOT_ROWS - 1) // EXPERT_BLOCK
    blk = lambda e, j, tot: (e, jnp.minimum(j, last(e, tot)), 0)
    return pl.pallas_call(
        _expert_kernel,
        grid_spec=pltpu.PrefetchScalarGridSpec(
            num_scalar_prefetch=1,
            grid=(ne, rl // EXPERT_BLOCK),
            in_specs=[pl.BlockSpec((1, EXPERT_BLOCK, d), blk),
                      pl.BlockSpec((1, d, f), lambda e, j, tot: (e, 0, 0)),
                      pl.BlockSpec((1, d, f), lambda e, j, tot: (e, 0, 0)),
                      pl.BlockSpec((1, f, d), lambda e, j, tot: (e, 0, 0))],
            out_specs=pl.BlockSpec((1, EXPERT_BLOCK, d), blk),
            scratch_shapes=[pltpu.VMEM((d, f), BF16), pltpu.VMEM((d, f), BF16), pltpu.VMEM((f, d), BF16)]),
        out_shape=jax.ShapeDtypeStruct((ne, rl, d), BF16),
        compiler_params=_cparams(("arbitrary", "arbitrary")),
        name="expert_ffn",
    )(tot, xs, w_gate, w_up, w_down)


def _combine_kernel(start_sm, cnt_sm, x1_ref, rank_ref, aff_ref, g2_ref, nf_ref, ye_hbm, o_ref, buf, sem, *,
                    final_norm):
    i = pl.program_id(0)
    nt = pl.num_programs(0)
    ne = N_EXPERTS
    slot = i % 2
    rank = rank_ref[0]
    aff = aff_ref[0]
    sub = lax.broadcasted_iota(jnp.int32, (SLOT_ROWS, rank.shape[1]), 0)
    dn_t = (((0,), (0,)), ((), ()))

    def copy(t, s, e, r):
        off = pl.multiple_of(start_sm[t, e] + r * SLOT_ROWS, ROW_ALIGN)
        return pltpu.make_async_copy(ye_hbm.at[e, pl.ds(off, SLOT_ROWS)],
                                     buf.at[s, pl.ds(e * SLOT_ROWS, SLOT_ROWS)], sem.at[s, e])

    def weighted(r):
        w = jnp.concatenate(
            [jnp.where(rank[e:e + 1, :] == sub + r * SLOT_ROWS, aff[e:e + 1, :], 0.0).astype(BF16)
             for e in range(ne)], axis=0)
        return lax.dot_general(w, buf[slot], dn_t, preferred_element_type=F32)

    @pl.when(i == 0)
    def _():
        for e in range(ne):
            copy(0, 0, e, 0).start()

    @pl.when(i + 1 < nt)
    def _prefetch_next_tile():
        for e in range(ne):
            copy(i + 1, 1 - slot, e, 0).start()

    for e in range(ne):
        copy(i, slot, e, 0).wait()
    y0 = weighted(0)

    def extra_round(r, acc):
        for e in range(ne):
            @pl.when(cnt_sm[i, e] > r * SLOT_ROWS)
            def _():
                copy(i, slot, e, r).start()
        for e in range(ne):
            @pl.when(cnt_sm[i, e] > r * SLOT_ROWS)
            def _():
                copy(i, slot, e, r).wait()
        return acc + weighted(r)

    y = lax.fori_loop(1, _num_rounds(cnt_sm, i), extra_round, y0)
    x2 = x1_ref[...] + g2_ref[0] * y
    if final_norm:
        ms = jnp.mean(x2 * x2, axis=-1, keepdims=True)
        x2 = x2 * lax.rsqrt(ms + EPS) * nf_ref[...]
    o_ref[...] = x2


def _combine(x1, rank, aff, start, cnt, g2, norm_f_g, ye, cond_of_tile):
    t, d = x1.shape
    nt, ne, tm = rank.shape
    final_norm = norm_f_g is not None
    if not final_norm:
        norm_f_g = jnp.ones((d,), F32)
    return pl.pallas_call(
        functools.partial(_combine_kernel, final_norm=final_norm),
        grid_spec=pltpu.PrefetchScalarGridSpec(
            num_scalar_prefetch=2,
            grid=(nt,),
            in_specs=[pl.BlockSpec((tm, d), lambda i, *_: (i, 0)),
                      pl.BlockSpec((1, ne, tm), lambda i, *_: (i, 0, 0)),
                      pl.BlockSpec((1, ne, tm), lambda i, *_: (i, 0, 0)),
                      pl.BlockSpec((1, 1, d), lambda i, *_: (cond_of_tile(i), 0, 0)),
                      pl.BlockSpec((1, d), lambda i, *_: (0, 0)),
                      pl.BlockSpec(memory_space=pl.ANY)],
            out_specs=pl.BlockSpec((tm, d), lambda i, *_: (i, 0)),
            scratch_shapes=[pltpu.VMEM((2, ne * SLOT_ROWS, d), BF16),
                            pltpu.SemaphoreType.DMA((2, ne))]),
        out_shape=jax.ShapeDtypeStruct((t, d), F32),
        compiler_params=_cparams(("arbitrary",)),
        name="expert_combine",
    )(start, cnt, x1, rank, aff, g2, norm_f_g.reshape(1, d), ye)


def _trunk_layer(x, batch, seq_len, mods, cond_of_tile, p, s0_f, s0_b, rope, emit_state, final_g):
    t, d = x.shape
    sh1, sc1, g1, sh2, sc2, g2 = mods
    z, qkvg, gates = _in_projection(x, p["norm1_g"], sc1, sh1, p["w_in"], cond_of_tile)
    y_hy = _hyena_branch(z.reshape(batch, seq_len, -1), p).reshape(t, -1)
    ret = _retention(qkvg.reshape(batch, seq_len, -1), p["ret_decay_fwd"], p["ret_decay_bwd"], p["ret_gn_g"],
                     s0_f, s0_b, rope, emit_state)
    y_ret = ret[0].reshape(t, -1)
    x1, h2, aff = _out_projection(y_hy, y_ret, gates, x, g1, sc2, sh2, p["norm2_g"], p["w_hy_o"], p["w_ret_o"],
                                  p["w_out"], p["w_router"], cond_of_tile)
    cap = (EC_CAPACITY_FACTOR * t) // N_EXPERTS
    rank, cnt, start, tot = _select(aff, cap)
    xs = _gather(h2, rank, start, cnt, tot, cap)
    ye = _experts(xs, tot, p["w_e_gate"], p["w_e_up"], p["w_e_down"])
    out = _combine(x1, rank, aff, start, cnt, g2, final_g, ye, cond_of_tile)
    return out, ret[1:]


def kernel(x_prompt, x_sample, state_ret_fwd, state_ret_bwd, c, c_ctx, w_mod, b_mod, norm1_g, w_in, hy_conv_w, hy_conv_b, hy_f_w1, hy_f_b1, hy_f_freq, hy_f_w2, hy_f_b2, hy_f_w3, hy_decay, hy_bias, ret_decay_fwd, ret_decay_bwd, ret_gn_g, w_hy_o, w_ret_o, w_out, norm2_g, w_router, w_e_gate, w_e_up, w_e_down, norm_f_g):
    b, s, d = x_prompt.shape
    bd, sd, _ = x_sample.shape
    depth = w_mod.shape[0]
    assert (b * s) % TOKEN_TILE == 0 and sd % TOKEN_TILE == 0 and b % 2 == 0 and bd % 2 == 0
    ncond = -(-(bd + 1) // 8) * 8
    conds = jnp.concatenate([c, c_ctx[None], jnp.zeros((ncond - bd - 1, d), F32)], axis=0)
    ctx_cond = lambda i: bd
    lat_cond = lambda i: i // (sd // TOKEN_TILE)
    xp = x_prompt.reshape(b * s, d)
    xs = x_sample.reshape(bd * sd, d)
    new_f, new_b = [], []
    for l in range(depth):
        mod = _modulation(conds, w_mod[l], b_mod[l])
        mods = [mod[:, k * d:(k + 1) * d].reshape(ncond, 1, d) for k in range(6)]
        p = {"norm1_g": norm1_g[l], "w_in": w_in[l].astype(BF16), "hy_conv_w": hy_conv_w[l],
             "hy_conv_b": hy_conv_b[l], "hy_f_w1": hy_f_w1[l], "hy_f_b1": hy_f_b1[l], "hy_f_freq": hy_f_freq[l],
             "hy_f_w2": hy_f_w2[l], "hy_f_b2": hy_f_b2[l], "hy_f_w3": hy_f_w3[l], "hy_decay": hy_decay[l],
             "hy_bias": hy_bias[l], "ret_decay_fwd": ret_decay_fwd[l], "ret_decay_bwd": ret_decay_bwd[l],
             "ret_gn_g": ret_gn_g[l], "w_hy_o": w_hy_o[l].astype(BF16), "w_ret_o": w_ret_o[l].astype(BF16),
             "w_out": w_out[l].astype(BF16), "norm2_g": norm2_g[l], "w_router": w_router[l],
             "w_e_gate": w_e_gate[l], "w_e_up": w_e_up[l], "w_e_down": w_e_down[l]}
        final_g = norm_f_g if l == depth - 1 else None
        xp, (s_f, s_b) = _trunk_layer(xp, b, s, mods, ctx_cond, p, None, None, False, True, final_g)
        new_f.append(s_f)
        new_b.append(s_b)
        xs, _ = _trunk_layer(xs, bd, sd, mods, lat_cond, p, state_ret_fwd[:, l], state_ret_bwd[:, l],
                             True, False, final_g)
    y_prompt = xp.reshape(b, s, d)
    y_sample = xs.reshape(bd, sd, d)
    return (y_prompt, y_sample, jnp.stack(new_f, axis=1), jnp.stack(new_b, axis=1))
```

```python
import functools
import math

import numpy as np
import jax
import jax.numpy as jnp
from jax import lax
from jax.experimental import pallas as pl
from jax.experimental.pallas import tpu as pltpu

F32 = jnp.float32
BF16 = jnp.bfloat16
HIGHEST = lax.Precision.HIGHEST

EPS = 1e-6
D_HYENA = 512
D_RET = 512
N_RET_HEADS = 4
RET_HEAD_DIM = 128
RET_CHUNK = 128
GRID_W = 64
FILTER_EMB = 33
ROPE_BASE = 10000.0
N_EXPERTS = 16
EC_CAPACITY_FACTOR = 2

TOKEN_TILE = 512
SLOT_ROWS = 128
ROW_ALIGN = 16
EXPERT_BLOCK = 512
FFT_MINOR = 64
FFT_GROUP = 4
VMEM_LIMIT = 56 * 1024 * 1024


def _cparams(sem):
    return pltpu.CompilerParams(dimension_semantics=sem, vmem_limit_bytes=VMEM_LIMIT)


def _silu(x):
    return x * jax.nn.sigmoid(x)


def _mod_kernel(c_ref, w_ref, b_ref, o_ref):
    s = _silu(c_ref[...])
    o_ref[...] = jnp.dot(s, w_ref[...], preferred_element_type=F32, precision=HIGHEST) + b_ref[...]


def _modulation(conds, w_mod, b_mod):
    nc, d = conds.shape
    return pl.pallas_call(
        _mod_kernel,
        grid=(6,),
        in_specs=[pl.BlockSpec((nc, d), lambda j: (0, 0)),
                  pl.BlockSpec((d, d), lambda j: (0, j)),
                  pl.BlockSpec((1, d), lambda j: (0, j))],
        out_specs=pl.BlockSpec((nc, d), lambda j: (0, j)),
        out_shape=jax.ShapeDtypeStruct((nc, 6 * d), F32),
        compiler_params=_cparams(("arbitrary",)),
        name="modulation",
    )(conds, w_mod, b_mod.reshape(1, -1))


def _inproj_kernel(x_ref, g_ref, sc_ref, sh_ref, w_ref, *rest, splits, rope):
    if rope:
        cos_ref, sin_ref, z_ref, qkvg_ref, gates_ref = rest
    else:
        z_ref, qkvg_ref, gates_ref = rest
    x = x_ref[...]
    ms = jnp.mean(x * x, axis=-1, keepdims=True)
    h = x * lax.rsqrt(ms + EPS) * g_ref[...]
    h = h * (1.0 + sc_ref[0]) + sh_ref[0]
    hb = h.astype(BF16)
    nz, nq, ng = splits
    cw = 512
    dh = RET_HEAD_DIM
    for c0 in range(0, nz + nq + ng, cw):
        acc = jnp.dot(hb, w_ref[:, c0:c0 + cw], preferred_element_type=F32)
        if c0 < nz:
            z_ref[:, c0:c0 + cw] = acc.astype(BF16)
        elif c0 < nz + nq:
            part = (c0 - nz) // D_RET
            if part == 1:
                acc = acc * (dh ** -0.5)
            if rope and part < 2:
                lane = lax.broadcasted_iota(jnp.int32, (acc.shape[0], dh), 1)
                swap_hi = (lane % (dh // 2)) < (dh // 4)
                cs, sn = cos_ref[...], sin_ref[...]
                heads = []
                for hh in range(cw // dh):
                    xh = acc[:, hh * dh:(hh + 1) * dh]
                    rot = jnp.where(swap_hi, pltpu.roll(xh, dh - dh // 4, axis=1), pltpu.roll(xh, dh // 4, axis=1))
                    heads.append(xh * cs + rot * sn)
                acc = jnp.concatenate(heads, axis=1)
            qkvg_ref[:, c0 - nz:c0 - nz + cw] = acc.astype(BF16)
        else:
            gates_ref[:, c0 - nz - nq:c0 - nz - nq + cw] = jax.nn.sigmoid(acc).astype(BF16)


def _in_projection(x, norm_g, sc, sh, w_in_bf, cond_of_tile, rope_len):
    t, d = x.shape
    splits = (3 * D_HYENA, 4 * D_RET, 2 * d)
    assert D_RET == 512
    tm = TOKEN_TILE
    row = lambda i: (i, 0)
    cond = lambda i: (cond_of_tile(i), 0, 0)
    rope = rope_len is not None
    rope_args, rope_specs = [], []
    if rope:
        tiles_per_seq = rope_len // tm
        rope_args = list(_rope_tables(rope_len))
        rope_specs = [pl.BlockSpec((tm, RET_HEAD_DIM), lambda i: (i % tiles_per_seq, 0))] * 2
    return pl.pallas_call(
        functools.partial(_inproj_kernel, splits=splits, rope=rope),
        grid=(t // tm,),
        in_specs=[pl.BlockSpec((tm, d), row),
                  pl.BlockSpec((1, d), lambda i: (0, 0)),
                  pl.BlockSpec((1, 1, d), cond),
                  pl.BlockSpec((1, 1, d), cond),
                  pl.BlockSpec(w_in_bf.shape, lambda i: (0, 0))] + rope_specs,
        out_specs=[pl.BlockSpec((tm, splits[0]), row),
                   pl.BlockSpec((tm, splits[1]), row),
                   pl.BlockSpec((tm, splits[2]), row)],
        out_shape=[jax.ShapeDtypeStruct((t, splits[0]), BF16),
                   jax.ShapeDtypeStruct((t, splits[1]), BF16),
                   jax.ShapeDtypeStruct((t, splits[2]), BF16)],
        compiler_params=_cparams(("parallel",)),
        name="in_projection",
    )(x, norm_g.reshape(1, d), sc, sh, w_in_bf, *rope_args)


def _filter_features(seq_len):
    t = np.linspace(0.0, 1.0, seq_len, dtype=np.float32)[:, None]
    bands = (FILTER_EMB - 1) // 2
    w = (np.float32(2.0 * math.pi) * np.arange(seq_len, dtype=np.float32)) / np.float32(seq_len)
    f = np.linspace(1e-4, bands - 1, bands, dtype=np.float32)
    ang = (w[:, None] * f[None, :]).astype(np.float64)
    z = np.concatenate([t, np.cos(ang), -np.sin(ang)], axis=-1).astype(np.float32)
    return np.pad(z, ((0, 0), (0, 128 - FILTER_EMB)))


def _filter_kernel(z_ref, w1_ref, b1_ref, fr_ref, w2_ref, b2_ref, w3_ref, dec_ref, o_ref):
    z = z_ref[...]
    fr = fr_ref[...]
    dot = functools.partial(jnp.dot, preferred_element_type=F32, precision=HIGHEST)
    h = jnp.sin(fr * (dot(z, w1_ref[...]) + b1_ref[...]))
    h = jnp.sin(fr * (dot(h, w2_ref[...]) + b2_ref[...]))
    h = dot(h, w3_ref[...]) * jnp.exp(-z[:, 0:1] * jnp.abs(dec_ref[...]))
    rows = h.shape[0]
    grow = pl.program_id(0) * rows + lax.broadcasted_iota(jnp.int32, (rows, 1), 0)
    o_ref[0] = h[:, :D_HYENA].astype(o_ref.dtype)
    o_ref[1] = jnp.zeros((rows, D_HYENA), o_ref.dtype)
    o_ref[2] = jnp.where(grow == 0, 0.0, h[:, D_HYENA:]).astype(o_ref.dtype)
    o_ref[3] = jnp.zeros((rows, D_HYENA), o_ref.dtype)


def _hyena_filters(seq_len, p, out_dtype):
    z = jnp.asarray(_filter_features(seq_len))
    w1 = jnp.pad(p["hy_f_w1"], ((0, 128 - FILTER_EMB), (0, 0)))
    rows = min(seq_len, 512)
    full = lambda a: pl.BlockSpec(a.shape, lambda i: (0,) * a.ndim)
    ops = [w1, p["hy_f_b1"].reshape(1, -1), p["hy_f_freq"].reshape(1, -1), p["hy_f_w2"],
           p["hy_f_b2"].reshape(1, -1), p["hy_f_w3"], p["hy_decay"].reshape(1, -1)]
    return pl.pallas_call(
        _filter_kernel,
        grid=(seq_len // rows,),
        in_specs=[pl.BlockSpec((rows, 128), lambda i: (i, 0))] + [full(a) for a in ops],
        out_specs=pl.BlockSpec((4, rows, D_HYENA), lambda i: (0, i, 0)),
        out_shape=jax.ShapeDtypeStruct((4, seq_len, D_HYENA), out_dtype),
        compiler_params=_cparams(("arbitrary",)),
        name="hyena_filter",
    )(z, *ops)


def _shortconv_kernel(v_ref, x1_ref, x2_ref, wv_ref, w1_ref, w2_ref, bv_ref, b1_ref, b2_ref, u_ref, x2c_ref):
    seq_len = v_ref.shape[1]
    row = lax.broadcasted_iota(jnp.int32, (seq_len, 1), 0)

    def conv(x_ref, w_ref, b_ref):
        x = x_ref[0].astype(F32)
        prev = jnp.where(row == 0, 0.0, pltpu.roll(x, 1, axis=0))
        nxt = jnp.where(row == seq_len - 1, 0.0, pltpu.roll(x, seq_len - 1, axis=0))
        return b_ref[...] + prev * w_ref[0:1] + x * w_ref[1:2] + nxt * w_ref[2:3]

    u_ref[0] = (conv(v_ref, wv_ref, bv_ref) * conv(x1_ref, w1_ref, b1_ref)).astype(u_ref.dtype)
    x2c_ref[0] = conv(x2_ref, w2_ref, b2_ref).astype(x2c_ref.dtype)


def _short_conv(z, conv_w, conv_b):
    b, seq_len, _ = z.shape
    cb = 128
    nb = D_HYENA // cb
    zspec = lambda off: pl.BlockSpec((1, seq_len, cb), lambda i, j: (i, 0, off + j))
    wspec = lambda off: pl.BlockSpec((3, cb), lambda i, j: (0, off + j))
    bspec = lambda off: pl.BlockSpec((1, cb), lambda i, j: (0, off + j))
    ospec = pl.BlockSpec((1, seq_len, cb), lambda i, j: (i, 0, j))
    cbias = conv_b.reshape(1, -1)
    return pl.pallas_call(
        _shortconv_kernel,
        grid=(b, nb),
        in_specs=[zspec(0), zspec(nb), zspec(2 * nb), wspec(0), wspec(nb), wspec(2 * nb),
                  bspec(0), bspec(nb), bspec(2 * nb)],
        out_specs=[ospec, ospec],
        out_shape=[jax.ShapeDtypeStruct((b, seq_len, D_HYENA), BF16)] * 2,
        compiler_params=_cparams(("parallel", "parallel")),
        name="hyena_short_conv",
    )(z, z, z, conv_w, conv_w, conv_w, cbias, cbias, cbias)


def _stacked_dft(n_out, n_in, modulus, scale=1.0):
    k = lax.broadcasted_iota(jnp.int32, (n_out, n_in), 0)
    n = lax.broadcasted_iota(jnp.int32, (n_out, n_in), 1)
    th = ((k * n) % modulus).astype(F32) * (2.0 * math.pi / modulus)
    c, s = jnp.cos(th), jnp.sin(th)
    fwd = jnp.block([[c, s], [-s, c]])
    inv = jnp.block([[c.T, -s.T], [s.T, c.T]]) * scale
    return fwd, inv


def _second_level_tables(n_total, n1, n2):
    shape = (n1, n2, n2)
    k1 = lax.broadcasted_iota(jnp.int32, shape, 0)
    k2 = lax.broadcasted_iota(jnp.int32, shape, 1)
    m = lax.broadcasted_iota(jnp.int32, shape, 2)
    th = ((m * (k1 + n1 * k2)) % n_total).astype(F32) * (2.0 * math.pi / n_total)
    c, s = jnp.cos(th), jnp.sin(th)
    g = jnp.concatenate([jnp.concatenate([c, s], axis=2), jnp.concatenate([-s, c], axis=2)], axis=1)
    return g, jnp.swapaxes(g, 1, 2)


def _fft1_core(u_sc, f_ref, a_ref, t_sc):
    n2 = FFT_MINOR
    _, seq_len, cb = u_sc.shape
    h1 = seq_len // n2
    nk = t_sc.shape[0] // n2

    def fwd(gi, carry):
        m0 = gi * FFT_GROUP
        cols = []
        for d in range(FFT_GROUP):
            xr = u_sc[0, pl.ds(m0 + d, h1, stride=n2), :]
            xi = u_sc[1, pl.ds(m0 + d, h1, stride=n2), :]
            cols.append(jnp.concatenate([xr, xi], axis=0))
        x = jnp.concatenate(cols, axis=1).astype(BF16)
        res = jnp.dot(f_ref[...], x, preferred_element_type=F32)
        for d in range(FFT_GROUP):
            t_sc[pl.ds(pl.multiple_of((m0 + d) * nk, nk), nk), :] = res[:, d * cb:(d + 1) * cb]
        return carry

    lax.fori_loop(0, n2 // FFT_GROUP, fwd, 0)

    def transpose(k, carry):
        a_ref[0, k] = t_sc[pl.ds(k, n2, stride=nk), :].astype(a_ref.dtype)
        return carry

    lax.fori_loop(0, nk, transpose, 0, unroll=8)


def _fft1_kernel(h_ref, f_ref, a_ref, u_sc, t_sc):
    u_sc[...] = h_ref[...].astype(F32)
    _fft1_core(u_sc, f_ref, a_ref, t_sc)


def _conv_fft1_kernel(v_ref, x1_ref, x2_ref, wv_ref, w1_ref, w2_ref, bv_ref, b1_ref, b2_ref, f_ref,
                      u_ref, x2c_ref, a_ref, u_sc, t_sc):
    seq_len = v_ref.shape[1]
    row = lax.broadcasted_iota(jnp.int32, (seq_len, 1), 0)

    def conv(x_ref, b, w_ref, b_ref):
        x = x_ref[b].astype(F32)
        prev = jnp.where(row == 0, 0.0, pltpu.roll(x, 1, axis=0))
        nxt = jnp.where(row == seq_len - 1, 0.0, pltpu.roll(x, seq_len - 1, axis=0))
        return b_ref[...] + prev * w_ref[0:1] + x * w_ref[1:2] + nxt * w_ref[2:3]

    for b in range(2):
        u = conv(v_ref, b, wv_ref, bv_ref) * conv(x1_ref, b, w1_ref, b1_ref)
        u_sc[b] = u
        u_ref[b] = u.astype(u_ref.dtype)
        x2c_ref[b] = conv(x2_ref, b, w2_ref, b2_ref).astype(x2c_ref.dtype)
    _fft1_core(u_sc, f_ref, a_ref, t_sc)


def _fft1_scratch(seq_len, cb, n_rows):
    return [pltpu.VMEM((2, seq_len, cb), F32), pltpu.VMEM((FFT_MINOR * n_rows, cb), F32)]


def _fft_first_level(h, f1):
    b, seq_len, c = h.shape
    cb = 128
    return pl.pallas_call(
        _fft1_kernel,
        grid=(b // 2, c // cb),
        in_specs=[pl.BlockSpec((2, seq_len, cb), lambda i, j: (i, 0, j)),
                  pl.BlockSpec(f1.shape, lambda i, j: (0, 0))],
        out_specs=pl.BlockSpec((1, f1.shape[0], FFT_MINOR, cb), lambda i, j: (i, 0, 0, j)),
        out_shape=jax.ShapeDtypeStruct((b // 2, f1.shape[0], FFT_MINOR, c), BF16),
        scratch_shapes=_fft1_scratch(seq_len, cb, f1.shape[0]),
        compiler_params=_cparams(("parallel", "parallel")),
        name="hyena_dft_level1",
    )(h, f1)


def _conv_fft_first_level(z, conv_w, conv_b, f1):
    b, seq_len, _ = z.shape
    cb = 128
    nb = D_HYENA // cb
    zspec = lambda off: pl.BlockSpec((2, seq_len, cb), lambda i, j: (i, 0, off + j))
    wspec = lambda off: pl.BlockSpec((3, cb), lambda i, j: (0, off + j))
    bspec = lambda off: pl.BlockSpec((1, cb), lambda i, j: (0, off + j))
    ospec = pl.BlockSpec((2, seq_len, cb), lambda i, j: (i, 0, j))
    cbias = conv_b.reshape(1, -1)
    return pl.pallas_call(
        _conv_fft1_kernel,
        grid=(b // 2, nb),
        in_specs=[zspec(0), zspec(nb), zspec(2 * nb), wspec(0), wspec(nb), wspec(2 * nb),
                  bspec(0), bspec(nb), bspec(2 * nb), pl.BlockSpec(f1.shape, lambda i, j: (0, 0))],
        out_specs=[ospec, ospec,
                   pl.BlockSpec((1, f1.shape[0], FFT_MINOR, cb), lambda i, j: (i, 0, 0, j))],
        out_shape=[jax.ShapeDtypeStruct((b, seq_len, D_HYENA), BF16)] * 2
        + [jax.ShapeDtypeStruct((b // 2, f1.shape[0], FFT_MINOR, D_HYENA), BF16)],
        scratch_shapes=_fft1_scratch(seq_len, cb, f1.shape[0]),
        compiler_params=_cparams(("parallel", "parallel")),
        name="hyena_conv_dft_level1",
    )(z, z, z, conv_w, conv_w, conv_w, cbias, cbias, cbias, f1)


def _fft_s2f_kernel(a_ref, g_ref, kr_ref, ki_ref):
    _, _, kb, n2, c = a_ref.shape
    for kk in range(kb):
        g = g_ref[kk]
        hf = jnp.dot(g, a_ref[0, :, kk].reshape(2 * n2, c), preferred_element_type=F32)
        hb = jnp.dot(g, a_ref[1, :, kk].reshape(2 * n2, c), preferred_element_type=F32)
        kr_ref[kk] = hf[:n2] + hb[:n2]
        ki_ref[kk] = hf[n2:] - hb[n2:]


def _fft_filter_second_level(a, g, kb=8):
    _, _, n1, n2, c = a.shape
    spec = pl.BlockSpec((kb, n2, c), lambda i: (i, 0, 0))
    return pl.pallas_call(
        _fft_s2f_kernel,
        grid=(n1 // kb,),
        in_specs=[pl.BlockSpec((2, 2, kb, n2, c), lambda i: (0, 0, i, 0, 0)),
                  pl.BlockSpec((kb, 2 * n2, 2 * n2), lambda i: (i, 0, 0))],
        out_specs=[spec, spec],
        out_shape=[jax.ShapeDtypeStruct((n1, n2, c), F32)] * 2,
        compiler_params=_cparams(("parallel",)),
        name="hyena_filter_spectrum",
    )(a, g)


def _fft_s2_kernel(a_ref, g_ref, gt_ref, kr_ref, ki_ref, b_ref, r_sc):
    npairs, _, kb, n2, cb = a_ref.shape
    for p in range(npairs):
        for kk in range(kb):
            x = jnp.dot(g_ref[kk], a_ref[p, :, kk].reshape(2 * n2, cb), preferred_element_type=F32)
            xr, xi = x[:n2], x[n2:]
            kr, ki = kr_ref[kk], ki_ref[kk]
            y = jnp.concatenate([xr * kr - xi * ki, xr * ki + xi * kr], axis=0).astype(BF16)
            res = jnp.dot(gt_ref[kk], y, preferred_element_type=F32)
            for q in range(cb // 128):
                r_sc[q, kk * 2 * n2:(kk + 1) * 2 * n2, :] = res[:, q * 128:(q + 1) * 128]

        def gather(row):
            return jnp.concatenate([r_sc[q, pl.ds(row, kb, stride=2 * n2), :] for q in range(cb // 128)], axis=1)

        def transpose(m, carry):
            b_ref[p, 0, m] = gather(m).astype(b_ref.dtype)
            b_ref[p, 1, m] = gather(n2 + m).astype(b_ref.dtype)
            return carry

        lax.fori_loop(0, n2, transpose, 0, unroll=8)


def _fft_second_level(a, g, gt, kr, ki):
    p, _, n1, n2, c = a.shape
    kb, cb = 16, 256
    gspec = pl.BlockSpec((kb, 2 * n2, 2 * n2), lambda i, j: (i, 0, 0))
    kspec = pl.BlockSpec((kb, n2, cb), lambda i, j: (i, 0, j))
    return pl.pallas_call(
        _fft_s2_kernel,
        grid=(n1 // kb, c // cb),
        in_specs=[pl.BlockSpec((p, 2, kb, n2, cb), lambda i, j: (0, 0, i, 0, j)), gspec, gspec, kspec, kspec],
        out_specs=pl.BlockSpec((p, 2, n2, kb, cb), lambda i, j: (0, 0, 0, i, j)),
        out_shape=jax.ShapeDtypeStruct((p, 2, n2, n1, c), BF16),
        scratch_shapes=[pltpu.VMEM((cb // 128, kb * 2 * n2, 128), F32)],
        compiler_params=_cparams(("parallel", "parallel")),
        name="hyena_dft_level2",
    )(a, g, gt, kr, ki)


def _fft_s3_kernel(b_ref, f_ref, u_ref, x2_ref, bias_ref, o_ref, t_sc):
    _, _, n2, n1, cb = b_ref.shape
    nr = t_sc.shape[0] // n2
    h1 = nr // 2

    def inv(gi, carry):
        m0 = gi * FFT_GROUP
        x = jnp.concatenate([b_ref[0, :, m0 + d].reshape(2 * n1, cb) for d in range(FFT_GROUP)], axis=1)
        res = jnp.dot(f_ref[...], x, preferred_element_type=F32)
        for d in range(FFT_GROUP):
            t_sc[pl.ds(pl.multiple_of((m0 + d) * nr, nr), nr), :] = res[:, d * cb:(d + 1) * cb]
        return carry

    lax.fori_loop(0, n2 // FFT_GROUP, inv, 0)

    bias = bias_ref[...]
    for b in range(2):
        def finish(q, carry):
            rows = pl.ds(pl.multiple_of(q * n2, n2), n2)
            conv = t_sc[pl.ds(b * h1 + q, n2, stride=nr), :]
            u = u_ref[b, rows, :].astype(F32)
            o_ref[b, rows, :] = ((conv + u * bias) * x2_ref[b, rows, :].astype(F32)).astype(o_ref.dtype)
            return carry

        lax.fori_loop(0, h1, finish, 0, unroll=4)


def _fft_last_level(bt, f1inv, u, x2c, bias):
    p, _, n2, n1, c = bt.shape
    b, seq_len, _ = u.shape
    cb = 128
    uspec = pl.BlockSpec((2, seq_len, cb), lambda i, j: (i, 0, j))
    return pl.pallas_call(
        _fft_s3_kernel,
        grid=(p, c // cb),
        in_specs=[pl.BlockSpec((1, 2, n2, n1, cb), lambda i, j: (i, 0, 0, 0, j)),
                  pl.BlockSpec(f1inv.shape, lambda i, j: (0, 0)),
                  uspec, uspec,
                  pl.BlockSpec((1, cb), lambda i, j: (0, j))],
        out_specs=uspec,
        out_shape=jax.ShapeDtypeStruct(u.shape, BF16),
        scratch_shapes=[pltpu.VMEM((n2 * f1inv.shape[0], cb), F32)],
        compiler_params=_cparams(("parallel", "parallel")),
        name="hyena_dft_inverse",
    )(bt, f1inv, u, x2c, bias)


def _kf_direct_kernel(h_ref, f_ref, kr_ref, ki_ref):
    n = kr_ref.shape[0]
    dot = functools.partial(jnp.dot, preferred_element_type=F32, precision=HIGHEST)
    hf = dot(f_ref[...], h_ref[0])
    hb = dot(f_ref[...], h_ref[2])
    kr_ref[...] = hf[:n] + hb[:n]
    ki_ref[...] = hf[n:] - hb[n:]


def _hyena_direct_kernel(u_ref, x2_ref, f_ref, fi_ref, kr_ref, ki_ref, bias_ref, o_ref):
    _, two, seq_len, cb = u_ref.shape
    n = kr_ref.shape[0]
    ub = u_ref[0].reshape(two * seq_len, cb)
    x = jnp.dot(f_ref[...], ub, preferred_element_type=F32)
    xr, xi = x[:n], x[n:]
    kr, ki = kr_ref[...], ki_ref[...]
    y = jnp.concatenate([xr * kr - xi * ki, xr * ki + xi * kr], axis=0).astype(BF16)
    conv = jnp.dot(fi_ref[...], y, preferred_element_type=F32)
    x2 = x2_ref[0].reshape(two * seq_len, cb).astype(F32)
    o = (conv + ub.astype(F32) * bias_ref[...]) * x2
    o_ref[0] = o.reshape(two, seq_len, cb).astype(o_ref.dtype)


def _hyena_branch(z, p):
    b, seq_len, _ = z.shape
    c = D_HYENA
    n = 2 * seq_len
    bias = p["hy_bias"].reshape(1, c)
    hh = _hyena_filters(seq_len, p, F32)
    if seq_len <= 512:
        u, x2c = _short_conv(z, p["hy_conv_w"], p["hy_conv_b"])
        fwd, inv = _stacked_dft(n, seq_len, n, scale=1.0 / n)
        kr, ki = pl.pallas_call(
            _kf_direct_kernel,
            out_shape=[jax.ShapeDtypeStruct((n, c), F32)] * 2,
            compiler_params=_cparams(None),
            name="hyena_filter_spectrum_direct",
        )(hh, fwd[:, :seq_len])
        cb = 256
        pair = lambda a: a.reshape(b // 2, 2, seq_len, c)
        uspec = pl.BlockSpec((1, 2, seq_len, cb), lambda i, j: (i, 0, 0, j))
        kspec = pl.BlockSpec((n, cb), lambda i, j: (0, j))
        out = pl.pallas_call(
            _hyena_direct_kernel,
            grid=(b // 2, c // cb),
            in_specs=[uspec, uspec,
                      pl.BlockSpec(fwd.shape, lambda i, j: (0, 0)),
                      pl.BlockSpec(inv.shape, lambda i, j: (0, 0)),
                      kspec, kspec,
                      pl.BlockSpec((1, cb), lambda i, j: (0, j))],
            out_specs=uspec,
            out_shape=jax.ShapeDtypeStruct((b // 2, 2, seq_len, c), BF16),
            compiler_params=_cparams(("parallel", "parallel")),
            name="hyena_dft_direct",
        )(pair(u), pair(x2c), fwd.astype(BF16), inv.astype(BF16), kr, ki, bias)
        return out.reshape(b, seq_len, c)

    n2 = FFT_MINOR
    n1 = n // n2
    h1 = seq_len // n2
    f1, f1inv = _stacked_dft(n1, h1, n1, scale=1.0 / n)
    f1, f1inv = f1.astype(BF16), f1inv.astype(BF16)
    g, gt = _second_level_tables(n, n1, n2)
    g, gt = g.astype(BF16), gt.astype(BF16)
    ha = _fft_first_level(hh, f1)
    kr, ki = _fft_filter_second_level(ha.reshape(2, 2, n1, n2, c), g)
    u, x2c, a = _conv_fft_first_level(z, p["hy_conv_w"], p["hy_conv_b"], f1)
    bt = _fft_second_level(a.reshape(b // 2, 2, n1, n2, c), g, gt, kr, ki)
    return _fft_last_level(bt, f1inv, u, x2c, bias)


def _rope_tables(seq_len):
    half = RET_HEAD_DIM // 2
    nf = half // 2
    t = jnp.arange(seq_len)
    row = (t // GRID_W).astype(F32)
    col = (t % GRID_W).astype(F32)
    inv = ROPE_BASE ** (-jnp.arange(nf, dtype=F32) / nf)
    ar = row[:, None] * inv[None, :]
    ac = col[:, None] * inv[None, :]
    cos = jnp.concatenate([jnp.cos(ar), jnp.cos(ar), jnp.cos(ac), jnp.cos(ac)], axis=-1)
    sin = jnp.concatenate([-jnp.sin(ar), jnp.sin(ar), -jnp.sin(ac), jnp.sin(ac)], axis=-1)
    return cos, sin


def _log_sigmoid(x):
    return jnp.minimum(x, 0.0) - jnp.log1p(jnp.exp(-jnp.abs(x)))


def _retention_kernel(*refs, has_init, emit_state, cpb):
    refs = list(refs)
    q_ref, k_ref, v_ref, g_ref, dec_ref, gn_ref = refs[:6]
    del refs[:6]
    if has_init:
        s0f_ref, s0b_ref = refs[:2]
        del refs[:2]
    o_ref = refs.pop(0)
    if emit_state:
        sf_out, sb_out = refs[:2]
        del refs[:2]
    sf_ref, sb_ref, sball_ref = refs

    c = RET_CHUNK
    dh = RET_HEAD_DIM
    nh = N_RET_HEADS
    phase = pl.program_id(1)
    j = pl.program_id(2)
    nb = pl.num_programs(2)

    ri = lax.broadcasted_iota(jnp.int32, (c, c), 0).astype(F32)
    ci = lax.broadcasted_iota(jnp.int32, (c, c), 1).astype(F32)
    diff = ri - ci

    def head_consts(h):
        lgf = _log_sigmoid(dec_ref[0, h])[0:1, :]
        lgb = _log_sigmoid(dec_ref[1, h])[0:1, :]
        return lgf, lgb

    def load(ref, r0, h):
        return ref[0, r0:r0 + c, h * dh:(h + 1) * dh]

    dn_t = (((0,), (0,)), ((), ()))
    dn_nt = (((1,), (1,)), ((), ()))

    @pl.when(phase == 0)
    def _backward_sweep():
        @pl.when(j == 0)
        def _():
            for h in range(nh):
                sb_ref[h] = s0b_ref[0, h] if has_init else jnp.zeros((dh, dh), F32)

        blk = nb - 1 - j
        for h in range(nh):
            _, lgb = head_consts(h)
            zeta_b = jnp.exp(lgb * ri)
            cdec_b = jnp.exp(lgb * float(c))
            for cc in reversed(range(cpb)):
                r0 = cc * c
                n = blk * cpb + cc
                s = sb_ref[h]
                sball_ref[n, h] = s.astype(BF16)
                kz = (load(k_ref, r0, h).astype(F32) * zeta_b).astype(BF16)
                vv = load(v_ref, r0, h)
                sb_ref[h] = cdec_b * s + lax.dot_general(kz, vv, dn_t, preferred_element_type=F32)

        if emit_state:
            @pl.when(j == nb - 1)
            def _():
                for h in range(nh):
                    sb_out[0, h] = sb_ref[h]

    @pl.when(phase == 1)
    def _forward_sweep():
        @pl.when(j == 0)
        def _():
            for h in range(nh):
                sf_ref[h] = s0f_ref[0, h] if has_init else jnp.zeros((dh, dh), F32)

        for h in range(nh):
            lgf, lgb = head_consts(h)
            mask = (jnp.where(diff >= 0, jnp.exp(lgf * jnp.maximum(diff, 0.0)), 0.0)
                    + jnp.where(diff <= 0, jnp.exp(lgb * jnp.maximum(-diff, 0.0)), 0.0))
            xi_f = jnp.exp(lgf * (ri + 1.0))
            xi_b = jnp.exp(lgb * (float(c) - ri))
            zeta_f = jnp.exp(lgf * (float(c - 1) - ri))
            cdec_f = jnp.exp(lgf * float(c))
            gn = gn_ref[:, h * dh:(h + 1) * dh]
            for cc in range(cpb):
                r0 = cc * c
                n = j * cpb + cc
                qb = load(q_ref, r0, h)
                kb = load(k_ref, r0, h)
                vv = load(v_ref, r0, h)
                gate = load(g_ref, r0, h).astype(F32)
                sc = lax.dot_general(qb, kb, dn_nt, preferred_element_type=F32)
                inner = jnp.dot((sc * mask).astype(BF16), vv, preferred_element_type=F32)
                s = sf_ref[h]
                q = qb.astype(F32)
                lhs = jnp.concatenate([q * xi_f, q * xi_b], axis=1).astype(BF16)
                rhs = jnp.concatenate([s.astype(BF16), sball_ref[n, h]], axis=0)
                o = inner + jnp.dot(lhs, rhs, preferred_element_type=F32)
                mu = jnp.mean(o, axis=-1, keepdims=True)
                d = o - mu
                var = jnp.mean(d * d, axis=-1, keepdims=True)
                y = d * lax.rsqrt(var + EPS) * gn * _silu(gate)
                o_ref[0, r0:r0 + c, h * dh:(h + 1) * dh] = y.astype(o_ref.dtype)
                kz = (kb.astype(F32) * zeta_f).astype(BF16)
                sf_ref[h] = cdec_f * s + lax.dot_general(kz, vv, dn_t, preferred_element_type=F32)

        if emit_state:
            @pl.when(j == nb - 1)
            def _():
                for h in range(nh):
                    sf_out[0, h] = sf_ref[h]


def _retention(qkvg, dec_f, dec_b, gn_g, s0_f, s0_b, emit_state):
    b, seq_len, _ = qkvg.shape
    nh, dh, c = N_RET_HEADS, RET_HEAD_DIM, RET_CHUNK
    rb = min(seq_len, 512)
    nb = seq_len // rb
    cpb = rb // c
    has_init = s0_f is not None
    dec = jnp.broadcast_to(jnp.stack([dec_f, dec_b])[:, :, None, None], (2, nh, 8, 128)).astype(F32)
    kv_blk = lambda i, p, j: jnp.where(p == 0, nb - 1 - j, j)
    q_blk = lambda i, p, j: jnp.where(p == 0, 0, j)
    in_specs = [pl.BlockSpec((1, rb, D_RET), lambda i, p, j: (i, q_blk(i, p, j), 0)),
                pl.BlockSpec((1, rb, D_RET), lambda i, p, j: (i, kv_blk(i, p, j), 1)),
                pl.BlockSpec((1, rb, D_RET), lambda i, p, j: (i, kv_blk(i, p, j), 2)),
                pl.BlockSpec((1, rb, D_RET), lambda i, p, j: (i, q_blk(i, p, j), 3)),
                pl.BlockSpec((2, nh, 8, 128), lambda i, p, j: (0, 0, 0, 0)),
                pl.BlockSpec((1, D_RET), lambda i, p, j: (0, 0))]
    args = [qkvg, qkvg, qkvg, qkvg, dec, gn_g.reshape(1, -1)]
    sspec = pl.BlockSpec((1, nh, dh, dh), lambda i, p, j: (i, 0, 0, 0))
    if has_init:
        in_specs += [sspec, sspec]
        args += [s0_f, s0_b]
    out_specs = [pl.BlockSpec((1, rb, D_RET), lambda i, p, j: (i, q_blk(i, p, j), 0))]
    out_shape = [jax.ShapeDtypeStruct((b, seq_len, D_RET), BF16)]
    if emit_state:
        out_specs += [sspec, sspec]
        out_shape += [jax.ShapeDtypeStruct((b, nh, dh, dh), F32)] * 2
    return pl.pallas_call(
        functools.partial(_retention_kernel, has_init=has_init, emit_state=emit_state, cpb=cpb),
        grid=(b, 2, nb),
        in_specs=in_specs,
        out_specs=out_specs,
        out_shape=out_shape,
        scratch_shapes=[pltpu.VMEM((nh, dh, dh), F32), pltpu.VMEM((nh, dh, dh), F32),
                        pltpu.VMEM((nb * cpb, nh, dh, dh), BF16)],
        compiler_params=_cparams(("parallel", "arbitrary", "arbitrary")),
        name="retention",
    )(*args)


def _outproj_kernel(yhy_ref, yret_ref, gates_ref, x_ref, g1_ref, sc_ref, sh_ref, ng_ref,
                    why_ref, wret_ref, wout_ref, wr_ref, x1_ref, h2_ref, aff_ref):
    d = x_ref.shape[1]
    a = jnp.dot(yhy_ref[...], why_ref[...], preferred_element_type=F32)
    b = jnp.dot(yret_ref[...], wret_ref[...], preferred_element_type=F32)
    merged = gates_ref[:, :d].astype(F32) * a + gates_ref[:, d:].astype(F32) * b
    out = jnp.dot(merged.astype(BF16), wout_ref[...], preferred_element_type=F32)
    x1 = x_ref[...] + g1_ref[0] * out
    x1_ref[...] = x1
    ms = jnp.mean(x1 * x1, axis=-1, keepdims=True)
    h = x1 * lax.rsqrt(ms + EPS) * ng_ref[...]
    h = h * (1.0 + sc_ref[0]) + sh_ref[0]
    h_hi = h.astype(BF16)
    h2_ref[...] = h_hi
    h_lo = (h - h_hi.astype(F32)).astype(BF16)
    t = jnp.dot(h_hi, wr_ref[...], preferred_element_type=F32)
    logits = t[:, :128] + t[:, 128:] + jnp.dot(h_lo, wr_ref[:, :128], preferred_element_type=F32)
    lane = lax.broadcasted_iota(jnp.int32, logits.shape, 1)
    logits = jnp.where(lane < N_EXPERTS, logits, -jnp.inf)
    e = jnp.exp(logits - jnp.max(logits, axis=-1, keepdims=True))
    aff = e / jnp.sum(e, axis=-1, keepdims=True)
    aff_ref[0] = aff.T[:N_EXPERTS, :]


def _out_projection(y_hy, y_ret, gates, x, g1, sc2, sh2, norm2_g, w_hy_o, w_ret_o, w_out, w_router, cond_of_tile):
    t, d = x.shape
    tm = TOKEN_TILE
    row = lambda i: (i, 0)
    cond = lambda i: (cond_of_tile(i), 0, 0)
    full = lambda a: pl.BlockSpec(a.shape, lambda i: (0, 0))
    wr = jnp.pad(w_router, ((0, 0), (0, 128 - N_EXPERTS)))
    wr_hi = wr.astype(BF16)
    wr = jnp.concatenate([wr_hi, (wr - wr_hi.astype(F32)).astype(BF16)], axis=1)
    return pl.pallas_call(
        _outproj_kernel,
        grid=(t // tm,),
        in_specs=[pl.BlockSpec((tm, D_HYENA), row), pl.BlockSpec((tm, D_RET), row),
                  pl.BlockSpec((tm, 2 * d), row), pl.BlockSpec((tm, d), row),
                  pl.BlockSpec((1, 1, d), cond), pl.BlockSpec((1, 1, d), cond), pl.BlockSpec((1, 1, d), cond),
                  pl.BlockSpec((1, d), lambda i: (0, 0)),
                  full(w_hy_o), full(w_ret_o), full(w_out), full(wr)],
        out_specs=[pl.BlockSpec((tm, d), row), pl.BlockSpec((tm, d), row),
                   pl.BlockSpec((1, N_EXPERTS, tm), lambda i: (i, 0, 0))],
        out_shape=[jax.ShapeDtypeStruct((t, d), F32), jax.ShapeDtypeStruct((t, d), BF16),
                   jax.ShapeDtypeStruct((t // tm, N_EXPERTS, tm), F32)],
        compiler_params=_cparams(("parallel",)),
        name="out_projection_router",
    )(y_hy, y_ret, gates, x, g1, sc2, sh2, norm2_g.reshape(1, d), w_hy_o, w_ret_o, w_out, wr)


def _select_kernel(aff_ref, tri_ref, rank_ref, cnt_ref, start_ref, tot_ref, *, cap, idx_bits):
    nt, ne, tm = aff_ref.shape
    a = aff_ref[...]

    def count(m):
        return jnp.sum(jnp.sum(m, axis=0, keepdims=True), axis=2, keepdims=True)

    def thr_step(s, thr):
        cand = thr | (1 << (30 - s))
        cnt = count(jnp.where(a >= pltpu.bitcast(cand, F32), 1.0, 0.0))
        return jnp.where(cnt >= float(cap), cand, thr)

    thr = pltpu.bitcast(lax.fori_loop(0, 31, thr_step, jnp.zeros((1, ne, 1), jnp.int32)), F32)
    gt = a > thr
    eq = a == thr
    need = float(cap) - count(jnp.where(gt, 1.0, 0.0))
    idx = (lax.broadcasted_iota(jnp.int32, (nt, 1, tm), 0) * tm
           + lax.broadcasted_iota(jnp.int32, (nt, 1, tm), 2))

    def idx_step(s, lim):
        cand = lim | (1 << (idx_bits - 1 - s))
        cnt = count(jnp.where(eq, jnp.where(idx < cand, 1.0, 0.0), 0.0))
        return jnp.where(cnt < need, cand, lim)

    lim = lax.fori_loop(0, idx_bits, idx_step, jnp.zeros((1, ne, 1), jnp.int32))
    sel = jnp.where(gt, 1.0, jnp.where(eq, jnp.where(idx <= lim, 1.0, 0.0), 0.0))
    sel2 = sel.reshape(nt * ne, tm).astype(BF16)
    prefix = jnp.dot(sel2, tri_ref[...], preferred_element_type=F32)
    rank = jnp.where(sel2 > 0, prefix, -1.0).astype(jnp.int32)
    rank_ref[...] = rank.reshape(nt, ne, tm)
    ones = jnp.ones((tm, 128), BF16)
    cnt = jnp.dot(sel2, ones, preferred_element_type=F32).astype(jnp.int32).reshape(nt, ne, 128)
    cnt_ref[...] = cnt
    acc = jnp.zeros((ne, 128), jnp.int32)
    for t in range(nt):
        start_ref[t] = acc
        acc = acc + ((cnt[t] + (ROW_ALIGN - 1)) & (-ROW_ALIGN))
    tot_ref[...] = acc


def _select(aff, cap):
    nt, ne, tm = aff.shape
    idx_bits = max(1, int(math.ceil(math.log2(nt * tm))))
    r = lax.broadcasted_iota(jnp.int32, (tm, tm), 0)
    c = lax.broadcasted_iota(jnp.int32, (tm, tm), 1)
    tri = (r < c).astype(BF16)
    rank, cnt, start, tot = pl.pallas_call(
        functools.partial(_select_kernel, cap=cap, idx_bits=idx_bits),
        out_shape=[jax.ShapeDtypeStruct((nt, ne, tm), jnp.int32),
                   jax.ShapeDtypeStruct((nt, ne, 128), jnp.int32),
                   jax.ShapeDtypeStruct((nt, ne, 128), jnp.int32),
                   jax.ShapeDtypeStruct((ne, 128), jnp.int32)],
        compiler_params=_cparams(None),
        name="expert_choice_select",
    )(aff, tri)
    return rank, cnt[:, :, 0], start[:, :, 0], tot[:, 0]


def _list_rows(cap, nt):
    need = cap + ROW_ALIGN * nt + EXPERT_BLOCK + SLOT_ROWS
    return -(-need // EXPERT_BLOCK) * EXPERT_BLOCK


def _num_rounds(cnt_sm, i):
    m = cnt_sm[i, 0]
    for e in range(1, N_EXPERTS):
        m = jnp.maximum(m, cnt_sm[i, e])
    return jnp.maximum((m + SLOT_ROWS - 1) // SLOT_ROWS, 1)


def _gather_kernel(start_sm, cnt_sm, tot_sm, h_ref, rank_ref, xs_hbm, stage, zbuf, sem):
    i = pl.program_id(0)
    ne = N_EXPERTS
    slot = i % 2
    rank = rank_ref[0]
    sub = lax.broadcasted_iota(jnp.int32, (SLOT_ROWS, rank.shape[1]), 0)

    def copy(s, e, off):
        return pltpu.make_async_copy(stage.at[s, pl.ds(e * SLOT_ROWS, SLOT_ROWS)],
                                     xs_hbm.at[e, pl.ds(off, SLOT_ROWS)], sem.at[e])

    def fill(r):
        h = h_ref[...]
        for e in range(ne):
            onehot = jnp.where(rank[e:e + 1, :] == sub + r * SLOT_ROWS, 1.0, 0.0).astype(BF16)
            stage[slot, e * SLOT_ROWS:(e + 1) * SLOT_ROWS, :] = jnp.dot(
                onehot, h, preferred_element_type=F32).astype(BF16)

    def start_all(r):
        for e in range(ne):
            copy(slot, e, pl.multiple_of(start_sm[i, e] + r * SLOT_ROWS, ROW_ALIGN)).start()

    def wait_all(s):
        for e in range(ne):
            copy(s, e, 0).wait()

    fill(0)

    @pl.when(i > 0)
    def _():
        wait_all(1 - slot)

    start_all(0)

    def extra_round(r, carry):
        wait_all(slot)
        fill(r)
        start_all(r)
        return carry

    lax.fori_loop(1, _num_rounds(cnt_sm, i), extra_round, 0)

    @pl.when(i == pl.num_programs(0) - 1)
    def _zero_tail():
        wait_all(slot)
        zbuf[...] = jnp.zeros(zbuf.shape, zbuf.dtype)

        def zcopy(e):
            off = pl.multiple_of(tot_sm[e], ROW_ALIGN)
            return pltpu.make_async_copy(zbuf, xs_hbm.at[e, pl.ds(off, zbuf.shape[0])], sem.at[e])

        for e in range(ne):
            zcopy(e).start()
        for e in range(ne):
            zcopy(e).wait()


def _gather(h2, rank, start, cnt, tot, cap):
    t, d = h2.shape
    nt, ne, tm = rank.shape
    rl = _list_rows(cap, nt)
    return pl.pallas_call(
        _gather_kernel,
        grid_spec=pltpu.PrefetchScalarGridSpec(
            num_scalar_prefetch=3,
            grid=(nt,),
            in_specs=[pl.BlockSpec((tm, d), lambda i, *_: (i, 0)),
                      pl.BlockSpec((1, ne, tm), lambda i, *_: (i, 0, 0))],
            out_specs=pl.BlockSpec(memory_space=pl.ANY),
            scratch_shapes=[pltpu.VMEM((2, ne * SLOT_ROWS, d), BF16),
                            pltpu.VMEM((EXPERT_BLOCK + SLOT_ROWS, d), BF16),
                            pltpu.SemaphoreType.DMA((ne,))]),
        out_shape=jax.ShapeDtypeStruct((ne, rl, d), BF16),
        compiler_params=_cparams(("arbitrary",)),
        name="expert_gather",
    )(start, cnt, tot, h2, rank)


def _expert_kernel(tot_sm, xs_ref, wg_ref, wu_ref, wd_ref, ye_ref, wg_bf, wu_bf, wd_bf):
    e = pl.program_id(0)
    j = pl.program_id(1)

    @pl.when(j == 0)
    def _():
        wg_bf[...] = wg_ref[0].astype(BF16)
        wu_bf[...] = wu_ref[0].astype(BF16)
        wd_bf[...] = wd_ref[0].astype(BF16)

    @pl.when(j * EXPERT_BLOCK < tot_sm[e] + SLOT_ROWS)
    def _():
        x = xs_ref[0]
        g = jnp.dot(x, wg_bf[...], preferred_element_type=F32)
        u = jnp.dot(x, wu_bf[...], preferred_element_type=F32)
        hid = (_silu(g) * u).astype(BF16)
        ye_ref[0] = jnp.dot(hid, wd_bf[...], preferred_element_type=F32).astype(ye_ref.dtype)


def _experts(xs, tot, w_gate, w_up, w_down):
    ne, rl, d = xs.shape
    f = w_gate.shape[2]
    last = lambda e, tot: (tot[e] + SLOT_ROWS - 1) // EXPERT_BLOCK
    blk = lambda e, j, tot: (e, jnp.minimum(j, last(e, tot)), 0)
    return pl.pallas_call(
        _expert_kernel,
        grid_spec=pltpu.PrefetchScalarGridSpec(
            num_scalar_prefetch=1,
            grid=(ne, rl // EXPERT_BLOCK),
            in_specs=[pl.BlockSpec((1, EXPERT_BLOCK, d), blk),
                      pl.BlockSpec((1, d, f), lambda e, j, tot: (e, 0, 0)),
                      pl.BlockSpec((1, d, f), lambda e, j, tot: (e, 0, 0)),
                      pl.BlockSpec((1, f, d), lambda e, j, tot: (e, 0, 0))],
            out_specs=pl.BlockSpec((1, EXPERT_BLOCK, d), blk),
            scratch_shapes=[pltpu.VMEM((d, f), BF16), pltpu.VMEM((d, f), BF16), pltpu.VMEM((f, d), BF16)]),
        out_shape=jax.ShapeDtypeStruct((ne, rl, d), BF16),
        compiler_params=_cparams(("arbitrary", "arbitrary")),
        name="expert_ffn",
    )(tot, xs, w_gate, w_up, w_down)


def _combine_kernel(start_sm, cnt_sm, x1_ref, rank_ref, aff_ref, g2_ref, nf_ref, ye_hbm, o_ref, buf, sem, *,
                    final_norm):
    i = pl.program_id(0)
    nt = pl.num_programs(0)
    ne = N_EXPERTS
    slot = i % 2
    rank = rank_ref[0]
    aff = aff_ref[0]
    sub = lax.broadcasted_iota(jnp.int32, (SLOT_ROWS, rank.shape[1]), 0)
    dn_t = (((0,), (0,)), ((), ()))

    def copy(t, s, e, r):
        off = pl.multiple_of(start_sm[t, e] + r * SLOT_ROWS, ROW_ALIGN)
        return pltpu.make_async_copy(ye_hbm.at[e, pl.ds(off, SLOT_ROWS)],
                                     buf.at[s, pl.ds(e * SLOT_ROWS, SLOT_ROWS)], sem.at[s, e])

    def weighted(r):
        w = jnp.concatenate(
            [jnp.where(rank[e:e + 1, :] == sub + r * SLOT_ROWS, aff[e:e + 1, :], 0.0).astype(BF16)
             for e in range(ne)], axis=0)
        return lax.dot_general(w, buf[slot], dn_t, preferred_element_type=F32)

    @pl.when(i == 0)
    def _():
        for e in range(ne):
            copy(0, 0, e, 0).start()

    @pl.when(i + 1 < nt)
    def _prefetch_next_tile():
        for e in range(ne):
            copy(i + 1, 1 - slot, e, 0).start()

    for e in range(ne):
        copy(i, slot, e, 0).wait()
    y0 = weighted(0)

    def extra_round(r, acc):
        for e in range(ne):
            @pl.when(cnt_sm[i, e] > r * SLOT_ROWS)
            def _():
                copy(i, slot, e, r).start()
        for e in range(ne):
            @pl.when(cnt_sm[i, e] > r * SLOT_ROWS)
            def _():
                copy(i, slot, e, r).wait()
        return acc + weighted(r)

    y = lax.fori_loop(1, _num_rounds(cnt_sm, i), extra_round, y0)
    x2 = x1_ref[...] + g2_ref[0] * y
    if final_norm:
        ms = jnp.mean(x2 * x2, axis=-1, keepdims=True)
        x2 = x2 * lax.rsqrt(ms + EPS) * nf_ref[...]
    o_ref[...] = x2


def _combine(x1, rank, aff, start, cnt, g2, norm_f_g, ye, cond_of_tile):
    t, d = x1.shape
    nt, ne, tm = rank.shape
    final_norm = norm_f_g is not None
    if not final_norm:
        norm_f_g = jnp.ones((d,), F32)
    return pl.pallas_call(
        functools.partial(_combine_kernel, final_norm=final_norm),
        grid_spec=pltpu.PrefetchScalarGridSpec(
            num_scalar_prefetch=2,
            grid=(nt,),
            in_specs=[pl.BlockSpec((tm, d), lambda i, *_: (i, 0)),
                      pl.BlockSpec((1, ne, tm), lambda i, *_: (i, 0, 0)),
                      pl.BlockSpec((1, ne, tm), lambda i, *_: (i, 0, 0)),
                      pl.BlockSpec((1, 1, d), lambda i, *_: (cond_of_tile(i), 0, 0)),
                      pl.BlockSpec((1, d), lambda i, *_: (0, 0)),
                      pl.BlockSpec(memory_space=pl.ANY)],
            out_specs=pl.BlockSpec((tm, d), lambda i, *_: (i, 0)),
            scratch_shapes=[pltpu.VMEM((2, ne * SLOT_ROWS, d), BF16),
                            pltpu.SemaphoreType.DMA((2, ne))]),
        out_shape=jax.ShapeDtypeStruct((t, d), F32),
        compiler_params=_cparams(("arbitrary",)),
        name="expert_combine",
    )(start, cnt, x1, rank, aff, g2, norm_f_g.reshape(1, d), ye)


def _trunk_layer(x, batch, seq_len, mods, cond_of_tile, p, s0_f, s0_b, rope, emit_state, final_g):
    t, d = x.shape
    sh1, sc1, g1, sh2, sc2, g2 = mods
    z, qkvg, gates = _in_projection(x, p["norm1_g"], sc1, sh1, p["w_in"], cond_of_tile,
                                    seq_len if rope else None)
    y_hy = _hyena_branch(z.reshape(batch, seq_len, -1), p).reshape(t, -1)
    ret = _retention(qkvg.reshape(batch, seq_len, -1), p["ret_decay_fwd"], p["ret_decay_bwd"], p["ret_gn_g"],
                     s0_f, s0_b, emit_state)
    y_ret = ret[0].reshape(t, -1)
    x1, h2, aff = _out_projection(y_hy, y_ret, gates, x, g1, sc2, sh2, p["norm2_g"], p["w_hy_o"], p["w_ret_o"],
                                  p["w_out"], p["w_router"], cond_of_tile)
    cap = (EC_CAPACITY_FACTOR * t) // N_EXPERTS
    rank, cnt, start, tot = _select(aff, cap)
    xs = _gather(h2, rank, start, cnt, tot, cap)
    ye = _experts(xs, tot, p["w_e_gate"], p["w_e_up"], p["w_e_down"])
    out = _combine(x1, rank, aff, start, cnt, g2, final_g, ye, cond_of_tile)
    return out, ret[1:]


def kernel(x_prompt, x_sample, state_ret_fwd, state_ret_bwd, c, c_ctx, w_mod, b_mod, norm1_g, w_in, hy_conv_w, hy_conv_b, hy_f_w1, hy_f_b1, hy_f_freq, hy_f_w2, hy_f_b2, hy_f_w3, hy_decay, hy_bias, ret_decay_fwd, ret_decay_bwd, ret_gn_g, w_hy_o, w_ret_o, w_out, norm2_g, w_router, w_e_gate, w_e_up, w_e_down, norm_f_g):
    b, s, d = x_prompt.shape
    bd, sd, _ = x_sample.shape
    depth = w_mod.shape[0]
    assert (b * s) % TOKEN_TILE == 0 and sd % TOKEN_TILE == 0 and b % 2 == 0 and bd % 2 == 0
    ncond = -(-(bd + 1) // 8) * 8
    conds = jnp.concatenate([c, c_ctx[None], jnp.zeros((ncond - bd - 1, d), F32)], axis=0)
    ctx_cond = lambda i: bd
    lat_cond = lambda i: i // (sd // TOKEN_TILE)
    xp = x_prompt.reshape(b * s, d)
    xs = x_sample.reshape(bd * sd, d)
    new_f, new_b = [], []
    for l in range(depth):
        mod = _modulation(conds, w_mod[l], b_mod[l])
        mods = [mod[:, k * d:(k + 1) * d].reshape(ncond, 1, d) for k in range(6)]
        p = {"norm1_g": norm1_g[l], "w_in": w_in[l].astype(BF16), "hy_conv_w": hy_conv_w[l],
             "hy_conv_b": hy_conv_b[l], "hy_f_w1": hy_f_w1[l], "hy_f_b1": hy_f_b1[l], "hy_f_freq": hy_f_freq[l],
             "hy_f_w2": hy_f_w2[l], "hy_f_b2": hy_f_b2[l], "hy_f_w3": hy_f_w3[l], "hy_decay": hy_decay[l],
             "hy_bias": hy_bias[l], "ret_decay_fwd": ret_decay_fwd[l], "ret_decay_bwd": ret_decay_bwd[l],
             "ret_gn_g": ret_gn_g[l], "w_hy_o": w_hy_o[l].astype(BF16), "w_ret_o": w_ret_o[l].astype(BF16),
             "w_out": w_out[l].astype(BF16), "norm2_g": norm2_g[l], "w_router": w_router[l],
             "w_e_gate": w_e_gate[l], "w_e_up": w_e_up[l], "w_e_down": w_e_down[l]}
        final_g = norm_f_g if l == depth - 1 else None
        xp, (s_f, s_b) = _trunk_layer(xp, b, s, mods, ctx_cond, p, None, None, False, True, final_g)
        new_f.append(s_f)
        new_b.append(s_b)
        xs, _ = _trunk_layer(xs, bd, sd, mods, lat_cond, p, state_ret_fwd[:, l], state_ret_bwd[:, l],
                             True, False, final_g)
    y_prompt = xp.reshape(b, s, d)
    y_sample = xs.reshape(bd, sd, d)
    return (y_prompt, y_sample, jnp.stack(new_f, axis=1), jnp.stack(new_b, axis=1))
```

```python
import functools
import math

import numpy as np
import jax
import jax.numpy as jnp
from jax import lax
from jax.experimental import pallas as pl
from jax.experimental.pallas import tpu as pltpu

F32 = jnp.float32
BF16 = jnp.bfloat16
HIGHEST = lax.Precision.HIGHEST

EPS = 1e-6
D_HYENA = 512
D_RET = 512
N_RET_HEADS = 4
RET_HEAD_DIM = 128
RET_CHUNK = 128
GRID_W = 64
FILTER_EMB = 33
ROPE_BASE = 10000.0
N_EXPERTS = 16
EC_CAPACITY_FACTOR = 2

TOKEN_TILE = 512
SLOT_ROWS = 96
ROW_ALIGN = 16
EXPERT_BLOCK = 256
FFT_MINOR = 64
FFT_GROUP = 4
VMEM_LIMIT = 56 * 1024 * 1024


def _cparams(sem):
    return pltpu.CompilerParams(dimension_semantics=sem, vmem_limit_bytes=VMEM_LIMIT)


def _silu(x):
    return x * jax.nn.sigmoid(x)


def _mod_kernel(c_ref, w_ref, b_ref, o_ref):
    s = _silu(c_ref[...])
    o_ref[...] = jnp.dot(s, w_ref[...], preferred_element_type=F32, precision=HIGHEST) + b_ref[...]


def _modulation(conds, w_mod, b_mod):
    nc, d = conds.shape
    return pl.pallas_call(
        _mod_kernel,
        grid=(6,),
        in_specs=[pl.BlockSpec((nc, d), lambda j: (0, 0)),
                  pl.BlockSpec((d, d), lambda j: (0, j)),
                  pl.BlockSpec((1, d), lambda j: (0, j))],
        out_specs=pl.BlockSpec((nc, d), lambda j: (0, j)),
        out_shape=jax.ShapeDtypeStruct((nc, 6 * d), F32),
        compiler_params=_cparams(("arbitrary",)),
        name="modulation",
    )(conds, w_mod, b_mod.reshape(1, -1))


def _inproj_kernel(x_ref, g_ref, sc_ref, sh_ref, w_ref, *rest, splits, rope):
    if rope:
        cos_ref, sin_ref, z_ref, qkvg_ref, gates_ref = rest
    else:
        z_ref, qkvg_ref, gates_ref = rest
    x = x_ref[...]
    ms = jnp.mean(x * x, axis=-1, keepdims=True)
    h = x * lax.rsqrt(ms + EPS) * g_ref[...]
    h = h * (1.0 + sc_ref[0]) + sh_ref[0]
    hb = h.astype(BF16)
    nz, nq, ng = splits
    cw = 512
    dh = RET_HEAD_DIM
    for c0 in range(0, nz + nq + ng, cw):
        acc = jnp.dot(hb, w_ref[:, c0:c0 + cw], preferred_element_type=F32)
        if c0 < nz:
            z_ref[:, c0:c0 + cw] = acc.astype(BF16)
        elif c0 < nz + nq:
            part = (c0 - nz) // D_RET
            if part == 1:
                acc = acc * (dh ** -0.5)
            if rope and part < 2:
                lane = lax.broadcasted_iota(jnp.int32, (acc.shape[0], dh), 1)
                swap_hi = (lane % (dh // 2)) < (dh // 4)
                cs, sn = cos_ref[...], sin_ref[...]
                heads = []
                for hh in range(cw // dh):
                    xh = acc[:, hh * dh:(hh + 1) * dh]
                    rot = jnp.where(swap_hi, pltpu.roll(xh, dh - dh // 4, axis=1), pltpu.roll(xh, dh // 4, axis=1))
                    heads.append(xh * cs + rot * sn)
                acc = jnp.concatenate(heads, axis=1)
            qkvg_ref[:, c0 - nz:c0 - nz + cw] = acc.astype(BF16)
        else:
            gates_ref[:, c0 - nz - nq:c0 - nz - nq + cw] = jax.nn.sigmoid(acc).astype(BF16)


def _in_projection(x, norm_g, sc, sh, w_in_bf, cond_of_tile, rope_len):
    t, d = x.shape
    splits = (3 * D_HYENA, 4 * D_RET, 2 * d)
    assert D_RET == 512
    tm = TOKEN_TILE
    row = lambda i: (i, 0)
    cond = lambda i: (cond_of_tile(i), 0, 0)
    rope = rope_len is not None
    rope_args, rope_specs = [], []
    if rope:
        tiles_per_seq = rope_len // tm
        rope_args = list(_rope_tables(rope_len))
        rope_specs = [pl.BlockSpec((tm, RET_HEAD_DIM), lambda i: (i % tiles_per_seq, 0))] * 2
    return pl.pallas_call(
        functools.partial(_inproj_kernel, splits=splits, rope=rope),
        grid=(t // tm,),
        in_specs=[pl.BlockSpec((tm, d), row),
                  pl.BlockSpec((1, d), lambda i: (0, 0)),
                  pl.BlockSpec((1, 1, d), cond),
                  pl.BlockSpec((1, 1, d), cond),
                  pl.BlockSpec(w_in_bf.shape, lambda i: (0, 0))] + rope_specs,
        out_specs=[pl.BlockSpec((tm, splits[0]), row),
                   pl.BlockSpec((tm, splits[1]), row),
                   pl.BlockSpec((tm, splits[2]), row)],
        out_shape=[jax.ShapeDtypeStruct((t, splits[0]), BF16),
                   jax.ShapeDtypeStruct((t, splits[1]), BF16),
                   jax.ShapeDtypeStruct((t, splits[2]), BF16)],
        compiler_params=_cparams(("parallel",)),
        name="in_projection",
    )(x, norm_g.reshape(1, d), sc, sh, w_in_bf, *rope_args)


def _filter_features(seq_len):
    t = np.linspace(0.0, 1.0, seq_len, dtype=np.float32)[:, None]
    bands = (FILTER_EMB - 1) // 2
    w = (np.float32(2.0 * math.pi) * np.arange(seq_len, dtype=np.float32)) / np.float32(seq_len)
    f = np.linspace(1e-4, bands - 1, bands, dtype=np.float32)
    ang = (w[:, None] * f[None, :]).astype(np.float64)
    z = np.concatenate([t, np.cos(ang), -np.sin(ang)], axis=-1).astype(np.float32)
    return np.pad(z, ((0, 0), (0, 128 - FILTER_EMB)))


def _filter_kernel(z_ref, w1_ref, b1_ref, fr_ref, w2_ref, b2_ref, w3_ref, dec_ref, o_ref):
    z = z_ref[...]
    fr = fr_ref[...]
    dot = functools.partial(jnp.dot, preferred_element_type=F32, precision=HIGHEST)
    h = jnp.sin(fr * (dot(z, w1_ref[...]) + b1_ref[...]))
    h = jnp.sin(fr * (dot(h, w2_ref[...]) + b2_ref[...]))
    h = dot(h, w3_ref[...]) * jnp.exp(-z[:, 0:1] * jnp.abs(dec_ref[...]))
    rows = h.shape[0]
    grow = pl.program_id(0) * rows + lax.broadcasted_iota(jnp.int32, (rows, 1), 0)
    o_ref[0] = h[:, :D_HYENA].astype(o_ref.dtype)
    o_ref[1] = jnp.zeros((rows, D_HYENA), o_ref.dtype)
    o_ref[2] = jnp.where(grow == 0, 0.0, h[:, D_HYENA:]).astype(o_ref.dtype)
    o_ref[3] = jnp.zeros((rows, D_HYENA), o_ref.dtype)


def _hyena_filters(seq_len, p, out_dtype):
    z = jnp.asarray(_filter_features(seq_len))
    w1 = jnp.pad(p["hy_f_w1"], ((0, 128 - FILTER_EMB), (0, 0)))
    rows = min(seq_len, 512)
    full = lambda a: pl.BlockSpec(a.shape, lambda i: (0,) * a.ndim)
    ops = [w1, p["hy_f_b1"].reshape(1, -1), p["hy_f_freq"].reshape(1, -1), p["hy_f_w2"],
           p["hy_f_b2"].reshape(1, -1), p["hy_f_w3"], p["hy_decay"].reshape(1, -1)]
    return pl.pallas_call(
        _filter_kernel,
        grid=(seq_len // rows,),
        in_specs=[pl.BlockSpec((rows, 128), lambda i: (i, 0))] + [full(a) for a in ops],
        out_specs=pl.BlockSpec((4, rows, D_HYENA), lambda i: (0, i, 0)),
        out_shape=jax.ShapeDtypeStruct((4, seq_len, D_HYENA), out_dtype),
        compiler_params=_cparams(("arbitrary",)),
        name="hyena_filter",
    )(z, *ops)


def _conv3(x_ref, b, w_ref, b_ref):
    x = x_ref[b].astype(F32)
    seq_len = x.shape[0]
    row = lax.broadcasted_iota(jnp.int32, (seq_len, 1), 0)
    prev = jnp.where(row == 0, 0.0, pltpu.roll(x, 1, axis=0))
    nxt = jnp.where(row == seq_len - 1, 0.0, pltpu.roll(x, seq_len - 1, axis=0))
    return b_ref[...] + prev * w_ref[0:1] + x * w_ref[1:2] + nxt * w_ref[2:3]


def _shortconv_kernel(v_ref, x1_ref, x2_ref, wv_ref, w1_ref, w2_ref, bv_ref, b1_ref, b2_ref, u_ref, x2c_ref):
    for b in range(v_ref.shape[0]):
        u = _conv3(v_ref, b, wv_ref, bv_ref) * _conv3(x1_ref, b, w1_ref, b1_ref)
        u_ref[b] = u.astype(u_ref.dtype)
        x2c_ref[b] = _conv3(x2_ref, b, w2_ref, b2_ref).astype(x2c_ref.dtype)


def _short_conv(z, conv_w, conv_b):
    b, seq_len, _ = z.shape
    cb = 128
    nb = D_HYENA // cb
    bb = 4 if b % 4 == 0 else 2
    zspec = lambda off: pl.BlockSpec((bb, seq_len, cb), lambda i, j: (i, 0, off + j))
    wspec = lambda off: pl.BlockSpec((3, cb), lambda i, j: (0, off + j))
    bspec = lambda off: pl.BlockSpec((1, cb), lambda i, j: (0, off + j))
    ospec = pl.BlockSpec((bb, seq_len, cb), lambda i, j: (i, 0, j))
    cbias = conv_b.reshape(1, -1)
    return pl.pallas_call(
        _shortconv_kernel,
        grid=(b // bb, nb),
        in_specs=[zspec(0), zspec(nb), zspec(2 * nb), wspec(0), wspec(nb), wspec(2 * nb),
                  bspec(0), bspec(nb), bspec(2 * nb)],
        out_specs=[ospec, ospec],
        out_shape=[jax.ShapeDtypeStruct((b, seq_len, D_HYENA), BF16)] * 2,
        compiler_params=_cparams(("parallel", "parallel")),
        name="hyena_short_conv",
    )(z, z, z, conv_w, conv_w, conv_w, cbias, cbias, cbias)


def _stacked_dft(n_out, n_in, modulus, scale=1.0):
    k = lax.broadcasted_iota(jnp.int32, (n_out, n_in), 0)
    n = lax.broadcasted_iota(jnp.int32, (n_out, n_in), 1)
    th = ((k * n) % modulus).astype(F32) * (2.0 * math.pi / modulus)
    c, s = jnp.cos(th), jnp.sin(th)
    fwd = jnp.block([[c, s], [-s, c]])
    inv = jnp.block([[c.T, -s.T], [s.T, c.T]]) * scale
    return fwd, inv


def _second_level_tables(n_total, n1, n2):
    shape = (n1, n2, n2)
    k1 = lax.broadcasted_iota(jnp.int32, shape, 0)
    k2 = lax.broadcasted_iota(jnp.int32, shape, 1)
    m = lax.broadcasted_iota(jnp.int32, shape, 2)
    th = ((m * (k1 + n1 * k2)) % n_total).astype(F32) * (2.0 * math.pi / n_total)
    c, s = jnp.cos(th), jnp.sin(th)
    g = jnp.concatenate([jnp.concatenate([c, s], axis=2), jnp.concatenate([-s, c], axis=2)], axis=1)
    return g, jnp.swapaxes(g, 1, 2)


def _fft1_core(u_sc, f_ref, a_ref, t_sc):
    n2 = FFT_MINOR
    _, seq_len, cb = u_sc.shape
    h1 = seq_len // n2
    nk = t_sc.shape[0] // n2

    def fwd(gi, carry):
        m0 = gi * FFT_GROUP
        cols = []
        for d in range(FFT_GROUP):
            xr = u_sc[0, pl.ds(m0 + d, h1, stride=n2), :]
            xi = u_sc[1, pl.ds(m0 + d, h1, stride=n2), :]
            cols.append(jnp.concatenate([xr, xi], axis=0))
        x = jnp.concatenate(cols, axis=1).astype(BF16)
        res = jnp.dot(f_ref[...], x, preferred_element_type=F32)
        for d in range(FFT_GROUP):
            t_sc[pl.ds(pl.multiple_of((m0 + d) * nk, nk), nk), :] = res[:, d * cb:(d + 1) * cb]
        return carry

    lax.fori_loop(0, n2 // FFT_GROUP, fwd, 0, unroll=2)

    def transpose(k, carry):
        a_ref[0, k] = t_sc[pl.ds(k, n2, stride=nk), :].astype(a_ref.dtype)
        return carry

    lax.fori_loop(0, nk, transpose, 0, unroll=8)


def _fft1_kernel(h_ref, f_ref, a_ref, u_sc, t_sc):
    u_sc[...] = h_ref[...].astype(F32)
    _fft1_core(u_sc, f_ref, a_ref, t_sc)


def _conv_fft1_kernel(v_ref, x1_ref, x2_ref, wv_ref, w1_ref, w2_ref, bv_ref, b1_ref, b2_ref, f_ref,
                      u_ref, x2c_ref, a_ref, u_sc, t_sc):
    for b in range(2):
        u = _conv3(v_ref, b, wv_ref, bv_ref) * _conv3(x1_ref, b, w1_ref, b1_ref)
        u_sc[b] = u
        u_ref[b] = u.astype(u_ref.dtype)
        x2c_ref[b] = _conv3(x2_ref, b, w2_ref, b2_ref).astype(x2c_ref.dtype)
    _fft1_core(u_sc, f_ref, a_ref, t_sc)


def _fft1_scratch(seq_len, cb, n_rows):
    return [pltpu.VMEM((2, seq_len, cb), F32), pltpu.VMEM((FFT_MINOR * n_rows, cb), F32)]


def _fft_first_level(h, f1):
    b, seq_len, c = h.shape
    cb = 128
    return pl.pallas_call(
        _fft1_kernel,
        grid=(b // 2, c // cb),
        in_specs=[pl.BlockSpec((2, seq_len, cb), lambda i, j: (i, 0, j)),
                  pl.BlockSpec(f1.shape, lambda i, j: (0, 0))],
        out_specs=pl.BlockSpec((1, f1.shape[0], FFT_MINOR, cb), lambda i, j: (i, 0, 0, j)),
        out_shape=jax.ShapeDtypeStruct((b // 2, f1.shape[0], FFT_MINOR, c), BF16),
        scratch_shapes=_fft1_scratch(seq_len, cb, f1.shape[0]),
        compiler_params=_cparams(("parallel", "parallel")),
        name="hyena_dft_level1",
    )(h, f1)


def _conv_fft_first_level(z, conv_w, conv_b, f1):
    b, seq_len, _ = z.shape
    cb = 128
    nb = D_HYENA // cb
    zspec = lambda off: pl.BlockSpec((2, seq_len, cb), lambda i, j: (i, 0, off + j))
    wspec = lambda off: pl.BlockSpec((3, cb), lambda i, j: (0, off + j))
    bspec = lambda off: pl.BlockSpec((1, cb), lambda i, j: (0, off + j))
    ospec = pl.BlockSpec((2, seq_len, cb), lambda i, j: (i, 0, j))
    cbias = conv_b.reshape(1, -1)
    return pl.pallas_call(
        _conv_fft1_kernel,
        grid=(b // 2, nb),
        in_specs=[zspec(0), zspec(nb), zspec(2 * nb), wspec(0), wspec(nb), wspec(2 * nb),
                  bspec(0), bspec(nb), bspec(2 * nb), pl.BlockSpec(f1.shape, lambda i, j: (0, 0))],
        out_specs=[ospec, ospec,
                   pl.BlockSpec((1, f1.shape[0], FFT_MINOR, cb), lambda i, j: (i, 0, 0, j))],
        out_shape=[jax.ShapeDtypeStruct((b, seq_len, D_HYENA), BF16)] * 2
        + [jax.ShapeDtypeStruct((b // 2, f1.shape[0], FFT_MINOR, D_HYENA), BF16)],
        scratch_shapes=_fft1_scratch(seq_len, cb, f1.shape[0]),
        compiler_params=_cparams(("parallel", "parallel")),
        name="hyena_conv_dft_level1",
    )(z, z, z, conv_w, conv_w, conv_w, cbias, cbias, cbias, f1)


def _fft_s2f_kernel(a_ref, g_ref, kr_ref, ki_ref):
    _, _, kb, n2, c = a_ref.shape
    for kk in range(kb):
        g = g_ref[kk]
        hf = jnp.dot(g, a_ref[0, :, kk].reshape(2 * n2, c), preferred_element_type=F32)
        hb = jnp.dot(g, a_ref[1, :, kk].reshape(2 * n2, c), preferred_element_type=F32)
        kr_ref[kk] = hf[:n2] + hb[:n2]
        ki_ref[kk] = hf[n2:] - hb[n2:]


def _fft_filter_second_level(a, g, kb=8):
    _, _, n1, n2, c = a.shape
    spec = pl.BlockSpec((kb, n2, c), lambda i: (i, 0, 0))
    return pl.pallas_call(
        _fft_s2f_kernel,
        grid=(n1 // kb,),
        in_specs=[pl.BlockSpec((2, 2, kb, n2, c), lambda i: (0, 0, i, 0, 0)),
                  pl.BlockSpec((kb, 2 * n2, 2 * n2), lambda i: (i, 0, 0))],
        out_specs=[spec, spec],
        out_shape=[jax.ShapeDtypeStruct((n1, n2, c), F32)] * 2,
        compiler_params=_cparams(("parallel",)),
        name="hyena_filter_spectrum",
    )(a, g)


def _fft_s2_kernel(a_ref, g_ref, gt_ref, kr_ref, ki_ref, b_ref, r_sc):
    npairs, _, kb, n2, cb = a_ref.shape
    for p in range(npairs):
        for kk in range(kb):
            x = jnp.dot(g_ref[kk], a_ref[p, :, kk].reshape(2 * n2, cb), preferred_element_type=F32)
            xr, xi = x[:n2], x[n2:]
            kr, ki = kr_ref[kk], ki_ref[kk]
            y = jnp.concatenate([xr * kr - xi * ki, xr * ki + xi * kr], axis=0).astype(BF16)
            res = jnp.dot(gt_ref[kk], y, preferred_element_type=F32)
            for q in range(cb // 128):
                r_sc[q, kk * 2 * n2:(kk + 1) * 2 * n2, :] = res[:, q * 128:(q + 1) * 128]

        def gather(row):
            return jnp.concatenate([r_sc[q, pl.ds(row, kb, stride=2 * n2), :] for q in range(cb // 128)], axis=1)

        def transpose(m, carry):
            b_ref[p, 0, m] = gather(m).astype(b_ref.dtype)
            b_ref[p, 1, m] = gather(n2 + m).astype(b_ref.dtype)
            return carry

        lax.fori_loop(0, n2, transpose, 0, unroll=8)


def _fft_second_level(a, g, gt, kr, ki):
    p, _, n1, n2, c = a.shape
    kb, cb = 16, 256
    gspec = pl.BlockSpec((kb, 2 * n2, 2 * n2), lambda i, j: (i, 0, 0))
    kspec = pl.BlockSpec((kb, n2, cb), lambda i, j: (i, 0, j))
    return pl.pallas_call(
        _fft_s2_kernel,
        grid=(n1 // kb, c // cb),
        in_specs=[pl.BlockSpec((p, 2, kb, n2, cb), lambda i, j: (0, 0, i, 0, j)), gspec, gspec, kspec, kspec],
        out_specs=pl.BlockSpec((p, 2, n2, kb, cb), lambda i, j: (0, 0, 0, i, j)),
        out_shape=jax.ShapeDtypeStruct((p, 2, n2, n1, c), BF16),
        scratch_shapes=[pltpu.VMEM((cb // 128, kb * 2 * n2, 128), F32)],
        compiler_params=_cparams(("parallel", "parallel")),
        name="hyena_dft_level2",
    )(a, g, gt, kr, ki)


def _fft_s3_kernel(b_ref, f_ref, u_ref, x2_ref, bias_ref, o_ref, t_sc):
    _, _, n2, n1, cb = b_ref.shape
    nr = t_sc.shape[0] // n2
    h1 = nr // 2

    def inv(gi, carry):
        m0 = gi * FFT_GROUP
        x = jnp.concatenate([b_ref[0, :, m0 + d].reshape(2 * n1, cb) for d in range(FFT_GROUP)], axis=1)
        res = jnp.dot(f_ref[...], x, preferred_element_type=F32)
        for d in range(FFT_GROUP):
            t_sc[pl.ds(pl.multiple_of((m0 + d) * nr, nr), nr), :] = res[:, d * cb:(d + 1) * cb]
        return carry

    lax.fori_loop(0, n2 // FFT_GROUP, inv, 0, unroll=2)

    bias = bias_ref[...]
    for b in range(2):
        def finish(q, carry):
            rows = pl.ds(pl.multiple_of(q * n2, n2), n2)
            conv = t_sc[pl.ds(b * h1 + q, n2, stride=nr), :]
            u = u_ref[b, rows, :].astype(F32)
            o_ref[b, rows, :] = ((conv + u * bias) * x2_ref[b, rows, :].astype(F32)).astype(o_ref.dtype)
            return carry

        lax.fori_loop(0, h1, finish, 0, unroll=4)


def _fft_last_level(bt, f1inv, u, x2c, bias):
    p, _, n2, n1, c = bt.shape
    b, seq_len, _ = u.shape
    cb = 128
    uspec = pl.BlockSpec((2, seq_len, cb), lambda i, j: (i, 0, j))
    return pl.pallas_call(
        _fft_s3_kernel,
        grid=(p, c // cb),
        in_specs=[pl.BlockSpec((1, 2, n2, n1, cb), lambda i, j: (i, 0, 0, 0, j)),
                  pl.BlockSpec(f1inv.shape, lambda i, j: (0, 0)),
                  uspec, uspec,
                  pl.BlockSpec((1, cb), lambda i, j: (0, j))],
        out_specs=uspec,
        out_shape=jax.ShapeDtypeStruct(u.shape, BF16),
        scratch_shapes=[pltpu.VMEM((n2 * f1inv.shape[0], cb), F32)],
        compiler_params=_cparams(("parallel", "parallel")),
        name="hyena_dft_inverse",
    )(bt, f1inv, u, x2c, bias)


def _kf_direct_kernel(h_ref, f_ref, kr_ref, ki_ref):
    n = kr_ref.shape[0]
    dot = functools.partial(jnp.dot, preferred_element_type=F32, precision=HIGHEST)
    hf = dot(f_ref[...], h_ref[0])
    hb = dot(f_ref[...], h_ref[2])
    kr_ref[...] = hf[:n] + hb[:n]
    ki_ref[...] = hf[n:] - hb[n:]


def _hyena_direct_kernel(u_ref, x2_ref, f_ref, fi_ref, kr_ref, ki_ref, bias_ref, o_ref):
    _, two, seq_len, cb = u_ref.shape
    n = kr_ref.shape[0]
    ub = u_ref[0].reshape(two * seq_len, cb)
    x = jnp.dot(f_ref[...], ub, preferred_element_type=F32)
    xr, xi = x[:n], x[n:]
    kr, ki = kr_ref[...], ki_ref[...]
    y = jnp.concatenate([xr * kr - xi * ki, xr * ki + xi * kr], axis=0).astype(BF16)
    conv = jnp.dot(fi_ref[...], y, preferred_element_type=F32)
    x2 = x2_ref[0].reshape(two * seq_len, cb).astype(F32)
    o = (conv + ub.astype(F32) * bias_ref[...]) * x2
    o_ref[0] = o.reshape(two, seq_len, cb).astype(o_ref.dtype)


def _hyena_branch(z, p):
    b, seq_len, _ = z.shape
    c = D_HYENA
    n = 2 * seq_len
    bias = p["hy_bias"].reshape(1, c)
    hh = _hyena_filters(seq_len, p, F32)
    if seq_len <= 512:
        u, x2c = _short_conv(z, p["hy_conv_w"], p["hy_conv_b"])
        fwd, inv = _stacked_dft(n, seq_len, n, scale=1.0 / n)
        kr, ki = pl.pallas_call(
            _kf_direct_kernel,
            out_shape=[jax.ShapeDtypeStruct((n, c), F32)] * 2,
            compiler_params=_cparams(None),
            name="hyena_filter_spectrum_direct",
        )(hh, fwd[:, :seq_len])
        cb = 256
        pair = lambda a: a.reshape(b // 2, 2, seq_len, c)
        uspec = pl.BlockSpec((1, 2, seq_len, cb), lambda i, j: (i, 0, 0, j))
        kspec = pl.BlockSpec((n, cb), lambda i, j: (0, j))
        out = pl.pallas_call(
            _hyena_direct_kernel,
            grid=(b // 2, c // cb),
            in_specs=[uspec, uspec,
                      pl.BlockSpec(fwd.shape, lambda i, j: (0, 0)),
                      pl.BlockSpec(inv.shape, lambda i, j: (0, 0)),
                      kspec, kspec,
                      pl.BlockSpec((1, cb), lambda i, j: (0, j))],
            out_specs=uspec,
            out_shape=jax.ShapeDtypeStruct((b // 2, 2, seq_len, c), BF16),
            compiler_params=_cparams(("parallel", "parallel")),
            name="hyena_dft_direct",
        )(pair(u), pair(x2c), fwd.astype(BF16), inv.astype(BF16), kr, ki, bias)
        return out.reshape(b, seq_len, c)

    n2 = FFT_MINOR
    n1 = n // n2
    h1 = seq_len // n2
    f1, f1inv = _stacked_dft(n1, h1, n1, scale=1.0 / n)
    f1, f1inv = f1.astype(BF16), f1inv.astype(BF16)
    g, gt = _second_level_tables(n, n1, n2)
    g, gt = g.astype(BF16), gt.astype(BF16)
    ha = _fft_first_level(hh, f1)
    kr, ki = _fft_filter_second_level(ha.reshape(2, 2, n1, n2, c), g)
    u, x2c, a = _conv_fft_first_level(z, p["hy_conv_w"], p["hy_conv_b"], f1)
    bt = _fft_second_level(a.reshape(b // 2, 2, n1, n2, c), g, gt, kr, ki)
    return _fft_last_level(bt, f1inv, u, x2c, bias)


def _rope_tables(seq_len):
    half = RET_HEAD_DIM // 2
    nf = half // 2
    t = jnp.arange(seq_len)
    row = (t // GRID_W).astype(F32)
    col = (t % GRID_W).astype(F32)
    inv = ROPE_BASE ** (-jnp.arange(nf, dtype=F32) / nf)
    ar = row[:, None] * inv[None, :]
    ac = col[:, None] * inv[None, :]
    cos = jnp.concatenate([jnp.cos(ar), jnp.cos(ar), jnp.cos(ac), jnp.cos(ac)], axis=-1)
    sin = jnp.concatenate([-jnp.sin(ar), jnp.sin(ar), -jnp.sin(ac), jnp.sin(ac)], axis=-1)
    return cos, sin


def _log_sigmoid(x):
    return jnp.minimum(x, 0.0) - jnp.log1p(jnp.exp(-jnp.abs(x)))


def _retention_kernel(*refs, has_init, emit_state, cpb):
    refs = list(refs)
    q_ref, k_ref, v_ref, g_ref, dec_ref, gn_ref = refs[:6]
    del refs[:6]
    if has_init:
        s0f_ref, s0b_ref = refs[:2]
        del refs[:2]
    o_ref = refs.pop(0)
    if emit_state:
        sf_out, sb_out = refs[:2]
        del refs[:2]
    sf_ref, sb_ref, sball_ref = refs

    c = RET_CHUNK
    dh = RET_HEAD_DIM
    nh = N_RET_HEADS
    phase = pl.program_id(1)
    j = pl.program_id(2)
    nb = pl.num_programs(2)

    ri = lax.broadcasted_iota(jnp.int32, (c, c), 0).astype(F32)
    ci = lax.broadcasted_iota(jnp.int32, (c, c), 1).astype(F32)
    diff = ri - ci

    def head_consts(h):
        lgf = _log_sigmoid(dec_ref[0, h])[0:1, :]
        lgb = _log_sigmoid(dec_ref[1, h])[0:1, :]
        return lgf, lgb

    def load(ref, r0, h):
        return ref[0, r0:r0 + c, h * dh:(h + 1) * dh]

    dn_t = (((0,), (0,)), ((), ()))
    dn_nt = (((1,), (1,)), ((), ()))

    @pl.when(phase == 0)
    def _backward_sweep():
        @pl.when(j == 0)
        def _():
            for h in range(nh):
                sb_ref[h] = s0b_ref[0, h] if has_init else jnp.zeros((dh, dh), F32)

        blk = nb - 1 - j
        for h in range(nh):
            _, lgb = head_consts(h)
            zeta_b = jnp.exp(lgb * ri)
            cdec_b = jnp.exp(lgb * float(c))
            for cc in reversed(range(cpb)):
                r0 = cc * c
                n = blk * cpb + cc
                s = sb_ref[h]
                sball_ref[n, h] = s.astype(BF16)
                kz = (load(k_ref, r0, h).astype(F32) * zeta_b).astype(BF16)
                vv = load(v_ref, r0, h)
                sb_ref[h] = cdec_b * s + lax.dot_general(kz, vv, dn_t, preferred_element_type=F32)

        if emit_state:
            @pl.when(j == nb - 1)
            def _():
                for h in range(nh):
                    sb_out[0, h] = sb_ref[h]

    @pl.when(phase == 1)
    def _forward_sweep():
        @pl.when(j == 0)
        def _():
            for h in range(nh):
                sf_ref[h] = s0f_ref[0, h] if has_init else jnp.zeros((dh, dh), F32)

        for h in range(nh):
            lgf, lgb = head_consts(h)
            mask = (jnp.where(diff >= 0, jnp.exp(lgf * jnp.maximum(diff, 0.0)), 0.0)
                    + jnp.where(diff <= 0, jnp.exp(lgb * jnp.maximum(-diff, 0.0)), 0.0))
            xi_f = jnp.exp(lgf * (ri + 1.0))
            xi_b = jnp.exp(lgb * (float(c) - ri))
            zeta_f = jnp.exp(lgf * (float(c - 1) - ri))
            cdec_f = jnp.exp(lgf * float(c))
            gn = gn_ref[:, h * dh:(h + 1) * dh]
            for cc in range(cpb):
                r0 = cc * c
                n = j * cpb + cc
                qb = load(q_ref, r0, h)
                kb = load(k_ref, r0, h)
                vv = load(v_ref, r0, h)
                gate = load(g_ref, r0, h).astype(F32)
                sc = lax.dot_general(qb, kb, dn_nt, preferred_element_type=F32)
                inner = jnp.dot((sc * mask).astype(BF16), vv, preferred_element_type=F32)
                s = sf_ref[h]
                q = qb.astype(F32)
                lhs = jnp.concatenate([q * xi_f, q * xi_b], axis=1).astype(BF16)
                rhs = jnp.concatenate([s.astype(BF16), sball_ref[n, h]], axis=0)
                o = inner + jnp.dot(lhs, rhs, preferred_element_type=F32)
                mu = jnp.mean(o, axis=-1, keepdims=True)
                d = o - mu
                var = jnp.mean(d * d, axis=-1, keepdims=True)
                y = d * lax.rsqrt(var + EPS) * gn * _silu(gate)
                o_ref[0, r0:r0 + c, h * dh:(h + 1) * dh] = y.astype(o_ref.dtype)
                kz = (kb.astype(F32) * zeta_f).astype(BF16)
                sf_ref[h] = cdec_f * s + lax.dot_general(kz, vv, dn_t, preferred_element_type=F32)

        if emit_state:
            @pl.when(j == nb - 1)
            def _():
                for h in range(nh):
                    sf_out[0, h] = sf_ref[h]


def _retention(qkvg, dec_f, dec_b, gn_g, s0_f, s0_b, emit_state):
    b, seq_len, _ = qkvg.shape
    nh, dh, c = N_RET_HEADS, RET_HEAD_DIM, RET_CHUNK
    rb = min(seq_len, 512)
    nb = seq_len // rb
    cpb = rb // c
    has_init = s0_f is not None
    dec = jnp.broadcast_to(jnp.stack([dec_f, dec_b])[:, :, None, None], (2, nh, 8, 128)).astype(F32)
    kv_blk = lambda i, p, j: jnp.where(p == 0, nb - 1 - j, j)
    q_blk = lambda i, p, j: jnp.where(p == 0, 0, j)
    in_specs = [pl.BlockSpec((1, rb, D_RET), lambda i, p, j: (i, q_blk(i, p, j), 0)),
                pl.BlockSpec((1, rb, D_RET), lambda i, p, j: (i, kv_blk(i, p, j), 1)),
                pl.BlockSpec((1, rb, D_RET), lambda i, p, j: (i, kv_blk(i, p, j), 2)),
                pl.BlockSpec((1, rb, D_RET), lambda i, p, j: (i, q_blk(i, p, j), 3)),
                pl.BlockSpec((2, nh, 8, 128), lambda i, p, j: (0, 0, 0, 0)),
                pl.BlockSpec((1, D_RET), lambda i, p, j: (0, 0))]
    args = [qkvg, qkvg, qkvg, qkvg, dec, gn_g.reshape(1, -1)]
    sspec = pl.BlockSpec((1, nh, dh, dh), lambda i, p, j: (i, 0, 0, 0))
    if has_init:
        in_specs += [sspec, sspec]
        args += [s0_f, s0_b]
    out_specs = [pl.BlockSpec((1, rb, D_RET), lambda i, p, j: (i, q_blk(i, p, j), 0))]
    out_shape = [jax.ShapeDtypeStruct((b, seq_len, D_RET), BF16)]
    if emit_state:
        out_specs += [sspec, sspec]
        out_shape += [jax.ShapeDtypeStruct((b, nh, dh, dh), F32)] * 2
    return pl.pallas_call(
        functools.partial(_retention_kernel, has_init=has_init, emit_state=emit_state, cpb=cpb),
        grid=(b, 2, nb),
        in_specs=in_specs,
        out_specs=out_specs,
        out_shape=out_shape,
        scratch_shapes=[pltpu.VMEM((nh, dh, dh), F32), pltpu.VMEM((nh, dh, dh), F32),
                        pltpu.VMEM((nb * cpb, nh, dh, dh), BF16)],
        compiler_params=_cparams(("parallel", "arbitrary", "arbitrary")),
        name="retention",
    )(*args)


def _outproj_kernel(yhy_ref, yret_ref, gates_ref, x_ref, g1_ref, sc_ref, sh_ref, ng_ref,
                    why_ref, wret_ref, wout_ref, wr_ref, x1_ref, h2_ref, aff_ref):
    d = x_ref.shape[1]
    a = jnp.dot(yhy_ref[...], why_ref[...], preferred_element_type=F32)
    b = jnp.dot(yret_ref[...], wret_ref[...], preferred_element_type=F32)
    merged = gates_ref[:, :d].astype(F32) * a + gates_ref[:, d:].astype(F32) * b
    out = jnp.dot(merged.astype(BF16), wout_ref[...], preferred_element_type=F32)
    x1 = x_ref[...] + g1_ref[0] * out
    x1_ref[...] = x1
    ms = jnp.mean(x1 * x1, axis=-1, keepdims=True)
    h = x1 * lax.rsqrt(ms + EPS) * ng_ref[...]
    h = h * (1.0 + sc_ref[0]) + sh_ref[0]
    h_hi = h.astype(BF16)
    h2_ref[...] = h_hi
    h_lo = (h - h_hi.astype(F32)).astype(BF16)
    t = jnp.dot(h_hi, wr_ref[...], preferred_element_type=F32)
    logits = t[:, :128] + t[:, 128:] + jnp.dot(h_lo, wr_ref[:, :128], preferred_element_type=F32)
    lane = lax.broadcasted_iota(jnp.int32, logits.shape, 1)
    logits = jnp.where(lane < N_EXPERTS, logits, -jnp.inf)
    e = jnp.exp(logits - jnp.max(logits, axis=-1, keepdims=True))
    aff = e / jnp.sum(e, axis=-1, keepdims=True)
    aff_ref[0] = aff.T[:N_EXPERTS, :]


def _out_projection(y_hy, y_ret, gates, x, g1, sc2, sh2, norm2_g, w_hy_o, w_ret_o, w_out, w_router, cond_of_tile):
    t, d = x.shape
    tm = TOKEN_TILE
    row = lambda i: (i, 0)
    cond = lambda i: (cond_of_tile(i), 0, 0)
    full = lambda a: pl.BlockSpec(a.shape, lambda i: (0, 0))
    wr = jnp.pad(w_router, ((0, 0), (0, 128 - N_EXPERTS)))
    wr_hi = wr.astype(BF16)
    wr = jnp.concatenate([wr_hi, (wr - wr_hi.astype(F32)).astype(BF16)], axis=1)
    return pl.pallas_call(
        _outproj_kernel,
        grid=(t // tm,),
        in_specs=[pl.BlockSpec((tm, D_HYENA), row), pl.BlockSpec((tm, D_RET), row),
                  pl.BlockSpec((tm, 2 * d), row), pl.BlockSpec((tm, d), row),
                  pl.BlockSpec((1, 1, d), cond), pl.BlockSpec((1, 1, d), cond), pl.BlockSpec((1, 1, d), cond),
                  pl.BlockSpec((1, d), lambda i: (0, 0)),
                  full(w_hy_o), full(w_ret_o), full(w_out), full(wr)],
        out_specs=[pl.BlockSpec((tm, d), row), pl.BlockSpec((tm, d), row),
                   pl.BlockSpec((1, N_EXPERTS, tm), lambda i: (i, 0, 0))],
        out_shape=[jax.ShapeDtypeStruct((t, d), F32), jax.ShapeDtypeStruct((t, d), BF16),
                   jax.ShapeDtypeStruct((t // tm, N_EXPERTS, tm), F32)],
        compiler_params=_cparams(("parallel",)),
        name="out_projection_router",
    )(y_hy, y_ret, gates, x, g1, sc2, sh2, norm2_g.reshape(1, d), w_hy_o, w_ret_o, w_out, wr)


def _select_kernel(aff_ref, tri_ref, rank_ref, cnt_ref, start_ref, tot_ref, *, cap, idx_bits):
    nt, ne, tm = aff_ref.shape
    a = aff_ref[...]

    def count(m):
        return jnp.sum(jnp.sum(m, axis=0, keepdims=True), axis=2, keepdims=True)

    def thr_step(s, thr):
        cand = thr | (1 << (30 - s))
        cnt = count(jnp.where(a >= pltpu.bitcast(cand, F32), 1.0, 0.0))
        return jnp.where(cnt >= float(cap), cand, thr)

    thr = pltpu.bitcast(lax.fori_loop(0, 31, thr_step, jnp.zeros((1, ne, 1), jnp.int32)), F32)
    gt = a > thr
    eq = a == thr
    need = float(cap) - count(jnp.where(gt, 1.0, 0.0))
    idx = (lax.broadcasted_iota(jnp.int32, (nt, 1, tm), 0) * tm
           + lax.broadcasted_iota(jnp.int32, (nt, 1, tm), 2))

    def idx_step(s, lim):
        cand = lim | (1 << (idx_bits - 1 - s))
        cnt = count(jnp.where(eq, jnp.where(idx < cand, 1.0, 0.0), 0.0))
        return jnp.where(cnt < need, cand, lim)

    lim = lax.fori_loop(0, idx_bits, idx_step, jnp.zeros((1, ne, 1), jnp.int32))
    sel = jnp.where(gt, 1.0, jnp.where(eq, jnp.where(idx <= lim, 1.0, 0.0), 0.0))
    sel2 = sel.reshape(nt * ne, tm).astype(BF16)
    prefix = jnp.dot(sel2, tri_ref[...], preferred_element_type=F32)
    rank = jnp.where(sel2 > 0, prefix, -1.0).astype(jnp.int32)
    rank_ref[...] = rank.reshape(nt, ne, tm)
    ones = jnp.ones((tm, 128), BF16)
    cnt = jnp.dot(sel2, ones, preferred_element_type=F32).astype(jnp.int32).reshape(nt, ne, 128)
    cnt_ref[...] = cnt
    acc = jnp.zeros((ne, 128), jnp.int32)
    for t in range(nt):
        start_ref[t] = acc
        acc = acc + ((cnt[t] + (ROW_ALIGN - 1)) & (-ROW_ALIGN))
    tot_ref[...] = acc


def _select(aff, cap):
    nt, ne, tm = aff.shape
    idx_bits = max(1, int(math.ceil(math.log2(nt * tm))))
    r = lax.broadcasted_iota(jnp.int32, (tm, tm), 0)
    c = lax.broadcasted_iota(jnp.int32, (tm, tm), 1)
    tri = (r < c).astype(BF16)
    rank, cnt, start, tot = pl.pallas_call(
        functools.partial(_select_kernel, cap=cap, idx_bits=idx_bits),
        out_shape=[jax.ShapeDtypeStruct((nt, ne, tm), jnp.int32),
                   jax.ShapeDtypeStruct((nt, ne, 128), jnp.int32),
                   jax.ShapeDtypeStruct((nt, ne, 128), jnp.int32),
                   jax.ShapeDtypeStruct((ne, 128), jnp.int32)],
        compiler_params=_cparams(None),
        name="expert_choice_select",
    )(aff, tri)
    return rank, cnt[:, :, 0], start[:, :, 0], tot[:, 0]


def _used_rows(cap, nt):
    return -(-(cap + ROW_ALIGN * nt) // EXPERT_BLOCK) * EXPERT_BLOCK


def _list_rows(cap, nt):
    max_rounds = -(-TOKEN_TILE // SLOT_ROWS)
    return _used_rows(cap, nt) + max(EXPERT_BLOCK, max_rounds * SLOT_ROWS)


def _num_rounds(cnt_sm, i):
    m = cnt_sm[i, 0]
    for e in range(1, N_EXPERTS):
        m = jnp.maximum(m, cnt_sm[i, e])
    return jnp.maximum((m + SLOT_ROWS - 1) // SLOT_ROWS, 1)


def _gather_kernel(start_sm, cnt_sm, tot_sm, h_ref, rank_ref, xs_hbm, stage, zbuf, sem):
    i = pl.program_id(0)
    ne = N_EXPERTS
    slot = i % 2
    rank = rank_ref[0]
    sub = lax.broadcasted_iota(jnp.int32, (SLOT_ROWS, rank.shape[1]), 0)

    def copy(s, e, off):
        return pltpu.make_async_copy(stage.at[s, pl.ds(e * SLOT_ROWS, SLOT_ROWS)],
                                     xs_hbm.at[e, pl.ds(off, SLOT_ROWS)], sem.at[e])

    def fill(r):
        h = h_ref[...]
        for e in range(ne):
            onehot = jnp.where(rank[e:e + 1, :] == sub + r * SLOT_ROWS, 1.0, 0.0).astype(BF16)
            stage[slot, e * SLOT_ROWS:(e + 1) * SLOT_ROWS, :] = jnp.dot(
                onehot, h, preferred_element_type=F32).astype(BF16)

    def start_all(r):
        for e in range(ne):
            copy(slot, e, pl.multiple_of(start_sm[i, e] + r * SLOT_ROWS, ROW_ALIGN)).start()

    def wait_all(s):
        for e in range(ne):
            copy(s, e, 0).wait()

    fill(0)

    @pl.when(i > 0)
    def _():
        wait_all(1 - slot)

    start_all(0)

    def extra_round(r, carry):
        wait_all(slot)
        fill(r)
        start_all(r)
        return carry

    lax.fori_loop(1, _num_rounds(cnt_sm, i), extra_round, 0)

    @pl.when(i == pl.num_programs(0) - 1)
    def _zero_tail():
        wait_all(slot)
        zbuf[...] = jnp.zeros(zbuf.shape, zbuf.dtype)

        def zcopy(e):
            off = pl.multiple_of(tot_sm[e], ROW_ALIGN)
            return pltpu.make_async_copy(zbuf, xs_hbm.at[e, pl.ds(off, zbuf.shape[0])], sem.at[e])

        for e in range(ne):
            zcopy(e).start()
        for e in range(ne):
            zcopy(e).wait()


def _gather(h2, rank, start, cnt, tot, cap):
    t, d = h2.shape
    nt, ne, tm = rank.shape
    rl = _list_rows(cap, nt)
    return pl.pallas_call(
        _gather_kernel,
        grid_spec=pltpu.PrefetchScalarGridSpec(
            num_scalar_prefetch=3,
            grid=(nt,),
            in_specs=[pl.BlockSpec((tm, d), lambda i, *_: (i, 0)),
                      pl.BlockSpec((1, ne, tm), lambda i, *_: (i, 0, 0))],
            out_specs=pl.BlockSpec(memory_space=pl.ANY),
            scratch_shapes=[pltpu.VMEM((2, ne * SLOT_ROWS, d), BF16),
                            pltpu.VMEM((EXPERT_BLOCK, d), BF16),
                            pltpu.SemaphoreType.DMA((ne,))]),
        out_shape=jax.ShapeDtypeStruct((ne, rl, d), BF16),
        compiler_params=_cparams(("arbitrary",)),
        name="expert_gather",
    )(start, cnt, tot, h2, rank)


def _expert_kernel(tot_sm, xs_ref, wg_ref, wu_ref, wd_ref, ye_ref, wg_bf, wu_bf, wd_bf):
    e = pl.program_id(0)
    j = pl.program_id(1)

    @pl.when(j == 0)
    def _():
        wg_bf[...] = wg_ref[0].astype(BF16)
        wu_bf[...] = wu_ref[0].astype(BF16)
        wd_bf[...] = wd_ref[0].astype(BF16)

    @pl.when(j * EXPERT_BLOCK < tot_sm[e])
    def _():
        x = xs_ref[0]
        g = jnp.dot(x, wg_bf[...], preferred_element_type=F32)
        u = jnp.dot(x, wu_bf[...], preferred_element_type=F32)
        hid = (_silu(g) * u).astype(BF16)
        ye_ref[0] = jnp.dot(hid, wd_bf[...], preferred_element_type=F32).astype(ye_ref.dtype)


def _experts(xs, tot, used_rows, w_gate, w_up, w_down):
    ne, _, d = xs.shape
    f = w_gate.shape[2]
    last = lambda e, tot: (tot[e] - 1) // EXPERT_BLOCK
    blk = lambda e, j, tot: (e, jnp.minimum(j, last(e, tot)), 0)
    return pl.pallas_call(
        _expert_kernel,
        grid_spec=pltpu.PrefetchScalarGridSpec(
            num_scalar_prefetch=1,
            grid=(ne, used_rows // EXPERT_BLOCK),
            in_specs=[pl.BlockSpec((1, EXPERT_BLOCK, d), blk),
                      pl.BlockSpec((1, d, f), lambda e, j, tot: (e, 0, 0)),
                      pl.BlockSpec((1, d, f), lambda e, j, tot: (e, 0, 0)),
                      pl.BlockSpec((1, f, d), lambda e, j, tot: (e, 0, 0))],
            out_specs=pl.BlockSpec((1, EXPERT_BLOCK, d), blk),
            scratch_shapes=[pltpu.VMEM((d, f), BF16), pltpu.VMEM((d, f), BF16), pltpu.VMEM((f, d), BF16)]),
        out_shape=jax.ShapeDtypeStruct((ne, used_rows, d), BF16),
        compiler_params=_cparams(("arbitrary", "arbitrary")),
        name="expert_ffn",
    )(tot, xs, w_gate, w_up, w_down)


def _combine_kernel(start_sm, cnt_sm, tot_sm, x1_ref, rank_ref, aff_ref, g2_ref, nf_ref, ye_hbm, o_ref, buf, sem, *,
                    final_norm):
    i = pl.program_id(0)
    nt = pl.num_programs(0)
    ne = N_EXPERTS
    slot = i % 2
    rank = rank_ref[0]
    aff = aff_ref[0]
    sub = lax.broadcasted_iota(jnp.int32, (SLOT_ROWS, rank.shape[1]), 0)
    dn_t = (((0,), (0,)), ((), ()))

    def window(t, e, r):
        want = start_sm[t, e] + r * SLOT_ROWS
        written = ((tot_sm[e] + EXPERT_BLOCK - 1) // EXPERT_BLOCK) * EXPERT_BLOCK
        off = jnp.minimum(want, written - SLOT_ROWS)
        return pl.multiple_of(off, ROW_ALIGN), want - off

    def copy(t, s, e, r):
        off, _ = window(t, e, r)
        return pltpu.make_async_copy(ye_hbm.at[e, pl.ds(off, SLOT_ROWS)],
                                     buf.at[s, pl.ds(e * SLOT_ROWS, SLOT_ROWS)], sem.at[s, e])

    def weighted(r):
        parts = []
        for e in range(ne):
            _, shift = window(i, e, r)
            local = rank[e:e + 1, :] - r * SLOT_ROWS
            hit = jnp.where(local >= 0, local + shift, -1) == sub
            parts.append(jnp.where(hit, aff[e:e + 1, :], 0.0).astype(BF16))
        return lax.dot_general(jnp.concatenate(parts, axis=0), buf[slot], dn_t, preferred_element_type=F32)

    @pl.when(i == 0)
    def _():
        for e in range(ne):
            copy(0, 0, e, 0).start()

    @pl.when(i + 1 < nt)
    def _prefetch_next_tile():
        for e in range(ne):
            copy(i + 1, 1 - slot, e, 0).start()

    for e in range(ne):
        copy(i, slot, e, 0).wait()
    y0 = weighted(0)

    def extra_round(r, acc):
        for e in range(ne):
            @pl.when(cnt_sm[i, e] > r * SLOT_ROWS)
            def _():
                copy(i, slot, e, r).start()
        for e in range(ne):
            @pl.when(cnt_sm[i, e] > r * SLOT_ROWS)
            def _():
                copy(i, slot, e, r).wait()
        return acc + weighted(r)

    y = lax.fori_loop(1, _num_rounds(cnt_sm, i), extra_round, y0)
    x2 = x1_ref[...] + g2_ref[0] * y
    if final_norm:
        ms = jnp.mean(x2 * x2, axis=-1, keepdims=True)
        x2 = x2 * lax.rsqrt(ms + EPS) * nf_ref[...]
    o_ref[...] = x2


def _combine(x1, rank, aff, start, cnt, tot, g2, norm_f_g, ye, cond_of_tile):
    t, d = x1.shape
    nt, ne, tm = rank.shape
    final_norm = norm_f_g is not None
    if not final_norm:
        norm_f_g = jnp.ones((d,), F32)
    return pl.pallas_call(
        functools.partial(_combine_kernel, final_norm=final_norm),
        grid_spec=pltpu.PrefetchScalarGridSpec(
            num_scalar_prefetch=3,
            grid=(nt,),
            in_specs=[pl.BlockSpec((tm, d), lambda i, *_: (i, 0)),
                      pl.BlockSpec((1, ne, tm), lambda i, *_: (i, 0, 0)),
                      pl.BlockSpec((1, ne, tm), lambda i, *_: (i, 0, 0)),
                      pl.BlockSpec((1, 1, d), lambda i, *_: (cond_of_tile(i), 0, 0)),
                      pl.BlockSpec((1, d), lambda i, *_: (0, 0)),
                      pl.BlockSpec(memory_space=pl.ANY)],
            out_specs=pl.BlockSpec((tm, d), lambda i, *_: (i, 0)),
            scratch_shapes=[pltpu.VMEM((2, ne * SLOT_ROWS, d), BF16),
                            pltpu.SemaphoreType.DMA((2, ne))]),
        out_shape=jax.ShapeDtypeStruct((t, d), F32),
        compiler_params=_cparams(("arbitrary",)),
        name="expert_combine",
    )(start, cnt, tot, x1, rank, aff, g2, norm_f_g.reshape(1, d), ye)


def _trunk_layer(x, batch, seq_len, mods, cond_of_tile, p, s0_f, s0_b, rope, emit_state, final_g):
    t, d = x.shape
    sh1, sc1, g1, sh2, sc2, g2 = mods
    z, qkvg, gates = _in_projection(x, p["norm1_g"], sc1, sh1, p["w_in"], cond_of_tile,
                                    seq_len if rope else None)
    y_hy = _hyena_branch(z.reshape(batch, seq_len, -1), p).reshape(t, -1)
    ret = _retention(qkvg.reshape(batch, seq_len, -1), p["ret_decay_fwd"], p["ret_decay_bwd"], p["ret_gn_g"],
                     s0_f, s0_b, emit_state)
    y_ret = ret[0].reshape(t, -1)
    x1, h2, aff = _out_projection(y_hy, y_ret, gates, x, g1, sc2, sh2, p["norm2_g"], p["w_hy_o"], p["w_ret_o"],
                                  p["w_out"], p["w_router"], cond_of_tile)
    cap = (EC_CAPACITY_FACTOR * t) // N_EXPERTS
    rank, cnt, start, tot = _select(aff, cap)
    xs = _gather(h2, rank, start, cnt, tot, cap)
    ye = _experts(xs, tot, _used_rows(cap, rank.shape[0]), p["w_e_gate"], p["w_e_up"], p["w_e_down"])
    out = _combine(x1, rank, aff, start, cnt, tot, g2, final_g, ye, cond_of_tile)
    return out, ret[1:]


def kernel(x_prompt, x_sample, state_ret_fwd, state_ret_bwd, c, c_ctx, w_mod, b_mod, norm1_g, w_in, hy_conv_w, hy_conv_b, hy_f_w1, hy_f_b1, hy_f_freq, hy_f_w2, hy_f_b2, hy_f_w3, hy_decay, hy_bias, ret_decay_fwd, ret_decay_bwd, ret_gn_g, w_hy_o, w_ret_o, w_out, norm2_g, w_router, w_e_gate, w_e_up, w_e_down, norm_f_g):
    b, s, d = x_prompt.shape
    bd, sd, _ = x_sample.shape
    depth = w_mod.shape[0]
    assert (b * s) % TOKEN_TILE == 0 and sd % TOKEN_TILE == 0 and b % 2 == 0 and bd % 2 == 0
    ncond = -(-(bd + 1) // 8) * 8
    conds = jnp.concatenate([c, c_ctx[None], jnp.zeros((ncond - bd - 1, d), F32)], axis=0)
    ctx_cond = lambda i: bd
    lat_cond = lambda i: i // (sd // TOKEN_TILE)
    xp = x_prompt.reshape(b * s, d)
    xs = x_sample.reshape(bd * sd, d)
    new_f, new_b = [], []
    for l in range(depth):
        mod = _modulation(conds, w_mod[l], b_mod[l])
        mods = [mod[:, k * d:(k + 1) * d].reshape(ncond, 1, d) for k in range(6)]
        p = {"norm1_g": norm1_g[l], "w_in": w_in[l].astype(BF16), "hy_conv_w": hy_conv_w[l],
             "hy_conv_b": hy_conv_b[l], "hy_f_w1": hy_f_w1[l], "hy_f_b1": hy_f_b1[l], "hy_f_freq": hy_f_freq[l],
             "hy_f_w2": hy_f_w2[l], "hy_f_b2": hy_f_b2[l], "hy_f_w3": hy_f_w3[l], "hy_decay": hy_decay[l],
             "hy_bias": hy_bias[l], "ret_decay_fwd": ret_decay_fwd[l], "ret_decay_bwd": ret_decay_bwd[l],
             "ret_gn_g": ret_gn_g[l], "w_hy_o": w_hy_o[l].astype(BF16), "w_ret_o": w_ret_o[l].astype(BF16),
             "w_out": w_out[l].astype(BF16), "norm2_g": norm2_g[l], "w_router": w_router[l],
             "w_e_gate": w_e_gate[l], "w_e_up": w_e_up[l], "w_e_down": w_e_down[l]}
        final_g = norm_f_g if l == depth - 1 else None
        xp, (s_f, s_b) = _trunk_layer(xp, b, s, mods, ctx_cond, p, None, None, False, True, final_g)
        new_f.append(s_f)
        new_b.append(s_b)
        xs, _ = _trunk_layer(xs, bd, sd, mods, lat_cond, p, state_ret_fwd[:, l], state_ret_bwd[:, l],
                             True, False, final_g)
    y_prompt = xp.reshape(b, s, d)
    y_sample = xs.reshape(bd, sd, d)
    return (y_prompt, y_sample, jnp.stack(new_f, axis=1), jnp.stack(new_b, axis=1))
```

```python
import functools
import math

import numpy as np
import jax
import jax.numpy as jnp
from jax import lax
from jax.experimental import pallas as pl
from jax.experimental.pallas import tpu as pltpu

F32 = jnp.float32
BF16 = jnp.bfloat16
HIGHEST = lax.Precision.HIGHEST

EPS = 1e-6
D_HYENA = 512
D_RET = 512
N_RET_HEADS = 4
RET_HEAD_DIM = 128
RET_CHUNK = 128
GRID_W = 64
FILTER_EMB = 33
ROPE_BASE = 10000.0
N_EXPERTS = 16
EC_CAPACITY_FACTOR = 2

TOKEN_TILE = 512
SLOT_ROWS = 96
ROW_ALIGN = 16
HALO_ROWS = 16
FFT_MINOR = 64
FFT_GROUP = 4
VMEM_LIMIT = 56 * 1024 * 1024


def _cparams(sem):
    return pltpu.CompilerParams(dimension_semantics=sem, vmem_limit_bytes=VMEM_LIMIT)


def _silu(x):
    return x * jax.nn.sigmoid(x)


def _mod_kernel(c_ref, w_ref, b_ref, o_ref):
    s = _silu(c_ref[...])
    o_ref[...] = jnp.dot(s, w_ref[...], preferred_element_type=F32, precision=HIGHEST) + b_ref[...]


def _modulation(conds, w_mod, b_mod):
    nc, d = conds.shape
    return pl.pallas_call(
        _mod_kernel,
        grid=(6,),
        in_specs=[pl.BlockSpec((nc, d), lambda j: (0, 0)),
                  pl.BlockSpec((d, d), lambda j: (0, j)),
                  pl.BlockSpec((1, d), lambda j: (0, j))],
        out_specs=pl.BlockSpec((nc, d), lambda j: (0, j)),
        out_shape=jax.ShapeDtypeStruct((nc, 6 * d), F32),
        compiler_params=_cparams(("arbitrary",)),
        name="modulation",
    )(conds, w_mod, b_mod.reshape(1, -1))


def _inproj_kernel(*refs, splits, rope, seq_len, halo):
    refs = list(refs)
    x_ref = refs.pop(0)
    if halo:
        xp_ref, xn_ref = refs.pop(0), refs.pop(0)
    g_ref, sc_ref, sh_ref, w_ref, cw_ref, cb_ref = refs[:6]
    del refs[:6]
    if rope:
        cos_ref, sin_ref = refs.pop(0), refs.pop(0)
    u_ref, x2c_ref, qkvg_ref, gates_ref, cv_sc = refs

    def normed(ref):
        x = ref[...]
        ms = jnp.mean(x * x, axis=-1, keepdims=True)
        h = x * lax.rsqrt(ms + EPS) * g_ref[...]
        return (h * (1.0 + sc_ref[0]) + sh_ref[0]).astype(BF16)

    hb = normed(x_ref)
    tm = hb.shape[0]
    nz, nq, ng = splits
    cw = 512
    dh = RET_HEAD_DIM
    hr = HALO_ROWS
    if halo:
        t0 = pl.program_id(0) * tm
        hp = jnp.where(t0 % seq_len != 0, normed(xp_ref), jnp.zeros((hr, hb.shape[1]), BF16))
        hn = jnp.where((t0 + tm) % seq_len != 0, normed(xn_ref), jnp.zeros((hr, hb.shape[1]), BF16))
        hext = jnp.concatenate([hp, hb, hn], axis=0)

    def conv(c0):
        taps = cw_ref[:, c0:c0 + cw]
        bias = cb_ref[:, c0:c0 + cw]
        if halo:
            acc = jnp.dot(hext, w_ref[:, c0:c0 + cw], preferred_element_type=F32)
            n = tm + 2 * hr
            prev = pltpu.roll(acc, 1, axis=0)[hr:hr + tm]
            nxt = pltpu.roll(acc, n - 1, axis=0)[hr:hr + tm]
            return bias + prev * taps[0:1] + acc[hr:hr + tm] * taps[1:2] + nxt * taps[2:3]
        acc = jnp.dot(hb, w_ref[:, c0:c0 + cw], preferred_element_type=F32)
        row = lax.broadcasted_iota(jnp.int32, (seq_len, 1), 0)
        outs = []
        for s0 in range(0, tm, seq_len):
            x = acc[s0:s0 + seq_len]
            prev = jnp.where(row == 0, 0.0, pltpu.roll(x, 1, axis=0))
            nxt = jnp.where(row == seq_len - 1, 0.0, pltpu.roll(x, seq_len - 1, axis=0))
            outs.append(bias + prev * taps[0:1] + x * taps[1:2] + nxt * taps[2:3])
        return jnp.concatenate(outs, axis=0)

    cv_sc[...] = conv(0)
    u_ref[...] = (cv_sc[...] * conv(D_HYENA)).astype(BF16)
    x2c_ref[...] = conv(2 * D_HYENA).astype(BF16)

    for c0 in range(nz, nz + nq + ng, cw):
        acc = jnp.dot(hb, w_ref[:, c0:c0 + cw], preferred_element_type=F32)
        if c0 < nz + nq:
            part = (c0 - nz) // D_RET
            if part == 1:
                acc = acc * (dh ** -0.5)
            if rope and part < 2:
                lane = lax.broadcasted_iota(jnp.int32, (acc.shape[0], dh), 1)
                swap_hi = (lane % (dh // 2)) < (dh // 4)
                cs, sn = cos_ref[...], sin_ref[...]
                heads = []
                for hh in range(cw // dh):
                    xh = acc[:, hh * dh:(hh + 1) * dh]
                    rot = jnp.where(swap_hi, pltpu.roll(xh, dh - dh // 4, axis=1), pltpu.roll(xh, dh // 4, axis=1))
                    heads.append(xh * cs + rot * sn)
                acc = jnp.concatenate(heads, axis=1)
            qkvg_ref[:, c0 - nz:c0 - nz + cw] = acc.astype(BF16)
        else:
            gates_ref[:, c0 - nz - nq:c0 - nz - nq + cw] = jax.nn.sigmoid(acc).astype(BF16)


def _in_projection(x, norm_g, sc, sh, w_in_bf, conv_w, conv_b, cond_of_tile, seq_len, rope):
    t, d = x.shape
    splits = (3 * D_HYENA, 4 * D_RET, 2 * d)
    assert D_RET == 512 and D_HYENA == 512
    tm = TOKEN_TILE
    assert seq_len % tm == 0 or tm % seq_len == 0
    halo = seq_len > tm
    row = lambda i: (i, 0)
    cond = lambda i: (cond_of_tile(i), 0, 0)
    full = lambda a: pl.BlockSpec(a.shape, lambda i: (0, 0))
    args, specs = [x], [pl.BlockSpec((tm, d), row)]
    if halo:
        hb_per_tile = tm // HALO_ROWS
        last = t // HALO_ROWS - 1
        args += [x, x]
        specs += [pl.BlockSpec((HALO_ROWS, d), lambda i: (jnp.maximum(i * hb_per_tile - 1, 0), 0)),
                  pl.BlockSpec((HALO_ROWS, d), lambda i: (jnp.minimum((i + 1) * hb_per_tile, last), 0))]
    cbias = conv_b.reshape(1, -1)
    args += [norm_g.reshape(1, d), sc, sh, w_in_bf, conv_w, cbias]
    specs += [pl.BlockSpec((1, d), lambda i: (0, 0)), pl.BlockSpec((1, 1, d), cond), pl.BlockSpec((1, 1, d), cond),
              full(w_in_bf), full(conv_w), full(cbias)]
    if rope:
        tiles_per_seq = seq_len // tm
        args += list(_rope_tables(seq_len))
        specs += [pl.BlockSpec((tm, RET_HEAD_DIM), lambda i: (i % tiles_per_seq, 0))] * 2
    return pl.pallas_call(
        functools.partial(_inproj_kernel, splits=splits, rope=rope, seq_len=seq_len, halo=halo),
        grid=(t // tm,),
        in_specs=specs,
        out_specs=[pl.BlockSpec((tm, D_HYENA), row),
                   pl.BlockSpec((tm, D_HYENA), row),
                   pl.BlockSpec((tm, splits[1]), row),
                   pl.BlockSpec((tm, splits[2]), row)],
        out_shape=[jax.ShapeDtypeStruct((t, D_HYENA), BF16),
                   jax.ShapeDtypeStruct((t, D_HYENA), BF16),
                   jax.ShapeDtypeStruct((t, splits[1]), BF16),
                   jax.ShapeDtypeStruct((t, splits[2]), BF16)],
        scratch_shapes=[pltpu.VMEM((tm, 512), F32)],
        compiler_params=_cparams(("parallel",)),
        name="in_projection",
    )(*args)


def _filter_features(seq_len):
    t = np.linspace(0.0, 1.0, seq_len, dtype=np.float32)[:, None]
    bands = (FILTER_EMB - 1) // 2
    w = (np.float32(2.0 * math.pi) * np.arange(seq_len, dtype=np.float32)) / np.float32(seq_len)
    f = np.linspace(1e-4, bands - 1, bands, dtype=np.float32)
    ang = (w[:, None] * f[None, :]).astype(np.float64)
    z = np.concatenate([t, np.cos(ang), -np.sin(ang)], axis=-1).astype(np.float32)
    return np.pad(z, ((0, 0), (0, 128 - FILTER_EMB)))


def _filter_kernel(z_ref, w1_ref, b1_ref, fr_ref, w2_ref, b2_ref, w3_ref, dec_ref, o_ref):
    z = z_ref[...]
    fr = fr_ref[...]
    dot = functools.partial(jnp.dot, preferred_element_type=F32, precision=HIGHEST)
    h = jnp.sin(fr * (dot(z, w1_ref[...]) + b1_ref[...]))
    h = jnp.sin(fr * (dot(h, w2_ref[...]) + b2_ref[...]))
    h = dot(h, w3_ref[...]) * jnp.exp(-z[:, 0:1] * jnp.abs(dec_ref[...]))
    rows = h.shape[0]
    grow = pl.program_id(0) * rows + lax.broadcasted_iota(jnp.int32, (rows, 1), 0)
    o_ref[0] = h[:, :D_HYENA].astype(o_ref.dtype)
    o_ref[1] = jnp.zeros((rows, D_HYENA), o_ref.dtype)
    o_ref[2] = jnp.where(grow == 0, 0.0, h[:, D_HYENA:]).astype(o_ref.dtype)
    o_ref[3] = jnp.zeros((rows, D_HYENA), o_ref.dtype)


def _hyena_filters(seq_len, p, out_dtype):
    z = jnp.asarray(_filter_features(seq_len))
    w1 = jnp.pad(p["hy_f_w1"], ((0, 128 - FILTER_EMB), (0, 0)))
    rows = min(seq_len, 512)
    full = lambda a: pl.BlockSpec(a.shape, lambda i: (0,) * a.ndim)
    ops = [w1, p["hy_f_b1"].reshape(1, -1), p["hy_f_freq"].reshape(1, -1), p["hy_f_w2"],
           p["hy_f_b2"].reshape(1, -1), p["hy_f_w3"], p["hy_decay"].reshape(1, -1)]
    return pl.pallas_call(
        _filter_kernel,
        grid=(seq_len // rows,),
        in_specs=[pl.BlockSpec((rows, 128), lambda i: (i, 0))] + [full(a) for a in ops],
        out_specs=pl.BlockSpec((4, rows, D_HYENA), lambda i: (0, i, 0)),
        out_shape=jax.ShapeDtypeStruct((4, seq_len, D_HYENA), out_dtype),
        compiler_params=_cparams(("arbitrary",)),
        name="hyena_filter",
    )(z, *ops)


def _stacked_dft(n_out, n_in, modulus, scale=1.0):
    k = lax.broadcasted_iota(jnp.int32, (n_out, n_in), 0)
    n = lax.broadcasted_iota(jnp.int32, (n_out, n_in), 1)
    th = ((k * n) % modulus).astype(F32) * (2.0 * math.pi / modulus)
    c, s = jnp.cos(th), jnp.sin(th)
    fwd = jnp.block([[c, s], [-s, c]])
    inv = jnp.block([[c.T, -s.T], [s.T, c.T]]) * scale
    return fwd, inv


def _second_level_tables(n_total, n1, n2):
    shape = (n1, n2, n2)
    k1 = lax.broadcasted_iota(jnp.int32, shape, 0)
    k2 = lax.broadcasted_iota(jnp.int32, shape, 1)
    m = lax.broadcasted_iota(jnp.int32, shape, 2)
    th = ((m * (k1 + n1 * k2)) % n_total).astype(F32) * (2.0 * math.pi / n_total)
    c, s = jnp.cos(th), jnp.sin(th)
    g = jnp.concatenate([jnp.concatenate([c, s], axis=2), jnp.concatenate([-s, c], axis=2)], axis=1)
    return g, jnp.swapaxes(g, 1, 2)


def _fft1_core(u_sc, f_ref, a_ref, t_sc):
    n2 = FFT_MINOR
    _, seq_len, cb = u_sc.shape
    h1 = seq_len // n2
    nk = t_sc.shape[0] // n2

    def fwd(gi, carry):
        m0 = gi * FFT_GROUP
        cols = []
        for d in range(FFT_GROUP):
            xr = u_sc[0, pl.ds(m0 + d, h1, stride=n2), :]
            xi = u_sc[1, pl.ds(m0 + d, h1, stride=n2), :]
            cols.append(jnp.concatenate([xr, xi], axis=0))
        x = jnp.concatenate(cols, axis=1).astype(BF16)
        res = jnp.dot(f_ref[...], x, preferred_element_type=F32)
        for d in range(FFT_GROUP):
            t_sc[pl.ds(pl.multiple_of((m0 + d) * nk, nk), nk), :] = res[:, d * cb:(d + 1) * cb]
        return carry

    lax.fori_loop(0, n2 // FFT_GROUP, fwd, 0, unroll=2)

    def transpose(k, carry):
        a_ref[0, k] = t_sc[pl.ds(k, n2, stride=nk), :].astype(a_ref.dtype)
        return carry

    lax.fori_loop(0, nk, transpose, 0, unroll=8)


def _fft1_kernel(h_ref, f_ref, a_ref, u_sc, t_sc):
    u_sc[...] = h_ref[...].astype(F32)
    _fft1_core(u_sc, f_ref, a_ref, t_sc)


def _fft_first_level(h, f1):
    b, seq_len, c = h.shape
    cb = 128
    return pl.pallas_call(
        _fft1_kernel,
        grid=(b // 2, c // cb),
        in_specs=[pl.BlockSpec((2, seq_len, cb), lambda i, j: (i, 0, j)),
                  pl.BlockSpec(f1.shape, lambda i, j: (0, 0))],
        out_specs=pl.BlockSpec((1, f1.shape[0], FFT_MINOR, cb), lambda i, j: (i, 0, 0, j)),
        out_shape=jax.ShapeDtypeStruct((b // 2, f1.shape[0], FFT_MINOR, c), BF16),
        scratch_shapes=[pltpu.VMEM((2, seq_len, cb), F32), pltpu.VMEM((FFT_MINOR * f1.shape[0], cb), F32)],
        compiler_params=_cparams(("parallel", "parallel")),
        name="hyena_dft_level1",
    )(h, f1)


def _fft_s2f_kernel(a_ref, g_ref, kr_ref, ki_ref):
    _, _, kb, n2, c = a_ref.shape
    for kk in range(kb):
        g = g_ref[kk]
        hf = jnp.dot(g, a_ref[0, :, kk].reshape(2 * n2, c), preferred_element_type=F32)
        hb = jnp.dot(g, a_ref[1, :, kk].reshape(2 * n2, c), preferred_element_type=F32)
        kr_ref[kk] = hf[:n2] + hb[:n2]
        ki_ref[kk] = hf[n2:] - hb[n2:]


def _fft_filter_second_level(a, g, kb=8):
    _, _, n1, n2, c = a.shape
    spec = pl.BlockSpec((kb, n2, c), lambda i: (i, 0, 0))
    return pl.pallas_call(
        _fft_s2f_kernel,
        grid=(n1 // kb,),
        in_specs=[pl.BlockSpec((2, 2, kb, n2, c), lambda i: (0, 0, i, 0, 0)),
                  pl.BlockSpec((kb, 2 * n2, 2 * n2), lambda i: (i, 0, 0))],
        out_specs=[spec, spec],
        out_shape=[jax.ShapeDtypeStruct((n1, n2, c), F32)] * 2,
        compiler_params=_cparams(("parallel",)),
        name="hyena_filter_spectrum",
    )(a, g)


def _fft_s2_kernel(a_ref, g_ref, gt_ref, kr_ref, ki_ref, b_ref, r_sc):
    npairs, _, kb, n2, cb = a_ref.shape
    nq = cb // 128
    for kk in range(kb):
        a = jnp.concatenate([a_ref[p, :, kk].reshape(2 * n2, cb) for p in range(npairs)], axis=1)
        x = jnp.dot(g_ref[kk], a, preferred_element_type=F32)
        xr, xi = x[:n2], x[n2:]
        kr = jnp.concatenate([kr_ref[kk]] * npairs, axis=1)
        ki = jnp.concatenate([ki_ref[kk]] * npairs, axis=1)
        y = jnp.concatenate([xr * kr - xi * ki, xr * ki + xi * kr], axis=0).astype(BF16)
        res = jnp.dot(gt_ref[kk], y, preferred_element_type=F32)
        for q in range(npairs * nq):
            r_sc[q, kk * 2 * n2:(kk + 1) * 2 * n2, :] = res[:, q * 128:(q + 1) * 128]

    for p in range(npairs):
        def gather(row):
            return jnp.concatenate([r_sc[p * nq + q, pl.ds(row, kb, stride=2 * n2), :] for q in range(nq)], axis=1)

        def transpose(m, carry):
            b_ref[p, 0, m] = gather(m).astype(b_ref.dtype)
            b_ref[p, 1, m] = gather(n2 + m).astype(b_ref.dtype)
            return carry

        lax.fori_loop(0, n2, transpose, 0, unroll=8)


def _fft_second_level(a, g, gt, kr, ki):
    p, _, n1, n2, c = a.shape
    kb, cb = 16, 256
    gspec = pl.BlockSpec((kb, 2 * n2, 2 * n2), lambda i, j: (i, 0, 0))
    kspec = pl.BlockSpec((kb, n2, cb), lambda i, j: (i, 0, j))
    return pl.pallas_call(
        _fft_s2_kernel,
        grid=(n1 // kb, c // cb),
        in_specs=[pl.BlockSpec((p, 2, kb, n2, cb), lambda i, j: (0, 0, i, 0, j)), gspec, gspec, kspec, kspec],
        out_specs=pl.BlockSpec((p, 2, n2, kb, cb), lambda i, j: (0, 0, 0, i, j)),
        out_shape=jax.ShapeDtypeStruct((p, 2, n2, n1, c), BF16),
        scratch_shapes=[pltpu.VMEM((p * cb // 128, kb * 2 * n2, 128), F32)],
        compiler_params=_cparams(("parallel", "parallel")),
        name="hyena_dft_level2",
    )(a, g, gt, kr, ki)


def _fft_s3_kernel(b_ref, f_ref, u_ref, x2_ref, bias_ref, o_ref, t_sc):
    _, _, n2, n1, cb = b_ref.shape
    nr = t_sc.shape[0] // n2
    h1 = nr // 2

    def inv(gi, carry):
        m0 = gi * FFT_GROUP
        x = jnp.concatenate([b_ref[0, :, m0 + d].reshape(2 * n1, cb) for d in range(FFT_GROUP)], axis=1)
        res = jnp.dot(f_ref[...], x, preferred_element_type=F32)
        for d in range(FFT_GROUP):
            t_sc[pl.ds(pl.multiple_of((m0 + d) * nr, nr), nr), :] = res[:, d * cb:(d + 1) * cb]
        return carry

    lax.fori_loop(0, n2 // FFT_GROUP, inv, 0, unroll=2)

    bias = bias_ref[...]
    for b in range(2):
        def finish(q, carry):
            rows = pl.ds(pl.multiple_of(q * n2, n2), n2)
            conv = t_sc[pl.ds(b * h1 + q, n2, stride=nr), :]
            u = u_ref[b, rows, :].astype(F32)
            o_ref[b, rows, :] = ((conv + u * bias) * x2_ref[b, rows, :].astype(F32)).astype(o_ref.dtype)
            return carry

        lax.fori_loop(0, h1, finish, 0, unroll=4)


def _fft_last_level(bt, f1inv, u, x2c, bias):
    p, _, n2, n1, c = bt.shape
    b, seq_len, _ = u.shape
    cb = 128
    uspec = pl.BlockSpec((2, seq_len, cb), lambda i, j: (i, 0, j))
    return pl.pallas_call(
        _fft_s3_kernel,
        grid=(p, c // cb),
        in_specs=[pl.BlockSpec((1, 2, n2, n1, cb), lambda i, j: (i, 0, 0, 0, j)),
                  pl.BlockSpec(f1inv.shape, lambda i, j: (0, 0)),
                  uspec, uspec,
                  pl.BlockSpec((1, cb), lambda i, j: (0, j))],
        out_specs=uspec,
        out_shape=jax.ShapeDtypeStruct(u.shape, BF16),
        scratch_shapes=[pltpu.VMEM((n2 * f1inv.shape[0], cb), F32)],
        compiler_params=_cparams(("parallel", "parallel")),
        name="hyena_dft_inverse",
    )(bt, f1inv, u, x2c, bias)


def _kf_direct_kernel(h_ref, f_ref, kr_ref, ki_ref):
    n = kr_ref.shape[0]
    dot = functools.partial(jnp.dot, preferred_element_type=F32, precision=HIGHEST)
    hf = dot(f_ref[...], h_ref[0])
    hb = dot(f_ref[...], h_ref[2])
    kr_ref[...] = hf[:n] + hb[:n]
    ki_ref[...] = hf[n:] - hb[n:]


def _hyena_direct_kernel(u_ref, x2_ref, f_ref, fi_ref, kr_ref, ki_ref, bias_ref, o_ref):
    _, two, seq_len, cb = u_ref.shape
    n = kr_ref.shape[0]
    ub = u_ref[0].reshape(two * seq_len, cb)
    x = jnp.dot(f_ref[...], ub, preferred_element_type=F32)
    xr, xi = x[:n], x[n:]
    kr, ki = kr_ref[...], ki_ref[...]
    y = jnp.concatenate([xr * kr - xi * ki, xr * ki + xi * kr], axis=0).astype(BF16)
    conv = jnp.dot(fi_ref[...], y, preferred_element_type=F32)
    x2 = x2_ref[0].reshape(two * seq_len, cb).astype(F32)
    o = (conv + ub.astype(F32) * bias_ref[...]) * x2
    o_ref[0] = o.reshape(two, seq_len, cb).astype(o_ref.dtype)


def _hyena_long_conv(u, x2c, p):
    b, seq_len, _ = u.shape
    c = D_HYENA
    n = 2 * seq_len
    bias = p["hy_bias"].reshape(1, c)
    hh = _hyena_filters(seq_len, p, F32)
    if seq_len <= 512:
        fwd, inv = _stacked_dft(n, seq_len, n, scale=1.0 / n)
        kr, ki = pl.pallas_call(
            _kf_direct_kernel,
            out_shape=[jax.ShapeDtypeStruct((n, c), F32)] * 2,
            compiler_params=_cparams(None),
            name="hyena_filter_spectrum_direct",
        )(hh, fwd[:, :seq_len])
        cb = 256
        pair = lambda a: a.reshape(b // 2, 2, seq_len, c)
        uspec = pl.BlockSpec((1, 2, seq_len, cb), lambda i, j: (i, 0, 0, j))
        kspec = pl.BlockSpec((n, cb), lambda i, j: (0, j))
        out = pl.pallas_call(
            _hyena_direct_kernel,
            grid=(b // 2, c // cb),
            in_specs=[uspec, uspec,
                      pl.BlockSpec(fwd.shape, lambda i, j: (0, 0)),
                      pl.BlockSpec(inv.shape, lambda i, j: (0, 0)),
                      kspec, kspec,
                      pl.BlockSpec((1, cb), lambda i, j: (0, j))],
            out_specs=uspec,
            out_shape=jax.ShapeDtypeStruct((b // 2, 2, seq_len, c), BF16),
            compiler_params=_cparams(("parallel", "parallel")),
            name="hyena_dft_direct",
        )(pair(u), pair(x2c), fwd.astype(BF16), inv.astype(BF16), kr, ki, bias)
        return out.reshape(b, seq_len, c)

    n2 = FFT_MINOR
    n1 = n // n2
    h1 = seq_len // n2
    f1, f1inv = _stacked_dft(n1, h1, n1, scale=1.0 / n)
    f1, f1inv = f1.astype(BF16), f1inv.astype(BF16)
    g, gt = _second_level_tables(n, n1, n2)
    g, gt = g.astype(BF16), gt.astype(BF16)
    ha = _fft_first_level(hh, f1)
    kr, ki = _fft_filter_second_level(ha.reshape(2, 2, n1, n2, c), g)
    a = _fft_first_level(u, f1)
    bt = _fft_second_level(a.reshape(b // 2, 2, n1, n2, c), g, gt, kr, ki)
    return _fft_last_level(bt, f1inv, u, x2c, bias)


def _rope_tables(seq_len):
    half = RET_HEAD_DIM // 2
    nf = half // 2
    t = jnp.arange(seq_len)
    row = (t // GRID_W).astype(F32)
    col = (t % GRID_W).astype(F32)
    inv = ROPE_BASE ** (-jnp.arange(nf, dtype=F32) / nf)
    ar = row[:, None] * inv[None, :]
    ac = col[:, None] * inv[None, :]
    cos = jnp.concatenate([jnp.cos(ar), jnp.cos(ar), jnp.cos(ac), jnp.cos(ac)], axis=-1)
    sin = jnp.concatenate([-jnp.sin(ar), jnp.sin(ar), -jnp.sin(ac), jnp.sin(ac)], axis=-1)
    return cos, sin


def _log_sigmoid(x):
    return jnp.minimum(x, 0.0) - jnp.log1p(jnp.exp(-jnp.abs(x)))


def _retention_kernel(*refs, has_init, emit_state, cpb):
    refs = list(refs)
    q_ref, k_ref, v_ref, g_ref, dec_ref, gn_ref = refs[:6]
    del refs[:6]
    if has_init:
        s0f_ref, s0b_ref = refs[:2]
        del refs[:2]
    o_ref = refs.pop(0)
    if emit_state:
        sf_out, sb_out = refs[:2]
        del refs[:2]
    sf_ref, sb_ref, sball_ref = refs

    c = RET_CHUNK
    dh = RET_HEAD_DIM
    nh = N_RET_HEADS
    phase = pl.program_id(1)
    j = pl.program_id(2)
    nb = pl.num_programs(2)

    ri = lax.broadcasted_iota(jnp.int32, (c, c), 0).astype(F32)
    ci = lax.broadcasted_iota(jnp.int32, (c, c), 1).astype(F32)
    diff = ri - ci

    def head_consts(h):
        lgf = _log_sigmoid(dec_ref[0, h])[0:1, :]
        lgb = _log_sigmoid(dec_ref[1, h])[0:1, :]
        return lgf, lgb

    def load(ref, r0, h):
        return ref[0, r0:r0 + c, h * dh:(h + 1) * dh]

    dn_t = (((0,), (0,)), ((), ()))
    dn_nt = (((1,), (1,)), ((), ()))

    @pl.when(phase == 0)
    def _backward_sweep():
        @pl.when(j == 0)
        def _():
            for h in range(nh):
                sb_ref[h] = s0b_ref[0, h] if has_init else jnp.zeros((dh, dh), F32)

        blk = nb - 1 - j
        for h in range(nh):
            _, lgb = head_consts(h)
            zeta_b = jnp.exp(lgb * ri)
            cdec_b = jnp.exp(lgb * float(c))
            for cc in reversed(range(cpb)):
                r0 = cc * c
                n = blk * cpb + cc
                s = sb_ref[h]
                sball_ref[n, h] = s.astype(BF16)
                kz = (load(k_ref, r0, h).astype(F32) * zeta_b).astype(BF16)
                vv = load(v_ref, r0, h)
                sb_ref[h] = cdec_b * s + lax.dot_general(kz, vv, dn_t, preferred_element_type=F32)

        if emit_state:
            @pl.when(j == nb - 1)
            def _():
                for h in range(nh):
                    sb_out[0, h] = sb_ref[h]

    @pl.when(phase == 1)
    def _forward_sweep():
        @pl.when(j == 0)
        def _():
            for h in range(nh):
                sf_ref[h] = s0f_ref[0, h] if has_init else jnp.zeros((dh, dh), F32)

        for h in range(nh):
            lgf, lgb = head_consts(h)
            mask = (jnp.where(diff >= 0, jnp.exp(lgf * jnp.maximum(diff, 0.0)), 0.0)
                    + jnp.where(diff <= 0, jnp.exp(lgb * jnp.maximum(-diff, 0.0)), 0.0))
            xi_f = jnp.exp(lgf * (ri + 1.0))
            xi_b = jnp.exp(lgb * (float(c) - ri))
            zeta_f = jnp.exp(lgf * (float(c - 1) - ri))
            cdec_f = jnp.exp(lgf * float(c))
            gn = gn_ref[:, h * dh:(h + 1) * dh]
            for cc in range(cpb):
                r0 = cc * c
                n = j * cpb + cc
                qb = load(q_ref, r0, h)
                kb = load(k_ref, r0, h)
                vv = load(v_ref, r0, h)
                gate = load(g_ref, r0, h).astype(F32)
                sc = lax.dot_general(qb, kb, dn_nt, preferred_element_type=F32)
                inner = jnp.dot((sc * mask).astype(BF16), vv, preferred_element_type=F32)
                s = sf_ref[h]
                q = qb.astype(F32)
                lhs = jnp.concatenate([q * xi_f, q * xi_b], axis=1).astype(BF16)
                rhs = jnp.concatenate([s.astype(BF16), sball_ref[n, h]], axis=0)
                o = inner + jnp.dot(lhs, rhs, preferred_element_type=F32)
                mu = jnp.mean(o, axis=-1, keepdims=True)
                d = o - mu
                var = jnp.mean(d * d, axis=-1, keepdims=True)
                y = d * lax.rsqrt(var + EPS) * gn * _silu(gate)
                o_ref[0, r0:r0 + c, h * dh:(h + 1) * dh] = y.astype(o_ref.dtype)
                kz = (kb.astype(F32) * zeta_f).astype(BF16)
                sf_ref[h] = cdec_f * s + lax.dot_general(kz, vv, dn_t, preferred_element_type=F32)

        if emit_state:
            @pl.when(j == nb - 1)
            def _():
                for h in range(nh):
                    sf_out[0, h] = sf_ref[h]


def _retention(qkvg, dec_f, dec_b, gn_g, s0_f, s0_b, emit_state):
    b, seq_len, _ = qkvg.shape
    nh, dh, c = N_RET_HEADS, RET_HEAD_DIM, RET_CHUNK
    rb = min(seq_len, 512)
    nb = seq_len // rb
    cpb = rb // c
    has_init = s0_f is not None
    dec = jnp.broadcast_to(jnp.stack([dec_f, dec_b])[:, :, None, None], (2, nh, 8, 128)).astype(F32)
    kv_blk = lambda i, p, j: jnp.where(p == 0, nb - 1 - j, j)
    q_blk = lambda i, p, j: jnp.where(p == 0, 0, j)
    in_specs = [pl.BlockSpec((1, rb, D_RET), lambda i, p, j: (i, q_blk(i, p, j), 0)),
                pl.BlockSpec((1, rb, D_RET), lambda i, p, j: (i, kv_blk(i, p, j), 1)),
                pl.BlockSpec((1, rb, D_RET), lambda i, p, j: (i, kv_blk(i, p, j), 2)),
                pl.BlockSpec((1, rb, D_RET), lambda i, p, j: (i, q_blk(i, p, j), 3)),
                pl.BlockSpec((2, nh, 8, 128), lambda i, p, j: (0, 0, 0, 0)),
                pl.BlockSpec((1, D_RET), lambda i, p, j: (0, 0))]
    args = [qkvg, qkvg, qkvg, qkvg, dec, gn_g.reshape(1, -1)]
    sspec = pl.BlockSpec((1, nh, dh, dh), lambda i, p, j: (i, 0, 0, 0))
    if has_init:
        in_specs += [sspec, sspec]
        args += [s0_f, s0_b]
    out_specs = [pl.BlockSpec((1, rb, D_RET), lambda i, p, j: (i, q_blk(i, p, j), 0))]
    out_shape = [jax.ShapeDtypeStruct((b, seq_len, D_RET), BF16)]
    if emit_state:
        out_specs += [sspec, sspec]
        out_shape += [jax.ShapeDtypeStruct((b, nh, dh, dh), F32)] * 2
    return pl.pallas_call(
        functools.partial(_retention_kernel, has_init=has_init, emit_state=emit_state, cpb=cpb),
        grid=(b, 2, nb),
        in_specs=in_specs,
        out_specs=out_specs,
        out_shape=out_shape,
        scratch_shapes=[pltpu.VMEM((nh, dh, dh), F32), pltpu.VMEM((nh, dh, dh), F32),
                        pltpu.VMEM((nb * cpb, nh, dh, dh), BF16)],
        compiler_params=_cparams(("parallel", "arbitrary", "arbitrary")),
        name="retention",
    )(*args)


def _outproj_kernel(yhy_ref, yret_ref, gates_ref, x_ref, g1_ref, sc_ref, sh_ref, ng_ref,
                    why_ref, wret_ref, wout_ref, wr_ref, x1_ref, h2_ref, aff_ref):
    g, m, d = x_ref.shape
    rows = lambda ref: ref[...].reshape(g * m, ref.shape[2])
    a = jnp.dot(rows(yhy_ref), why_ref[...], preferred_element_type=F32)
    b = jnp.dot(rows(yret_ref), wret_ref[...], preferred_element_type=F32)
    gates = rows(gates_ref)
    merged = gates[:, :d].astype(F32) * a + gates[:, d:].astype(F32) * b
    out = jnp.dot(merged.astype(BF16), wout_ref[...], preferred_element_type=F32)
    x1 = x_ref[...] + g1_ref[...] * out.reshape(g, m, d)
    x1_ref[...] = x1
    ms = jnp.mean(x1 * x1, axis=-1, keepdims=True)
    h = x1 * lax.rsqrt(ms + EPS) * ng_ref[...]
    h = (h * (1.0 + sc_ref[...]) + sh_ref[...]).reshape(g * m, d)
    h_hi = h.astype(BF16)
    h2_ref[...] = h_hi.reshape(g, m, d)
    h_lo = (h - h_hi.astype(F32)).astype(BF16)
    t = jnp.dot(h_hi, wr_ref[...], preferred_element_type=F32)
    logits = t[:, :128] + t[:, 128:] + jnp.dot(h_lo, wr_ref[:, :128], preferred_element_type=F32)
    lane = lax.broadcasted_iota(jnp.int32, logits.shape, 1)
    logits = jnp.where(lane < N_EXPERTS, logits, -jnp.inf)
    e = jnp.exp(logits - jnp.max(logits, axis=-1, keepdims=True))
    aff = e / jnp.sum(e, axis=-1, keepdims=True)
    aff_ref[0] = aff.T[:N_EXPERTS, :]


def _out_projection(y_hy, y_ret, gates, x, g1, sc2, sh2, norm2_g, w_hy_o, w_ret_o, w_out, w_router, cond_block):
    g, s, d = x.shape
    tm = TOKEN_TILE
    m = tm // g
    tok = lambda c: pl.BlockSpec((g, m, c), lambda i: (0, i, 0))
    cond = pl.BlockSpec((g, 1, d), lambda i: (cond_block, 0, 0))
    full = lambda a: pl.BlockSpec(a.shape, lambda i: (0, 0))
    wr = jnp.pad(w_router, ((0, 0), (0, 128 - N_EXPERTS)))
    wr_hi = wr.astype(BF16)
    wr = jnp.concatenate([wr_hi, (wr - wr_hi.astype(F32)).astype(BF16)], axis=1)
    return pl.pallas_call(
        _outproj_kernel,
        grid=(s // m,),
        in_specs=[tok(D_HYENA), tok(D_RET), tok(2 * d), tok(d), cond, cond, cond,
                  pl.BlockSpec((1, d), lambda i: (0, 0)),
                  full(w_hy_o), full(w_ret_o), full(w_out), full(wr)],
        out_specs=[tok(d), tok(d), pl.BlockSpec((1, N_EXPERTS, tm), lambda i: (i, 0, 0))],
        out_shape=[jax.ShapeDtypeStruct((g, s, d), F32), jax.ShapeDtypeStruct((g, s, d), BF16),
                   jax.ShapeDtypeStruct((s // m, N_EXPERTS, tm), F32)],
        compiler_params=_cparams(("parallel",)),
        name="out_projection_router",
    )(y_hy, y_ret, gates, x, g1, sc2, sh2, norm2_g.reshape(1, d), w_hy_o, w_ret_o, w_out, wr)


def _select_kernel(aff_ref, tri_ref, rank_ref, cnt_ref, start_ref, tot_ref, *, cap, idx_bits, groups):
    nt, ne, tm = aff_ref.shape
    a = aff_ref[...]

    def count(m):
        return jnp.sum(jnp.sum(m, axis=0, keepdims=True), axis=2, keepdims=True)

    def thr_step(s, thr):
        cand = thr | (1 << (30 - s))
        cnt = count(jnp.where(a >= pltpu.bitcast(cand, F32), 1.0, 0.0))
        return jnp.where(cnt >= float(cap), cand, thr)

    thr = pltpu.bitcast(lax.fori_loop(0, 31, thr_step, jnp.zeros((1, ne, 1), jnp.int32)), F32)
    gt = a > thr
    eq = a == thr
    need = float(cap) - count(jnp.where(gt, 1.0, 0.0))
    m = tm // groups
    tile = lax.broadcasted_iota(jnp.int32, (nt, 1, tm), 0)
    lane = lax.broadcasted_iota(jnp.int32, (nt, 1, tm), 2)
    idx = (lane // m) * (nt * m) + tile * m + lane % m

    def idx_step(s, lim):
        cand = lim | (1 << (idx_bits - 1 - s))
        cnt = count(jnp.where(eq, jnp.where(idx < cand, 1.0, 0.0), 0.0))
        return jnp.where(cnt < need, cand, lim)

    lim = lax.fori_loop(0, idx_bits, idx_step, jnp.zeros((1, ne, 1), jnp.int32))
    sel = jnp.where(gt, 1.0, jnp.where(eq, jnp.where(idx <= lim, 1.0, 0.0), 0.0))
    sel2 = sel.reshape(nt * ne, tm).astype(BF16)
    prefix = jnp.dot(sel2, tri_ref[...], preferred_element_type=F32)
    rank = jnp.where(sel2 > 0, prefix, -1.0).astype(jnp.int32)
    rank_ref[...] = rank.reshape(nt, ne, tm)
    ones = jnp.ones((tm, 128), BF16)
    cnt = jnp.dot(sel2, ones, preferred_element_type=F32).astype(jnp.int32).reshape(nt, ne, 128)
    cnt_ref[...] = cnt
    acc = jnp.zeros((ne, 128), jnp.int32)
    for t in range(nt):
        start_ref[t] = acc
        acc = acc + ((cnt[t] + (ROW_ALIGN - 1)) & (-ROW_ALIGN))
    tot_ref[...] = acc


def _select(aff, cap, groups):
    nt, ne, tm = aff.shape
    idx_bits = max(1, int(math.ceil(math.log2(nt * tm))))
    r = lax.broadcasted_iota(jnp.int32, (tm, tm), 0)
    c = lax.broadcasted_iota(jnp.int32, (tm, tm), 1)
    tri = (r < c).astype(BF16)
    rank, cnt, start, tot = pl.pallas_call(
        functools.partial(_select_kernel, cap=cap, idx_bits=idx_bits, groups=groups),
        out_shape=[jax.ShapeDtypeStruct((nt, ne, tm), jnp.int32),
                   jax.ShapeDtypeStruct((nt, ne, 128), jnp.int32),
                   jax.ShapeDtypeStruct((nt, ne, 128), jnp.int32),
                   jax.ShapeDtypeStruct((ne, 128), jnp.int32)],
        compiler_params=_cparams(None),
        name="expert_choice_select",
    )(aff, tri)
    return rank, cnt[:, :, 0], start[:, :, 0], tot[:, 0]


def _expert_block(cap):
    return 512 if cap >= 2048 else 256


def _used_rows(cap, nt):
    eb = _expert_block(cap)
    return -(-(cap + ROW_ALIGN * nt) // eb) * eb


def _list_rows(cap, nt):
    max_rounds = -(-TOKEN_TILE // SLOT_ROWS)
    return _used_rows(cap, nt) + max(_expert_block(cap), max_rounds * SLOT_ROWS)


def _num_rounds(cnt_sm, i):
    m = cnt_sm[i, 0]
    for e in range(1, N_EXPERTS):
        m = jnp.maximum(m, cnt_sm[i, e])
    return jnp.maximum((m + SLOT_ROWS - 1) // SLOT_ROWS, 1)


def _gather_kernel(start_sm, cnt_sm, tot_sm, h_ref, rank_ref, xs_hbm, stage, zbuf, sem):
    i = pl.program_id(0)
    ne = N_EXPERTS
    slot = i % 2
    rank = rank_ref[0]
    sub = lax.broadcasted_iota(jnp.int32, (SLOT_ROWS, rank.shape[1]), 0)

    def copy(s, e, off):
        return pltpu.make_async_copy(stage.at[s, pl.ds(e * SLOT_ROWS, SLOT_ROWS)],
                                     xs_hbm.at[e, pl.ds(off, SLOT_ROWS)], sem.at[e])

    def fill(r):
        h = h_ref[...].reshape(rank.shape[1], h_ref.shape[2])
        for e in range(ne):
            onehot = jnp.where(rank[e:e + 1, :] == sub + r * SLOT_ROWS, 1.0, 0.0).astype(BF16)
            stage[slot, e * SLOT_ROWS:(e + 1) * SLOT_ROWS, :] = jnp.dot(
                onehot, h, preferred_element_type=F32).astype(BF16)

    def start_all(r):
        for e in range(ne):
            copy(slot, e, pl.multiple_of(start_sm[i, e] + r * SLOT_ROWS, ROW_ALIGN)).start()

    def wait_all(s):
        for e in range(ne):
            copy(s, e, 0).wait()

    fill(0)

    @pl.when(i > 0)
    def _():
        wait_all(1 - slot)

    start_all(0)

    def extra_round(r, carry):
        wait_all(slot)
        fill(r)
        start_all(r)
        return carry

    lax.fori_loop(1, _num_rounds(cnt_sm, i), extra_round, 0)

    @pl.when(i == pl.num_programs(0) - 1)
    def _zero_tail():
        wait_all(slot)
        zbuf[...] = jnp.zeros(zbuf.shape, zbuf.dtype)

        def zcopy(e):
            off = pl.multiple_of(tot_sm[e], ROW_ALIGN)
            return pltpu.make_async_copy(zbuf, xs_hbm.at[e, pl.ds(off, zbuf.shape[0])], sem.at[e])

        for e in range(ne):
            zcopy(e).start()
        for e in range(ne):
            zcopy(e).wait()


def _gather(h2, rank, start, cnt, tot, cap):
    g, _, d = h2.shape
    nt, ne, tm = rank.shape
    rl = _list_rows(cap, nt)
    return pl.pallas_call(
        _gather_kernel,
        grid_spec=pltpu.PrefetchScalarGridSpec(
            num_scalar_prefetch=3,
            grid=(nt,),
            in_specs=[pl.BlockSpec((g, tm // g, d), lambda i, *_: (0, i, 0)),
                      pl.BlockSpec((1, ne, tm), lambda i, *_: (i, 0, 0))],
            out_specs=pl.BlockSpec(memory_space=pl.ANY),
            scratch_shapes=[pltpu.VMEM((2, ne * SLOT_ROWS, d), BF16),
                            pltpu.VMEM((_expert_block(cap), d), BF16),
                            pltpu.SemaphoreType.DMA((ne,))]),
        out_shape=jax.ShapeDtypeStruct((ne, rl, d), BF16),
        compiler_params=_cparams(("arbitrary",)),
        name="expert_gather",
    )(start, cnt, tot, h2, rank)


def _expert_kernel(tot_sm, xs_ref, wg_ref, wu_ref, wd_ref, ye_ref, wg_bf, wu_bf, wd_bf):
    e = pl.program_id(0)
    j = pl.program_id(1)
    eb = xs_ref.shape[1]

    @pl.when(j == 0)
    def _():
        wg_bf[...] = wg_ref[0].astype(BF16)
        wu_bf[...] = wu_ref[0].astype(BF16)
        wd_bf[...] = wd_ref[0].astype(BF16)

    @pl.when(j * eb < tot_sm[e])
    def _():
        x = xs_ref[0]
        g = jnp.dot(x, wg_bf[...], preferred_element_type=F32)
        u = jnp.dot(x, wu_bf[...], preferred_element_type=F32)
        hid = (_silu(g) * u).astype(BF16)
        ye_ref[0] = jnp.dot(hid, wd_bf[...], preferred_element_type=F32).astype(ye_ref.dtype)


def _experts(xs, tot, used_rows, eb, w_gate, w_up, w_down):
    ne, _, d = xs.shape
    f = w_gate.shape[2]
    last = lambda e, tot: (tot[e] - 1) // eb
    blk = lambda e, j, tot: (e, jnp.minimum(j, last(e, tot)), 0)
    return pl.pallas_call(
        _expert_kernel,
        grid_spec=pltpu.PrefetchScalarGridSpec(
            num_scalar_prefetch=1,
            grid=(ne, used_rows // eb),
            in_specs=[pl.BlockSpec((1, eb, d), blk),
                      pl.BlockSpec((1, d, f), lambda e, j, tot: (e, 0, 0)),
                      pl.BlockSpec((1, d, f), lambda e, j, tot: (e, 0, 0)),
                      pl.BlockSpec((1, f, d), lambda e, j, tot: (e, 0, 0))],
            out_specs=pl.BlockSpec((1, eb, d), blk),
            scratch_shapes=[pltpu.VMEM((d, f), BF16), pltpu.VMEM((d, f), BF16), pltpu.VMEM((f, d), BF16)]),
        out_shape=jax.ShapeDtypeStruct((ne, used_rows, d), BF16),
        compiler_params=_cparams(("arbitrary", "arbitrary")),
        name="expert_ffn",
    )(tot, xs, w_gate, w_up, w_down)


def _combine_kernel(start_sm, cnt_sm, tot_sm, x1_ref, rank_ref, aff_ref, g2_ref, nf_ref, ye_hbm, o_ref, buf, sem, *,
                    final_norm, eb):
    i = pl.program_id(0)
    nt = pl.num_programs(0)
    ne = N_EXPERTS
    slot = i % 2
    rank = rank_ref[0]
    aff = aff_ref[0]
    sub = lax.broadcasted_iota(jnp.int32, (SLOT_ROWS, rank.shape[1]), 0)
    dn_t = (((0,), (0,)), ((), ()))

    def window(t, e, r):
        want = start_sm[t, e] + r * SLOT_ROWS
        written = ((tot_sm[e] + eb - 1) // eb) * eb
        off = jnp.minimum(want, written - SLOT_ROWS)
        return pl.multiple_of(off, ROW_ALIGN), want - off

    def copy(t, s, e, r):
        off, _ = window(t, e, r)
        return pltpu.make_async_copy(ye_hbm.at[e, pl.ds(off, SLOT_ROWS)],
                                     buf.at[s, pl.ds(e * SLOT_ROWS, SLOT_ROWS)], sem.at[s, e])

    def weighted(r):
        parts = []
        for e in range(ne):
            _, shift = window(i, e, r)
            local = rank[e:e + 1, :] - r * SLOT_ROWS
            hit = jnp.where(local >= 0, local + shift, -1) == sub
            parts.append(jnp.where(hit, aff[e:e + 1, :], 0.0).astype(BF16))
        return lax.dot_general(jnp.concatenate(parts, axis=0), buf[slot], dn_t, preferred_element_type=F32)

    @pl.when(i == 0)
    def _():
        for e in range(ne):
            copy(0, 0, e, 0).start()

    @pl.when(i + 1 < nt)
    def _prefetch_next_tile():
        for e in range(ne):
            copy(i + 1, 1 - slot, e, 0).start()

    for e in range(ne):
        copy(i, slot, e, 0).wait()
    y0 = weighted(0)

    def extra_round(r, acc):
        for e in range(ne):
            @pl.when(cnt_sm[i, e] > r * SLOT_ROWS)
            def _():
                copy(i, slot, e, r).start()
        for e in range(ne):
            @pl.when(cnt_sm[i, e] > r * SLOT_ROWS)
            def _():
                copy(i, slot, e, r).wait()
        return acc + weighted(r)

    y = lax.fori_loop(1, _num_rounds(cnt_sm, i), extra_round, y0)
    x2 = x1_ref[...] + g2_ref[...] * y.reshape(x1_ref.shape)
    if final_norm:
        ms = jnp.mean(x2 * x2, axis=-1, keepdims=True)
        x2 = x2 * lax.rsqrt(ms + EPS) * nf_ref[...]
    o_ref[...] = x2


def _combine(x1, rank, aff, start, cnt, tot, g2, norm_f_g, ye, cond_block, eb):
    g, s, d = x1.shape
    nt, ne, tm = rank.shape
    tok = pl.BlockSpec((g, tm // g, d), lambda i, *_: (0, i, 0))
    final_norm = norm_f_g is not None
    if not final_norm:
        norm_f_g = jnp.ones((d,), F32)
    return pl.pallas_call(
        functools.partial(_combine_kernel, final_norm=final_norm, eb=eb),
        grid_spec=pltpu.PrefetchScalarGridSpec(
            num_scalar_prefetch=3,
            grid=(nt,),
            in_specs=[tok,
                      pl.BlockSpec((1, ne, tm), lambda i, *_: (i, 0, 0)),
                      pl.BlockSpec((1, ne, tm), lambda i, *_: (i, 0, 0)),
                      pl.BlockSpec((g, 1, d), lambda i, *_: (cond_block, 0, 0)),
                      pl.BlockSpec((1, d), lambda i, *_: (0, 0)),
                      pl.BlockSpec(memory_space=pl.ANY)],
            out_specs=tok,
            scratch_shapes=[pltpu.VMEM((2, ne * SLOT_ROWS, d), BF16),
                            pltpu.SemaphoreType.DMA((2, ne))]),
        out_shape=jax.ShapeDtypeStruct((g, s, d), F32),
        compiler_params=_cparams(("arbitrary",)),
        name="expert_combine",
    )(start, cnt, tot, x1, rank, aff, g2, norm_f_g.reshape(1, d), ye)


def _trunk_layer(x, batch, seq_len, mods, cond_of_tile, moe_groups, cond_block, p, s0_f, s0_b, rope, emit_state,
                 final_g):
    t, d = x.shape
    sh1, sc1, g1, sh2, sc2, g2 = mods
    u, x2c, qkvg, gates = _in_projection(x, p["norm1_g"], sc1, sh1, p["w_in"], p["hy_conv_w"], p["hy_conv_b"],
                                         cond_of_tile, seq_len, rope)
    seq = lambda a: a.reshape(batch, seq_len, a.shape[-1])
    y_hy = _hyena_long_conv(seq(u), seq(x2c), p).reshape(t, -1)
    ret = _retention(qkvg.reshape(batch, seq_len, -1), p["ret_decay_fwd"], p["ret_decay_bwd"], p["ret_gn_g"],
                     s0_f, s0_b, emit_state)
    y_ret = ret[0].reshape(t, -1)
    view = lambda a: a.reshape(moe_groups, t // moe_groups, a.shape[-1])
    x1, h2, aff = _out_projection(view(y_hy), view(y_ret), view(gates), view(x), g1, sc2, sh2, p["norm2_g"],
                                  p["w_hy_o"], p["w_ret_o"], p["w_out"], p["w_router"], cond_block)
    cap = (EC_CAPACITY_FACTOR * t) // N_EXPERTS
    eb = _expert_block(cap)
    rank, cnt, start, tot = _select(aff, cap, moe_groups)
    xs = _gather(h2, rank, start, cnt, tot, cap)
    ye = _experts(xs, tot, _used_rows(cap, rank.shape[0]), eb, p["w_e_gate"], p["w_e_up"], p["w_e_down"])
    out = _combine(x1, rank, aff, start, cnt, tot, g2, final_g, ye, cond_block, eb)
    return out.reshape(t, d), ret[1:]


def kernel(x_prompt, x_sample, state_ret_fwd, state_ret_bwd, c, c_ctx, w_mod, b_mod, norm1_g, w_in, hy_conv_w, hy_conv_b, hy_f_w1, hy_f_b1, hy_f_freq, hy_f_w2, hy_f_b2, hy_f_w3, hy_decay, hy_bias, ret_decay_fwd, ret_decay_bwd, ret_gn_g, w_hy_o, w_ret_o, w_out, norm2_g, w_router, w_e_gate, w_e_up, w_e_down, norm_f_g):
    b, s, d = x_prompt.shape
    bd, sd, _ = x_sample.shape
    depth = w_mod.shape[0]
    assert (b * s) % TOKEN_TILE == 0 and sd % TOKEN_TILE == 0 and b % 2 == 0 and bd % 2 == 0
    ncond = -(-(bd + 1) // 8) * 8
    lat_groups = bd if (TOKEN_TILE % (16 * bd) == 0 and sd % (TOKEN_TILE // bd) == 0) else 1
    assert lat_groups == bd or bd == 1
    conds = jnp.concatenate([c, c_ctx[None], jnp.zeros((ncond - bd - 1, d), F32)], axis=0)
    ctx_cond = lambda i: bd
    lat_cond = lambda i: i // (sd // TOKEN_TILE)
    xp = x_prompt.reshape(b * s, d)
    xs = x_sample.reshape(bd * sd, d)
    new_f, new_b = [], []
    for l in range(depth):
        mod = _modulation(conds, w_mod[l], b_mod[l])
        mods = [mod[:, k * d:(k + 1) * d].reshape(ncond, 1, d) for k in range(6)]
        p = {"norm1_g": norm1_g[l], "w_in": w_in[l].astype(BF16), "hy_conv_w": hy_conv_w[l],
             "hy_conv_b": hy_conv_b[l], "hy_f_w1": hy_f_w1[l], "hy_f_b1": hy_f_b1[l], "hy_f_freq": hy_f_freq[l],
             "hy_f_w2": hy_f_w2[l], "hy_f_b2": hy_f_b2[l], "hy_f_w3": hy_f_w3[l], "hy_decay": hy_decay[l],
             "hy_bias": hy_bias[l], "ret_decay_fwd": ret_decay_fwd[l], "ret_decay_bwd": ret_decay_bwd[l],
             "ret_gn_g": ret_gn_g[l], "w_hy_o": w_hy_o[l].astype(BF16), "w_ret_o": w_ret_o[l].astype(BF16),
             "w_out": w_out[l].astype(BF16), "norm2_g": norm2_g[l], "w_router": w_router[l],
             "w_e_gate": w_e_gate[l], "w_e_up": w_e_up[l], "w_e_down": w_e_down[l]}
        final_g = norm_f_g if l == depth - 1 else None
        xp, (s_f, s_b) = _trunk_layer(xp, b, s, mods, ctx_cond, 1, bd, p, None, None, False, True, final_g)
        new_f.append(s_f)
        new_b.append(s_b)
        xs, _ = _trunk_layer(xs, bd, sd, mods, lat_cond, lat_groups, 0, p, state_ret_fwd[:, l],
                             state_ret_bwd[:, l], True, False, final_g)
    y_prompt = xp.reshape(b, s, d)
    y_sample = xs.reshape(bd, sd, d)
    return (y_prompt, y_sample, jnp.stack(new_f, axis=1), jnp.stack(new_b, axis=1))
```

```python
import functools
import math

import numpy as np
import jax
import jax.numpy as jnp
from jax import lax
from jax.experimental import pallas as pl
from jax.experimental.pallas import tpu as pltpu

F32 = jnp.float32
BF16 = jnp.bfloat16
HIGHEST = lax.Precision.HIGHEST

EPS = 1e-6
D_HYENA = 512
D_RET = 512
N_RET_HEADS = 4
RET_HEAD_DIM = 128
RET_CHUNK = 128
GRID_W = 64
FILTER_EMB = 33
ROPE_BASE = 10000.0
N_EXPERTS = 16
EC_CAPACITY_FACTOR = 2

TOKEN_TILE = 512
SLOT_ROWS = 96
ROW_ALIGN = 16
HALO_ROWS = 16
FFT_MINOR = 64
FFT_GROUP = 4
VMEM_LIMIT = 56 * 1024 * 1024


def _cparams(sem):
    return pltpu.CompilerParams(dimension_semantics=sem, vmem_limit_bytes=VMEM_LIMIT)


def _silu(x):
    return x * jax.nn.sigmoid(x)


def _mod_kernel(c_ref, w_ref, b_ref, o_ref):
    s = _silu(c_ref[...])
    o_ref[...] = jnp.dot(s, w_ref[...], preferred_element_type=F32, precision=HIGHEST) + b_ref[...]


def _modulation(conds, w_mod, b_mod):
    nc, d = conds.shape
    return pl.pallas_call(
        _mod_kernel,
        grid=(6,),
        in_specs=[pl.BlockSpec((nc, d), lambda j: (0, 0)),
                  pl.BlockSpec((d, d), lambda j: (0, j)),
                  pl.BlockSpec((1, d), lambda j: (0, j))],
        out_specs=pl.BlockSpec((nc, d), lambda j: (0, j)),
        out_shape=jax.ShapeDtypeStruct((nc, 6 * d), F32),
        compiler_params=_cparams(("arbitrary",)),
        name="modulation",
    )(conds, w_mod, b_mod.reshape(1, -1))


def _inproj_kernel(*refs, splits, rope, seq_len, halo):
    refs = list(refs)
    x_ref = refs.pop(0)
    if halo:
        xp_ref, xn_ref = refs.pop(0), refs.pop(0)
    g_ref, sc_ref, sh_ref, w_ref, cw_ref, cb_ref = refs[:6]
    del refs[:6]
    if rope:
        cos_ref, sin_ref = refs.pop(0), refs.pop(0)
    u_ref, x2c_ref, qkvg_ref, gates_ref, cv_sc = refs

    def normed(ref):
        x = ref[...]
        ms = jnp.mean(x * x, axis=-1, keepdims=True)
        h = x * lax.rsqrt(ms + EPS) * g_ref[...]
        return (h * (1.0 + sc_ref[0]) + sh_ref[0]).astype(BF16)

    hb = normed(x_ref)
    tm = hb.shape[0]
    nz, nq, ng = splits
    cw = 512
    dh = RET_HEAD_DIM
    hr = HALO_ROWS
    if halo:
        t0 = pl.program_id(0) * tm
        hp = jnp.where(t0 % seq_len != 0, normed(xp_ref), jnp.zeros((hr, hb.shape[1]), BF16))
        hn = jnp.where((t0 + tm) % seq_len != 0, normed(xn_ref), jnp.zeros((hr, hb.shape[1]), BF16))
        hext = jnp.concatenate([hp, hb, hn], axis=0)

    def conv(c0):
        taps = cw_ref[:, c0:c0 + cw]
        bias = cb_ref[:, c0:c0 + cw]
        if halo:
            acc = jnp.dot(hext, w_ref[:, c0:c0 + cw], preferred_element_type=F32)
            n = tm + 2 * hr
            prev = pltpu.roll(acc, 1, axis=0)[hr:hr + tm]
            nxt = pltpu.roll(acc, n - 1, axis=0)[hr:hr + tm]
            return bias + prev * taps[0:1] + acc[hr:hr + tm] * taps[1:2] + nxt * taps[2:3]
        acc = jnp.dot(hb, w_ref[:, c0:c0 + cw], preferred_element_type=F32)
        row = lax.broadcasted_iota(jnp.int32, (seq_len, 1), 0)
        outs = []
        for s0 in range(0, tm, seq_len):
            x = acc[s0:s0 + seq_len]
            prev = jnp.where(row == 0, 0.0, pltpu.roll(x, 1, axis=0))
            nxt = jnp.where(row == seq_len - 1, 0.0, pltpu.roll(x, seq_len - 1, axis=0))
            outs.append(bias + prev * taps[0:1] + x * taps[1:2] + nxt * taps[2:3])
        return jnp.concatenate(outs, axis=0)

    cv_sc[...] = conv(0)
    u_ref[...] = (cv_sc[...] * conv(D_HYENA)).astype(BF16)
    x2c_ref[...] = conv(2 * D_HYENA).astype(BF16)

    for c0 in range(nz, nz + nq + ng, cw):
        acc = jnp.dot(hb, w_ref[:, c0:c0 + cw], preferred_element_type=F32)
        if c0 < nz + nq:
            part = (c0 - nz) // D_RET
            if part == 1:
                acc = acc * (dh ** -0.5)
            if rope and part < 2:
                lane = lax.broadcasted_iota(jnp.int32, (acc.shape[0], dh), 1)
                swap_hi = (lane % (dh // 2)) < (dh // 4)
                cs, sn = cos_ref[...], sin_ref[...]
                heads = []
                for hh in range(cw // dh):
                    xh = acc[:, hh * dh:(hh + 1) * dh]
                    rot = jnp.where(swap_hi, pltpu.roll(xh, dh - dh // 4, axis=1), pltpu.roll(xh, dh // 4, axis=1))
                    heads.append(xh * cs + rot * sn)
                acc = jnp.concatenate(heads, axis=1)
            qkvg_ref[:, c0 - nz:c0 - nz + cw] = acc.astype(BF16)
        else:
            gates_ref[:, c0 - nz - nq:c0 - nz - nq + cw] = jax.nn.sigmoid(acc).astype(BF16)


def _in_projection(x, norm_g, sc, sh, w_in_bf, conv_w, conv_b, cond_of_tile, seq_len, rope):
    t, d = x.shape
    splits = (3 * D_HYENA, 4 * D_RET, 2 * d)
    assert D_RET == 512 and D_HYENA == 512
    tm = TOKEN_TILE
    assert seq_len % tm == 0 or tm % seq_len == 0
    halo = seq_len > tm
    row = lambda i: (i, 0)
    cond = lambda i: (cond_of_tile(i), 0, 0)
    full = lambda a: pl.BlockSpec(a.shape, lambda i: (0, 0))
    args, specs = [x], [pl.BlockSpec((tm, d), row)]
    if halo:
        hb_per_tile = tm // HALO_ROWS
        last = t // HALO_ROWS - 1
        args += [x, x]
        specs += [pl.BlockSpec((HALO_ROWS, d), lambda i: (jnp.maximum(i * hb_per_tile - 1, 0), 0)),
                  pl.BlockSpec((HALO_ROWS, d), lambda i: (jnp.minimum((i + 1) * hb_per_tile, last), 0))]
    cbias = conv_b.reshape(1, -1)
    args += [norm_g.reshape(1, d), sc, sh, w_in_bf, conv_w, cbias]
    specs += [pl.BlockSpec((1, d), lambda i: (0, 0)), pl.BlockSpec((1, 1, d), cond), pl.BlockSpec((1, 1, d), cond),
              full(w_in_bf), full(conv_w), full(cbias)]
    if rope:
        tiles_per_seq = seq_len // tm
        args += list(_rope_tables(seq_len))
        specs += [pl.BlockSpec((tm, RET_HEAD_DIM), lambda i: (i % tiles_per_seq, 0))] * 2
    return pl.pallas_call(
        functools.partial(_inproj_kernel, splits=splits, rope=rope, seq_len=seq_len, halo=halo),
        grid=(t // tm,),
        in_specs=specs,
        out_specs=[pl.BlockSpec((tm, D_HYENA), row),
                   pl.BlockSpec((tm, D_HYENA), row),
                   pl.BlockSpec((tm, splits[1]), row),
                   pl.BlockSpec((tm, splits[2]), row)],
        out_shape=[jax.ShapeDtypeStruct((t, D_HYENA), BF16),
                   jax.ShapeDtypeStruct((t, D_HYENA), BF16),
                   jax.ShapeDtypeStruct((t, splits[1]), BF16),
                   jax.ShapeDtypeStruct((t, splits[2]), BF16)],
        scratch_shapes=[pltpu.VMEM((tm, 512), F32)],
        compiler_params=_cparams(("parallel",)),
        name="in_projection",
    )(*args)


def _filter_features(seq_len):
    t = np.linspace(0.0, 1.0, seq_len, dtype=np.float32)[:, None]
    bands = (FILTER_EMB - 1) // 2
    w = (np.float32(2.0 * math.pi) * np.arange(seq_len, dtype=np.float32)) / np.float32(seq_len)
    f = np.linspace(1e-4, bands - 1, bands, dtype=np.float32)
    ang = (w[:, None] * f[None, :]).astype(np.float64)
    z = np.concatenate([t, np.cos(ang), -np.sin(ang)], axis=-1).astype(np.float32)
    return np.pad(z, ((0, 0), (0, 128 - FILTER_EMB)))


def _filter_kernel(z_ref, w1_ref, b1_ref, fr_ref, w2_ref, b2_ref, w3_ref, dec_ref, o_ref):
    z = z_ref[...]
    fr = fr_ref[...]
    dot = functools.partial(jnp.dot, preferred_element_type=F32, precision=HIGHEST)
    h = jnp.sin(fr * (dot(z, w1_ref[...]) + b1_ref[...]))
    h = jnp.sin(fr * (dot(h, w2_ref[...]) + b2_ref[...]))
    h = dot(h, w3_ref[...]) * jnp.exp(-z[:, 0:1] * jnp.abs(dec_ref[...]))
    rows = h.shape[0]
    grow = pl.program_id(0) * rows + lax.broadcasted_iota(jnp.int32, (rows, 1), 0)
    o_ref[0] = h[:, :D_HYENA].astype(o_ref.dtype)
    o_ref[1] = jnp.zeros((rows, D_HYENA), o_ref.dtype)
    o_ref[2] = jnp.where(grow == 0, 0.0, h[:, D_HYENA:]).astype(o_ref.dtype)
    o_ref[3] = jnp.zeros((rows, D_HYENA), o_ref.dtype)


def _hyena_filters(seq_len, p, out_dtype):
    z = jnp.asarray(_filter_features(seq_len))
    w1 = jnp.pad(p["hy_f_w1"], ((0, 128 - FILTER_EMB), (0, 0)))
    rows = min(seq_len, 512)
    full = lambda a: pl.BlockSpec(a.shape, lambda i: (0,) * a.ndim)
    ops = [w1, p["hy_f_b1"].reshape(1, -1), p["hy_f_freq"].reshape(1, -1), p["hy_f_w2"],
           p["hy_f_b2"].reshape(1, -1), p["hy_f_w3"], p["hy_decay"].reshape(1, -1)]
    return pl.pallas_call(
        _filter_kernel,
        grid=(seq_len // rows,),
        in_specs=[pl.BlockSpec((rows, 128), lambda i: (i, 0))] + [full(a) for a in ops],
        out_specs=pl.BlockSpec((4, rows, D_HYENA), lambda i: (0, i, 0)),
        out_shape=jax.ShapeDtypeStruct((4, seq_len, D_HYENA), out_dtype),
        compiler_params=_cparams(("arbitrary",)),
        name="hyena_filter",
    )(z, *ops)


def _stacked_dft(n_out, n_in, modulus, scale=1.0):
    k = np.arange(n_out, dtype=np.int64)[:, None]
    n = np.arange(n_in, dtype=np.int64)[None, :]
    th = ((k * n) % modulus) * (2.0 * math.pi / modulus)
    c, s = np.cos(th), np.sin(th)
    fwd = np.block([[c, s], [-s, c]]).astype(np.float32)
    inv = (np.block([[c.T, -s.T], [s.T, c.T]]) * scale).astype(np.float32)
    return fwd, inv


def _second_level_tables(n_total, n1, n2):
    k1 = np.arange(n1, dtype=np.int64)[:, None, None]
    k2 = np.arange(n2, dtype=np.int64)[None, :, None]
    m = np.arange(n2, dtype=np.int64)[None, None, :]
    th = ((m * (k1 + n1 * k2)) % n_total) * (2.0 * math.pi / n_total)
    c, s = np.cos(th), np.sin(th)
    g = np.concatenate([np.concatenate([c, s], axis=2), np.concatenate([-s, c], axis=2)], axis=1)
    return g.astype(np.float32), np.swapaxes(g, 1, 2).astype(np.float32)


def _skewed(n):
    return n + 1


def _fft1_kernel(h_ref, f_ref, a_ref, u_sc, t_sc):
    n2 = FFT_MINOR
    _, seq_len, cb = h_ref.shape
    h1 = seq_len // n2
    nk = a_ref.shape[1]
    up, tp = _skewed(n2), _skewed(nk)
    for b in range(2):
        for q in range(h1):
            u_sc[b, q * up:q * up + n2, :] = h_ref[b, q * n2:(q + 1) * n2, :].astype(F32)

    def fwd(gi, carry):
        m0 = gi * FFT_GROUP
        cols = []
        for d in range(FFT_GROUP):
            xr = u_sc[0, pl.ds(m0 + d, h1, stride=up), :]
            xi = u_sc[1, pl.ds(m0 + d, h1, stride=up), :]
            cols.append(jnp.concatenate([xr, xi], axis=0))
        x = jnp.concatenate(cols, axis=1).astype(BF16)
        res = jnp.dot(f_ref[...], x, preferred_element_type=F32)
        for d in range(FFT_GROUP):
            t_sc[pl.ds((m0 + d) * tp, nk), :] = res[:, d * cb:(d + 1) * cb]
        return carry

    lax.fori_loop(0, n2 // FFT_GROUP, fwd, 0, unroll=2)

    def transpose(k, carry):
        a_ref[0, k] = t_sc[pl.ds(k, n2, stride=tp), :].astype(a_ref.dtype)
        return carry

    lax.fori_loop(0, nk, transpose, 0, unroll=8)


def _fft_first_level(h, f1):
    b, seq_len, c = h.shape
    cb = 128
    return pl.pallas_call(
        _fft1_kernel,
        grid=(b // 2, c // cb),
        in_specs=[pl.BlockSpec((2, seq_len, cb), lambda i, j: (i, 0, j)),
                  pl.BlockSpec(f1.shape, lambda i, j: (0, 0))],
        out_specs=pl.BlockSpec((1, f1.shape[0], FFT_MINOR, cb), lambda i, j: (i, 0, 0, j)),
        out_shape=jax.ShapeDtypeStruct((b // 2, f1.shape[0], FFT_MINOR, c), BF16),
        scratch_shapes=[pltpu.VMEM((2, (seq_len // FFT_MINOR) * _skewed(FFT_MINOR), cb), F32),
                        pltpu.VMEM((FFT_MINOR * _skewed(f1.shape[0]), cb), F32)],
        compiler_params=_cparams(("parallel", "parallel")),
        name="hyena_dft_level1",
    )(h, f1)


def _fft_s2f_kernel(a_ref, g_ref, kr_ref, ki_ref):
    _, _, kb, n2, c = a_ref.shape
    for kk in range(kb):
        g = g_ref[kk]
        hf = jnp.dot(g, a_ref[0, :, kk].reshape(2 * n2, c), preferred_element_type=F32)
        hb = jnp.dot(g, a_ref[1, :, kk].reshape(2 * n2, c), preferred_element_type=F32)
        kr_ref[kk] = hf[:n2] + hb[:n2]
        ki_ref[kk] = hf[n2:] - hb[n2:]


def _fft_filter_second_level(a, g, kb=8):
    _, _, n1, n2, c = a.shape
    spec = pl.BlockSpec((kb, n2, c), lambda i: (i, 0, 0))
    return pl.pallas_call(
        _fft_s2f_kernel,
        grid=(n1 // kb,),
        in_specs=[pl.BlockSpec((2, 2, kb, n2, c), lambda i: (0, 0, i, 0, 0)),
                  pl.BlockSpec((kb, 2 * n2, 2 * n2), lambda i: (i, 0, 0))],
        out_specs=[spec, spec],
        out_shape=[jax.ShapeDtypeStruct((n1, n2, c), F32)] * 2,
        compiler_params=_cparams(("parallel",)),
        name="hyena_filter_spectrum",
    )(a, g)


def _fft_s2_kernel(a_ref, g_ref, gt_ref, kr_ref, ki_ref, b_ref, r_sc):
    npairs, _, kb, n2, cb = a_ref.shape
    nq = cb // 128
    rp = _skewed(2 * n2)
    for kk in range(kb):
        a = jnp.concatenate([a_ref[p, :, kk].reshape(2 * n2, cb) for p in range(npairs)], axis=1)
        x = jnp.dot(g_ref[kk], a, preferred_element_type=F32)
        xr, xi = x[:n2], x[n2:]
        kr = jnp.concatenate([kr_ref[kk]] * npairs, axis=1)
        ki = jnp.concatenate([ki_ref[kk]] * npairs, axis=1)
        y = jnp.concatenate([xr * kr - xi * ki, xr * ki + xi * kr], axis=0).astype(BF16)
        res = jnp.dot(gt_ref[kk], y, preferred_element_type=F32)
        for q in range(npairs * nq):
            r_sc[q, kk * rp:kk * rp + 2 * n2, :] = res[:, q * 128:(q + 1) * 128]

    for p in range(npairs):
        def gather(row):
            return jnp.concatenate([r_sc[p * nq + q, pl.ds(row, kb, stride=rp), :] for q in range(nq)], axis=1)

        def transpose(m, carry):
            b_ref[p, 0, m] = gather(m).astype(b_ref.dtype)
            b_ref[p, 1, m] = gather(n2 + m).astype(b_ref.dtype)
            return carry

        lax.fori_loop(0, n2, transpose, 0, unroll=8)


def _fft_second_level(a, g, gt, kr, ki):
    p, _, n1, n2, c = a.shape
    kb, cb = 16, 256
    gspec = pl.BlockSpec((kb, 2 * n2, 2 * n2), lambda i, j: (i, 0, 0))
    kspec = pl.BlockSpec((kb, n2, cb), lambda i, j: (i, 0, j))
    return pl.pallas_call(
        _fft_s2_kernel,
        grid=(n1 // kb, c // cb),
        in_specs=[pl.BlockSpec((p, 2, kb, n2, cb), lambda i, j: (0, 0, i, 0, j)), gspec, gspec, kspec, kspec],
        out_specs=pl.BlockSpec((p, 2, n2, kb, cb), lambda i, j: (0, 0, 0, i, j)),
        out_shape=jax.ShapeDtypeStruct((p, 2, n2, n1, c), BF16),
        scratch_shapes=[pltpu.VMEM((p * cb // 128, kb * _skewed(2 * n2), 128), F32)],
        compiler_params=_cparams(("parallel", "parallel")),
        name="hyena_dft_level2",
    )(a, g, gt, kr, ki)


def _fft_s3_kernel(b_ref, f_ref, u_ref, x2_ref, bias_ref, o_ref, t_sc):
    _, _, n2, n1, cb = b_ref.shape
    nr = f_ref.shape[0]
    h1 = nr // 2
    tp = _skewed(nr)

    def inv(gi, carry):
        m0 = gi * FFT_GROUP
        x = jnp.concatenate([b_ref[0, :, m0 + d].reshape(2 * n1, cb) for d in range(FFT_GROUP)], axis=1)
        res = jnp.dot(f_ref[...], x, preferred_element_type=F32)
        for d in range(FFT_GROUP):
            t_sc[pl.ds((m0 + d) * tp, nr), :] = res[:, d * cb:(d + 1) * cb]
        return carry

    lax.fori_loop(0, n2 // FFT_GROUP, inv, 0, unroll=2)

    bias = bias_ref[...]
    for b in range(2):
        def finish(q, carry):
            rows = pl.ds(pl.multiple_of(q * n2, n2), n2)
            conv = t_sc[pl.ds(b * h1 + q, n2, stride=tp), :]
            u = u_ref[b, rows, :].astype(F32)
            o_ref[b, rows, :] = ((conv + u * bias) * x2_ref[b, rows, :].astype(F32)).astype(o_ref.dtype)
            return carry

        lax.fori_loop(0, h1, finish, 0, unroll=4)


def _fft_last_level(bt, f1inv, u, x2c, bias):
    p, _, n2, n1, c = bt.shape
    b, seq_len, _ = u.shape
    cb = 128
    uspec = pl.BlockSpec((2, seq_len, cb), lambda i, j: (i, 0, j))
    return pl.pallas_call(
        _fft_s3_kernel,
        grid=(p, c // cb),
        in_specs=[pl.BlockSpec((1, 2, n2, n1, cb), lambda i, j: (i, 0, 0, 0, j)),
                  pl.BlockSpec(f1inv.shape, lambda i, j: (0, 0)),
                  uspec, uspec,
                  pl.BlockSpec((1, cb), lambda i, j: (0, j))],
        out_specs=uspec,
        out_shape=jax.ShapeDtypeStruct(u.shape, BF16),
        scratch_shapes=[pltpu.VMEM((n2 * _skewed(f1inv.shape[0]), cb), F32)],
        compiler_params=_cparams(("parallel", "parallel")),
        name="hyena_dft_inverse",
    )(bt, f1inv, u, x2c, bias)


def _kf_direct_kernel(h_ref, f_ref, kr_ref, ki_ref):
    n = kr_ref.shape[0]
    dot = functools.partial(jnp.dot, preferred_element_type=F32, precision=HIGHEST)
    hf = dot(f_ref[...], h_ref[0])
    hb = dot(f_ref[...], h_ref[2])
    kr_ref[...] = hf[:n] + hb[:n]
    ki_ref[...] = hf[n:] - hb[n:]


def _hyena_direct_kernel(u_ref, x2_ref, f_ref, fi_ref, kr_ref, ki_ref, bias_ref, o_ref):
    _, two, seq_len, cb = u_ref.shape
    n = kr_ref.shape[0]
    ub = u_ref[0].reshape(two * seq_len, cb)
    x = jnp.dot(f_ref[...], ub, preferred_element_type=F32)
    xr, xi = x[:n], x[n:]
    kr, ki = kr_ref[...], ki_ref[...]
    y = jnp.concatenate([xr * kr - xi * ki, xr * ki + xi * kr], axis=0).astype(BF16)
    conv = jnp.dot(fi_ref[...], y, preferred_element_type=F32)
    x2 = x2_ref[0].reshape(two * seq_len, cb).astype(F32)
    o = (conv + ub.astype(F32) * bias_ref[...]) * x2
    o_ref[0] = o.reshape(two, seq_len, cb).astype(o_ref.dtype)


def _hyena_long_conv(u, x2c, p):
    b, seq_len, _ = u.shape
    c = D_HYENA
    n = 2 * seq_len
    bias = p["hy_bias"].reshape(1, c)
    hh = _hyena_filters(seq_len, p, F32)
    if seq_len <= 512:
        fwd, inv = _stacked_dft(n, seq_len, n, scale=1.0 / n)
        kr, ki = pl.pallas_call(
            _kf_direct_kernel,
            out_shape=[jax.ShapeDtypeStruct((n, c), F32)] * 2,
            compiler_params=_cparams(None),
            name="hyena_filter_spectrum_direct",
        )(hh, fwd[:, :seq_len])
        cb = 256
        pair = lambda a: a.reshape(b // 2, 2, seq_len, c)
        uspec = pl.BlockSpec((1, 2, seq_len, cb), lambda i, j: (i, 0, 0, j))
        kspec = pl.BlockSpec((n, cb), lambda i, j: (0, j))
        out = pl.pallas_call(
            _hyena_direct_kernel,
            grid=(b // 2, c // cb),
            in_specs=[uspec, uspec,
                      pl.BlockSpec(fwd.shape, lambda i, j: (0, 0)),
                      pl.BlockSpec(inv.shape, lambda i, j: (0, 0)),
                      kspec, kspec,
                      pl.BlockSpec((1, cb), lambda i, j: (0, j))],
            out_specs=uspec,
            out_shape=jax.ShapeDtypeStruct((b // 2, 2, seq_len, c), BF16),
            compiler_params=_cparams(("parallel", "parallel")),
            name="hyena_dft_direct",
        )(pair(u), pair(x2c), jnp.asarray(fwd).astype(BF16), jnp.asarray(inv).astype(BF16), kr, ki, bias)
        return out.reshape(b, seq_len, c)

    n2 = FFT_MINOR
    n1 = n // n2
    h1 = seq_len // n2
    f1, f1inv = _stacked_dft(n1, h1, n1, scale=1.0 / n)
    f1, f1inv = jnp.asarray(f1).astype(BF16), jnp.asarray(f1inv).astype(BF16)
    g, gt = _second_level_tables(n, n1, n2)
    g, gt = jnp.asarray(g).astype(BF16), jnp.asarray(gt).astype(BF16)
    ha = _fft_first_level(hh, f1)
    kr, ki = _fft_filter_second_level(ha.reshape(2, 2, n1, n2, c), g)
    a = _fft_first_level(u, f1)
    bt = _fft_second_level(a.reshape(b // 2, 2, n1, n2, c), g, gt, kr, ki)
    return _fft_last_level(bt, f1inv, u, x2c, bias)


def _rope_tables(seq_len):
    half = RET_HEAD_DIM // 2
    nf = half // 2
    t = np.arange(seq_len)
    inv = ROPE_BASE ** (-np.arange(nf, dtype=np.float64) / nf)
    ar = (t // GRID_W)[:, None] * inv[None, :]
    ac = (t % GRID_W)[:, None] * inv[None, :]
    cos = np.concatenate([np.cos(ar), np.cos(ar), np.cos(ac), np.cos(ac)], axis=-1)
    sin = np.concatenate([-np.sin(ar), np.sin(ar), -np.sin(ac), np.sin(ac)], axis=-1)
    return cos.astype(np.float32), sin.astype(np.float32)


def _log_sigmoid(x):
    return jnp.minimum(x, 0.0) - jnp.log1p(jnp.exp(-jnp.abs(x)))


def _retention_kernel(*refs, has_init, emit_state, cpb):
    refs = list(refs)
    q_ref, k_ref, v_ref, g_ref, dec_ref, gn_ref = refs[:6]
    del refs[:6]
    if has_init:
        s0f_ref, s0b_ref = refs[:2]
        del refs[:2]
    o_ref = refs.pop(0)
    if emit_state:
        sf_out, sb_out = refs[:2]
        del refs[:2]
    sf_ref, sb_ref, sball_ref = refs

    c = RET_CHUNK
    dh = RET_HEAD_DIM
    nh = N_RET_HEADS
    phase = pl.program_id(1)
    j = pl.program_id(2)
    nb = pl.num_programs(2)

    ri = lax.broadcasted_iota(jnp.int32, (c, c), 0).astype(F32)
    ci = lax.broadcasted_iota(jnp.int32, (c, c), 1).astype(F32)
    diff = ri - ci

    def head_consts(h):
        lgf = _log_sigmoid(dec_ref[0, h])[0:1, :]
        lgb = _log_sigmoid(dec_ref[1, h])[0:1, :]
        return lgf, lgb

    def load(ref, r0, h):
        return ref[0, r0:r0 + c, h * dh:(h + 1) * dh]

    dn_t = (((0,), (0,)), ((), ()))
    dn_nt = (((1,), (1,)), ((), ()))

    @pl.when(phase == 0)
    def _backward_sweep():
        @pl.when(j == 0)
        def _():
            for h in range(nh):
                sb_ref[h] = s0b_ref[0, h] if has_init else jnp.zeros((dh, dh), F32)

        blk = nb - 1 - j
        for h in range(nh):
            _, lgb = head_consts(h)
            zeta_b = jnp.exp(lgb * ri)
            cdec_b = jnp.exp(lgb * float(c))
            for cc in reversed(range(cpb)):
                r0 = cc * c
                n = blk * cpb + cc
                s = sb_ref[h]
                sball_ref[n, h] = s.astype(BF16)
                kz = (load(k_ref, r0, h).astype(F32) * zeta_b).astype(BF16)
                vv = load(v_ref, r0, h)
                sb_ref[h] = cdec_b * s + lax.dot_general(kz, vv, dn_t, preferred_element_type=F32)

        if emit_state:
            @pl.when(j == nb - 1)
            def _():
                for h in range(nh):
                    sb_out[0, h] = sb_ref[h]

    @pl.when(phase == 1)
    def _forward_sweep():
        @pl.when(j == 0)
        def _():
            for h in range(nh):
                sf_ref[h] = s0f_ref[0, h] if has_init else jnp.zeros((dh, dh), F32)

        for h in range(nh):
            lgf, lgb = head_consts(h)
            mask = (jnp.where(diff >= 0, jnp.exp(lgf * jnp.maximum(diff, 0.0)), 0.0)
                    + jnp.where(diff <= 0, jnp.exp(lgb * jnp.maximum(-diff, 0.0)), 0.0))
            xi_f = jnp.exp(lgf * (ri + 1.0))
            xi_b = jnp.exp(lgb * (float(c) - ri))
            zeta_f = jnp.exp(lgf * (float(c - 1) - ri))
            cdec_f = jnp.exp(lgf * float(c))
            gn = gn_ref[:, h * dh:(h + 1) * dh]
            for cc in range(cpb):
                r0 = cc * c
                n = j * cpb + cc
                qb = load(q_ref, r0, h)
                kb = load(k_ref, r0, h)
                vv = load(v_ref, r0, h)
                gate = load(g_ref, r0, h).astype(F32)
                sc = lax.dot_general(qb, kb, dn_nt, preferred_element_type=F32)
                inner = jnp.dot((sc * mask).astype(BF16), vv, preferred_element_type=F32)
                s = sf_ref[h]
                q = qb.astype(F32)
                lhs = jnp.concatenate([q * xi_f, q * xi_b], axis=1).astype(BF16)
                rhs = jnp.concatenate([s.astype(BF16), sball_ref[n, h]], axis=0)
                o = inner + jnp.dot(lhs, rhs, preferred_element_type=F32)
                mu = jnp.mean(o, axis=-1, keepdims=True)
                d = o - mu
                var = jnp.mean(d * d, axis=-1, keepdims=True)
                y = d * lax.rsqrt(var + EPS) * gn * _silu(gate)
                o_ref[0, r0:r0 + c, h * dh:(h + 1) * dh] = y.astype(o_ref.dtype)
                kz = (kb.astype(F32) * zeta_f).astype(BF16)
                sf_ref[h] = cdec_f * s + lax.dot_general(kz, vv, dn_t, preferred_element_type=F32)

        if emit_state:
            @pl.when(j == nb - 1)
            def _():
                for h in range(nh):
                    sf_out[0, h] = sf_ref[h]


def _retention(qkvg, dec_f, dec_b, gn_g, s0_f, s0_b, emit_state):
    b, seq_len, _ = qkvg.shape
    nh, dh, c = N_RET_HEADS, RET_HEAD_DIM, RET_CHUNK
    rb = min(seq_len, 1024)
    nb = seq_len // rb
    cpb = rb // c
    has_init = s0_f is not None
    dec = jnp.broadcast_to(jnp.stack([dec_f, dec_b])[:, :, None, None], (2, nh, 8, 128)).astype(F32)
    kv_blk = lambda i, p, j: jnp.where(p == 0, nb - 1 - j, j)
    q_blk = lambda i, p, j: jnp.where(p == 0, 0, j)
    in_specs = [pl.BlockSpec((1, rb, D_RET), lambda i, p, j: (i, q_blk(i, p, j), 0)),
                pl.BlockSpec((1, rb, D_RET), lambda i, p, j: (i, kv_blk(i, p, j), 1)),
                pl.BlockSpec((1, rb, D_RET), lambda i, p, j: (i, kv_blk(i, p, j), 2)),
                pl.BlockSpec((1, rb, D_RET), lambda i, p, j: (i, q_blk(i, p, j), 3)),
                pl.BlockSpec((2, nh, 8, 128), lambda i, p, j: (0, 0, 0, 0)),
                pl.BlockSpec((1, D_RET), lambda i, p, j: (0, 0))]
    args = [qkvg, qkvg, qkvg, qkvg, dec, gn_g.reshape(1, -1)]
    sspec = pl.BlockSpec((1, nh, dh, dh), lambda i, p, j: (i, 0, 0, 0))
    if has_init:
        in_specs += [sspec, sspec]
        args += [s0_f, s0_b]
    out_specs = [pl.BlockSpec((1, rb, D_RET), lambda i, p, j: (i, q_blk(i, p, j), 0))]
    out_shape = [jax.ShapeDtypeStruct((b, seq_len, D_RET), BF16)]
    if emit_state:
        out_specs += [sspec, sspec]
        out_shape += [jax.ShapeDtypeStruct((b, nh, dh, dh), F32)] * 2
    return pl.pallas_call(
        functools.partial(_retention_kernel, has_init=has_init, emit_state=emit_state, cpb=cpb),
        grid=(b, 2, nb),
        in_specs=in_specs,
        out_specs=out_specs,
        out_shape=out_shape,
        scratch_shapes=[pltpu.VMEM((nh, dh, dh), F32), pltpu.VMEM((nh, dh, dh), F32),
                        pltpu.VMEM((nb * cpb, nh, dh, dh), BF16)],
        compiler_params=_cparams(("parallel", "arbitrary", "arbitrary")),
        name="retention",
    )(*args)


def _outproj_kernel(yhy_ref, yret_ref, gates_ref, x_ref, g1_ref, sc_ref, sh_ref, ng_ref,
                    why_ref, wret_ref, wout_ref, wr_ref, x1_ref, h2_ref, aff_ref):
    g, m, d = x_ref.shape
    rows = lambda ref: ref[...].reshape(g * m, ref.shape[2])
    a = jnp.dot(rows(yhy_ref), why_ref[...], preferred_element_type=F32)
    b = jnp.dot(rows(yret_ref), wret_ref[...], preferred_element_type=F32)
    gates = rows(gates_ref)
    merged = gates[:, :d].astype(F32) * a + gates[:, d:].astype(F32) * b
    out = jnp.dot(merged.astype(BF16), wout_ref[...], preferred_element_type=F32)
    x1 = x_ref[...] + g1_ref[...] * out.reshape(g, m, d)
    x1_ref[...] = x1
    ms = jnp.mean(x1 * x1, axis=-1, keepdims=True)
    h = x1 * lax.rsqrt(ms + EPS) * ng_ref[...]
    h = (h * (1.0 + sc_ref[...]) + sh_ref[...]).reshape(g * m, d)
    h_hi = h.astype(BF16)
    h2_ref[...] = h_hi.reshape(g, m, d)
    h_lo = (h - h_hi.astype(F32)).astype(BF16)
    t = jnp.dot(h_hi, wr_ref[...], preferred_element_type=F32)
    logits = t[:, :128] + t[:, 128:] + jnp.dot(h_lo, wr_ref[:, :128], preferred_element_type=F32)
    lane = lax.broadcasted_iota(jnp.int32, logits.shape, 1)
    logits = jnp.where(lane < N_EXPERTS, logits, -jnp.inf)
    e = jnp.exp(logits - jnp.max(logits, axis=-1, keepdims=True))
    aff = e / jnp.sum(e, axis=-1, keepdims=True)
    aff_ref[0] = aff.T[:N_EXPERTS, :]


def _out_projection(y_hy, y_ret, gates, x, g1, sc2, sh2, norm2_g, w_hy_o, w_ret_o, w_out, w_router, cond_block):
    g, s, d = x.shape
    tm = TOKEN_TILE
    m = tm // g
    tok = lambda c: pl.BlockSpec((g, m, c), lambda i: (0, i, 0))
    cond = pl.BlockSpec((g, 1, d), lambda i: (cond_block, 0, 0))
    full = lambda a: pl.BlockSpec(a.shape, lambda i: (0, 0))
    wr = jnp.pad(w_router, ((0, 0), (0, 128 - N_EXPERTS)))
    wr_hi = wr.astype(BF16)
    wr = jnp.concatenate([wr_hi, (wr - wr_hi.astype(F32)).astype(BF16)], axis=1)
    return pl.pallas_call(
        _outproj_kernel,
        grid=(s // m,),
        in_specs=[tok(D_HYENA), tok(D_RET), tok(2 * d), tok(d), cond, cond, cond,
                  pl.BlockSpec((1, d), lambda i: (0, 0)),
                  full(w_hy_o), full(w_ret_o), full(w_out), full(wr)],
        out_specs=[tok(d), tok(d), pl.BlockSpec((1, N_EXPERTS, tm), lambda i: (i, 0, 0))],
        out_shape=[jax.ShapeDtypeStruct((g, s, d), F32), jax.ShapeDtypeStruct((g, s, d), BF16),
                   jax.ShapeDtypeStruct((s // m, N_EXPERTS, tm), F32)],
        compiler_params=_cparams(("parallel",)),
        name="out_projection_router",
    )(y_hy, y_ret, gates, x, g1, sc2, sh2, norm2_g.reshape(1, d), w_hy_o, w_ret_o, w_out, wr)


def _select_kernel(aff_ref, tri_ref, rank_ref, cnt_ref, start_ref, tot_ref, *, cap, idx_bits, groups):
    nt, ne, tm = aff_ref.shape
    a = aff_ref[...]

    def count(m):
        return jnp.sum(jnp.sum(m, axis=0, keepdims=True), axis=2, keepdims=True)

    def thr_step(s, thr):
        cand = thr | (1 << (30 - s))
        cnt = count(jnp.where(a >= pltpu.bitcast(cand, F32), 1.0, 0.0))
        return jnp.where(cnt >= float(cap), cand, thr)

    thr = pltpu.bitcast(lax.fori_loop(0, 31, thr_step, jnp.zeros((1, ne, 1), jnp.int32)), F32)
    gt = a > thr
    eq = a == thr
    need = float(cap) - count(jnp.where(gt, 1.0, 0.0))
    m = tm // groups
    tile = lax.broadcasted_iota(jnp.int32, (nt, 1, tm), 0)
    lane = lax.broadcasted_iota(jnp.int32, (nt, 1, tm), 2)
    idx = (lane // m) * (nt * m) + tile * m + lane % m

    def idx_step(s, lim):
        cand = lim | (1 << (idx_bits - 1 - s))
        cnt = count(jnp.where(eq, jnp.where(idx < cand, 1.0, 0.0), 0.0))
        return jnp.where(cnt < need, cand, lim)

    lim = lax.fori_loop(0, idx_bits, idx_step, jnp.zeros((1, ne, 1), jnp.int32))
    sel = jnp.where(gt, 1.0, jnp.where(eq, jnp.where(idx <= lim, 1.0, 0.0), 0.0))
    sel2 = sel.reshape(nt * ne, tm).astype(BF16)
    prefix = jnp.dot(sel2, tri_ref[...], preferred_element_type=F32)
    rank = jnp.where(sel2 > 0, prefix, -1.0).astype(jnp.int32)
    rank_ref[...] = rank.reshape(nt, ne, tm)
    ones = jnp.ones((tm, 128), BF16)
    cnt = jnp.dot(sel2, ones, preferred_element_type=F32).astype(jnp.int32).reshape(nt, ne, 128)
    cnt_ref[...] = cnt
    acc = jnp.zeros((ne, 128), jnp.int32)
    for t in range(nt):
        start_ref[t] = acc
        acc = acc + ((cnt[t] + (ROW_ALIGN - 1)) & (-ROW_ALIGN))
    tot_ref[...] = acc


def _select(aff, cap, groups):
    nt, ne, tm = aff.shape
    idx_bits = max(1, int(math.ceil(math.log2(nt * tm))))
    r = lax.broadcasted_iota(jnp.int32, (tm, tm), 0)
    c = lax.broadcasted_iota(jnp.int32, (tm, tm), 1)
    tri = (r < c).astype(BF16)
    rank, cnt, start, tot = pl.pallas_call(
        functools.partial(_select_kernel, cap=cap, idx_bits=idx_bits, groups=groups),
        out_shape=[jax.ShapeDtypeStruct((nt, ne, tm), jnp.int32),
                   jax.ShapeDtypeStruct((nt, ne, 128), jnp.int32),
                   jax.ShapeDtypeStruct((nt, ne, 128), jnp.int32),
                   jax.ShapeDtypeStruct((ne, 128), jnp.int32)],
        compiler_params=_cparams(None),
        name="expert_choice_select",
    )(aff, tri)
    return rank, cnt[:, :, 0], start[:, :, 0], tot[:, 0]


def _expert_block(cap):
    return 512 if cap >= 2048 else 256


def _used_rows(cap, nt):
    eb = _expert_block(cap)
    return -(-(cap + ROW_ALIGN * nt) // eb) * eb


def _list_rows(cap, nt):
    max_rounds = -(-TOKEN_TILE // SLOT_ROWS)
    return _used_rows(cap, nt) + max(_expert_block(cap), max_rounds * SLOT_ROWS)


def _num_rounds(cnt_sm, i):
    m = cnt_sm[i, 0]
    for e in range(1, N_EXPERTS):
        m = jnp.maximum(m, cnt_sm[i, e])
    return jnp.maximum((m + SLOT_ROWS - 1) // SLOT_ROWS, 1)


def _gather_kernel(start_sm, cnt_sm, tot_sm, h_ref, rank_ref, xs_hbm, stage, zbuf, sem):
    i = pl.program_id(0)
    ne = N_EXPERTS
    slot = i % 2
    rank = rank_ref[0]
    sub = lax.broadcasted_iota(jnp.int32, (SLOT_ROWS, rank.shape[1]), 0)

    def copy(s, e, off):
        return pltpu.make_async_copy(stage.at[s, pl.ds(e * SLOT_ROWS, SLOT_ROWS)],
                                     xs_hbm.at[e, pl.ds(off, SLOT_ROWS)], sem.at[e])

    def fill(r):
        h = h_ref[...].reshape(rank.shape[1], h_ref.shape[2])
        for e in range(ne):
            onehot = jnp.where(rank[e:e + 1, :] == sub + r * SLOT_ROWS, 1.0, 0.0).astype(BF16)
            stage[slot, e * SLOT_ROWS:(e + 1) * SLOT_ROWS, :] = jnp.dot(
                onehot, h, preferred_element_type=F32).astype(BF16)

    def start_all(r):
        for e in range(ne):
            copy(slot, e, pl.multiple_of(start_sm[i, e] + r * SLOT_ROWS, ROW_ALIGN)).start()

    def wait_all(s):
        for e in range(ne):
            copy(s, e, 0).wait()

    fill(0)

    @pl.when(i > 0)
    def _():
        wait_all(1 - slot)

    start_all(0)

    def extra_round(r, carry):
        wait_all(slot)
        fill(r)
        start_all(r)
        return carry

    lax.fori_loop(1, _num_rounds(cnt_sm, i), extra_round, 0)

    @pl.when(i == pl.num_programs(0) - 1)
    def _zero_tail():
        wait_all(slot)
        zbuf[...] = jnp.zeros(zbuf.shape, zbuf.dtype)

        def zcopy(e):
            off = pl.multiple_of(tot_sm[e], ROW_ALIGN)
            return pltpu.make_async_copy(zbuf, xs_hbm.at[e, pl.ds(off, zbuf.shape[0])], sem.at[e])

        for e in range(ne):
            zcopy(e).start()
        for e in range(ne):
            zcopy(e).wait()


def _gather(h2, rank, start, cnt, tot, cap):
    g, _, d = h2.shape
    nt, ne, tm = rank.shape
    rl = _list_rows(cap, nt)
    return pl.pallas_call(
        _gather_kernel,
        grid_spec=pltpu.PrefetchScalarGridSpec(
            num_scalar_prefetch=3,
            grid=(nt,),
            in_specs=[pl.BlockSpec((g, tm // g, d), lambda i, *_: (0, i, 0)),
                      pl.BlockSpec((1, ne, tm), lambda i, *_: (i, 0, 0))],
            out_specs=pl.BlockSpec(memory_space=pl.ANY),
            scratch_shapes=[pltpu.VMEM((2, ne * SLOT_ROWS, d), BF16),
                            pltpu.VMEM((_expert_block(cap), d), BF16),
                            pltpu.SemaphoreType.DMA((ne,))]),
        out_shape=jax.ShapeDtypeStruct((ne, rl, d), BF16),
        compiler_params=_cparams(("arbitrary",)),
        name="expert_gather",
    )(start, cnt, tot, h2, rank)


def _expert_kernel(tot_sm, xs_ref, wg_ref, wu_ref, wd_ref, ye_ref, wg_bf, wu_bf, wd_bf):
    e = pl.program_id(0)
    j = pl.program_id(1)
    eb = xs_ref.shape[1]

    @pl.when(j == 0)
    def _():
        wg_bf[...] = wg_ref[0].astype(BF16)
        wu_bf[...] = wu_ref[0].astype(BF16)
        wd_bf[...] = wd_ref[0].astype(BF16)

    @pl.when(j * eb < tot_sm[e])
    def _():
        x = xs_ref[0]
        g = jnp.dot(x, wg_bf[...], preferred_element_type=F32)
        u = jnp.dot(x, wu_bf[...], preferred_element_type=F32)
        hid = (_silu(g) * u).astype(BF16)
        ye_ref[0] = jnp.dot(hid, wd_bf[...], preferred_element_type=F32).astype(ye_ref.dtype)


def _experts(xs, tot, used_rows, eb, w_gate, w_up, w_down):
    ne, _, d = xs.shape
    f = w_gate.shape[2]
    last = lambda e, tot: (tot[e] - 1) // eb
    blk = lambda e, j, tot: (e, jnp.minimum(j, last(e, tot)), 0)
    return pl.pallas_call(
        _expert_kernel,
        grid_spec=pltpu.PrefetchScalarGridSpec(
            num_scalar_prefetch=1,
            grid=(ne, used_rows // eb),
            in_specs=[pl.BlockSpec((1, eb, d), blk),
                      pl.BlockSpec((1, d, f), lambda e, j, tot: (e, 0, 0)),
                      pl.BlockSpec((1, d, f), lambda e, j, tot: (e, 0, 0)),
                      pl.BlockSpec((1, f, d), lambda e, j, tot: (e, 0, 0))],
            out_specs=pl.BlockSpec((1, eb, d), blk),
            scratch_shapes=[pltpu.VMEM((d, f), BF16), pltpu.VMEM((d, f), BF16), pltpu.VMEM((f, d), BF16)]),
        out_shape=jax.ShapeDtypeStruct((ne, used_rows, d), BF16),
        compiler_params=_cparams(("arbitrary", "arbitrary")),
        name="expert_ffn",
    )(tot, xs, w_gate, w_up, w_down)


def _combine_kernel(start_sm, cnt_sm, tot_sm, x1_ref, rank_ref, aff_ref, g2_ref, nf_ref, ye_hbm, o_ref, buf, sem, *,
                    final_norm, eb):
    i = pl.program_id(0)
    nt = pl.num_programs(0)
    ne = N_EXPERTS
    slot = i % 2
    rank = rank_ref[0]
    aff = aff_ref[0]
    sub = lax.broadcasted_iota(jnp.int32, (SLOT_ROWS, rank.shape[1]), 0)
    dn_t = (((0,), (0,)), ((), ()))

    def window(t, e, r):
        want = start_sm[t, e] + r * SLOT_ROWS
        written = ((tot_sm[e] + eb - 1) // eb) * eb
        off = jnp.minimum(want, written - SLOT_ROWS)
        return pl.multiple_of(off, ROW_ALIGN), want - off

    def copy(t, s, e, r):
        off, _ = window(t, e, r)
        return pltpu.make_async_copy(ye_hbm.at[e, pl.ds(off, SLOT_ROWS)],
                                     buf.at[s, pl.ds(e * SLOT_ROWS, SLOT_ROWS)], sem.at[s, e])

    def weighted(r):
        parts = []
        for e in range(ne):
            _, shift = window(i, e, r)
            local = rank[e:e + 1, :] - r * SLOT_ROWS
            hit = jnp.where(local >= 0, local + shift, -1) == sub
            parts.append(jnp.where(hit, aff[e:e + 1, :], 0.0).astype(BF16))
        return lax.dot_general(jnp.concatenate(parts, axis=0), buf[slot], dn_t, preferred_element_type=F32)

    @pl.when(i == 0)
    def _():
        for e in range(ne):
            copy(0, 0, e, 0).start()

    @pl.when(i + 1 < nt)
    def _prefetch_next_tile():
        for e in range(ne):
            copy(i + 1, 1 - slot, e, 0).start()

    for e in range(ne):
        copy(i, slot, e, 0).wait()
    y0 = weighted(0)

    def extra_round(r, acc):
        for e in range(ne):
            @pl.when(cnt_sm[i, e] > r * SLOT_ROWS)
            def _():
                copy(i, slot, e, r).start()
        for e in range(ne):
            @pl.when(cnt_sm[i, e] > r * SLOT_ROWS)
            def _():
                copy(i, slot, e, r).wait()
        return acc + weighted(r)

    y = lax.fori_loop(1, _num_rounds(cnt_sm, i), extra_round, y0)
    x2 = x1_ref[...] + g2_ref[...] * y.reshape(x1_ref.shape)
    if final_norm:
        ms = jnp.mean(x2 * x2, axis=-1, keepdims=True)
        x2 = x2 * lax.rsqrt(ms + EPS) * nf_ref[...]
    o_ref[...] = x2


def _combine(x1, rank, aff, start, cnt, tot, g2, norm_f_g, ye, cond_block, eb):
    g, s, d = x1.shape
    nt, ne, tm = rank.shape
    tok = pl.BlockSpec((g, tm // g, d), lambda i, *_: (0, i, 0))
    final_norm = norm_f_g is not None
    if not final_norm:
        norm_f_g = jnp.ones((d,), F32)
    return pl.pallas_call(
        functools.partial(_combine_kernel, final_norm=final_norm, eb=eb),
        grid_spec=pltpu.PrefetchScalarGridSpec(
            num_scalar_prefetch=3,
            grid=(nt,),
            in_specs=[tok,
                      pl.BlockSpec((1, ne, tm), lambda i, *_: (i, 0, 0)),
                      pl.BlockSpec((1, ne, tm), lambda i, *_: (i, 0, 0)),
                      pl.BlockSpec((g, 1, d), lambda i, *_: (cond_block, 0, 0)),
                      pl.BlockSpec((1, d), lambda i, *_: (0, 0)),
                      pl.BlockSpec(memory_space=pl.ANY)],
            out_specs=tok,
            scratch_shapes=[pltpu.VMEM((2, ne * SLOT_ROWS, d), BF16),
                            pltpu.SemaphoreType.DMA((2, ne))]),
        out_shape=jax.ShapeDtypeStruct((g, s, d), F32),
        compiler_params=_cparams(("arbitrary",)),
        name="expert_combine",
    )(start, cnt, tot, x1, rank, aff, g2, norm_f_g.reshape(1, d), ye)


def _trunk_layer(x, batch, seq_len, mods, cond_of_tile, moe_groups, cond_block, p, s0_f, s0_b, rope, emit_state,
                 final_g):
    t, d = x.shape
    sh1, sc1, g1, sh2, sc2, g2 = mods
    u, x2c, qkvg, gates = _in_projection(x, p["norm1_g"], sc1, sh1, p["w_in"], p["hy_conv_w"], p["hy_conv_b"],
                                         cond_of_tile, seq_len, rope)
    seq = lambda a: a.reshape(batch, seq_len, a.shape[-1])
    y_hy = _hyena_long_conv(seq(u), seq(x2c), p).reshape(t, -1)
    ret = _retention(qkvg.reshape(batch, seq_len, -1), p["ret_decay_fwd"], p["ret_decay_bwd"], p["ret_gn_g"],
                     s0_f, s0_b, emit_state)
    y_ret = ret[0].reshape(t, -1)
    view = lambda a: a.reshape(moe_groups, t // moe_groups, a.shape[-1])
    x1, h2, aff = _out_projection(view(y_hy), view(y_ret), view(gates), view(x), g1, sc2, sh2, p["norm2_g"],
                                  p["w_hy_o"], p["w_ret_o"], p["w_out"], p["w_router"], cond_block)
    cap = (EC_CAPACITY_FACTOR * t) // N_EXPERTS
    eb = _expert_block(cap)
    rank, cnt, start, tot = _select(aff, cap, moe_groups)
    xs = _gather(h2, rank, start, cnt, tot, cap)
    ye = _experts(xs, tot, _used_rows(cap, rank.shape[0]), eb, p["w_e_gate"], p["w_e_up"], p["w_e_down"])
    out = _combine(x1, rank, aff, start, cnt, tot, g2, final_g, ye, cond_block, eb)
    return out.reshape(t, d), ret[1:]


def kernel(x_prompt, x_sample, state_ret_fwd, state_ret_bwd, c, c_ctx, w_mod, b_mod, norm1_g, w_in, hy_conv_w, hy_conv_b, hy_f_w1, hy_f_b1, hy_f_freq, hy_f_w2, hy_f_b2, hy_f_w3, hy_decay, hy_bias, ret_decay_fwd, ret_decay_bwd, ret_gn_g, w_hy_o, w_ret_o, w_out, norm2_g, w_router, w_e_gate, w_e_up, w_e_down, norm_f_g):
    b, s, d = x_prompt.shape
    bd, sd, _ = x_sample.shape
    depth = w_mod.shape[0]
    assert (b * s) % TOKEN_TILE == 0 and sd % TOKEN_TILE == 0 and b % 2 == 0 and bd % 2 == 0
    ncond = -(-(bd + 1) // 8) * 8
    lat_groups = bd if (TOKEN_TILE % (16 * bd) == 0 and sd % (TOKEN_TILE // bd) == 0) else 1
    assert lat_groups == bd or bd == 1
    conds = jnp.concatenate([c, c_ctx[None], jnp.zeros((ncond - bd - 1, d), F32)], axis=0)
    ctx_cond = lambda i: bd
    lat_cond = lambda i: i // (sd // TOKEN_TILE)
    xp = x_prompt.reshape(b * s, d)
    xs = x_sample.reshape(bd * sd, d)
    new_f, new_b = [], []
    for l in range(depth):
        mod = _modulation(conds, w_mod[l], b_mod[l])
        mods = [mod[:, k * d:(k + 1) * d].reshape(ncond, 1, d) for k in range(6)]
        p = {"norm1_g": norm1_g[l], "w_in": w_in[l].astype(BF16), "hy_conv_w": hy_conv_w[l],
             "hy_conv_b": hy_conv_b[l], "hy_f_w1": hy_f_w1[l], "hy_f_b1": hy_f_b1[l], "hy_f_freq": hy_f_freq[l],
             "hy_f_w2": hy_f_w2[l], "hy_f_b2": hy_f_b2[l], "hy_f_w3": hy_f_w3[l], "hy_decay": hy_decay[l],
             "hy_bias": hy_bias[l], "ret_decay_fwd": ret_decay_fwd[l], "ret_decay_bwd": ret_decay_bwd[l],
             "ret_gn_g": ret_gn_g[l], "w_hy_o": w_hy_o[l].astype(BF16), "w_ret_o": w_ret_o[l].astype(BF16),
             "w_out": w_out[l].astype(BF16), "norm2_g": norm2_g[l], "w_router": w_router[l],
             "w_e_gate": w_e_gate[l], "w_e_up": w_e_up[l], "w_e_down": w_e_down[l]}
        final_g = norm_f_g if l == depth - 1 else None
        xp, (s_f, s_b) = _trunk_layer(xp, b, s, mods, ctx_cond, 1, bd, p, None, None, False, True, final_g)
        new_f.append(s_f)
        new_b.append(s_b)
        xs, _ = _trunk_layer(xs, bd, sd, mods, lat_cond, lat_groups, 0, p, state_ret_fwd[:, l],
                             state_ret_bwd[:, l], True, False, final_g)
    y_prompt = xp.reshape(b, s, d)
    y_sample = xs.reshape(bd, sd, d)
    return (y_prompt, y_sample, jnp.stack(new_f, axis=1), jnp.stack(new_b, axis=1))
```

```python
import functools
import math

import numpy as np
import jax
import jax.numpy as jnp
from jax import lax
from jax.experimental import pallas as pl
from jax.experimental.pallas import tpu as pltpu

F32 = jnp.float32
BF16 = jnp.bfloat16
HIGHEST = lax.Precision.HIGHEST

EPS = 1e-6
D_HYENA = 512
D_RET = 512
N_RET_HEADS = 4
RET_HEAD_DIM = 128
RET_CHUNK = 128
GRID_W = 64
FILTER_EMB = 33
ROPE_BASE = 10000.0
N_EXPERTS = 16
EC_CAPACITY_FACTOR = 2

TOKEN_TILE = 512
SLOT_ROWS = 96
ROW_ALIGN = 16
HALO_ROWS = 16
FFT_MINOR = 64
FFT_GROUP = 4
VMEM_LIMIT = 56 * 1024 * 1024


def _cparams(sem):
    return pltpu.CompilerParams(dimension_semantics=sem, vmem_limit_bytes=VMEM_LIMIT)


def _silu(x):
    return x * jax.nn.sigmoid(x)


def _mod_kernel(c_ref, w_ref, b_ref, o_ref):
    s = _silu(c_ref[...])
    o_ref[...] = jnp.dot(s, w_ref[...], preferred_element_type=F32, precision=HIGHEST) + b_ref[...]


def _modulation(conds, w_mod, b_mod):
    nc, d = conds.shape
    return pl.pallas_call(
        _mod_kernel,
        grid=(6,),
        in_specs=[pl.BlockSpec((nc, d), lambda j: (0, 0)),
                  pl.BlockSpec((d, d), lambda j: (0, j)),
                  pl.BlockSpec((1, d), lambda j: (0, j))],
        out_specs=pl.BlockSpec((nc, d), lambda j: (0, j)),
        out_shape=jax.ShapeDtypeStruct((nc, 6 * d), F32),
        compiler_params=_cparams(("arbitrary",)),
        name="modulation",
    )(conds, w_mod, b_mod.reshape(1, -1))


def _inproj_kernel(*refs, splits, rope, seq_len, halo):
    refs = list(refs)
    x_ref = refs.pop(0)
    if halo:
        xp_ref, xn_ref = refs.pop(0), refs.pop(0)
    g_ref, sc_ref, sh_ref, w_ref, cw_ref, cb_ref = refs[:6]
    del refs[:6]
    if rope:
        cos_ref, sin_ref = refs.pop(0), refs.pop(0)
    u_ref, x2c_ref, qkvg_ref, gates_ref, cv_sc = refs

    def normed(ref):
        x = ref[...]
        ms = jnp.mean(x * x, axis=-1, keepdims=True)
        h = x * lax.rsqrt(ms + EPS) * g_ref[...]
        return (h * (1.0 + sc_ref[0]) + sh_ref[0]).astype(BF16)

    hb = normed(x_ref)
    tm = hb.shape[0]
    nz, nq, ng = splits
    cw = 512
    dh = RET_HEAD_DIM
    hr = HALO_ROWS
    if halo:
        t0 = pl.program_id(0) * tm
        hp = jnp.where(t0 % seq_len != 0, normed(xp_ref), jnp.zeros((hr, hb.shape[1]), BF16))
        hn = jnp.where((t0 + tm) % seq_len != 0, normed(xn_ref), jnp.zeros((hr, hb.shape[1]), BF16))
        hext = jnp.concatenate([hp, hb, hn], axis=0)

    def conv(c0):
        taps = cw_ref[:, c0:c0 + cw]
        bias = cb_ref[:, c0:c0 + cw]
        if halo:
            acc = jnp.dot(hext, w_ref[:, c0:c0 + cw], preferred_element_type=F32)
            n = tm + 2 * hr
            prev = pltpu.roll(acc, 1, axis=0)[hr:hr + tm]
            nxt = pltpu.roll(acc, n - 1, axis=0)[hr:hr + tm]
            return bias + prev * taps[0:1] + acc[hr:hr + tm] * taps[1:2] + nxt * taps[2:3]
        acc = jnp.dot(hb, w_ref[:, c0:c0 + cw], preferred_element_type=F32)
        row = lax.broadcasted_iota(jnp.int32, (seq_len, 1), 0)
        outs = []
        for s0 in range(0, tm, seq_len):
            x = acc[s0:s0 + seq_len]
            prev = jnp.where(row == 0, 0.0, pltpu.roll(x, 1, axis=0))
            nxt = jnp.where(row == seq_len - 1, 0.0, pltpu.roll(x, seq_len - 1, axis=0))
            outs.append(bias + prev * taps[0:1] + x * taps[1:2] + nxt * taps[2:3])
        return jnp.concatenate(outs, axis=0)

    cv_sc[...] = conv(0)
    u_ref[...] = (cv_sc[...] * conv(D_HYENA)).astype(BF16)
    x2c_ref[...] = conv(2 * D_HYENA).astype(BF16)

    for c0 in range(nz, nz + nq + ng, cw):
        acc = jnp.dot(hb, w_ref[:, c0:c0 + cw], preferred_element_type=F32)
        if c0 < nz + nq:
            part = (c0 - nz) // D_RET
            if part == 1:
                acc = acc * (dh ** -0.5)
            if rope and part < 2:
                lane = lax.broadcasted_iota(jnp.int32, (acc.shape[0], dh), 1)
                swap_hi = (lane % (dh // 2)) < (dh // 4)
                cs, sn = cos_ref[...], sin_ref[...]
                heads = []
                for hh in range(cw // dh):
                    xh = acc[:, hh * dh:(hh + 1) * dh]
                    rot = jnp.where(swap_hi, pltpu.roll(xh, dh - dh // 4, axis=1), pltpu.roll(xh, dh // 4, axis=1))
                    heads.append(xh * cs + rot * sn)
                acc = jnp.concatenate(heads, axis=1)
            qkvg_ref[:, c0 - nz:c0 - nz + cw] = acc.astype(BF16)
        else:
            gates_ref[:, c0 - nz - nq:c0 - nz - nq + cw] = jax.nn.sigmoid(acc).astype(BF16)


def _in_projection(x, norm_g, sc, sh, w_in_bf, conv_w, conv_b, cond_of_tile, seq_len, rope):
    t, d = x.shape
    splits = (3 * D_HYENA, 4 * D_RET, 2 * d)
    assert D_RET == 512 and D_HYENA == 512
    tm = TOKEN_TILE
    assert seq_len % tm == 0 or tm % seq_len == 0
    halo = seq_len > tm
    row = lambda i: (i, 0)
    cond = lambda i: (cond_of_tile(i), 0, 0)
    full = lambda a: pl.BlockSpec(a.shape, lambda i: (0, 0))
    args, specs = [x], [pl.BlockSpec((tm, d), row)]
    if halo:
        hb_per_tile = tm // HALO_ROWS
        last = t // HALO_ROWS - 1
        args += [x, x]
        specs += [pl.BlockSpec((HALO_ROWS, d), lambda i: (jnp.maximum(i * hb_per_tile - 1, 0), 0)),
                  pl.BlockSpec((HALO_ROWS, d), lambda i: (jnp.minimum((i + 1) * hb_per_tile, last), 0))]
    cbias = conv_b.reshape(1, -1)
    args += [norm_g.reshape(1, d), sc, sh, w_in_bf, conv_w, cbias]
    specs += [pl.BlockSpec((1, d), lambda i: (0, 0)), pl.BlockSpec((1, 1, d), cond), pl.BlockSpec((1, 1, d), cond),
              full(w_in_bf), full(conv_w), full(cbias)]
    if rope:
        tiles_per_seq = seq_len // tm
        args += list(_rope_tables(seq_len))
        specs += [pl.BlockSpec((tm, RET_HEAD_DIM), lambda i: (i % tiles_per_seq, 0))] * 2
    return pl.pallas_call(
        functools.partial(_inproj_kernel, splits=splits, rope=rope, seq_len=seq_len, halo=halo),
        grid=(t // tm,),
        in_specs=specs,
        out_specs=[pl.BlockSpec((tm, D_HYENA), row),
                   pl.BlockSpec((tm, D_HYENA), row),
                   pl.BlockSpec((tm, splits[1]), row),
                   pl.BlockSpec((tm, splits[2]), row)],
        out_shape=[jax.ShapeDtypeStruct((t, D_HYENA), BF16),
                   jax.ShapeDtypeStruct((t, D_HYENA), BF16),
                   jax.ShapeDtypeStruct((t, splits[1]), BF16),
                   jax.ShapeDtypeStruct((t, splits[2]), BF16)],
        scratch_shapes=[pltpu.VMEM((tm, 512), F32)],
        compiler_params=_cparams(("parallel",)),
        name="in_projection",
    )(*args)


def _filter_features(seq_len):
    t = np.linspace(0.0, 1.0, seq_len, dtype=np.float32)[:, None]
    bands = (FILTER_EMB - 1) // 2
    w = (np.float32(2.0 * math.pi) * np.arange(seq_len, dtype=np.float32)) / np.float32(seq_len)
    f = np.linspace(1e-4, bands - 1, bands, dtype=np.float32)
    ang = (w[:, None] * f[None, :]).astype(np.float64)
    z = np.concatenate([t, np.cos(ang), -np.sin(ang)], axis=-1).astype(np.float32)
    return np.pad(z, ((0, 0), (0, 128 - FILTER_EMB)))


def _filter_kernel(z_ref, w1_ref, b1_ref, fr_ref, w2_ref, b2_ref, w3_ref, dec_ref, o_ref):
    z = z_ref[...]
    fr = fr_ref[...]
    dot = functools.partial(jnp.dot, preferred_element_type=F32, precision=HIGHEST)
    h = jnp.sin(fr * (dot(z, w1_ref[...]) + b1_ref[...]))
    h = jnp.sin(fr * (dot(h, w2_ref[...]) + b2_ref[...]))
    h = dot(h, w3_ref[...]) * jnp.exp(-z[:, 0:1] * jnp.abs(dec_ref[...]))
    rows = h.shape[0]
    grow = pl.program_id(0) * rows + lax.broadcasted_iota(jnp.int32, (rows, 1), 0)
    o_ref[0] = h[:, :D_HYENA].astype(o_ref.dtype)
    o_ref[1] = jnp.zeros((rows, D_HYENA), o_ref.dtype)
    o_ref[2] = jnp.where(grow == 0, 0.0, h[:, D_HYENA:]).astype(o_ref.dtype)
    o_ref[3] = jnp.zeros((rows, D_HYENA), o_ref.dtype)


def _hyena_filters(seq_len, p, out_dtype):
    z = jnp.asarray(_filter_features(seq_len))
    w1 = jnp.pad(p["hy_f_w1"], ((0, 128 - FILTER_EMB), (0, 0)))
    rows = min(seq_len, 512)
    full = lambda a: pl.BlockSpec(a.shape, lambda i: (0,) * a.ndim)
    ops = [w1, p["hy_f_b1"].reshape(1, -1), p["hy_f_freq"].reshape(1, -1), p["hy_f_w2"],
           p["hy_f_b2"].reshape(1, -1), p["hy_f_w3"], p["hy_decay"].reshape(1, -1)]
    return pl.pallas_call(
        _filter_kernel,
        grid=(seq_len // rows,),
        in_specs=[pl.BlockSpec((rows, 128), lambda i: (i, 0))] + [full(a) for a in ops],
        out_specs=pl.BlockSpec((4, rows, D_HYENA), lambda i: (0, i, 0)),
        out_shape=jax.ShapeDtypeStruct((4, seq_len, D_HYENA), out_dtype),
        compiler_params=_cparams(("arbitrary",)),
        name="hyena_filter",
    )(z, *ops)


def _stacked_dft(n_out, n_in, modulus, scale=1.0):
    k = np.arange(n_out, dtype=np.int64)[:, None]
    n = np.arange(n_in, dtype=np.int64)[None, :]
    th = ((k * n) % modulus) * (2.0 * math.pi / modulus)
    c, s = np.cos(th), np.sin(th)
    fwd = np.block([[c, s], [-s, c]]).astype(np.float32)
    inv = (np.block([[c.T, -s.T], [s.T, c.T]]) * scale).astype(np.float32)
    return fwd, inv


def _second_level_tables(n_total, n1, n2):
    k1 = np.arange(n1, dtype=np.int64)[:, None, None]
    k2 = np.arange(n2, dtype=np.int64)[None, :, None]
    m = np.arange(n2, dtype=np.int64)[None, None, :]
    th = ((m * (k1 + n1 * k2)) % n_total) * (2.0 * math.pi / n_total)
    c, s = np.cos(th), np.sin(th)
    g = np.concatenate([np.concatenate([c, s], axis=2), np.concatenate([-s, c], axis=2)], axis=1)
    return g.astype(np.float32), np.swapaxes(g, 1, 2).astype(np.float32)


def _skewed(n):
    return n + 1


def _fft1_kernel(h_ref, f_ref, a_ref, u_sc, t_sc):
    n2 = FFT_MINOR
    _, seq_len, cb = h_ref.shape
    h1 = seq_len // n2
    nk = a_ref.shape[1]
    up, tp = _skewed(n2), _skewed(nk)
    for b in range(2):
        for q in range(h1):
            u_sc[b, q * up:q * up + n2, :] = h_ref[b, q * n2:(q + 1) * n2, :].astype(F32)

    def fwd(gi, carry):
        m0 = gi * FFT_GROUP
        cols = []
        for d in range(FFT_GROUP):
            xr = u_sc[0, pl.ds(m0 + d, h1, stride=up), :]
            xi = u_sc[1, pl.ds(m0 + d, h1, stride=up), :]
            cols.append(jnp.concatenate([xr, xi], axis=0))
        x = jnp.concatenate(cols, axis=1).astype(BF16)
        res = jnp.dot(f_ref[...], x, preferred_element_type=F32)
        for d in range(FFT_GROUP):
            t_sc[pl.ds((m0 + d) * tp, nk), :] = res[:, d * cb:(d + 1) * cb]
        return carry

    lax.fori_loop(0, n2 // FFT_GROUP, fwd, 0, unroll=2)

    def transpose(k, carry):
        a_ref[0, k] = t_sc[pl.ds(k, n2, stride=tp), :].astype(a_ref.dtype)
        return carry

    lax.fori_loop(0, nk, transpose, 0, unroll=8)


def _fft_first_level(h, f1):
    b, seq_len, c = h.shape
    cb = 128
    return pl.pallas_call(
        _fft1_kernel,
        grid=(b // 2, c // cb),
        in_specs=[pl.BlockSpec((2, seq_len, cb), lambda i, j: (i, 0, j)),
                  pl.BlockSpec(f1.shape, lambda i, j: (0, 0))],
        out_specs=pl.BlockSpec((1, f1.shape[0], FFT_MINOR, cb), lambda i, j: (i, 0, 0, j)),
        out_shape=jax.ShapeDtypeStruct((b // 2, f1.shape[0], FFT_MINOR, c), BF16),
        scratch_shapes=[pltpu.VMEM((2, (seq_len // FFT_MINOR) * _skewed(FFT_MINOR), cb), F32),
                        pltpu.VMEM((FFT_MINOR * _skewed(f1.shape[0]), cb), F32)],
        compiler_params=_cparams(("parallel", "parallel")),
        name="hyena_dft_level1",
    )(h, f1)


def _fft_s2f_kernel(a_ref, g_ref, kr_ref, ki_ref):
    _, _, kb, n2, c = a_ref.shape
    for kk in range(kb):
        g = g_ref[kk]
        hf = jnp.dot(g, a_ref[0, :, kk].reshape(2 * n2, c), preferred_element_type=F32)
        hb = jnp.dot(g, a_ref[1, :, kk].reshape(2 * n2, c), preferred_element_type=F32)
        kr_ref[kk] = hf[:n2] + hb[:n2]
        ki_ref[kk] = hf[n2:] - hb[n2:]


def _fft_filter_second_level(a, g, kb=8):
    _, _, n1, n2, c = a.shape
    spec = pl.BlockSpec((kb, n2, c), lambda i: (i, 0, 0))
    return pl.pallas_call(
        _fft_s2f_kernel,
        grid=(n1 // kb,),
        in_specs=[pl.BlockSpec((2, 2, kb, n2, c), lambda i: (0, 0, i, 0, 0)),
                  pl.BlockSpec((kb, 2 * n2, 2 * n2), lambda i: (i, 0, 0))],
        out_specs=[spec, spec],
        out_shape=[jax.ShapeDtypeStruct((n1, n2, c), F32)] * 2,
        compiler_params=_cparams(("parallel",)),
        name="hyena_filter_spectrum",
    )(a, g)


def _fft_s2_kernel(a_ref, g_ref, gt_ref, kr_ref, ki_ref, b_ref, r_sc):
    npairs, _, kb, n2, cb = a_ref.shape
    nq = cb // 128
    rp = _skewed(2 * n2)
    for kk in range(kb):
        a = jnp.concatenate([a_ref[p, :, kk].reshape(2 * n2, cb) for p in range(npairs)], axis=1)
        x = jnp.dot(g_ref[kk], a, preferred_element_type=F32)
        xr, xi = x[:n2], x[n2:]
        kr = jnp.concatenate([kr_ref[kk]] * npairs, axis=1)
        ki = jnp.concatenate([ki_ref[kk]] * npairs, axis=1)
        y = jnp.concatenate([xr * kr - xi * ki, xr * ki + xi * kr], axis=0).astype(BF16)
        res = jnp.dot(gt_ref[kk], y, preferred_element_type=F32)
        for q in range(npairs * nq):
            r_sc[q, kk * rp:kk * rp + 2 * n2, :] = res[:, q * 128:(q + 1) * 128]

    for p in range(npairs):
        def gather(row):
            return jnp.concatenate([r_sc[p * nq + q, pl.ds(row, kb, stride=rp), :] for q in range(nq)], axis=1)

        def transpose(m, carry):
            b_ref[p, 0, m] = gather(m).astype(b_ref.dtype)
            b_ref[p, 1, m] = gather(n2 + m).astype(b_ref.dtype)
            return carry

        lax.fori_loop(0, n2, transpose, 0, unroll=8)


def _fft_second_level(a, g, gt, kr, ki):
    p, _, n1, n2, c = a.shape
    kb, cb = 16, 256
    gspec = pl.BlockSpec((kb, 2 * n2, 2 * n2), lambda i, j: (i, 0, 0))
    kspec = pl.BlockSpec((kb, n2, cb), lambda i, j: (i, 0, j))
    return pl.pallas_call(
        _fft_s2_kernel,
        grid=(n1 // kb, c // cb),
        in_specs=[pl.BlockSpec((p, 2, kb, n2, cb), lambda i, j: (0, 0, i, 0, j)), gspec, gspec, kspec, kspec],
        out_specs=pl.BlockSpec((p, 2, n2, kb, cb), lambda i, j: (0, 0, 0, i, j)),
        out_shape=jax.ShapeDtypeStruct((p, 2, n2, n1, c), BF16),
        scratch_shapes=[pltpu.VMEM((p * cb // 128, kb * _skewed(2 * n2), 128), F32)],
        compiler_params=_cparams(("parallel", "parallel")),
        name="hyena_dft_level2",
    )(a, g, gt, kr, ki)


def _fft_s3_kernel(b_ref, f_ref, u_ref, x2_ref, bias_ref, o_ref, t_sc):
    _, _, n2, n1, cb = b_ref.shape
    nr = f_ref.shape[0]
    h1 = nr // 2
    tp = _skewed(nr)

    def inv(gi, carry):
        m0 = gi * FFT_GROUP
        x = jnp.concatenate([b_ref[0, :, m0 + d].reshape(2 * n1, cb) for d in range(FFT_GROUP)], axis=1)
        res = jnp.dot(f_ref[...], x, preferred_element_type=F32)
        for d in range(FFT_GROUP):
            t_sc[pl.ds((m0 + d) * tp, nr), :] = res[:, d * cb:(d + 1) * cb]
        return carry

    lax.fori_loop(0, n2 // FFT_GROUP, inv, 0, unroll=2)

    bias = bias_ref[...]
    for b in range(2):
        def finish(q, carry):
            rows = pl.ds(pl.multiple_of(q * n2, n2), n2)
            conv = t_sc[pl.ds(b * h1 + q, n2, stride=tp), :]
            u = u_ref[b, rows, :].astype(F32)
            o_ref[b, rows, :] = ((conv + u * bias) * x2_ref[b, rows, :].astype(F32)).astype(o_ref.dtype)
            return carry

        lax.fori_loop(0, h1, finish, 0, unroll=4)


def _fft_last_level(bt, f1inv, u, x2c, bias):
    p, _, n2, n1, c = bt.shape
    b, seq_len, _ = u.shape
    cb = 128
    uspec = pl.BlockSpec((2, seq_len, cb), lambda i, j: (i, 0, j))
    return pl.pallas_call(
        _fft_s3_kernel,
        grid=(p, c // cb),
        in_specs=[pl.BlockSpec((1, 2, n2, n1, cb), lambda i, j: (i, 0, 0, 0, j)),
                  pl.BlockSpec(f1inv.shape, lambda i, j: (0, 0)),
                  uspec, uspec,
                  pl.BlockSpec((1, cb), lambda i, j: (0, j))],
        out_specs=uspec,
        out_shape=jax.ShapeDtypeStruct(u.shape, BF16),
        scratch_shapes=[pltpu.VMEM((n2 * _skewed(f1inv.shape[0]), cb), F32)],
        compiler_params=_cparams(("parallel", "parallel")),
        name="hyena_dft_inverse",
    )(bt, f1inv, u, x2c, bias)


def _kf_direct_kernel(h_ref, f_ref, kr_ref, ki_ref):
    n = kr_ref.shape[0]
    dot = functools.partial(jnp.dot, preferred_element_type=F32, precision=HIGHEST)
    hf = dot(f_ref[...], h_ref[0])
    hb = dot(f_ref[...], h_ref[2])
    kr_ref[...] = hf[:n] + hb[:n]
    ki_ref[...] = hf[n:] - hb[n:]


def _hyena_direct_kernel(u_ref, x2_ref, f_ref, fi_ref, kr_ref, ki_ref, bias_ref, o_ref):
    _, two, seq_len, cb = u_ref.shape
    n = kr_ref.shape[0]
    ub = u_ref[0].reshape(two * seq_len, cb)
    x = jnp.dot(f_ref[...], ub, preferred_element_type=F32)
    xr, xi = x[:n], x[n:]
    kr, ki = kr_ref[...], ki_ref[...]
    y = jnp.concatenate([xr * kr - xi * ki, xr * ki + xi * kr], axis=0).astype(BF16)
    conv = jnp.dot(fi_ref[...], y, preferred_element_type=F32)
    x2 = x2_ref[0].reshape(two * seq_len, cb).astype(F32)
    o = (conv + ub.astype(F32) * bias_ref[...]) * x2
    o_ref[0] = o.reshape(two, seq_len, cb).astype(o_ref.dtype)


def _hyena_long_conv(u, x2c, p):
    b, seq_len, _ = u.shape
    c = D_HYENA
    n = 2 * seq_len
    bias = p["hy_bias"].reshape(1, c)
    hh = _hyena_filters(seq_len, p, F32)
    if seq_len <= 512:
        fwd, inv = _stacked_dft(n, seq_len, n, scale=1.0 / n)
        kr, ki = pl.pallas_call(
            _kf_direct_kernel,
            out_shape=[jax.ShapeDtypeStruct((n, c), F32)] * 2,
            compiler_params=_cparams(None),
            name="hyena_filter_spectrum_direct",
        )(hh, fwd[:, :seq_len])
        cb = 256
        pair = lambda a: a.reshape(b // 2, 2, seq_len, c)
        uspec = pl.BlockSpec((1, 2, seq_len, cb), lambda i, j: (i, 0, 0, j))
        kspec = pl.BlockSpec((n, cb), lambda i, j: (0, j))
        out = pl.pallas_call(
            _hyena_direct_kernel,
            grid=(b // 2, c // cb),
            in_specs=[uspec, uspec,
                      pl.BlockSpec(fwd.shape, lambda i, j: (0, 0)),
                      pl.BlockSpec(inv.shape, lambda i, j: (0, 0)),
                      kspec, kspec,
                      pl.BlockSpec((1, cb), lambda i, j: (0, j))],
            out_specs=uspec,
            out_shape=jax.ShapeDtypeStruct((b // 2, 2, seq_len, c), BF16),
            compiler_params=_cparams(("parallel", "parallel")),
            name="hyena_dft_direct",
        )(pair(u), pair(x2c), jnp.asarray(fwd).astype(BF16), jnp.asarray(inv).astype(BF16), kr, ki, bias)
        return out.reshape(b, seq_len, c)

    n2 = FFT_MINOR
    n1 = n // n2
    h1 = seq_len // n2
    f1, f1inv = _stacked_dft(n1, h1, n1, scale=1.0 / n)
    f1, f1inv = jnp.asarray(f1).astype(BF16), jnp.asarray(f1inv).astype(BF16)
    g, gt = _second_level_tables(n, n1, n2)
    g, gt = jnp.asarray(g).astype(BF16), jnp.asarray(gt).astype(BF16)
    ha = _fft_first_level(hh, f1)
    kr, ki = _fft_filter_second_level(ha.reshape(2, 2, n1, n2, c), g)
    a = _fft_first_level(u, f1)
    bt = _fft_second_level(a.reshape(b // 2, 2, n1, n2, c), g, gt, kr, ki)
    return _fft_last_level(bt, f1inv, u, x2c, bias)


def _rope_tables(seq_len):
    half = RET_HEAD_DIM // 2
    nf = half // 2
    t = np.arange(seq_len)
    inv = ROPE_BASE ** (-np.arange(nf, dtype=np.float64) / nf)
    ar = (t // GRID_W)[:, None] * inv[None, :]
    ac = (t % GRID_W)[:, None] * inv[None, :]
    cos = np.concatenate([np.cos(ar), np.cos(ar), np.cos(ac), np.cos(ac)], axis=-1)
    sin = np.concatenate([-np.sin(ar), np.sin(ar), -np.sin(ac), np.sin(ac)], axis=-1)
    return cos.astype(np.float32), sin.astype(np.float32)


def _log_sigmoid(x):
    return jnp.minimum(x, 0.0) - jnp.log1p(jnp.exp(-jnp.abs(x)))


def _retention_kernel(*refs, has_init, emit_state, cpb):
    refs = list(refs)
    q_ref, k_ref, v_ref, g_ref, dec_ref, gn_ref = refs[:6]
    del refs[:6]
    if has_init:
        s0f_ref, s0b_ref = refs[:2]
        del refs[:2]
    o_ref = refs.pop(0)
    if emit_state:
        sf_out, sb_out = refs[:2]
        del refs[:2]
    sf_ref, sb_ref, sball_ref = refs

    c = RET_CHUNK
    dh = RET_HEAD_DIM
    nh = N_RET_HEADS
    phase = pl.program_id(1)
    j = pl.program_id(2)
    nb = pl.num_programs(2)

    ri = lax.broadcasted_iota(jnp.int32, (c, c), 0).astype(F32)
    ci = lax.broadcasted_iota(jnp.int32, (c, c), 1).astype(F32)
    diff = ri - ci

    def head_consts(h):
        lgf = _log_sigmoid(dec_ref[0, h])[0:1, :]
        lgb = _log_sigmoid(dec_ref[1, h])[0:1, :]
        return lgf, lgb

    bb = q_ref.shape[0]
    heads = [(bi, h) for bi in range(bb) for h in range(nh)]

    def load(ref, bi, r0, h):
        return ref[bi, r0:r0 + c, h * dh:(h + 1) * dh]

    dn_t = (((0,), (0,)), ((), ()))
    dn_nt = (((1,), (1,)), ((), ()))

    @pl.when(phase == 0)
    def _backward_sweep():
        @pl.when(j == 0)
        def _():
            for bi, h in heads:
                sb_ref[bi * nh + h] = s0b_ref[bi, h] if has_init else jnp.zeros((dh, dh), F32)

        blk = nb - 1 - j
        for bi, h in heads:
            hs = bi * nh + h
            _, lgb = head_consts(h)
            zeta_b = jnp.exp(lgb * ri)
            cdec_b = jnp.exp(lgb * float(c))
            for cc in reversed(range(cpb)):
                r0 = cc * c
                n = blk * cpb + cc
                s = sb_ref[hs]
                sball_ref[n, hs] = s.astype(BF16)
                kz = (load(k_ref, bi, r0, h).astype(F32) * zeta_b).astype(BF16)
                vv = load(v_ref, bi, r0, h)
                sb_ref[hs] = cdec_b * s + lax.dot_general(kz, vv, dn_t, preferred_element_type=F32)

        if emit_state:
            @pl.when(j == nb - 1)
            def _():
                for bi, h in heads:
                    sb_out[bi, h] = sb_ref[bi * nh + h]

    @pl.when(phase == 1)
    def _forward_sweep():
        @pl.when(j == 0)
        def _():
            for bi, h in heads:
                sf_ref[bi * nh + h] = s0f_ref[bi, h] if has_init else jnp.zeros((dh, dh), F32)

        for bi, h in heads:
            hs = bi * nh + h
            lgf, lgb = head_consts(h)
            mask = (jnp.where(diff >= 0, jnp.exp(lgf * jnp.maximum(diff, 0.0)), 0.0)
                    + jnp.where(diff <= 0, jnp.exp(lgb * jnp.maximum(-diff, 0.0)), 0.0))
            xi_f = jnp.exp(lgf * (ri + 1.0))
            xi_b = jnp.exp(lgb * (float(c) - ri))
            zeta_f = jnp.exp(lgf * (float(c - 1) - ri))
            cdec_f = jnp.exp(lgf * float(c))
            gn = gn_ref[:, h * dh:(h + 1) * dh]
            for cc in range(cpb):
                r0 = cc * c
                n = j * cpb + cc
                qb = load(q_ref, bi, r0, h)
                kb = load(k_ref, bi, r0, h)
                vv = load(v_ref, bi, r0, h)
                gate = load(g_ref, bi, r0, h).astype(F32)
                sc = lax.dot_general(qb, kb, dn_nt, preferred_element_type=F32)
                inner = jnp.dot((sc * mask).astype(BF16), vv, preferred_element_type=F32)
                s = sf_ref[hs]
                q = qb.astype(F32)
                lhs = jnp.concatenate([q * xi_f, q * xi_b], axis=1).astype(BF16)
                rhs = jnp.concatenate([s.astype(BF16), sball_ref[n, hs]], axis=0)
                o = inner + jnp.dot(lhs, rhs, preferred_element_type=F32)
                mu = jnp.mean(o, axis=-1, keepdims=True)
                d = o - mu
                var = jnp.mean(d * d, axis=-1, keepdims=True)
                y = d * lax.rsqrt(var + EPS) * gn * _silu(gate)
                o_ref[bi, r0:r0 + c, h * dh:(h + 1) * dh] = y.astype(o_ref.dtype)
                kz = (kb.astype(F32) * zeta_f).astype(BF16)
                sf_ref[hs] = cdec_f * s + lax.dot_general(kz, vv, dn_t, preferred_element_type=F32)

        if emit_state:
            @pl.when(j == nb - 1)
            def _():
                for bi, h in heads:
                    sf_out[bi, h] = sf_ref[bi * nh + h]


def _retention(qkvg, dec_f, dec_b, gn_g, s0_f, s0_b, emit_state):
    b, seq_len, _ = qkvg.shape
    nh, dh, c = N_RET_HEADS, RET_HEAD_DIM, RET_CHUNK
    rb = min(seq_len, 1024)
    nb = seq_len // rb
    cpb = rb // c
    bb = max(1, min(4, 1024 // seq_len))
    while b % bb:
        bb //= 2
    has_init = s0_f is not None
    dec = jnp.broadcast_to(jnp.stack([dec_f, dec_b])[:, :, None, None], (2, nh, 8, 128)).astype(F32)
    kv_blk = lambda i, p, j: jnp.where(p == 0, nb - 1 - j, j)
    q_blk = lambda i, p, j: jnp.where(p == 0, 0, j)
    in_specs = [pl.BlockSpec((bb, rb, D_RET), lambda i, p, j: (i, q_blk(i, p, j), 0)),
                pl.BlockSpec((bb, rb, D_RET), lambda i, p, j: (i, kv_blk(i, p, j), 1)),
                pl.BlockSpec((bb, rb, D_RET), lambda i, p, j: (i, kv_blk(i, p, j), 2)),
                pl.BlockSpec((bb, rb, D_RET), lambda i, p, j: (i, q_blk(i, p, j), 3)),
                pl.BlockSpec((2, nh, 8, 128), lambda i, p, j: (0, 0, 0, 0)),
                pl.BlockSpec((1, D_RET), lambda i, p, j: (0, 0))]
    args = [qkvg, qkvg, qkvg, qkvg, dec, gn_g.reshape(1, -1)]
    sspec = pl.BlockSpec((bb, nh, dh, dh), lambda i, p, j: (i, 0, 0, 0))
    if has_init:
        in_specs += [sspec, sspec]
        args += [s0_f, s0_b]
    out_specs = [pl.BlockSpec((bb, rb, D_RET), lambda i, p, j: (i, q_blk(i, p, j), 0))]
    out_shape = [jax.ShapeDtypeStruct((b, seq_len, D_RET), BF16)]
    if emit_state:
        out_specs += [sspec, sspec]
        out_shape += [jax.ShapeDtypeStruct((b, nh, dh, dh), F32)] * 2
    return pl.pallas_call(
        functools.partial(_retention_kernel, has_init=has_init, emit_state=emit_state, cpb=cpb),
        grid=(b // bb, 2, nb),
        in_specs=in_specs,
        out_specs=out_specs,
        out_shape=out_shape,
        scratch_shapes=[pltpu.VMEM((bb * nh, dh, dh), F32), pltpu.VMEM((bb * nh, dh, dh), F32),
                        pltpu.VMEM((nb * cpb, bb * nh, dh, dh), BF16)],
        compiler_params=_cparams(("parallel", "arbitrary", "arbitrary")),
        name="retention",
    )(*args)


def _outproj_kernel(yhy_ref, yret_ref, gates_ref, x_ref, g1_ref, sc_ref, sh_ref, ng_ref,
                    why_ref, wret_ref, wout_ref, wr_ref, x1_ref, h2_ref, aff_ref):
    g, m, d = x_ref.shape
    rows = lambda ref: ref[...].reshape(g * m, ref.shape[2])
    a = jnp.dot(rows(yhy_ref), why_ref[...], preferred_element_type=F32)
    b = jnp.dot(rows(yret_ref), wret_ref[...], preferred_element_type=F32)
    gates = rows(gates_ref)
    merged = gates[:, :d].astype(F32) * a + gates[:, d:].astype(F32) * b
    out = jnp.dot(merged.astype(BF16), wout_ref[...], preferred_element_type=F32)
    x1 = x_ref[...] + g1_ref[...] * out.reshape(g, m, d)
    x1_ref[...] = x1
    ms = jnp.mean(x1 * x1, axis=-1, keepdims=True)
    h = x1 * lax.rsqrt(ms + EPS) * ng_ref[...]
    h = (h * (1.0 + sc_ref[...]) + sh_ref[...]).reshape(g * m, d)
    h_hi = h.astype(BF16)
    h2_ref[...] = h_hi.reshape(g, m, d)
    h_lo = (h - h_hi.astype(F32)).astype(BF16)
    t = jnp.dot(h_hi, wr_ref[...], preferred_element_type=F32)
    logits = t[:, :128] + t[:, 128:] + jnp.dot(h_lo, wr_ref[:, :128], preferred_element_type=F32)
    lane = lax.broadcasted_iota(jnp.int32, logits.shape, 1)
    logits = jnp.where(lane < N_EXPERTS, logits, -jnp.inf)
    e = jnp.exp(logits - jnp.max(logits, axis=-1, keepdims=True))
    aff = e / jnp.sum(e, axis=-1, keepdims=True)
    aff_ref[0] = aff.T[:N_EXPERTS, :]


def _out_projection(y_hy, y_ret, gates, x, g1, sc2, sh2, norm2_g, w_hy_o, w_ret_o, w_out, w_router, cond_block):
    g, s, d = x.shape
    tm = TOKEN_TILE
    m = tm // g
    tok = lambda c: pl.BlockSpec((g, m, c), lambda i: (0, i, 0))
    cond = pl.BlockSpec((g, 1, d), lambda i: (cond_block, 0, 0))
    full = lambda a: pl.BlockSpec(a.shape, lambda i: (0, 0))
    wr = jnp.pad(w_router, ((0, 0), (0, 128 - N_EXPERTS)))
    wr_hi = wr.astype(BF16)
    wr = jnp.concatenate([wr_hi, (wr - wr_hi.astype(F32)).astype(BF16)], axis=1)
    return pl.pallas_call(
        _outproj_kernel,
        grid=(s // m,),
        in_specs=[tok(D_HYENA), tok(D_RET), tok(2 * d), tok(d), cond, cond, cond,
                  pl.BlockSpec((1, d), lambda i: (0, 0)),
                  full(w_hy_o), full(w_ret_o), full(w_out), full(wr)],
        out_specs=[tok(d), tok(d), pl.BlockSpec((1, N_EXPERTS, tm), lambda i: (i, 0, 0))],
        out_shape=[jax.ShapeDtypeStruct((g, s, d), F32), jax.ShapeDtypeStruct((g, s, d), BF16),
                   jax.ShapeDtypeStruct((s // m, N_EXPERTS, tm), F32)],
        compiler_params=_cparams(("parallel",)),
        name="out_projection_router",
    )(y_hy, y_ret, gates, x, g1, sc2, sh2, norm2_g.reshape(1, d), w_hy_o, w_ret_o, w_out, wr)


def _select_kernel(aff_ref, tri_ref, rank_ref, cnt_ref, start_ref, tot_ref, *, cap, idx_bits, groups):
    nt, ne, tm = aff_ref.shape
    a = aff_ref[...]

    def count(m):
        return jnp.sum(jnp.sum(m, axis=0, keepdims=True), axis=2, keepdims=True)

    def thr_step(s, thr):
        cand = thr | (1 << (30 - s))
        cnt = count(jnp.where(a >= pltpu.bitcast(cand, F32), 1.0, 0.0))
        return jnp.where(cnt >= float(cap), cand, thr)

    thr = pltpu.bitcast(lax.fori_loop(0, 31, thr_step, jnp.zeros((1, ne, 1), jnp.int32)), F32)
    gt = a > thr
    eq = a == thr
    need = float(cap) - count(jnp.where(gt, 1.0, 0.0))
    m = tm // groups
    tile = lax.broadcasted_iota(jnp.int32, (nt, 1, tm), 0)
    lane = lax.broadcasted_iota(jnp.int32, (nt, 1, tm), 2)
    idx = (lane // m) * (nt * m) + tile * m + lane % m

    def idx_step(s, lim):
        cand = lim | (1 << (idx_bits - 1 - s))
        cnt = count(jnp.where(eq, jnp.where(idx < cand, 1.0, 0.0), 0.0))
        return jnp.where(cnt < need, cand, lim)

    lim = lax.fori_loop(0, idx_bits, idx_step, jnp.zeros((1, ne, 1), jnp.int32))
    sel = jnp.where(gt, 1.0, jnp.where(eq, jnp.where(idx <= lim, 1.0, 0.0), 0.0))
    sel2 = sel.reshape(nt * ne, tm).astype(BF16)
    prefix = jnp.dot(sel2, tri_ref[...], preferred_element_type=F32)
    rank = jnp.where(sel2 > 0, prefix, -1.0).astype(jnp.int32)
    rank_ref[...] = rank.reshape(nt, ne, tm)
    ones = jnp.ones((tm, 128), BF16)
    cnt = jnp.dot(sel2, ones, preferred_element_type=F32).astype(jnp.int32).reshape(nt, ne, 128)
    cnt_ref[...] = cnt
    acc = jnp.zeros((ne, 128), jnp.int32)
    for t in range(nt):
        start_ref[t] = acc
        acc = acc + ((cnt[t] + (ROW_ALIGN - 1)) & (-ROW_ALIGN))
    tot_ref[...] = acc


def _select(aff, cap, groups):
    nt, ne, tm = aff.shape
    idx_bits = max(1, int(math.ceil(math.log2(nt * tm))))
    r = lax.broadcasted_iota(jnp.int32, (tm, tm), 0)
    c = lax.broadcasted_iota(jnp.int32, (tm, tm), 1)
    tri = (r < c).astype(BF16)
    rank, cnt, start, tot = pl.pallas_call(
        functools.partial(_select_kernel, cap=cap, idx_bits=idx_bits, groups=groups),
        out_shape=[jax.ShapeDtypeStruct((nt, ne, tm), jnp.int32),
                   jax.ShapeDtypeStruct((nt, ne, 128), jnp.int32),
                   jax.ShapeDtypeStruct((nt, ne, 128), jnp.int32),
                   jax.ShapeDtypeStruct((ne, 128), jnp.int32)],
        compiler_params=_cparams(None),
        name="expert_choice_select",
    )(aff, tri)
    return rank, cnt[:, :, 0], start[:, :, 0], tot[:, 0]


def _expert_block(cap):
    return 512 if cap >= 2048 else 256


def _used_rows(cap, nt):
    eb = _expert_block(cap)
    return -(-(cap + ROW_ALIGN * nt) // eb) * eb


def _list_rows(cap, nt):
    max_rounds = -(-TOKEN_TILE // SLOT_ROWS)
    return _used_rows(cap, nt) + max(_expert_block(cap), max_rounds * SLOT_ROWS)


def _num_rounds(cnt_sm, i):
    m = cnt_sm[i, 0]
    for e in range(1, N_EXPERTS):
        m = jnp.maximum(m, cnt_sm[i, e])
    return jnp.maximum((m + SLOT_ROWS - 1) // SLOT_ROWS, 1)


def _gather_kernel(start_sm, cnt_sm, tot_sm, h_ref, rank_ref, xs_hbm, stage, zbuf, sem):
    i = pl.program_id(0)
    ne = N_EXPERTS
    slot = i % 2
    rank = rank_ref[0]
    sub = lax.broadcasted_iota(jnp.int32, (SLOT_ROWS, rank.shape[1]), 0)

    def copy(s, e, off):
        return pltpu.make_async_copy(stage.at[s, pl.ds(e * SLOT_ROWS, SLOT_ROWS)],
                                     xs_hbm.at[e, pl.ds(off, SLOT_ROWS)], sem.at[e])

    def fill(r):
        h = h_ref[...].reshape(rank.shape[1], h_ref.shape[2])
        for e in range(ne):
            onehot = jnp.where(rank[e:e + 1, :] == sub + r * SLOT_ROWS, 1.0, 0.0).astype(BF16)
            stage[slot, e * SLOT_ROWS:(e + 1) * SLOT_ROWS, :] = jnp.dot(
                onehot, h, preferred_element_type=F32).astype(BF16)

    def start_all(r):
        for e in range(ne):
            copy(slot, e, pl.multiple_of(start_sm[i, e] + r * SLOT_ROWS, ROW_ALIGN)).start()

    def wait_all(s):
        for e in range(ne):
            copy(s, e, 0).wait()

    fill(0)

    @pl.when(i > 0)
    def _():
        wait_all(1 - slot)

    start_all(0)

    def extra_round(r, carry):
        wait_all(slot)
        fill(r)
        start_all(r)
        return carry

    lax.fori_loop(1, _num_rounds(cnt_sm, i), extra_round, 0)

    @pl.when(i == pl.num_programs(0) - 1)
    def _zero_tail():
        wait_all(slot)
        zbuf[...] = jnp.zeros(zbuf.shape, zbuf.dtype)

        def zcopy(e):
            off = pl.multiple_of(tot_sm[e], ROW_ALIGN)
            return pltpu.make_async_copy(zbuf, xs_hbm.at[e, pl.ds(off, zbuf.shape[0])], sem.at[e])

        for e in range(ne):
            zcopy(e).start()
        for e in range(ne):
            zcopy(e).wait()


def _gather(h2, rank, start, cnt, tot, cap):
    g, _, d = h2.shape
    nt, ne, tm = rank.shape
    rl = _list_rows(cap, nt)
    return pl.pallas_call(
        _gather_kernel,
        grid_spec=pltpu.PrefetchScalarGridSpec(
            num_scalar_prefetch=3,
            grid=(nt,),
            in_specs=[pl.BlockSpec((g, tm // g, d), lambda i, *_: (0, i, 0)),
                      pl.BlockSpec((1, ne, tm), lambda i, *_: (i, 0, 0))],
            out_specs=pl.BlockSpec(memory_space=pl.ANY),
            scratch_shapes=[pltpu.VMEM((2, ne * SLOT_ROWS, d), BF16),
                            pltpu.VMEM((_expert_block(cap), d), BF16),
                            pltpu.SemaphoreType.DMA((ne,))]),
        out_shape=jax.ShapeDtypeStruct((ne, rl, d), BF16),
        compiler_params=_cparams(("arbitrary",)),
        name="expert_gather",
    )(start, cnt, tot, h2, rank)


def _expert_kernel(*refs, nblocks):
    ns = len(nblocks)
    tot_sms, xs_refs = refs[:ns], refs[ns:2 * ns]
    wg_ref, wu_ref, wd_ref = refs[2 * ns:2 * ns + 3]
    ye_refs = refs[2 * ns + 3:3 * ns + 3]
    wg_bf, wu_bf, wd_bf = refs[3 * ns + 3:]
    e = pl.program_id(0)
    j = pl.program_id(1)

    @pl.when(j == 0)
    def _():
        wg_bf[...] = wg_ref[0].astype(BF16)
        wu_bf[...] = wu_ref[0].astype(BF16)
        wd_bf[...] = wd_ref[0].astype(BF16)

    base = 0
    for s in range(ns):
        jj = j - base
        live = jnp.logical_and(jnp.logical_and(jj >= 0, jj < nblocks[s]), jj * xs_refs[s].shape[1] < tot_sms[s][e])

        @pl.when(live)
        def _(xs_ref=xs_refs[s], ye_ref=ye_refs[s]):
            x = xs_ref[0]
            g = jnp.dot(x, wg_bf[...], preferred_element_type=F32)
            u = jnp.dot(x, wu_bf[...], preferred_element_type=F32)
            hid = (_silu(g) * u).astype(BF16)
            ye_ref[0] = jnp.dot(hid, wd_bf[...], preferred_element_type=F32).astype(ye_ref.dtype)

        base += nblocks[s]


def _experts(sets, w_gate, w_up, w_down):
    ns = len(sets)
    ne, _, d = sets[0][0].shape
    f = w_gate.shape[2]
    nblocks = tuple(used // eb for _, _, used, eb in sets)

    def block_spec(s):
        eb = sets[s][3]
        base = sum(nblocks[:s])

        def index(e, j, *tots):
            return (e, jnp.clip(j - base, 0, (tots[s][e] - 1) // eb), 0)

        return pl.BlockSpec((1, eb, d), index)

    wspec = lambda shape: pl.BlockSpec(shape, lambda e, j, *tots: (e, 0, 0))
    return pl.pallas_call(
        functools.partial(_expert_kernel, nblocks=nblocks),
        grid_spec=pltpu.PrefetchScalarGridSpec(
            num_scalar_prefetch=ns,
            grid=(ne, sum(nblocks)),
            in_specs=[block_spec(s) for s in range(ns)] + [wspec((1, d, f)), wspec((1, d, f)), wspec((1, f, d))],
            out_specs=[block_spec(s) for s in range(ns)],
            scratch_shapes=[pltpu.VMEM((d, f), BF16), pltpu.VMEM((d, f), BF16), pltpu.VMEM((f, d), BF16)]),
        out_shape=[jax.ShapeDtypeStruct((ne, used, d), BF16) for _, _, used, _ in sets],
        compiler_params=_cparams(("arbitrary", "arbitrary")),
        name="expert_ffn",
    )(*[t for _, t, _, _ in sets], *[x for x, _, _, _ in sets], w_gate, w_up, w_down)


def _combine_kernel(start_sm, cnt_sm, tot_sm, x1_ref, rank_ref, aff_ref, g2_ref, nf_ref, ye_hbm, o_ref, buf, sem, *,
                    final_norm, eb):
    i = pl.program_id(0)
    nt = pl.num_programs(0)
    ne = N_EXPERTS
    slot = i % 2
    rank = rank_ref[0]
    aff = aff_ref[0]
    sub = lax.broadcasted_iota(jnp.int32, (SLOT_ROWS, rank.shape[1]), 0)
    dn_t = (((0,), (0,)), ((), ()))

    def window(t, e, r):
        want = start_sm[t, e] + r * SLOT_ROWS
        written = ((tot_sm[e] + eb - 1) // eb) * eb
        off = jnp.minimum(want, written - SLOT_ROWS)
        return pl.multiple_of(off, ROW_ALIGN), want - off

    def copy(t, s, e, r):
        off, _ = window(t, e, r)
        return pltpu.make_async_copy(ye_hbm.at[e, pl.ds(off, SLOT_ROWS)],
                                     buf.at[s, pl.ds(e * SLOT_ROWS, SLOT_ROWS)], sem.at[s, e])

    def weighted(r):
        parts = []
        for e in range(ne):
            _, shift = window(i, e, r)
            local = rank[e:e + 1, :] - r * SLOT_ROWS
            hit = jnp.where(local >= 0, local + shift, -1) == sub
            parts.append(jnp.where(hit, aff[e:e + 1, :], 0.0).astype(BF16))
        return lax.dot_general(jnp.concatenate(parts, axis=0), buf[slot], dn_t, preferred_element_type=F32)

    @pl.when(i == 0)
    def _():
        for e in range(ne):
            copy(0, 0, e, 0).start()

    @pl.when(i + 1 < nt)
    def _prefetch_next_tile():
        for e in range(ne):
            copy(i + 1, 1 - slot, e, 0).start()

    for e in range(ne):
        copy(i, slot, e, 0).wait()
    y0 = weighted(0)

    def extra_round(r, acc):
        for e in range(ne):
            @pl.when(cnt_sm[i, e] > r * SLOT_ROWS)
            def _():
                copy(i, slot, e, r).start()
        for e in range(ne):
            @pl.when(cnt_sm[i, e] > r * SLOT_ROWS)
            def _():
                copy(i, slot, e, r).wait()
        return acc + weighted(r)

    y = lax.fori_loop(1, _num_rounds(cnt_sm, i), extra_round, y0)
    x2 = x1_ref[...] + g2_ref[...] * y.reshape(x1_ref.shape)
    if final_norm:
        ms = jnp.mean(x2 * x2, axis=-1, keepdims=True)
        x2 = x2 * lax.rsqrt(ms + EPS) * nf_ref[...]
    o_ref[...] = x2


def _combine(x1, rank, aff, start, cnt, tot, g2, norm_f_g, ye, cond_block, eb):
    g, s, d = x1.shape
    nt, ne, tm = rank.shape
    tok = pl.BlockSpec((g, tm // g, d), lambda i, *_: (0, i, 0))
    final_norm = norm_f_g is not None
    if not final_norm:
        norm_f_g = jnp.ones((d,), F32)
    return pl.pallas_call(
        functools.partial(_combine_kernel, final_norm=final_norm, eb=eb),
        grid_spec=pltpu.PrefetchScalarGridSpec(
            num_scalar_prefetch=3,
            grid=(nt,),
            in_specs=[tok,
                      pl.BlockSpec((1, ne, tm), lambda i, *_: (i, 0, 0)),
                      pl.BlockSpec((1, ne, tm), lambda i, *_: (i, 0, 0)),
                      pl.BlockSpec((g, 1, d), lambda i, *_: (cond_block, 0, 0)),
                      pl.BlockSpec((1, d), lambda i, *_: (0, 0)),
                      pl.BlockSpec(memory_space=pl.ANY)],
            out_specs=tok,
            scratch_shapes=[pltpu.VMEM((2, ne * SLOT_ROWS, d), BF16),
                            pltpu.SemaphoreType.DMA((2, ne))]),
        out_shape=jax.ShapeDtypeStruct((g, s, d), F32),
        compiler_params=_cparams(("arbitrary",)),
        name="expert_combine",
    )(start, cnt, tot, x1, rank, aff, g2, norm_f_g.reshape(1, d), ye)


def _layer_front(x, batch, seq_len, mods, cond_of_tile, moe_groups, cond_block, p, s0_f, s0_b, rope, emit_state):
    t, d = x.shape
    sh1, sc1, g1, sh2, sc2, g2 = mods
    u, x2c, qkvg, gates = _in_projection(x, p["norm1_g"], sc1, sh1, p["w_in"], p["hy_conv_w"], p["hy_conv_b"],
                                         cond_of_tile, seq_len, rope)
    seq = lambda a: a.reshape(batch, seq_len, a.shape[-1])
    y_hy = _hyena_long_conv(seq(u), seq(x2c), p).reshape(t, -1)
    ret = _retention(qkvg.reshape(batch, seq_len, -1), p["ret_decay_fwd"], p["ret_decay_bwd"], p["ret_gn_g"],
                     s0_f, s0_b, emit_state)
    y_ret = ret[0].reshape(t, -1)
    view = lambda a: a.reshape(moe_groups, t // moe_groups, a.shape[-1])
    x1, h2, aff = _out_projection(view(y_hy), view(y_ret), view(gates), view(x), g1, sc2, sh2, p["norm2_g"],
                                  p["w_hy_o"], p["w_ret_o"], p["w_out"], p["w_router"], cond_block)
    cap = (EC_CAPACITY_FACTOR * t) // N_EXPERTS
    eb = _expert_block(cap)
    rank, cnt, start, tot = _select(aff, cap, moe_groups)
    xs = _gather(h2, rank, start, cnt, tot, cap)
    routed = (xs, tot, _used_rows(cap, rank.shape[0]), eb)
    back = dict(x1=x1, rank=rank, aff=aff, start=start, cnt=cnt, tot=tot, g2=g2, cond_block=cond_block, eb=eb)
    return routed, back, ret[1:]


def _layer_back(back, ye, final_g):
    out = _combine(back["x1"], back["rank"], back["aff"], back["start"], back["cnt"], back["tot"], back["g2"],
                   final_g, ye, back["cond_block"], back["eb"])
    return out.reshape(-1, out.shape[-1])


def kernel(x_prompt, x_sample, state_ret_fwd, state_ret_bwd, c, c_ctx, w_mod, b_mod, norm1_g, w_in, hy_conv_w, hy_conv_b, hy_f_w1, hy_f_b1, hy_f_freq, hy_f_w2, hy_f_b2, hy_f_w3, hy_decay, hy_bias, ret_decay_fwd, ret_decay_bwd, ret_gn_g, w_hy_o, w_ret_o, w_out, norm2_g, w_router, w_e_gate, w_e_up, w_e_down, norm_f_g):
    b, s, d = x_prompt.shape
    bd, sd, _ = x_sample.shape
    depth = w_mod.shape[0]
    assert (b * s) % TOKEN_TILE == 0 and sd % TOKEN_TILE == 0 and b % 2 == 0 and bd % 2 == 0
    ncond = -(-(bd + 1) // 8) * 8
    lat_groups = bd if (TOKEN_TILE % (16 * bd) == 0 and sd % (TOKEN_TILE // bd) == 0) else 1
    assert lat_groups == bd or bd == 1
    conds = jnp.concatenate([c, c_ctx[None], jnp.zeros((ncond - bd - 1, d), F32)], axis=0)
    ctx_cond = lambda i: bd
    lat_cond = lambda i: i // (sd // TOKEN_TILE)
    xp = x_prompt.reshape(b * s, d)
    xs = x_sample.reshape(bd * sd, d)
    new_f, new_b = [], []
    for l in range(depth):
        mod = _modulation(conds, w_mod[l], b_mod[l])
        mods = [mod[:, k * d:(k + 1) * d].reshape(ncond, 1, d) for k in range(6)]
        p = {"norm1_g": norm1_g[l], "w_in": w_in[l].astype(BF16), "hy_conv_w": hy_conv_w[l],
             "hy_conv_b": hy_conv_b[l], "hy_f_w1": hy_f_w1[l], "hy_f_b1": hy_f_b1[l], "hy_f_freq": hy_f_freq[l],
             "hy_f_w2": hy_f_w2[l], "hy_f_b2": hy_f_b2[l], "hy_f_w3": hy_f_w3[l], "hy_decay": hy_decay[l],
             "hy_bias": hy_bias[l], "ret_decay_fwd": ret_decay_fwd[l], "ret_decay_bwd": ret_decay_bwd[l],
             "ret_gn_g": ret_gn_g[l], "w_hy_o": w_hy_o[l].astype(BF16), "w_ret_o": w_ret_o[l].astype(BF16),
             "w_out": w_out[l].astype(BF16), "norm2_g": norm2_g[l], "w_router": w_router[l],
             "w_e_gate": w_e_gate[l], "w_e_up": w_e_up[l], "w_e_down": w_e_down[l]}
        final_g = norm_f_g if l == depth - 1 else None
        routed_p, back_p, (s_f, s_b) = _layer_front(xp, b, s, mods, ctx_cond, 1, bd, p, None, None, False, True)
        new_f.append(s_f)
        new_b.append(s_b)
        routed_s, back_s, _ = _layer_front(xs, bd, sd, mods, lat_cond, lat_groups, 0, p, state_ret_fwd[:, l],
                                           state_ret_bwd[:, l], True, False)
        ye_p, ye_s = _experts([routed_p, routed_s], p["w_e_gate"], p["w_e_up"], p["w_e_down"])
        xp = _layer_back(back_p, ye_p, final_g)
        xs = _layer_back(back_s, ye_s, final_g)
    y_prompt = xp.reshape(b, s, d)
    y_sample = xs.reshape(bd, sd, d)
    return (y_prompt, y_sample, jnp.stack(new_f, axis=1), jnp.stack(new_b, axis=1))
```

```python
import functools
import math

import numpy as np
import jax
import jax.numpy as jnp
from jax import lax
from jax.experimental import pallas as pl
from jax.experimental.pallas import tpu as pltpu

F32 = jnp.float32
BF16 = jnp.bfloat16
HIGHEST = lax.Precision.HIGHEST

EPS = 1e-6
D_HYENA = 512
D_RET = 512
N_RET_HEADS = 4
RET_HEAD_DIM = 128
RET_CHUNK = 128
GRID_W = 64
FILTER_EMB = 33
ROPE_BASE = 10000.0
N_EXPERTS = 16
EC_CAPACITY_FACTOR = 2

TOKEN_TILE = 512
SLOT_ROWS = 96
ROW_ALIGN = 16
HALO_ROWS = 16
FFT_MINOR = 64
FFT_GROUP = 4
VMEM_LIMIT = 56 * 1024 * 1024


def _cparams(sem):
    return pltpu.CompilerParams(dimension_semantics=sem, vmem_limit_bytes=VMEM_LIMIT)


def _silu(x):
    return x * jax.nn.sigmoid(x)


def _mod_kernel(c_ref, w_ref, b_ref, o_ref):
    s = _silu(c_ref[...])
    o_ref[...] = jnp.dot(s, w_ref[...], preferred_element_type=F32, precision=HIGHEST) + b_ref[...]


def _modulation(conds, w_mod, b_mod):
    nc, d = conds.shape
    return pl.pallas_call(
        _mod_kernel,
        grid=(6,),
        in_specs=[pl.BlockSpec((nc, d), lambda j: (0, 0)),
                  pl.BlockSpec((d, d), lambda j: (0, j)),
                  pl.BlockSpec((1, d), lambda j: (0, j))],
        out_specs=pl.BlockSpec((nc, d), lambda j: (0, j)),
        out_shape=jax.ShapeDtypeStruct((nc, 6 * d), F32),
        compiler_params=_cparams(("arbitrary",)),
        name="modulation",
    )(conds, w_mod, b_mod.reshape(1, -1))


def _inproj_kernel(*refs, splits, rope, seq_len, halo):
    refs = list(refs)
    x_ref = refs.pop(0)
    if halo:
        xp_ref, xn_ref = refs.pop(0), refs.pop(0)
    g_ref, sc_ref, sh_ref, w_ref, cw_ref, cb_ref = refs[:6]
    del refs[:6]
    if rope:
        cos_ref, sin_ref = refs.pop(0), refs.pop(0)
    u_ref, x2c_ref, qkvg_ref, gates_ref, cv_sc, cx_sc = refs

    def normed(ref):
        x = ref[...]
        ms = jnp.mean(x * x, axis=-1, keepdims=True)
        h = x * lax.rsqrt(ms + EPS) * g_ref[...]
        return (h * (1.0 + sc_ref[0]) + sh_ref[0]).astype(BF16)

    hb = normed(x_ref)
    tm = hb.shape[0]
    nz, nq, ng = splits
    cw = 512
    dh = RET_HEAD_DIM
    hr = HALO_ROWS
    if halo:
        t0 = pl.program_id(0) * tm
        hp = jnp.where(t0 % seq_len != 0, normed(xp_ref), jnp.zeros((hr, hb.shape[1]), BF16))
        hn = jnp.where((t0 + tm) % seq_len != 0, normed(xn_ref), jnp.zeros((hr, hb.shape[1]), BF16))
        hext = jnp.concatenate([hp, hb, hn], axis=0)

    piece = tm if halo else seq_len
    pitch = piece + hr
    if not halo:
        for s in range(tm // piece + 1):
            cx_sc[s * pitch:s * pitch + hr, :] = jnp.zeros((hr, cw), F32)

    def conv(c0):
        taps = cw_ref[:, c0:c0 + cw]
        bias = cb_ref[:, c0:c0 + cw]
        if halo:
            cx_sc[...] = jnp.dot(hext, w_ref[:, c0:c0 + cw], preferred_element_type=F32)
        else:
            acc = jnp.dot(hb, w_ref[:, c0:c0 + cw], preferred_element_type=F32)
            for s in range(tm // piece):
                cx_sc[hr + s * pitch:hr + s * pitch + piece, :] = acc[s * piece:(s + 1) * piece]
        outs = []
        for s in range(tm // piece):
            r0 = hr + s * pitch
            outs.append(bias + cx_sc[r0 - 1:r0 - 1 + piece, :] * taps[0:1] + cx_sc[r0:r0 + piece, :] * taps[1:2]
                        + cx_sc[r0 + 1:r0 + 1 + piece, :] * taps[2:3])
        return outs[0] if len(outs) == 1 else jnp.concatenate(outs, axis=0)

    cv_sc[...] = conv(0)
    u_ref[...] = (cv_sc[...] * conv(D_HYENA)).astype(BF16)
    x2c_ref[...] = conv(2 * D_HYENA).astype(BF16)

    for c0 in range(nz, nz + nq + ng, cw):
        acc = jnp.dot(hb, w_ref[:, c0:c0 + cw], preferred_element_type=F32)
        if c0 < nz + nq:
            part = (c0 - nz) // D_RET
            if part == 1:
                acc = acc * (dh ** -0.5)
            if rope and part < 2:
                lane = lax.broadcasted_iota(jnp.int32, (acc.shape[0], dh), 1)
                swap_hi = (lane % (dh // 2)) < (dh // 4)
                cs, sn = cos_ref[...], sin_ref[...]
                heads = []
                for hh in range(cw // dh):
                    xh = acc[:, hh * dh:(hh + 1) * dh]
                    rot = jnp.where(swap_hi, pltpu.roll(xh, dh - dh // 4, axis=1), pltpu.roll(xh, dh // 4, axis=1))
                    heads.append(xh * cs + rot * sn)
                acc = jnp.concatenate(heads, axis=1)
            qkvg_ref[:, c0 - nz:c0 - nz + cw] = acc.astype(BF16)
        else:
            gates_ref[:, c0 - nz - nq:c0 - nz - nq + cw] = jax.nn.sigmoid(acc).astype(BF16)


def _in_projection(x, norm_g, sc, sh, w_in_bf, conv_w, conv_b, cond_of_tile, seq_len, rope):
    t, d = x.shape
    splits = (3 * D_HYENA, 4 * D_RET, 2 * d)
    assert D_RET == 512 and D_HYENA == 512
    tm = TOKEN_TILE
    assert seq_len % tm == 0 or tm % seq_len == 0
    halo = seq_len > tm
    row = lambda i: (i, 0)
    cond = lambda i: (cond_of_tile(i), 0, 0)
    full = lambda a: pl.BlockSpec(a.shape, lambda i: (0, 0))
    args, specs = [x], [pl.BlockSpec((tm, d), row)]
    if halo:
        hb_per_tile = tm // HALO_ROWS
        last = t // HALO_ROWS - 1
        args += [x, x]
        specs += [pl.BlockSpec((HALO_ROWS, d), lambda i: (jnp.maximum(i * hb_per_tile - 1, 0), 0)),
                  pl.BlockSpec((HALO_ROWS, d), lambda i: (jnp.minimum((i + 1) * hb_per_tile, last), 0))]
    cbias = conv_b.reshape(1, -1)
    args += [norm_g.reshape(1, d), sc, sh, w_in_bf, conv_w, cbias]
    specs += [pl.BlockSpec((1, d), lambda i: (0, 0)), pl.BlockSpec((1, 1, d), cond), pl.BlockSpec((1, 1, d), cond),
              full(w_in_bf), full(conv_w), full(cbias)]
    if rope:
        tiles_per_seq = seq_len // tm
        args += list(_rope_tables(seq_len))
        specs += [pl.BlockSpec((tm, RET_HEAD_DIM), lambda i: (i % tiles_per_seq, 0))] * 2
    return pl.pallas_call(
        functools.partial(_inproj_kernel, splits=splits, rope=rope, seq_len=seq_len, halo=halo),
        grid=(t // tm,),
        in_specs=specs,
        out_specs=[pl.BlockSpec((tm, D_HYENA), row),
                   pl.BlockSpec((tm, D_HYENA), row),
                   pl.BlockSpec((tm, splits[1]), row),
                   pl.BlockSpec((tm, splits[2]), row)],
        out_shape=[jax.ShapeDtypeStruct((t, D_HYENA), BF16),
                   jax.ShapeDtypeStruct((t, D_HYENA), BF16),
                   jax.ShapeDtypeStruct((t, splits[1]), BF16),
                   jax.ShapeDtypeStruct((t, splits[2]), BF16)],
        scratch_shapes=[pltpu.VMEM((tm, 512), F32),
                        pltpu.VMEM((tm + 2 * HALO_ROWS if halo else (tm // seq_len) * (seq_len + HALO_ROWS) + HALO_ROWS,
                                    512), F32)],
        compiler_params=_cparams(("parallel",)),
        name="in_projection",
    )(*args)


def _filter_features(seq_len):
    t = np.linspace(0.0, 1.0, seq_len, dtype=np.float32)[:, None]
    bands = (FILTER_EMB - 1) // 2
    w = (np.float32(2.0 * math.pi) * np.arange(seq_len, dtype=np.float32)) / np.float32(seq_len)
    f = np.linspace(1e-4, bands - 1, bands, dtype=np.float32)
    ang = (w[:, None] * f[None, :]).astype(np.float64)
    z = np.concatenate([t, np.cos(ang), -np.sin(ang)], axis=-1).astype(np.float32)
    return np.pad(z, ((0, 0), (0, 128 - FILTER_EMB)))


def _filter_kernel(z_ref, w1_ref, b1_ref, fr_ref, w2_ref, b2_ref, w3_ref, dec_ref, o_ref):
    z = z_ref[...]
    fr = fr_ref[...]
    dot = functools.partial(jnp.dot, preferred_element_type=F32, precision=HIGHEST)
    h = jnp.sin(fr * (dot(z, w1_ref[...]) + b1_ref[...]))
    h = jnp.sin(fr * (dot(h, w2_ref[...]) + b2_ref[...]))
    h = dot(h, w3_ref[...]) * jnp.exp(-z[:, 0:1] * jnp.abs(dec_ref[...]))
    rows = h.shape[0]
    grow = pl.program_id(0) * rows + lax.broadcasted_iota(jnp.int32, (rows, 1), 0)
    o_ref[0] = h[:, :D_HYENA].astype(o_ref.dtype)
    o_ref[1] = jnp.zeros((rows, D_HYENA), o_ref.dtype)
    o_ref[2] = jnp.where(grow == 0, 0.0, h[:, D_HYENA:]).astype(o_ref.dtype)
    o_ref[3] = jnp.zeros((rows, D_HYENA), o_ref.dtype)


def _hyena_filters(seq_len, p, out_dtype):
    z = jnp.asarray(_filter_features(seq_len))
    w1 = jnp.pad(p["hy_f_w1"], ((0, 128 - FILTER_EMB), (0, 0)))
    rows = min(seq_len, 512)
    full = lambda a: pl.BlockSpec(a.shape, lambda i: (0,) * a.ndim)
    ops = [w1, p["hy_f_b1"].reshape(1, -1), p["hy_f_freq"].reshape(1, -1), p["hy_f_w2"],
           p["hy_f_b2"].reshape(1, -1), p["hy_f_w3"], p["hy_decay"].reshape(1, -1)]
    return pl.pallas_call(
        _filter_kernel,
        grid=(seq_len // rows,),
        in_specs=[pl.BlockSpec((rows, 128), lambda i: (i, 0))] + [full(a) for a in ops],
        out_specs=pl.BlockSpec((4, rows, D_HYENA), lambda i: (0, i, 0)),
        out_shape=jax.ShapeDtypeStruct((4, seq_len, D_HYENA), out_dtype),
        compiler_params=_cparams(("arbitrary",)),
        name="hyena_filter",
    )(z, *ops)


def _stacked_dft(n_out, n_in, modulus, scale=1.0):
    k = np.arange(n_out, dtype=np.int64)[:, None]
    n = np.arange(n_in, dtype=np.int64)[None, :]
    th = ((k * n) % modulus) * (2.0 * math.pi / modulus)
    c, s = np.cos(th), np.sin(th)
    fwd = np.block([[c, s], [-s, c]]).astype(np.float32)
    inv = (np.block([[c.T, -s.T], [s.T, c.T]]) * scale).astype(np.float32)
    return fwd, inv


def _second_level_tables(n_total, n1, n2):
    k1 = np.arange(n1, dtype=np.int64)[:, None, None]
    k2 = np.arange(n2, dtype=np.int64)[None, :, None]
    m = np.arange(n2, dtype=np.int64)[None, None, :]
    th = ((m * (k1 + n1 * k2)) % n_total) * (2.0 * math.pi / n_total)
    c, s = np.cos(th), np.sin(th)
    g = np.concatenate([np.concatenate([c, s], axis=2), np.concatenate([-s, c], axis=2)], axis=1)
    return g.astype(np.float32), np.swapaxes(g, 1, 2).astype(np.float32)


def _skewed(n):
    return n + 1


def _fft1_kernel(h_ref, f_ref, a_ref, u_sc, t_sc):
    n2 = FFT_MINOR
    _, seq_len, cb = h_ref.shape
    h1 = seq_len // n2
    nk = a_ref.shape[1]
    up, tp = _skewed(n2), _skewed(nk)
    for b in range(2):
        for q in range(h1):
            u_sc[b, q * up:q * up + n2, :] = h_ref[b, q * n2:(q + 1) * n2, :].astype(F32)

    def fwd(gi, carry):
        m0 = gi * FFT_GROUP
        cols = []
        for d in range(FFT_GROUP):
            xr = u_sc[0, pl.ds(m0 + d, h1, stride=up), :]
            xi = u_sc[1, pl.ds(m0 + d, h1, stride=up), :]
            cols.append(jnp.concatenate([xr, xi], axis=0))
        x = jnp.concatenate(cols, axis=1).astype(BF16)
        res = jnp.dot(f_ref[...], x, preferred_element_type=F32)
        for d in range(FFT_GROUP):
            t_sc[pl.ds((m0 + d) * tp, nk), :] = res[:, d * cb:(d + 1) * cb]
        return carry

    lax.fori_loop(0, n2 // FFT_GROUP, fwd, 0, unroll=2)

    def transpose(k, carry):
        a_ref[0, k] = t_sc[pl.ds(k, n2, stride=tp), :].astype(a_ref.dtype)
        return carry

    lax.fori_loop(0, nk, transpose, 0, unroll=8)


def _fft_first_level(h, f1):
    b, seq_len, c = h.shape
    cb = 128
    return pl.pallas_call(
        _fft1_kernel,
        grid=(b // 2, c // cb),
        in_specs=[pl.BlockSpec((2, seq_len, cb), lambda i, j: (i, 0, j)),
                  pl.BlockSpec(f1.shape, lambda i, j: (0, 0))],
        out_specs=pl.BlockSpec((1, f1.shape[0], FFT_MINOR, cb), lambda i, j: (i, 0, 0, j)),
        out_shape=jax.ShapeDtypeStruct((b // 2, f1.shape[0], FFT_MINOR, c), BF16),
        scratch_shapes=[pltpu.VMEM((2, (seq_len // FFT_MINOR) * _skewed(FFT_MINOR), cb), F32),
                        pltpu.VMEM((FFT_MINOR * _skewed(f1.shape[0]), cb), F32)],
        compiler_params=_cparams(("parallel", "parallel")),
        name="hyena_dft_level1",
    )(h, f1)


def _fft_s2f_kernel(a_ref, g_ref, kr_ref, ki_ref):
    _, _, kb, n2, c = a_ref.shape
    for kk in range(kb):
        g = g_ref[kk]
        hf = jnp.dot(g, a_ref[0, :, kk].reshape(2 * n2, c), preferred_element_type=F32)
        hb = jnp.dot(g, a_ref[1, :, kk].reshape(2 * n2, c), preferred_element_type=F32)
        kr_ref[kk] = hf[:n2] + hb[:n2]
        ki_ref[kk] = hf[n2:] - hb[n2:]


def _fft_filter_second_level(a, g, kb=8):
    _, _, n1, n2, c = a.shape
    spec = pl.BlockSpec((kb, n2, c), lambda i: (i, 0, 0))
    return pl.pallas_call(
        _fft_s2f_kernel,
        grid=(n1 // kb,),
        in_specs=[pl.BlockSpec((2, 2, kb, n2, c), lambda i: (0, 0, i, 0, 0)),
                  pl.BlockSpec((kb, 2 * n2, 2 * n2), lambda i: (i, 0, 0))],
        out_specs=[spec, spec],
        out_shape=[jax.ShapeDtypeStruct((n1, n2, c), F32)] * 2,
        compiler_params=_cparams(("parallel",)),
        name="hyena_filter_spectrum",
    )(a, g)


def _fft_s2_kernel(a_ref, g_ref, gt_ref, kr_ref, ki_ref, b_ref, r_sc):
    npairs, _, kb, n2, cb = a_ref.shape
    nq = cb // 128
    rp = _skewed(2 * n2)
    for kk in range(kb):
        a = jnp.concatenate([a_ref[p, :, kk].reshape(2 * n2, cb) for p in range(npairs)], axis=1)
        x = jnp.dot(g_ref[kk], a, preferred_element_type=F32)
        xr, xi = x[:n2], x[n2:]
        kr = jnp.concatenate([kr_ref[kk]] * npairs, axis=1)
        ki = jnp.concatenate([ki_ref[kk]] * npairs, axis=1)
        y = jnp.concatenate([xr * kr - xi * ki, xr * ki + xi * kr], axis=0).astype(BF16)
        res = jnp.dot(gt_ref[kk], y, preferred_element_type=F32)
        for q in range(npairs * nq):
            r_sc[q, kk * rp:kk * rp + 2 * n2, :] = res[:, q * 128:(q + 1) * 128]

    for p in range(npairs):
        def gather(row):
            return jnp.concatenate([r_sc[p * nq + q, pl.ds(row, kb, stride=rp), :] for q in range(nq)], axis=1)

        def transpose(m, carry):
            b_ref[p, 0, m] = gather(m).astype(b_ref.dtype)
            b_ref[p, 1, m] = gather(n2 + m).astype(b_ref.dtype)
            return carry

        lax.fori_loop(0, n2, transpose, 0, unroll=8)


def _fft_second_level(a, g, gt, kr, ki):
    p, _, n1, n2, c = a.shape
    kb, cb = 16, 256
    gspec = pl.BlockSpec((kb, 2 * n2, 2 * n2), lambda i, j: (i, 0, 0))
    kspec = pl.BlockSpec((kb, n2, cb), lambda i, j: (i, 0, j))
    return pl.pallas_call(
        _fft_s2_kernel,
        grid=(n1 // kb, c // cb),
        in_specs=[pl.BlockSpec((p, 2, kb, n2, cb), lambda i, j: (0, 0, i, 0, j)), gspec, gspec, kspec, kspec],
        out_specs=pl.BlockSpec((p, 2, n2, kb, cb), lambda i, j: (0, 0, 0, i, j)),
        out_shape=jax.ShapeDtypeStruct((p, 2, n2, n1, c), BF16),
        scratch_shapes=[pltpu.VMEM((p * cb // 128, kb * _skewed(2 * n2), 128), F32)],
        compiler_params=_cparams(("parallel", "parallel")),
        name="hyena_dft_level2",
    )(a, g, gt, kr, ki)


def _fft_s3_kernel(b_ref, f_ref, u_ref, x2_ref, bias_ref, o_ref, t_sc):
    _, _, n2, n1, cb = b_ref.shape
    nr = f_ref.shape[0]
    h1 = nr // 2
    tp = _skewed(nr)

    def inv(gi, carry):
        m0 = gi * FFT_GROUP
        x = jnp.concatenate([b_ref[0, :, m0 + d].reshape(2 * n1, cb) for d in range(FFT_GROUP)], axis=1)
        res = jnp.dot(f_ref[...], x, preferred_element_type=F32)
        for d in range(FFT_GROUP):
            t_sc[pl.ds((m0 + d) * tp, nr), :] = res[:, d * cb:(d + 1) * cb]
        return carry

    lax.fori_loop(0, n2 // FFT_GROUP, inv, 0, unroll=2)

    bias = bias_ref[...]
    for b in range(2):
        def finish(q, carry):
            rows = pl.ds(pl.multiple_of(q * n2, n2), n2)
            conv = t_sc[pl.ds(b * h1 + q, n2, stride=tp), :]
            u = u_ref[b, rows, :].astype(F32)
            o_ref[b, rows, :] = ((conv + u * bias) * x2_ref[b, rows, :].astype(F32)).astype(o_ref.dtype)
            return carry

        lax.fori_loop(0, h1, finish, 0, unroll=4)


def _fft_last_level(bt, f1inv, u, x2c, bias):
    p, _, n2, n1, c = bt.shape
    b, seq_len, _ = u.shape
    cb = 128
    uspec = pl.BlockSpec((2, seq_len, cb), lambda i, j: (i, 0, j))
    return pl.pallas_call(
        _fft_s3_kernel,
        grid=(p, c // cb),
        in_specs=[pl.BlockSpec((1, 2, n2, n1, cb), lambda i, j: (i, 0, 0, 0, j)),
                  pl.BlockSpec(f1inv.shape, lambda i, j: (0, 0)),
                  uspec, uspec,
                  pl.BlockSpec((1, cb), lambda i, j: (0, j))],
        out_specs=uspec,
        out_shape=jax.ShapeDtypeStruct(u.shape, BF16),
        scratch_shapes=[pltpu.VMEM((n2 * _skewed(f1inv.shape[0]), cb), F32)],
        compiler_params=_cparams(("parallel", "parallel")),
        name="hyena_dft_inverse",
    )(bt, f1inv, u, x2c, bias)


def _kf_direct_kernel(h_ref, f_ref, kr_ref, ki_ref):
    n = kr_ref.shape[0]
    dot = functools.partial(jnp.dot, preferred_element_type=F32, precision=HIGHEST)
    hf = dot(f_ref[...], h_ref[0])
    hb = dot(f_ref[...], h_ref[2])
    kr_ref[...] = hf[:n] + hb[:n]
    ki_ref[...] = hf[n:] - hb[n:]


def _hyena_direct_kernel(u_ref, x2_ref, f_ref, fi_ref, kr_ref, ki_ref, bias_ref, o_ref):
    _, two, seq_len, cb = u_ref.shape
    n = kr_ref.shape[0]
    ub = u_ref[0].reshape(two * seq_len, cb)
    x = jnp.dot(f_ref[...], ub, preferred_element_type=F32)
    xr, xi = x[:n], x[n:]
    kr, ki = kr_ref[...], ki_ref[...]
    y = jnp.concatenate([xr * kr - xi * ki, xr * ki + xi * kr], axis=0).astype(BF16)
    conv = jnp.dot(fi_ref[...], y, preferred_element_type=F32)
    x2 = x2_ref[0].reshape(two * seq_len, cb).astype(F32)
    o = (conv + ub.astype(F32) * bias_ref[...]) * x2
    o_ref[0] = o.reshape(two, seq_len, cb).astype(o_ref.dtype)


def _hyena_long_conv(u, x2c, p):
    b, seq_len, _ = u.shape
    c = D_HYENA
    n = 2 * seq_len
    bias = p["hy_bias"].reshape(1, c)
    hh = _hyena_filters(seq_len, p, F32)
    if seq_len <= 512:
        fwd, inv = _stacked_dft(n, seq_len, n, scale=1.0 / n)
        kr, ki = pl.pallas_call(
            _kf_direct_kernel,
            out_shape=[jax.ShapeDtypeStruct((n, c), F32)] * 2,
            compiler_params=_cparams(None),
            name="hyena_filter_spectrum_direct",
        )(hh, fwd[:, :seq_len])
        cb = 256
        pair = lambda a: a.reshape(b // 2, 2, seq_len, c)
        uspec = pl.BlockSpec((1, 2, seq_len, cb), lambda i, j: (i, 0, 0, j))
        kspec = pl.BlockSpec((n, cb), lambda i, j: (0, j))
        out = pl.pallas_call(
            _hyena_direct_kernel,
            grid=(b // 2, c // cb),
            in_specs=[uspec, uspec,
                      pl.BlockSpec(fwd.shape, lambda i, j: (0, 0)),
                      pl.BlockSpec(inv.shape, lambda i, j: (0, 0)),
                      kspec, kspec,
                      pl.BlockSpec((1, cb), lambda i, j: (0, j))],
            out_specs=uspec,
            out_shape=jax.ShapeDtypeStruct((b // 2, 2, seq_len, c), BF16),
            compiler_params=_cparams(("parallel", "parallel")),
            name="hyena_dft_direct",
        )(pair(u), pair(x2c), jnp.asarray(fwd).astype(BF16), jnp.asarray(inv).astype(BF16), kr, ki, bias)
        return out.reshape(b, seq_len, c)

    n2 = FFT_MINOR
    n1 = n // n2
    h1 = seq_len // n2
    f1, f1inv = _stacked_dft(n1, h1, n1, scale=1.0 / n)
    f1, f1inv = jnp.asarray(f1).astype(BF16), jnp.asarray(f1inv).astype(BF16)
    g, gt = _second_level_tables(n, n1, n2)
    g, gt = jnp.asarray(g).astype(BF16), jnp.asarray(gt).astype(BF16)
    ha = _fft_first_level(hh, f1)
    kr, ki = _fft_filter_second_level(ha.reshape(2, 2, n1, n2, c), g)
    a = _fft_first_level(u, f1)
    bt = _fft_second_level(a.reshape(b // 2, 2, n1, n2, c), g, gt, kr, ki)
    return _fft_last_level(bt, f1inv, u, x2c, bias)


def _rope_tables(seq_len):
    half = RET_HEAD_DIM // 2
    nf = half // 2
    t = np.arange(seq_len)
    inv = ROPE_BASE ** (-np.arange(nf, dtype=np.float64) / nf)
    ar = (t // GRID_W)[:, None] * inv[None, :]
    ac = (t % GRID_W)[:, None] * inv[None, :]
    cos = np.concatenate([np.cos(ar), np.cos(ar), np.cos(ac), np.cos(ac)], axis=-1)
    sin = np.concatenate([-np.sin(ar), np.sin(ar), -np.sin(ac), np.sin(ac)], axis=-1)
    return cos.astype(np.float32), sin.astype(np.float32)


def _log_sigmoid(x):
    return jnp.minimum(x, 0.0) - jnp.log1p(jnp.exp(-jnp.abs(x)))


def _retention_kernel(*refs, has_init, emit_state, cpb):
    refs = list(refs)
    q_ref, k_ref, v_ref, g_ref, dec_ref, gn_ref = refs[:6]
    del refs[:6]
    if has_init:
        s0f_ref, s0b_ref = refs[:2]
        del refs[:2]
    o_ref = refs.pop(0)
    if emit_state:
        sf_out, sb_out = refs[:2]
        del refs[:2]
    sf_ref, sb_ref, sball_ref = refs

    c = RET_CHUNK
    dh = RET_HEAD_DIM
    nh = N_RET_HEADS
    phase = pl.program_id(1)
    j = pl.program_id(2)
    nb = pl.num_programs(2)

    ri = lax.broadcasted_iota(jnp.int32, (c, c), 0).astype(F32)
    ci = lax.broadcasted_iota(jnp.int32, (c, c), 1).astype(F32)
    diff = ri - ci

    def head_consts(h):
        lgf = _log_sigmoid(dec_ref[0, h])[0:1, :]
        lgb = _log_sigmoid(dec_ref[1, h])[0:1, :]
        return lgf, lgb

    bb = q_ref.shape[0]
    heads = [(bi, h) for bi in range(bb) for h in range(nh)]

    def load(ref, bi, r0, h):
        return ref[bi, r0:r0 + c, h * dh:(h + 1) * dh]

    dn_t = (((0,), (0,)), ((), ()))
    dn_nt = (((1,), (1,)), ((), ()))

    @pl.when(phase == 0)
    def _backward_sweep():
        @pl.when(j == 0)
        def _():
            for bi, h in heads:
                sb_ref[bi * nh + h] = s0b_ref[bi, h] if has_init else jnp.zeros((dh, dh), F32)

        blk = nb - 1 - j
        for bi, h in heads:
            hs = bi * nh + h
            _, lgb = head_consts(h)
            zeta_b = jnp.exp(lgb * ri)
            cdec_b = jnp.exp(lgb * float(c))
            for cc in reversed(range(cpb)):
                r0 = cc * c
                n = blk * cpb + cc
                s = sb_ref[hs]
                sball_ref[n, hs] = s.astype(BF16)
                kz = (load(k_ref, bi, r0, h).astype(F32) * zeta_b).astype(BF16)
                vv = load(v_ref, bi, r0, h)
                sb_ref[hs] = cdec_b * s + lax.dot_general(kz, vv, dn_t, preferred_element_type=F32)

        if emit_state:
            @pl.when(j == nb - 1)
            def _():
                for bi, h in heads:
                    sb_out[bi, h] = sb_ref[bi * nh + h]

    @pl.when(phase == 1)
    def _forward_sweep():
        @pl.when(j == 0)
        def _():
            for bi, h in heads:
                sf_ref[bi * nh + h] = s0f_ref[bi, h] if has_init else jnp.zeros((dh, dh), F32)

        for bi, h in heads:
            hs = bi * nh + h
            lgf, lgb = head_consts(h)
            mask = (jnp.where(diff >= 0, jnp.exp(lgf * jnp.maximum(diff, 0.0)), 0.0)
                    + jnp.where(diff <= 0, jnp.exp(lgb * jnp.maximum(-diff, 0.0)), 0.0))
            xi_f = jnp.exp(lgf * (ri + 1.0))
            xi_b = jnp.exp(lgb * (float(c) - ri))
            zeta_f = jnp.exp(lgf * (float(c - 1) - ri))
            cdec_f = jnp.exp(lgf * float(c))
            gn = gn_ref[:, h * dh:(h + 1) * dh]
            for cc in range(cpb):
                r0 = cc * c
                n = j * cpb + cc
                qb = load(q_ref, bi, r0, h)
                kb = load(k_ref, bi, r0, h)
                vv = load(v_ref, bi, r0, h)
                gate = load(g_ref, bi, r0, h).astype(F32)
                sc = lax.dot_general(qb, kb, dn_nt, preferred_element_type=F32)
                inner = jnp.dot((sc * mask).astype(BF16), vv, preferred_element_type=F32)
                s = sf_ref[hs]
                q = qb.astype(F32)
                lhs = jnp.concatenate([q * xi_f, q * xi_b], axis=1).astype(BF16)
                rhs = jnp.concatenate([s.astype(BF16), sball_ref[n, hs]], axis=0)
                o = inner + jnp.dot(lhs, rhs, preferred_element_type=F32)
                mu = jnp.mean(o, axis=-1, keepdims=True)
                d = o - mu
                var = jnp.mean(d * d, axis=-1, keepdims=True)
                y = d * lax.rsqrt(var + EPS) * gn * _silu(gate)
                o_ref[bi, r0:r0 + c, h * dh:(h + 1) * dh] = y.astype(o_ref.dtype)
                kz = (kb.astype(F32) * zeta_f).astype(BF16)
                sf_ref[hs] = cdec_f * s + lax.dot_general(kz, vv, dn_t, preferred_element_type=F32)

        if emit_state:
            @pl.when(j == nb - 1)
            def _():
                for bi, h in heads:
                    sf_out[bi, h] = sf_ref[bi * nh + h]


def _retention(qkvg, dec_f, dec_b, gn_g, s0_f, s0_b, emit_state):
    b, seq_len, _ = qkvg.shape
    nh, dh, c = N_RET_HEADS, RET_HEAD_DIM, RET_CHUNK
    rb = min(seq_len, 1024)
    nb = seq_len // rb
    cpb = rb // c
    bb = max(1, min(4, 1024 // seq_len))
    while b % bb:
        bb //= 2
    has_init = s0_f is not None
    dec = jnp.broadcast_to(jnp.stack([dec_f, dec_b])[:, :, None, None], (2, nh, 8, 128)).astype(F32)
    kv_blk = lambda i, p, j: jnp.where(p == 0, nb - 1 - j, j)
    q_blk = lambda i, p, j: jnp.where(p == 0, 0, j)
    in_specs = [pl.BlockSpec((bb, rb, D_RET), lambda i, p, j: (i, q_blk(i, p, j), 0)),
                pl.BlockSpec((bb, rb, D_RET), lambda i, p, j: (i, kv_blk(i, p, j), 1)),
                pl.BlockSpec((bb, rb, D_RET), lambda i, p, j: (i, kv_blk(i, p, j), 2)),
                pl.BlockSpec((bb, rb, D_RET), lambda i, p, j: (i, q_blk(i, p, j), 3)),
                pl.BlockSpec((2, nh, 8, 128), lambda i, p, j: (0, 0, 0, 0)),
                pl.BlockSpec((1, D_RET), lambda i, p, j: (0, 0))]
    args = [qkvg, qkvg, qkvg, qkvg, dec, gn_g.reshape(1, -1)]
    sspec = pl.BlockSpec((bb, nh, dh, dh), lambda i, p, j: (i, 0, 0, 0))
    if has_init:
        in_specs += [sspec, sspec]
        args += [s0_f, s0_b]
    out_specs = [pl.BlockSpec((bb, rb, D_RET), lambda i, p, j: (i, q_blk(i, p, j), 0))]
    out_shape = [jax.ShapeDtypeStruct((b, seq_len, D_RET), BF16)]
    if emit_state:
        out_specs += [sspec, sspec]
        out_shape += [jax.ShapeDtypeStruct((b, nh, dh, dh), F32)] * 2
    return pl.pallas_call(
        functools.partial(_retention_kernel, has_init=has_init, emit_state=emit_state, cpb=cpb),
        grid=(b // bb, 2, nb),
        in_specs=in_specs,
        out_specs=out_specs,
        out_shape=out_shape,
        scratch_shapes=[pltpu.VMEM((bb * nh, dh, dh), F32), pltpu.VMEM((bb * nh, dh, dh), F32),
                        pltpu.VMEM((nb * cpb, bb * nh, dh, dh), BF16)],
        compiler_params=_cparams(("parallel", "arbitrary", "arbitrary")),
        name="retention",
    )(*args)


def _outproj_kernel(yhy_ref, yret_ref, gates_ref, x_ref, g1_ref, sc_ref, sh_ref, ng_ref,
                    why_ref, wret_ref, wout_ref, wr_ref, x1_ref, h2_ref, aff_ref):
    g, m, d = x_ref.shape
    rows = lambda ref: ref[...].reshape(g * m, ref.shape[2])
    a = jnp.dot(rows(yhy_ref), why_ref[...], preferred_element_type=F32)
    b = jnp.dot(rows(yret_ref), wret_ref[...], preferred_element_type=F32)
    gates = rows(gates_ref)
    merged = gates[:, :d].astype(F32) * a + gates[:, d:].astype(F32) * b
    out = jnp.dot(merged.astype(BF16), wout_ref[...], preferred_element_type=F32)
    x1 = x_ref[...] + g1_ref[...] * out.reshape(g, m, d)
    x1_ref[...] = x1
    ms = jnp.mean(x1 * x1, axis=-1, keepdims=True)
    h = x1 * lax.rsqrt(ms + EPS) * ng_ref[...]
    h = (h * (1.0 + sc_ref[...]) + sh_ref[...]).reshape(g * m, d)
    h_hi = h.astype(BF16)
    h2_ref[...] = h_hi.reshape(g, m, d)
    h_lo = (h - h_hi.astype(F32)).astype(BF16)
    t = jnp.dot(h_hi, wr_ref[...], preferred_element_type=F32)
    logits = t[:, :128] + t[:, 128:] + jnp.dot(h_lo, wr_ref[:, :128], preferred_element_type=F32)
    lane = lax.broadcasted_iota(jnp.int32, logits.shape, 1)
    logits = jnp.where(lane < N_EXPERTS, logits, -jnp.inf)
    e = jnp.exp(logits - jnp.max(logits, axis=-1, keepdims=True))
    aff = e / jnp.sum(e, axis=-1, keepdims=True)
    aff_ref[0] = aff.T[:N_EXPERTS, :]


def _out_projection(y_hy, y_ret, gates, x, g1, sc2, sh2, norm2_g, w_hy_o, w_ret_o, w_out, w_router, cond_block):
    g, s, d = x.shape
    tm = TOKEN_TILE
    m = tm // g
    tok = lambda c: pl.BlockSpec((g, m, c), lambda i: (0, i, 0))
    cond = pl.BlockSpec((g, 1, d), lambda i: (cond_block, 0, 0))
    full = lambda a: pl.BlockSpec(a.shape, lambda i: (0, 0))
    wr = jnp.pad(w_router, ((0, 0), (0, 128 - N_EXPERTS)))
    wr_hi = wr.astype(BF16)
    wr = jnp.concatenate([wr_hi, (wr - wr_hi.astype(F32)).astype(BF16)], axis=1)
    return pl.pallas_call(
        _outproj_kernel,
        grid=(s // m,),
        in_specs=[tok(D_HYENA), tok(D_RET), tok(2 * d), tok(d), cond, cond, cond,
                  pl.BlockSpec((1, d), lambda i: (0, 0)),
                  full(w_hy_o), full(w_ret_o), full(w_out), full(wr)],
        out_specs=[tok(d), tok(d), pl.BlockSpec((1, N_EXPERTS, tm), lambda i: (i, 0, 0))],
        out_shape=[jax.ShapeDtypeStruct((g, s, d), F32), jax.ShapeDtypeStruct((g, s, d), BF16),
                   jax.ShapeDtypeStruct((s // m, N_EXPERTS, tm), F32)],
        compiler_params=_cparams(("parallel",)),
        name="out_projection_router",
    )(y_hy, y_ret, gates, x, g1, sc2, sh2, norm2_g.reshape(1, d), w_hy_o, w_ret_o, w_out, wr)


def _select_kernel(aff_ref, tri_ref, rank_ref, cnt_ref, start_ref, tot_ref, *, cap, idx_bits, groups):
    nt, ne, tm = aff_ref.shape
    a = aff_ref[...]

    def count(m):
        return jnp.sum(jnp.sum(m, axis=0, keepdims=True), axis=2, keepdims=True)

    def thr_step(s, thr):
        cand = thr | (1 << (30 - s))
        cnt = count(jnp.where(a >= pltpu.bitcast(cand, F32), 1.0, 0.0))
        return jnp.where(cnt >= float(cap), cand, thr)

    thr = pltpu.bitcast(lax.fori_loop(0, 31, thr_step, jnp.zeros((1, ne, 1), jnp.int32)), F32)
    gt = a > thr
    eq = a == thr
    need = float(cap) - count(jnp.where(gt, 1.0, 0.0))
    m = tm // groups
    tile = lax.broadcasted_iota(jnp.int32, (nt, 1, tm), 0)
    lane = lax.broadcasted_iota(jnp.int32, (nt, 1, tm), 2)
    idx = (lane // m) * (nt * m) + tile * m + lane % m

    def idx_step(s, lim):
        cand = lim | (1 << (idx_bits - 1 - s))
        cnt = count(jnp.where(eq, jnp.where(idx < cand, 1.0, 0.0), 0.0))
        return jnp.where(cnt < need, cand, lim)

    lim = lax.fori_loop(0, idx_bits, idx_step, jnp.zeros((1, ne, 1), jnp.int32))
    sel = jnp.where(gt, 1.0, jnp.where(eq, jnp.where(idx <= lim, 1.0, 0.0), 0.0))
    sel2 = sel.reshape(nt * ne, tm).astype(BF16)
    prefix = jnp.dot(sel2, tri_ref[...], preferred_element_type=F32)
    rank = jnp.where(sel2 > 0, prefix, -1.0).astype(jnp.int32)
    rank_ref[...] = rank.reshape(nt, ne, tm)
    ones = jnp.ones((tm, 128), BF16)
    cnt = jnp.dot(sel2, ones, preferred_element_type=F32).astype(jnp.int32).reshape(nt, ne, 128)
    cnt_ref[...] = cnt
    acc = jnp.zeros((ne, 128), jnp.int32)
    for t in range(nt):
        start_ref[t] = acc
        acc = acc + ((cnt[t] + (ROW_ALIGN - 1)) & (-ROW_ALIGN))
    tot_ref[...] = acc


def _select(aff, cap, groups):
    nt, ne, tm = aff.shape
    idx_bits = max(1, int(math.ceil(math.log2(nt * tm))))
    r = lax.broadcasted_iota(jnp.int32, (tm, tm), 0)
    c = lax.broadcasted_iota(jnp.int32, (tm, tm), 1)
    tri = (r < c).astype(BF16)
    rank, cnt, start, tot = pl.pallas_call(
        functools.partial(_select_kernel, cap=cap, idx_bits=idx_bits, groups=groups),
        out_shape=[jax.ShapeDtypeStruct((nt, ne, tm), jnp.int32),
                   jax.ShapeDtypeStruct((nt, ne, 128), jnp.int32),
                   jax.ShapeDtypeStruct((nt, ne, 128), jnp.int32),
                   jax.ShapeDtypeStruct((ne, 128), jnp.int32)],
        compiler_params=_cparams(None),
        name="expert_choice_select",
    )(aff, tri)
    return rank, cnt[:, :, 0], start[:, :, 0], tot[:, 0]


def _expert_block(cap):
    return 512 if cap >= 2048 else 256


def _used_rows(cap, nt):
    eb = _expert_block(cap)
    return -(-(cap + ROW_ALIGN * nt) // eb) * eb


def _list_rows(cap, nt):
    max_rounds = -(-TOKEN_TILE // SLOT_ROWS)
    return _used_rows(cap, nt) + max(_expert_block(cap), max_rounds * SLOT_ROWS)


def _num_rounds(cnt_sm, i):
    m = cnt_sm[i, 0]
    for e in range(1, N_EXPERTS):
        m = jnp.maximum(m, cnt_sm[i, e])
    return jnp.maximum((m + SLOT_ROWS - 1) // SLOT_ROWS, 1)


def _gather_kernel(start_sm, cnt_sm, tot_sm, h_ref, rank_ref, xs_hbm, stage, zbuf, sem):
    i = pl.program_id(0)
    ne = N_EXPERTS
    slot = i % 2
    rank = rank_ref[0]
    sub = lax.broadcasted_iota(jnp.int32, (SLOT_ROWS, rank.shape[1]), 0)

    def copy(s, e, off):
        return pltpu.make_async_copy(stage.at[s, pl.ds(e * SLOT_ROWS, SLOT_ROWS)],
                                     xs_hbm.at[e, pl.ds(off, SLOT_ROWS)], sem.at[e])

    def fill(r):
        h = h_ref[...].reshape(rank.shape[1], h_ref.shape[2])
        for e in range(ne):
            onehot = jnp.where(rank[e:e + 1, :] == sub + r * SLOT_ROWS, 1.0, 0.0).astype(BF16)
            stage[slot, e * SLOT_ROWS:(e + 1) * SLOT_ROWS, :] = jnp.dot(
                onehot, h, preferred_element_type=F32).astype(BF16)

    def start_all(r):
        for e in range(ne):
            copy(slot, e, pl.multiple_of(start_sm[i, e] + r * SLOT_ROWS, ROW_ALIGN)).start()

    def wait_all(s):
        for e in range(ne):
            copy(s, e, 0).wait()

    fill(0)

    @pl.when(i > 0)
    def _():
        wait_all(1 - slot)

    start_all(0)

    def extra_round(r, carry):
        wait_all(slot)
        fill(r)
        start_all(r)
        return carry

    lax.fori_loop(1, _num_rounds(cnt_sm, i), extra_round, 0)

    @pl.when(i == pl.num_programs(0) - 1)
    def _zero_tail():
        wait_all(slot)
        zbuf[...] = jnp.zeros(zbuf.shape, zbuf.dtype)

        def zcopy(e):
            off = pl.multiple_of(tot_sm[e], ROW_ALIGN)
            return pltpu.make_async_copy(zbuf, xs_hbm.at[e, pl.ds(off, zbuf.shape[0])], sem.at[e])

        for e in range(ne):
            zcopy(e).start()
        for e in range(ne):
            zcopy(e).wait()


def _gather(h2, rank, start, cnt, tot, cap):
    g, _, d = h2.shape
    nt, ne, tm = rank.shape
    rl = _list_rows(cap, nt)
    return pl.pallas_call(
        _gather_kernel,
        grid_spec=pltpu.PrefetchScalarGridSpec(
            num_scalar_prefetch=3,
            grid=(nt,),
            in_specs=[pl.BlockSpec((g, tm // g, d), lambda i, *_: (0, i, 0)),
                      pl.BlockSpec((1, ne, tm), lambda i, *_: (i, 0, 0))],
            out_specs=pl.BlockSpec(memory_space=pl.ANY),
            scratch_shapes=[pltpu.VMEM((2, ne * SLOT_ROWS, d), BF16),
                            pltpu.VMEM((_expert_block(cap), d), BF16),
                            pltpu.SemaphoreType.DMA((ne,))]),
        out_shape=jax.ShapeDtypeStruct((ne, rl, d), BF16),
        compiler_params=_cparams(("arbitrary",)),
        name="expert_gather",
    )(start, cnt, tot, h2, rank)


def _expert_kernel(*refs, nblocks):
    ns = len(nblocks)
    tot_sms, xs_refs = refs[:ns], refs[ns:2 * ns]
    wg_ref, wu_ref, wd_ref = refs[2 * ns:2 * ns + 3]
    ye_refs = refs[2 * ns + 3:3 * ns + 3]
    wg_bf, wu_bf, wd_bf = refs[3 * ns + 3:]
    e = pl.program_id(0)
    j = pl.program_id(1)

    @pl.when(j == 0)
    def _():
        wg_bf[...] = wg_ref[0].astype(BF16)
        wu_bf[...] = wu_ref[0].astype(BF16)
        wd_bf[...] = wd_ref[0].astype(BF16)

    base = 0
    for s in range(ns):
        jj = j - base
        live = jnp.logical_and(jnp.logical_and(jj >= 0, jj < nblocks[s]), jj * xs_refs[s].shape[1] < tot_sms[s][e])

        @pl.when(live)
        def _(xs_ref=xs_refs[s], ye_ref=ye_refs[s]):
            x = xs_ref[0]
            g = jnp.dot(x, wg_bf[...], preferred_element_type=F32)
            u = jnp.dot(x, wu_bf[...], preferred_element_type=F32)
            hid = (_silu(g) * u).astype(BF16)
            ye_ref[0] = jnp.dot(hid, wd_bf[...], preferred_element_type=F32).astype(ye_ref.dtype)

        base += nblocks[s]


def _experts(sets, w_gate, w_up, w_down):
    ns = len(sets)
    ne, _, d = sets[0][0].shape
    f = w_gate.shape[2]
    nblocks = tuple(used // eb for _, _, used, eb in sets)

    def block_spec(s):
        eb = sets[s][3]
        base = sum(nblocks[:s])

        def index(e, j, *tots):
            return (e, jnp.clip(j - base, 0, (tots[s][e] - 1) // eb), 0)

        return pl.BlockSpec((1, eb, d), index)

    wspec = lambda shape: pl.BlockSpec(shape, lambda e, j, *tots: (e, 0, 0))
    return pl.pallas_call(
        functools.partial(_expert_kernel, nblocks=nblocks),
        grid_spec=pltpu.PrefetchScalarGridSpec(
            num_scalar_prefetch=ns,
            grid=(ne, sum(nblocks)),
            in_specs=[block_spec(s) for s in range(ns)] + [wspec((1, d, f)), wspec((1, d, f)), wspec((1, f, d))],
            out_specs=[block_spec(s) for s in range(ns)],
            scratch_shapes=[pltpu.VMEM((d, f), BF16), pltpu.VMEM((d, f), BF16), pltpu.VMEM((f, d), BF16)]),
        out_shape=[jax.ShapeDtypeStruct((ne, used, d), BF16) for _, _, used, _ in sets],
        compiler_params=_cparams(("arbitrary", "arbitrary")),
        name="expert_ffn",
    )(*[t for _, t, _, _ in sets], *[x for x, _, _, _ in sets], w_gate, w_up, w_down)


def _combine_kernel(start_sm, cnt_sm, tot_sm, x1_ref, rank_ref, aff_ref, g2_ref, nf_ref, ye_hbm, o_ref, buf, sem, *,
                    final_norm, eb):
    i = pl.program_id(0)
    nt = pl.num_programs(0)
    ne = N_EXPERTS
    slot = i % 2
    rank = rank_ref[0]
    aff = aff_ref[0]
    sub = lax.broadcasted_iota(jnp.int32, (SLOT_ROWS, rank.shape[1]), 0)
    dn_t = (((0,), (0,)), ((), ()))

    def window(t, e, r):
        want = start_sm[t, e] + r * SLOT_ROWS
        written = ((tot_sm[e] + eb - 1) // eb) * eb
        off = jnp.minimum(want, written - SLOT_ROWS)
        return pl.multiple_of(off, ROW_ALIGN), want - off

    def copy(t, s, e, r):
        off, _ = window(t, e, r)
        return pltpu.make_async_copy(ye_hbm.at[e, pl.ds(off, SLOT_ROWS)],
                                     buf.at[s, pl.ds(e * SLOT_ROWS, SLOT_ROWS)], sem.at[s, e])

    def weighted(r):
        parts = []
        for e in range(ne):
            _, shift = window(i, e, r)
            local = rank[e:e + 1, :] - r * SLOT_ROWS
            hit = jnp.where(local >= 0, local + shift, -1) == sub
            parts.append(jnp.where(hit, aff[e:e + 1, :], 0.0).astype(BF16))
        return lax.dot_general(jnp.concatenate(parts, axis=0), buf[slot], dn_t, preferred_element_type=F32)

    @pl.when(i == 0)
    def _():
        for e in range(ne):
            copy(0, 0, e, 0).start()

    @pl.when(i + 1 < nt)
    def _prefetch_next_tile():
        for e in range(ne):
            copy(i + 1, 1 - slot, e, 0).start()

    for e in range(ne):
        copy(i, slot, e, 0).wait()
    y0 = weighted(0)

    def extra_round(r, acc):
        for e in range(ne):
            @pl.when(cnt_sm[i, e] > r * SLOT_ROWS)
            def _():
                copy(i, slot, e, r).start()
        for e in range(ne):
            @pl.when(cnt_sm[i, e] > r * SLOT_ROWS)
            def _():
                copy(i, slot, e, r).wait()
        return acc + weighted(r)

    y = lax.fori_loop(1, _num_rounds(cnt_sm, i), extra_round, y0)
    x2 = x1_ref[...] + g2_ref[...] * y.reshape(x1_ref.shape)
    if final_norm:
        ms = jnp.mean(x2 * x2, axis=-1, keepdims=True)
        x2 = x2 * lax.rsqrt(ms + EPS) * nf_ref[...]
    o_ref[...] = x2


def _combine(x1, rank, aff, start, cnt, tot, g2, norm_f_g, ye, cond_block, eb):
    g, s, d = x1.shape
    nt, ne, tm = rank.shape
    tok = pl.BlockSpec((g, tm // g, d), lambda i, *_: (0, i, 0))
    final_norm = norm_f_g is not None
    if not final_norm:
        norm_f_g = jnp.ones((d,), F32)
    return pl.pallas_call(
        functools.partial(_combine_kernel, final_norm=final_norm, eb=eb),
        grid_spec=pltpu.PrefetchScalarGridSpec(
            num_scalar_prefetch=3,
            grid=(nt,),
            in_specs=[tok,
                      pl.BlockSpec((1, ne, tm), lambda i, *_: (i, 0, 0)),
                      pl.BlockSpec((1, ne, tm), lambda i, *_: (i, 0, 0)),
                      pl.BlockSpec((g, 1, d), lambda i, *_: (cond_block, 0, 0)),
                      pl.BlockSpec((1, d), lambda i, *_: (0, 0)),
                      pl.BlockSpec(memory_space=pl.ANY)],
            out_specs=tok,
            scratch_shapes=[pltpu.VMEM((2, ne * SLOT_ROWS, d), BF16),
                            pltpu.SemaphoreType.DMA((2, ne))]),
        out_shape=jax.ShapeDtypeStruct((g, s, d), F32),
        compiler_params=_cparams(("arbitrary",)),
        name="expert_combine",
    )(start, cnt, tot, x1, rank, aff, g2, norm_f_g.reshape(1, d), ye)


def _layer_front(x, batch, seq_len, mods, cond_of_tile, moe_groups, cond_block, p, s0_f, s0_b, rope, emit_state):
    t, d = x.shape
    sh1, sc1, g1, sh2, sc2, g2 = mods
    u, x2c, qkvg, gates = _in_projection(x, p["norm1_g"], sc1, sh1, p["w_in"], p["hy_conv_w"], p["hy_conv_b"],
                                         cond_of_tile, seq_len, rope)
    seq = lambda a: a.reshape(batch, seq_len, a.shape[-1])
    y_hy = _hyena_long_conv(seq(u), seq(x2c), p).reshape(t, -1)
    ret = _retention(qkvg.reshape(batch, seq_len, -1), p["ret_decay_fwd"], p["ret_decay_bwd"], p["ret_gn_g"],
                     s0_f, s0_b, emit_state)
    y_ret = ret[0].reshape(t, -1)
    view = lambda a: a.reshape(moe_groups, t // moe_groups, a.shape[-1])
    x1, h2, aff = _out_projection(view(y_hy), view(y_ret), view(gates), view(x), g1, sc2, sh2, p["norm2_g"],
                                  p["w_hy_o"], p["w_ret_o"], p["w_out"], p["w_router"], cond_block)
    cap = (EC_CAPACITY_FACTOR * t) // N_EXPERTS
    eb = _expert_block(cap)
    rank, cnt, start, tot = _select(aff, cap, moe_groups)
    xs = _gather(h2, rank, start, cnt, tot, cap)
    routed = (xs, tot, _used_rows(cap, rank.shape[0]), eb)
    back = dict(x1=x1, rank=rank, aff=aff, start=start, cnt=cnt, tot=tot, g2=g2, cond_block=cond_block, eb=eb)
    return routed, back, ret[1:]


def _layer_back(back, ye, final_g):
    out = _combine(back["x1"], back["rank"], back["aff"], back["start"], back["cnt"], back["tot"], back["g2"],
                   final_g, ye, back["cond_block"], back["eb"])
    return out.reshape(-1, out.shape[-1])


def kernel(x_prompt, x_sample, state_ret_fwd, state_ret_bwd, c, c_ctx, w_mod, b_mod, norm1_g, w_in, hy_conv_w, hy_conv_b, hy_f_w1, hy_f_b1, hy_f_freq, hy_f_w2, hy_f_b2, hy_f_w3, hy_decay, hy_bias, ret_decay_fwd, ret_decay_bwd, ret_gn_g, w_hy_o, w_ret_o, w_out, norm2_g, w_router, w_e_gate, w_e_up, w_e_down, norm_f_g):
    b, s, d = x_prompt.shape
    bd, sd, _ = x_sample.shape
    depth = w_mod.shape[0]
    assert (b * s) % TOKEN_TILE == 0 and sd % TOKEN_TILE == 0 and b % 2 == 0 and bd % 2 == 0
    ncond = -(-(bd + 1) // 8) * 8
    lat_groups = bd if (TOKEN_TILE % (16 * bd) == 0 and sd % (TOKEN_TILE // bd) == 0) else 1
    assert lat_groups == bd or bd == 1
    conds = jnp.concatenate([c, c_ctx[None], jnp.zeros((ncond - bd - 1, d), F32)], axis=0)
    ctx_cond = lambda i: bd
    lat_cond = lambda i: i // (sd // TOKEN_TILE)
    xp = x_prompt.reshape(b * s, d)
    xs = x_sample.reshape(bd * sd, d)
    new_f, new_b = [], []
    for l in range(depth):
        mod = _modulation(conds, w_mod[l], b_mod[l])
        mods = [mod[:, k * d:(k + 1) * d].reshape(ncond, 1, d) for k in range(6)]
        p = {"norm1_g": norm1_g[l], "w_in": w_in[l].astype(BF16), "hy_conv_w": hy_conv_w[l],
             "hy_conv_b": hy_conv_b[l], "hy_f_w1": hy_f_w1[l], "hy_f_b1": hy_f_b1[l], "hy_f_freq": hy_f_freq[l],
             "hy_f_w2": hy_f_w2[l], "hy_f_b2": hy_f_b2[l], "hy_f_w3": hy_f_w3[l], "hy_decay": hy_decay[l],
             "hy_bias": hy_bias[l], "ret_decay_fwd": ret_decay_fwd[l], "ret_decay_bwd": ret_decay_bwd[l],
             "ret_gn_g": ret_gn_g[l], "w_hy_o": w_hy_o[l].astype(BF16), "w_ret_o": w_ret_o[l].astype(BF16),
             "w_out": w_out[l].astype(BF16), "norm2_g": norm2_g[l], "w_router": w_router[l],
             "w_e_gate": w_e_gate[l], "w_e_up": w_e_up[l], "w_e_down": w_e_down[l]}
        final_g = norm_f_g if l == depth - 1 else None
        routed_p, back_p, (s_f, s_b) = _layer_front(xp, b, s, mods, ctx_cond, 1, bd, p, None, None, False, True)
        new_f.append(s_f)
        new_b.append(s_b)
        routed_s, back_s, _ = _layer_front(xs, bd, sd, mods, lat_cond, lat_groups, 0, p, state_ret_fwd[:, l],
                                           state_ret_bwd[:, l], True, False)
        ye_p, ye_s = _experts([routed_p, routed_s], p["w_e_gate"], p["w_e_up"], p["w_e_down"])
        xp = _layer_back(back_p, ye_p, final_g)
        xs = _layer_back(back_s, ye_s, final_g)
    y_prompt = xp.reshape(b, s, d)
    y_sample = xs.reshape(bd, sd, d)
    return (y_prompt, y_sample, jnp.stack(new_f, axis=1), jnp.stack(new_b, axis=1))
```

```python
import functools
import math

import numpy as np
import jax
import jax.numpy as jnp
from jax import lax
from jax.experimental import pallas as pl
from jax.experimental.pallas import tpu as pltpu

F32 = jnp.float32
BF16 = jnp.bfloat16
HIGHEST = lax.Precision.HIGHEST

EPS = 1e-6
D_HYENA = 512
D_RET = 512
N_RET_HEADS = 4
RET_HEAD_DIM = 128
RET_CHUNK = 256
GRID_W = 64
FILTER_EMB = 33
ROPE_BASE = 10000.0
N_EXPERTS = 16
EC_CAPACITY_FACTOR = 2

TOKEN_TILE = 512
SLOT_ROWS = 96
ROW_ALIGN = 16
HALO_ROWS = 16
FFT_MINOR = 64
FFT_GROUP = 4
VMEM_LIMIT = 56 * 1024 * 1024


def _cparams(sem):
    return pltpu.CompilerParams(dimension_semantics=sem, vmem_limit_bytes=VMEM_LIMIT)


def _silu(x):
    return x * jax.nn.sigmoid(x)


def _mod_kernel(c_ref, w_ref, b_ref, o_ref):
    s = _silu(c_ref[...])
    o_ref[...] = jnp.dot(s, w_ref[...], preferred_element_type=F32, precision=HIGHEST) + b_ref[...]


def _modulation(conds, w_mod, b_mod):
    nc, d = conds.shape
    return pl.pallas_call(
        _mod_kernel,
        grid=(6,),
        in_specs=[pl.BlockSpec((nc, d), lambda j: (0, 0)),
                  pl.BlockSpec((d, d), lambda j: (0, j)),
                  pl.BlockSpec((1, d), lambda j: (0, j))],
        out_specs=pl.BlockSpec((nc, d), lambda j: (0, j)),
        out_shape=jax.ShapeDtypeStruct((nc, 6 * d), F32),
        compiler_params=_cparams(("arbitrary",)),
        name="modulation",
    )(conds, w_mod, b_mod.reshape(1, -1))


def _inproj_kernel(*refs, splits, rope, seq_len, halo):
    refs = list(refs)
    x_ref = refs.pop(0)
    if halo:
        xp_ref, xn_ref = refs.pop(0), refs.pop(0)
    g_ref, sc_ref, sh_ref, w_ref, cw_ref, cb_ref = refs[:6]
    del refs[:6]
    if rope:
        cos_ref, sin_ref = refs.pop(0), refs.pop(0)
    u_ref, x2c_ref, qkvg_ref, gates_ref, cv_sc, cx_sc = refs

    def normed(ref):
        x = ref[...]
        ms = jnp.mean(x * x, axis=-1, keepdims=True)
        h = x * lax.rsqrt(ms + EPS) * g_ref[...]
        return (h * (1.0 + sc_ref[0]) + sh_ref[0]).astype(BF16)

    hb = normed(x_ref)
    tm = hb.shape[0]
    nz, nq, ng = splits
    cw = 512
    dh = RET_HEAD_DIM
    hr = HALO_ROWS
    if halo:
        t0 = pl.program_id(0) * tm
        hp = jnp.where(t0 % seq_len != 0, normed(xp_ref), jnp.zeros((hr, hb.shape[1]), BF16))
        hn = jnp.where((t0 + tm) % seq_len != 0, normed(xn_ref), jnp.zeros((hr, hb.shape[1]), BF16))
        hext = jnp.concatenate([hp, hb, hn], axis=0)

    piece = tm if halo else seq_len
    pitch = piece + hr
    if not halo:
        for s in range(tm // piece + 1):
            cx_sc[s * pitch:s * pitch + hr, :] = jnp.zeros((hr, cw), F32)

    def conv(c0):
        taps = cw_ref[:, c0:c0 + cw]
        bias = cb_ref[:, c0:c0 + cw]
        if halo:
            cx_sc[...] = jnp.dot(hext, w_ref[:, c0:c0 + cw], preferred_element_type=F32)
        else:
            acc = jnp.dot(hb, w_ref[:, c0:c0 + cw], preferred_element_type=F32)
            for s in range(tm // piece):
                cx_sc[hr + s * pitch:hr + s * pitch + piece, :] = acc[s * piece:(s + 1) * piece]
        outs = []
        for s in range(tm // piece):
            r0 = hr + s * pitch
            outs.append(bias + cx_sc[r0 - 1:r0 - 1 + piece, :] * taps[0:1] + cx_sc[r0:r0 + piece, :] * taps[1:2]
                        + cx_sc[r0 + 1:r0 + 1 + piece, :] * taps[2:3])
        return outs[0] if len(outs) == 1 else jnp.concatenate(outs, axis=0)

    cv_sc[...] = conv(0)
    u_ref[...] = (cv_sc[...] * conv(D_HYENA)).astype(BF16)
    x2c_ref[...] = conv(2 * D_HYENA).astype(BF16)

    for c0 in range(nz, nz + nq + ng, cw):
        acc = jnp.dot(hb, w_ref[:, c0:c0 + cw], preferred_element_type=F32)
        if c0 < nz + nq:
            part = (c0 - nz) // D_RET
            if part == 1:
                acc = acc * (dh ** -0.5)
            if rope and part < 2:
                lane = lax.broadcasted_iota(jnp.int32, (acc.shape[0], dh), 1)
                swap_hi = (lane % (dh // 2)) < (dh // 4)
                cs, sn = cos_ref[...], sin_ref[...]
                heads = []
                for hh in range(cw // dh):
                    xh = acc[:, hh * dh:(hh + 1) * dh]
                    rot = jnp.where(swap_hi, pltpu.roll(xh, dh - dh // 4, axis=1), pltpu.roll(xh, dh // 4, axis=1))
                    heads.append(xh * cs + rot * sn)
                acc = jnp.concatenate(heads, axis=1)
            qkvg_ref[:, c0 - nz:c0 - nz + cw] = acc.astype(BF16)
        else:
            gates_ref[:, c0 - nz - nq:c0 - nz - nq + cw] = jax.nn.sigmoid(acc).astype(BF16)


def _in_projection(x, norm_g, sc, sh, w_in_bf, conv_w, conv_b, cond_of_tile, seq_len, rope):
    t, d = x.shape
    splits = (3 * D_HYENA, 4 * D_RET, 2 * d)
    assert D_RET == 512 and D_HYENA == 512
    tm = TOKEN_TILE
    assert seq_len % tm == 0 or tm % seq_len == 0
    halo = seq_len > tm
    row = lambda i: (i, 0)
    cond = lambda i: (cond_of_tile(i), 0, 0)
    full = lambda a: pl.BlockSpec(a.shape, lambda i: (0, 0))
    args, specs = [x], [pl.BlockSpec((tm, d), row)]
    if halo:
        hb_per_tile = tm // HALO_ROWS
        last = t // HALO_ROWS - 1
        args += [x, x]
        specs += [pl.BlockSpec((HALO_ROWS, d), lambda i: (jnp.maximum(i * hb_per_tile - 1, 0), 0)),
                  pl.BlockSpec((HALO_ROWS, d), lambda i: (jnp.minimum((i + 1) * hb_per_tile, last), 0))]
    cbias = conv_b.reshape(1, -1)
    args += [norm_g.reshape(1, d), sc, sh, w_in_bf, conv_w, cbias]
    specs += [pl.BlockSpec((1, d), lambda i: (0, 0)), pl.BlockSpec((1, 1, d), cond), pl.BlockSpec((1, 1, d), cond),
              full(w_in_bf), full(conv_w), full(cbias)]
    if rope:
        tiles_per_seq = seq_len // tm
        args += list(_rope_tables(seq_len))
        specs += [pl.BlockSpec((tm, RET_HEAD_DIM), lambda i: (i % tiles_per_seq, 0))] * 2
    return pl.pallas_call(
        functools.partial(_inproj_kernel, splits=splits, rope=rope, seq_len=seq_len, halo=halo),
        grid=(t // tm,),
        in_specs=specs,
        out_specs=[pl.BlockSpec((tm, D_HYENA), row),
                   pl.BlockSpec((tm, D_HYENA), row),
                   pl.BlockSpec((tm, splits[1]), row),
                   pl.BlockSpec((tm, splits[2]), row)],
        out_shape=[jax.ShapeDtypeStruct((t, D_HYENA), BF16),
                   jax.ShapeDtypeStruct((t, D_HYENA), BF16),
                   jax.ShapeDtypeStruct((t, splits[1]), BF16),
                   jax.ShapeDtypeStruct((t, splits[2]), BF16)],
        scratch_shapes=[pltpu.VMEM((tm, 512), F32),
                        pltpu.VMEM((tm + 2 * HALO_ROWS if halo else (tm // seq_len) * (seq_len + HALO_ROWS) + HALO_ROWS,
                                    512), F32)],
        compiler_params=_cparams(("parallel",)),
        name="in_projection",
    )(*args)


def _filter_features(seq_len):
    t = np.linspace(0.0, 1.0, seq_len, dtype=np.float32)[:, None]
    bands = (FILTER_EMB - 1) // 2
    w = (np.float32(2.0 * math.pi) * np.arange(seq_len, dtype=np.float32)) / np.float32(seq_len)
    f = np.linspace(1e-4, bands - 1, bands, dtype=np.float32)
    ang = (w[:, None] * f[None, :]).astype(np.float64)
    z = np.concatenate([t, np.cos(ang), -np.sin(ang)], axis=-1).astype(np.float32)
    return np.pad(z, ((0, 0), (0, 128 - FILTER_EMB)))


def _filter_kernel(z_ref, w1_ref, b1_ref, fr_ref, w2_ref, b2_ref, w3_ref, dec_ref, o_ref):
    z = z_ref[...]
    fr = fr_ref[...]
    dot = functools.partial(jnp.dot, preferred_element_type=F32, precision=HIGHEST)
    h = jnp.sin(fr * (dot(z, w1_ref[...]) + b1_ref[...]))
    h = jnp.sin(fr * (dot(h, w2_ref[...]) + b2_ref[...]))
    h = dot(h, w3_ref[...]) * jnp.exp(-z[:, 0:1] * jnp.abs(dec_ref[...]))
    rows = h.shape[0]
    grow = pl.program_id(0) * rows + lax.broadcasted_iota(jnp.int32, (rows, 1), 0)
    o_ref[0] = h[:, :D_HYENA].astype(o_ref.dtype)
    o_ref[1] = jnp.zeros((rows, D_HYENA), o_ref.dtype)
    o_ref[2] = jnp.where(grow == 0, 0.0, h[:, D_HYENA:]).astype(o_ref.dtype)
    o_ref[3] = jnp.zeros((rows, D_HYENA), o_ref.dtype)


def _hyena_filters(seq_len, p, out_dtype):
    z = jnp.asarray(_filter_features(seq_len))
    w1 = jnp.pad(p["hy_f_w1"], ((0, 128 - FILTER_EMB), (0, 0)))
    rows = min(seq_len, 512)
    full = lambda a: pl.BlockSpec(a.shape, lambda i: (0,) * a.ndim)
    ops = [w1, p["hy_f_b1"].reshape(1, -1), p["hy_f_freq"].reshape(1, -1), p["hy_f_w2"],
           p["hy_f_b2"].reshape(1, -1), p["hy_f_w3"], p["hy_decay"].reshape(1, -1)]
    return pl.pallas_call(
        _filter_kernel,
        grid=(seq_len // rows,),
        in_specs=[pl.BlockSpec((rows, 128), lambda i: (i, 0))] + [full(a) for a in ops],
        out_specs=pl.BlockSpec((4, rows, D_HYENA), lambda i: (0, i, 0)),
        out_shape=jax.ShapeDtypeStruct((4, seq_len, D_HYENA), out_dtype),
        compiler_params=_cparams(("arbitrary",)),
        name="hyena_filter",
    )(z, *ops)


def _stacked_dft(n_out, n_in, modulus, scale=1.0):
    k = np.arange(n_out, dtype=np.int64)[:, None]
    n = np.arange(n_in, dtype=np.int64)[None, :]
    th = ((k * n) % modulus) * (2.0 * math.pi / modulus)
    c, s = np.cos(th), np.sin(th)
    fwd = np.block([[c, s], [-s, c]]).astype(np.float32)
    inv = (np.block([[c.T, -s.T], [s.T, c.T]]) * scale).astype(np.float32)
    return fwd, inv


def _second_level_tables(n_total, n1, n2):
    k1 = np.arange(n1, dtype=np.int64)[:, None, None]
    k2 = np.arange(n2, dtype=np.int64)[None, :, None]
    m = np.arange(n2, dtype=np.int64)[None, None, :]
    th = ((m * (k1 + n1 * k2)) % n_total) * (2.0 * math.pi / n_total)
    c, s = np.cos(th), np.sin(th)
    g = np.concatenate([np.concatenate([c, s], axis=2), np.concatenate([-s, c], axis=2)], axis=1)
    return g.astype(np.float32), np.swapaxes(g, 1, 2).astype(np.float32)


def _skewed(n):
    return n + 1


def _fft1_kernel(h_ref, f_ref, a_ref, u_sc, t_sc):
    n2 = FFT_MINOR
    _, seq_len, cb = h_ref.shape
    h1 = seq_len // n2
    nk = a_ref.shape[1]
    up, tp = _skewed(n2), _skewed(nk)
    for b in range(2):
        for q in range(h1):
            u_sc[b, q * up:q * up + n2, :] = h_ref[b, q * n2:(q + 1) * n2, :].astype(F32)

    def fwd(gi, carry):
        m0 = gi * FFT_GROUP
        cols = []
        for d in range(FFT_GROUP):
            xr = u_sc[0, pl.ds(m0 + d, h1, stride=up), :]
            xi = u_sc[1, pl.ds(m0 + d, h1, stride=up), :]
            cols.append(jnp.concatenate([xr, xi], axis=0))
        x = jnp.concatenate(cols, axis=1).astype(BF16)
        res = jnp.dot(f_ref[...], x, preferred_element_type=F32)
        for d in range(FFT_GROUP):
            t_sc[pl.ds((m0 + d) * tp, nk), :] = res[:, d * cb:(d + 1) * cb]
        return carry

    lax.fori_loop(0, n2 // FFT_GROUP, fwd, 0, unroll=2)

    def transpose(k, carry):
        a_ref[0, k] = t_sc[pl.ds(k, n2, stride=tp), :].astype(a_ref.dtype)
        return carry

    lax.fori_loop(0, nk, transpose, 0, unroll=8)


def _fft_first_level(h, f1):
    b, seq_len, c = h.shape
    cb = 128
    return pl.pallas_call(
        _fft1_kernel,
        grid=(b // 2, c // cb),
        in_specs=[pl.BlockSpec((2, seq_len, cb), lambda i, j: (i, 0, j)),
                  pl.BlockSpec(f1.shape, lambda i, j: (0, 0))],
        out_specs=pl.BlockSpec((1, f1.shape[0], FFT_MINOR, cb), lambda i, j: (i, 0, 0, j)),
        out_shape=jax.ShapeDtypeStruct((b // 2, f1.shape[0], FFT_MINOR, c), BF16),
        scratch_shapes=[pltpu.VMEM((2, (seq_len // FFT_MINOR) * _skewed(FFT_MINOR), cb), F32),
                        pltpu.VMEM((FFT_MINOR * _skewed(f1.shape[0]), cb), F32)],
        compiler_params=_cparams(("parallel", "parallel")),
        name="hyena_dft_level1",
    )(h, f1)


def _fft_s2f_kernel(a_ref, g_ref, kr_ref, ki_ref):
    _, _, kb, n2, c = a_ref.shape
    for kk in range(kb):
        g = g_ref[kk]
        hf = jnp.dot(g, a_ref[0, :, kk].reshape(2 * n2, c), preferred_element_type=F32)
        hb = jnp.dot(g, a_ref[1, :, kk].reshape(2 * n2, c), preferred_element_type=F32)
        kr_ref[kk] = hf[:n2] + hb[:n2]
        ki_ref[kk] = hf[n2:] - hb[n2:]


def _fft_filter_second_level(a, g, kb=8):
    _, _, n1, n2, c = a.shape
    spec = pl.BlockSpec((kb, n2, c), lambda i: (i, 0, 0))
    return pl.pallas_call(
        _fft_s2f_kernel,
        grid=(n1 // kb,),
        in_specs=[pl.BlockSpec((2, 2, kb, n2, c), lambda i: (0, 0, i, 0, 0)),
                  pl.BlockSpec((kb, 2 * n2, 2 * n2), lambda i: (i, 0, 0))],
        out_specs=[spec, spec],
        out_shape=[jax.ShapeDtypeStruct((n1, n2, c), F32)] * 2,
        compiler_params=_cparams(("parallel",)),
        name="hyena_filter_spectrum",
    )(a, g)


def _fft_s2_kernel(a_ref, g_ref, gt_ref, kr_ref, ki_ref, b_ref, r_sc):
    npairs, _, kb, n2, cb = a_ref.shape
    nq = cb // 128
    rp = _skewed(2 * n2)
    for kk in range(kb):
        a = jnp.concatenate([a_ref[p, :, kk].reshape(2 * n2, cb) for p in range(npairs)], axis=1)
        x = jnp.dot(g_ref[kk], a, preferred_element_type=F32)
        xr, xi = x[:n2], x[n2:]
        kr = jnp.concatenate([kr_ref[kk]] * npairs, axis=1)
        ki = jnp.concatenate([ki_ref[kk]] * npairs, axis=1)
        y = jnp.concatenate([xr * kr - xi * ki, xr * ki + xi * kr], axis=0).astype(BF16)
        res = jnp.dot(gt_ref[kk], y, preferred_element_type=F32)
        for q in range(npairs * nq):
            r_sc[q, kk * rp:kk * rp + 2 * n2, :] = res[:, q * 128:(q + 1) * 128]

    for p in range(npairs):
        def gather(row):
            return jnp.concatenate([r_sc[p * nq + q, pl.ds(row, kb, stride=rp), :] for q in range(nq)], axis=1)

        def transpose(m, carry):
            b_ref[p, 0, m] = gather(m).astype(b_ref.dtype)
            b_ref[p, 1, m] = gather(n2 + m).astype(b_ref.dtype)
            return carry

        lax.fori_loop(0, n2, transpose, 0, unroll=8)


def _fft_second_level(a, g, gt, kr, ki):
    p, _, n1, n2, c = a.shape
    kb, cb = 16, 256
    gspec = pl.BlockSpec((kb, 2 * n2, 2 * n2), lambda i, j: (i, 0, 0))
    kspec = pl.BlockSpec((kb, n2, cb), lambda i, j: (i, 0, j))
    return pl.pallas_call(
        _fft_s2_kernel,
        grid=(n1 // kb, c // cb),
        in_specs=[pl.BlockSpec((p, 2, kb, n2, cb), lambda i, j: (0, 0, i, 0, j)), gspec, gspec, kspec, kspec],
        out_specs=pl.BlockSpec((p, 2, n2, kb, cb), lambda i, j: (0, 0, 0, i, j)),
        out_shape=jax.ShapeDtypeStruct((p, 2, n2, n1, c), BF16),
        scratch_shapes=[pltpu.VMEM((p * cb // 128, kb * _skewed(2 * n2), 128), F32)],
        compiler_params=_cparams(("parallel", "parallel")),
        name="hyena_dft_level2",
    )(a, g, gt, kr, ki)


def _fft_s3_kernel(b_ref, f_ref, u_ref, x2_ref, bias_ref, o_ref, t_sc):
    _, _, n2, n1, cb = b_ref.shape
    nr = f_ref.shape[0]
    h1 = nr // 2
    tp = _skewed(nr)

    def inv(gi, carry):
        m0 = gi * FFT_GROUP
        x = jnp.concatenate([b_ref[0, :, m0 + d].reshape(2 * n1, cb) for d in range(FFT_GROUP)], axis=1)
        res = jnp.dot(f_ref[...], x, preferred_element_type=F32)
        for d in range(FFT_GROUP):
            t_sc[pl.ds((m0 + d) * tp, nr), :] = res[:, d * cb:(d + 1) * cb]
        return carry

    lax.fori_loop(0, n2 // FFT_GROUP, inv, 0, unroll=2)

    bias = bias_ref[...]
    for b in range(2):
        def finish(q, carry):
            rows = pl.ds(pl.multiple_of(q * n2, n2), n2)
            conv = t_sc[pl.ds(b * h1 + q, n2, stride=tp), :]
            u = u_ref[b, rows, :].astype(F32)
            o_ref[b, rows, :] = ((conv + u * bias) * x2_ref[b, rows, :].astype(F32)).astype(o_ref.dtype)
            return carry

        lax.fori_loop(0, h1, finish, 0, unroll=4)


def _fft_last_level(bt, f1inv, u, x2c, bias):
    p, _, n2, n1, c = bt.shape
    b, seq_len, _ = u.shape
    cb = 128
    uspec = pl.BlockSpec((2, seq_len, cb), lambda i, j: (i, 0, j))
    return pl.pallas_call(
        _fft_s3_kernel,
        grid=(p, c // cb),
        in_specs=[pl.BlockSpec((1, 2, n2, n1, cb), lambda i, j: (i, 0, 0, 0, j)),
                  pl.BlockSpec(f1inv.shape, lambda i, j: (0, 0)),
                  uspec, uspec,
                  pl.BlockSpec((1, cb), lambda i, j: (0, j))],
        out_specs=uspec,
        out_shape=jax.ShapeDtypeStruct(u.shape, BF16),
        scratch_shapes=[pltpu.VMEM((n2 * _skewed(f1inv.shape[0]), cb), F32)],
        compiler_params=_cparams(("parallel", "parallel")),
        name="hyena_dft_inverse",
    )(bt, f1inv, u, x2c, bias)


def _kf_direct_kernel(h_ref, f_ref, kr_ref, ki_ref):
    n = kr_ref.shape[0]
    dot = functools.partial(jnp.dot, preferred_element_type=F32, precision=HIGHEST)
    hf = dot(f_ref[...], h_ref[0])
    hb = dot(f_ref[...], h_ref[2])
    kr_ref[...] = hf[:n] + hb[:n]
    ki_ref[...] = hf[n:] - hb[n:]


def _hyena_direct_kernel(u_ref, x2_ref, f_ref, fi_ref, kr_ref, ki_ref, bias_ref, o_ref):
    _, two, seq_len, cb = u_ref.shape
    n = kr_ref.shape[0]
    ub = u_ref[0].reshape(two * seq_len, cb)
    x = jnp.dot(f_ref[...], ub, preferred_element_type=F32)
    xr, xi = x[:n], x[n:]
    kr, ki = kr_ref[...], ki_ref[...]
    y = jnp.concatenate([xr * kr - xi * ki, xr * ki + xi * kr], axis=0).astype(BF16)
    conv = jnp.dot(fi_ref[...], y, preferred_element_type=F32)
    x2 = x2_ref[0].reshape(two * seq_len, cb).astype(F32)
    o = (conv + ub.astype(F32) * bias_ref[...]) * x2
    o_ref[0] = o.reshape(two, seq_len, cb).astype(o_ref.dtype)


def _hyena_long_conv(u, x2c, p):
    b, seq_len, _ = u.shape
    c = D_HYENA
    n = 2 * seq_len
    bias = p["hy_bias"].reshape(1, c)
    hh = _hyena_filters(seq_len, p, F32)
    if seq_len <= 512:
        fwd, inv = _stacked_dft(n, seq_len, n, scale=1.0 / n)
        kr, ki = pl.pallas_call(
            _kf_direct_kernel,
            out_shape=[jax.ShapeDtypeStruct((n, c), F32)] * 2,
            compiler_params=_cparams(None),
            name="hyena_filter_spectrum_direct",
        )(hh, fwd[:, :seq_len])
        cb = 256
        pair = lambda a: a.reshape(b // 2, 2, seq_len, c)
        uspec = pl.BlockSpec((1, 2, seq_len, cb), lambda i, j: (i, 0, 0, j))
        kspec = pl.BlockSpec((n, cb), lambda i, j: (0, j))
        out = pl.pallas_call(
            _hyena_direct_kernel,
            grid=(b // 2, c // cb),
            in_specs=[uspec, uspec,
                      pl.BlockSpec(fwd.shape, lambda i, j: (0, 0)),
                      pl.BlockSpec(inv.shape, lambda i, j: (0, 0)),
                      kspec, kspec,
                      pl.BlockSpec((1, cb), lambda i, j: (0, j))],
            out_specs=uspec,
            out_shape=jax.ShapeDtypeStruct((b // 2, 2, seq_len, c), BF16),
            compiler_params=_cparams(("parallel", "parallel")),
            name="hyena_dft_direct",
        )(pair(u), pair(x2c), jnp.asarray(fwd).astype(BF16), jnp.asarray(inv).astype(BF16), kr, ki, bias)
        return out.reshape(b, seq_len, c)

    n2 = FFT_MINOR
    n1 = n // n2
    h1 = seq_len // n2
    f1, f1inv = _stacked_dft(n1, h1, n1, scale=1.0 / n)
    f1, f1inv = jnp.asarray(f1).astype(BF16), jnp.asarray(f1inv).astype(BF16)
    g, gt = _second_level_tables(n, n1, n2)
    g, gt = jnp.asarray(g).astype(BF16), jnp.asarray(gt).astype(BF16)
    ha = _fft_first_level(hh, f1)
    kr, ki = _fft_filter_second_level(ha.reshape(2, 2, n1, n2, c), g)
    a = _fft_first_level(u, f1)
    bt = _fft_second_level(a.reshape(b // 2, 2, n1, n2, c), g, gt, kr, ki)
    return _fft_last_level(bt, f1inv, u, x2c, bias)


def _rope_tables(seq_len):
    half = RET_HEAD_DIM // 2
    nf = half // 2
    t = np.arange(seq_len)
    inv = ROPE_BASE ** (-np.arange(nf, dtype=np.float64) / nf)
    ar = (t // GRID_W)[:, None] * inv[None, :]
    ac = (t % GRID_W)[:, None] * inv[None, :]
    cos = np.concatenate([np.cos(ar), np.cos(ar), np.cos(ac), np.cos(ac)], axis=-1)
    sin = np.concatenate([-np.sin(ar), np.sin(ar), -np.sin(ac), np.sin(ac)], axis=-1)
    return cos.astype(np.float32), sin.astype(np.float32)


def _log_sigmoid(x):
    return jnp.minimum(x, 0.0) - jnp.log1p(jnp.exp(-jnp.abs(x)))


def _retention_kernel(*refs, has_init, emit_state, cpb):
    refs = list(refs)
    q_ref, k_ref, v_ref, g_ref, dec_ref, gn_ref = refs[:6]
    del refs[:6]
    if has_init:
        s0f_ref, s0b_ref = refs[:2]
        del refs[:2]
    o_ref = refs.pop(0)
    if emit_state:
        sf_out, sb_out = refs[:2]
        del refs[:2]
    sf_ref, sb_ref, sball_ref = refs

    c = RET_CHUNK
    dh = RET_HEAD_DIM
    nh = N_RET_HEADS
    phase = pl.program_id(1)
    j = pl.program_id(2)
    nb = pl.num_programs(2)

    diff = (lax.broadcasted_iota(jnp.int32, (c, c), 0) - lax.broadcasted_iota(jnp.int32, (c, c), 1)).astype(F32)
    ri = lax.broadcasted_iota(jnp.int32, (c, dh), 0).astype(F32)

    def head_consts(h):
        lgf = _log_sigmoid(dec_ref[0, h])[0:1, :]
        lgb = _log_sigmoid(dec_ref[1, h])[0:1, :]
        return lgf, lgb

    def chunk_wide(lg):
        return jnp.concatenate([lg] * (c // dh), axis=1)

    bb = q_ref.shape[0]
    heads = [(bi, h) for bi in range(bb) for h in range(nh)]

    def load(ref, bi, r0, h):
        return ref[bi, r0:r0 + c, h * dh:(h + 1) * dh]

    dn_t = (((0,), (0,)), ((), ()))
    dn_nt = (((1,), (1,)), ((), ()))

    @pl.when(phase == 0)
    def _backward_sweep():
        @pl.when(j == 0)
        def _():
            for bi, h in heads:
                sb_ref[bi * nh + h] = s0b_ref[bi, h] if has_init else jnp.zeros((dh, dh), F32)

        blk = nb - 1 - j
        for bi, h in heads:
            hs = bi * nh + h
            _, lgb = head_consts(h)
            zeta_b = jnp.exp(lgb * ri)
            cdec_b = jnp.exp(lgb * float(c))
            for cc in reversed(range(cpb)):
                r0 = cc * c
                n = blk * cpb + cc
                s = sb_ref[hs]
                sball_ref[n, hs] = s.astype(BF16)
                kz = (load(k_ref, bi, r0, h).astype(F32) * zeta_b).astype(BF16)
                vv = load(v_ref, bi, r0, h)
                sb_ref[hs] = cdec_b * s + lax.dot_general(kz, vv, dn_t, preferred_element_type=F32)

        if emit_state:
            @pl.when(j == nb - 1)
            def _():
                for bi, h in heads:
                    sb_out[bi, h] = sb_ref[bi * nh + h]

    @pl.when(phase == 1)
    def _forward_sweep():
        @pl.when(j == 0)
        def _():
            for bi, h in heads:
                sf_ref[bi * nh + h] = s0f_ref[bi, h] if has_init else jnp.zeros((dh, dh), F32)

        for bi, h in heads:
            hs = bi * nh + h
            lgf, lgb = head_consts(h)
            mask = (jnp.where(diff >= 0, jnp.exp(chunk_wide(lgf) * jnp.maximum(diff, 0.0)), 0.0)
                    + jnp.where(diff <= 0, jnp.exp(chunk_wide(lgb) * jnp.maximum(-diff, 0.0)), 0.0))
            xi_f = jnp.exp(lgf * (ri + 1.0))
            xi_b = jnp.exp(lgb * (float(c) - ri))
            zeta_f = jnp.exp(lgf * (float(c - 1) - ri))
            cdec_f = jnp.exp(lgf * float(c))
            gn = gn_ref[:, h * dh:(h + 1) * dh]
            for cc in range(cpb):
                r0 = cc * c
                n = j * cpb + cc
                qb = load(q_ref, bi, r0, h)
                kb = load(k_ref, bi, r0, h)
                vv = load(v_ref, bi, r0, h)
                gate = load(g_ref, bi, r0, h).astype(F32)
                sc = lax.dot_general(qb, kb, dn_nt, preferred_element_type=F32)
                inner = jnp.dot((sc * mask).astype(BF16), vv, preferred_element_type=F32)
                s = sf_ref[hs]
                q = qb.astype(F32)
                lhs = jnp.concatenate([q * xi_f, q * xi_b], axis=1).astype(BF16)
                rhs = jnp.concatenate([s.astype(BF16), sball_ref[n, hs]], axis=0)
                o = inner + jnp.dot(lhs, rhs, preferred_element_type=F32)
                mu = jnp.mean(o, axis=-1, keepdims=True)
                d = o - mu
                var = jnp.mean(d * d, axis=-1, keepdims=True)
                y = d * lax.rsqrt(var + EPS) * gn * _silu(gate)
                o_ref[bi, r0:r0 + c, h * dh:(h + 1) * dh] = y.astype(o_ref.dtype)
                kz = (kb.astype(F32) * zeta_f).astype(BF16)
                sf_ref[hs] = cdec_f * s + lax.dot_general(kz, vv, dn_t, preferred_element_type=F32)

        if emit_state:
            @pl.when(j == nb - 1)
            def _():
                for bi, h in heads:
                    sf_out[bi, h] = sf_ref[bi * nh + h]


def _retention(qkvg, dec_f, dec_b, gn_g, s0_f, s0_b, emit_state):
    b, seq_len, _ = qkvg.shape
    nh, dh, c = N_RET_HEADS, RET_HEAD_DIM, RET_CHUNK
    rb = min(seq_len, 1024)
    nb = seq_len // rb
    cpb = rb // c
    bb = max(1, min(4, 1024 // seq_len))
    while b % bb:
        bb //= 2
    has_init = s0_f is not None
    dec = jnp.broadcast_to(jnp.stack([dec_f, dec_b])[:, :, None, None], (2, nh, 8, 128)).astype(F32)
    kv_blk = lambda i, p, j: jnp.where(p == 0, nb - 1 - j, j)
    q_blk = lambda i, p, j: jnp.where(p == 0, 0, j)
    in_specs = [pl.BlockSpec((bb, rb, D_RET), lambda i, p, j: (i, q_blk(i, p, j), 0)),
                pl.BlockSpec((bb, rb, D_RET), lambda i, p, j: (i, kv_blk(i, p, j), 1)),
                pl.BlockSpec((bb, rb, D_RET), lambda i, p, j: (i, kv_blk(i, p, j), 2)),
                pl.BlockSpec((bb, rb, D_RET), lambda i, p, j: (i, q_blk(i, p, j), 3)),
                pl.BlockSpec((2, nh, 8, 128), lambda i, p, j: (0, 0, 0, 0)),
                pl.BlockSpec((1, D_RET), lambda i, p, j: (0, 0))]
    args = [qkvg, qkvg, qkvg, qkvg, dec, gn_g.reshape(1, -1)]
    sspec = pl.BlockSpec((bb, nh, dh, dh), lambda i, p, j: (i, 0, 0, 0))
    if has_init:
        in_specs += [sspec, sspec]
        args += [s0_f, s0_b]
    out_specs = [pl.BlockSpec((bb, rb, D_RET), lambda i, p, j: (i, q_blk(i, p, j), 0))]
    out_shape = [jax.ShapeDtypeStruct((b, seq_len, D_RET), BF16)]
    if emit_state:
        out_specs += [sspec, sspec]
        out_shape += [jax.ShapeDtypeStruct((b, nh, dh, dh), F32)] * 2
    return pl.pallas_call(
        functools.partial(_retention_kernel, has_init=has_init, emit_state=emit_state, cpb=cpb),
        grid=(b // bb, 2, nb),
        in_specs=in_specs,
        out_specs=out_specs,
        out_shape=out_shape,
        scratch_shapes=[pltpu.VMEM((bb * nh, dh, dh), F32), pltpu.VMEM((bb * nh, dh, dh), F32),
                        pltpu.VMEM((nb * cpb, bb * nh, dh, dh), BF16)],
        compiler_params=_cparams(("parallel", "arbitrary", "arbitrary")),
        name="retention",
    )(*args)


def _outproj_kernel(yhy_ref, yret_ref, gates_ref, x_ref, g1_ref, sc_ref, sh_ref, ng_ref,
                    why_ref, wret_ref, wout_ref, wr_ref, x1_ref, h2_ref, aff_ref):
    g, m, d = x_ref.shape
    rows = lambda ref: ref[...].reshape(g * m, ref.shape[2])
    a = jnp.dot(rows(yhy_ref), why_ref[...], preferred_element_type=F32)
    b = jnp.dot(rows(yret_ref), wret_ref[...], preferred_element_type=F32)
    gates = rows(gates_ref)
    merged = gates[:, :d].astype(F32) * a + gates[:, d:].astype(F32) * b
    out = jnp.dot(merged.astype(BF16), wout_ref[...], preferred_element_type=F32)
    x1 = x_ref[...] + g1_ref[...] * out.reshape(g, m, d)
    x1_ref[...] = x1
    ms = jnp.mean(x1 * x1, axis=-1, keepdims=True)
    h = x1 * lax.rsqrt(ms + EPS) * ng_ref[...]
    h = (h * (1.0 + sc_ref[...]) + sh_ref[...]).reshape(g * m, d)
    h_hi = h.astype(BF16)
    h2_ref[...] = h_hi.reshape(g, m, d)
    h_lo = (h - h_hi.astype(F32)).astype(BF16)
    t = jnp.dot(h_hi, wr_ref[...], preferred_element_type=F32)
    logits = t[:, :128] + t[:, 128:] + jnp.dot(h_lo, wr_ref[:, :128], preferred_element_type=F32)
    lane = lax.broadcasted_iota(jnp.int32, logits.shape, 1)
    logits = jnp.where(lane < N_EXPERTS, logits, -jnp.inf)
    e = jnp.exp(logits - jnp.max(logits, axis=-1, keepdims=True))
    aff = e / jnp.sum(e, axis=-1, keepdims=True)
    aff_ref[0] = aff.T[:N_EXPERTS, :]


def _out_projection(y_hy, y_ret, gates, x, g1, sc2, sh2, norm2_g, w_hy_o, w_ret_o, w_out, w_router, cond_block):
    g, s, d = x.shape
    tm = TOKEN_TILE
    m = tm // g
    tok = lambda c: pl.BlockSpec((g, m, c), lambda i: (0, i, 0))
    cond = pl.BlockSpec((g, 1, d), lambda i: (cond_block, 0, 0))
    full = lambda a: pl.BlockSpec(a.shape, lambda i: (0, 0))
    wr = jnp.pad(w_router, ((0, 0), (0, 128 - N_EXPERTS)))
    wr_hi = wr.astype(BF16)
    wr = jnp.concatenate([wr_hi, (wr - wr_hi.astype(F32)).astype(BF16)], axis=1)
    return pl.pallas_call(
        _outproj_kernel,
        grid=(s // m,),
        in_specs=[tok(D_HYENA), tok(D_RET), tok(2 * d), tok(d), cond, cond, cond,
                  pl.BlockSpec((1, d), lambda i: (0, 0)),
                  full(w_hy_o), full(w_ret_o), full(w_out), full(wr)],
        out_specs=[tok(d), tok(d), pl.BlockSpec((1, N_EXPERTS, tm), lambda i: (i, 0, 0))],
        out_shape=[jax.ShapeDtypeStruct((g, s, d), F32), jax.ShapeDtypeStruct((g, s, d), BF16),
                   jax.ShapeDtypeStruct((s // m, N_EXPERTS, tm), F32)],
        compiler_params=_cparams(("parallel",)),
        name="out_projection_router",
    )(y_hy, y_ret, gates, x, g1, sc2, sh2, norm2_g.reshape(1, d), w_hy_o, w_ret_o, w_out, wr)


def _select_kernel(aff_ref, tri_ref, rank_ref, cnt_ref, start_ref, tot_ref, *, cap, idx_bits, groups):
    nt, ne, tm = aff_ref.shape
    a = aff_ref[...]

    def count(m):
        return jnp.sum(jnp.sum(m, axis=0, keepdims=True), axis=2, keepdims=True)

    def thr_step(s, thr):
        cand = thr | (1 << (30 - s))
        cnt = count(jnp.where(a >= pltpu.bitcast(cand, F32), 1.0, 0.0))
        return jnp.where(cnt >= float(cap), cand, thr)

    thr = pltpu.bitcast(lax.fori_loop(0, 31, thr_step, jnp.zeros((1, ne, 1), jnp.int32)), F32)
    gt = a > thr
    eq = a == thr
    need = float(cap) - count(jnp.where(gt, 1.0, 0.0))
    m = tm // groups
    tile = lax.broadcasted_iota(jnp.int32, (nt, 1, tm), 0)
    lane = lax.broadcasted_iota(jnp.int32, (nt, 1, tm), 2)
    idx = (lane // m) * (nt * m) + tile * m + lane % m

    def idx_step(s, lim):
        cand = lim | (1 << (idx_bits - 1 - s))
        cnt = count(jnp.where(eq, jnp.where(idx < cand, 1.0, 0.0), 0.0))
        return jnp.where(cnt < need, cand, lim)

    lim = lax.fori_loop(0, idx_bits, idx_step, jnp.zeros((1, ne, 1), jnp.int32))
    sel = jnp.where(gt, 1.0, jnp.where(eq, jnp.where(idx <= lim, 1.0, 0.0), 0.0))
    sel2 = sel.reshape(nt * ne, tm).astype(BF16)
    prefix = jnp.dot(sel2, tri_ref[...], preferred_element_type=F32)
    rank = jnp.where(sel2 > 0, prefix, -1.0).astype(jnp.int32)
    rank_ref[...] = rank.reshape(nt, ne, tm)
    ones = jnp.ones((tm, 128), BF16)
    cnt = jnp.dot(sel2, ones, preferred_element_type=F32).astype(jnp.int32).reshape(nt, ne, 128)
    cnt_ref[...] = cnt
    acc = jnp.zeros((ne, 128), jnp.int32)
    for t in range(nt):
        start_ref[t] = acc
        acc = acc + ((cnt[t] + (ROW_ALIGN - 1)) & (-ROW_ALIGN))
    tot_ref[...] = acc


def _select(aff, cap, groups):
    nt, ne, tm = aff.shape
    idx_bits = max(1, int(math.ceil(math.log2(nt * tm))))
    r = lax.broadcasted_iota(jnp.int32, (tm, tm), 0)
    c = lax.broadcasted_iota(jnp.int32, (tm, tm), 1)
    tri = (r < c).astype(BF16)
    rank, cnt, start, tot = pl.pallas_call(
        functools.partial(_select_kernel, cap=cap, idx_bits=idx_bits, groups=groups),
        out_shape=[jax.ShapeDtypeStruct((nt, ne, tm), jnp.int32),
                   jax.ShapeDtypeStruct((nt, ne, 128), jnp.int32),
                   jax.ShapeDtypeStruct((nt, ne, 128), jnp.int32),
                   jax.ShapeDtypeStruct((ne, 128), jnp.int32)],
        compiler_params=_cparams(None),
        name="expert_choice_select",
    )(aff, tri)
    return rank, cnt[:, :, 0], start[:, :, 0], tot[:, 0]


def _expert_block(cap):
    return 512 if cap >= 2048 else 256


def _used_rows(cap, nt):
    eb = _expert_block(cap)
    return -(-(cap + ROW_ALIGN * nt) // eb) * eb


def _list_rows(cap, nt):
    max_rounds = -(-TOKEN_TILE // SLOT_ROWS)
    return _used_rows(cap, nt) + max(_expert_block(cap), max_rounds * SLOT_ROWS)


def _num_rounds(cnt_sm, i):
    m = cnt_sm[i, 0]
    for e in range(1, N_EXPERTS):
        m = jnp.maximum(m, cnt_sm[i, e])
    return jnp.maximum((m + SLOT_ROWS - 1) // SLOT_ROWS, 1)


def _gather_kernel(start_sm, cnt_sm, tot_sm, h_ref, rank_ref, xs_hbm, stage, zbuf, sem):
    i = pl.program_id(0)
    ne = N_EXPERTS
    slot = i % 2
    rank = rank_ref[0]
    sub = lax.broadcasted_iota(jnp.int32, (SLOT_ROWS, rank.shape[1]), 0)

    def copy(s, e, off):
        return pltpu.make_async_copy(stage.at[s, pl.ds(e * SLOT_ROWS, SLOT_ROWS)],
                                     xs_hbm.at[e, pl.ds(off, SLOT_ROWS)], sem.at[e])

    def fill(r):
        h = h_ref[...].reshape(rank.shape[1], h_ref.shape[2])
        for e in range(ne):
            onehot = jnp.where(rank[e:e + 1, :] == sub + r * SLOT_ROWS, 1.0, 0.0).astype(BF16)
            stage[slot, e * SLOT_ROWS:(e + 1) * SLOT_ROWS, :] = jnp.dot(
                onehot, h, preferred_element_type=F32).astype(BF16)

    def start_all(r):
        for e in range(ne):
            copy(slot, e, pl.multiple_of(start_sm[i, e] + r * SLOT_ROWS, ROW_ALIGN)).start()

    def wait_all(s):
        for e in range(ne):
            copy(s, e, 0).wait()

    fill(0)

    @pl.when(i > 0)
    def _():
        wait_all(1 - slot)

    start_all(0)

    def extra_round(r, carry):
        wait_all(slot)
        fill(r)
        start_all(r)
        return carry

    lax.fori_loop(1, _num_rounds(cnt_sm, i), extra_round, 0)

    @pl.when(i == pl.num_programs(0) - 1)
    def _zero_tail():
        wait_all(slot)
        zbuf[...] = jnp.zeros(zbuf.shape, zbuf.dtype)

        def zcopy(e):
            off = pl.multiple_of(tot_sm[e], ROW_ALIGN)
            return pltpu.make_async_copy(zbuf, xs_hbm.at[e, pl.ds(off, zbuf.shape[0])], sem.at[e])

        for e in range(ne):
            zcopy(e).start()
        for e in range(ne):
            zcopy(e).wait()


def _gather(h2, rank, start, cnt, tot, cap):
    g, _, d = h2.shape
    nt, ne, tm = rank.shape
    rl = _list_rows(cap, nt)
    return pl.pallas_call(
        _gather_kernel,
        grid_spec=pltpu.PrefetchScalarGridSpec(
            num_scalar_prefetch=3,
            grid=(nt,),
            in_specs=[pl.BlockSpec((g, tm // g, d), lambda i, *_: (0, i, 0)),
                      pl.BlockSpec((1, ne, tm), lambda i, *_: (i, 0, 0))],
            out_specs=pl.BlockSpec(memory_space=pl.ANY),
            scratch_shapes=[pltpu.VMEM((2, ne * SLOT_ROWS, d), BF16),
                            pltpu.VMEM((_expert_block(cap), d), BF16),
                            pltpu.SemaphoreType.DMA((ne,))]),
        out_shape=jax.ShapeDtypeStruct((ne, rl, d), BF16),
        compiler_params=_cparams(("arbitrary",)),
        name="expert_gather",
    )(start, cnt, tot, h2, rank)


def _expert_kernel(*refs, nblocks):
    ns = len(nblocks)
    tot_sms, xs_refs = refs[:ns], refs[ns:2 * ns]
    wg_ref, wu_ref, wd_ref = refs[2 * ns:2 * ns + 3]
    ye_refs = refs[2 * ns + 3:3 * ns + 3]
    wg_bf, wu_bf, wd_bf = refs[3 * ns + 3:]
    e = pl.program_id(0)
    j = pl.program_id(1)

    @pl.when(j == 0)
    def _():
        wg_bf[...] = wg_ref[0].astype(BF16)
        wu_bf[...] = wu_ref[0].astype(BF16)
        wd_bf[...] = wd_ref[0].astype(BF16)

    base = 0
    for s in range(ns):
        jj = j - base
        live = jnp.logical_and(jnp.logical_and(jj >= 0, jj < nblocks[s]), jj * xs_refs[s].shape[1] < tot_sms[s][e])

        @pl.when(live)
        def _(xs_ref=xs_refs[s], ye_ref=ye_refs[s]):
            x = xs_ref[0]
            g = jnp.dot(x, wg_bf[...], preferred_element_type=F32)
            u = jnp.dot(x, wu_bf[...], preferred_element_type=F32)
            hid = (_silu(g) * u).astype(BF16)
            ye_ref[0] = jnp.dot(hid, wd_bf[...], preferred_element_type=F32).astype(ye_ref.dtype)

        base += nblocks[s]


def _experts(sets, w_gate, w_up, w_down):
    ns = len(sets)
    ne, _, d = sets[0][0].shape
    f = w_gate.shape[2]
    nblocks = tuple(used // eb for _, _, used, eb in sets)

    def block_spec(s):
        eb = sets[s][3]
        base = sum(nblocks[:s])

        def index(e, j, *tots):
            return (e, jnp.clip(j - base, 0, (tots[s][e] - 1) // eb), 0)

        return pl.BlockSpec((1, eb, d), index)

    wspec = lambda shape: pl.BlockSpec(shape, lambda e, j, *tots: (e, 0, 0))
    return pl.pallas_call(
        functools.partial(_expert_kernel, nblocks=nblocks),
        grid_spec=pltpu.PrefetchScalarGridSpec(
            num_scalar_prefetch=ns,
            grid=(ne, sum(nblocks)),
            in_specs=[block_spec(s) for s in range(ns)] + [wspec((1, d, f)), wspec((1, d, f)), wspec((1, f, d))],
            out_specs=[block_spec(s) for s in range(ns)],
            scratch_shapes=[pltpu.VMEM((d, f), BF16), pltpu.VMEM((d, f), BF16), pltpu.VMEM((f, d), BF16)]),
        out_shape=[jax.ShapeDtypeStruct((ne, used, d), BF16) for _, _, used, _ in sets],
        compiler_params=_cparams(("arbitrary", "arbitrary")),
        name="expert_ffn",
    )(*[t for _, t, _, _ in sets], *[x for x, _, _, _ in sets], w_gate, w_up, w_down)


def _combine_kernel(start_sm, cnt_sm, tot_sm, x1_ref, rank_ref, aff_ref, g2_ref, nf_ref, ye_hbm, o_ref, buf, sem, *,
                    final_norm, eb):
    i = pl.program_id(0)
    nt = pl.num_programs(0)
    ne = N_EXPERTS
    slot = i % 2
    rank = rank_ref[0]
    aff = aff_ref[0]
    sub = lax.broadcasted_iota(jnp.int32, (SLOT_ROWS, rank.shape[1]), 0)
    dn_t = (((0,), (0,)), ((), ()))

    def window(t, e, r):
        want = start_sm[t, e] + r * SLOT_ROWS
        written = ((tot_sm[e] + eb - 1) // eb) * eb
        off = jnp.minimum(want, written - SLOT_ROWS)
        return pl.multiple_of(off, ROW_ALIGN), want - off

    def copy(t, s, e, r):
        off, _ = window(t, e, r)
        return pltpu.make_async_copy(ye_hbm.at[e, pl.ds(off, SLOT_ROWS)],
                                     buf.at[s, pl.ds(e * SLOT_ROWS, SLOT_ROWS)], sem.at[s, e])

    def weighted(r):
        parts = []
        for e in range(ne):
            _, shift = window(i, e, r)
            local = rank[e:e + 1, :] - r * SLOT_ROWS
            hit = jnp.where(local >= 0, local + shift, -1) == sub
            parts.append(jnp.where(hit, aff[e:e + 1, :], 0.0).astype(BF16))
        return lax.dot_general(jnp.concatenate(parts, axis=0), buf[slot], dn_t, preferred_element_type=F32)

    @pl.when(i == 0)
    def _():
        for e in range(ne):
            copy(0, 0, e, 0).start()

    @pl.when(i + 1 < nt)
    def _prefetch_next_tile():
        for e in range(ne):
            copy(i + 1, 1 - slot, e, 0).start()

    for e in range(ne):
        copy(i, slot, e, 0).wait()
    y0 = weighted(0)

    def extra_round(r, acc):
        for e in range(ne):
            @pl.when(cnt_sm[i, e] > r * SLOT_ROWS)
            def _():
                copy(i, slot, e, r).start()
        for e in range(ne):
            @pl.when(cnt_sm[i, e] > r * SLOT_ROWS)
            def _():
                copy(i, slot, e, r).wait()
        return acc + weighted(r)

    y = lax.fori_loop(1, _num_rounds(cnt_sm, i), extra_round, y0)
    x2 = x1_ref[...] + g2_ref[...] * y.reshape(x1_ref.shape)
    if final_norm:
        ms = jnp.mean(x2 * x2, axis=-1, keepdims=True)
        x2 = x2 * lax.rsqrt(ms + EPS) * nf_ref[...]
    o_ref[...] = x2


def _combine(x1, rank, aff, start, cnt, tot, g2, norm_f_g, ye, cond_block, eb):
    g, s, d = x1.shape
    nt, ne, tm = rank.shape
    tok = pl.BlockSpec((g, tm // g, d), lambda i, *_: (0, i, 0))
    final_norm = norm_f_g is not None
    if not final_norm:
        norm_f_g = jnp.ones((d,), F32)
    return pl.pallas_call(
        functools.partial(_combine_kernel, final_norm=final_norm, eb=eb),
        grid_spec=pltpu.PrefetchScalarGridSpec(
            num_scalar_prefetch=3,
            grid=(nt,),
            in_specs=[tok,
                      pl.BlockSpec((1, ne, tm), lambda i, *_: (i, 0, 0)),
                      pl.BlockSpec((1, ne, tm), lambda i, *_: (i, 0, 0)),
                      pl.BlockSpec((g, 1, d), lambda i, *_: (cond_block, 0, 0)),
                      pl.BlockSpec((1, d), lambda i, *_: (0, 0)),
                      pl.BlockSpec(memory_space=pl.ANY)],
            out_specs=tok,
            scratch_shapes=[pltpu.VMEM((2, ne * SLOT_ROWS, d), BF16),
                            pltpu.SemaphoreType.DMA((2, ne))]),
        out_shape=jax.ShapeDtypeStruct((g, s, d), F32),
        compiler_params=_cparams(("arbitrary",)),
        name="expert_combine",
    )(start, cnt, tot, x1, rank, aff, g2, norm_f_g.reshape(1, d), ye)


def _layer_front(x, batch, seq_len, mods, cond_of_tile, moe_groups, cond_block, p, s0_f, s0_b, rope, emit_state):
    t, d = x.shape
    sh1, sc1, g1, sh2, sc2, g2 = mods
    u, x2c, qkvg, gates = _in_projection(x, p["norm1_g"], sc1, sh1, p["w_in"], p["hy_conv_w"], p["hy_conv_b"],
                                         cond_of_tile, seq_len, rope)
    seq = lambda a: a.reshape(batch, seq_len, a.shape[-1])
    y_hy = _hyena_long_conv(seq(u), seq(x2c), p).reshape(t, -1)
    ret = _retention(qkvg.reshape(batch, seq_len, -1), p["ret_decay_fwd"], p["ret_decay_bwd"], p["ret_gn_g"],
                     s0_f, s0_b, emit_state)
    y_ret = ret[0].reshape(t, -1)
    view = lambda a: a.reshape(moe_groups, t // moe_groups, a.shape[-1])
    x1, h2, aff = _out_projection(view(y_hy), view(y_ret), view(gates), view(x), g1, sc2, sh2, p["norm2_g"],
                                  p["w_hy_o"], p["w_ret_o"], p["w_out"], p["w_router"], cond_block)
    cap = (EC_CAPACITY_FACTOR * t) // N_EXPERTS
    eb = _expert_block(cap)
    rank, cnt, start, tot = _select(aff, cap, moe_groups)
    xs = _gather(h2, rank, start, cnt, tot, cap)
    routed = (xs, tot, _used_rows(cap, rank.shape[0]), eb)
    back = dict(x1=x1, rank=rank, aff=aff, start=start, cnt=cnt, tot=tot, g2=g2, cond_block=cond_block, eb=eb)
    return routed, back, ret[1:]


def _layer_back(back, ye, final_g):
    out = _combine(back["x1"], back["rank"], back["aff"], back["start"], back["cnt"], back["tot"], back["g2"],
                   final_g, ye, back["cond_block"], back["eb"])
    return out.reshape(-1, out.shape[-1])


def kernel(x_prompt, x_sample, state_ret_fwd, state_ret_bwd, c, c_ctx, w_mod, b_mod, norm1_g, w_in, hy_conv_w, hy_conv_b, hy_f_w1, hy_f_b1, hy_f_freq, hy_f_w2, hy_f_b2, hy_f_w3, hy_decay, hy_bias, ret_decay_fwd, ret_decay_bwd, ret_gn_g, w_hy_o, w_ret_o, w_out, norm2_g, w_router, w_e_gate, w_e_up, w_e_down, norm_f_g):
    b, s, d = x_prompt.shape
    bd, sd, _ = x_sample.shape
    depth = w_mod.shape[0]
    assert (b * s) % TOKEN_TILE == 0 and sd % TOKEN_TILE == 0 and b % 2 == 0 and bd % 2 == 0
    ncond = -(-(bd + 1) // 8) * 8
    lat_groups = bd if (TOKEN_TILE % (16 * bd) == 0 and sd % (TOKEN_TILE // bd) == 0) else 1
    assert lat_groups == bd or bd == 1
    conds = jnp.concatenate([c, c_ctx[None], jnp.zeros((ncond - bd - 1, d), F32)], axis=0)
    ctx_cond = lambda i: bd
    lat_cond = lambda i: i // (sd // TOKEN_TILE)
    xp = x_prompt.reshape(b * s, d)
    xs = x_sample.reshape(bd * sd, d)
    new_f, new_b = [], []
    for l in range(depth):
        mod = _modulation(conds, w_mod[l], b_mod[l])
        mods = [mod[:, k * d:(k + 1) * d].reshape(ncond, 1, d) for k in range(6)]
        p = {"norm1_g": norm1_g[l], "w_in": w_in[l].astype(BF16), "hy_conv_w": hy_conv_w[l],
             "hy_conv_b": hy_conv_b[l], "hy_f_w1": hy_f_w1[l], "hy_f_b1": hy_f_b1[l], "hy_f_freq": hy_f_freq[l],
             "hy_f_w2": hy_f_w2[l], "hy_f_b2": hy_f_b2[l], "hy_f_w3": hy_f_w3[l], "hy_decay": hy_decay[l],
             "hy_bias": hy_bias[l], "ret_decay_fwd": ret_decay_fwd[l], "ret_decay_bwd": ret_decay_bwd[l],
             "ret_gn_g": ret_gn_g[l], "w_hy_o": w_hy_o[l].astype(BF16), "w_ret_o": w_ret_o[l].astype(BF16),
             "w_out": w_out[l].astype(BF16), "norm2_g": norm2_g[l], "w_router": w_router[l],
             "w_e_gate": w_e_gate[l], "w_e_up": w_e_up[l], "w_e_down": w_e_down[l]}
        final_g = norm_f_g if l == depth - 1 else None
        routed_p, back_p, (s_f, s_b) = _layer_front(xp, b, s, mods, ctx_cond, 1, bd, p, None, None, False, True)
        new_f.append(s_f)
        new_b.append(s_b)
        routed_s, back_s, _ = _layer_front(xs, bd, sd, mods, lat_cond, lat_groups, 0, p, state_ret_fwd[:, l],
                                           state_ret_bwd[:, l], True, False)
        ye_p, ye_s = _experts([routed_p, routed_s], p["w_e_gate"], p["w_e_up"], p["w_e_down"])
        xp = _layer_back(back_p, ye_p, final_g)
        xs = _layer_back(back_s, ye_s, final_g)
    y_prompt = xp.reshape(b, s, d)
    y_sample = xs.reshape(bd, sd, d)
    return (y_prompt, y_sample, jnp.stack(new_f, axis=1), jnp.stack(new_b, axis=1))
```

```python
import functools
import math

import numpy as np
import jax
import jax.numpy as jnp
from jax import lax
from jax.experimental import pallas as pl
from jax.experimental.pallas import tpu as pltpu

F32 = jnp.float32
BF16 = jnp.bfloat16
HIGHEST = lax.Precision.HIGHEST

EPS = 1e-6
D_HYENA = 512
D_RET = 512
N_RET_HEADS = 4
RET_HEAD_DIM = 128
RET_CHUNK = 256
GRID_W = 64
FILTER_EMB = 33
ROPE_BASE = 10000.0
N_EXPERTS = 16
EC_CAPACITY_FACTOR = 2

TOKEN_TILE = 512
SLOT_ROWS = 96
ROW_ALIGN = 16
HALO_ROWS = 16
FFT_MINOR = 64
FFT_GROUP = 8
VMEM_LIMIT = 56 * 1024 * 1024


def _cparams(sem):
    return pltpu.CompilerParams(dimension_semantics=sem, vmem_limit_bytes=VMEM_LIMIT)


def _silu(x):
    return x * jax.nn.sigmoid(x)


def _mod_kernel(c_ref, w_ref, b_ref, o_ref):
    s = _silu(c_ref[...])
    o_ref[...] = jnp.dot(s, w_ref[...], preferred_element_type=F32, precision=HIGHEST) + b_ref[...]


def _modulation(conds, w_mod, b_mod):
    nc, d = conds.shape
    return pl.pallas_call(
        _mod_kernel,
        grid=(6,),
        in_specs=[pl.BlockSpec((nc, d), lambda j: (0, 0)),
                  pl.BlockSpec((d, d), lambda j: (0, j)),
                  pl.BlockSpec((1, d), lambda j: (0, j))],
        out_specs=pl.BlockSpec((nc, d), lambda j: (0, j)),
        out_shape=jax.ShapeDtypeStruct((nc, 6 * d), F32),
        compiler_params=_cparams(("arbitrary",)),
        name="modulation",
    )(conds, w_mod, b_mod.reshape(1, -1))


def _inproj_kernel(*refs, splits, rope, seq_len, halo):
    refs = list(refs)
    x_ref = refs.pop(0)
    if halo:
        xp_ref, xn_ref = refs.pop(0), refs.pop(0)
    g_ref, sc_ref, sh_ref, w_ref, cw_ref, cb_ref = refs[:6]
    del refs[:6]
    if rope:
        cos_ref, sin_ref = refs.pop(0), refs.pop(0)
    u_ref, x2c_ref, qkvg_ref, gates_ref, cv_sc, cx_sc = refs

    def normed(ref):
        x = ref[...]
        ms = jnp.mean(x * x, axis=-1, keepdims=True)
        h = x * lax.rsqrt(ms + EPS) * g_ref[...]
        return (h * (1.0 + sc_ref[0]) + sh_ref[0]).astype(BF16)

    hb = normed(x_ref)
    tm = hb.shape[0]
    nz, nq, ng = splits
    cw = 512
    dh = RET_HEAD_DIM
    hr = HALO_ROWS
    if halo:
        t0 = pl.program_id(0) * tm
        hp = jnp.where(t0 % seq_len != 0, normed(xp_ref), jnp.zeros((hr, hb.shape[1]), BF16))
        hn = jnp.where((t0 + tm) % seq_len != 0, normed(xn_ref), jnp.zeros((hr, hb.shape[1]), BF16))
        hext = jnp.concatenate([hp, hb, hn], axis=0)

    piece = tm if halo else seq_len
    pitch = piece + hr
    if not halo:
        for s in range(tm // piece + 1):
            cx_sc[s * pitch:s * pitch + hr, :] = jnp.zeros((hr, cw), F32)

    def conv(c0):
        taps = cw_ref[:, c0:c0 + cw]
        bias = cb_ref[:, c0:c0 + cw]
        if halo:
            cx_sc[...] = jnp.dot(hext, w_ref[:, c0:c0 + cw], preferred_element_type=F32)
        else:
            acc = jnp.dot(hb, w_ref[:, c0:c0 + cw], preferred_element_type=F32)
            for s in range(tm // piece):
                cx_sc[hr + s * pitch:hr + s * pitch + piece, :] = acc[s * piece:(s + 1) * piece]
        outs = []
        for s in range(tm // piece):
            r0 = hr + s * pitch
            outs.append(bias + cx_sc[r0 - 1:r0 - 1 + piece, :] * taps[0:1] + cx_sc[r0:r0 + piece, :] * taps[1:2]
                        + cx_sc[r0 + 1:r0 + 1 + piece, :] * taps[2:3])
        return outs[0] if len(outs) == 1 else jnp.concatenate(outs, axis=0)

    cv_sc[...] = conv(0)
    u_ref[...] = (cv_sc[...] * conv(D_HYENA)).astype(BF16)
    x2c_ref[...] = conv(2 * D_HYENA).astype(BF16)

    for c0 in range(nz, nz + nq + ng, cw):
        acc = jnp.dot(hb, w_ref[:, c0:c0 + cw], preferred_element_type=F32)
        if c0 < nz + nq:
            part = (c0 - nz) // D_RET
            if part == 1:
                acc = acc * (dh ** -0.5)
            if rope and part < 2:
                lane = lax.broadcasted_iota(jnp.int32, (acc.shape[0], dh), 1)
                swap_hi = (lane % (dh // 2)) < (dh // 4)
                cs, sn = cos_ref[...], sin_ref[...]
                heads = []
                for hh in range(cw // dh):
                    xh = acc[:, hh * dh:(hh + 1) * dh]
                    rot = jnp.where(swap_hi, pltpu.roll(xh, dh - dh // 4, axis=1), pltpu.roll(xh, dh // 4, axis=1))
                    heads.append(xh * cs + rot * sn)
                acc = jnp.concatenate(heads, axis=1)
            qkvg_ref[:, c0 - nz:c0 - nz + cw] = acc.astype(BF16)
        else:
            gates_ref[:, c0 - nz - nq:c0 - nz - nq + cw] = jax.nn.sigmoid(acc).astype(BF16)


def _in_projection(x, norm_g, sc, sh, w_in_bf, conv_w, conv_b, cond_of_tile, seq_len, rope):
    t, d = x.shape
    splits = (3 * D_HYENA, 4 * D_RET, 2 * d)
    assert D_RET == 512 and D_HYENA == 512
    tm = TOKEN_TILE
    assert seq_len % tm == 0 or tm % seq_len == 0
    halo = seq_len > tm
    row = lambda i: (i, 0)
    cond = lambda i: (cond_of_tile(i), 0, 0)
    full = lambda a: pl.BlockSpec(a.shape, lambda i: (0, 0))
    args, specs = [x], [pl.BlockSpec((tm, d), row)]
    if halo:
        hb_per_tile = tm // HALO_ROWS
        last = t // HALO_ROWS - 1
        args += [x, x]
        specs += [pl.BlockSpec((HALO_ROWS, d), lambda i: (jnp.maximum(i * hb_per_tile - 1, 0), 0)),
                  pl.BlockSpec((HALO_ROWS, d), lambda i: (jnp.minimum((i + 1) * hb_per_tile, last), 0))]
    cbias = conv_b.reshape(1, -1)
    args += [norm_g.reshape(1, d), sc, sh, w_in_bf, conv_w, cbias]
    specs += [pl.BlockSpec((1, d), lambda i: (0, 0)), pl.BlockSpec((1, 1, d), cond), pl.BlockSpec((1, 1, d), cond),
              full(w_in_bf), full(conv_w), full(cbias)]
    if rope:
        tiles_per_seq = seq_len // tm
        args += list(_rope_tables(seq_len))
        specs += [pl.BlockSpec((tm, RET_HEAD_DIM), lambda i: (i % tiles_per_seq, 0))] * 2
    return pl.pallas_call(
        functools.partial(_inproj_kernel, splits=splits, rope=rope, seq_len=seq_len, halo=halo),
        grid=(t // tm,),
        in_specs=specs,
        out_specs=[pl.BlockSpec((tm, D_HYENA), row),
                   pl.BlockSpec((tm, D_HYENA), row),
                   pl.BlockSpec((tm, splits[1]), row),
                   pl.BlockSpec((tm, splits[2]), row)],
        out_shape=[jax.ShapeDtypeStruct((t, D_HYENA), BF16),
                   jax.ShapeDtypeStruct((t, D_HYENA), BF16),
                   jax.ShapeDtypeStruct((t, splits[1]), BF16),
                   jax.ShapeDtypeStruct((t, splits[2]), BF16)],
        scratch_shapes=[pltpu.VMEM((tm, 512), F32),
                        pltpu.VMEM((tm + 2 * HALO_ROWS if halo else (tm // seq_len) * (seq_len + HALO_ROWS) + HALO_ROWS,
                                    512), F32)],
        compiler_params=_cparams(("parallel",)),
        name="in_projection",
    )(*args)


def _filter_features(seq_len):
    t = np.linspace(0.0, 1.0, seq_len, dtype=np.float32)[:, None]
    bands = (FILTER_EMB - 1) // 2
    w = (np.float32(2.0 * math.pi) * np.arange(seq_len, dtype=np.float32)) / np.float32(seq_len)
    f = np.linspace(1e-4, bands - 1, bands, dtype=np.float32)
    ang = (w[:, None] * f[None, :]).astype(np.float64)
    z = np.concatenate([t, np.cos(ang), -np.sin(ang)], axis=-1).astype(np.float32)
    return np.pad(z, ((0, 0), (0, 128 - FILTER_EMB)))


def _filter_kernel(z_ref, w1_ref, b1_ref, fr_ref, w2_ref, b2_ref, w3_ref, dec_ref, o_ref):
    z = z_ref[...]
    fr = fr_ref[...]
    dot = functools.partial(jnp.dot, preferred_element_type=F32, precision=HIGHEST)
    h = jnp.sin(fr * (dot(z, w1_ref[...]) + b1_ref[...]))
    h = jnp.sin(fr * (dot(h, w2_ref[...]) + b2_ref[...]))
    h = dot(h, w3_ref[...]) * jnp.exp(-z[:, 0:1] * jnp.abs(dec_ref[...]))
    rows = h.shape[0]
    grow = pl.program_id(0) * rows + lax.broadcasted_iota(jnp.int32, (rows, 1), 0)
    o_ref[0] = h[:, :D_HYENA].astype(o_ref.dtype)
    o_ref[1] = jnp.zeros((rows, D_HYENA), o_ref.dtype)
    o_ref[2] = jnp.where(grow == 0, 0.0, h[:, D_HYENA:]).astype(o_ref.dtype)
    o_ref[3] = jnp.zeros((rows, D_HYENA), o_ref.dtype)


def _hyena_filters(seq_len, p, out_dtype):
    z = jnp.asarray(_filter_features(seq_len))
    w1 = jnp.pad(p["hy_f_w1"], ((0, 128 - FILTER_EMB), (0, 0)))
    rows = min(seq_len, 512)
    full = lambda a: pl.BlockSpec(a.shape, lambda i: (0,) * a.ndim)
    ops = [w1, p["hy_f_b1"].reshape(1, -1), p["hy_f_freq"].reshape(1, -1), p["hy_f_w2"],
           p["hy_f_b2"].reshape(1, -1), p["hy_f_w3"], p["hy_decay"].reshape(1, -1)]
    return pl.pallas_call(
        _filter_kernel,
        grid=(seq_len // rows,),
        in_specs=[pl.BlockSpec((rows, 128), lambda i: (i, 0))] + [full(a) for a in ops],
        out_specs=pl.BlockSpec((4, rows, D_HYENA), lambda i: (0, i, 0)),
        out_shape=jax.ShapeDtypeStruct((4, seq_len, D_HYENA), out_dtype),
        compiler_params=_cparams(("arbitrary",)),
        name="hyena_filter",
    )(z, *ops)


def _stacked_dft(n_out, n_in, modulus, scale=1.0):
    k = np.arange(n_out, dtype=np.int64)[:, None]
    n = np.arange(n_in, dtype=np.int64)[None, :]
    th = ((k * n) % modulus) * (2.0 * math.pi / modulus)
    c, s = np.cos(th), np.sin(th)
    fwd = np.block([[c, s], [-s, c]]).astype(np.float32)
    inv = (np.block([[c.T, -s.T], [s.T, c.T]]) * scale).astype(np.float32)
    return fwd, inv


def _second_level_tables(n_total, n1, n2):
    k1 = np.arange(n1, dtype=np.int64)[:, None, None]
    k2 = np.arange(n2, dtype=np.int64)[None, :, None]
    m = np.arange(n2, dtype=np.int64)[None, None, :]
    th = ((m * (k1 + n1 * k2)) % n_total) * (2.0 * math.pi / n_total)
    c, s = np.cos(th), np.sin(th)
    g = np.concatenate([np.concatenate([c, s], axis=2), np.concatenate([-s, c], axis=2)], axis=1)
    return g.astype(np.float32), np.swapaxes(g, 1, 2).astype(np.float32)


def _skewed(n):
    return n + 1


def _fft1_kernel(h_ref, f_ref, a_ref, u_sc, t_sc):
    n2 = FFT_MINOR
    _, seq_len, cb = h_ref.shape
    h1 = seq_len // n2
    nk = a_ref.shape[1]
    up, tp = _skewed(n2), _skewed(nk)
    for b in range(2):
        for q in range(h1):
            u_sc[b, q * up:q * up + n2, :] = h_ref[b, q * n2:(q + 1) * n2, :].astype(F32)

    def fwd(gi, carry):
        m0 = gi * FFT_GROUP
        cols = []
        for d in range(FFT_GROUP):
            xr = u_sc[0, pl.ds(m0 + d, h1, stride=up), :]
            xi = u_sc[1, pl.ds(m0 + d, h1, stride=up), :]
            cols.append(jnp.concatenate([xr, xi], axis=0))
        x = jnp.concatenate(cols, axis=1).astype(BF16)
        res = jnp.dot(f_ref[...], x, preferred_element_type=F32)
        for d in range(FFT_GROUP):
            t_sc[pl.ds((m0 + d) * tp, nk), :] = res[:, d * cb:(d + 1) * cb]
        return carry

    lax.fori_loop(0, n2 // FFT_GROUP, fwd, 0, unroll=2)

    def transpose(k, carry):
        a_ref[0, k] = t_sc[pl.ds(k, n2, stride=tp), :].astype(a_ref.dtype)
        return carry

    lax.fori_loop(0, nk, transpose, 0, unroll=8)


def _fft_first_level(h, f1):
    b, seq_len, c = h.shape
    cb = 128
    return pl.pallas_call(
        _fft1_kernel,
        grid=(b // 2, c // cb),
        in_specs=[pl.BlockSpec((2, seq_len, cb), lambda i, j: (i, 0, j)),
                  pl.BlockSpec(f1.shape, lambda i, j: (0, 0))],
        out_specs=pl.BlockSpec((1, f1.shape[0], FFT_MINOR, cb), lambda i, j: (i, 0, 0, j)),
        out_shape=jax.ShapeDtypeStruct((b // 2, f1.shape[0], FFT_MINOR, c), BF16),
        scratch_shapes=[pltpu.VMEM((2, (seq_len // FFT_MINOR) * _skewed(FFT_MINOR), cb), F32),
                        pltpu.VMEM((FFT_MINOR * _skewed(f1.shape[0]), cb), F32)],
        compiler_params=_cparams(("parallel", "parallel")),
        name="hyena_dft_level1",
    )(h, f1)


def _fft_s2f_kernel(a_ref, g_ref, kr_ref, ki_ref):
    _, _, kb, n2, c = a_ref.shape
    for kk in range(kb):
        g = g_ref[kk]
        hf = jnp.dot(g, a_ref[0, :, kk].reshape(2 * n2, c), preferred_element_type=F32)
        hb = jnp.dot(g, a_ref[1, :, kk].reshape(2 * n2, c), preferred_element_type=F32)
        kr_ref[kk] = hf[:n2] + hb[:n2]
        ki_ref[kk] = hf[n2:] - hb[n2:]


def _fft_filter_second_level(a, g, kb=8):
    _, _, n1, n2, c = a.shape
    spec = pl.BlockSpec((kb, n2, c), lambda i: (i, 0, 0))
    return pl.pallas_call(
        _fft_s2f_kernel,
        grid=(n1 // kb,),
        in_specs=[pl.BlockSpec((2, 2, kb, n2, c), lambda i: (0, 0, i, 0, 0)),
                  pl.BlockSpec((kb, 2 * n2, 2 * n2), lambda i: (i, 0, 0))],
        out_specs=[spec, spec],
        out_shape=[jax.ShapeDtypeStruct((n1, n2, c), F32)] * 2,
        compiler_params=_cparams(("parallel",)),
        name="hyena_filter_spectrum",
    )(a, g)


def _fft_s2_kernel(a_ref, g_ref, gt_ref, kr_ref, ki_ref, b_ref, r_sc):
    npairs, _, kb, n2, cb = a_ref.shape
    nq = cb // 128
    rp = _skewed(2 * n2)
    for kk in range(kb):
        a = jnp.concatenate([a_ref[p, :, kk].reshape(2 * n2, cb) for p in range(npairs)], axis=1)
        x = jnp.dot(g_ref[kk], a, preferred_element_type=F32)
        xr, xi = x[:n2], x[n2:]
        kr = jnp.concatenate([kr_ref[kk]] * npairs, axis=1)
        ki = jnp.concatenate([ki_ref[kk]] * npairs, axis=1)
        y = jnp.concatenate([xr * kr - xi * ki, xr * ki + xi * kr], axis=0).astype(BF16)
        res = jnp.dot(gt_ref[kk], y, preferred_element_type=F32)
        for q in range(npairs * nq):
            r_sc[q, kk * rp:kk * rp + 2 * n2, :] = res[:, q * 128:(q + 1) * 128]

    for p in range(npairs):
        def gather(row):
            return jnp.concatenate([r_sc[p * nq + q, pl.ds(row, kb, stride=rp), :] for q in range(nq)], axis=1)

        def transpose(m, carry):
            b_ref[p, 0, m] = gather(m).astype(b_ref.dtype)
            b_ref[p, 1, m] = gather(n2 + m).astype(b_ref.dtype)
            return carry

        lax.fori_loop(0, n2, transpose, 0, unroll=8)


def _fft_second_level(a, g, gt, kr, ki):
    p, _, n1, n2, c = a.shape
    kb, cb = 16, 256
    gspec = pl.BlockSpec((kb, 2 * n2, 2 * n2), lambda i, j: (i, 0, 0))
    kspec = pl.BlockSpec((kb, n2, cb), lambda i, j: (i, 0, j))
    return pl.pallas_call(
        _fft_s2_kernel,
        grid=(n1 // kb, c // cb),
        in_specs=[pl.BlockSpec((p, 2, kb, n2, cb), lambda i, j: (0, 0, i, 0, j)), gspec, gspec, kspec, kspec],
        out_specs=pl.BlockSpec((p, 2, n2, kb, cb), lambda i, j: (0, 0, 0, i, j)),
        out_shape=jax.ShapeDtypeStruct((p, 2, n2, n1, c), BF16),
        scratch_shapes=[pltpu.VMEM((p * cb // 128, kb * _skewed(2 * n2), 128), F32)],
        compiler_params=_cparams(("parallel", "parallel")),
        name="hyena_dft_level2",
    )(a, g, gt, kr, ki)


def _fft_s3_kernel(b_ref, f_ref, u_ref, x2_ref, bias_ref, o_ref, t_sc):
    _, _, n2, n1, cb = b_ref.shape
    nr = f_ref.shape[0]
    h1 = nr // 2
    tp = _skewed(nr)

    def inv(gi, carry):
        m0 = gi * FFT_GROUP
        x = jnp.concatenate([b_ref[0, :, m0 + d].reshape(2 * n1, cb) for d in range(FFT_GROUP)], axis=1)
        res = jnp.dot(f_ref[...], x, preferred_element_type=F32)
        for d in range(FFT_GROUP):
            t_sc[pl.ds((m0 + d) * tp, nr), :] = res[:, d * cb:(d + 1) * cb]
        return carry

    lax.fori_loop(0, n2 // FFT_GROUP, inv, 0, unroll=2)

    bias = bias_ref[...]
    for b in range(2):
        def finish(q, carry):
            rows = pl.ds(pl.multiple_of(q * n2, n2), n2)
            conv = t_sc[pl.ds(b * h1 + q, n2, stride=tp), :]
            u = u_ref[b, rows, :].astype(F32)
            o_ref[b, rows, :] = ((conv + u * bias) * x2_ref[b, rows, :].astype(F32)).astype(o_ref.dtype)
            return carry

        lax.fori_loop(0, h1, finish, 0, unroll=4)


def _fft_last_level(bt, f1inv, u, x2c, bias):
    p, _, n2, n1, c = bt.shape
    b, seq_len, _ = u.shape
    cb = 128
    uspec = pl.BlockSpec((2, seq_len, cb), lambda i, j: (i, 0, j))
    return pl.pallas_call(
        _fft_s3_kernel,
        grid=(p, c // cb),
        in_specs=[pl.BlockSpec((1, 2, n2, n1, cb), lambda i, j: (i, 0, 0, 0, j)),
                  pl.BlockSpec(f1inv.shape, lambda i, j: (0, 0)),
                  uspec, uspec,
                  pl.BlockSpec((1, cb), lambda i, j: (0, j))],
        out_specs=uspec,
        out_shape=jax.ShapeDtypeStruct(u.shape, BF16),
        scratch_shapes=[pltpu.VMEM((n2 * _skewed(f1inv.shape[0]), cb), F32)],
        compiler_params=_cparams(("parallel", "parallel")),
        name="hyena_dft_inverse",
    )(bt, f1inv, u, x2c, bias)


def _kf_direct_kernel(h_ref, f_ref, kr_ref, ki_ref):
    n = kr_ref.shape[0]
    dot = functools.partial(jnp.dot, preferred_element_type=F32, precision=HIGHEST)
    hf = dot(f_ref[...], h_ref[0])
    hb = dot(f_ref[...], h_ref[2])
    kr_ref[...] = hf[:n] + hb[:n]
    ki_ref[...] = hf[n:] - hb[n:]


def _hyena_direct_kernel(u_ref, x2_ref, f_ref, fi_ref, kr_ref, ki_ref, bias_ref, o_ref):
    _, two, seq_len, cb = u_ref.shape
    n = kr_ref.shape[0]
    ub = u_ref[0].reshape(two * seq_len, cb)
    x = jnp.dot(f_ref[...], ub, preferred_element_type=F32)
    xr, xi = x[:n], x[n:]
    kr, ki = kr_ref[...], ki_ref[...]
    y = jnp.concatenate([xr * kr - xi * ki, xr * ki + xi * kr], axis=0).astype(BF16)
    conv = jnp.dot(fi_ref[...], y, preferred_element_type=F32)
    x2 = x2_ref[0].reshape(two * seq_len, cb).astype(F32)
    o = (conv + ub.astype(F32) * bias_ref[...]) * x2
    o_ref[0] = o.reshape(two, seq_len, cb).astype(o_ref.dtype)


def _hyena_long_conv(u, x2c, p):
    b, seq_len, _ = u.shape
    c = D_HYENA
    n = 2 * seq_len
    bias = p["hy_bias"].reshape(1, c)
    hh = _hyena_filters(seq_len, p, F32)
    if seq_len <= 512:
        fwd, inv = _stacked_dft(n, seq_len, n, scale=1.0 / n)
        kr, ki = pl.pallas_call(
            _kf_direct_kernel,
            out_shape=[jax.ShapeDtypeStruct((n, c), F32)] * 2,
            compiler_params=_cparams(None),
            name="hyena_filter_spectrum_direct",
        )(hh, fwd[:, :seq_len])
        cb = 512
        pair = lambda a: a.reshape(b // 2, 2, seq_len, c)
        uspec = pl.BlockSpec((1, 2, seq_len, cb), lambda i, j: (i, 0, 0, j))
        kspec = pl.BlockSpec((n, cb), lambda i, j: (0, j))
        out = pl.pallas_call(
            _hyena_direct_kernel,
            grid=(b // 2, c // cb),
            in_specs=[uspec, uspec,
                      pl.BlockSpec(fwd.shape, lambda i, j: (0, 0)),
                      pl.BlockSpec(inv.shape, lambda i, j: (0, 0)),
                      kspec, kspec,
                      pl.BlockSpec((1, cb), lambda i, j: (0, j))],
            out_specs=uspec,
            out_shape=jax.ShapeDtypeStruct((b // 2, 2, seq_len, c), BF16),
            compiler_params=_cparams(("parallel", "parallel")),
            name="hyena_dft_direct",
        )(pair(u), pair(x2c), jnp.asarray(fwd).astype(BF16), jnp.asarray(inv).astype(BF16), kr, ki, bias)
        return out.reshape(b, seq_len, c)

    n2 = FFT_MINOR
    n1 = n // n2
    h1 = seq_len // n2
    f1, f1inv = _stacked_dft(n1, h1, n1, scale=1.0 / n)
    f1, f1inv = jnp.asarray(f1).astype(BF16), jnp.asarray(f1inv).astype(BF16)
    g, gt = _second_level_tables(n, n1, n2)
    g, gt = jnp.asarray(g).astype(BF16), jnp.asarray(gt).astype(BF16)
    ha = _fft_first_level(hh, f1)
    kr, ki = _fft_filter_second_level(ha.reshape(2, 2, n1, n2, c), g)
    a = _fft_first_level(u, f1)
    bt = _fft_second_level(a.reshape(b // 2, 2, n1, n2, c), g, gt, kr, ki)
    return _fft_last_level(bt, f1inv, u, x2c, bias)


def _rope_tables(seq_len):
    half = RET_HEAD_DIM // 2
    nf = half // 2
    t = np.arange(seq_len)
    inv = ROPE_BASE ** (-np.arange(nf, dtype=np.float64) / nf)
    ar = (t // GRID_W)[:, None] * inv[None, :]
    ac = (t % GRID_W)[:, None] * inv[None, :]
    cos = np.concatenate([np.cos(ar), np.cos(ar), np.cos(ac), np.cos(ac)], axis=-1)
    sin = np.concatenate([-np.sin(ar), np.sin(ar), -np.sin(ac), np.sin(ac)], axis=-1)
    return cos.astype(np.float32), sin.astype(np.float32)


def _log_sigmoid(x):
    return jnp.minimum(x, 0.0) - jnp.log1p(jnp.exp(-jnp.abs(x)))


def _retention_kernel(*refs, has_init, emit_state, cpb):
    refs = list(refs)
    q_ref, k_ref, v_ref, g_ref, dec_ref, gn_ref = refs[:6]
    del refs[:6]
    if has_init:
        s0f_ref, s0b_ref = refs[:2]
        del refs[:2]
    o_ref = refs.pop(0)
    if emit_state:
        sf_out, sb_out = refs[:2]
        del refs[:2]
    sf_ref, sb_ref, sball_ref = refs

    c = RET_CHUNK
    dh = RET_HEAD_DIM
    nh = N_RET_HEADS
    phase = pl.program_id(1)
    j = pl.program_id(2)
    nb = pl.num_programs(2)

    diff = (lax.broadcasted_iota(jnp.int32, (c, c), 0) - lax.broadcasted_iota(jnp.int32, (c, c), 1)).astype(F32)
    ri = lax.broadcasted_iota(jnp.int32, (c, dh), 0).astype(F32)

    def head_consts(h):
        lgf = _log_sigmoid(dec_ref[0, h])[0:1, :]
        lgb = _log_sigmoid(dec_ref[1, h])[0:1, :]
        return lgf, lgb

    def chunk_wide(lg):
        return jnp.concatenate([lg] * (c // dh), axis=1)

    bb = q_ref.shape[0]
    heads = [(bi, h) for bi in range(bb) for h in range(nh)]

    def load(ref, bi, r0, h):
        return ref[bi, r0:r0 + c, h * dh:(h + 1) * dh]

    dn_t = (((0,), (0,)), ((), ()))
    dn_nt = (((1,), (1,)), ((), ()))

    @pl.when(phase == 0)
    def _backward_sweep():
        @pl.when(j == 0)
        def _():
            for bi, h in heads:
                sb_ref[bi * nh + h] = s0b_ref[bi, h] if has_init else jnp.zeros((dh, dh), F32)

        blk = nb - 1 - j
        for bi, h in heads:
            hs = bi * nh + h
            _, lgb = head_consts(h)
            zeta_b = jnp.exp(lgb * ri)
            cdec_b = jnp.exp(lgb * float(c))
            for cc in reversed(range(cpb)):
                r0 = cc * c
                n = blk * cpb + cc
                s = sb_ref[hs]
                sball_ref[n, hs] = s.astype(BF16)
                kz = (load(k_ref, bi, r0, h).astype(F32) * zeta_b).astype(BF16)
                vv = load(v_ref, bi, r0, h)
                sb_ref[hs] = cdec_b * s + lax.dot_general(kz, vv, dn_t, preferred_element_type=F32)

        if emit_state:
            @pl.when(j == nb - 1)
            def _():
                for bi, h in heads:
                    sb_out[bi, h] = sb_ref[bi * nh + h]

    @pl.when(phase == 1)
    def _forward_sweep():
        @pl.when(j == 0)
        def _():
            for bi, h in heads:
                sf_ref[bi * nh + h] = s0f_ref[bi, h] if has_init else jnp.zeros((dh, dh), F32)

        for bi, h in heads:
            hs = bi * nh + h
            lgf, lgb = head_consts(h)
            mask = (jnp.where(diff >= 0, jnp.exp(chunk_wide(lgf) * jnp.maximum(diff, 0.0)), 0.0)
                    + jnp.where(diff <= 0, jnp.exp(chunk_wide(lgb) * jnp.maximum(-diff, 0.0)), 0.0))
            xi_f = jnp.exp(lgf * (ri + 1.0))
            xi_b = jnp.exp(lgb * (float(c) - ri))
            zeta_f = jnp.exp(lgf * (float(c - 1) - ri))
            cdec_f = jnp.exp(lgf * float(c))
            gn = gn_ref[:, h * dh:(h + 1) * dh]
            for cc in range(cpb):
                r0 = cc * c
                n = j * cpb + cc
                qb = load(q_ref, bi, r0, h)
                kb = load(k_ref, bi, r0, h)
                vv = load(v_ref, bi, r0, h)
                gate = load(g_ref, bi, r0, h).astype(F32)
                sc = lax.dot_general(qb, kb, dn_nt, preferred_element_type=F32)
                inner = jnp.dot((sc * mask).astype(BF16), vv, preferred_element_type=F32)
                s = sf_ref[hs]
                q = qb.astype(F32)
                lhs = jnp.concatenate([q * xi_f, q * xi_b], axis=1).astype(BF16)
                rhs = jnp.concatenate([s.astype(BF16), sball_ref[n, hs]], axis=0)
                o = inner + jnp.dot(lhs, rhs, preferred_element_type=F32)
                mu = jnp.mean(o, axis=-1, keepdims=True)
                d = o - mu
                var = jnp.mean(d * d, axis=-1, keepdims=True)
                y = d * lax.rsqrt(var + EPS) * gn * _silu(gate)
                o_ref[bi, r0:r0 + c, h * dh:(h + 1) * dh] = y.astype(o_ref.dtype)
                kz = (kb.astype(F32) * zeta_f).astype(BF16)
                sf_ref[hs] = cdec_f * s + lax.dot_general(kz, vv, dn_t, preferred_element_type=F32)

        if emit_state:
            @pl.when(j == nb - 1)
            def _():
                for bi, h in heads:
                    sf_out[bi, h] = sf_ref[bi * nh + h]


def _retention(qkvg, dec_f, dec_b, gn_g, s0_f, s0_b, emit_state):
    b, seq_len, _ = qkvg.shape
    nh, dh, c = N_RET_HEADS, RET_HEAD_DIM, RET_CHUNK
    rb = min(seq_len, 1024)
    nb = seq_len // rb
    cpb = rb // c
    bb = max(1, min(4, 1024 // seq_len))
    while b % bb:
        bb //= 2
    has_init = s0_f is not None
    dec = jnp.broadcast_to(jnp.stack([dec_f, dec_b])[:, :, None, None], (2, nh, 8, 128)).astype(F32)
    kv_blk = lambda i, p, j: jnp.where(p == 0, nb - 1 - j, j)
    q_blk = lambda i, p, j: jnp.where(p == 0, 0, j)
    in_specs = [pl.BlockSpec((bb, rb, D_RET), lambda i, p, j: (i, q_blk(i, p, j), 0)),
                pl.BlockSpec((bb, rb, D_RET), lambda i, p, j: (i, kv_blk(i, p, j), 1)),
                pl.BlockSpec((bb, rb, D_RET), lambda i, p, j: (i, kv_blk(i, p, j), 2)),
                pl.BlockSpec((bb, rb, D_RET), lambda i, p, j: (i, q_blk(i, p, j), 3)),
                pl.BlockSpec((2, nh, 8, 128), lambda i, p, j: (0, 0, 0, 0)),
                pl.BlockSpec((1, D_RET), lambda i, p, j: (0, 0))]
    args = [qkvg, qkvg, qkvg, qkvg, dec, gn_g.reshape(1, -1)]
    sspec = pl.BlockSpec((bb, nh, dh, dh), lambda i, p, j: (i, 0, 0, 0))
    if has_init:
        in_specs += [sspec, sspec]
        args += [s0_f, s0_b]
    out_specs = [pl.BlockSpec((bb, rb, D_RET), lambda i, p, j: (i, q_blk(i, p, j), 0))]
    out_shape = [jax.ShapeDtypeStruct((b, seq_len, D_RET), BF16)]
    if emit_state:
        out_specs += [sspec, sspec]
        out_shape += [jax.ShapeDtypeStruct((b, nh, dh, dh), F32)] * 2
    return pl.pallas_call(
        functools.partial(_retention_kernel, has_init=has_init, emit_state=emit_state, cpb=cpb),
        grid=(b // bb, 2, nb),
        in_specs=in_specs,
        out_specs=out_specs,
        out_shape=out_shape,
        scratch_shapes=[pltpu.VMEM((bb * nh, dh, dh), F32), pltpu.VMEM((bb * nh, dh, dh), F32),
                        pltpu.VMEM((nb * cpb, bb * nh, dh, dh), BF16)],
        compiler_params=_cparams(("parallel", "arbitrary", "arbitrary")),
        name="retention",
    )(*args)


def _outproj_kernel(yhy_ref, yret_ref, gates_ref, x_ref, g1_ref, sc_ref, sh_ref, ng_ref,
                    why_ref, wret_ref, wout_ref, wr_ref, x1_ref, h2_ref, aff_ref):
    g, m, d = x_ref.shape
    rows = lambda ref: ref[...].reshape(g * m, ref.shape[2])
    a = jnp.dot(rows(yhy_ref), why_ref[...], preferred_element_type=F32)
    b = jnp.dot(rows(yret_ref), wret_ref[...], preferred_element_type=F32)
    gates = rows(gates_ref)
    merged = gates[:, :d].astype(F32) * a + gates[:, d:].astype(F32) * b
    out = jnp.dot(merged.astype(BF16), wout_ref[...], preferred_element_type=F32)
    x1 = x_ref[...] + g1_ref[...] * out.reshape(g, m, d)
    x1_ref[...] = x1
    ms = jnp.mean(x1 * x1, axis=-1, keepdims=True)
    h = x1 * lax.rsqrt(ms + EPS) * ng_ref[...]
    h = (h * (1.0 + sc_ref[...]) + sh_ref[...]).reshape(g * m, d)
    h_hi = h.astype(BF16)
    h2_ref[...] = h_hi.reshape(g, m, d)
    h_lo = (h - h_hi.astype(F32)).astype(BF16)
    t = jnp.dot(h_hi, wr_ref[...], preferred_element_type=F32)
    logits = t[:, :128] + t[:, 128:] + jnp.dot(h_lo, wr_ref[:, :128], preferred_element_type=F32)
    lane = lax.broadcasted_iota(jnp.int32, logits.shape, 1)
    logits = jnp.where(lane < N_EXPERTS, logits, -jnp.inf)
    e = jnp.exp(logits - jnp.max(logits, axis=-1, keepdims=True))
    aff = e / jnp.sum(e, axis=-1, keepdims=True)
    aff_ref[0] = aff.T[:N_EXPERTS, :]


def _out_projection(y_hy, y_ret, gates, x, g1, sc2, sh2, norm2_g, w_hy_o, w_ret_o, w_out, w_router, cond_block):
    g, s, d = x.shape
    tm = TOKEN_TILE
    m = tm // g
    tok = lambda c: pl.BlockSpec((g, m, c), lambda i: (0, i, 0))
    cond = pl.BlockSpec((g, 1, d), lambda i: (cond_block, 0, 0))
    full = lambda a: pl.BlockSpec(a.shape, lambda i: (0, 0))
    wr = jnp.pad(w_router, ((0, 0), (0, 128 - N_EXPERTS)))
    wr_hi = wr.astype(BF16)
    wr = jnp.concatenate([wr_hi, (wr - wr_hi.astype(F32)).astype(BF16)], axis=1)
    return pl.pallas_call(
        _outproj_kernel,
        grid=(s // m,),
        in_specs=[tok(D_HYENA), tok(D_RET), tok(2 * d), tok(d), cond, cond, cond,
                  pl.BlockSpec((1, d), lambda i: (0, 0)),
                  full(w_hy_o), full(w_ret_o), full(w_out), full(wr)],
        out_specs=[tok(d), tok(d), pl.BlockSpec((1, N_EXPERTS, tm), lambda i: (i, 0, 0))],
        out_shape=[jax.ShapeDtypeStruct((g, s, d), F32), jax.ShapeDtypeStruct((g, s, d), BF16),
                   jax.ShapeDtypeStruct((s // m, N_EXPERTS, tm), F32)],
        compiler_params=_cparams(("parallel",)),
        name="out_projection_router",
    )(y_hy, y_ret, gates, x, g1, sc2, sh2, norm2_g.reshape(1, d), w_hy_o, w_ret_o, w_out, wr)


def _select_kernel(aff_ref, tri_ref, rank_ref, cnt_ref, start_ref, tot_ref, *, cap, idx_bits, groups):
    nt, ne, tm = aff_ref.shape
    a = aff_ref[...]

    def count(m):
        return jnp.sum(jnp.sum(m, axis=0, keepdims=True), axis=2, keepdims=True)

    def thr_step(s, thr):
        cand = thr | (1 << (30 - s))
        cnt = count(jnp.where(a >= pltpu.bitcast(cand, F32), 1.0, 0.0))
        return jnp.where(cnt >= float(cap), cand, thr)

    thr = pltpu.bitcast(lax.fori_loop(0, 31, thr_step, jnp.zeros((1, ne, 1), jnp.int32)), F32)
    gt = a > thr
    eq = a == thr
    need = float(cap) - count(jnp.where(gt, 1.0, 0.0))
    m = tm // groups
    tile = lax.broadcasted_iota(jnp.int32, (nt, 1, tm), 0)
    lane = lax.broadcasted_iota(jnp.int32, (nt, 1, tm), 2)
    idx = (lane // m) * (nt * m) + tile * m + lane % m

    def idx_step(s, lim):
        cand = lim | (1 << (idx_bits - 1 - s))
        cnt = count(jnp.where(eq, jnp.where(idx < cand, 1.0, 0.0), 0.0))
        return jnp.where(cnt < need, cand, lim)

    lim = lax.fori_loop(0, idx_bits, idx_step, jnp.zeros((1, ne, 1), jnp.int32))
    sel = jnp.where(gt, 1.0, jnp.where(eq, jnp.where(idx <= lim, 1.0, 0.0), 0.0))
    sel2 = sel.reshape(nt * ne, tm).astype(BF16)
    prefix = jnp.dot(sel2, tri_ref[...], preferred_element_type=F32)
    rank = jnp.where(sel2 > 0, prefix, -1.0).astype(jnp.int32)
    rank_ref[...] = rank.reshape(nt, ne, tm)
    ones = jnp.ones((tm, 128), BF16)
    cnt = jnp.dot(sel2, ones, preferred_element_type=F32).astype(jnp.int32).reshape(nt, ne, 128)
    cnt_ref[...] = cnt
    acc = jnp.zeros((ne, 128), jnp.int32)
    for t in range(nt):
        start_ref[t] = acc
        acc = acc + ((cnt[t] + (ROW_ALIGN - 1)) & (-ROW_ALIGN))
    tot_ref[...] = acc


def _select(aff, cap, groups):
    nt, ne, tm = aff.shape
    idx_bits = max(1, int(math.ceil(math.log2(nt * tm))))
    r = lax.broadcasted_iota(jnp.int32, (tm, tm), 0)
    c = lax.broadcasted_iota(jnp.int32, (tm, tm), 1)
    tri = (r < c).astype(BF16)
    rank, cnt, start, tot = pl.pallas_call(
        functools.partial(_select_kernel, cap=cap, idx_bits=idx_bits, groups=groups),
        out_shape=[jax.ShapeDtypeStruct((nt, ne, tm), jnp.int32),
                   jax.ShapeDtypeStruct((nt, ne, 128), jnp.int32),
                   jax.ShapeDtypeStruct((nt, ne, 128), jnp.int32),
                   jax.ShapeDtypeStruct((ne, 128), jnp.int32)],
        compiler_params=_cparams(None),
        name="expert_choice_select",
    )(aff, tri)
    return rank, cnt[:, :, 0], start[:, :, 0], tot[:, 0]


def _expert_block(cap):
    return 512 if cap >= 2048 else 256


def _used_rows(cap, nt):
    eb = _expert_block(cap)
    return -(-(cap + ROW_ALIGN * nt) // eb) * eb


def _list_rows(cap, nt):
    max_rounds = -(-TOKEN_TILE // SLOT_ROWS)
    return _used_rows(cap, nt) + max(_expert_block(cap), max_rounds * SLOT_ROWS)


def _num_rounds(cnt_sm, i):
    m = cnt_sm[i, 0]
    for e in range(1, N_EXPERTS):
        m = jnp.maximum(m, cnt_sm[i, e])
    return jnp.maximum((m + SLOT_ROWS - 1) // SLOT_ROWS, 1)


def _gather_kernel(start_sm, cnt_sm, tot_sm, h_ref, rank_ref, xs_hbm, stage, zbuf, sem):
    i = pl.program_id(0)
    ne = N_EXPERTS
    slot = i % 2
    rank = rank_ref[0]
    sub = lax.broadcasted_iota(jnp.int32, (SLOT_ROWS, rank.shape[1]), 0)

    def copy(s, e, off):
        return pltpu.make_async_copy(stage.at[s, pl.ds(e * SLOT_ROWS, SLOT_ROWS)],
                                     xs_hbm.at[e, pl.ds(off, SLOT_ROWS)], sem.at[e])

    def fill(r):
        h = h_ref[...].reshape(rank.shape[1], h_ref.shape[2])
        for e in range(ne):
            onehot = jnp.where(rank[e:e + 1, :] == sub + r * SLOT_ROWS, 1.0, 0.0).astype(BF16)
            stage[slot, e * SLOT_ROWS:(e + 1) * SLOT_ROWS, :] = jnp.dot(
                onehot, h, preferred_element_type=F32).astype(BF16)

    def start_all(r):
        for e in range(ne):
            copy(slot, e, pl.multiple_of(start_sm[i, e] + r * SLOT_ROWS, ROW_ALIGN)).start()

    def wait_all(s):
        for e in range(ne):
            copy(s, e, 0).wait()

    fill(0)

    @pl.when(i > 0)
    def _():
        wait_all(1 - slot)

    start_all(0)

    def extra_round(r, carry):
        wait_all(slot)
        fill(r)
        start_all(r)
        return carry

    lax.fori_loop(1, _num_rounds(cnt_sm, i), extra_round, 0)

    @pl.when(i == pl.num_programs(0) - 1)
    def _zero_tail():
        wait_all(slot)
        zbuf[...] = jnp.zeros(zbuf.shape, zbuf.dtype)

        def zcopy(e):
            off = pl.multiple_of(tot_sm[e], ROW_ALIGN)
            return pltpu.make_async_copy(zbuf, xs_hbm.at[e, pl.ds(off, zbuf.shape[0])], sem.at[e])

        for e in range(ne):
            zcopy(e).start()
        for e in range(ne):
            zcopy(e).wait()


def _gather(h2, rank, start, cnt, tot, cap):
    g, _, d = h2.shape
    nt, ne, tm = rank.shape
    rl = _list_rows(cap, nt)
    return pl.pallas_call(
        _gather_kernel,
        grid_spec=pltpu.PrefetchScalarGridSpec(
            num_scalar_prefetch=3,
            grid=(nt,),
            in_specs=[pl.BlockSpec((g, tm // g, d), lambda i, *_: (0, i, 0)),
                      pl.BlockSpec((1, ne, tm), lambda i, *_: (i, 0, 0))],
            out_specs=pl.BlockSpec(memory_space=pl.ANY),
            scratch_shapes=[pltpu.VMEM((2, ne * SLOT_ROWS, d), BF16),
                            pltpu.VMEM((_expert_block(cap), d), BF16),
                            pltpu.SemaphoreType.DMA((ne,))]),
        out_shape=jax.ShapeDtypeStruct((ne, rl, d), BF16),
        compiler_params=_cparams(("arbitrary",)),
        name="expert_gather",
    )(start, cnt, tot, h2, rank)


def _expert_kernel(*refs, nblocks):
    ns = len(nblocks)
    tot_sms, xs_refs = refs[:ns], refs[ns:2 * ns]
    wg_ref, wu_ref, wd_ref = refs[2 * ns:2 * ns + 3]
    ye_refs = refs[2 * ns + 3:3 * ns + 3]
    wg_bf, wu_bf, wd_bf = refs[3 * ns + 3:]
    e = pl.program_id(0)
    j = pl.program_id(1)

    @pl.when(j == 0)
    def _():
        wg_bf[...] = wg_ref[0].astype(BF16)
        wu_bf[...] = wu_ref[0].astype(BF16)
        wd_bf[...] = wd_ref[0].astype(BF16)

    base = 0
    for s in range(ns):
        jj = j - base
        live = jnp.logical_and(jnp.logical_and(jj >= 0, jj < nblocks[s]), jj * xs_refs[s].shape[1] < tot_sms[s][e])

        @pl.when(live)
        def _(xs_ref=xs_refs[s], ye_ref=ye_refs[s]):
            x = xs_ref[0]
            g = jnp.dot(x, wg_bf[...], preferred_element_type=F32)
            u = jnp.dot(x, wu_bf[...], preferred_element_type=F32)
            hid = (_silu(g) * u).astype(BF16)
            ye_ref[0] = jnp.dot(hid, wd_bf[...], preferred_element_type=F32).astype(ye_ref.dtype)

        base += nblocks[s]


def _experts(sets, w_gate, w_up, w_down):
    ns = len(sets)
    ne, _, d = sets[0][0].shape
    f = w_gate.shape[2]
    nblocks = tuple(used // eb for _, _, used, eb in sets)

    def block_spec(s):
        eb = sets[s][3]
        base = sum(nblocks[:s])

        def index(e, j, *tots):
            return (e, jnp.clip(j - base, 0, (tots[s][e] - 1) // eb), 0)

        return pl.BlockSpec((1, eb, d), index)

    wspec = lambda shape: pl.BlockSpec(shape, lambda e, j, *tots: (e, 0, 0))
    return pl.pallas_call(
        functools.partial(_expert_kernel, nblocks=nblocks),
        grid_spec=pltpu.PrefetchScalarGridSpec(
            num_scalar_prefetch=ns,
            grid=(ne, sum(nblocks)),
            in_specs=[block_spec(s) for s in range(ns)] + [wspec((1, d, f)), wspec((1, d, f)), wspec((1, f, d))],
            out_specs=[block_spec(s) for s in range(ns)],
            scratch_shapes=[pltpu.VMEM((d, f), BF16), pltpu.VMEM((d, f), BF16), pltpu.VMEM((f, d), BF16)]),
        out_shape=[jax.ShapeDtypeStruct((ne, used, d), BF16) for _, _, used, _ in sets],
        compiler_params=_cparams(("arbitrary", "arbitrary")),
        name="expert_ffn",
    )(*[t for _, t, _, _ in sets], *[x for x, _, _, _ in sets], w_gate, w_up, w_down)


def _combine_kernel(start_sm, cnt_sm, tot_sm, x1_ref, rank_ref, aff_ref, g2_ref, nf_ref, ye_hbm, o_ref, buf, sem, *,
                    final_norm, eb):
    i = pl.program_id(0)
    nt = pl.num_programs(0)
    ne = N_EXPERTS
    slot = i % 2
    rank = rank_ref[0]
    aff = aff_ref[0]
    sub = lax.broadcasted_iota(jnp.int32, (SLOT_ROWS, rank.shape[1]), 0)
    dn_t = (((0,), (0,)), ((), ()))

    def window(t, e, r):
        want = start_sm[t, e] + r * SLOT_ROWS
        written = ((tot_sm[e] + eb - 1) // eb) * eb
        off = jnp.minimum(want, written - SLOT_ROWS)
        return pl.multiple_of(off, ROW_ALIGN), want - off

    def copy(t, s, e, r):
        off, _ = window(t, e, r)
        return pltpu.make_async_copy(ye_hbm.at[e, pl.ds(off, SLOT_ROWS)],
                                     buf.at[s, pl.ds(e * SLOT_ROWS, SLOT_ROWS)], sem.at[s, e])

    def weighted(r):
        parts = []
        for e in range(ne):
            _, shift = window(i, e, r)
            local = rank[e:e + 1, :] - r * SLOT_ROWS
            hit = jnp.where(local >= 0, local + shift, -1) == sub
            parts.append(jnp.where(hit, aff[e:e + 1, :], 0.0).astype(BF16))
        return lax.dot_general(jnp.concatenate(parts, axis=0), buf[slot], dn_t, preferred_element_type=F32)

    @pl.when(i == 0)
    def _():
        for e in range(ne):
            copy(0, 0, e, 0).start()

    @pl.when(i + 1 < nt)
    def _prefetch_next_tile():
        for e in range(ne):
            copy(i + 1, 1 - slot, e, 0).start()

    for e in range(ne):
        copy(i, slot, e, 0).wait()
    y0 = weighted(0)

    def extra_round(r, acc):
        for e in range(ne):
            @pl.when(cnt_sm[i, e] > r * SLOT_ROWS)
            def _():
                copy(i, slot, e, r).start()
        for e in range(ne):
            @pl.when(cnt_sm[i, e] > r * SLOT_ROWS)
            def _():
                copy(i, slot, e, r).wait()
        return acc + weighted(r)

    y = lax.fori_loop(1, _num_rounds(cnt_sm, i), extra_round, y0)
    x2 = x1_ref[...] + g2_ref[...] * y.reshape(x1_ref.shape)
    if final_norm:
        ms = jnp.mean(x2 * x2, axis=-1, keepdims=True)
        x2 = x2 * lax.rsqrt(ms + EPS) * nf_ref[...]
    o_ref[...] = x2


def _combine(x1, rank, aff, start, cnt, tot, g2, norm_f_g, ye, cond_block, eb):
    g, s, d = x1.shape
    nt, ne, tm = rank.shape
    tok = pl.BlockSpec((g, tm // g, d), lambda i, *_: (0, i, 0))
    final_norm = norm_f_g is not None
    if not final_norm:
        norm_f_g = jnp.ones((d,), F32)
    return pl.pallas_call(
        functools.partial(_combine_kernel, final_norm=final_norm, eb=eb),
        grid_spec=pltpu.PrefetchScalarGridSpec(
            num_scalar_prefetch=3,
            grid=(nt,),
            in_specs=[tok,
                      pl.BlockSpec((1, ne, tm), lambda i, *_: (i, 0, 0)),
                      pl.BlockSpec((1, ne, tm), lambda i, *_: (i, 0, 0)),
                      pl.BlockSpec((g, 1, d), lambda i, *_: (cond_block, 0, 0)),
                      pl.BlockSpec((1, d), lambda i, *_: (0, 0)),
                      pl.BlockSpec(memory_space=pl.ANY)],
            out_specs=tok,
            scratch_shapes=[pltpu.VMEM((2, ne * SLOT_ROWS, d), BF16),
                            pltpu.SemaphoreType.DMA((2, ne))]),
        out_shape=jax.ShapeDtypeStruct((g, s, d), F32),
        compiler_params=_cparams(("arbitrary",)),
        name="expert_combine",
    )(start, cnt, tot, x1, rank, aff, g2, norm_f_g.reshape(1, d), ye)


def _layer_front(x, batch, seq_len, mods, cond_of_tile, moe_groups, cond_block, p, s0_f, s0_b, rope, emit_state):
    t, d = x.shape
    sh1, sc1, g1, sh2, sc2, g2 = mods
    u, x2c, qkvg, gates = _in_projection(x, p["norm1_g"], sc1, sh1, p["w_in"], p["hy_conv_w"], p["hy_conv_b"],
                                         cond_of_tile, seq_len, rope)
    seq = lambda a: a.reshape(batch, seq_len, a.shape[-1])
    y_hy = _hyena_long_conv(seq(u), seq(x2c), p).reshape(t, -1)
    ret = _retention(qkvg.reshape(batch, seq_len, -1), p["ret_decay_fwd"], p["ret_decay_bwd"], p["ret_gn_g"],
                     s0_f, s0_b, emit_state)
    y_ret = ret[0].reshape(t, -1)
    view = lambda a: a.reshape(moe_groups, t // moe_groups, a.shape[-1])
    x1, h2, aff = _out_projection(view(y_hy), view(y_ret), view(gates), view(x), g1, sc2, sh2, p["norm2_g"],
                                  p["w_hy_o"], p["w_ret_o"], p["w_out"], p["w_router"], cond_block)
    cap = (EC_CAPACITY_FACTOR * t) // N_EXPERTS
    eb = _expert_block(cap)
    rank, cnt, start, tot = _select(aff, cap, moe_groups)
    xs = _gather(h2, rank, start, cnt, tot, cap)
    routed = (xs, tot, _used_rows(cap, rank.shape[0]), eb)
    back = dict(x1=x1, rank=rank, aff=aff, start=start, cnt=cnt, tot=tot, g2=g2, cond_block=cond_block, eb=eb)
    return routed, back, ret[1:]


def _layer_back(back, ye, final_g):
    out = _combine(back["x1"], back["rank"], back["aff"], back["start"], back["cnt"], back["tot"], back["g2"],
                   final_g, ye, back["cond_block"], back["eb"])
    return out.reshape(-1, out.shape[-1])


def kernel(x_prompt, x_sample, state_ret_fwd, state_ret_bwd, c, c_ctx, w_mod, b_mod, norm1_g, w_in, hy_conv_w, hy_conv_b, hy_f_w1, hy_f_b1, hy_f_freq, hy_f_w2, hy_f_b2, hy_f_w3, hy_decay, hy_bias, ret_decay_fwd, ret_decay_bwd, ret_gn_g, w_hy_o, w_ret_o, w_out, norm2_g, w_router, w_e_gate, w_e_up, w_e_down, norm_f_g):
    b, s, d = x_prompt.shape
    bd, sd, _ = x_sample.shape
    depth = w_mod.shape[0]
    assert (b * s) % TOKEN_TILE == 0 and sd % TOKEN_TILE == 0 and b % 2 == 0 and bd % 2 == 0
    ncond = -(-(bd + 1) // 8) * 8
    lat_groups = bd if (TOKEN_TILE % (16 * bd) == 0 and sd % (TOKEN_TILE // bd) == 0) else 1
    assert lat_groups == bd or bd == 1
    conds = jnp.concatenate([c, c_ctx[None], jnp.zeros((ncond - bd - 1, d), F32)], axis=0)
    ctx_cond = lambda i: bd
    lat_cond = lambda i: i // (sd // TOKEN_TILE)
    xp = x_prompt.reshape(b * s, d)
    xs = x_sample.reshape(bd * sd, d)
    new_f, new_b = [], []
    for l in range(depth):
        mod = _modulation(conds, w_mod[l], b_mod[l])
        mods = [mod[:, k * d:(k + 1) * d].reshape(ncond, 1, d) for k in range(6)]
        p = {"norm1_g": norm1_g[l], "w_in": w_in[l].astype(BF16), "hy_conv_w": hy_conv_w[l],
             "hy_conv_b": hy_conv_b[l], "hy_f_w1": hy_f_w1[l], "hy_f_b1": hy_f_b1[l], "hy_f_freq": hy_f_freq[l],
             "hy_f_w2": hy_f_w2[l], "hy_f_b2": hy_f_b2[l], "hy_f_w3": hy_f_w3[l], "hy_decay": hy_decay[l],
             "hy_bias": hy_bias[l], "ret_decay_fwd": ret_decay_fwd[l], "ret_decay_bwd": ret_decay_bwd[l],
             "ret_gn_g": ret_gn_g[l], "w_hy_o": w_hy_o[l].astype(BF16), "w_ret_o": w_ret_o[l].astype(BF16),
             "w_out": w_out[l].astype(BF16), "norm2_g": norm2_g[l], "w_router": w_router[l],
             "w_e_gate": w_e_gate[l], "w_e_up": w_e_up[l], "w_e_down": w_e_down[l]}
        final_g = norm_f_g if l == depth - 1 else None
        routed_p, back_p, (s_f, s_b) = _layer_front(xp, b, s, mods, ctx_cond, 1, bd, p, None, None, False, True)
        new_f.append(s_f)
        new_b.append(s_b)
        routed_s, back_s, _ = _layer_front(xs, bd, sd, mods, lat_cond, lat_groups, 0, p, state_ret_fwd[:, l],
                                           state_ret_bwd[:, l], True, False)
        ye_p, ye_s = _experts([routed_p, routed_s], p["w_e_gate"], p["w_e_up"], p["w_e_down"])
        xp = _layer_back(back_p, ye_p, final_g)
        xs = _layer_back(back_s, ye_s, final_g)
    y_prompt = xp.reshape(b, s, d)
    y_sample = xs.reshape(bd, sd, d)
    return (y_prompt, y_sample, jnp.stack(new_f, axis=1), jnp.stack(new_b, axis=1))
```

```python
import functools
import math

import numpy as np
import jax
import jax.numpy as jnp
from jax import lax
from jax.experimental import pallas as pl
from jax.experimental.pallas import tpu as pltpu

F32 = jnp.float32
BF16 = jnp.bfloat16
HIGHEST = lax.Precision.HIGHEST

EPS = 1e-6
D_HYENA = 512
D_RET = 512
N_RET_HEADS = 4
RET_HEAD_DIM = 128
RET_CHUNK = 256
GRID_W = 64
FILTER_EMB = 33
ROPE_BASE = 10000.0
N_EXPERTS = 16
EC_CAPACITY_FACTOR = 2

TOKEN_TILE = 512
SLOT_ROWS = 96
ROW_ALIGN = 16
HALO_ROWS = 16
FFT_MINOR = 64
FFT_GROUP = 16
VMEM_LIMIT = 56 * 1024 * 1024


def _cparams(sem):
    return pltpu.CompilerParams(dimension_semantics=sem, vmem_limit_bytes=VMEM_LIMIT)


def _silu(x):
    return x * jax.nn.sigmoid(x)


def _mod_kernel(c_ref, w_ref, b_ref, o_ref):
    s = _silu(c_ref[...])
    o_ref[...] = jnp.dot(s, w_ref[...], preferred_element_type=F32, precision=HIGHEST) + b_ref[...]


def _modulation(conds, w_mod, b_mod):
    nc, d = conds.shape
    return pl.pallas_call(
        _mod_kernel,
        grid=(6,),
        in_specs=[pl.BlockSpec((nc, d), lambda j: (0, 0)),
                  pl.BlockSpec((d, d), lambda j: (0, j)),
                  pl.BlockSpec((1, d), lambda j: (0, j))],
        out_specs=pl.BlockSpec((nc, d), lambda j: (0, j)),
        out_shape=jax.ShapeDtypeStruct((nc, 6 * d), F32),
        compiler_params=_cparams(("arbitrary",)),
        name="modulation",
    )(conds, w_mod, b_mod.reshape(1, -1))


def _inproj_kernel(*refs, splits, rope, seq_len, halo):
    refs = list(refs)
    x_ref = refs.pop(0)
    if halo:
        xp_ref, xn_ref = refs.pop(0), refs.pop(0)
    g_ref, sc_ref, sh_ref, w_ref, cw_ref, cb_ref = refs[:6]
    del refs[:6]
    if rope:
        cos_ref, sin_ref = refs.pop(0), refs.pop(0)
    u_ref, x2c_ref, qkvg_ref, gates_ref, cv_sc, cx_sc = refs

    def normed(ref):
        x = ref[...]
        ms = jnp.mean(x * x, axis=-1, keepdims=True)
        h = x * lax.rsqrt(ms + EPS) * g_ref[...]
        return (h * (1.0 + sc_ref[0]) + sh_ref[0]).astype(BF16)

    hb = normed(x_ref)
    tm = hb.shape[0]
    nz, nq, ng = splits
    cw = 512
    dh = RET_HEAD_DIM
    hr = HALO_ROWS
    if halo:
        t0 = pl.program_id(0) * tm
        hp = jnp.where(t0 % seq_len != 0, normed(xp_ref), jnp.zeros((hr, hb.shape[1]), BF16))
        hn = jnp.where((t0 + tm) % seq_len != 0, normed(xn_ref), jnp.zeros((hr, hb.shape[1]), BF16))
        hext = jnp.concatenate([hp, hb, hn], axis=0)

    piece = tm if halo else seq_len
    pitch = piece + hr
    if not halo:
        for s in range(tm // piece + 1):
            cx_sc[s * pitch:s * pitch + hr, :] = jnp.zeros((hr, cw), F32)

    def conv(c0):
        taps = cw_ref[:, c0:c0 + cw]
        bias = cb_ref[:, c0:c0 + cw]
        if halo:
            cx_sc[...] = jnp.dot(hext, w_ref[:, c0:c0 + cw], preferred_element_type=F32)
        else:
            acc = jnp.dot(hb, w_ref[:, c0:c0 + cw], preferred_element_type=F32)
            for s in range(tm // piece):
                cx_sc[hr + s * pitch:hr + s * pitch + piece, :] = acc[s * piece:(s + 1) * piece]
        outs = []
        for s in range(tm // piece):
            r0 = hr + s * pitch
            outs.append(bias + cx_sc[r0 - 1:r0 - 1 + piece, :] * taps[0:1] + cx_sc[r0:r0 + piece, :] * taps[1:2]
                        + cx_sc[r0 + 1:r0 + 1 + piece, :] * taps[2:3])
        return outs[0] if len(outs) == 1 else jnp.concatenate(outs, axis=0)

    cv_sc[...] = conv(0)
    u_ref[...] = (cv_sc[...] * conv(D_HYENA)).astype(BF16)
    x2c_ref[...] = conv(2 * D_HYENA).astype(BF16)

    for c0 in range(nz, nz + nq + ng, cw):
        acc = jnp.dot(hb, w_ref[:, c0:c0 + cw], preferred_element_type=F32)
        if c0 < nz + nq:
            part = (c0 - nz) // D_RET
            if part == 1:
                acc = acc * (dh ** -0.5)
            if rope and part < 2:
                lane = lax.broadcasted_iota(jnp.int32, (acc.shape[0], dh), 1)
                swap_hi = (lane % (dh // 2)) < (dh // 4)
                cs, sn = cos_ref[...], sin_ref[...]
                heads = []
                for hh in range(cw // dh):
                    xh = acc[:, hh * dh:(hh + 1) * dh]
                    rot = jnp.where(swap_hi, pltpu.roll(xh, dh - dh // 4, axis=1), pltpu.roll(xh, dh // 4, axis=1))
                    heads.append(xh * cs + rot * sn)
                acc = jnp.concatenate(heads, axis=1)
            qkvg_ref[:, c0 - nz:c0 - nz + cw] = acc.astype(BF16)
        else:
            gates_ref[:, c0 - nz - nq:c0 - nz - nq + cw] = jax.nn.sigmoid(acc).astype(BF16)


def _in_projection(x, norm_g, sc, sh, w_in_bf, conv_w, conv_b, cond_of_tile, seq_len, rope):
    t, d = x.shape
    splits = (3 * D_HYENA, 4 * D_RET, 2 * d)
    assert D_RET == 512 and D_HYENA == 512
    tm = TOKEN_TILE
    assert seq_len % tm == 0 or tm % seq_len == 0
    halo = seq_len > tm
    row = lambda i: (i, 0)
    cond = lambda i: (cond_of_tile(i), 0, 0)
    full = lambda a: pl.BlockSpec(a.shape, lambda i: (0, 0))
    args, specs = [x], [pl.BlockSpec((tm, d), row)]
    if halo:
        hb_per_tile = tm // HALO_ROWS
        last = t // HALO_ROWS - 1
        args += [x, x]
        specs += [pl.BlockSpec((HALO_ROWS, d), lambda i: (jnp.maximum(i * hb_per_tile - 1, 0), 0)),
                  pl.BlockSpec((HALO_ROWS, d), lambda i: (jnp.minimum((i + 1) * hb_per_tile, last), 0))]
    cbias = conv_b.reshape(1, -1)
    args += [norm_g.reshape(1, d), sc, sh, w_in_bf, conv_w, cbias]
    specs += [pl.BlockSpec((1, d), lambda i: (0, 0)), pl.BlockSpec((1, 1, d), cond), pl.BlockSpec((1, 1, d), cond),
              full(w_in_bf), full(conv_w), full(cbias)]
    if rope:
        tiles_per_seq = seq_len // tm
        args += list(_rope_tables(seq_len))
        specs += [pl.BlockSpec((tm, RET_HEAD_DIM), lambda i: (i % tiles_per_seq, 0))] * 2
    return pl.pallas_call(
        functools.partial(_inproj_kernel, splits=splits, rope=rope, seq_len=seq_len, halo=halo),
        grid=(t // tm,),
        in_specs=specs,
        out_specs=[pl.BlockSpec((tm, D_HYENA), row),
                   pl.BlockSpec((tm, D_HYENA), row),
                   pl.BlockSpec((tm, splits[1]), row),
                   pl.BlockSpec((tm, splits[2]), row)],
        out_shape=[jax.ShapeDtypeStruct((t, D_HYENA), BF16),
                   jax.ShapeDtypeStruct((t, D_HYENA), BF16),
                   jax.ShapeDtypeStruct((t, splits[1]), BF16),
                   jax.ShapeDtypeStruct((t, splits[2]), BF16)],
        scratch_shapes=[pltpu.VMEM((tm, 512), F32),
                        pltpu.VMEM((tm + 2 * HALO_ROWS if halo else (tm // seq_len) * (seq_len + HALO_ROWS) + HALO_ROWS,
                                    512), F32)],
        compiler_params=_cparams(("parallel",)),
        name="in_projection",
    )(*args)


def _filter_features(seq_len):
    t = np.linspace(0.0, 1.0, seq_len, dtype=np.float32)[:, None]
    bands = (FILTER_EMB - 1) // 2
    w = (np.float32(2.0 * math.pi) * np.arange(seq_len, dtype=np.float32)) / np.float32(seq_len)
    f = np.linspace(1e-4, bands - 1, bands, dtype=np.float32)
    ang = (w[:, None] * f[None, :]).astype(np.float64)
    z = np.concatenate([t, np.cos(ang), -np.sin(ang)], axis=-1).astype(np.float32)
    return np.pad(z, ((0, 0), (0, 128 - FILTER_EMB)))


def _filter_kernel(z_ref, w1_ref, b1_ref, fr_ref, w2_ref, b2_ref, w3_ref, dec_ref, o_ref):
    z = z_ref[...]
    fr = fr_ref[...]
    dot = functools.partial(jnp.dot, preferred_element_type=F32, precision=HIGHEST)
    h = jnp.sin(fr * (dot(z, w1_ref[...]) + b1_ref[...]))
    h = jnp.sin(fr * (dot(h, w2_ref[...]) + b2_ref[...]))
    split = lambda a: (a.astype(BF16), (a - a.astype(BF16).astype(F32)).astype(BF16))
    (h_hi, h_lo), (w_hi, w_lo) = split(h), split(w3_ref[...])
    bdot = functools.partial(jnp.dot, preferred_element_type=F32)
    h = (bdot(h_hi, w_hi) + bdot(h_hi, w_lo) + bdot(h_lo, w_hi)) * jnp.exp(-z[:, 0:1] * jnp.abs(dec_ref[...]))
    rows = h.shape[0]
    grow = pl.program_id(0) * rows + lax.broadcasted_iota(jnp.int32, (rows, 1), 0)
    o_ref[0] = h[:, :D_HYENA].astype(o_ref.dtype)
    o_ref[1] = jnp.zeros((rows, D_HYENA), o_ref.dtype)
    o_ref[2] = jnp.where(grow == 0, 0.0, h[:, D_HYENA:]).astype(o_ref.dtype)
    o_ref[3] = jnp.zeros((rows, D_HYENA), o_ref.dtype)


def _hyena_filters(seq_len, p, out_dtype):
    z = jnp.asarray(_filter_features(seq_len))
    w1 = jnp.pad(p["hy_f_w1"], ((0, 128 - FILTER_EMB), (0, 0)))
    rows = min(seq_len, 512)
    full = lambda a: pl.BlockSpec(a.shape, lambda i: (0,) * a.ndim)
    ops = [w1, p["hy_f_b1"].reshape(1, -1), p["hy_f_freq"].reshape(1, -1), p["hy_f_w2"],
           p["hy_f_b2"].reshape(1, -1), p["hy_f_w3"], p["hy_decay"].reshape(1, -1)]
    return pl.pallas_call(
        _filter_kernel,
        grid=(seq_len // rows,),
        in_specs=[pl.BlockSpec((rows, 128), lambda i: (i, 0))] + [full(a) for a in ops],
        out_specs=pl.BlockSpec((4, rows, D_HYENA), lambda i: (0, i, 0)),
        out_shape=jax.ShapeDtypeStruct((4, seq_len, D_HYENA), out_dtype),
        compiler_params=_cparams(("arbitrary",)),
        name="hyena_filter",
    )(z, *ops)


def _stacked_dft(n_out, n_in, modulus, scale=1.0):
    k = np.arange(n_out, dtype=np.int64)[:, None]
    n = np.arange(n_in, dtype=np.int64)[None, :]
    th = ((k * n) % modulus) * (2.0 * math.pi / modulus)
    c, s = np.cos(th), np.sin(th)
    fwd = np.block([[c, s], [-s, c]]).astype(np.float32)
    inv = (np.block([[c.T, -s.T], [s.T, c.T]]) * scale).astype(np.float32)
    return fwd, inv


def _second_level_tables(n_total, n1, n2):
    k1 = np.arange(n1, dtype=np.int64)[:, None, None]
    k2 = np.arange(n2, dtype=np.int64)[None, :, None]
    m = np.arange(n2, dtype=np.int64)[None, None, :]
    th = ((m * (k1 + n1 * k2)) % n_total) * (2.0 * math.pi / n_total)
    c, s = np.cos(th), np.sin(th)
    g = np.concatenate([np.concatenate([c, s], axis=2), np.concatenate([-s, c], axis=2)], axis=1)
    return g.astype(np.float32), np.swapaxes(g, 1, 2).astype(np.float32)


def _skewed(n):
    return n + 1


def _fft1_kernel(h_ref, f_ref, a_ref, u_sc, t_sc):
    n2 = FFT_MINOR
    _, seq_len, cb = h_ref.shape
    h1 = seq_len // n2
    nk = a_ref.shape[1]
    up, tp = _skewed(n2), _skewed(nk)
    for b in range(2):
        for q in range(h1):
            u_sc[b, q * up:q * up + n2, :] = h_ref[b, q * n2:(q + 1) * n2, :].astype(F32)

    def fwd(gi, carry):
        m0 = gi * FFT_GROUP
        cols = []
        for d in range(FFT_GROUP):
            xr = u_sc[0, pl.ds(m0 + d, h1, stride=up), :]
            xi = u_sc[1, pl.ds(m0 + d, h1, stride=up), :]
            cols.append(jnp.concatenate([xr, xi], axis=0))
        x = jnp.concatenate(cols, axis=1).astype(BF16)
        res = jnp.dot(f_ref[...], x, preferred_element_type=F32)
        for d in range(FFT_GROUP):
            t_sc[pl.ds((m0 + d) * tp, nk), :] = res[:, d * cb:(d + 1) * cb]
        return carry

    lax.fori_loop(0, n2 // FFT_GROUP, fwd, 0, unroll=2)

    def transpose(k, carry):
        a_ref[0, k] = t_sc[pl.ds(k, n2, stride=tp), :].astype(a_ref.dtype)
        return carry

    lax.fori_loop(0, nk, transpose, 0, unroll=8)


def _fft_first_level(h, f1):
    b, seq_len, c = h.shape
    cb = 128
    return pl.pallas_call(
        _fft1_kernel,
        grid=(b // 2, c // cb),
        in_specs=[pl.BlockSpec((2, seq_len, cb), lambda i, j: (i, 0, j)),
                  pl.BlockSpec(f1.shape, lambda i, j: (0, 0))],
        out_specs=pl.BlockSpec((1, f1.shape[0], FFT_MINOR, cb), lambda i, j: (i, 0, 0, j)),
        out_shape=jax.ShapeDtypeStruct((b // 2, f1.shape[0], FFT_MINOR, c), BF16),
        scratch_shapes=[pltpu.VMEM((2, (seq_len // FFT_MINOR) * _skewed(FFT_MINOR), cb), F32),
                        pltpu.VMEM((FFT_MINOR * _skewed(f1.shape[0]), cb), F32)],
        compiler_params=_cparams(("parallel", "parallel")),
        name="hyena_dft_level1",
    )(h, f1)


def _fft_s2f_kernel(a_ref, g_ref, kr_ref, ki_ref):
    _, _, kb, n2, c = a_ref.shape
    for kk in range(kb):
        g = g_ref[kk]
        hf = jnp.dot(g, a_ref[0, :, kk].reshape(2 * n2, c), preferred_element_type=F32)
        hb = jnp.dot(g, a_ref[1, :, kk].reshape(2 * n2, c), preferred_element_type=F32)
        kr_ref[kk] = hf[:n2] + hb[:n2]
        ki_ref[kk] = hf[n2:] - hb[n2:]


def _fft_filter_second_level(a, g, kb=16):
    _, _, n1, n2, c = a.shape
    spec = pl.BlockSpec((kb, n2, c), lambda i: (i, 0, 0))
    return pl.pallas_call(
        _fft_s2f_kernel,
        grid=(n1 // kb,),
        in_specs=[pl.BlockSpec((2, 2, kb, n2, c), lambda i: (0, 0, i, 0, 0)),
                  pl.BlockSpec((kb, 2 * n2, 2 * n2), lambda i: (i, 0, 0))],
        out_specs=[spec, spec],
        out_shape=[jax.ShapeDtypeStruct((n1, n2, c), F32)] * 2,
        compiler_params=_cparams(("parallel",)),
        name="hyena_filter_spectrum",
    )(a, g)


def _fft_s2_kernel(a_ref, g_ref, gt_ref, kr_ref, ki_ref, b_ref, r_sc):
    npairs, _, kb, n2, cb = a_ref.shape
    nq = cb // 128
    rp = _skewed(2 * n2)
    for kk in range(kb):
        a = jnp.concatenate([a_ref[p, :, kk].reshape(2 * n2, cb) for p in range(npairs)], axis=1)
        x = jnp.dot(g_ref[kk], a, preferred_element_type=F32)
        xr, xi = x[:n2], x[n2:]
        kr = jnp.concatenate([kr_ref[kk]] * npairs, axis=1)
        ki = jnp.concatenate([ki_ref[kk]] * npairs, axis=1)
        y = jnp.concatenate([xr * kr - xi * ki, xr * ki + xi * kr], axis=0).astype(BF16)
        res = jnp.dot(gt_ref[kk], y, preferred_element_type=F32)
        for q in range(npairs * nq):
            r_sc[q, kk * rp:kk * rp + 2 * n2, :] = res[:, q * 128:(q + 1) * 128]

    for p in range(npairs):
        def gather(row):
            return jnp.concatenate([r_sc[p * nq + q, pl.ds(row, kb, stride=rp), :] for q in range(nq)], axis=1)

        def transpose(m, carry):
            b_ref[p, 0, m] = gather(m).astype(b_ref.dtype)
            b_ref[p, 1, m] = gather(n2 + m).astype(b_ref.dtype)
            return carry

        lax.fori_loop(0, n2, transpose, 0, unroll=8)


def _fft_second_level(a, g, gt, kr, ki):
    p, _, n1, n2, c = a.shape
    kb, cb = 16, 256
    gspec = pl.BlockSpec((kb, 2 * n2, 2 * n2), lambda i, j: (i, 0, 0))
    kspec = pl.BlockSpec((kb, n2, cb), lambda i, j: (i, 0, j))
    return pl.pallas_call(
        _fft_s2_kernel,
        grid=(n1 // kb, c // cb),
        in_specs=[pl.BlockSpec((p, 2, kb, n2, cb), lambda i, j: (0, 0, i, 0, j)), gspec, gspec, kspec, kspec],
        out_specs=pl.BlockSpec((p, 2, n2, kb, cb), lambda i, j: (0, 0, 0, i, j)),
        out_shape=jax.ShapeDtypeStruct((p, 2, n2, n1, c), BF16),
        scratch_shapes=[pltpu.VMEM((p * cb // 128, kb * _skewed(2 * n2), 128), F32)],
        compiler_params=_cparams(("parallel", "parallel")),
        name="hyena_dft_level2",
    )(a, g, gt, kr, ki)


def _fft_s3_kernel(b_ref, f_ref, u_ref, x2_ref, bias_ref, o_ref, t_sc):
    _, _, n2, n1, cb = b_ref.shape
    nr = f_ref.shape[0]
    h1 = nr // 2
    tp = _skewed(nr)

    def inv(gi, carry):
        m0 = gi * FFT_GROUP
        x = jnp.concatenate([b_ref[0, :, m0 + d].reshape(2 * n1, cb) for d in range(FFT_GROUP)], axis=1)
        res = jnp.dot(f_ref[...], x, preferred_element_type=F32)
        for d in range(FFT_GROUP):
            t_sc[pl.ds((m0 + d) * tp, nr), :] = res[:, d * cb:(d + 1) * cb]
        return carry

    lax.fori_loop(0, n2 // FFT_GROUP, inv, 0, unroll=2)

    bias = bias_ref[...]
    for b in range(2):
        def finish(q, carry):
            rows = pl.ds(pl.multiple_of(q * n2, n2), n2)
            conv = t_sc[pl.ds(b * h1 + q, n2, stride=tp), :]
            u = u_ref[b, rows, :].astype(F32)
            o_ref[b, rows, :] = ((conv + u * bias) * x2_ref[b, rows, :].astype(F32)).astype(o_ref.dtype)
            return carry

        lax.fori_loop(0, h1, finish, 0, unroll=4)


def _fft_last_level(bt, f1inv, u, x2c, bias):
    p, _, n2, n1, c = bt.shape
    b, seq_len, _ = u.shape
    cb = 128
    uspec = pl.BlockSpec((2, seq_len, cb), lambda i, j: (i, 0, j))
    return pl.pallas_call(
        _fft_s3_kernel,
        grid=(p, c // cb),
        in_specs=[pl.BlockSpec((1, 2, n2, n1, cb), lambda i, j: (i, 0, 0, 0, j)),
                  pl.BlockSpec(f1inv.shape, lambda i, j: (0, 0)),
                  uspec, uspec,
                  pl.BlockSpec((1, cb), lambda i, j: (0, j))],
        out_specs=uspec,
        out_shape=jax.ShapeDtypeStruct(u.shape, BF16),
        scratch_shapes=[pltpu.VMEM((n2 * _skewed(f1inv.shape[0]), cb), F32)],
        compiler_params=_cparams(("parallel", "parallel")),
        name="hyena_dft_inverse",
    )(bt, f1inv, u, x2c, bias)


def _kf_direct_kernel(h_ref, f_ref, kr_ref, ki_ref):
    n = kr_ref.shape[0]
    dot = functools.partial(jnp.dot, preferred_element_type=F32, precision=HIGHEST)
    hf = dot(f_ref[...], h_ref[0])
    hb = dot(f_ref[...], h_ref[2])
    kr_ref[...] = hf[:n] + hb[:n]
    ki_ref[...] = hf[n:] - hb[n:]


def _hyena_direct_kernel(u_ref, x2_ref, f_ref, fi_ref, kr_ref, ki_ref, bias_ref, o_ref):
    _, two, seq_len, cb = u_ref.shape
    n = kr_ref.shape[0]
    ub = u_ref[0].reshape(two * seq_len, cb)
    x = jnp.dot(f_ref[...], ub, preferred_element_type=F32)
    xr, xi = x[:n], x[n:]
    kr, ki = kr_ref[...], ki_ref[...]
    y = jnp.concatenate([xr * kr - xi * ki, xr * ki + xi * kr], axis=0).astype(BF16)
    conv = jnp.dot(fi_ref[...], y, preferred_element_type=F32)
    x2 = x2_ref[0].reshape(two * seq_len, cb).astype(F32)
    o = (conv + ub.astype(F32) * bias_ref[...]) * x2
    o_ref[0] = o.reshape(two, seq_len, cb).astype(o_ref.dtype)


def _hyena_long_conv(u, x2c, p):
    b, seq_len, _ = u.shape
    c = D_HYENA
    n = 2 * seq_len
    bias = p["hy_bias"].reshape(1, c)
    hh = _hyena_filters(seq_len, p, F32)
    if seq_len <= 512:
        fwd, inv = _stacked_dft(n, seq_len, n, scale=1.0 / n)
        kr, ki = pl.pallas_call(
            _kf_direct_kernel,
            out_shape=[jax.ShapeDtypeStruct((n, c), F32)] * 2,
            compiler_params=_cparams(None),
            name="hyena_filter_spectrum_direct",
        )(hh, fwd[:, :seq_len])
        cb = 512
        pair = lambda a: a.reshape(b // 2, 2, seq_len, c)
        uspec = pl.BlockSpec((1, 2, seq_len, cb), lambda i, j: (i, 0, 0, j))
        kspec = pl.BlockSpec((n, cb), lambda i, j: (0, j))
        out = pl.pallas_call(
            _hyena_direct_kernel,
            grid=(b // 2, c // cb),
            in_specs=[uspec, uspec,
                      pl.BlockSpec(fwd.shape, lambda i, j: (0, 0)),
                      pl.BlockSpec(inv.shape, lambda i, j: (0, 0)),
                      kspec, kspec,
                      pl.BlockSpec((1, cb), lambda i, j: (0, j))],
            out_specs=uspec,
            out_shape=jax.ShapeDtypeStruct((b // 2, 2, seq_len, c), BF16),
            compiler_params=_cparams(("parallel", "parallel")),
            name="hyena_dft_direct",
        )(pair(u), pair(x2c), jnp.asarray(fwd).astype(BF16), jnp.asarray(inv).astype(BF16), kr, ki, bias)
        return out.reshape(b, seq_len, c)

    n2 = FFT_MINOR
    n1 = n // n2
    h1 = seq_len // n2
    f1, f1inv = _stacked_dft(n1, h1, n1, scale=1.0 / n)
    f1, f1inv = jnp.asarray(f1).astype(BF16), jnp.asarray(f1inv).astype(BF16)
    g, gt = _second_level_tables(n, n1, n2)
    g, gt = jnp.asarray(g).astype(BF16), jnp.asarray(gt).astype(BF16)
    ha = _fft_first_level(hh, f1)
    kr, ki = _fft_filter_second_level(ha.reshape(2, 2, n1, n2, c), g)
    a = _fft_first_level(u, f1)
    bt = _fft_second_level(a.reshape(b // 2, 2, n1, n2, c), g, gt, kr, ki)
    return _fft_last_level(bt, f1inv, u, x2c, bias)


def _rope_tables(seq_len):
    half = RET_HEAD_DIM // 2
    nf = half // 2
    t = np.arange(seq_len)
    inv = ROPE_BASE ** (-np.arange(nf, dtype=np.float64) / nf)
    ar = (t // GRID_W)[:, None] * inv[None, :]
    ac = (t % GRID_W)[:, None] * inv[None, :]
    cos = np.concatenate([np.cos(ar), np.cos(ar), np.cos(ac), np.cos(ac)], axis=-1)
    sin = np.concatenate([-np.sin(ar), np.sin(ar), -np.sin(ac), np.sin(ac)], axis=-1)
    return cos.astype(np.float32), sin.astype(np.float32)


def _log_sigmoid(x):
    return jnp.minimum(x, 0.0) - jnp.log1p(jnp.exp(-jnp.abs(x)))


def _retention_kernel(*refs, has_init, emit_state, cpb):
    refs = list(refs)
    q_ref, k_ref, v_ref, g_ref, dec_ref, gn_ref = refs[:6]
    del refs[:6]
    if has_init:
        s0f_ref, s0b_ref = refs[:2]
        del refs[:2]
    o_ref = refs.pop(0)
    if emit_state:
        sf_out, sb_out = refs[:2]
        del refs[:2]
    sf_ref, sb_ref, sball_ref = refs

    c = RET_CHUNK
    dh = RET_HEAD_DIM
    nh = N_RET_HEADS
    phase = pl.program_id(1)
    j = pl.program_id(2)
    nb = pl.num_programs(2)

    diff = (lax.broadcasted_iota(jnp.int32, (c, c), 0) - lax.broadcasted_iota(jnp.int32, (c, c), 1)).astype(F32)
    ri = lax.broadcasted_iota(jnp.int32, (c, dh), 0).astype(F32)

    def head_consts(h):
        lgf = _log_sigmoid(dec_ref[0, h])[0:1, :]
        lgb = _log_sigmoid(dec_ref[1, h])[0:1, :]
        return lgf, lgb

    def chunk_wide(lg):
        return jnp.concatenate([lg] * (c // dh), axis=1)

    bb = q_ref.shape[0]
    heads = [(bi, h) for bi in range(bb) for h in range(nh)]

    def load(ref, bi, r0, h):
        return ref[bi, r0:r0 + c, h * dh:(h + 1) * dh]

    dn_t = (((0,), (0,)), ((), ()))
    dn_nt = (((1,), (1,)), ((), ()))

    @pl.when(phase == 0)
    def _backward_sweep():
        @pl.when(j == 0)
        def _():
            for bi, h in heads:
                sb_ref[bi * nh + h] = s0b_ref[bi, h] if has_init else jnp.zeros((dh, dh), F32)

        blk = nb - 1 - j
        for bi, h in heads:
            hs = bi * nh + h
            _, lgb = head_consts(h)
            zeta_b = jnp.exp(lgb * ri)
            cdec_b = jnp.exp(lgb * float(c))
            for cc in reversed(range(cpb)):
                r0 = cc * c
                n = blk * cpb + cc
                s = sb_ref[hs]
                sball_ref[n, hs] = s.astype(BF16)
                kz = (load(k_ref, bi, r0, h).astype(F32) * zeta_b).astype(BF16)
                vv = load(v_ref, bi, r0, h)
                sb_ref[hs] = cdec_b * s + lax.dot_general(kz, vv, dn_t, preferred_element_type=F32)

        if emit_state:
            @pl.when(j == nb - 1)
            def _():
                for bi, h in heads:
                    sb_out[bi, h] = sb_ref[bi * nh + h]

    @pl.when(phase == 1)
    def _forward_sweep():
        @pl.when(j == 0)
        def _():
            for bi, h in heads:
                sf_ref[bi * nh + h] = s0f_ref[bi, h] if has_init else jnp.zeros((dh, dh), F32)

        for bi, h in heads:
            hs = bi * nh + h
            lgf, lgb = head_consts(h)
            mask = (jnp.where(diff >= 0, jnp.exp(chunk_wide(lgf) * jnp.maximum(diff, 0.0)), 0.0)
                    + jnp.where(diff <= 0, jnp.exp(chunk_wide(lgb) * jnp.maximum(-diff, 0.0)), 0.0))
            xi_f = jnp.exp(lgf * (ri + 1.0))
            xi_b = jnp.exp(lgb * (float(c) - ri))
            zeta_f = jnp.exp(lgf * (float(c - 1) - ri))
            cdec_f = jnp.exp(lgf * float(c))
            gn = gn_ref[:, h * dh:(h + 1) * dh]
            for cc in range(cpb):
                r0 = cc * c
                n = j * cpb + cc
                qb = load(q_ref, bi, r0, h)
                kb = load(k_ref, bi, r0, h)
                vv = load(v_ref, bi, r0, h)
                gate = load(g_ref, bi, r0, h).astype(F32)
                sc = lax.dot_general(qb, kb, dn_nt, preferred_element_type=F32)
                inner = jnp.dot((sc * mask).astype(BF16), vv, preferred_element_type=F32)
                s = sf_ref[hs]
                q = qb.astype(F32)
                lhs = jnp.concatenate([q * xi_f, q * xi_b], axis=1).astype(BF16)
                rhs = jnp.concatenate([s.astype(BF16), sball_ref[n, hs]], axis=0)
                o = inner + jnp.dot(lhs, rhs, preferred_element_type=F32)
                mu = jnp.mean(o, axis=-1, keepdims=True)
                d = o - mu
                var = jnp.mean(d * d, axis=-1, keepdims=True)
                y = d * lax.rsqrt(var + EPS) * gn * _silu(gate)
                o_ref[bi, r0:r0 + c, h * dh:(h + 1) * dh] = y.astype(o_ref.dtype)
                kz = (kb.astype(F32) * zeta_f).astype(BF16)
                sf_ref[hs] = cdec_f * s + lax.dot_general(kz, vv, dn_t, preferred_element_type=F32)

        if emit_state:
            @pl.when(j == nb - 1)
            def _():
                for bi, h in heads:
                    sf_out[bi, h] = sf_ref[bi * nh + h]


def _retention(qkvg, dec_f, dec_b, gn_g, s0_f, s0_b, emit_state):
    b, seq_len, _ = qkvg.shape
    nh, dh, c = N_RET_HEADS, RET_HEAD_DIM, RET_CHUNK
    rb = min(seq_len, 1024)
    nb = seq_len // rb
    cpb = rb // c
    bb = max(1, min(4, 1024 // seq_len))
    while b % bb:
        bb //= 2
    has_init = s0_f is not None
    dec = jnp.broadcast_to(jnp.stack([dec_f, dec_b])[:, :, None, None], (2, nh, 8, 128)).astype(F32)
    kv_blk = lambda i, p, j: jnp.where(p == 0, nb - 1 - j, j)
    q_blk = lambda i, p, j: jnp.where(p == 0, 0, j)
    in_specs = [pl.BlockSpec((bb, rb, D_RET), lambda i, p, j: (i, q_blk(i, p, j), 0)),
                pl.BlockSpec((bb, rb, D_RET), lambda i, p, j: (i, kv_blk(i, p, j), 1)),
                pl.BlockSpec((bb, rb, D_RET), lambda i, p, j: (i, kv_blk(i, p, j), 2)),
                pl.BlockSpec((bb, rb, D_RET), lambda i, p, j: (i, q_blk(i, p, j), 3)),
                pl.BlockSpec((2, nh, 8, 128), lambda i, p, j: (0, 0, 0, 0)),
                pl.BlockSpec((1, D_RET), lambda i, p, j: (0, 0))]
    args = [qkvg, qkvg, qkvg, qkvg, dec, gn_g.reshape(1, -1)]
    sspec = pl.BlockSpec((bb, nh, dh, dh), lambda i, p, j: (i, 0, 0, 0))
    if has_init:
        in_specs += [sspec, sspec]
        args += [s0_f, s0_b]
    out_specs = [pl.BlockSpec((bb, rb, D_RET), lambda i, p, j: (i, q_blk(i, p, j), 0))]
    out_shape = [jax.ShapeDtypeStruct((b, seq_len, D_RET), BF16)]
    if emit_state:
        out_specs += [sspec, sspec]
        out_shape += [jax.ShapeDtypeStruct((b, nh, dh, dh), F32)] * 2
    return pl.pallas_call(
        functools.partial(_retention_kernel, has_init=has_init, emit_state=emit_state, cpb=cpb),
        grid=(b // bb, 2, nb),
        in_specs=in_specs,
        out_specs=out_specs,
        out_shape=out_shape,
        scratch_shapes=[pltpu.VMEM((bb * nh, dh, dh), F32), pltpu.VMEM((bb * nh, dh, dh), F32),
                        pltpu.VMEM((nb * cpb, bb * nh, dh, dh), BF16)],
        compiler_params=_cparams(("parallel", "arbitrary", "arbitrary")),
        name="retention",
    )(*args)


def _outproj_kernel(yhy_ref, yret_ref, gates_ref, x_ref, g1_ref, sc_ref, sh_ref, ng_ref,
                    why_ref, wret_ref, wout_ref, wr_ref, x1_ref, h2_ref, aff_ref):
    g, m, d = x_ref.shape
    rows = lambda ref: ref[...].reshape(g * m, ref.shape[2])
    a = jnp.dot(rows(yhy_ref), why_ref[...], preferred_element_type=F32)
    b = jnp.dot(rows(yret_ref), wret_ref[...], preferred_element_type=F32)
    gates = rows(gates_ref)
    merged = gates[:, :d].astype(F32) * a + gates[:, d:].astype(F32) * b
    out = jnp.dot(merged.astype(BF16), wout_ref[...], preferred_element_type=F32)
    x1 = x_ref[...] + g1_ref[...] * out.reshape(g, m, d)
    x1_ref[...] = x1
    ms = jnp.mean(x1 * x1, axis=-1, keepdims=True)
    h = x1 * lax.rsqrt(ms + EPS) * ng_ref[...]
    h = (h * (1.0 + sc_ref[...]) + sh_ref[...]).reshape(g * m, d)
    h_hi = h.astype(BF16)
    h2_ref[...] = h_hi.reshape(g, m, d)
    h_lo = (h - h_hi.astype(F32)).astype(BF16)
    t = jnp.dot(h_hi, wr_ref[...], preferred_element_type=F32)
    logits = t[:, :128] + t[:, 128:] + jnp.dot(h_lo, wr_ref[:, :128], preferred_element_type=F32)
    lane = lax.broadcasted_iota(jnp.int32, logits.shape, 1)
    logits = jnp.where(lane < N_EXPERTS, logits, -jnp.inf)
    e = jnp.exp(logits - jnp.max(logits, axis=-1, keepdims=True))
    aff = e / jnp.sum(e, axis=-1, keepdims=True)
    aff_ref[0] = aff.T[:N_EXPERTS, :]


def _out_projection(y_hy, y_ret, gates, x, g1, sc2, sh2, norm2_g, w_hy_o, w_ret_o, w_out, w_router, cond_block):
    g, s, d = x.shape
    tm = TOKEN_TILE
    m = tm // g
    tok = lambda c: pl.BlockSpec((g, m, c), lambda i: (0, i, 0))
    cond = pl.BlockSpec((g, 1, d), lambda i: (cond_block, 0, 0))
    full = lambda a: pl.BlockSpec(a.shape, lambda i: (0, 0))
    wr = jnp.pad(w_router, ((0, 0), (0, 128 - N_EXPERTS)))
    wr_hi = wr.astype(BF16)
    wr = jnp.concatenate([wr_hi, (wr - wr_hi.astype(F32)).astype(BF16)], axis=1)
    return pl.pallas_call(
        _outproj_kernel,
        grid=(s // m,),
        in_specs=[tok(D_HYENA), tok(D_RET), tok(2 * d), tok(d), cond, cond, cond,
                  pl.BlockSpec((1, d), lambda i: (0, 0)),
                  full(w_hy_o), full(w_ret_o), full(w_out), full(wr)],
        out_specs=[tok(d), tok(d), pl.BlockSpec((1, N_EXPERTS, tm), lambda i: (i, 0, 0))],
        out_shape=[jax.ShapeDtypeStruct((g, s, d), F32), jax.ShapeDtypeStruct((g, s, d), BF16),
                   jax.ShapeDtypeStruct((s // m, N_EXPERTS, tm), F32)],
        compiler_params=_cparams(("parallel",)),
        name="out_projection_router",
    )(y_hy, y_ret, gates, x, g1, sc2, sh2, norm2_g.reshape(1, d), w_hy_o, w_ret_o, w_out, wr)


def _select_kernel(aff_ref, tri_ref, rank_ref, cnt_ref, start_ref, tot_ref, *, cap, idx_bits, groups):
    nt, ne, tm = aff_ref.shape
    a = aff_ref[...]

    def count(m):
        return jnp.sum(jnp.sum(m, axis=0, keepdims=True), axis=2, keepdims=True)

    def thr_step(s, thr):
        cand = thr | (1 << (30 - s))
        cnt = count(jnp.where(a >= pltpu.bitcast(cand, F32), 1.0, 0.0))
        return jnp.where(cnt >= float(cap), cand, thr)

    thr = pltpu.bitcast(lax.fori_loop(0, 31, thr_step, jnp.zeros((1, ne, 1), jnp.int32)), F32)
    gt = a > thr
    eq = a == thr
    need = float(cap) - count(jnp.where(gt, 1.0, 0.0))
    m = tm // groups
    tile = lax.broadcasted_iota(jnp.int32, (nt, 1, tm), 0)
    lane = lax.broadcasted_iota(jnp.int32, (nt, 1, tm), 2)
    idx = (lane // m) * (nt * m) + tile * m + lane % m

    def idx_step(s, lim):
        cand = lim | (1 << (idx_bits - 1 - s))
        cnt = count(jnp.where(eq, jnp.where(idx < cand, 1.0, 0.0), 0.0))
        return jnp.where(cnt < need, cand, lim)

    lim = lax.fori_loop(0, idx_bits, idx_step, jnp.zeros((1, ne, 1), jnp.int32))
    sel = jnp.where(gt, 1.0, jnp.where(eq, jnp.where(idx <= lim, 1.0, 0.0), 0.0))
    sel2 = sel.reshape(nt * ne, tm).astype(BF16)
    prefix = jnp.dot(sel2, tri_ref[...], preferred_element_type=F32)
    rank = jnp.where(sel2 > 0, prefix, -1.0).astype(jnp.int32)
    rank_ref[...] = rank.reshape(nt, ne, tm)
    ones = jnp.ones((tm, 128), BF16)
    cnt = jnp.dot(sel2, ones, preferred_element_type=F32).astype(jnp.int32).reshape(nt, ne, 128)
    cnt_ref[...] = cnt
    acc = jnp.zeros((ne, 128), jnp.int32)
    for t in range(nt):
        start_ref[t] = acc
        acc = acc + ((cnt[t] + (ROW_ALIGN - 1)) & (-ROW_ALIGN))
    tot_ref[...] = acc


def _select(aff, cap, groups):
    nt, ne, tm = aff.shape
    idx_bits = max(1, int(math.ceil(math.log2(nt * tm))))
    r = lax.broadcasted_iota(jnp.int32, (tm, tm), 0)
    c = lax.broadcasted_iota(jnp.int32, (tm, tm), 1)
    tri = (r < c).astype(BF16)
    rank, cnt, start, tot = pl.pallas_call(
        functools.partial(_select_kernel, cap=cap, idx_bits=idx_bits, groups=groups),
        out_shape=[jax.ShapeDtypeStruct((nt, ne, tm), jnp.int32),
                   jax.ShapeDtypeStruct((nt, ne, 128), jnp.int32),
                   jax.ShapeDtypeStruct((nt, ne, 128), jnp.int32),
                   jax.ShapeDtypeStruct((ne, 128), jnp.int32)],
        compiler_params=_cparams(None),
        name="expert_choice_select",
    )(aff, tri)
    return rank, cnt[:, :, 0], start[:, :, 0], tot[:, 0]


def _expert_block(cap):
    return 512 if cap >= 2048 else 256


def _used_rows(cap, nt):
    eb = _expert_block(cap)
    return -(-(cap + ROW_ALIGN * nt) // eb) * eb


def _list_rows(cap, nt):
    max_rounds = -(-TOKEN_TILE // SLOT_ROWS)
    return _used_rows(cap, nt) + max(_expert_block(cap), max_rounds * SLOT_ROWS)


def _num_rounds(cnt_sm, i):
    m = cnt_sm[i, 0]
    for e in range(1, N_EXPERTS):
        m = jnp.maximum(m, cnt_sm[i, e])
    return jnp.maximum((m + SLOT_ROWS - 1) // SLOT_ROWS, 1)


def _gather_kernel(start_sm, cnt_sm, tot_sm, h_ref, rank_ref, xs_hbm, stage, zbuf, sem):
    i = pl.program_id(0)
    ne = N_EXPERTS
    slot = i % 2
    rank = rank_ref[0]
    sub = lax.broadcasted_iota(jnp.int32, (SLOT_ROWS, rank.shape[1]), 0)

    def copy(s, e, off):
        return pltpu.make_async_copy(stage.at[s, pl.ds(e * SLOT_ROWS, SLOT_ROWS)],
                                     xs_hbm.at[e, pl.ds(off, SLOT_ROWS)], sem.at[e])

    def fill(r):
        h = h_ref[...].reshape(rank.shape[1], h_ref.shape[2])
        for e in range(ne):
            onehot = jnp.where(rank[e:e + 1, :] == sub + r * SLOT_ROWS, 1.0, 0.0).astype(BF16)
            stage[slot, e * SLOT_ROWS:(e + 1) * SLOT_ROWS, :] = jnp.dot(
                onehot, h, preferred_element_type=F32).astype(BF16)

    def start_all(r):
        for e in range(ne):
            copy(slot, e, pl.multiple_of(start_sm[i, e] + r * SLOT_ROWS, ROW_ALIGN)).start()

    def wait_all(s):
        for e in range(ne):
            copy(s, e, 0).wait()

    fill(0)

    @pl.when(i > 0)
    def _():
        wait_all(1 - slot)

    start_all(0)

    def extra_round(r, carry):
        wait_all(slot)
        fill(r)
        start_all(r)
        return carry

    lax.fori_loop(1, _num_rounds(cnt_sm, i), extra_round, 0)

    @pl.when(i == pl.num_programs(0) - 1)
    def _zero_tail():
        wait_all(slot)
        zbuf[...] = jnp.zeros(zbuf.shape, zbuf.dtype)

        def zcopy(e):
            off = pl.multiple_of(tot_sm[e], ROW_ALIGN)
            return pltpu.make_async_copy(zbuf, xs_hbm.at[e, pl.ds(off, zbuf.shape[0])], sem.at[e])

        for e in range(ne):
            zcopy(e).start()
        for e in range(ne):
            zcopy(e).wait()


def _gather(h2, rank, start, cnt, tot, cap):
    g, _, d = h2.shape
    nt, ne, tm = rank.shape
    rl = _list_rows(cap, nt)
    return pl.pallas_call(
        _gather_kernel,
        grid_spec=pltpu.PrefetchScalarGridSpec(
            num_scalar_prefetch=3,
            grid=(nt,),
            in_specs=[pl.BlockSpec((g, tm // g, d), lambda i, *_: (0, i, 0)),
                      pl.BlockSpec((1, ne, tm), lambda i, *_: (i, 0, 0))],
            out_specs=pl.BlockSpec(memory_space=pl.ANY),
            scratch_shapes=[pltpu.VMEM((2, ne * SLOT_ROWS, d), BF16),
                            pltpu.VMEM((_expert_block(cap), d), BF16),
                            pltpu.SemaphoreType.DMA((ne,))]),
        out_shape=jax.ShapeDtypeStruct((ne, rl, d), BF16),
        compiler_params=_cparams(("arbitrary",)),
        name="expert_gather",
    )(start, cnt, tot, h2, rank)


def _expert_kernel(*refs, nblocks):
    ns = len(nblocks)
    tot_sms, xs_refs = refs[:ns], refs[ns:2 * ns]
    wg_ref, wu_ref, wd_ref = refs[2 * ns:2 * ns + 3]
    ye_refs = refs[2 * ns + 3:3 * ns + 3]
    wg_bf, wu_bf, wd_bf = refs[3 * ns + 3:]
    e = pl.program_id(0)
    j = pl.program_id(1)

    @pl.when(j == 0)
    def _():
        wg_bf[...] = wg_ref[0].astype(BF16)
        wu_bf[...] = wu_ref[0].astype(BF16)
        wd_bf[...] = wd_ref[0].astype(BF16)

    base = 0
    for s in range(ns):
        jj = j - base
        live = jnp.logical_and(jnp.logical_and(jj >= 0, jj < nblocks[s]), jj * xs_refs[s].shape[1] < tot_sms[s][e])

        @pl.when(live)
        def _(xs_ref=xs_refs[s], ye_ref=ye_refs[s]):
            x = xs_ref[0]
            g = jnp.dot(x, wg_bf[...], preferred_element_type=F32)
            u = jnp.dot(x, wu_bf[...], preferred_element_type=F32)
            hid = (_silu(g) * u).astype(BF16)
            ye_ref[0] = jnp.dot(hid, wd_bf[...], preferred_element_type=F32).astype(ye_ref.dtype)

        base += nblocks[s]


def _experts(sets, w_gate, w_up, w_down):
    ns = len(sets)
    ne, _, d = sets[0][0].shape
    f = w_gate.shape[2]
    nblocks = tuple(used // eb for _, _, used, eb in sets)

    def block_spec(s):
        eb = sets[s][3]
        base = sum(nblocks[:s])

        def index(e, j, *tots):
            return (e, jnp.clip(j - base, 0, (tots[s][e] - 1) // eb), 0)

        return pl.BlockSpec((1, eb, d), index)

    wspec = lambda shape: pl.BlockSpec(shape, lambda e, j, *tots: (e, 0, 0))
    return pl.pallas_call(
        functools.partial(_expert_kernel, nblocks=nblocks),
        grid_spec=pltpu.PrefetchScalarGridSpec(
            num_scalar_prefetch=ns,
            grid=(ne, sum(nblocks)),
            in_specs=[block_spec(s) for s in range(ns)] + [wspec((1, d, f)), wspec((1, d, f)), wspec((1, f, d))],
            out_specs=[block_spec(s) for s in range(ns)],
            scratch_shapes=[pltpu.VMEM((d, f), BF16), pltpu.VMEM((d, f), BF16), pltpu.VMEM((f, d), BF16)]),
        out_shape=[jax.ShapeDtypeStruct((ne, used, d), BF16) for _, _, used, _ in sets],
        compiler_params=_cparams(("arbitrary", "arbitrary")),
        name="expert_ffn",
    )(*[t for _, t, _, _ in sets], *[x for x, _, _, _ in sets], w_gate, w_up, w_down)


def _combine_kernel(start_sm, cnt_sm, tot_sm, x1_ref, rank_ref, aff_ref, g2_ref, nf_ref, ye_hbm, o_ref, buf, sem, *,
                    final_norm, eb):
    i = pl.program_id(0)
    nt = pl.num_programs(0)
    ne = N_EXPERTS
    slot = i % 2
    rank = rank_ref[0]
    aff = aff_ref[0]
    sub = lax.broadcasted_iota(jnp.int32, (SLOT_ROWS, rank.shape[1]), 0)
    dn_t = (((0,), (0,)), ((), ()))

    def window(t, e, r):
        want = start_sm[t, e] + r * SLOT_ROWS
        written = ((tot_sm[e] + eb - 1) // eb) * eb
        off = jnp.minimum(want, written - SLOT_ROWS)
        return pl.multiple_of(off, ROW_ALIGN), want - off

    def copy(t, s, e, r):
        off, _ = window(t, e, r)
        return pltpu.make_async_copy(ye_hbm.at[e, pl.ds(off, SLOT_ROWS)],
                                     buf.at[s, pl.ds(e * SLOT_ROWS, SLOT_ROWS)], sem.at[s, e])

    def weighted(r):
        parts = []
        for e in range(ne):
            _, shift = window(i, e, r)
            local = rank[e:e + 1, :] - r * SLOT_ROWS
            hit = jnp.where(local >= 0, local + shift, -1) == sub
            parts.append(jnp.where(hit, aff[e:e + 1, :], 0.0).astype(BF16))
        return lax.dot_general(jnp.concatenate(parts, axis=0), buf[slot], dn_t, preferred_element_type=F32)

    @pl.when(i == 0)
    def _():
        for e in range(ne):
            copy(0, 0, e, 0).start()

    @pl.when(i + 1 < nt)
    def _prefetch_next_tile():
        for e in range(ne):
            copy(i + 1, 1 - slot, e, 0).start()

    for e in range(ne):
        copy(i, slot, e, 0).wait()
    y0 = weighted(0)

    def extra_round(r, acc):
        for e in range(ne):
            @pl.when(cnt_sm[i, e] > r * SLOT_ROWS)
            def _():
                copy(i, slot, e, r).start()
        for e in range(ne):
            @pl.when(cnt_sm[i, e] > r * SLOT_ROWS)
            def _():
                copy(i, slot, e, r).wait()
        return acc + weighted(r)

    y = lax.fori_loop(1, _num_rounds(cnt_sm, i), extra_round, y0)
    x2 = x1_ref[...] + g2_ref[...] * y.reshape(x1_ref.shape)
    if final_norm:
        ms = jnp.mean(x2 * x2, axis=-1, keepdims=True)
        x2 = x2 * lax.rsqrt(ms + EPS) * nf_ref[...]
    o_ref[...] = x2


def _combine(x1, rank, aff, start, cnt, tot, g2, norm_f_g, ye, cond_block, eb):
    g, s, d = x1.shape
    nt, ne, tm = rank.shape
    tok = pl.BlockSpec((g, tm // g, d), lambda i, *_: (0, i, 0))
    final_norm = norm_f_g is not None
    if not final_norm:
        norm_f_g = jnp.ones((d,), F32)
    return pl.pallas_call(
        functools.partial(_combine_kernel, final_norm=final_norm, eb=eb),
        grid_spec=pltpu.PrefetchScalarGridSpec(
            num_scalar_prefetch=3,
            grid=(nt,),
            in_specs=[tok,
                      pl.BlockSpec((1, ne, tm), lambda i, *_: (i, 0, 0)),
                      pl.BlockSpec((1, ne, tm), lambda i, *_: (i, 0, 0)),
                      pl.BlockSpec((g, 1, d), lambda i, *_: (cond_block, 0, 0)),
                      pl.BlockSpec((1, d), lambda i, *_: (0, 0)),
                      pl.BlockSpec(memory_space=pl.ANY)],
            out_specs=tok,
            scratch_shapes=[pltpu.VMEM((2, ne * SLOT_ROWS, d), BF16),
                            pltpu.SemaphoreType.DMA((2, ne))]),
        out_shape=jax.ShapeDtypeStruct((g, s, d), F32),
        compiler_params=_cparams(("arbitrary",)),
        name="expert_combine",
    )(start, cnt, tot, x1, rank, aff, g2, norm_f_g.reshape(1, d), ye)


def _layer_front(x, batch, seq_len, mods, cond_of_tile, moe_groups, cond_block, p, s0_f, s0_b, rope, emit_state):
    t, d = x.shape
    sh1, sc1, g1, sh2, sc2, g2 = mods
    u, x2c, qkvg, gates = _in_projection(x, p["norm1_g"], sc1, sh1, p["w_in"], p["hy_conv_w"], p["hy_conv_b"],
                                         cond_of_tile, seq_len, rope)
    seq = lambda a: a.reshape(batch, seq_len, a.shape[-1])
    y_hy = _hyena_long_conv(seq(u), seq(x2c), p).reshape(t, -1)
    ret = _retention(qkvg.reshape(batch, seq_len, -1), p["ret_decay_fwd"], p["ret_decay_bwd"], p["ret_gn_g"],
                     s0_f, s0_b, emit_state)
    y_ret = ret[0].reshape(t, -1)
    view = lambda a: a.reshape(moe_groups, t // moe_groups, a.shape[-1])
    x1, h2, aff = _out_projection(view(y_hy), view(y_ret), view(gates), view(x), g1, sc2, sh2, p["norm2_g"],
                                  p["w_hy_o"], p["w_ret_o"], p["w_out"], p["w_router"], cond_block)
    cap = (EC_CAPACITY_FACTOR * t) // N_EXPERTS
    eb = _expert_block(cap)
    rank, cnt, start, tot = _select(aff, cap, moe_groups)
    xs = _gather(h2, rank, start, cnt, tot, cap)
    routed = (xs, tot, _used_rows(cap, rank.shape[0]), eb)
    back = dict(x1=x1, rank=rank, aff=aff, start=start, cnt=cnt, tot=tot, g2=g2, cond_block=cond_block, eb=eb)
    return routed, back, ret[1:]


def _layer_back(back, ye, final_g):
    out = _combine(back["x1"], back["rank"], back["aff"], back["start"], back["cnt"], back["tot"], back["g2"],
                   final_g, ye, back["cond_block"], back["eb"])
    return out.reshape(-1, out.shape[-1])


def kernel(x_prompt, x_sample, state_ret_fwd, state_ret_bwd, c, c_ctx, w_mod, b_mod, norm1_g, w_in, hy_conv_w, hy_conv_b, hy_f_w1, hy_f_b1, hy_f_freq, hy_f_w2, hy_f_b2, hy_f_w3, hy_decay, hy_bias, ret_decay_fwd, ret_decay_bwd, ret_gn_g, w_hy_o, w_ret_o, w_out, norm2_g, w_router, w_e_gate, w_e_up, w_e_down, norm_f_g):
    b, s, d = x_prompt.shape
    bd, sd, _ = x_sample.shape
    depth = w_mod.shape[0]
    assert (b * s) % TOKEN_TILE == 0 and sd % TOKEN_TILE == 0 and b % 2 == 0 and bd % 2 == 0
    ncond = -(-(bd + 1) // 8) * 8
    lat_groups = bd if (TOKEN_TILE % (16 * bd) == 0 and sd % (TOKEN_TILE // bd) == 0) else 1
    assert lat_groups == bd or bd == 1
    conds = jnp.concatenate([c, c_ctx[None], jnp.zeros((ncond - bd - 1, d), F32)], axis=0)
    ctx_cond = lambda i: bd
    lat_cond = lambda i: i // (sd // TOKEN_TILE)
    xp = x_prompt.reshape(b * s, d)
    xs = x_sample.reshape(bd * sd, d)
    new_f, new_b = [], []
    for l in range(depth):
        mod = _modulation(conds, w_mod[l], b_mod[l])
        mods = [mod[:, k * d:(k + 1) * d].reshape(ncond, 1, d) for k in range(6)]
        p = {"norm1_g": norm1_g[l], "w_in": w_in[l].astype(BF16), "hy_conv_w": hy_conv_w[l],
             "hy_conv_b": hy_conv_b[l], "hy_f_w1": hy_f_w1[l], "hy_f_b1": hy_f_b1[l], "hy_f_freq": hy_f_freq[l],
             "hy_f_w2": hy_f_w2[l], "hy_f_b2": hy_f_b2[l], "hy_f_w3": hy_f_w3[l], "hy_decay": hy_decay[l],
             "hy_bias": hy_bias[l], "ret_decay_fwd": ret_decay_fwd[l], "ret_decay_bwd": ret_decay_bwd[l],
             "ret_gn_g": ret_gn_g[l], "w_hy_o": w_hy_o[l].astype(BF16), "w_ret_o": w_ret_o[l].astype(BF16),
             "w_out": w_out[l].astype(BF16), "norm2_g": norm2_g[l], "w_router": w_router[l],
             "w_e_gate": w_e_gate[l], "w_e_up": w_e_up[l], "w_e_down": w_e_down[l]}
        final_g = norm_f_g if l == depth - 1 else None
        routed_p, back_p, (s_f, s_b) = _layer_front(xp, b, s, mods, ctx_cond, 1, bd, p, None, None, False, True)
        new_f.append(s_f)
        new_b.append(s_b)
        routed_s, back_s, _ = _layer_front(xs, bd, sd, mods, lat_cond, lat_groups, 0, p, state_ret_fwd[:, l],
                                           state_ret_bwd[:, l], True, False)
        ye_p, ye_s = _experts([routed_p, routed_s], p["w_e_gate"], p["w_e_up"], p["w_e_down"])
        xp = _layer_back(back_p, ye_p, final_g)
        xs = _layer_back(back_s, ye_s, final_g)
    y_prompt = xp.reshape(b, s, d)
    y_sample = xs.reshape(bd, sd, d)
    return (y_prompt, y_sample, jnp.stack(new_f, axis=1), jnp.stack(new_b, axis=1))
```

```python
import functools
import math

import numpy as np
import jax
import jax.numpy as jnp
from jax import lax
from jax.experimental import pallas as pl
from jax.experimental.pallas import tpu as pltpu

F32 = jnp.float32
BF16 = jnp.bfloat16
HIGHEST = lax.Precision.HIGHEST

EPS = 1e-6
D_HYENA = 512
D_RET = 512
N_RET_HEADS = 4
RET_HEAD_DIM = 128
RET_CHUNK = 256
GRID_W = 64
FILTER_EMB = 33
ROPE_BASE = 10000.0
N_EXPERTS = 16
EC_CAPACITY_FACTOR = 2

TOKEN_TILE = 512
SLOT_ROWS = 96
ROW_ALIGN = 16
HALO_ROWS = 16
FFT_MINOR = 64
FFT_GROUP = 16
VMEM_LIMIT = 56 * 1024 * 1024


def _cparams(sem):
    return pltpu.CompilerParams(dimension_semantics=sem, vmem_limit_bytes=VMEM_LIMIT)


def _silu(x):
    return x * jax.nn.sigmoid(x)


def _mod_kernel(c_ref, w_ref, b_ref, o_ref):
    s = _silu(c_ref[...])
    o_ref[...] = jnp.dot(s, w_ref[...], preferred_element_type=F32, precision=HIGHEST) + b_ref[...]


def _modulation(conds, w_mod, b_mod):
    nc, d = conds.shape
    return pl.pallas_call(
        _mod_kernel,
        grid=(6,),
        in_specs=[pl.BlockSpec((nc, d), lambda j: (0, 0)),
                  pl.BlockSpec((d, d), lambda j: (0, j)),
                  pl.BlockSpec((1, d), lambda j: (0, j))],
        out_specs=pl.BlockSpec((nc, d), lambda j: (0, j)),
        out_shape=jax.ShapeDtypeStruct((nc, 6 * d), F32),
        compiler_params=_cparams(("arbitrary",)),
        name="modulation",
    )(conds, w_mod, b_mod.reshape(1, -1))


def _inproj_kernel(*refs, splits, rope, seq_len, halo):
    refs = list(refs)
    x_ref = refs.pop(0)
    if halo:
        xp_ref, xn_ref = refs.pop(0), refs.pop(0)
    g_ref, sc_ref, sh_ref, w_ref, cw_ref, cb_ref = refs[:6]
    del refs[:6]
    if rope:
        cos_ref, sin_ref = refs.pop(0), refs.pop(0)
    u_ref, x2c_ref, qkvg_ref, gates_ref, cv_sc, cx_sc = refs

    def normed(ref):
        x = ref[...]
        ms = jnp.mean(x * x, axis=-1, keepdims=True)
        h = x * lax.rsqrt(ms + EPS) * g_ref[...]
        return (h * (1.0 + sc_ref[0]) + sh_ref[0]).astype(BF16)

    hb = normed(x_ref)
    tm = hb.shape[0]
    nz, nq, ng = splits
    cw = 512
    dh = RET_HEAD_DIM
    hr = HALO_ROWS
    if halo:
        t0 = pl.program_id(0) * tm
        hp = jnp.where(t0 % seq_len != 0, normed(xp_ref), jnp.zeros((hr, hb.shape[1]), BF16))
        hn = jnp.where((t0 + tm) % seq_len != 0, normed(xn_ref), jnp.zeros((hr, hb.shape[1]), BF16))
        hext = jnp.concatenate([hp, hb, hn], axis=0)

    piece = tm if halo else seq_len
    pitch = piece + hr
    if not halo:
        for s in range(tm // piece + 1):
            cx_sc[s * pitch:s * pitch + hr, :] = jnp.zeros((hr, cw), F32)

    def conv(c0):
        taps = cw_ref[:, c0:c0 + cw]
        bias = cb_ref[:, c0:c0 + cw]
        if halo:
            cx_sc[...] = jnp.dot(hext, w_ref[:, c0:c0 + cw], preferred_element_type=F32)
        else:
            acc = jnp.dot(hb, w_ref[:, c0:c0 + cw], preferred_element_type=F32)
            for s in range(tm // piece):
                cx_sc[hr + s * pitch:hr + s * pitch + piece, :] = acc[s * piece:(s + 1) * piece]
        outs = []
        for s in range(tm // piece):
            r0 = hr + s * pitch
            outs.append(bias + cx_sc[r0 - 1:r0 - 1 + piece, :] * taps[0:1] + cx_sc[r0:r0 + piece, :] * taps[1:2]
                        + cx_sc[r0 + 1:r0 + 1 + piece, :] * taps[2:3])
        return outs[0] if len(outs) == 1 else jnp.concatenate(outs, axis=0)

    cv_sc[...] = conv(0)
    u_ref[...] = (cv_sc[...] * conv(D_HYENA)).astype(BF16)
    x2c_ref[...] = conv(2 * D_HYENA).astype(BF16)

    for c0 in range(nz, nz + nq + ng, cw):
        acc = jnp.dot(hb, w_ref[:, c0:c0 + cw], preferred_element_type=F32)
        if c0 < nz + nq:
            part = (c0 - nz) // D_RET
            if part == 1:
                acc = acc * (dh ** -0.5)
            if rope and part < 2:
                lane = lax.broadcasted_iota(jnp.int32, (acc.shape[0], dh), 1)
                swap_hi = (lane % (dh // 2)) < (dh // 4)
                cs, sn = cos_ref[...], sin_ref[...]
                heads = []
                for hh in range(cw // dh):
                    xh = acc[:, hh * dh:(hh + 1) * dh]
                    rot = jnp.where(swap_hi, pltpu.roll(xh, dh - dh // 4, axis=1), pltpu.roll(xh, dh // 4, axis=1))
                    heads.append(xh * cs + rot * sn)
                acc = jnp.concatenate(heads, axis=1)
            qkvg_ref[:, c0 - nz:c0 - nz + cw] = acc.astype(BF16)
        else:
            gates_ref[:, c0 - nz - nq:c0 - nz - nq + cw] = jax.nn.sigmoid(acc).astype(BF16)


def _in_projection(x, norm_g, sc, sh, w_in_bf, conv_w, conv_b, cond_of_tile, seq_len, rope):
    t, d = x.shape
    splits = (3 * D_HYENA, 4 * D_RET, 2 * d)
    assert D_RET == 512 and D_HYENA == 512
    tm = TOKEN_TILE
    assert seq_len % tm == 0 or tm % seq_len == 0
    halo = seq_len > tm
    row = lambda i: (i, 0)
    cond = lambda i: (cond_of_tile(i), 0, 0)
    full = lambda a: pl.BlockSpec(a.shape, lambda i: (0, 0))
    args, specs = [x], [pl.BlockSpec((tm, d), row)]
    if halo:
        hb_per_tile = tm // HALO_ROWS
        last = t // HALO_ROWS - 1
        args += [x, x]
        specs += [pl.BlockSpec((HALO_ROWS, d), lambda i: (jnp.maximum(i * hb_per_tile - 1, 0), 0)),
                  pl.BlockSpec((HALO_ROWS, d), lambda i: (jnp.minimum((i + 1) * hb_per_tile, last), 0))]
    cbias = conv_b.reshape(1, -1)
    args += [norm_g.reshape(1, d), sc, sh, w_in_bf, conv_w, cbias]
    specs += [pl.BlockSpec((1, d), lambda i: (0, 0)), pl.BlockSpec((1, 1, d), cond), pl.BlockSpec((1, 1, d), cond),
              full(w_in_bf), full(conv_w), full(cbias)]
    if rope:
        tiles_per_seq = seq_len // tm
        args += list(_rope_tables(seq_len))
        specs += [pl.BlockSpec((tm, RET_HEAD_DIM), lambda i: (i % tiles_per_seq, 0))] * 2
    return pl.pallas_call(
        functools.partial(_inproj_kernel, splits=splits, rope=rope, seq_len=seq_len, halo=halo),
        grid=(t // tm,),
        in_specs=specs,
        out_specs=[pl.BlockSpec((tm, D_HYENA), row),
                   pl.BlockSpec((tm, D_HYENA), row),
                   pl.BlockSpec((tm, splits[1]), row),
                   pl.BlockSpec((tm, splits[2]), row)],
        out_shape=[jax.ShapeDtypeStruct((t, D_HYENA), BF16),
                   jax.ShapeDtypeStruct((t, D_HYENA), BF16),
                   jax.ShapeDtypeStruct((t, splits[1]), BF16),
                   jax.ShapeDtypeStruct((t, splits[2]), BF16)],
        scratch_shapes=[pltpu.VMEM((tm, 512), F32),
                        pltpu.VMEM((tm + 2 * HALO_ROWS if halo else (tm // seq_len) * (seq_len + HALO_ROWS) + HALO_ROWS,
                                    512), F32)],
        compiler_params=_cparams(("parallel",)),
        name="in_projection",
    )(*args)


def _filter_features(seq_len):
    t = np.linspace(0.0, 1.0, seq_len, dtype=np.float32)[:, None]
    bands = (FILTER_EMB - 1) // 2
    w = (np.float32(2.0 * math.pi) * np.arange(seq_len, dtype=np.float32)) / np.float32(seq_len)
    f = np.linspace(1e-4, bands - 1, bands, dtype=np.float32)
    ang = (w[:, None] * f[None, :]).astype(np.float64)
    z = np.concatenate([t, np.cos(ang), -np.sin(ang)], axis=-1).astype(np.float32)
    return np.pad(z, ((0, 0), (0, 128 - FILTER_EMB)))


def _filter_kernel(z_ref, w1_ref, b1_ref, fr_ref, w2_ref, b2_ref, w3_ref, dec_ref, o_ref):
    z = z_ref[...]
    fr = fr_ref[...]
    dot = functools.partial(jnp.dot, preferred_element_type=F32, precision=HIGHEST)
    h = jnp.sin(fr * (dot(z, w1_ref[...]) + b1_ref[...]))
    h = jnp.sin(fr * (dot(h, w2_ref[...]) + b2_ref[...]))
    split = lambda a: (a.astype(BF16), (a - a.astype(BF16).astype(F32)).astype(BF16))
    (h_hi, h_lo), (w_hi, w_lo) = split(h), split(w3_ref[...])
    bdot = functools.partial(jnp.dot, preferred_element_type=F32)
    h = (bdot(h_hi, w_hi) + bdot(h_hi, w_lo) + bdot(h_lo, w_hi)) * jnp.exp(-z[:, 0:1] * jnp.abs(dec_ref[...]))
    rows = h.shape[0]
    grow = pl.program_id(0) * rows + lax.broadcasted_iota(jnp.int32, (rows, 1), 0)
    o_ref[0] = h[:, :D_HYENA].astype(o_ref.dtype)
    o_ref[1] = jnp.zeros((rows, D_HYENA), o_ref.dtype)
    o_ref[2] = jnp.where(grow == 0, 0.0, h[:, D_HYENA:]).astype(o_ref.dtype)
    o_ref[3] = jnp.zeros((rows, D_HYENA), o_ref.dtype)


def _hyena_filters(seq_len, p, out_dtype):
    z = jnp.asarray(_filter_features(seq_len))
    w1 = jnp.pad(p["hy_f_w1"], ((0, 128 - FILTER_EMB), (0, 0)))
    rows = min(seq_len, 512)
    full = lambda a: pl.BlockSpec(a.shape, lambda i: (0,) * a.ndim)
    ops = [w1, p["hy_f_b1"].reshape(1, -1), p["hy_f_freq"].reshape(1, -1), p["hy_f_w2"],
           p["hy_f_b2"].reshape(1, -1), p["hy_f_w3"], p["hy_decay"].reshape(1, -1)]
    return pl.pallas_call(
        _filter_kernel,
        grid=(seq_len // rows,),
        in_specs=[pl.BlockSpec((rows, 128), lambda i: (i, 0))] + [full(a) for a in ops],
        out_specs=pl.BlockSpec((4, rows, D_HYENA), lambda i: (0, i, 0)),
        out_shape=jax.ShapeDtypeStruct((4, seq_len, D_HYENA), out_dtype),
        compiler_params=_cparams(("arbitrary",)),
        name="hyena_filter",
    )(z, *ops)


def _stacked_dft(n_out, n_in, modulus, scale=1.0):
    k = np.arange(n_out, dtype=np.int64)[:, None]
    n = np.arange(n_in, dtype=np.int64)[None, :]
    th = ((k * n) % modulus) * (2.0 * math.pi / modulus)
    c, s = np.cos(th), np.sin(th)
    fwd = np.block([[c, s], [-s, c]]).astype(np.float32)
    inv = (np.block([[c.T, -s.T], [s.T, c.T]]) * scale).astype(np.float32)
    return fwd, inv


def _second_level_tables(n_total, n1, n2):
    k1 = np.arange(n1, dtype=np.int64)[:, None, None]
    k2 = np.arange(n2, dtype=np.int64)[None, :, None]
    m = np.arange(n2, dtype=np.int64)[None, None, :]
    th = ((m * (k1 + n1 * k2)) % n_total) * (2.0 * math.pi / n_total)
    c, s = np.cos(th), np.sin(th)
    g = np.concatenate([np.concatenate([c, s], axis=2), np.concatenate([-s, c], axis=2)], axis=1)
    return g.astype(np.float32), np.swapaxes(g, 1, 2).astype(np.float32)


def _skewed(n):
    return n + 1


def _fft1_kernel(h_ref, f_ref, a_ref, u_sc, t_sc):
    n2 = FFT_MINOR
    _, seq_len, cb = h_ref.shape
    h1 = seq_len // n2
    nk = a_ref.shape[1]
    up, tp = _skewed(n2), _skewed(nk)
    for b in range(2):
        for q in range(h1):
            u_sc[b, q * up:q * up + n2, :] = h_ref[b, q * n2:(q + 1) * n2, :].astype(F32)

    def fwd(gi, carry):
        m0 = gi * FFT_GROUP
        cols = []
        for d in range(FFT_GROUP):
            xr = u_sc[0, pl.ds(m0 + d, h1, stride=up), :]
            xi = u_sc[1, pl.ds(m0 + d, h1, stride=up), :]
            cols.append(jnp.concatenate([xr, xi], axis=0))
        x = jnp.concatenate(cols, axis=1).astype(BF16)
        res = jnp.dot(f_ref[...], x, preferred_element_type=F32)
        for d in range(FFT_GROUP):
            t_sc[pl.ds((m0 + d) * tp, nk), :] = res[:, d * cb:(d + 1) * cb]
        return carry

    lax.fori_loop(0, n2 // FFT_GROUP, fwd, 0, unroll=2)

    def transpose(k, carry):
        a_ref[0, k] = t_sc[pl.ds(k, n2, stride=tp), :].astype(a_ref.dtype)
        return carry

    lax.fori_loop(0, nk, transpose, 0, unroll=8)


def _fft_first_level(h, f1):
    b, seq_len, c = h.shape
    cb = 128
    return pl.pallas_call(
        _fft1_kernel,
        grid=(b // 2, c // cb),
        in_specs=[pl.BlockSpec((2, seq_len, cb), lambda i, j: (i, 0, j)),
                  pl.BlockSpec(f1.shape, lambda i, j: (0, 0))],
        out_specs=pl.BlockSpec((1, f1.shape[0], FFT_MINOR, cb), lambda i, j: (i, 0, 0, j)),
        out_shape=jax.ShapeDtypeStruct((b // 2, f1.shape[0], FFT_MINOR, c), BF16),
        scratch_shapes=[pltpu.VMEM((2, (seq_len // FFT_MINOR) * _skewed(FFT_MINOR), cb), F32),
                        pltpu.VMEM((FFT_MINOR * _skewed(f1.shape[0]), cb), F32)],
        compiler_params=_cparams(("parallel", "parallel")),
        name="hyena_dft_level1",
    )(h, f1)


def _fft_s2f_kernel(a_ref, g_ref, kr_ref, ki_ref):
    _, _, kb, n2, c = a_ref.shape
    for kk in range(kb):
        g = g_ref[kk]
        hf = jnp.dot(g, a_ref[0, :, kk].reshape(2 * n2, c), preferred_element_type=F32)
        hb = jnp.dot(g, a_ref[1, :, kk].reshape(2 * n2, c), preferred_element_type=F32)
        kr_ref[kk] = hf[:n2] + hb[:n2]
        ki_ref[kk] = hf[n2:] - hb[n2:]


def _fft_filter_second_level(a, g, kb=16):
    _, _, n1, n2, c = a.shape
    spec = pl.BlockSpec((kb, n2, c), lambda i: (i, 0, 0))
    return pl.pallas_call(
        _fft_s2f_kernel,
        grid=(n1 // kb,),
        in_specs=[pl.BlockSpec((2, 2, kb, n2, c), lambda i: (0, 0, i, 0, 0)),
                  pl.BlockSpec((kb, 2 * n2, 2 * n2), lambda i: (i, 0, 0))],
        out_specs=[spec, spec],
        out_shape=[jax.ShapeDtypeStruct((n1, n2, c), F32)] * 2,
        compiler_params=_cparams(("parallel",)),
        name="hyena_filter_spectrum",
    )(a, g)


def _fft_s2_kernel(a_ref, g_ref, gt_ref, kr_ref, ki_ref, b_ref, r_sc):
    npairs, _, kb, n2, cb = a_ref.shape
    nq = cb // 128
    rp = _skewed(2 * n2)
    for kk in range(kb):
        kr, ki = kr_ref[kk], ki_ref[kk]
        for p in range(npairs):
            x = jnp.dot(g_ref[kk], a_ref[p, :, kk].reshape(2 * n2, cb), preferred_element_type=F32)
            xr, xi = x[:n2], x[n2:]
            y = jnp.concatenate([xr * kr - xi * ki, xr * ki + xi * kr], axis=0).astype(BF16)
            res = jnp.dot(gt_ref[kk], y, preferred_element_type=F32)
            for q in range(nq):
                r_sc[p * nq + q, kk * rp:kk * rp + 2 * n2, :] = res[:, q * 128:(q + 1) * 128]

    for p in range(npairs):
        def gather(row):
            return jnp.concatenate([r_sc[p * nq + q, pl.ds(row, kb, stride=rp), :] for q in range(nq)], axis=1)

        def transpose(m, carry):
            b_ref[p, 0, m] = gather(m).astype(b_ref.dtype)
            b_ref[p, 1, m] = gather(n2 + m).astype(b_ref.dtype)
            return carry

        lax.fori_loop(0, n2, transpose, 0, unroll=8)


def _fft_second_level(a, g, gt, kr, ki):
    p, _, n1, n2, c = a.shape
    kb, cb = 16, 256
    gspec = pl.BlockSpec((kb, 2 * n2, 2 * n2), lambda i, j: (i, 0, 0))
    kspec = pl.BlockSpec((kb, n2, cb), lambda i, j: (i, 0, j))
    return pl.pallas_call(
        _fft_s2_kernel,
        grid=(n1 // kb, c // cb),
        in_specs=[pl.BlockSpec((p, 2, kb, n2, cb), lambda i, j: (0, 0, i, 0, j)), gspec, gspec, kspec, kspec],
        out_specs=pl.BlockSpec((p, 2, n2, kb, cb), lambda i, j: (0, 0, 0, i, j)),
        out_shape=jax.ShapeDtypeStruct((p, 2, n2, n1, c), BF16),
        scratch_shapes=[pltpu.VMEM((p * cb // 128, kb * _skewed(2 * n2), 128), F32)],
        compiler_params=_cparams(("parallel", "parallel")),
        name="hyena_dft_level2",
    )(a, g, gt, kr, ki)


def _fft_s3_kernel(b_ref, f_ref, u_ref, x2_ref, bias_ref, o_ref, t_sc):
    _, _, n2, n1, cb = b_ref.shape
    nr = f_ref.shape[0]
    h1 = nr // 2
    tp = _skewed(nr)

    def inv(gi, carry):
        m0 = gi * FFT_GROUP
        x = jnp.concatenate([b_ref[0, :, m0 + d].reshape(2 * n1, cb) for d in range(FFT_GROUP)], axis=1)
        res = jnp.dot(f_ref[...], x, preferred_element_type=F32)
        for d in range(FFT_GROUP):
            t_sc[pl.ds((m0 + d) * tp, nr), :] = res[:, d * cb:(d + 1) * cb]
        return carry

    lax.fori_loop(0, n2 // FFT_GROUP, inv, 0, unroll=2)

    bias = bias_ref[...]
    for b in range(2):
        def finish(q, carry):
            rows = pl.ds(pl.multiple_of(q * n2, n2), n2)
            conv = t_sc[pl.ds(b * h1 + q, n2, stride=tp), :]
            u = u_ref[b, rows, :].astype(F32)
            o_ref[b, rows, :] = ((conv + u * bias) * x2_ref[b, rows, :].astype(F32)).astype(o_ref.dtype)
            return carry

        lax.fori_loop(0, h1, finish, 0, unroll=4)


def _fft_last_level(bt, f1inv, u, x2c, bias):
    p, _, n2, n1, c = bt.shape
    b, seq_len, _ = u.shape
    cb = 128
    uspec = pl.BlockSpec((2, seq_len, cb), lambda i, j: (i, 0, j))
    return pl.pallas_call(
        _fft_s3_kernel,
        grid=(p, c // cb),
        in_specs=[pl.BlockSpec((1, 2, n2, n1, cb), lambda i, j: (i, 0, 0, 0, j)),
                  pl.BlockSpec(f1inv.shape, lambda i, j: (0, 0)),
                  uspec, uspec,
                  pl.BlockSpec((1, cb), lambda i, j: (0, j))],
        out_specs=uspec,
        out_shape=jax.ShapeDtypeStruct(u.shape, BF16),
        scratch_shapes=[pltpu.VMEM((n2 * _skewed(f1inv.shape[0]), cb), F32)],
        compiler_params=_cparams(("parallel", "parallel")),
        name="hyena_dft_inverse",
    )(bt, f1inv, u, x2c, bias)


def _kf_direct_kernel(h_ref, f_ref, kr_ref, ki_ref):
    n = kr_ref.shape[0]
    dot = functools.partial(jnp.dot, preferred_element_type=F32, precision=HIGHEST)
    hf = dot(f_ref[...], h_ref[0])
    hb = dot(f_ref[...], h_ref[2])
    kr_ref[...] = hf[:n] + hb[:n]
    ki_ref[...] = hf[n:] - hb[n:]


def _hyena_direct_kernel(u_ref, x2_ref, f_ref, fi_ref, kr_ref, ki_ref, bias_ref, o_ref):
    _, two, seq_len, cb = u_ref.shape
    n = kr_ref.shape[0]
    ub = u_ref[0].reshape(two * seq_len, cb)
    x = jnp.dot(f_ref[...], ub, preferred_element_type=F32)
    xr, xi = x[:n], x[n:]
    kr, ki = kr_ref[...], ki_ref[...]
    y = jnp.concatenate([xr * kr - xi * ki, xr * ki + xi * kr], axis=0).astype(BF16)
    conv = jnp.dot(fi_ref[...], y, preferred_element_type=F32)
    x2 = x2_ref[0].reshape(two * seq_len, cb).astype(F32)
    o = (conv + ub.astype(F32) * bias_ref[...]) * x2
    o_ref[0] = o.reshape(two, seq_len, cb).astype(o_ref.dtype)


def _hyena_long_conv(u, x2c, p):
    b, seq_len, _ = u.shape
    c = D_HYENA
    n = 2 * seq_len
    bias = p["hy_bias"].reshape(1, c)
    hh = _hyena_filters(seq_len, p, F32)
    if seq_len <= 512:
        fwd, inv = _stacked_dft(n, seq_len, n, scale=1.0 / n)
        kr, ki = pl.pallas_call(
            _kf_direct_kernel,
            out_shape=[jax.ShapeDtypeStruct((n, c), F32)] * 2,
            compiler_params=_cparams(None),
            name="hyena_filter_spectrum_direct",
        )(hh, fwd[:, :seq_len])
        cb = 512
        pair = lambda a: a.reshape(b // 2, 2, seq_len, c)
        uspec = pl.BlockSpec((1, 2, seq_len, cb), lambda i, j: (i, 0, 0, j))
        kspec = pl.BlockSpec((n, cb), lambda i, j: (0, j))
        out = pl.pallas_call(
            _hyena_direct_kernel,
            grid=(b // 2, c // cb),
            in_specs=[uspec, uspec,
                      pl.BlockSpec(fwd.shape, lambda i, j: (0, 0)),
                      pl.BlockSpec(inv.shape, lambda i, j: (0, 0)),
                      kspec, kspec,
                      pl.BlockSpec((1, cb), lambda i, j: (0, j))],
            out_specs=uspec,
            out_shape=jax.ShapeDtypeStruct((b // 2, 2, seq_len, c), BF16),
            compiler_params=_cparams(("parallel", "parallel")),
            name="hyena_dft_direct",
        )(pair(u), pair(x2c), jnp.asarray(fwd).astype(BF16), jnp.asarray(inv).astype(BF16), kr, ki, bias)
        return out.reshape(b, seq_len, c)

    n2 = FFT_MINOR
    n1 = n // n2
    h1 = seq_len // n2
    f1, f1inv = _stacked_dft(n1, h1, n1, scale=1.0 / n)
    f1, f1inv = jnp.asarray(f1).astype(BF16), jnp.asarray(f1inv).astype(BF16)
    g, gt = _second_level_tables(n, n1, n2)
    g, gt = jnp.asarray(g).astype(BF16), jnp.asarray(gt).astype(BF16)
    ha = _fft_first_level(hh, f1)
    kr, ki = _fft_filter_second_level(ha.reshape(2, 2, n1, n2, c), g)
    a = _fft_first_level(u, f1)
    bt = _fft_second_level(a.reshape(b // 2, 2, n1, n2, c), g, gt, kr, ki)
    return _fft_last_level(bt, f1inv, u, x2c, bias)


def _rope_tables(seq_len):
    half = RET_HEAD_DIM // 2
    nf = half // 2
    t = np.arange(seq_len)
    inv = ROPE_BASE ** (-np.arange(nf, dtype=np.float64) / nf)
    ar = (t // GRID_W)[:, None] * inv[None, :]
    ac = (t % GRID_W)[:, None] * inv[None, :]
    cos = np.concatenate([np.cos(ar), np.cos(ar), np.cos(ac), np.cos(ac)], axis=-1)
    sin = np.concatenate([-np.sin(ar), np.sin(ar), -np.sin(ac), np.sin(ac)], axis=-1)
    return cos.astype(np.float32), sin.astype(np.float32)


def _log_sigmoid(x):
    return jnp.minimum(x, 0.0) - jnp.log1p(jnp.exp(-jnp.abs(x)))


def _retention_kernel(*refs, has_init, emit_state, cpb):
    refs = list(refs)
    q_ref, k_ref, v_ref, g_ref, dec_ref, gn_ref = refs[:6]
    del refs[:6]
    if has_init:
        s0f_ref, s0b_ref = refs[:2]
        del refs[:2]
    o_ref = refs.pop(0)
    if emit_state:
        sf_out, sb_out = refs[:2]
        del refs[:2]
    sf_ref, sb_ref, sball_ref = refs

    c = RET_CHUNK
    dh = RET_HEAD_DIM
    nh = N_RET_HEADS
    phase = pl.program_id(1)
    j = pl.program_id(2)
    nb = pl.num_programs(2)

    diff = (lax.broadcasted_iota(jnp.int32, (c, c), 0) - lax.broadcasted_iota(jnp.int32, (c, c), 1)).astype(F32)
    ri = lax.broadcasted_iota(jnp.int32, (c, dh), 0).astype(F32)

    def head_consts(h):
        lgf = _log_sigmoid(dec_ref[0, h])[0:1, :]
        lgb = _log_sigmoid(dec_ref[1, h])[0:1, :]
        return lgf, lgb

    def chunk_wide(lg):
        return jnp.concatenate([lg] * (c // dh), axis=1)

    bb = q_ref.shape[0]
    heads = [(bi, h) for bi in range(bb) for h in range(nh)]

    def load(ref, bi, r0, h):
        return ref[bi, r0:r0 + c, h * dh:(h + 1) * dh]

    dn_t = (((0,), (0,)), ((), ()))
    dn_nt = (((1,), (1,)), ((), ()))

    @pl.when(phase == 0)
    def _backward_sweep():
        @pl.when(j == 0)
        def _():
            for bi, h in heads:
                sb_ref[bi * nh + h] = s0b_ref[bi, h] if has_init else jnp.zeros((dh, dh), F32)

        blk = nb - 1 - j
        for bi, h in heads:
            hs = bi * nh + h
            _, lgb = head_consts(h)
            zeta_b = jnp.exp(lgb * ri)
            cdec_b = jnp.exp(lgb * float(c))
            for cc in reversed(range(cpb)):
                r0 = cc * c
                n = blk * cpb + cc
                s = sb_ref[hs]
                sball_ref[n, hs] = s.astype(BF16)
                kz = (load(k_ref, bi, r0, h).astype(F32) * zeta_b).astype(BF16)
                vv = load(v_ref, bi, r0, h)
                sb_ref[hs] = cdec_b * s + lax.dot_general(kz, vv, dn_t, preferred_element_type=F32)

        if emit_state:
            @pl.when(j == nb - 1)
            def _():
                for bi, h in heads:
                    sb_out[bi, h] = sb_ref[bi * nh + h]

    @pl.when(phase == 1)
    def _forward_sweep():
        @pl.when(j == 0)
        def _():
            for bi, h in heads:
                sf_ref[bi * nh + h] = s0f_ref[bi, h] if has_init else jnp.zeros((dh, dh), F32)

        for bi, h in heads:
            hs = bi * nh + h
            lgf, lgb = head_consts(h)
            mask = (jnp.where(diff >= 0, jnp.exp(chunk_wide(lgf) * jnp.maximum(diff, 0.0)), 0.0)
                    + jnp.where(diff <= 0, jnp.exp(chunk_wide(lgb) * jnp.maximum(-diff, 0.0)), 0.0))
            xi_f = jnp.exp(lgf * (ri + 1.0))
            xi_b = jnp.exp(lgb * (float(c) - ri))
            zeta_f = jnp.exp(lgf * (float(c - 1) - ri))
            cdec_f = jnp.exp(lgf * float(c))
            gn = gn_ref[:, h * dh:(h + 1) * dh]
            for cc in range(cpb):
                r0 = cc * c
                n = j * cpb + cc
                qb = load(q_ref, bi, r0, h)
                kb = load(k_ref, bi, r0, h)
                vv = load(v_ref, bi, r0, h)
                gate = load(g_ref, bi, r0, h).astype(F32)
                sc = lax.dot_general(qb, kb, dn_nt, preferred_element_type=F32)
                inner = jnp.dot((sc * mask).astype(BF16), vv, preferred_element_type=F32)
                s = sf_ref[hs]
                q = qb.astype(F32)
                lhs = jnp.concatenate([q * xi_f, q * xi_b], axis=1).astype(BF16)
                rhs = jnp.concatenate([s.astype(BF16), sball_ref[n, hs]], axis=0)
                o = inner + jnp.dot(lhs, rhs, preferred_element_type=F32)
                mu = jnp.mean(o, axis=-1, keepdims=True)
                d = o - mu
                var = jnp.mean(d * d, axis=-1, keepdims=True)
                y = d * lax.rsqrt(var + EPS) * gn * _silu(gate)
                o_ref[bi, r0:r0 + c, h * dh:(h + 1) * dh] = y.astype(o_ref.dtype)
                kz = (kb.astype(F32) * zeta_f).astype(BF16)
                sf_ref[hs] = cdec_f * s + lax.dot_general(kz, vv, dn_t, preferred_element_type=F32)

        if emit_state:
            @pl.when(j == nb - 1)
            def _():
                for bi, h in heads:
                    sf_out[bi, h] = sf_ref[bi * nh + h]


def _retention(qkvg, dec_f, dec_b, gn_g, s0_f, s0_b, emit_state):
    b, seq_len, _ = qkvg.shape
    nh, dh, c = N_RET_HEADS, RET_HEAD_DIM, RET_CHUNK
    rb = min(seq_len, 1024)
    nb = seq_len // rb
    cpb = rb // c
    bb = max(1, min(4, 1024 // seq_len))
    while b % bb:
        bb //= 2
    has_init = s0_f is not None
    dec = jnp.broadcast_to(jnp.stack([dec_f, dec_b])[:, :, None, None], (2, nh, 8, 128)).astype(F32)
    kv_blk = lambda i, p, j: jnp.where(p == 0, nb - 1 - j, j)
    q_blk = lambda i, p, j: jnp.where(p == 0, 0, j)
    in_specs = [pl.BlockSpec((bb, rb, D_RET), lambda i, p, j: (i, q_blk(i, p, j), 0)),
                pl.BlockSpec((bb, rb, D_RET), lambda i, p, j: (i, kv_blk(i, p, j), 1)),
                pl.BlockSpec((bb, rb, D_RET), lambda i, p, j: (i, kv_blk(i, p, j), 2)),
                pl.BlockSpec((bb, rb, D_RET), lambda i, p, j: (i, q_blk(i, p, j), 3)),
                pl.BlockSpec((2, nh, 8, 128), lambda i, p, j: (0, 0, 0, 0)),
                pl.BlockSpec((1, D_RET), lambda i, p, j: (0, 0))]
    args = [qkvg, qkvg, qkvg, qkvg, dec, gn_g.reshape(1, -1)]
    sspec = pl.BlockSpec((bb, nh, dh, dh), lambda i, p, j: (i, 0, 0, 0))
    if has_init:
        in_specs += [sspec, sspec]
        args += [s0_f, s0_b]
    out_specs = [pl.BlockSpec((bb, rb, D_RET), lambda i, p, j: (i, q_blk(i, p, j), 0))]
    out_shape = [jax.ShapeDtypeStruct((b, seq_len, D_RET), BF16)]
    if emit_state:
        out_specs += [sspec, sspec]
        out_shape += [jax.ShapeDtypeStruct((b, nh, dh, dh), F32)] * 2
    return pl.pallas_call(
        functools.partial(_retention_kernel, has_init=has_init, emit_state=emit_state, cpb=cpb),
        grid=(b // bb, 2, nb),
        in_specs=in_specs,
        out_specs=out_specs,
        out_shape=out_shape,
        scratch_shapes=[pltpu.VMEM((bb * nh, dh, dh), F32), pltpu.VMEM((bb * nh, dh, dh), F32),
                        pltpu.VMEM((nb * cpb, bb * nh, dh, dh), BF16)],
        compiler_params=_cparams(("parallel", "arbitrary", "arbitrary")),
        name="retention",
    )(*args)


def _outproj_kernel(yhy_ref, yret_ref, gates_ref, x_ref, g1_ref, sc_ref, sh_ref, ng_ref,
                    why_ref, wret_ref, wout_ref, wr_ref, x1_ref, h2_ref, aff_ref):
    g, m, d = x_ref.shape
    rows = lambda ref: ref[...].reshape(g * m, ref.shape[2])
    a = jnp.dot(rows(yhy_ref), why_ref[...], preferred_element_type=F32)
    b = jnp.dot(rows(yret_ref), wret_ref[...], preferred_element_type=F32)
    gates = rows(gates_ref)
    merged = gates[:, :d].astype(F32) * a + gates[:, d:].astype(F32) * b
    out = jnp.dot(merged.astype(BF16), wout_ref[...], preferred_element_type=F32)
    x1 = x_ref[...] + g1_ref[...] * out.reshape(g, m, d)
    x1_ref[...] = x1
    ms = jnp.mean(x1 * x1, axis=-1, keepdims=True)
    h = x1 * lax.rsqrt(ms + EPS) * ng_ref[...]
    h = (h * (1.0 + sc_ref[...]) + sh_ref[...]).reshape(g * m, d)
    h_hi = h.astype(BF16)
    h2_ref[...] = h_hi.reshape(g, m, d)
    h_lo = (h - h_hi.astype(F32)).astype(BF16)
    t = jnp.dot(h_hi, wr_ref[...], preferred_element_type=F32)
    logits = t[:, :128] + t[:, 128:] + jnp.dot(h_lo, wr_ref[:, :128], preferred_element_type=F32)
    lane = lax.broadcasted_iota(jnp.int32, logits.shape, 1)
    logits = jnp.where(lane < N_EXPERTS, logits, -jnp.inf)
    e = jnp.exp(logits - jnp.max(logits, axis=-1, keepdims=True))
    aff = e / jnp.sum(e, axis=-1, keepdims=True)
    aff_ref[0] = aff.T[:N_EXPERTS, :]


def _out_projection(y_hy, y_ret, gates, x, g1, sc2, sh2, norm2_g, w_hy_o, w_ret_o, w_out, w_router, cond_block):
    g, s, d = x.shape
    tm = TOKEN_TILE
    m = tm // g
    tok = lambda c: pl.BlockSpec((g, m, c), lambda i: (0, i, 0))
    cond = pl.BlockSpec((g, 1, d), lambda i: (cond_block, 0, 0))
    full = lambda a: pl.BlockSpec(a.shape, lambda i: (0, 0))
    wr = jnp.pad(w_router, ((0, 0), (0, 128 - N_EXPERTS)))
    wr_hi = wr.astype(BF16)
    wr = jnp.concatenate([wr_hi, (wr - wr_hi.astype(F32)).astype(BF16)], axis=1)
    return pl.pallas_call(
        _outproj_kernel,
        grid=(s // m,),
        in_specs=[tok(D_HYENA), tok(D_RET), tok(2 * d), tok(d), cond, cond, cond,
                  pl.BlockSpec((1, d), lambda i: (0, 0)),
                  full(w_hy_o), full(w_ret_o), full(w_out), full(wr)],
        out_specs=[tok(d), tok(d), pl.BlockSpec((1, N_EXPERTS, tm), lambda i: (i, 0, 0))],
        out_shape=[jax.ShapeDtypeStruct((g, s, d), F32), jax.ShapeDtypeStruct((g, s, d), BF16),
                   jax.ShapeDtypeStruct((s // m, N_EXPERTS, tm), F32)],
        compiler_params=_cparams(("parallel",)),
        name="out_projection_router",
    )(y_hy, y_ret, gates, x, g1, sc2, sh2, norm2_g.reshape(1, d), w_hy_o, w_ret_o, w_out, wr)


def _select_kernel(aff_ref, tri_ref, rank_ref, cnt_ref, start_ref, tot_ref, *, cap, idx_bits, groups):
    nt, ne, tm = aff_ref.shape
    a = aff_ref[...]

    def count(m):
        return jnp.sum(jnp.sum(m, axis=0, keepdims=True), axis=2, keepdims=True)

    def thr_step(s, thr):
        cand = thr | (1 << (30 - s))
        cnt = count(jnp.where(a >= pltpu.bitcast(cand, F32), 1.0, 0.0))
        return jnp.where(cnt >= float(cap), cand, thr)

    thr = pltpu.bitcast(lax.fori_loop(0, 31, thr_step, jnp.zeros((1, ne, 1), jnp.int32)), F32)
    gt = a > thr
    eq = a == thr
    need = float(cap) - count(jnp.where(gt, 1.0, 0.0))
    m = tm // groups
    tile = lax.broadcasted_iota(jnp.int32, (nt, 1, tm), 0)
    lane = lax.broadcasted_iota(jnp.int32, (nt, 1, tm), 2)
    idx = (lane // m) * (nt * m) + tile * m + lane % m

    def idx_step(s, lim):
        cand = lim | (1 << (idx_bits - 1 - s))
        cnt = count(jnp.where(eq, jnp.where(idx < cand, 1.0, 0.0), 0.0))
        return jnp.where(cnt < need, cand, lim)

    lim = lax.fori_loop(0, idx_bits, idx_step, jnp.zeros((1, ne, 1), jnp.int32))
    sel = jnp.where(gt, 1.0, jnp.where(eq, jnp.where(idx <= lim, 1.0, 0.0), 0.0))
    sel2 = sel.reshape(nt * ne, tm).astype(BF16)
    prefix = jnp.dot(sel2, tri_ref[...], preferred_element_type=F32)
    rank = jnp.where(sel2 > 0, prefix, -1.0).astype(jnp.int32)
    rank_ref[...] = rank.reshape(nt, ne, tm)
    ones = jnp.ones((tm, 128), BF16)
    cnt = jnp.dot(sel2, ones, preferred_element_type=F32).astype(jnp.int32).reshape(nt, ne, 128)
    cnt_ref[...] = cnt
    acc = jnp.zeros((ne, 128), jnp.int32)
    for t in range(nt):
        start_ref[t] = acc
        acc = acc + ((cnt[t] + (ROW_ALIGN - 1)) & (-ROW_ALIGN))
    tot_ref[...] = acc


def _select(aff, cap, groups):
    nt, ne, tm = aff.shape
    idx_bits = max(1, int(math.ceil(math.log2(nt * tm))))
    r = lax.broadcasted_iota(jnp.int32, (tm, tm), 0)
    c = lax.broadcasted_iota(jnp.int32, (tm, tm), 1)
    tri = (r < c).astype(BF16)
    rank, cnt, start, tot = pl.pallas_call(
        functools.partial(_select_kernel, cap=cap, idx_bits=idx_bits, groups=groups),
        out_shape=[jax.ShapeDtypeStruct((nt, ne, tm), jnp.int32),
                   jax.ShapeDtypeStruct((nt, ne, 128), jnp.int32),
                   jax.ShapeDtypeStruct((nt, ne, 128), jnp.int32),
                   jax.ShapeDtypeStruct((ne, 128), jnp.int32)],
        compiler_params=_cparams(None),
        name="expert_choice_select",
    )(aff, tri)
    return rank, cnt[:, :, 0], start[:, :, 0], tot[:, 0]


def _expert_block(cap):
    return 512 if cap >= 2048 else 256


def _used_rows(cap, nt):
    eb = _expert_block(cap)
    return -(-(cap + ROW_ALIGN * nt) // eb) * eb


def _list_rows(cap, nt):
    max_rounds = -(-TOKEN_TILE // SLOT_ROWS)
    return _used_rows(cap, nt) + max(_expert_block(cap), max_rounds * SLOT_ROWS)


def _num_rounds(cnt_sm, i):
    m = cnt_sm[i, 0]
    for e in range(1, N_EXPERTS):
        m = jnp.maximum(m, cnt_sm[i, e])
    return jnp.maximum((m + SLOT_ROWS - 1) // SLOT_ROWS, 1)


def _gather_kernel(start_sm, cnt_sm, tot_sm, h_ref, rank_ref, xs_hbm, stage, zbuf, sem):
    i = pl.program_id(0)
    ne = N_EXPERTS
    slot = i % 2
    rank = rank_ref[0]
    sub = lax.broadcasted_iota(jnp.int32, (SLOT_ROWS, rank.shape[1]), 0)

    def copy(s, e, off):
        return pltpu.make_async_copy(stage.at[s, pl.ds(e * SLOT_ROWS, SLOT_ROWS)],
                                     xs_hbm.at[e, pl.ds(off, SLOT_ROWS)], sem.at[e])

    def fill(r):
        h = h_ref[...].reshape(rank.shape[1], h_ref.shape[2])
        for e in range(ne):
            onehot = jnp.where(rank[e:e + 1, :] == sub + r * SLOT_ROWS, 1.0, 0.0).astype(BF16)
            stage[slot, e * SLOT_ROWS:(e + 1) * SLOT_ROWS, :] = jnp.dot(
                onehot, h, preferred_element_type=F32).astype(BF16)

    def start_all(r):
        for e in range(ne):
            copy(slot, e, pl.multiple_of(start_sm[i, e] + r * SLOT_ROWS, ROW_ALIGN)).start()

    def wait_all(s):
        for e in range(ne):
            copy(s, e, 0).wait()

    fill(0)

    @pl.when(i > 0)
    def _():
        wait_all(1 - slot)

    start_all(0)

    def extra_round(r, carry):
        wait_all(slot)
        fill(r)
        start_all(r)
        return carry

    lax.fori_loop(1, _num_rounds(cnt_sm, i), extra_round, 0)

    @pl.when(i == pl.num_programs(0) - 1)
    def _zero_tail():
        wait_all(slot)
        zbuf[...] = jnp.zeros(zbuf.shape, zbuf.dtype)

        def zcopy(e):
            off = pl.multiple_of(tot_sm[e], ROW_ALIGN)
            return pltpu.make_async_copy(zbuf, xs_hbm.at[e, pl.ds(off, zbuf.shape[0])], sem.at[e])

        for e in range(ne):
            zcopy(e).start()
        for e in range(ne):
            zcopy(e).wait()


def _gather(h2, rank, start, cnt, tot, cap):
    g, _, d = h2.shape
    nt, ne, tm = rank.shape
    rl = _list_rows(cap, nt)
    return pl.pallas_call(
        _gather_kernel,
        grid_spec=pltpu.PrefetchScalarGridSpec(
            num_scalar_prefetch=3,
            grid=(nt,),
            in_specs=[pl.BlockSpec((g, tm // g, d), lambda i, *_: (0, i, 0)),
                      pl.BlockSpec((1, ne, tm), lambda i, *_: (i, 0, 0))],
            out_specs=pl.BlockSpec(memory_space=pl.ANY),
            scratch_shapes=[pltpu.VMEM((2, ne * SLOT_ROWS, d), BF16),
                            pltpu.VMEM((_expert_block(cap), d), BF16),
                            pltpu.SemaphoreType.DMA((ne,))]),
        out_shape=jax.ShapeDtypeStruct((ne, rl, d), BF16),
        compiler_params=_cparams(("arbitrary",)),
        name="expert_gather",
    )(start, cnt, tot, h2, rank)


def _expert_kernel(*refs, nblocks):
    ns = len(nblocks)
    tot_sms, xs_refs = refs[:ns], refs[ns:2 * ns]
    wg_ref, wu_ref, wd_ref = refs[2 * ns:2 * ns + 3]
    ye_refs = refs[2 * ns + 3:3 * ns + 3]
    wg_bf, wu_bf, wd_bf = refs[3 * ns + 3:]
    e = pl.program_id(0)
    j = pl.program_id(1)

    @pl.when(j == 0)
    def _():
        wg_bf[...] = wg_ref[0].astype(BF16)
        wu_bf[...] = wu_ref[0].astype(BF16)
        wd_bf[...] = wd_ref[0].astype(BF16)

    base = 0
    for s in range(ns):
        jj = j - base
        live = jnp.logical_and(jnp.logical_and(jj >= 0, jj < nblocks[s]), jj * xs_refs[s].shape[1] < tot_sms[s][e])

        @pl.when(live)
        def _(xs_ref=xs_refs[s], ye_ref=ye_refs[s]):
            x = xs_ref[0]
            g = jnp.dot(x, wg_bf[...], preferred_element_type=F32)
            u = jnp.dot(x, wu_bf[...], preferred_element_type=F32)
            hid = (_silu(g) * u).astype(BF16)
            ye_ref[0] = jnp.dot(hid, wd_bf[...], preferred_element_type=F32).astype(ye_ref.dtype)

        base += nblocks[s]


def _experts(sets, w_gate, w_up, w_down):
    ns = len(sets)
    ne, _, d = sets[0][0].shape
    f = w_gate.shape[2]
    nblocks = tuple(used // eb for _, _, used, eb in sets)

    def block_spec(s):
        eb = sets[s][3]
        base = sum(nblocks[:s])

        def index(e, j, *tots):
            return (e, jnp.clip(j - base, 0, (tots[s][e] - 1) // eb), 0)

        return pl.BlockSpec((1, eb, d), index)

    wspec = lambda shape: pl.BlockSpec(shape, lambda e, j, *tots: (e, 0, 0))
    return pl.pallas_call(
        functools.partial(_expert_kernel, nblocks=nblocks),
        grid_spec=pltpu.PrefetchScalarGridSpec(
            num_scalar_prefetch=ns,
            grid=(ne, sum(nblocks)),
            in_specs=[block_spec(s) for s in range(ns)] + [wspec((1, d, f)), wspec((1, d, f)), wspec((1, f, d))],
            out_specs=[block_spec(s) for s in range(ns)],
            scratch_shapes=[pltpu.VMEM((d, f), BF16), pltpu.VMEM((d, f), BF16), pltpu.VMEM((f, d), BF16)]),
        out_shape=[jax.ShapeDtypeStruct((ne, used, d), BF16) for _, _, used, _ in sets],
        compiler_params=_cparams(("arbitrary", "arbitrary")),
        name="expert_ffn",
    )(*[t for _, t, _, _ in sets], *[x for x, _, _, _ in sets], w_gate, w_up, w_down)


def _combine_kernel(start_sm, cnt_sm, tot_sm, x1_ref, rank_ref, aff_ref, g2_ref, nf_ref, ye_hbm, o_ref, buf, sem, *,
                    final_norm, eb):
    i = pl.program_id(0)
    nt = pl.num_programs(0)
    ne = N_EXPERTS
    slot = i % 2
    rank = rank_ref[0]
    aff = aff_ref[0]
    sub = lax.broadcasted_iota(jnp.int32, (SLOT_ROWS, rank.shape[1]), 0)
    dn_t = (((0,), (0,)), ((), ()))

    def window(t, e, r):
        want = start_sm[t, e] + r * SLOT_ROWS
        written = ((tot_sm[e] + eb - 1) // eb) * eb
        off = jnp.minimum(want, written - SLOT_ROWS)
        return pl.multiple_of(off, ROW_ALIGN), want - off

    def copy(t, s, e, r):
        off, _ = window(t, e, r)
        return pltpu.make_async_copy(ye_hbm.at[e, pl.ds(off, SLOT_ROWS)],
                                     buf.at[s, pl.ds(e * SLOT_ROWS, SLOT_ROWS)], sem.at[s, e])

    def weighted(r):
        parts = []
        for e in range(ne):
            _, shift = window(i, e, r)
            local = rank[e:e + 1, :] - r * SLOT_ROWS
            hit = jnp.where(local >= 0, local + shift, -1) == sub
            parts.append(jnp.where(hit, aff[e:e + 1, :], 0.0).astype(BF16))
        return lax.dot_general(jnp.concatenate(parts, axis=0), buf[slot], dn_t, preferred_element_type=F32)

    @pl.when(i == 0)
    def _():
        for e in range(ne):
            copy(0, 0, e, 0).start()

    @pl.when(i + 1 < nt)
    def _prefetch_next_tile():
        for e in range(ne):
            copy(i + 1, 1 - slot, e, 0).start()

    for e in range(ne):
        copy(i, slot, e, 0).wait()
    y0 = weighted(0)

    def extra_round(r, acc):
        for e in range(ne):
            @pl.when(cnt_sm[i, e] > r * SLOT_ROWS)
            def _():
                copy(i, slot, e, r).start()
        for e in range(ne):
            @pl.when(cnt_sm[i, e] > r * SLOT_ROWS)
            def _():
                copy(i, slot, e, r).wait()
        return acc + weighted(r)

    y = lax.fori_loop(1, _num_rounds(cnt_sm, i), extra_round, y0)
    x2 = x1_ref[...] + g2_ref[...] * y.reshape(x1_ref.shape)
    if final_norm:
        ms = jnp.mean(x2 * x2, axis=-1, keepdims=True)
        x2 = x2 * lax.rsqrt(ms + EPS) * nf_ref[...]
    o_ref[...] = x2


def _combine(x1, rank, aff, start, cnt, tot, g2, norm_f_g, ye, cond_block, eb):
    g, s, d = x1.shape
    nt, ne, tm = rank.shape
    tok = pl.BlockSpec((g, tm // g, d), lambda i, *_: (0, i, 0))
    final_norm = norm_f_g is not None
    if not final_norm:
        norm_f_g = jnp.ones((d,), F32)
    return pl.pallas_call(
        functools.partial(_combine_kernel, final_norm=final_norm, eb=eb),
        grid_spec=pltpu.PrefetchScalarGridSpec(
            num_scalar_prefetch=3,
            grid=(nt,),
            in_specs=[tok,
                      pl.BlockSpec((1, ne, tm), lambda i, *_: (i, 0, 0)),
                      pl.BlockSpec((1, ne, tm), lambda i, *_: (i, 0, 0)),
                      pl.BlockSpec((g, 1, d), lambda i, *_: (cond_block, 0, 0)),
                      pl.BlockSpec((1, d), lambda i, *_: (0, 0)),
                      pl.BlockSpec(memory_space=pl.ANY)],
            out_specs=tok,
            scratch_shapes=[pltpu.VMEM((2, ne * SLOT_ROWS, d), BF16),
                            pltpu.SemaphoreType.DMA((2, ne))]),
        out_shape=jax.ShapeDtypeStruct((g, s, d), F32),
        compiler_params=_cparams(("arbitrary",)),
        name="expert_combine",
    )(start, cnt, tot, x1, rank, aff, g2, norm_f_g.reshape(1, d), ye)


def _layer_front(x, batch, seq_len, mods, cond_of_tile, moe_groups, cond_block, p, s0_f, s0_b, rope, emit_state):
    t, d = x.shape
    sh1, sc1, g1, sh2, sc2, g2 = mods
    u, x2c, qkvg, gates = _in_projection(x, p["norm1_g"], sc1, sh1, p["w_in"], p["hy_conv_w"], p["hy_conv_b"],
                                         cond_of_tile, seq_len, rope)
    seq = lambda a: a.reshape(batch, seq_len, a.shape[-1])
    y_hy = _hyena_long_conv(seq(u), seq(x2c), p).reshape(t, -1)
    ret = _retention(qkvg.reshape(batch, seq_len, -1), p["ret_decay_fwd"], p["ret_decay_bwd"], p["ret_gn_g"],
                     s0_f, s0_b, emit_state)
    y_ret = ret[0].reshape(t, -1)
    view = lambda a: a.reshape(moe_groups, t // moe_groups, a.shape[-1])
    x1, h2, aff = _out_projection(view(y_hy), view(y_ret), view(gates), view(x), g1, sc2, sh2, p["norm2_g"],
                                  p["w_hy_o"], p["w_ret_o"], p["w_out"], p["w_router"], cond_block)
    cap = (EC_CAPACITY_FACTOR * t) // N_EXPERTS
    eb = _expert_block(cap)
    rank, cnt, start, tot = _select(aff, cap, moe_groups)
    xs = _gather(h2, rank, start, cnt, tot, cap)
    routed = (xs, tot, _used_rows(cap, rank.shape[0]), eb)
    back = dict(x1=x1, rank=rank, aff=aff, start=start, cnt=cnt, tot=tot, g2=g2, cond_block=cond_block, eb=eb)
    return routed, back, ret[1:]


def _layer_back(back, ye, final_g):
    out = _combine(back["x1"], back["rank"], back["aff"], back["start"], back["cnt"], back["tot"], back["g2"],
                   final_g, ye, back["cond_block"], back["eb"])
    return out.reshape(-1, out.shape[-1])


def kernel(x_prompt, x_sample, state_ret_fwd, state_ret_bwd, c, c_ctx, w_mod, b_mod, norm1_g, w_in, hy_conv_w, hy_conv_b, hy_f_w1, hy_f_b1, hy_f_freq, hy_f_w2, hy_f_b2, hy_f_w3, hy_decay, hy_bias, ret_decay_fwd, ret_decay_bwd, ret_gn_g, w_hy_o, w_ret_o, w_out, norm2_g, w_router, w_e_gate, w_e_up, w_e_down, norm_f_g):
    b, s, d = x_prompt.shape
    bd, sd, _ = x_sample.shape
    depth = w_mod.shape[0]
    assert (b * s) % TOKEN_TILE == 0 and sd % TOKEN_TILE == 0 and b % 2 == 0 and bd % 2 == 0
    ncond = -(-(bd + 1) // 8) * 8
    lat_groups = bd if (TOKEN_TILE % (16 * bd) == 0 and sd % (TOKEN_TILE // bd) == 0) else 1
    assert lat_groups == bd or bd == 1
    conds = jnp.concatenate([c, c_ctx[None], jnp.zeros((ncond - bd - 1, d), F32)], axis=0)
    ctx_cond = lambda i: bd
    lat_cond = lambda i: i // (sd // TOKEN_TILE)
    xp = x_prompt.reshape(b * s, d)
    xs = x_sample.reshape(bd * sd, d)
    new_f, new_b = [], []
    for l in range(depth):
        mod = _modulation(conds, w_mod[l], b_mod[l])
        mods = [mod[:, k * d:(k + 1) * d].reshape(ncond, 1, d) for k in range(6)]
        p = {"norm1_g": norm1_g[l], "w_in": w_in[l].astype(BF16), "hy_conv_w": hy_conv_w[l],
             "hy_conv_b": hy_conv_b[l], "hy_f_w1": hy_f_w1[l], "hy_f_b1": hy_f_b1[l], "hy_f_freq": hy_f_freq[l],
             "hy_f_w2": hy_f_w2[l], "hy_f_b2": hy_f_b2[l], "hy_f_w3": hy_f_w3[l], "hy_decay": hy_decay[l],
             "hy_bias": hy_bias[l], "ret_decay_fwd": ret_decay_fwd[l], "ret_decay_bwd": ret_decay_bwd[l],
             "ret_gn_g": ret_gn_g[l], "w_hy_o": w_hy_o[l].astype(BF16), "w_ret_o": w_ret_o[l].astype(BF16),
             "w_out": w_out[l].astype(BF16), "norm2_g": norm2_g[l], "w_router": w_router[l],
             "w_e_gate": w_e_gate[l], "w_e_up": w_e_up[l], "w_e_down": w_e_down[l]}
        final_g = norm_f_g if l == depth - 1 else None
        routed_p, back_p, (s_f, s_b) = _layer_front(xp, b, s, mods, ctx_cond, 1, bd, p, None, None, False, True)
        new_f.append(s_f)
        new_b.append(s_b)
        routed_s, back_s, _ = _layer_front(xs, bd, sd, mods, lat_cond, lat_groups, 0, p, state_ret_fwd[:, l],
                                           state_ret_bwd[:, l], True, False)
        ye_p, ye_s = _experts([routed_p, routed_s], p["w_e_gate"], p["w_e_up"], p["w_e_down"])
        xp = _layer_back(back_p, ye_p, final_g)
        xs = _layer_back(back_s, ye_s, final_g)
    y_prompt = xp.reshape(b, s, d)
    y_sample = xs.reshape(bd, sd, d)
    return (y_prompt, y_sample, jnp.stack(new_f, axis=1), jnp.stack(new_b, axis=1))
```

```python
import functools
import math

import numpy as np
import jax
import jax.numpy as jnp
from jax import lax
from jax.experimental import pallas as pl
from jax.experimental.pallas import tpu as pltpu

F32 = jnp.float32
BF16 = jnp.bfloat16
HIGHEST = lax.Precision.HIGHEST

EPS = 1e-6
D_HYENA = 512
D_RET = 512
N_RET_HEADS = 4
RET_HEAD_DIM = 128
RET_CHUNK = 256
GRID_W = 64
FILTER_EMB = 33
ROPE_BASE = 10000.0
N_EXPERTS = 16
EC_CAPACITY_FACTOR = 2

TOKEN_TILE = 512
SLOT_ROWS = 96
ROW_ALIGN = 16
HALO_ROWS = 16
FFT_MINOR = 64
FFT_GROUP = 16
VMEM_LIMIT = 56 * 1024 * 1024


def _cparams(sem):
    return pltpu.CompilerParams(dimension_semantics=sem, vmem_limit_bytes=VMEM_LIMIT)


def _silu(x):
    return x * jax.nn.sigmoid(x)


def _mod_kernel(c_ref, w_ref, b_ref, o_ref):
    s = _silu(c_ref[...])
    o_ref[...] = jnp.dot(s, w_ref[...], preferred_element_type=F32, precision=HIGHEST) + b_ref[...]


def _modulation(conds, w_mod, b_mod):
    nc, d = conds.shape
    return pl.pallas_call(
        _mod_kernel,
        grid=(6,),
        in_specs=[pl.BlockSpec((nc, d), lambda j: (0, 0)),
                  pl.BlockSpec((d, d), lambda j: (0, j)),
                  pl.BlockSpec((1, d), lambda j: (0, j))],
        out_specs=pl.BlockSpec((nc, d), lambda j: (0, j)),
        out_shape=jax.ShapeDtypeStruct((nc, 6 * d), F32),
        compiler_params=_cparams(("arbitrary",)),
        name="modulation",
    )(conds, w_mod, b_mod.reshape(1, -1))


def _inproj_kernel(*refs, splits, rope, seq_len, halo):
    refs = list(refs)
    x_ref = refs.pop(0)
    if halo:
        xp_ref, xn_ref = refs.pop(0), refs.pop(0)
    g_ref, sc_ref, sh_ref, w_ref, cw_ref, cb_ref = refs[:6]
    del refs[:6]
    if rope:
        cos_ref, sin_ref = refs.pop(0), refs.pop(0)
    u_ref, x2c_ref, qkvg_ref, gates_ref, cv_sc, cx_sc = refs

    def normed(ref):
        x = ref[...]
        ms = jnp.mean(x * x, axis=-1, keepdims=True)
        h = x * lax.rsqrt(ms + EPS) * g_ref[...]
        return (h * (1.0 + sc_ref[0]) + sh_ref[0]).astype(BF16)

    hb = normed(x_ref)
    tm = hb.shape[0]
    nz, nq, ng = splits
    cw = 512
    dh = RET_HEAD_DIM
    hr = HALO_ROWS
    if halo:
        t0 = pl.program_id(0) * tm
        hp = jnp.where(t0 % seq_len != 0, normed(xp_ref), jnp.zeros((hr, hb.shape[1]), BF16))
        hn = jnp.where((t0 + tm) % seq_len != 0, normed(xn_ref), jnp.zeros((hr, hb.shape[1]), BF16))
        hext = jnp.concatenate([hp, hb, hn], axis=0)

    piece = tm if halo else seq_len
    pitch = piece + hr
    if not halo:
        for s in range(tm // piece + 1):
            cx_sc[s * pitch:s * pitch + hr, :] = jnp.zeros((hr, cw), F32)

    def conv(c0):
        taps = cw_ref[:, c0:c0 + cw]
        bias = cb_ref[:, c0:c0 + cw]
        if halo:
            cx_sc[...] = jnp.dot(hext, w_ref[:, c0:c0 + cw], preferred_element_type=F32)
        else:
            acc = jnp.dot(hb, w_ref[:, c0:c0 + cw], preferred_element_type=F32)
            for s in range(tm // piece):
                cx_sc[hr + s * pitch:hr + s * pitch + piece, :] = acc[s * piece:(s + 1) * piece]
        outs = []
        for s in range(tm // piece):
            r0 = hr + s * pitch
            outs.append(bias + cx_sc[r0 - 1:r0 - 1 + piece, :] * taps[0:1] + cx_sc[r0:r0 + piece, :] * taps[1:2]
                        + cx_sc[r0 + 1:r0 + 1 + piece, :] * taps[2:3])
        return outs[0] if len(outs) == 1 else jnp.concatenate(outs, axis=0)

    cv_sc[...] = conv(0)
    u_ref[...] = (cv_sc[...] * conv(D_HYENA)).astype(BF16)
    x2c_ref[...] = conv(2 * D_HYENA).astype(BF16)

    for c0 in range(nz, nz + nq + ng, cw):
        acc = jnp.dot(hb, w_ref[:, c0:c0 + cw], preferred_element_type=F32)
        if c0 < nz + nq:
            part = (c0 - nz) // D_RET
            if part == 1:
                acc = acc * (dh ** -0.5)
            if rope and part < 2:
                lane = lax.broadcasted_iota(jnp.int32, (acc.shape[0], dh), 1)
                swap_hi = (lane % (dh // 2)) < (dh // 4)
                cs, sn = cos_ref[...], sin_ref[...]
                heads = []
                for hh in range(cw // dh):
                    xh = acc[:, hh * dh:(hh + 1) * dh]
                    rot = jnp.where(swap_hi, pltpu.roll(xh, dh - dh // 4, axis=1), pltpu.roll(xh, dh // 4, axis=1))
                    heads.append(xh * cs + rot * sn)
                acc = jnp.concatenate(heads, axis=1)
            qkvg_ref[:, c0 - nz:c0 - nz + cw] = acc.astype(BF16)
        else:
            gates_ref[:, c0 - nz - nq:c0 - nz - nq + cw] = jax.nn.sigmoid(acc).astype(BF16)


def _in_projection(x, norm_g, sc, sh, w_in_bf, conv_w, conv_b, cond_of_tile, seq_len, rope):
    t, d = x.shape
    splits = (3 * D_HYENA, 4 * D_RET, 2 * d)
    assert D_RET == 512 and D_HYENA == 512
    tm = TOKEN_TILE
    assert seq_len % tm == 0 or tm % seq_len == 0
    halo = seq_len > tm
    row = lambda i: (i, 0)
    cond = lambda i: (cond_of_tile(i), 0, 0)
    full = lambda a: pl.BlockSpec(a.shape, lambda i: (0, 0))
    args, specs = [x], [pl.BlockSpec((tm, d), row)]
    if halo:
        hb_per_tile = tm // HALO_ROWS
        last = t // HALO_ROWS - 1
        args += [x, x]
        specs += [pl.BlockSpec((HALO_ROWS, d), lambda i: (jnp.maximum(i * hb_per_tile - 1, 0), 0)),
                  pl.BlockSpec((HALO_ROWS, d), lambda i: (jnp.minimum((i + 1) * hb_per_tile, last), 0))]
    cbias = conv_b.reshape(1, -1)
    args += [norm_g.reshape(1, d), sc, sh, w_in_bf, conv_w, cbias]
    specs += [pl.BlockSpec((1, d), lambda i: (0, 0)), pl.BlockSpec((1, 1, d), cond), pl.BlockSpec((1, 1, d), cond),
              full(w_in_bf), full(conv_w), full(cbias)]
    if rope:
        tiles_per_seq = seq_len // tm
        args += list(_rope_tables(seq_len))
        specs += [pl.BlockSpec((tm, RET_HEAD_DIM), lambda i: (i % tiles_per_seq, 0))] * 2
    return pl.pallas_call(
        functools.partial(_inproj_kernel, splits=splits, rope=rope, seq_len=seq_len, halo=halo),
        grid=(t // tm,),
        in_specs=specs,
        out_specs=[pl.BlockSpec((tm, D_HYENA), row),
                   pl.BlockSpec((tm, D_HYENA), row),
                   pl.BlockSpec((tm, splits[1]), row),
                   pl.BlockSpec((tm, splits[2]), row)],
        out_shape=[jax.ShapeDtypeStruct((t, D_HYENA), BF16),
                   jax.ShapeDtypeStruct((t, D_HYENA), BF16),
                   jax.ShapeDtypeStruct((t, splits[1]), BF16),
                   jax.ShapeDtypeStruct((t, splits[2]), BF16)],
        scratch_shapes=[pltpu.VMEM((tm, 512), F32),
                        pltpu.VMEM((tm + 2 * HALO_ROWS if halo else (tm // seq_len) * (seq_len + HALO_ROWS) + HALO_ROWS,
                                    512), F32)],
        compiler_params=_cparams(("parallel",)),
        name="in_projection",
    )(*args)


def _filter_features(seq_len):
    t = np.linspace(0.0, 1.0, seq_len, dtype=np.float32)[:, None]
    bands = (FILTER_EMB - 1) // 2
    w = (np.float32(2.0 * math.pi) * np.arange(seq_len, dtype=np.float32)) / np.float32(seq_len)
    f = np.linspace(1e-4, bands - 1, bands, dtype=np.float32)
    ang = (w[:, None] * f[None, :]).astype(np.float64)
    z = np.concatenate([t, np.cos(ang), -np.sin(ang)], axis=-1).astype(np.float32)
    return np.pad(z, ((0, 0), (0, 128 - FILTER_EMB)))


def _filter_kernel(z_ref, w1_ref, b1_ref, fr_ref, w2_ref, b2_ref, w3_ref, dec_ref, o_ref):
    z = z_ref[...]
    fr = fr_ref[...]
    dot = functools.partial(jnp.dot, preferred_element_type=F32, precision=HIGHEST)
    h = jnp.sin(fr * (dot(z, w1_ref[...]) + b1_ref[...]))
    h = jnp.sin(fr * (dot(h, w2_ref[...]) + b2_ref[...]))
    split = lambda a: (a.astype(BF16), (a - a.astype(BF16).astype(F32)).astype(BF16))
    (h_hi, h_lo), (w_hi, w_lo) = split(h), split(w3_ref[...])
    bdot = functools.partial(jnp.dot, preferred_element_type=F32)
    h = (bdot(h_hi, w_hi) + bdot(h_hi, w_lo) + bdot(h_lo, w_hi)) * jnp.exp(-z[:, 0:1] * jnp.abs(dec_ref[...]))
    rows = h.shape[0]
    grow = pl.program_id(0) * rows + lax.broadcasted_iota(jnp.int32, (rows, 1), 0)
    o_ref[0] = h[:, :D_HYENA].astype(o_ref.dtype)
    o_ref[1] = jnp.zeros((rows, D_HYENA), o_ref.dtype)
    o_ref[2] = jnp.where(grow == 0, 0.0, h[:, D_HYENA:]).astype(o_ref.dtype)
    o_ref[3] = jnp.zeros((rows, D_HYENA), o_ref.dtype)


def _hyena_filters(seq_len, p, out_dtype):
    z = jnp.asarray(_filter_features(seq_len))
    w1 = jnp.pad(p["hy_f_w1"], ((0, 128 - FILTER_EMB), (0, 0)))
    rows = min(seq_len, 512)
    full = lambda a: pl.BlockSpec(a.shape, lambda i: (0,) * a.ndim)
    ops = [w1, p["hy_f_b1"].reshape(1, -1), p["hy_f_freq"].reshape(1, -1), p["hy_f_w2"],
           p["hy_f_b2"].reshape(1, -1), p["hy_f_w3"], p["hy_decay"].reshape(1, -1)]
    return pl.pallas_call(
        _filter_kernel,
        grid=(seq_len // rows,),
        in_specs=[pl.BlockSpec((rows, 128), lambda i: (i, 0))] + [full(a) for a in ops],
        out_specs=pl.BlockSpec((4, rows, D_HYENA), lambda i: (0, i, 0)),
        out_shape=jax.ShapeDtypeStruct((4, seq_len, D_HYENA), out_dtype),
        compiler_params=_cparams(("arbitrary",)),
        name="hyena_filter",
    )(z, *ops)


def _stacked_dft(n_out, n_in, modulus, scale=1.0):
    k = np.arange(n_out, dtype=np.int64)[:, None]
    n = np.arange(n_in, dtype=np.int64)[None, :]
    th = ((k * n) % modulus) * (2.0 * math.pi / modulus)
    c, s = np.cos(th), np.sin(th)
    fwd = np.block([[c, s], [-s, c]]).astype(np.float32)
    inv = (np.block([[c.T, -s.T], [s.T, c.T]]) * scale).astype(np.float32)
    return fwd, inv


def _second_level_tables(n_total, n1, n2):
    k1 = np.arange(n1, dtype=np.int64)[:, None, None]
    k2 = np.arange(n2, dtype=np.int64)[None, :, None]
    m = np.arange(n2, dtype=np.int64)[None, None, :]
    th = ((m * (k1 + n1 * k2)) % n_total) * (2.0 * math.pi / n_total)
    c, s = np.cos(th), np.sin(th)
    g = np.concatenate([np.concatenate([c, s], axis=2), np.concatenate([-s, c], axis=2)], axis=1)
    return g.astype(np.float32), np.swapaxes(g, 1, 2).astype(np.float32)


def _skewed(n):
    return n + 1


def _fft1_kernel(h_ref, f_ref, a_ref, u_sc, t_sc):
    n2 = FFT_MINOR
    _, seq_len, cb = h_ref.shape
    h1 = seq_len // n2
    nk = a_ref.shape[1]
    up, tp = _skewed(n2), _skewed(nk)
    for b in range(2):
        for q in range(h1):
            u_sc[b, q * up:q * up + n2, :] = h_ref[b, q * n2:(q + 1) * n2, :].astype(F32)

    def fwd(gi, carry):
        m0 = gi * FFT_GROUP
        cols = []
        for d in range(FFT_GROUP):
            xr = u_sc[0, pl.ds(m0 + d, h1, stride=up), :]
            xi = u_sc[1, pl.ds(m0 + d, h1, stride=up), :]
            cols.append(jnp.concatenate([xr, xi], axis=0))
        x = jnp.concatenate(cols, axis=1).astype(BF16)
        res = jnp.dot(f_ref[...], x, preferred_element_type=F32)
        for d in range(FFT_GROUP):
            t_sc[pl.ds((m0 + d) * tp, nk), :] = res[:, d * cb:(d + 1) * cb]
        return carry

    lax.fori_loop(0, n2 // FFT_GROUP, fwd, 0, unroll=2)

    def transpose(k, carry):
        a_ref[0, k] = t_sc[pl.ds(k, n2, stride=tp), :].astype(a_ref.dtype)
        return carry

    lax.fori_loop(0, nk, transpose, 0, unroll=8)


def _fft_first_level(h, f1):
    b, seq_len, c = h.shape
    cb = 128
    return pl.pallas_call(
        _fft1_kernel,
        grid=(b // 2, c // cb),
        in_specs=[pl.BlockSpec((2, seq_len, cb), lambda i, j: (i, 0, j)),
                  pl.BlockSpec(f1.shape, lambda i, j: (0, 0))],
        out_specs=pl.BlockSpec((1, f1.shape[0], FFT_MINOR, cb), lambda i, j: (i, 0, 0, j)),
        out_shape=jax.ShapeDtypeStruct((b // 2, f1.shape[0], FFT_MINOR, c), BF16),
        scratch_shapes=[pltpu.VMEM((2, (seq_len // FFT_MINOR) * _skewed(FFT_MINOR), cb), F32),
                        pltpu.VMEM((FFT_MINOR * _skewed(f1.shape[0]), cb), F32)],
        compiler_params=_cparams(("parallel", "parallel")),
        name="hyena_dft_level1",
    )(h, f1)


def _fft_s2f_kernel(a_ref, g_ref, kr_ref, ki_ref):
    _, _, kb, n2, c = a_ref.shape
    for kk in range(kb):
        g = g_ref[kk]
        hf = jnp.dot(g, a_ref[0, :, kk].reshape(2 * n2, c), preferred_element_type=F32)
        hb = jnp.dot(g, a_ref[1, :, kk].reshape(2 * n2, c), preferred_element_type=F32)
        kr_ref[kk] = hf[:n2] + hb[:n2]
        ki_ref[kk] = hf[n2:] - hb[n2:]


def _fft_filter_second_level(a, g, kb=16):
    _, _, n1, n2, c = a.shape
    spec = pl.BlockSpec((kb, n2, c), lambda i: (i, 0, 0))
    return pl.pallas_call(
        _fft_s2f_kernel,
        grid=(n1 // kb,),
        in_specs=[pl.BlockSpec((2, 2, kb, n2, c), lambda i: (0, 0, i, 0, 0)),
                  pl.BlockSpec((kb, 2 * n2, 2 * n2), lambda i: (i, 0, 0))],
        out_specs=[spec, spec],
        out_shape=[jax.ShapeDtypeStruct((n1, n2, c), F32)] * 2,
        compiler_params=_cparams(("parallel",)),
        name="hyena_filter_spectrum",
    )(a, g)


def _fft_s2_kernel(a_ref, g_ref, gt_ref, kr_ref, ki_ref, b_ref, r_sc):
    npairs, _, kb, n2, cb = a_ref.shape
    nq = cb // 128
    rp = _skewed(2 * n2)
    for kk in range(kb):
        kr, ki = kr_ref[kk], ki_ref[kk]
        for p in range(npairs):
            x = jnp.dot(g_ref[kk], a_ref[p, :, kk].reshape(2 * n2, cb), preferred_element_type=F32)
            xr, xi = x[:n2], x[n2:]
            y = jnp.concatenate([xr * kr - xi * ki, xr * ki + xi * kr], axis=0).astype(BF16)
            res = jnp.dot(gt_ref[kk], y, preferred_element_type=F32)
            for q in range(nq):
                r_sc[p * nq + q, kk * rp:kk * rp + 2 * n2, :] = res[:, q * 128:(q + 1) * 128]

    for p in range(npairs):
        def gather(row):
            return jnp.concatenate([r_sc[p * nq + q, pl.ds(row, kb, stride=rp), :] for q in range(nq)], axis=1)

        def transpose(m, carry):
            b_ref[p, 0, m] = gather(m).astype(b_ref.dtype)
            b_ref[p, 1, m] = gather(n2 + m).astype(b_ref.dtype)
            return carry

        lax.fori_loop(0, n2, transpose, 0, unroll=8)


def _fft_second_level(a, g, gt, kr, ki):
    p, _, n1, n2, c = a.shape
    kb, cb = 16, 256
    gspec = pl.BlockSpec((kb, 2 * n2, 2 * n2), lambda i, j: (i, 0, 0))
    kspec = pl.BlockSpec((kb, n2, cb), lambda i, j: (i, 0, j))
    return pl.pallas_call(
        _fft_s2_kernel,
        grid=(n1 // kb, c // cb),
        in_specs=[pl.BlockSpec((p, 2, kb, n2, cb), lambda i, j: (0, 0, i, 0, j)), gspec, gspec, kspec, kspec],
        out_specs=pl.BlockSpec((p, 2, n2, kb, cb), lambda i, j: (0, 0, 0, i, j)),
        out_shape=jax.ShapeDtypeStruct((p, 2, n2, n1, c), BF16),
        scratch_shapes=[pltpu.VMEM((p * cb // 128, kb * _skewed(2 * n2), 128), F32)],
        compiler_params=_cparams(("parallel", "parallel")),
        name="hyena_dft_level2",
    )(a, g, gt, kr, ki)


def _fft_s3_kernel(b_ref, f_ref, u_ref, x2_ref, bias_ref, o_ref, t_sc):
    _, _, n2, n1, cb = b_ref.shape
    nr = f_ref.shape[0]
    h1 = nr // 2
    tp = _skewed(nr)

    def inv(gi, carry):
        m0 = gi * FFT_GROUP
        x = jnp.concatenate([b_ref[0, :, m0 + d].reshape(2 * n1, cb) for d in range(FFT_GROUP)], axis=1)
        res = jnp.dot(f_ref[...], x, preferred_element_type=F32)
        for d in range(FFT_GROUP):
            t_sc[pl.ds((m0 + d) * tp, nr), :] = res[:, d * cb:(d + 1) * cb]
        return carry

    lax.fori_loop(0, n2 // FFT_GROUP, inv, 0, unroll=2)

    bias = bias_ref[...]
    for b in range(2):
        def finish(q, carry):
            rows = pl.ds(pl.multiple_of(q * n2, n2), n2)
            conv = t_sc[pl.ds(b * h1 + q, n2, stride=tp), :]
            u = u_ref[b, rows, :].astype(F32)
            o_ref[b, rows, :] = ((conv + u * bias) * x2_ref[b, rows, :].astype(F32)).astype(o_ref.dtype)
            return carry

        lax.fori_loop(0, h1, finish, 0, unroll=4)


def _fft_last_level(bt, f1inv, u, x2c, bias):
    p, _, n2, n1, c = bt.shape
    b, seq_len, _ = u.shape
    cb = 128
    uspec = pl.BlockSpec((2, seq_len, cb), lambda i, j: (i, 0, j))
    return pl.pallas_call(
        _fft_s3_kernel,
        grid=(p, c // cb),
        in_specs=[pl.BlockSpec((1, 2, n2, n1, cb), lambda i, j: (i, 0, 0, 0, j)),
                  pl.BlockSpec(f1inv.shape, lambda i, j: (0, 0)),
                  uspec, uspec,
                  pl.BlockSpec((1, cb), lambda i, j: (0, j))],
        out_specs=uspec,
        out_shape=jax.ShapeDtypeStruct(u.shape, BF16),
        scratch_shapes=[pltpu.VMEM((n2 * _skewed(f1inv.shape[0]), cb), F32)],
        compiler_params=_cparams(("parallel", "parallel")),
        name="hyena_dft_inverse",
    )(bt, f1inv, u, x2c, bias)


def _kf_direct_kernel(h_ref, f_ref, kr_ref, ki_ref):
    n = kr_ref.shape[0]
    dot = functools.partial(jnp.dot, preferred_element_type=F32, precision=HIGHEST)
    hf = dot(f_ref[...], h_ref[0])
    hb = dot(f_ref[...], h_ref[2])
    kr_ref[...] = hf[:n] + hb[:n]
    ki_ref[...] = hf[n:] - hb[n:]


def _hyena_direct_kernel(u_ref, x2_ref, f_ref, fi_ref, kr_ref, ki_ref, bias_ref, o_ref):
    _, two, seq_len, cb = u_ref.shape
    n = kr_ref.shape[0]
    ub = u_ref[0].reshape(two * seq_len, cb)
    x = jnp.dot(f_ref[...], ub, preferred_element_type=F32)
    xr, xi = x[:n], x[n:]
    kr, ki = kr_ref[...], ki_ref[...]
    y = jnp.concatenate([xr * kr - xi * ki, xr * ki + xi * kr], axis=0).astype(BF16)
    conv = jnp.dot(fi_ref[...], y, preferred_element_type=F32)
    x2 = x2_ref[0].reshape(two * seq_len, cb).astype(F32)
    o = (conv + ub.astype(F32) * bias_ref[...]) * x2
    o_ref[0] = o.reshape(two, seq_len, cb).astype(o_ref.dtype)


def _hyena_long_conv(u, x2c, p):
    b, seq_len, _ = u.shape
    c = D_HYENA
    n = 2 * seq_len
    bias = p["hy_bias"].reshape(1, c)
    hh = _hyena_filters(seq_len, p, F32)
    if seq_len <= 512:
        fwd, inv = _stacked_dft(n, seq_len, n, scale=1.0 / n)
        kr, ki = pl.pallas_call(
            _kf_direct_kernel,
            out_shape=[jax.ShapeDtypeStruct((n, c), F32)] * 2,
            compiler_params=_cparams(None),
            name="hyena_filter_spectrum_direct",
        )(hh, fwd[:, :seq_len])
        cb = 512
        pair = lambda a: a.reshape(b // 2, 2, seq_len, c)
        uspec = pl.BlockSpec((1, 2, seq_len, cb), lambda i, j: (i, 0, 0, j))
        kspec = pl.BlockSpec((n, cb), lambda i, j: (0, j))
        out = pl.pallas_call(
            _hyena_direct_kernel,
            grid=(b // 2, c // cb),
            in_specs=[uspec, uspec,
                      pl.BlockSpec(fwd.shape, lambda i, j: (0, 0)),
                      pl.BlockSpec(inv.shape, lambda i, j: (0, 0)),
                      kspec, kspec,
                      pl.BlockSpec((1, cb), lambda i, j: (0, j))],
            out_specs=uspec,
            out_shape=jax.ShapeDtypeStruct((b // 2, 2, seq_len, c), BF16),
            compiler_params=_cparams(("parallel", "parallel")),
            name="hyena_dft_direct",
        )(pair(u), pair(x2c), jnp.asarray(fwd).astype(BF16), jnp.asarray(inv).astype(BF16), kr, ki, bias)
        return out.reshape(b, seq_len, c)

    n2 = FFT_MINOR
    n1 = n // n2
    h1 = seq_len // n2
    f1, f1inv = _stacked_dft(n1, h1, n1, scale=1.0 / n)
    f1, f1inv = jnp.asarray(f1).astype(BF16), jnp.asarray(f1inv).astype(BF16)
    g, gt = _second_level_tables(n, n1, n2)
    g, gt = jnp.asarray(g).astype(BF16), jnp.asarray(gt).astype(BF16)
    ha = _fft_first_level(hh, f1)
    kr, ki = _fft_filter_second_level(ha.reshape(2, 2, n1, n2, c), g)
    a = _fft_first_level(u, f1)
    bt = _fft_second_level(a.reshape(b // 2, 2, n1, n2, c), g, gt, kr, ki)
    return _fft_last_level(bt, f1inv, u, x2c, bias)


def _rope_tables(seq_len):
    half = RET_HEAD_DIM // 2
    nf = half // 2
    t = np.arange(seq_len)
    inv = ROPE_BASE ** (-np.arange(nf, dtype=np.float64) / nf)
    ar = (t // GRID_W)[:, None] * inv[None, :]
    ac = (t % GRID_W)[:, None] * inv[None, :]
    cos = np.concatenate([np.cos(ar), np.cos(ar), np.cos(ac), np.cos(ac)], axis=-1)
    sin = np.concatenate([-np.sin(ar), np.sin(ar), -np.sin(ac), np.sin(ac)], axis=-1)
    return cos.astype(np.float32), sin.astype(np.float32)


def _log_sigmoid(x):
    return jnp.minimum(x, 0.0) - jnp.log1p(jnp.exp(-jnp.abs(x)))


def _retention_kernel(*refs, has_init, emit_state, cpb):
    refs = list(refs)
    q_ref, k_ref, v_ref, g_ref, dec_ref, gn_ref = refs[:6]
    del refs[:6]
    if has_init:
        s0f_ref, s0b_ref = refs[:2]
        del refs[:2]
    o_ref = refs.pop(0)
    if emit_state:
        sf_out, sb_out = refs[:2]
        del refs[:2]
    sf_ref, sb_ref, sball_ref, mask_sc, scale_sc = refs

    c = RET_CHUNK
    dh = RET_HEAD_DIM
    nh = N_RET_HEADS
    phase = pl.program_id(1)
    j = pl.program_id(2)
    nb = pl.num_programs(2)

    diff = (lax.broadcasted_iota(jnp.int32, (c, c), 0) - lax.broadcasted_iota(jnp.int32, (c, c), 1)).astype(F32)
    ri = lax.broadcasted_iota(jnp.int32, (c, dh), 0).astype(F32)

    def head_consts(h):
        lgf = _log_sigmoid(dec_ref[0, h])[0:1, :]
        lgb = _log_sigmoid(dec_ref[1, h])[0:1, :]
        return lgf, lgb

    def chunk_wide(lg):
        return jnp.concatenate([lg] * (c // dh), axis=1)

    bb = q_ref.shape[0]
    heads = [(bi, h) for bi in range(bb) for h in range(nh)]

    def load(ref, bi, r0, h):
        return ref[bi, r0:r0 + c, h * dh:(h + 1) * dh]

    dn_t = (((0,), (0,)), ((), ()))
    dn_nt = (((1,), (1,)), ((), ()))

    @pl.when(jnp.logical_and(pl.program_id(0) == 0, jnp.logical_and(phase == 0, j == 0)))
    def _decay_tables():
        for h in range(nh):
            lgf, lgb = head_consts(h)
            mask_sc[h] = (jnp.where(diff >= 0, jnp.exp(chunk_wide(lgf) * jnp.maximum(diff, 0.0)), 0.0)
                          + jnp.where(diff <= 0, jnp.exp(chunk_wide(lgb) * jnp.maximum(-diff, 0.0)), 0.0))
            scale_sc[h, 0] = jnp.exp(lgb * ri)
            scale_sc[h, 1] = jnp.exp(lgf * (ri + 1.0))
            scale_sc[h, 2] = jnp.exp(lgb * (float(c) - ri))
            scale_sc[h, 3] = jnp.exp(lgf * (float(c - 1) - ri))

    @pl.when(phase == 0)
    def _backward_sweep():
        @pl.when(j == 0)
        def _():
            for bi, h in heads:
                sb_ref[bi * nh + h] = s0b_ref[bi, h] if has_init else jnp.zeros((dh, dh), F32)

        blk = nb - 1 - j
        for bi, h in heads:
            hs = bi * nh + h
            _, lgb = head_consts(h)
            zeta_b = scale_sc[h, 0]
            cdec_b = jnp.exp(lgb * float(c))
            for cc in reversed(range(cpb)):
                r0 = cc * c
                n = blk * cpb + cc
                s = sb_ref[hs]
                sball_ref[n, hs] = s.astype(BF16)
                kz = (load(k_ref, bi, r0, h).astype(F32) * zeta_b).astype(BF16)
                vv = load(v_ref, bi, r0, h)
                sb_ref[hs] = cdec_b * s + lax.dot_general(kz, vv, dn_t, preferred_element_type=F32)

        if emit_state:
            @pl.when(j == nb - 1)
            def _():
                for bi, h in heads:
                    sb_out[bi, h] = sb_ref[bi * nh + h]

    @pl.when(phase == 1)
    def _forward_sweep():
        @pl.when(j == 0)
        def _():
            for bi, h in heads:
                sf_ref[bi * nh + h] = s0f_ref[bi, h] if has_init else jnp.zeros((dh, dh), F32)

        for bi, h in heads:
            hs = bi * nh + h
            lgf, _ = head_consts(h)
            mask = mask_sc[h]
            xi_f, xi_b, zeta_f = scale_sc[h, 1], scale_sc[h, 2], scale_sc[h, 3]
            cdec_f = jnp.exp(lgf * float(c))
            gn = gn_ref[:, h * dh:(h + 1) * dh]
            for cc in range(cpb):
                r0 = cc * c
                n = j * cpb + cc
                qb = load(q_ref, bi, r0, h)
                kb = load(k_ref, bi, r0, h)
                vv = load(v_ref, bi, r0, h)
                gate = load(g_ref, bi, r0, h).astype(F32)
                sc = lax.dot_general(qb, kb, dn_nt, preferred_element_type=F32)
                inner = jnp.dot((sc * mask).astype(BF16), vv, preferred_element_type=F32)
                s = sf_ref[hs]
                q = qb.astype(F32)
                lhs = jnp.concatenate([q * xi_f, q * xi_b], axis=1).astype(BF16)
                rhs = jnp.concatenate([s.astype(BF16), sball_ref[n, hs]], axis=0)
                o = inner + jnp.dot(lhs, rhs, preferred_element_type=F32)
                mu = jnp.mean(o, axis=-1, keepdims=True)
                d = o - mu
                var = jnp.mean(d * d, axis=-1, keepdims=True)
                y = d * lax.rsqrt(var + EPS) * gn * _silu(gate)
                o_ref[bi, r0:r0 + c, h * dh:(h + 1) * dh] = y.astype(o_ref.dtype)
                kz = (kb.astype(F32) * zeta_f).astype(BF16)
                sf_ref[hs] = cdec_f * s + lax.dot_general(kz, vv, dn_t, preferred_element_type=F32)

        if emit_state:
            @pl.when(j == nb - 1)
            def _():
                for bi, h in heads:
                    sf_out[bi, h] = sf_ref[bi * nh + h]


def _retention(qkvg, dec_f, dec_b, gn_g, s0_f, s0_b, emit_state):
    b, seq_len, _ = qkvg.shape
    nh, dh, c = N_RET_HEADS, RET_HEAD_DIM, RET_CHUNK
    rb = min(seq_len, 1024)
    nb = seq_len // rb
    cpb = rb // c
    bb = max(1, min(4, 1024 // seq_len))
    while b % bb:
        bb //= 2
    has_init = s0_f is not None
    dec = jnp.broadcast_to(jnp.stack([dec_f, dec_b])[:, :, None, None], (2, nh, 8, 128)).astype(F32)
    kv_blk = lambda i, p, j: jnp.where(p == 0, nb - 1 - j, j)
    q_blk = lambda i, p, j: jnp.where(p == 0, 0, j)
    in_specs = [pl.BlockSpec((bb, rb, D_RET), lambda i, p, j: (i, q_blk(i, p, j), 0)),
                pl.BlockSpec((bb, rb, D_RET), lambda i, p, j: (i, kv_blk(i, p, j), 1)),
                pl.BlockSpec((bb, rb, D_RET), lambda i, p, j: (i, kv_blk(i, p, j), 2)),
                pl.BlockSpec((bb, rb, D_RET), lambda i, p, j: (i, q_blk(i, p, j), 3)),
                pl.BlockSpec((2, nh, 8, 128), lambda i, p, j: (0, 0, 0, 0)),
                pl.BlockSpec((1, D_RET), lambda i, p, j: (0, 0))]
    args = [qkvg, qkvg, qkvg, qkvg, dec, gn_g.reshape(1, -1)]
    sspec = pl.BlockSpec((bb, nh, dh, dh), lambda i, p, j: (i, 0, 0, 0))
    if has_init:
        in_specs += [sspec, sspec]
        args += [s0_f, s0_b]
    out_specs = [pl.BlockSpec((bb, rb, D_RET), lambda i, p, j: (i, q_blk(i, p, j), 0))]
    out_shape = [jax.ShapeDtypeStruct((b, seq_len, D_RET), BF16)]
    if emit_state:
        out_specs += [sspec, sspec]
        out_shape += [jax.ShapeDtypeStruct((b, nh, dh, dh), F32)] * 2
    return pl.pallas_call(
        functools.partial(_retention_kernel, has_init=has_init, emit_state=emit_state, cpb=cpb),
        grid=(b // bb, 2, nb),
        in_specs=in_specs,
        out_specs=out_specs,
        out_shape=out_shape,
        scratch_shapes=[pltpu.VMEM((bb * nh, dh, dh), F32), pltpu.VMEM((bb * nh, dh, dh), F32),
                        pltpu.VMEM((nb * cpb, bb * nh, dh, dh), BF16),
                        pltpu.VMEM((nh, c, c), F32), pltpu.VMEM((nh, 4, c, dh), F32)],
        compiler_params=_cparams(("arbitrary", "arbitrary", "arbitrary")),
        name="retention",
    )(*args)


def _outproj_kernel(yhy_ref, yret_ref, gates_ref, x_ref, g1_ref, sc_ref, sh_ref, ng_ref,
                    why_ref, wret_ref, wout_ref, wr_ref, x1_ref, h2_ref, aff_ref):
    g, m, d = x_ref.shape
    rows = lambda ref: ref[...].reshape(g * m, ref.shape[2])
    a = jnp.dot(rows(yhy_ref), why_ref[...], preferred_element_type=F32)
    b = jnp.dot(rows(yret_ref), wret_ref[...], preferred_element_type=F32)
    gates = rows(gates_ref)
    merged = gates[:, :d].astype(F32) * a + gates[:, d:].astype(F32) * b
    out = jnp.dot(merged.astype(BF16), wout_ref[...], preferred_element_type=F32)
    x1 = x_ref[...] + g1_ref[...] * out.reshape(g, m, d)
    x1_ref[...] = x1
    ms = jnp.mean(x1 * x1, axis=-1, keepdims=True)
    h = x1 * lax.rsqrt(ms + EPS) * ng_ref[...]
    h = (h * (1.0 + sc_ref[...]) + sh_ref[...]).reshape(g * m, d)
    h_hi = h.astype(BF16)
    h2_ref[...] = h_hi.reshape(g, m, d)
    h_lo = (h - h_hi.astype(F32)).astype(BF16)
    t = jnp.dot(h_hi, wr_ref[...], preferred_element_type=F32)
    logits = t[:, :128] + t[:, 128:] + jnp.dot(h_lo, wr_ref[:, :128], preferred_element_type=F32)
    lane = lax.broadcasted_iota(jnp.int32, logits.shape, 1)
    logits = jnp.where(lane < N_EXPERTS, logits, -jnp.inf)
    e = jnp.exp(logits - jnp.max(logits, axis=-1, keepdims=True))
    aff = e / jnp.sum(e, axis=-1, keepdims=True)
    aff_ref[0] = aff.T[:N_EXPERTS, :]


def _out_projection(y_hy, y_ret, gates, x, g1, sc2, sh2, norm2_g, w_hy_o, w_ret_o, w_out, w_router, cond_block):
    g, s, d = x.shape
    tm = TOKEN_TILE
    m = tm // g
    tok = lambda c: pl.BlockSpec((g, m, c), lambda i: (0, i, 0))
    cond = pl.BlockSpec((g, 1, d), lambda i: (cond_block, 0, 0))
    full = lambda a: pl.BlockSpec(a.shape, lambda i: (0, 0))
    wr = jnp.pad(w_router, ((0, 0), (0, 128 - N_EXPERTS)))
    wr_hi = wr.astype(BF16)
    wr = jnp.concatenate([wr_hi, (wr - wr_hi.astype(F32)).astype(BF16)], axis=1)
    return pl.pallas_call(
        _outproj_kernel,
        grid=(s // m,),
        in_specs=[tok(D_HYENA), tok(D_RET), tok(2 * d), tok(d), cond, cond, cond,
                  pl.BlockSpec((1, d), lambda i: (0, 0)),
                  full(w_hy_o), full(w_ret_o), full(w_out), full(wr)],
        out_specs=[tok(d), tok(d), pl.BlockSpec((1, N_EXPERTS, tm), lambda i: (i, 0, 0))],
        out_shape=[jax.ShapeDtypeStruct((g, s, d), F32), jax.ShapeDtypeStruct((g, s, d), BF16),
                   jax.ShapeDtypeStruct((s // m, N_EXPERTS, tm), F32)],
        compiler_params=_cparams(("parallel",)),
        name="out_projection_router",
    )(y_hy, y_ret, gates, x, g1, sc2, sh2, norm2_g.reshape(1, d), w_hy_o, w_ret_o, w_out, wr)


def _select_kernel(aff_ref, tri_ref, rank_ref, cnt_ref, start_ref, tot_ref, *, cap, idx_bits, groups):
    nt, ne, tm = aff_ref.shape
    a = aff_ref[...]

    def count(m):
        return jnp.sum(jnp.sum(m, axis=0, keepdims=True), axis=2, keepdims=True)

    def thr_step(s, thr):
        cand = thr | (1 << (30 - s))
        cnt = count(jnp.where(a >= pltpu.bitcast(cand, F32), 1.0, 0.0))
        return jnp.where(cnt >= float(cap), cand, thr)

    thr = pltpu.bitcast(lax.fori_loop(0, 31, thr_step, jnp.zeros((1, ne, 1), jnp.int32)), F32)
    gt = a > thr
    eq = a == thr
    need = float(cap) - count(jnp.where(gt, 1.0, 0.0))
    m = tm // groups
    tile = lax.broadcasted_iota(jnp.int32, (nt, 1, tm), 0)
    lane = lax.broadcasted_iota(jnp.int32, (nt, 1, tm), 2)
    idx = (lane // m) * (nt * m) + tile * m + lane % m

    def idx_step(s, lim):
        cand = lim | (1 << (idx_bits - 1 - s))
        cnt = count(jnp.where(eq, jnp.where(idx < cand, 1.0, 0.0), 0.0))
        return jnp.where(cnt < need, cand, lim)

    lim = lax.fori_loop(0, idx_bits, idx_step, jnp.zeros((1, ne, 1), jnp.int32))
    sel = jnp.where(gt, 1.0, jnp.where(eq, jnp.where(idx <= lim, 1.0, 0.0), 0.0))
    sel2 = sel.reshape(nt * ne, tm).astype(BF16)
    prefix = jnp.dot(sel2, tri_ref[...], preferred_element_type=F32)
    rank = jnp.where(sel2 > 0, prefix, -1.0).astype(jnp.int32)
    rank_ref[...] = rank.reshape(nt, ne, tm)
    ones = jnp.ones((tm, 128), BF16)
    cnt = jnp.dot(sel2, ones, preferred_element_type=F32).astype(jnp.int32).reshape(nt, ne, 128)
    cnt_ref[...] = cnt
    acc = jnp.zeros((ne, 128), jnp.int32)
    for t in range(nt):
        start_ref[t] = acc
        acc = acc + ((cnt[t] + (ROW_ALIGN - 1)) & (-ROW_ALIGN))
    tot_ref[...] = acc


def _select(aff, cap, groups):
    nt, ne, tm = aff.shape
    idx_bits = max(1, int(math.ceil(math.log2(nt * tm))))
    r = lax.broadcasted_iota(jnp.int32, (tm, tm), 0)
    c = lax.broadcasted_iota(jnp.int32, (tm, tm), 1)
    tri = (r < c).astype(BF16)
    rank, cnt, start, tot = pl.pallas_call(
        functools.partial(_select_kernel, cap=cap, idx_bits=idx_bits, groups=groups),
        out_shape=[jax.ShapeDtypeStruct((nt, ne, tm), jnp.int32),
                   jax.ShapeDtypeStruct((nt, ne, 128), jnp.int32),
                   jax.ShapeDtypeStruct((nt, ne, 128), jnp.int32),
                   jax.ShapeDtypeStruct((ne, 128), jnp.int32)],
        compiler_params=_cparams(None),
        name="expert_choice_select",
    )(aff, tri)
    return rank, cnt[:, :, 0], start[:, :, 0], tot[:, 0]


def _expert_block(cap):
    return 512 if cap >= 2048 else 256


def _used_rows(cap, nt):
    eb = _expert_block(cap)
    return -(-(cap + ROW_ALIGN * nt) // eb) * eb


def _list_rows(cap, nt):
    max_rounds = -(-TOKEN_TILE // SLOT_ROWS)
    return _used_rows(cap, nt) + max(_expert_block(cap), max_rounds * SLOT_ROWS)


def _num_rounds(cnt_sm, i):
    m = cnt_sm[i, 0]
    for e in range(1, N_EXPERTS):
        m = jnp.maximum(m, cnt_sm[i, e])
    return jnp.maximum((m + SLOT_ROWS - 1) // SLOT_ROWS, 1)


def _gather_kernel(start_sm, cnt_sm, tot_sm, h_ref, rank_ref, xs_hbm, stage, zbuf, sem):
    i = pl.program_id(0)
    ne = N_EXPERTS
    slot = i % 2
    rank = rank_ref[0]
    sub = lax.broadcasted_iota(jnp.int32, (SLOT_ROWS, rank.shape[1]), 0)

    def copy(s, e, off):
        return pltpu.make_async_copy(stage.at[s, pl.ds(e * SLOT_ROWS, SLOT_ROWS)],
                                     xs_hbm.at[e, pl.ds(off, SLOT_ROWS)], sem.at[e])

    def fill(r):
        h = h_ref[...].reshape(rank.shape[1], h_ref.shape[2])
        for e in range(ne):
            onehot = jnp.where(rank[e:e + 1, :] == sub + r * SLOT_ROWS, 1.0, 0.0).astype(BF16)
            stage[slot, e * SLOT_ROWS:(e + 1) * SLOT_ROWS, :] = jnp.dot(
                onehot, h, preferred_element_type=F32).astype(BF16)

    def start_all(r):
        for e in range(ne):
            copy(slot, e, pl.multiple_of(start_sm[i, e] + r * SLOT_ROWS, ROW_ALIGN)).start()

    def wait_all(s):
        for e in range(ne):
            copy(s, e, 0).wait()

    fill(0)

    @pl.when(i > 0)
    def _():
        wait_all(1 - slot)

    start_all(0)

    def extra_round(r, carry):
        wait_all(slot)
        fill(r)
        start_all(r)
        return carry

    lax.fori_loop(1, _num_rounds(cnt_sm, i), extra_round, 0)

    @pl.when(i == pl.num_programs(0) - 1)
    def _zero_tail():
        wait_all(slot)
        zbuf[...] = jnp.zeros(zbuf.shape, zbuf.dtype)

        def zcopy(e):
            off = pl.multiple_of(tot_sm[e], ROW_ALIGN)
            return pltpu.make_async_copy(zbuf, xs_hbm.at[e, pl.ds(off, zbuf.shape[0])], sem.at[e])

        for e in range(ne):
            zcopy(e).start()
        for e in range(ne):
            zcopy(e).wait()


def _gather(h2, rank, start, cnt, tot, cap):
    g, _, d = h2.shape
    nt, ne, tm = rank.shape
    rl = _list_rows(cap, nt)
    return pl.pallas_call(
        _gather_kernel,
        grid_spec=pltpu.PrefetchScalarGridSpec(
            num_scalar_prefetch=3,
            grid=(nt,),
            in_specs=[pl.BlockSpec((g, tm // g, d), lambda i, *_: (0, i, 0)),
                      pl.BlockSpec((1, ne, tm), lambda i, *_: (i, 0, 0))],
            out_specs=pl.BlockSpec(memory_space=pl.ANY),
            scratch_shapes=[pltpu.VMEM((2, ne * SLOT_ROWS, d), BF16),
                            pltpu.VMEM((_expert_block(cap), d), BF16),
                            pltpu.SemaphoreType.DMA((ne,))]),
        out_shape=jax.ShapeDtypeStruct((ne, rl, d), BF16),
        compiler_params=_cparams(("arbitrary",)),
        name="expert_gather",
    )(start, cnt, tot, h2, rank)


def _expert_kernel(*refs, nblocks):
    ns = len(nblocks)
    tot_sms, xs_refs = refs[:ns], refs[ns:2 * ns]
    wg_ref, wu_ref, wd_ref = refs[2 * ns:2 * ns + 3]
    ye_refs = refs[2 * ns + 3:3 * ns + 3]
    wg_bf, wu_bf, wd_bf = refs[3 * ns + 3:]
    e = pl.program_id(0)
    j = pl.program_id(1)

    @pl.when(j == 0)
    def _():
        wg_bf[...] = wg_ref[0].astype(BF16)
        wu_bf[...] = wu_ref[0].astype(BF16)
        wd_bf[...] = wd_ref[0].astype(BF16)

    base = 0
    for s in range(ns):
        jj = j - base
        live = jnp.logical_and(jnp.logical_and(jj >= 0, jj < nblocks[s]), jj * xs_refs[s].shape[1] < tot_sms[s][e])

        @pl.when(live)
        def _(xs_ref=xs_refs[s], ye_ref=ye_refs[s]):
            x = xs_ref[0]
            g = jnp.dot(x, wg_bf[...], preferred_element_type=F32)
            u = jnp.dot(x, wu_bf[...], preferred_element_type=F32)
            hid = (_silu(g) * u).astype(BF16)
            ye_ref[0] = jnp.dot(hid, wd_bf[...], preferred_element_type=F32).astype(ye_ref.dtype)

        base += nblocks[s]


def _experts(sets, w_gate, w_up, w_down):
    ns = len(sets)
    ne, _, d = sets[0][0].shape
    f = w_gate.shape[2]
    nblocks = tuple(used // eb for _, _, used, eb in sets)

    def block_spec(s):
        eb = sets[s][3]
        base = sum(nblocks[:s])

        def index(e, j, *tots):
            return (e, jnp.clip(j - base, 0, (tots[s][e] - 1) // eb), 0)

        return pl.BlockSpec((1, eb, d), index)

    wspec = lambda shape: pl.BlockSpec(shape, lambda e, j, *tots: (e, 0, 0))
    return pl.pallas_call(
        functools.partial(_expert_kernel, nblocks=nblocks),
        grid_spec=pltpu.PrefetchScalarGridSpec(
            num_scalar_prefetch=ns,
            grid=(ne, sum(nblocks)),
            in_specs=[block_spec(s) for s in range(ns)] + [wspec((1, d, f)), wspec((1, d, f)), wspec((1, f, d))],
            out_specs=[block_spec(s) for s in range(ns)],
            scratch_shapes=[pltpu.VMEM((d, f), BF16), pltpu.VMEM((d, f), BF16), pltpu.VMEM((f, d), BF16)]),
        out_shape=[jax.ShapeDtypeStruct((ne, used, d), BF16) for _, _, used, _ in sets],
        compiler_params=_cparams(("arbitrary", "arbitrary")),
        name="expert_ffn",
    )(*[t for _, t, _, _ in sets], *[x for x, _, _, _ in sets], w_gate, w_up, w_down)


def _combine_kernel(start_sm, cnt_sm, tot_sm, x1_ref, rank_ref, aff_ref, g2_ref, nf_ref, ye_hbm, o_ref, buf, sem, *,
                    final_norm, eb):
    i = pl.program_id(0)
    nt = pl.num_programs(0)
    ne = N_EXPERTS
    slot = i % 2
    rank = rank_ref[0]
    aff = aff_ref[0]
    sub = lax.broadcasted_iota(jnp.int32, (SLOT_ROWS, rank.shape[1]), 0)
    dn_t = (((0,), (0,)), ((), ()))

    def window(t, e, r):
        want = start_sm[t, e] + r * SLOT_ROWS
        written = ((tot_sm[e] + eb - 1) // eb) * eb
        off = jnp.minimum(want, written - SLOT_ROWS)
        return pl.multiple_of(off, ROW_ALIGN), want - off

    def copy(t, s, e, r):
        off, _ = window(t, e, r)
        return pltpu.make_async_copy(ye_hbm.at[e, pl.ds(off, SLOT_ROWS)],
                                     buf.at[s, pl.ds(e * SLOT_ROWS, SLOT_ROWS)], sem.at[s, e])

    def weighted(r):
        parts = []
        for e in range(ne):
            _, shift = window(i, e, r)
            local = rank[e:e + 1, :] - r * SLOT_ROWS
            hit = jnp.where(local >= 0, local + shift, -1) == sub
            parts.append(jnp.where(hit, aff[e:e + 1, :], 0.0).astype(BF16))
        return lax.dot_general(jnp.concatenate(parts, axis=0), buf[slot], dn_t, preferred_element_type=F32)

    @pl.when(i == 0)
    def _():
        for e in range(ne):
            copy(0, 0, e, 0).start()

    @pl.when(i + 1 < nt)
    def _prefetch_next_tile():
        for e in range(ne):
            copy(i + 1, 1 - slot, e, 0).start()

    for e in range(ne):
        copy(i, slot, e, 0).wait()
    y0 = weighted(0)

    def extra_round(r, acc):
        for e in range(ne):
            @pl.when(cnt_sm[i, e] > r * SLOT_ROWS)
            def _():
                copy(i, slot, e, r).start()
        for e in range(ne):
            @pl.when(cnt_sm[i, e] > r * SLOT_ROWS)
            def _():
                copy(i, slot, e, r).wait()
        return acc + weighted(r)

    y = lax.fori_loop(1, _num_rounds(cnt_sm, i), extra_round, y0)
    x2 = x1_ref[...] + g2_ref[...] * y.reshape(x1_ref.shape)
    if final_norm:
        ms = jnp.mean(x2 * x2, axis=-1, keepdims=True)
        x2 = x2 * lax.rsqrt(ms + EPS) * nf_ref[...]
    o_ref[...] = x2


def _combine(x1, rank, aff, start, cnt, tot, g2, norm_f_g, ye, cond_block, eb):
    g, s, d = x1.shape
    nt, ne, tm = rank.shape
    tok = pl.BlockSpec((g, tm // g, d), lambda i, *_: (0, i, 0))
    final_norm = norm_f_g is not None
    if not final_norm:
        norm_f_g = jnp.ones((d,), F32)
    return pl.pallas_call(
        functools.partial(_combine_kernel, final_norm=final_norm, eb=eb),
        grid_spec=pltpu.PrefetchScalarGridSpec(
            num_scalar_prefetch=3,
            grid=(nt,),
            in_specs=[tok,
                      pl.BlockSpec((1, ne, tm), lambda i, *_: (i, 0, 0)),
                      pl.BlockSpec((1, ne, tm), lambda i, *_: (i, 0, 0)),
                      pl.BlockSpec((g, 1, d), lambda i, *_: (cond_block, 0, 0)),
                      pl.BlockSpec((1, d), lambda i, *_: (0, 0)),
                      pl.BlockSpec(memory_space=pl.ANY)],
            out_specs=tok,
            scratch_shapes=[pltpu.VMEM((2, ne * SLOT_ROWS, d), BF16),
                            pltpu.SemaphoreType.DMA((2, ne))]),
        out_shape=jax.ShapeDtypeStruct((g, s, d), F32),
        compiler_params=_cparams(("arbitrary",)),
        name="expert_combine",
    )(start, cnt, tot, x1, rank, aff, g2, norm_f_g.reshape(1, d), ye)


def _layer_front(x, batch, seq_len, mods, cond_of_tile, moe_groups, cond_block, p, s0_f, s0_b, rope, emit_state):
    t, d = x.shape
    sh1, sc1, g1, sh2, sc2, g2 = mods
    u, x2c, qkvg, gates = _in_projection(x, p["norm1_g"], sc1, sh1, p["w_in"], p["hy_conv_w"], p["hy_conv_b"],
                                         cond_of_tile, seq_len, rope)
    seq = lambda a: a.reshape(batch, seq_len, a.shape[-1])
    y_hy = _hyena_long_conv(seq(u), seq(x2c), p).reshape(t, -1)
    ret = _retention(qkvg.reshape(batch, seq_len, -1), p["ret_decay_fwd"], p["ret_decay_bwd"], p["ret_gn_g"],
                     s0_f, s0_b, emit_state)
    y_ret = ret[0].reshape(t, -1)
    view = lambda a: a.reshape(moe_groups, t // moe_groups, a.shape[-1])
    x1, h2, aff = _out_projection(view(y_hy), view(y_ret), view(gates), view(x), g1, sc2, sh2, p["norm2_g"],
                                  p["w_hy_o"], p["w_ret_o"], p["w_out"], p["w_router"], cond_block)
    cap = (EC_CAPACITY_FACTOR * t) // N_EXPERTS
    eb = _expert_block(cap)
    rank, cnt, start, tot = _select(aff, cap, moe_groups)
    xs = _gather(h2, rank, start, cnt, tot, cap)
    routed = (xs, tot, _used_rows(cap, rank.shape[0]), eb)
    back = dict(x1=x1, rank=rank, aff=aff, start=start, cnt=cnt, tot=tot, g2=g2, cond_block=cond_block, eb=eb)
    return routed, back, ret[1:]


def _layer_back(back, ye, final_g):
    out = _combine(back["x1"], back["rank"], back["aff"], back["start"], back["cnt"], back["tot"], back["g2"],
                   final_g, ye, back["cond_block"], back["eb"])
    return out.reshape(-1, out.shape[-1])


def kernel(x_prompt, x_sample, state_ret_fwd, state_ret_bwd, c, c_ctx, w_mod, b_mod, norm1_g, w_in, hy_conv_w, hy_conv_b, hy_f_w1, hy_f_b1, hy_f_freq, hy_f_w2, hy_f_b2, hy_f_w3, hy_decay, hy_bias, ret_decay_fwd, ret_decay_bwd, ret_gn_g, w_hy_o, w_ret_o, w_out, norm2_g, w_router, w_e_gate, w_e_up, w_e_down, norm_f_g):
    b, s, d = x_prompt.shape
    bd, sd, _ = x_sample.shape
    depth = w_mod.shape[0]
    assert (b * s) % TOKEN_TILE == 0 and sd % TOKEN_TILE == 0 and b % 2 == 0 and bd % 2 == 0
    ncond = -(-(bd + 1) // 8) * 8
    lat_groups = bd if (TOKEN_TILE % (16 * bd) == 0 and sd % (TOKEN_TILE // bd) == 0) else 1
    assert lat_groups == bd or bd == 1
    conds = jnp.concatenate([c, c_ctx[None], jnp.zeros((ncond - bd - 1, d), F32)], axis=0)
    ctx_cond = lambda i: bd
    lat_cond = lambda i: i // (sd // TOKEN_TILE)
    xp = x_prompt.reshape(b * s, d)
    xs = x_sample.reshape(bd * sd, d)
    new_f, new_b = [], []
    for l in range(depth):
        mod = _modulation(conds, w_mod[l], b_mod[l])
        mods = [mod[:, k * d:(k + 1) * d].reshape(ncond, 1, d) for k in range(6)]
        p = {"norm1_g": norm1_g[l], "w_in": w_in[l].astype(BF16), "hy_conv_w": hy_conv_w[l],
             "hy_conv_b": hy_conv_b[l], "hy_f_w1": hy_f_w1[l], "hy_f_b1": hy_f_b1[l], "hy_f_freq": hy_f_freq[l],
             "hy_f_w2": hy_f_w2[l], "hy_f_b2": hy_f_b2[l], "hy_f_w3": hy_f_w3[l], "hy_decay": hy_decay[l],
             "hy_bias": hy_bias[l], "ret_decay_fwd": ret_decay_fwd[l], "ret_decay_bwd": ret_decay_bwd[l],
             "ret_gn_g": ret_gn_g[l], "w_hy_o": w_hy_o[l].astype(BF16), "w_ret_o": w_ret_o[l].astype(BF16),
             "w_out": w_out[l].astype(BF16), "norm2_g": norm2_g[l], "w_router": w_router[l],
             "w_e_gate": w_e_gate[l], "w_e_up": w_e_up[l], "w_e_down": w_e_down[l]}
        final_g = norm_f_g if l == depth - 1 else None
        routed_p, back_p, (s_f, s_b) = _layer_front(xp, b, s, mods, ctx_cond, 1, bd, p, None, None, False, True)
        new_f.append(s_f)
        new_b.append(s_b)
        routed_s, back_s, _ = _layer_front(xs, bd, sd, mods, lat_cond, lat_groups, 0, p, state_ret_fwd[:, l],
                                           state_ret_bwd[:, l], True, False)
        ye_p, ye_s = _experts([routed_p, routed_s], p["w_e_gate"], p["w_e_up"], p["w_e_down"])
        xp = _layer_back(back_p, ye_p, final_g)
        xs = _layer_back(back_s, ye_s, final_g)
    y_prompt = xp.reshape(b, s, d)
    y_sample = xs.reshape(bd, sd, d)
    return (y_prompt, y_sample, jnp.stack(new_f, axis=1), jnp.stack(new_b, axis=1))
```

```python
import functools
import math

import numpy as np
import jax
import jax.numpy as jnp
from jax import lax
from jax.experimental import pallas as pl
from jax.experimental.pallas import tpu as pltpu

F32 = jnp.float32
BF16 = jnp.bfloat16
HIGHEST = lax.Precision.HIGHEST

EPS = 1e-6
D_HYENA = 512
D_RET = 512
N_RET_HEADS = 4
RET_HEAD_DIM = 128
RET_CHUNK = 256
GRID_W = 64
FILTER_EMB = 33
ROPE_BASE = 10000.0
N_EXPERTS = 16
EC_CAPACITY_FACTOR = 2

TOKEN_TILE = 512
SLOT_ROWS = 96
ROW_ALIGN = 16
HALO_ROWS = 16
FFT_MINOR = 64
FFT_GROUP = 16
VMEM_LIMIT = 56 * 1024 * 1024


def _cparams(sem):
    return pltpu.CompilerParams(dimension_semantics=sem, vmem_limit_bytes=VMEM_LIMIT)


def _silu(x):
    return x * jax.nn.sigmoid(x)


def _mod_kernel(c_ref, w_ref, b_ref, o_ref):
    s = _silu(c_ref[...])
    o_ref[...] = jnp.dot(s, w_ref[...], preferred_element_type=F32, precision=HIGHEST) + b_ref[...]


def _modulation(conds, w_mod, b_mod):
    nc, d = conds.shape
    return pl.pallas_call(
        _mod_kernel,
        grid=(6,),
        in_specs=[pl.BlockSpec((nc, d), lambda j: (0, 0)),
                  pl.BlockSpec((d, d), lambda j: (0, j)),
                  pl.BlockSpec((1, d), lambda j: (0, j))],
        out_specs=pl.BlockSpec((nc, d), lambda j: (0, j)),
        out_shape=jax.ShapeDtypeStruct((nc, 6 * d), F32),
        compiler_params=_cparams(("arbitrary",)),
        name="modulation",
    )(conds, w_mod, b_mod.reshape(1, -1))


def _inproj_kernel(*refs, splits, rope, seq_len, halo):
    refs = list(refs)
    x_ref = refs.pop(0)
    if halo:
        xp_ref, xn_ref = refs.pop(0), refs.pop(0)
    g_ref, sc_ref, sh_ref, w_ref, cw_ref, cb_ref = refs[:6]
    del refs[:6]
    if rope:
        cos_ref, sin_ref = refs.pop(0), refs.pop(0)
    u_ref, x2c_ref, qkvg_ref, gates_ref, cv_sc, cx_sc = refs

    def normed(ref):
        x = ref[...]
        ms = jnp.mean(x * x, axis=-1, keepdims=True)
        h = x * lax.rsqrt(ms + EPS) * g_ref[...]
        return (h * (1.0 + sc_ref[0]) + sh_ref[0]).astype(BF16)

    hb = normed(x_ref)
    tm = hb.shape[0]
    nz, nq, ng = splits
    cw = 512
    dh = RET_HEAD_DIM
    hr = HALO_ROWS
    if halo:
        t0 = pl.program_id(0) * tm
        hp = jnp.where(t0 % seq_len != 0, normed(xp_ref), jnp.zeros((hr, hb.shape[1]), BF16))
        hn = jnp.where((t0 + tm) % seq_len != 0, normed(xn_ref), jnp.zeros((hr, hb.shape[1]), BF16))
        hext = jnp.concatenate([hp, hb, hn], axis=0)

    piece = tm if halo else seq_len
    pitch = piece + hr
    if not halo:
        for s in range(tm // piece + 1):
            cx_sc[s * pitch:s * pitch + hr, :] = jnp.zeros((hr, cw), F32)

    def conv(c0):
        taps = cw_ref[:, c0:c0 + cw]
        bias = cb_ref[:, c0:c0 + cw]
        if halo:
            cx_sc[...] = jnp.dot(hext, w_ref[:, c0:c0 + cw], preferred_element_type=F32)
        else:
            acc = jnp.dot(hb, w_ref[:, c0:c0 + cw], preferred_element_type=F32)
            for s in range(tm // piece):
                cx_sc[hr + s * pitch:hr + s * pitch + piece, :] = acc[s * piece:(s + 1) * piece]
        outs = []
        for s in range(tm // piece):
            r0 = hr + s * pitch
            outs.append(bias + cx_sc[r0 - 1:r0 - 1 + piece, :] * taps[0:1] + cx_sc[r0:r0 + piece, :] * taps[1:2]
                        + cx_sc[r0 + 1:r0 + 1 + piece, :] * taps[2:3])
        return outs[0] if len(outs) == 1 else jnp.concatenate(outs, axis=0)

    cv_sc[...] = conv(0)
    u_ref[...] = (cv_sc[...] * conv(D_HYENA)).astype(BF16)
    x2c_ref[...] = conv(2 * D_HYENA).astype(BF16)

    for c0 in range(nz, nz + nq + ng, cw):
        acc = jnp.dot(hb, w_ref[:, c0:c0 + cw], preferred_element_type=F32)
        if c0 < nz + nq:
            part = (c0 - nz) // D_RET
            if part == 1:
                acc = acc * (dh ** -0.5)
            if part == 3:
                acc = _silu(acc)
            if rope and part < 2:
                lane = lax.broadcasted_iota(jnp.int32, (acc.shape[0], dh), 1)
                swap_hi = (lane % (dh // 2)) < (dh // 4)
                cs, sn = cos_ref[...], sin_ref[...]
                heads = []
                for hh in range(cw // dh):
                    xh = acc[:, hh * dh:(hh + 1) * dh]
                    rot = jnp.where(swap_hi, pltpu.roll(xh, dh - dh // 4, axis=1), pltpu.roll(xh, dh // 4, axis=1))
                    heads.append(xh * cs + rot * sn)
                acc = jnp.concatenate(heads, axis=1)
            qkvg_ref[:, c0 - nz:c0 - nz + cw] = acc.astype(BF16)
        else:
            gates_ref[:, c0 - nz - nq:c0 - nz - nq + cw] = jax.nn.sigmoid(acc).astype(BF16)


def _in_projection(x, norm_g, sc, sh, w_in_bf, conv_w, conv_b, cond_of_tile, seq_len, rope):
    t, d = x.shape
    splits = (3 * D_HYENA, 4 * D_RET, 2 * d)
    assert D_RET == 512 and D_HYENA == 512
    tm = TOKEN_TILE
    assert seq_len % tm == 0 or tm % seq_len == 0
    halo = seq_len > tm
    row = lambda i: (i, 0)
    cond = lambda i: (cond_of_tile(i), 0, 0)
    full = lambda a: pl.BlockSpec(a.shape, lambda i: (0, 0))
    args, specs = [x], [pl.BlockSpec((tm, d), row)]
    if halo:
        hb_per_tile = tm // HALO_ROWS
        last = t // HALO_ROWS - 1
        args += [x, x]
        specs += [pl.BlockSpec((HALO_ROWS, d), lambda i: (jnp.maximum(i * hb_per_tile - 1, 0), 0)),
                  pl.BlockSpec((HALO_ROWS, d), lambda i: (jnp.minimum((i + 1) * hb_per_tile, last), 0))]
    cbias = conv_b.reshape(1, -1)
    args += [norm_g.reshape(1, d), sc, sh, w_in_bf, conv_w, cbias]
    specs += [pl.BlockSpec((1, d), lambda i: (0, 0)), pl.BlockSpec((1, 1, d), cond), pl.BlockSpec((1, 1, d), cond),
              full(w_in_bf), full(conv_w), full(cbias)]
    if rope:
        tiles_per_seq = seq_len // tm
        args += list(_rope_tables(seq_len))
        specs += [pl.BlockSpec((tm, RET_HEAD_DIM), lambda i: (i % tiles_per_seq, 0))] * 2
    return pl.pallas_call(
        functools.partial(_inproj_kernel, splits=splits, rope=rope, seq_len=seq_len, halo=halo),
        grid=(t // tm,),
        in_specs=specs,
        out_specs=[pl.BlockSpec((tm, D_HYENA), row),
                   pl.BlockSpec((tm, D_HYENA), row),
                   pl.BlockSpec((tm, splits[1]), row),
                   pl.BlockSpec((tm, splits[2]), row)],
        out_shape=[jax.ShapeDtypeStruct((t, D_HYENA), BF16),
                   jax.ShapeDtypeStruct((t, D_HYENA), BF16),
                   jax.ShapeDtypeStruct((t, splits[1]), BF16),
                   jax.ShapeDtypeStruct((t, splits[2]), BF16)],
        scratch_shapes=[pltpu.VMEM((tm, 512), F32),
                        pltpu.VMEM((tm + 2 * HALO_ROWS if halo else (tm // seq_len) * (seq_len + HALO_ROWS) + HALO_ROWS,
                                    512), F32)],
        compiler_params=_cparams(("parallel",)),
        name="in_projection",
    )(*args)


def _filter_features(seq_len):
    t = np.linspace(0.0, 1.0, seq_len, dtype=np.float32)[:, None]
    bands = (FILTER_EMB - 1) // 2
    w = (np.float32(2.0 * math.pi) * np.arange(seq_len, dtype=np.float32)) / np.float32(seq_len)
    f = np.linspace(1e-4, bands - 1, bands, dtype=np.float32)
    ang = (w[:, None] * f[None, :]).astype(np.float64)
    z = np.concatenate([t, np.cos(ang), -np.sin(ang)], axis=-1).astype(np.float32)
    return np.pad(z, ((0, 0), (0, 128 - FILTER_EMB)))


def _filter_kernel(z_ref, w1_ref, b1_ref, fr_ref, w2_ref, b2_ref, w3_ref, dec_ref, o_ref):
    z = z_ref[...]
    fr = fr_ref[...]
    dot = functools.partial(jnp.dot, preferred_element_type=F32, precision=HIGHEST)
    h = jnp.sin(fr * (dot(z, w1_ref[...]) + b1_ref[...]))
    h = jnp.sin(fr * (dot(h, w2_ref[...]) + b2_ref[...]))
    split = lambda a: (a.astype(BF16), (a - a.astype(BF16).astype(F32)).astype(BF16))
    (h_hi, h_lo), (w_hi, w_lo) = split(h), split(w3_ref[...])
    bdot = functools.partial(jnp.dot, preferred_element_type=F32)
    h = (bdot(h_hi, w_hi) + bdot(h_hi, w_lo) + bdot(h_lo, w_hi)) * jnp.exp(-z[:, 0:1] * jnp.abs(dec_ref[...]))
    rows = h.shape[0]
    grow = pl.program_id(0) * rows + lax.broadcasted_iota(jnp.int32, (rows, 1), 0)
    o_ref[0] = h[:, :D_HYENA].astype(o_ref.dtype)
    o_ref[1] = jnp.zeros((rows, D_HYENA), o_ref.dtype)
    o_ref[2] = jnp.where(grow == 0, 0.0, h[:, D_HYENA:]).astype(o_ref.dtype)
    o_ref[3] = jnp.zeros((rows, D_HYENA), o_ref.dtype)


def _hyena_filters(seq_len, p, out_dtype):
    z = jnp.asarray(_filter_features(seq_len))
    w1 = jnp.pad(p["hy_f_w1"], ((0, 128 - FILTER_EMB), (0, 0)))
    rows = min(seq_len, 512)
    full = lambda a: pl.BlockSpec(a.shape, lambda i: (0,) * a.ndim)
    ops = [w1, p["hy_f_b1"].reshape(1, -1), p["hy_f_freq"].reshape(1, -1), p["hy_f_w2"],
           p["hy_f_b2"].reshape(1, -1), p["hy_f_w3"], p["hy_decay"].reshape(1, -1)]
    return pl.pallas_call(
        _filter_kernel,
        grid=(seq_len // rows,),
        in_specs=[pl.BlockSpec((rows, 128), lambda i: (i, 0))] + [full(a) for a in ops],
        out_specs=pl.BlockSpec((4, rows, D_HYENA), lambda i: (0, i, 0)),
        out_shape=jax.ShapeDtypeStruct((4, seq_len, D_HYENA), out_dtype),
        compiler_params=_cparams(("arbitrary",)),
        name="hyena_filter",
    )(z, *ops)


def _stacked_dft(n_out, n_in, modulus, scale=1.0):
    k = np.arange(n_out, dtype=np.int64)[:, None]
    n = np.arange(n_in, dtype=np.int64)[None, :]
    th = ((k * n) % modulus) * (2.0 * math.pi / modulus)
    c, s = np.cos(th), np.sin(th)
    fwd = np.block([[c, s], [-s, c]]).astype(np.float32)
    inv = (np.block([[c.T, -s.T], [s.T, c.T]]) * scale).astype(np.float32)
    return fwd, inv


def _second_level_tables(n_total, n1, n2):
    k1 = np.arange(n1, dtype=np.int64)[:, None, None]
    k2 = np.arange(n2, dtype=np.int64)[None, :, None]
    m = np.arange(n2, dtype=np.int64)[None, None, :]
    th = ((m * (k1 + n1 * k2)) % n_total) * (2.0 * math.pi / n_total)
    c, s = np.cos(th), np.sin(th)
    g = np.concatenate([np.concatenate([c, s], axis=2), np.concatenate([-s, c], axis=2)], axis=1)
    return g.astype(np.float32), np.swapaxes(g, 1, 2).astype(np.float32)


def _skewed(n):
    return n + 1


def _fft1_kernel(h_ref, f_ref, a_ref, u_sc, t_sc):
    n2 = FFT_MINOR
    _, seq_len, cb = h_ref.shape
    h1 = seq_len // n2
    nk = a_ref.shape[1]
    up, tp = _skewed(n2), _skewed(nk)
    for b in range(2):
        for q in range(h1):
            u_sc[b, q * up:q * up + n2, :] = h_ref[b, q * n2:(q + 1) * n2, :].astype(F32)

    def fwd(gi, carry):
        m0 = gi * FFT_GROUP
        cols = []
        for d in range(FFT_GROUP):
            xr = u_sc[0, pl.ds(m0 + d, h1, stride=up), :]
            xi = u_sc[1, pl.ds(m0 + d, h1, stride=up), :]
            cols.append(jnp.concatenate([xr, xi], axis=0))
        x = jnp.concatenate(cols, axis=1).astype(BF16)
        res = jnp.dot(f_ref[...], x, preferred_element_type=F32)
        for d in range(FFT_GROUP):
            t_sc[pl.ds((m0 + d) * tp, nk), :] = res[:, d * cb:(d + 1) * cb]
        return carry

    lax.fori_loop(0, n2 // FFT_GROUP, fwd, 0, unroll=2)

    def transpose(k, carry):
        a_ref[0, k] = t_sc[pl.ds(k, n2, stride=tp), :].astype(a_ref.dtype)
        return carry

    lax.fori_loop(0, nk, transpose, 0, unroll=8)


def _fft_first_level(h, f1):
    b, seq_len, c = h.shape
    cb = 128
    return pl.pallas_call(
        _fft1_kernel,
        grid=(b // 2, c // cb),
        in_specs=[pl.BlockSpec((2, seq_len, cb), lambda i, j: (i, 0, j)),
                  pl.BlockSpec(f1.shape, lambda i, j: (0, 0))],
        out_specs=pl.BlockSpec((1, f1.shape[0], FFT_MINOR, cb), lambda i, j: (i, 0, 0, j)),
        out_shape=jax.ShapeDtypeStruct((b // 2, f1.shape[0], FFT_MINOR, c), BF16),
        scratch_shapes=[pltpu.VMEM((2, (seq_len // FFT_MINOR) * _skewed(FFT_MINOR), cb), F32),
                        pltpu.VMEM((FFT_MINOR * _skewed(f1.shape[0]), cb), F32)],
        compiler_params=_cparams(("parallel", "parallel")),
        name="hyena_dft_level1",
    )(h, f1)


def _fft_s2f_kernel(a_ref, g_ref, kr_ref, ki_ref):
    _, _, kb, n2, c = a_ref.shape
    for kk in range(kb):
        g = g_ref[kk]
        hf = jnp.dot(g, a_ref[0, :, kk].reshape(2 * n2, c), preferred_element_type=F32)
        hb = jnp.dot(g, a_ref[1, :, kk].reshape(2 * n2, c), preferred_element_type=F32)
        kr_ref[kk] = hf[:n2] + hb[:n2]
        ki_ref[kk] = hf[n2:] - hb[n2:]


def _fft_filter_second_level(a, g, kb=16):
    _, _, n1, n2, c = a.shape
    spec = pl.BlockSpec((kb, n2, c), lambda i: (i, 0, 0))
    return pl.pallas_call(
        _fft_s2f_kernel,
        grid=(n1 // kb,),
        in_specs=[pl.BlockSpec((2, 2, kb, n2, c), lambda i: (0, 0, i, 0, 0)),
                  pl.BlockSpec((kb, 2 * n2, 2 * n2), lambda i: (i, 0, 0))],
        out_specs=[spec, spec],
        out_shape=[jax.ShapeDtypeStruct((n1, n2, c), F32)] * 2,
        compiler_params=_cparams(("parallel",)),
        name="hyena_filter_spectrum",
    )(a, g)


def _fft_s2_kernel(a_ref, g_ref, gt_ref, kr_ref, ki_ref, b_ref, r_sc):
    npairs, _, kb, n2, cb = a_ref.shape
    nq = cb // 128
    rp = _skewed(2 * n2)
    for kk in range(kb):
        kr, ki = kr_ref[kk], ki_ref[kk]
        for p in range(npairs):
            x = jnp.dot(g_ref[kk], a_ref[p, :, kk].reshape(2 * n2, cb), preferred_element_type=F32)
            xr, xi = x[:n2], x[n2:]
            y = jnp.concatenate([xr * kr - xi * ki, xr * ki + xi * kr], axis=0).astype(BF16)
            res = jnp.dot(gt_ref[kk], y, preferred_element_type=F32)
            for q in range(nq):
                r_sc[p * nq + q, kk * rp:kk * rp + 2 * n2, :] = res[:, q * 128:(q + 1) * 128]

    for p in range(npairs):
        def gather(row):
            return jnp.concatenate([r_sc[p * nq + q, pl.ds(row, kb, stride=rp), :] for q in range(nq)], axis=1)

        def transpose(m, carry):
            b_ref[p, 0, m] = gather(m).astype(b_ref.dtype)
            b_ref[p, 1, m] = gather(n2 + m).astype(b_ref.dtype)
            return carry

        lax.fori_loop(0, n2, transpose, 0, unroll=8)


def _fft_second_level(a, g, gt, kr, ki):
    p, _, n1, n2, c = a.shape
    kb, cb = 16, 256
    gspec = pl.BlockSpec((kb, 2 * n2, 2 * n2), lambda i, j: (i, 0, 0))
    kspec = pl.BlockSpec((kb, n2, cb), lambda i, j: (i, 0, j))
    return pl.pallas_call(
        _fft_s2_kernel,
        grid=(n1 // kb, c // cb),
        in_specs=[pl.BlockSpec((p, 2, kb, n2, cb), lambda i, j: (0, 0, i, 0, j)), gspec, gspec, kspec, kspec],
        out_specs=pl.BlockSpec((p, 2, n2, kb, cb), lambda i, j: (0, 0, 0, i, j)),
        out_shape=jax.ShapeDtypeStruct((p, 2, n2, n1, c), BF16),
        scratch_shapes=[pltpu.VMEM((p * cb // 128, kb * _skewed(2 * n2), 128), F32)],
        compiler_params=_cparams(("parallel", "parallel")),
        name="hyena_dft_level2",
    )(a, g, gt, kr, ki)


def _fft_s3_kernel(b_ref, f_ref, u_ref, x2_ref, bias_ref, o_ref, t_sc):
    _, _, n2, n1, cb = b_ref.shape
    nr = f_ref.shape[0]
    h1 = nr // 2
    tp = _skewed(nr)

    def inv(gi, carry):
        m0 = gi * FFT_GROUP
        x = jnp.concatenate([b_ref[0, :, m0 + d].reshape(2 * n1, cb) for d in range(FFT_GROUP)], axis=1)
        res = jnp.dot(f_ref[...], x, preferred_element_type=F32)
        for d in range(FFT_GROUP):
            t_sc[pl.ds((m0 + d) * tp, nr), :] = res[:, d * cb:(d + 1) * cb]
        return carry

    lax.fori_loop(0, n2 // FFT_GROUP, inv, 0, unroll=2)

    bias = bias_ref[...]
    for b in range(2):
        def finish(q, carry):
            rows = pl.ds(pl.multiple_of(q * n2, n2), n2)
            conv = t_sc[pl.ds(b * h1 + q, n2, stride=tp), :]
            u = u_ref[b, rows, :].astype(F32)
            o_ref[b, rows, :] = ((conv + u * bias) * x2_ref[b, rows, :].astype(F32)).astype(o_ref.dtype)
            return carry

        lax.fori_loop(0, h1, finish, 0, unroll=4)


def _fft_last_level(bt, f1inv, u, x2c, bias):
    p, _, n2, n1, c = bt.shape
    b, seq_len, _ = u.shape
    cb = 128
    uspec = pl.BlockSpec((2, seq_len, cb), lambda i, j: (i, 0, j))
    return pl.pallas_call(
        _fft_s3_kernel,
        grid=(p, c // cb),
        in_specs=[pl.BlockSpec((1, 2, n2, n1, cb), lambda i, j: (i, 0, 0, 0, j)),
                  pl.BlockSpec(f1inv.shape, lambda i, j: (0, 0)),
                  uspec, uspec,
                  pl.BlockSpec((1, cb), lambda i, j: (0, j))],
        out_specs=uspec,
        out_shape=jax.ShapeDtypeStruct(u.shape, BF16),
        scratch_shapes=[pltpu.VMEM((n2 * _skewed(f1inv.shape[0]), cb), F32)],
        compiler_params=_cparams(("parallel", "parallel")),
        name="hyena_dft_inverse",
    )(bt, f1inv, u, x2c, bias)


def _kf_direct_kernel(h_ref, f_ref, kr_ref, ki_ref):
    n = kr_ref.shape[0]
    dot = functools.partial(jnp.dot, preferred_element_type=F32, precision=HIGHEST)
    hf = dot(f_ref[...], h_ref[0])
    hb = dot(f_ref[...], h_ref[2])
    kr_ref[...] = hf[:n] + hb[:n]
    ki_ref[...] = hf[n:] - hb[n:]


def _hyena_direct_kernel(u_ref, x2_ref, f_ref, fi_ref, kr_ref, ki_ref, bias_ref, o_ref):
    _, two, seq_len, cb = u_ref.shape
    n = kr_ref.shape[0]
    ub = u_ref[0].reshape(two * seq_len, cb)
    x = jnp.dot(f_ref[...], ub, preferred_element_type=F32)
    xr, xi = x[:n], x[n:]
    kr, ki = kr_ref[...], ki_ref[...]
    y = jnp.concatenate([xr * kr - xi * ki, xr * ki + xi * kr], axis=0).astype(BF16)
    conv = jnp.dot(fi_ref[...], y, preferred_element_type=F32)
    x2 = x2_ref[0].reshape(two * seq_len, cb).astype(F32)
    o = (conv + ub.astype(F32) * bias_ref[...]) * x2
    o_ref[0] = o.reshape(two, seq_len, cb).astype(o_ref.dtype)


def _hyena_long_conv(u, x2c, p):
    b, seq_len, _ = u.shape
    c = D_HYENA
    n = 2 * seq_len
    bias = p["hy_bias"].reshape(1, c)
    hh = _hyena_filters(seq_len, p, F32)
    if seq_len <= 512:
        fwd, inv = _stacked_dft(n, seq_len, n, scale=1.0 / n)
        kr, ki = pl.pallas_call(
            _kf_direct_kernel,
            out_shape=[jax.ShapeDtypeStruct((n, c), F32)] * 2,
            compiler_params=_cparams(None),
            name="hyena_filter_spectrum_direct",
        )(hh, fwd[:, :seq_len])
        cb = 512
        pair = lambda a: a.reshape(b // 2, 2, seq_len, c)
        uspec = pl.BlockSpec((1, 2, seq_len, cb), lambda i, j: (i, 0, 0, j))
        kspec = pl.BlockSpec((n, cb), lambda i, j: (0, j))
        out = pl.pallas_call(
            _hyena_direct_kernel,
            grid=(b // 2, c // cb),
            in_specs=[uspec, uspec,
                      pl.BlockSpec(fwd.shape, lambda i, j: (0, 0)),
                      pl.BlockSpec(inv.shape, lambda i, j: (0, 0)),
                      kspec, kspec,
                      pl.BlockSpec((1, cb), lambda i, j: (0, j))],
            out_specs=uspec,
            out_shape=jax.ShapeDtypeStruct((b // 2, 2, seq_len, c), BF16),
            compiler_params=_cparams(("parallel", "parallel")),
            name="hyena_dft_direct",
        )(pair(u), pair(x2c), jnp.asarray(fwd).astype(BF16), jnp.asarray(inv).astype(BF16), kr, ki, bias)
        return out.reshape(b, seq_len, c)

    n2 = FFT_MINOR
    n1 = n // n2
    h1 = seq_len // n2
    f1, f1inv = _stacked_dft(n1, h1, n1, scale=1.0 / n)
    f1, f1inv = jnp.asarray(f1).astype(BF16), jnp.asarray(f1inv).astype(BF16)
    g, gt = _second_level_tables(n, n1, n2)
    g, gt = jnp.asarray(g).astype(BF16), jnp.asarray(gt).astype(BF16)
    ha = _fft_first_level(hh, f1)
    kr, ki = _fft_filter_second_level(ha.reshape(2, 2, n1, n2, c), g)
    a = _fft_first_level(u, f1)
    bt = _fft_second_level(a.reshape(b // 2, 2, n1, n2, c), g, gt, kr, ki)
    return _fft_last_level(bt, f1inv, u, x2c, bias)


def _rope_tables(seq_len):
    half = RET_HEAD_DIM // 2
    nf = half // 2
    t = np.arange(seq_len)
    inv = ROPE_BASE ** (-np.arange(nf, dtype=np.float64) / nf)
    ar = (t // GRID_W)[:, None] * inv[None, :]
    ac = (t % GRID_W)[:, None] * inv[None, :]
    cos = np.concatenate([np.cos(ar), np.cos(ar), np.cos(ac), np.cos(ac)], axis=-1)
    sin = np.concatenate([-np.sin(ar), np.sin(ar), -np.sin(ac), np.sin(ac)], axis=-1)
    return cos.astype(np.float32), sin.astype(np.float32)


def _log_sigmoid(x):
    return jnp.minimum(x, 0.0) - jnp.log1p(jnp.exp(-jnp.abs(x)))


def _retention_kernel(*refs, has_init, emit_state, cpb):
    refs = list(refs)
    q_ref, k_ref, v_ref, g_ref, dec_ref, gn_ref = refs[:6]
    del refs[:6]
    if has_init:
        s0f_ref, s0b_ref = refs[:2]
        del refs[:2]
    o_ref = refs.pop(0)
    if emit_state:
        sf_out, sb_out = refs[:2]
        del refs[:2]
    sf_ref, sb_ref, sball_ref, mask_sc, scale_sc = refs

    c = RET_CHUNK
    dh = RET_HEAD_DIM
    nh = N_RET_HEADS
    phase = pl.program_id(1)
    j = pl.program_id(2)
    nb = pl.num_programs(2)

    diff = (lax.broadcasted_iota(jnp.int32, (c, c), 0) - lax.broadcasted_iota(jnp.int32, (c, c), 1)).astype(F32)
    ri = lax.broadcasted_iota(jnp.int32, (c, dh), 0).astype(F32)

    def head_consts(h):
        lgf = _log_sigmoid(dec_ref[0, h])[0:1, :]
        lgb = _log_sigmoid(dec_ref[1, h])[0:1, :]
        return lgf, lgb

    def chunk_wide(lg):
        return jnp.concatenate([lg] * (c // dh), axis=1)

    bb = q_ref.shape[0]
    heads = [(bi, h) for bi in range(bb) for h in range(nh)]

    def load(ref, bi, r0, h):
        return ref[bi, r0:r0 + c, h * dh:(h + 1) * dh]

    dn_t = (((0,), (0,)), ((), ()))
    dn_nt = (((1,), (1,)), ((), ()))

    @pl.when(jnp.logical_and(pl.program_id(0) == 0, jnp.logical_and(phase == 0, j == 0)))
    def _decay_tables():
        for h in range(nh):
            lgf, lgb = head_consts(h)
            mask_sc[h] = (jnp.where(diff >= 0, jnp.exp(chunk_wide(lgf) * jnp.maximum(diff, 0.0)), 0.0)
                          + jnp.where(diff <= 0, jnp.exp(chunk_wide(lgb) * jnp.maximum(-diff, 0.0)), 0.0))
            scale_sc[h, 0] = jnp.exp(lgb * ri)
            scale_sc[h, 1] = jnp.exp(lgf * (ri + 1.0))
            scale_sc[h, 2] = jnp.exp(lgb * (float(c) - ri))
            scale_sc[h, 3] = jnp.exp(lgf * (float(c - 1) - ri))

    @pl.when(phase == 0)
    def _backward_sweep():
        @pl.when(j == 0)
        def _():
            for bi, h in heads:
                sb_ref[bi * nh + h] = s0b_ref[bi, h] if has_init else jnp.zeros((dh, dh), F32)

        blk = nb - 1 - j
        for bi, h in heads:
            hs = bi * nh + h
            _, lgb = head_consts(h)
            zeta_b = scale_sc[h, 0]
            cdec_b = jnp.exp(lgb * float(c))
            for cc in reversed(range(cpb)):
                r0 = cc * c
                n = blk * cpb + cc
                s = sb_ref[hs]
                sball_ref[n, hs] = s.astype(BF16)
                kz = (load(k_ref, bi, r0, h).astype(F32) * zeta_b).astype(BF16)
                vv = load(v_ref, bi, r0, h)
                sb_ref[hs] = cdec_b * s + lax.dot_general(kz, vv, dn_t, preferred_element_type=F32)

        if emit_state:
            @pl.when(j == nb - 1)
            def _():
                for bi, h in heads:
                    sb_out[bi, h] = sb_ref[bi * nh + h]

    @pl.when(phase == 1)
    def _forward_sweep():
        @pl.when(j == 0)
        def _():
            for bi, h in heads:
                sf_ref[bi * nh + h] = s0f_ref[bi, h] if has_init else jnp.zeros((dh, dh), F32)

        for bi, h in heads:
            hs = bi * nh + h
            lgf, _ = head_consts(h)
            mask = mask_sc[h]
            xi_f, xi_b, zeta_f = scale_sc[h, 1], scale_sc[h, 2], scale_sc[h, 3]
            cdec_f = jnp.exp(lgf * float(c))
            gn = gn_ref[:, h * dh:(h + 1) * dh]
            for cc in range(cpb):
                r0 = cc * c
                n = j * cpb + cc
                qb = load(q_ref, bi, r0, h)
                kb = load(k_ref, bi, r0, h)
                vv = load(v_ref, bi, r0, h)
                gate = load(g_ref, bi, r0, h).astype(F32)
                sc = lax.dot_general(qb, kb, dn_nt, preferred_element_type=F32)
                inner = jnp.dot((sc * mask).astype(BF16), vv, preferred_element_type=F32)
                s = sf_ref[hs]
                q = qb.astype(F32)
                lhs = jnp.concatenate([q * xi_f, q * xi_b], axis=1).astype(BF16)
                rhs = jnp.concatenate([s.astype(BF16), sball_ref[n, hs]], axis=0)
                o = inner + jnp.dot(lhs, rhs, preferred_element_type=F32)
                mu = jnp.mean(o, axis=-1, keepdims=True)
                d = o - mu
                var = jnp.mean(d * d, axis=-1, keepdims=True)
                y = d * lax.rsqrt(var + EPS) * gn * gate
                o_ref[bi, r0:r0 + c, h * dh:(h + 1) * dh] = y.astype(o_ref.dtype)
                kz = (kb.astype(F32) * zeta_f).astype(BF16)
                sf_ref[hs] = cdec_f * s + lax.dot_general(kz, vv, dn_t, preferred_element_type=F32)

        if emit_state:
            @pl.when(j == nb - 1)
            def _():
                for bi, h in heads:
                    sf_out[bi, h] = sf_ref[bi * nh + h]


def _retention(qkvg, dec_f, dec_b, gn_g, s0_f, s0_b, emit_state):
    b, seq_len, _ = qkvg.shape
    nh, dh, c = N_RET_HEADS, RET_HEAD_DIM, RET_CHUNK
    rb = min(seq_len, 1024)
    nb = seq_len // rb
    cpb = rb // c
    bb = max(1, min(4, 1024 // seq_len))
    while b % bb:
        bb //= 2
    has_init = s0_f is not None
    dec = jnp.broadcast_to(jnp.stack([dec_f, dec_b])[:, :, None, None], (2, nh, 8, 128)).astype(F32)
    kv_blk = lambda i, p, j: jnp.where(p == 0, nb - 1 - j, j)
    q_blk = lambda i, p, j: jnp.where(p == 0, 0, j)
    in_specs = [pl.BlockSpec((bb, rb, D_RET), lambda i, p, j: (i, q_blk(i, p, j), 0)),
                pl.BlockSpec((bb, rb, D_RET), lambda i, p, j: (i, kv_blk(i, p, j), 1)),
                pl.BlockSpec((bb, rb, D_RET), lambda i, p, j: (i, kv_blk(i, p, j), 2)),
                pl.BlockSpec((bb, rb, D_RET), lambda i, p, j: (i, q_blk(i, p, j), 3)),
                pl.BlockSpec((2, nh, 8, 128), lambda i, p, j: (0, 0, 0, 0)),
                pl.BlockSpec((1, D_RET), lambda i, p, j: (0, 0))]
    args = [qkvg, qkvg, qkvg, qkvg, dec, gn_g.reshape(1, -1)]
    sspec = pl.BlockSpec((bb, nh, dh, dh), lambda i, p, j: (i, 0, 0, 0))
    if has_init:
        in_specs += [sspec, sspec]
        args += [s0_f, s0_b]
    out_specs = [pl.BlockSpec((bb, rb, D_RET), lambda i, p, j: (i, q_blk(i, p, j), 0))]
    out_shape = [jax.ShapeDtypeStruct((b, seq_len, D_RET), BF16)]
    if emit_state:
        out_specs += [sspec, sspec]
        out_shape += [jax.ShapeDtypeStruct((b, nh, dh, dh), F32)] * 2
    return pl.pallas_call(
        functools.partial(_retention_kernel, has_init=has_init, emit_state=emit_state, cpb=cpb),
        grid=(b // bb, 2, nb),
        in_specs=in_specs,
        out_specs=out_specs,
        out_shape=out_shape,
        scratch_shapes=[pltpu.VMEM((bb * nh, dh, dh), F32), pltpu.VMEM((bb * nh, dh, dh), F32),
                        pltpu.VMEM((nb * cpb, bb * nh, dh, dh), BF16),
                        pltpu.VMEM((nh, c, c), F32), pltpu.VMEM((nh, 4, c, dh), F32)],
        compiler_params=_cparams(("arbitrary", "arbitrary", "arbitrary")),
        name="retention",
    )(*args)


def _outproj_kernel(yhy_ref, yret_ref, gates_ref, x_ref, g1_ref, sc_ref, sh_ref, ng_ref,
                    why_ref, wret_ref, wout_ref, wr_ref, x1_ref, h2_ref, aff_ref):
    g, m, d = x_ref.shape
    rows = lambda ref: ref[...].reshape(g * m, ref.shape[2])
    a = jnp.dot(rows(yhy_ref), why_ref[...], preferred_element_type=F32)
    b = jnp.dot(rows(yret_ref), wret_ref[...], preferred_element_type=F32)
    gates = rows(gates_ref)
    merged = gates[:, :d].astype(F32) * a + gates[:, d:].astype(F32) * b
    out = jnp.dot(merged.astype(BF16), wout_ref[...], preferred_element_type=F32)
    x1 = x_ref[...] + g1_ref[...] * out.reshape(g, m, d)
    x1_ref[...] = x1
    ms = jnp.mean(x1 * x1, axis=-1, keepdims=True)
    h = x1 * lax.rsqrt(ms + EPS) * ng_ref[...]
    h = (h * (1.0 + sc_ref[...]) + sh_ref[...]).reshape(g * m, d)
    h_hi = h.astype(BF16)
    h2_ref[...] = h_hi.reshape(g, m, d)
    h_lo = (h - h_hi.astype(F32)).astype(BF16)
    t = jnp.dot(h_hi, wr_ref[...], preferred_element_type=F32)
    logits = t[:, :128] + t[:, 128:] + jnp.dot(h_lo, wr_ref[:, :128], preferred_element_type=F32)
    lane = lax.broadcasted_iota(jnp.int32, logits.shape, 1)
    logits = jnp.where(lane < N_EXPERTS, logits, -jnp.inf)
    e = jnp.exp(logits - jnp.max(logits, axis=-1, keepdims=True))
    aff = e / jnp.sum(e, axis=-1, keepdims=True)
    aff_ref[0] = aff.T[:N_EXPERTS, :]


def _out_projection(y_hy, y_ret, gates, x, g1, sc2, sh2, norm2_g, w_hy_o, w_ret_o, w_out, w_router, cond_block):
    g, s, d = x.shape
    tm = TOKEN_TILE
    m = tm // g
    tok = lambda c: pl.BlockSpec((g, m, c), lambda i: (0, i, 0))
    cond = pl.BlockSpec((g, 1, d), lambda i: (cond_block, 0, 0))
    full = lambda a: pl.BlockSpec(a.shape, lambda i: (0, 0))
    wr = jnp.pad(w_router, ((0, 0), (0, 128 - N_EXPERTS)))
    wr_hi = wr.astype(BF16)
    wr = jnp.concatenate([wr_hi, (wr - wr_hi.astype(F32)).astype(BF16)], axis=1)
    return pl.pallas_call(
        _outproj_kernel,
        grid=(s // m,),
        in_specs=[tok(D_HYENA), tok(D_RET), tok(2 * d), tok(d), cond, cond, cond,
                  pl.BlockSpec((1, d), lambda i: (0, 0)),
                  full(w_hy_o), full(w_ret_o), full(w_out), full(wr)],
        out_specs=[tok(d), tok(d), pl.BlockSpec((1, N_EXPERTS, tm), lambda i: (i, 0, 0))],
        out_shape=[jax.ShapeDtypeStruct((g, s, d), F32), jax.ShapeDtypeStruct((g, s, d), BF16),
                   jax.ShapeDtypeStruct((s // m, N_EXPERTS, tm), F32)],
        compiler_params=_cparams(("parallel",)),
        name="out_projection_router",
    )(y_hy, y_ret, gates, x, g1, sc2, sh2, norm2_g.reshape(1, d), w_hy_o, w_ret_o, w_out, wr)


def _select_kernel(aff_ref, tri_ref, rank_ref, cnt_ref, start_ref, tot_ref, *, cap, idx_bits, groups):
    nt, ne, tm = aff_ref.shape
    a = aff_ref[...]

    def count(m):
        return jnp.sum(jnp.sum(m, axis=0, keepdims=True), axis=2, keepdims=True)

    def thr_step(s, thr):
        cand = thr | (1 << (30 - s))
        cnt = count(jnp.where(a >= pltpu.bitcast(cand, F32), 1.0, 0.0))
        return jnp.where(cnt >= float(cap), cand, thr)

    thr = pltpu.bitcast(lax.fori_loop(0, 31, thr_step, jnp.zeros((1, ne, 1), jnp.int32)), F32)
    gt = a > thr
    eq = a == thr
    need = float(cap) - count(jnp.where(gt, 1.0, 0.0))
    m = tm // groups
    tile = lax.broadcasted_iota(jnp.int32, (nt, 1, tm), 0)
    lane = lax.broadcasted_iota(jnp.int32, (nt, 1, tm), 2)
    idx = (lane // m) * (nt * m) + tile * m + lane % m

    def idx_step(s, lim):
        cand = lim | (1 << (idx_bits - 1 - s))
        cnt = count(jnp.where(eq, jnp.where(idx < cand, 1.0, 0.0), 0.0))
        return jnp.where(cnt < need, cand, lim)

    lim = lax.fori_loop(0, idx_bits, idx_step, jnp.zeros((1, ne, 1), jnp.int32))
    sel = jnp.where(gt, 1.0, jnp.where(eq, jnp.where(idx <= lim, 1.0, 0.0), 0.0))
    sel2 = sel.reshape(nt * ne, tm).astype(BF16)
    prefix = jnp.dot(sel2, tri_ref[...], preferred_element_type=F32)
    rank = jnp.where(sel2 > 0, prefix, -1.0).astype(jnp.int32)
    rank_ref[...] = rank.reshape(nt, ne, tm)
    ones = jnp.ones((tm, 128), BF16)
    cnt = jnp.dot(sel2, ones, preferred_element_type=F32).astype(jnp.int32).reshape(nt, ne, 128)
    cnt_ref[...] = cnt
    acc = jnp.zeros((ne, 128), jnp.int32)
    for t in range(nt):
        start_ref[t] = acc
        acc = acc + ((cnt[t] + (ROW_ALIGN - 1)) & (-ROW_ALIGN))
    tot_ref[...] = acc


def _select(aff, cap, groups):
    nt, ne, tm = aff.shape
    idx_bits = max(1, int(math.ceil(math.log2(nt * tm))))
    r = lax.broadcasted_iota(jnp.int32, (tm, tm), 0)
    c = lax.broadcasted_iota(jnp.int32, (tm, tm), 1)
    tri = (r < c).astype(BF16)
    rank, cnt, start, tot = pl.pallas_call(
        functools.partial(_select_kernel, cap=cap, idx_bits=idx_bits, groups=groups),
        out_shape=[jax.ShapeDtypeStruct((nt, ne, tm), jnp.int32),
                   jax.ShapeDtypeStruct((nt, ne, 128), jnp.int32),
                   jax.ShapeDtypeStruct((nt, ne, 128), jnp.int32),
                   jax.ShapeDtypeStruct((ne, 128), jnp.int32)],
        compiler_params=_cparams(None),
        name="expert_choice_select",
    )(aff, tri)
    return rank, cnt[:, :, 0], start[:, :, 0], tot[:, 0]


def _expert_block(cap):
    return 512 if cap >= 2048 else 256


def _used_rows(cap, nt):
    eb = _expert_block(cap)
    return -(-(cap + ROW_ALIGN * nt) // eb) * eb


def _list_rows(cap, nt):
    max_rounds = -(-TOKEN_TILE // SLOT_ROWS)
    return _used_rows(cap, nt) + max(_expert_block(cap), max_rounds * SLOT_ROWS)


def _num_rounds(cnt_sm, i):
    m = cnt_sm[i, 0]
    for e in range(1, N_EXPERTS):
        m = jnp.maximum(m, cnt_sm[i, e])
    return jnp.maximum((m + SLOT_ROWS - 1) // SLOT_ROWS, 1)


def _gather_kernel(start_sm, cnt_sm, tot_sm, h_ref, rank_ref, xs_hbm, stage, zbuf, sem):
    i = pl.program_id(0)
    ne = N_EXPERTS
    slot = i % 2
    rank = rank_ref[0]
    sub = lax.broadcasted_iota(jnp.int32, (SLOT_ROWS, rank.shape[1]), 0)

    def copy(s, e, off):
        return pltpu.make_async_copy(stage.at[s, pl.ds(e * SLOT_ROWS, SLOT_ROWS)],
                                     xs_hbm.at[e, pl.ds(off, SLOT_ROWS)], sem.at[e])

    def fill(r):
        h = h_ref[...].reshape(rank.shape[1], h_ref.shape[2])
        for e in range(ne):
            onehot = jnp.where(rank[e:e + 1, :] == sub + r * SLOT_ROWS, 1.0, 0.0).astype(BF16)
            stage[slot, e * SLOT_ROWS:(e + 1) * SLOT_ROWS, :] = jnp.dot(
                onehot, h, preferred_element_type=F32).astype(BF16)

    def start_all(r):
        for e in range(ne):
            copy(slot, e, pl.multiple_of(start_sm[i, e] + r * SLOT_ROWS, ROW_ALIGN)).start()

    def wait_all(s):
        for e in range(ne):
            copy(s, e, 0).wait()

    fill(0)

    @pl.when(i > 0)
    def _():
        wait_all(1 - slot)

    start_all(0)

    def extra_round(r, carry):
        wait_all(slot)
        fill(r)
        start_all(r)
        return carry

    lax.fori_loop(1, _num_rounds(cnt_sm, i), extra_round, 0)

    @pl.when(i == pl.num_programs(0) - 1)
    def _zero_tail():
        wait_all(slot)
        zbuf[...] = jnp.zeros(zbuf.shape, zbuf.dtype)

        def zcopy(e):
            off = pl.multiple_of(tot_sm[e], ROW_ALIGN)
            return pltpu.make_async_copy(zbuf, xs_hbm.at[e, pl.ds(off, zbuf.shape[0])], sem.at[e])

        for e in range(ne):
            zcopy(e).start()
        for e in range(ne):
            zcopy(e).wait()


def _gather(h2, rank, start, cnt, tot, cap):
    g, _, d = h2.shape
    nt, ne, tm = rank.shape
    rl = _list_rows(cap, nt)
    return pl.pallas_call(
        _gather_kernel,
        grid_spec=pltpu.PrefetchScalarGridSpec(
            num_scalar_prefetch=3,
            grid=(nt,),
            in_specs=[pl.BlockSpec((g, tm // g, d), lambda i, *_: (0, i, 0)),
                      pl.BlockSpec((1, ne, tm), lambda i, *_: (i, 0, 0))],
            out_specs=pl.BlockSpec(memory_space=pl.ANY),
            scratch_shapes=[pltpu.VMEM((2, ne * SLOT_ROWS, d), BF16),
                            pltpu.VMEM((_expert_block(cap), d), BF16),
                            pltpu.SemaphoreType.DMA((ne,))]),
        out_shape=jax.ShapeDtypeStruct((ne, rl, d), BF16),
        compiler_params=_cparams(("arbitrary",)),
        name="expert_gather",
    )(start, cnt, tot, h2, rank)


def _expert_kernel(*refs, nblocks):
    ns = len(nblocks)
    tot_sms, xs_refs = refs[:ns], refs[ns:2 * ns]
    wg_ref, wu_ref, wd_ref = refs[2 * ns:2 * ns + 3]
    ye_refs = refs[2 * ns + 3:3 * ns + 3]
    wg_bf, wu_bf, wd_bf = refs[3 * ns + 3:]
    e = pl.program_id(0)
    j = pl.program_id(1)

    @pl.when(j == 0)
    def _():
        wg_bf[...] = wg_ref[0].astype(BF16)
        wu_bf[...] = wu_ref[0].astype(BF16)
        wd_bf[...] = wd_ref[0].astype(BF16)

    base = 0
    for s in range(ns):
        jj = j - base
        live = jnp.logical_and(jnp.logical_and(jj >= 0, jj < nblocks[s]), jj * xs_refs[s].shape[1] < tot_sms[s][e])

        @pl.when(live)
        def _(xs_ref=xs_refs[s], ye_ref=ye_refs[s]):
            x = xs_ref[0]
            g = jnp.dot(x, wg_bf[...], preferred_element_type=F32)
            u = jnp.dot(x, wu_bf[...], preferred_element_type=F32)
            hid = (_silu(g) * u).astype(BF16)
            ye_ref[0] = jnp.dot(hid, wd_bf[...], preferred_element_type=F32).astype(ye_ref.dtype)

        base += nblocks[s]


def _experts(sets, w_gate, w_up, w_down):
    ns = len(sets)
    ne, _, d = sets[0][0].shape
    f = w_gate.shape[2]
    nblocks = tuple(used // eb for _, _, used, eb in sets)

    def block_spec(s):
        eb = sets[s][3]
        base = sum(nblocks[:s])

        def index(e, j, *tots):
            return (e, jnp.clip(j - base, 0, (tots[s][e] - 1) // eb), 0)

        return pl.BlockSpec((1, eb, d), index)

    wspec = lambda shape: pl.BlockSpec(shape, lambda e, j, *tots: (e, 0, 0))
    return pl.pallas_call(
        functools.partial(_expert_kernel, nblocks=nblocks),
        grid_spec=pltpu.PrefetchScalarGridSpec(
            num_scalar_prefetch=ns,
            grid=(ne, sum(nblocks)),
            in_specs=[block_spec(s) for s in range(ns)] + [wspec((1, d, f)), wspec((1, d, f)), wspec((1, f, d))],
            out_specs=[block_spec(s) for s in range(ns)],
            scratch_shapes=[pltpu.VMEM((d, f), BF16), pltpu.VMEM((d, f), BF16), pltpu.VMEM((f, d), BF16)]),
        out_shape=[jax.ShapeDtypeStruct((ne, used, d), BF16) for _, _, used, _ in sets],
        compiler_params=_cparams(("arbitrary", "arbitrary")),
        name="expert_ffn",
    )(*[t for _, t, _, _ in sets], *[x for x, _, _, _ in sets], w_gate, w_up, w_down)


def _combine_kernel(start_sm, cnt_sm, tot_sm, x1_ref, rank_ref, aff_ref, g2_ref, nf_ref, ye_hbm, o_ref, buf, sem, *,
                    final_norm, eb):
    i = pl.program_id(0)
    nt = pl.num_programs(0)
    ne = N_EXPERTS
    slot = i % 2
    rank = rank_ref[0]
    aff = aff_ref[0]
    sub = lax.broadcasted_iota(jnp.int32, (SLOT_ROWS, rank.shape[1]), 0)
    dn_t = (((0,), (0,)), ((), ()))

    def window(t, e, r):
        want = start_sm[t, e] + r * SLOT_ROWS
        written = ((tot_sm[e] + eb - 1) // eb) * eb
        off = jnp.minimum(want, written - SLOT_ROWS)
        return pl.multiple_of(off, ROW_ALIGN), want - off

    def copy(t, s, e, r):
        off, _ = window(t, e, r)
        return pltpu.make_async_copy(ye_hbm.at[e, pl.ds(off, SLOT_ROWS)],
                                     buf.at[s, pl.ds(e * SLOT_ROWS, SLOT_ROWS)], sem.at[s, e])

    def weighted(r):
        parts = []
        for e in range(ne):
            _, shift = window(i, e, r)
            local = rank[e:e + 1, :] - r * SLOT_ROWS
            hit = jnp.where(local >= 0, local + shift, -1) == sub
            parts.append(jnp.where(hit, aff[e:e + 1, :], 0.0).astype(BF16))
        return lax.dot_general(jnp.concatenate(parts, axis=0), buf[slot], dn_t, preferred_element_type=F32)

    @pl.when(i == 0)
    def _():
        for e in range(ne):
            copy(0, 0, e, 0).start()

    @pl.when(i + 1 < nt)
    def _prefetch_next_tile():
        for e in range(ne):
            copy(i + 1, 1 - slot, e, 0).start()

    for e in range(ne):
        copy(i, slot, e, 0).wait()
    y0 = weighted(0)

    def extra_round(r, acc):
        for e in range(ne):
            @pl.when(cnt_sm[i, e] > r * SLOT_ROWS)
            def _():
                copy(i, slot, e, r).start()
        for e in range(ne):
            @pl.when(cnt_sm[i, e] > r * SLOT_ROWS)
            def _():
                copy(i, slot, e, r).wait()
        return acc + weighted(r)

    y = lax.fori_loop(1, _num_rounds(cnt_sm, i), extra_round, y0)
    x2 = x1_ref[...] + g2_ref[...] * y.reshape(x1_ref.shape)
    if final_norm:
        ms = jnp.mean(x2 * x2, axis=-1, keepdims=True)
        x2 = x2 * lax.rsqrt(ms + EPS) * nf_ref[...]
    o_ref[...] = x2


def _combine(x1, rank, aff, start, cnt, tot, g2, norm_f_g, ye, cond_block, eb):
    g, s, d = x1.shape
    nt, ne, tm = rank.shape
    tok = pl.BlockSpec((g, tm // g, d), lambda i, *_: (0, i, 0))
    final_norm = norm_f_g is not None
    if not final_norm:
        norm_f_g = jnp.ones((d,), F32)
    return pl.pallas_call(
        functools.partial(_combine_kernel, final_norm=final_norm, eb=eb),
        grid_spec=pltpu.PrefetchScalarGridSpec(
            num_scalar_prefetch=3,
            grid=(nt,),
            in_specs=[tok,
                      pl.BlockSpec((1, ne, tm), lambda i, *_: (i, 0, 0)),
                      pl.BlockSpec((1, ne, tm), lambda i, *_: (i, 0, 0)),
                      pl.BlockSpec((g, 1, d), lambda i, *_: (cond_block, 0, 0)),
                      pl.BlockSpec((1, d), lambda i, *_: (0, 0)),
                      pl.BlockSpec(memory_space=pl.ANY)],
            out_specs=tok,
            scratch_shapes=[pltpu.VMEM((2, ne * SLOT_ROWS, d), BF16),
                            pltpu.SemaphoreType.DMA((2, ne))]),
        out_shape=jax.ShapeDtypeStruct((g, s, d), F32),
        compiler_params=_cparams(("arbitrary",)),
        name="expert_combine",
    )(start, cnt, tot, x1, rank, aff, g2, norm_f_g.reshape(1, d), ye)


def _layer_front(x, batch, seq_len, mods, cond_of_tile, moe_groups, cond_block, p, s0_f, s0_b, rope, emit_state):
    t, d = x.shape
    sh1, sc1, g1, sh2, sc2, g2 = mods
    u, x2c, qkvg, gates = _in_projection(x, p["norm1_g"], sc1, sh1, p["w_in"], p["hy_conv_w"], p["hy_conv_b"],
                                         cond_of_tile, seq_len, rope)
    seq = lambda a: a.reshape(batch, seq_len, a.shape[-1])
    y_hy = _hyena_long_conv(seq(u), seq(x2c), p).reshape(t, -1)
    ret = _retention(qkvg.reshape(batch, seq_len, -1), p["ret_decay_fwd"], p["ret_decay_bwd"], p["ret_gn_g"],
                     s0_f, s0_b, emit_state)
    y_ret = ret[0].reshape(t, -1)
    view = lambda a: a.reshape(moe_groups, t // moe_groups, a.shape[-1])
    x1, h2, aff = _out_projection(view(y_hy), view(y_ret), view(gates), view(x), g1, sc2, sh2, p["norm2_g"],
                                  p["w_hy_o"], p["w_ret_o"], p["w_out"], p["w_router"], cond_block)
    cap = (EC_CAPACITY_FACTOR * t) // N_EXPERTS
    eb = _expert_block(cap)
    rank, cnt, start, tot = _select(aff, cap, moe_groups)
    xs = _gather(h2, rank, start, cnt, tot, cap)
    routed = (xs, tot, _used_rows(cap, rank.shape[0]), eb)
    back = dict(x1=x1, rank=rank, aff=aff, start=start, cnt=cnt, tot=tot, g2=g2, cond_block=cond_block, eb=eb)
    return routed, back, ret[1:]


def _layer_back(back, ye, final_g):
    out = _combine(back["x1"], back["rank"], back["aff"], back["start"], back["cnt"], back["tot"], back["g2"],
                   final_g, ye, back["cond_block"], back["eb"])
    return out.reshape(-1, out.shape[-1])


def kernel(x_prompt, x_sample, state_ret_fwd, state_ret_bwd, c, c_ctx, w_mod, b_mod, norm1_g, w_in, hy_conv_w, hy_conv_b, hy_f_w1, hy_f_b1, hy_f_freq, hy_f_w2, hy_f_b2, hy_f_w3, hy_decay, hy_bias, ret_decay_fwd, ret_decay_bwd, ret_gn_g, w_hy_o, w_ret_o, w_out, norm2_g, w_router, w_e_gate, w_e_up, w_e_down, norm_f_g):
    b, s, d = x_prompt.shape
    bd, sd, _ = x_sample.shape
    depth = w_mod.shape[0]
    assert (b * s) % TOKEN_TILE == 0 and sd % TOKEN_TILE == 0 and b % 2 == 0 and bd % 2 == 0
    ncond = -(-(bd + 1) // 8) * 8
    lat_groups = bd if (TOKEN_TILE % (16 * bd) == 0 and sd % (TOKEN_TILE // bd) == 0) else 1
    assert lat_groups == bd or bd == 1
    conds = jnp.concatenate([c, c_ctx[None], jnp.zeros((ncond - bd - 1, d), F32)], axis=0)
    ctx_cond = lambda i: bd
    lat_cond = lambda i: i // (sd // TOKEN_TILE)
    xp = x_prompt.reshape(b * s, d)
    xs = x_sample.reshape(bd * sd, d)
    new_f, new_b = [], []
    for l in range(depth):
        mod = _modulation(conds, w_mod[l], b_mod[l])
        mods = [mod[:, k * d:(k + 1) * d].reshape(ncond, 1, d) for k in range(6)]
        p = {"norm1_g": norm1_g[l], "w_in": w_in[l].astype(BF16), "hy_conv_w": hy_conv_w[l],
             "hy_conv_b": hy_conv_b[l], "hy_f_w1": hy_f_w1[l], "hy_f_b1": hy_f_b1[l], "hy_f_freq": hy_f_freq[l],
             "hy_f_w2": hy_f_w2[l], "hy_f_b2": hy_f_b2[l], "hy_f_w3": hy_f_w3[l], "hy_decay": hy_decay[l],
             "hy_bias": hy_bias[l], "ret_decay_fwd": ret_decay_fwd[l], "ret_decay_bwd": ret_decay_bwd[l],
             "ret_gn_g": ret_gn_g[l], "w_hy_o": w_hy_o[l].astype(BF16), "w_ret_o": w_ret_o[l].astype(BF16),
             "w_out": w_out[l].astype(BF16), "norm2_g": norm2_g[l], "w_router": w_router[l],
             "w_e_gate": w_e_gate[l], "w_e_up": w_e_up[l], "w_e_down": w_e_down[l]}
        final_g = norm_f_g if l == depth - 1 else None
        routed_p, back_p, (s_f, s_b) = _layer_front(xp, b, s, mods, ctx_cond, 1, bd, p, None, None, False, True)
        new_f.append(s_f)
        new_b.append(s_b)
        routed_s, back_s, _ = _layer_front(xs, bd, sd, mods, lat_cond, lat_groups, 0, p, state_ret_fwd[:, l],
                                           state_ret_bwd[:, l], True, False)
        ye_p, ye_s = _experts([routed_p, routed_s], p["w_e_gate"], p["w_e_up"], p["w_e_down"])
        xp = _layer_back(back_p, ye_p, final_g)
        xs = _layer_back(back_s, ye_s, final_g)
    y_prompt = xp.reshape(b, s, d)
    y_sample = xs.reshape(bd, sd, d)
    return (y_prompt, y_sample, jnp.stack(new_f, axis=1), jnp.stack(new_b, axis=1))
```

```python
import functools
import math

import numpy as np
import jax
import jax.numpy as jnp
from jax import lax
from jax.experimental import pallas as pl
from jax.experimental.pallas import tpu as pltpu

F32 = jnp.float32
BF16 = jnp.bfloat16
HIGHEST = lax.Precision.HIGHEST

EPS = 1e-6
D_HYENA = 512
D_RET = 512
N_RET_HEADS = 4
RET_HEAD_DIM = 128
RET_CHUNK = 256
GRID_W = 64
FILTER_EMB = 33
ROPE_BASE = 10000.0
N_EXPERTS = 16
EC_CAPACITY_FACTOR = 2

TOKEN_TILE = 512
SLOT_ROWS = 96
ROW_ALIGN = 16
HALO_ROWS = 16
FFT_MINOR = 64
FFT_GROUP = 16
VMEM_LIMIT = 56 * 1024 * 1024


def _cparams(sem):
    return pltpu.CompilerParams(dimension_semantics=sem, vmem_limit_bytes=VMEM_LIMIT)


def _silu(x):
    return x * jax.nn.sigmoid(x)


def _mod_kernel(c_ref, w_ref, b_ref, o_ref):
    s = _silu(c_ref[...])
    o_ref[...] = jnp.dot(s, w_ref[...], preferred_element_type=F32, precision=HIGHEST) + b_ref[...]


def _modulation(conds, w_mod, b_mod):
    nc, d = conds.shape
    return pl.pallas_call(
        _mod_kernel,
        grid=(6,),
        in_specs=[pl.BlockSpec((nc, d), lambda j: (0, 0)),
                  pl.BlockSpec((d, d), lambda j: (0, j)),
                  pl.BlockSpec((1, d), lambda j: (0, j))],
        out_specs=pl.BlockSpec((nc, d), lambda j: (0, j)),
        out_shape=jax.ShapeDtypeStruct((nc, 6 * d), F32),
        compiler_params=_cparams(("arbitrary",)),
        name="modulation",
    )(conds, w_mod, b_mod.reshape(1, -1))


def _inproj_kernel(*refs, splits, rope, seq_len, halo):
    refs = list(refs)
    x_ref = refs.pop(0)
    if halo:
        xp_ref, xn_ref = refs.pop(0), refs.pop(0)
    g_ref, sc_ref, sh_ref, w_ref, cw_ref, cb_ref = refs[:6]
    del refs[:6]
    if rope:
        cos_ref, sin_ref = refs.pop(0), refs.pop(0)
    u_ref, x2c_ref, qkvg_ref, gates_ref, cv_sc, cx_sc = refs

    def normed(ref):
        x = ref[...]
        ms = jnp.mean(x * x, axis=-1, keepdims=True)
        h = x * lax.rsqrt(ms + EPS) * g_ref[...]
        return (h * (1.0 + sc_ref[0]) + sh_ref[0]).astype(BF16)

    hb = normed(x_ref)
    tm = hb.shape[0]
    nz, nq, ng = splits
    cw = 512
    dh = RET_HEAD_DIM
    hr = HALO_ROWS
    if halo:
        t0 = pl.program_id(0) * tm
        hp = jnp.where(t0 % seq_len != 0, normed(xp_ref), jnp.zeros((hr, hb.shape[1]), BF16))
        hn = jnp.where((t0 + tm) % seq_len != 0, normed(xn_ref), jnp.zeros((hr, hb.shape[1]), BF16))
        hext = jnp.concatenate([hp, hb, hn], axis=0)

    piece = tm if halo else seq_len
    pitch = piece + hr
    if not halo:
        for s in range(tm // piece + 1):
            cx_sc[s * pitch:s * pitch + hr, :] = jnp.zeros((hr, cw), F32)

    def conv(c0):
        taps = cw_ref[:, c0:c0 + cw]
        bias = cb_ref[:, c0:c0 + cw]
        if halo:
            cx_sc[...] = jnp.dot(hext, w_ref[:, c0:c0 + cw], preferred_element_type=F32)
        else:
            acc = jnp.dot(hb, w_ref[:, c0:c0 + cw], preferred_element_type=F32)
            for s in range(tm // piece):
                cx_sc[hr + s * pitch:hr + s * pitch + piece, :] = acc[s * piece:(s + 1) * piece]
        outs = []
        for s in range(tm // piece):
            r0 = hr + s * pitch
            outs.append(bias + cx_sc[r0 - 1:r0 - 1 + piece, :] * taps[0:1] + cx_sc[r0:r0 + piece, :] * taps[1:2]
                        + cx_sc[r0 + 1:r0 + 1 + piece, :] * taps[2:3])
        return outs[0] if len(outs) == 1 else jnp.concatenate(outs, axis=0)

    cv_sc[...] = conv(0)
    u_ref[...] = (cv_sc[...] * conv(D_HYENA)).astype(BF16)
    x2c_ref[...] = conv(2 * D_HYENA).astype(BF16)

    for c0 in range(nz, nz + nq + ng, cw):
        acc = jnp.dot(hb, w_ref[:, c0:c0 + cw], preferred_element_type=F32)
        if c0 < nz + nq:
            part = (c0 - nz) // D_RET
            if part == 1:
                acc = acc * (dh ** -0.5)
            if rope and part < 2:
                lane = lax.broadcasted_iota(jnp.int32, (acc.shape[0], dh), 1)
                swap_hi = (lane % (dh // 2)) < (dh // 4)
                cs, sn = cos_ref[...], sin_ref[...]
                heads = []
                for hh in range(cw // dh):
                    xh = acc[:, hh * dh:(hh + 1) * dh]
                    rot = jnp.where(swap_hi, pltpu.roll(xh, dh - dh // 4, axis=1), pltpu.roll(xh, dh // 4, axis=1))
                    heads.append(xh * cs + rot * sn)
                acc = jnp.concatenate(heads, axis=1)
            qkvg_ref[:, c0 - nz:c0 - nz + cw] = acc.astype(BF16)
        else:
            gates_ref[:, c0 - nz - nq:c0 - nz - nq + cw] = jax.nn.sigmoid(acc).astype(BF16)


def _in_projection(x, norm_g, sc, sh, w_in_bf, conv_w, conv_b, cond_of_tile, seq_len, rope):
    t, d = x.shape
    splits = (3 * D_HYENA, 4 * D_RET, 2 * d)
    assert D_RET == 512 and D_HYENA == 512
    tm = TOKEN_TILE
    assert seq_len % tm == 0 or tm % seq_len == 0
    halo = seq_len > tm
    row = lambda i: (i, 0)
    cond = lambda i: (cond_of_tile(i), 0, 0)
    full = lambda a: pl.BlockSpec(a.shape, lambda i: (0, 0))
    args, specs = [x], [pl.BlockSpec((tm, d), row)]
    if halo:
        hb_per_tile = tm // HALO_ROWS
        last = t // HALO_ROWS - 1
        args += [x, x]
        specs += [pl.BlockSpec((HALO_ROWS, d), lambda i: (jnp.maximum(i * hb_per_tile - 1, 0), 0)),
                  pl.BlockSpec((HALO_ROWS, d), lambda i: (jnp.minimum((i + 1) * hb_per_tile, last), 0))]
    cbias = conv_b.reshape(1, -1)
    args += [norm_g.reshape(1, d), sc, sh, w_in_bf, conv_w, cbias]
    specs += [pl.BlockSpec((1, d), lambda i: (0, 0)), pl.BlockSpec((1, 1, d), cond), pl.BlockSpec((1, 1, d), cond),
              full(w_in_bf), full(conv_w), full(cbias)]
    if rope:
        tiles_per_seq = seq_len // tm
        args += list(_rope_tables(seq_len))
        specs += [pl.BlockSpec((tm, RET_HEAD_DIM), lambda i: (i % tiles_per_seq, 0))] * 2
    return pl.pallas_call(
        functools.partial(_inproj_kernel, splits=splits, rope=rope, seq_len=seq_len, halo=halo),
        grid=(t // tm,),
        in_specs=specs,
        out_specs=[pl.BlockSpec((tm, D_HYENA), row),
                   pl.BlockSpec((tm, D_HYENA), row),
                   pl.BlockSpec((tm, splits[1]), row),
                   pl.BlockSpec((tm, splits[2]), row)],
        out_shape=[jax.ShapeDtypeStruct((t, D_HYENA), BF16),
                   jax.ShapeDtypeStruct((t, D_HYENA), BF16),
                   jax.ShapeDtypeStruct((t, splits[1]), BF16),
                   jax.ShapeDtypeStruct((t, splits[2]), BF16)],
        scratch_shapes=[pltpu.VMEM((tm, 512), F32),
                        pltpu.VMEM((tm + 2 * HALO_ROWS if halo else (tm // seq_len) * (seq_len + HALO_ROWS) + HALO_ROWS,
                                    512), F32)],
        compiler_params=_cparams(("parallel",)),
        name="in_projection",
    )(*args)


def _filter_features(seq_len):
    t = np.linspace(0.0, 1.0, seq_len, dtype=np.float32)[:, None]
    bands = (FILTER_EMB - 1) // 2
    w = (np.float32(2.0 * math.pi) * np.arange(seq_len, dtype=np.float32)) / np.float32(seq_len)
    f = np.linspace(1e-4, bands - 1, bands, dtype=np.float32)
    ang = (w[:, None] * f[None, :]).astype(np.float64)
    z = np.concatenate([t, np.cos(ang), -np.sin(ang)], axis=-1).astype(np.float32)
    return np.pad(z, ((0, 0), (0, 128 - FILTER_EMB)))


def _filter_kernel(z_ref, w1_ref, b1_ref, fr_ref, w2_ref, b2_ref, w3_ref, dec_ref, o_ref):
    z = z_ref[...]
    fr = fr_ref[...]
    dot = functools.partial(jnp.dot, preferred_element_type=F32, precision=HIGHEST)
    h = jnp.sin(fr * (dot(z, w1_ref[...]) + b1_ref[...]))
    h = jnp.sin(fr * (dot(h, w2_ref[...]) + b2_ref[...]))
    split = lambda a: (a.astype(BF16), (a - a.astype(BF16).astype(F32)).astype(BF16))
    (h_hi, h_lo), (w_hi, w_lo) = split(h), split(w3_ref[...])
    bdot = functools.partial(jnp.dot, preferred_element_type=F32)
    h = (bdot(h_hi, w_hi) + bdot(h_hi, w_lo) + bdot(h_lo, w_hi)) * jnp.exp(-z[:, 0:1] * jnp.abs(dec_ref[...]))
    rows = h.shape[0]
    grow = pl.program_id(0) * rows + lax.broadcasted_iota(jnp.int32, (rows, 1), 0)
    o_ref[0] = h[:, :D_HYENA].astype(o_ref.dtype)
    o_ref[1] = jnp.zeros((rows, D_HYENA), o_ref.dtype)
    o_ref[2] = jnp.where(grow == 0, 0.0, h[:, D_HYENA:]).astype(o_ref.dtype)
    o_ref[3] = jnp.zeros((rows, D_HYENA), o_ref.dtype)


def _hyena_filters(seq_len, p, out_dtype):
    z = jnp.asarray(_filter_features(seq_len))
    w1 = jnp.pad(p["hy_f_w1"], ((0, 128 - FILTER_EMB), (0, 0)))
    rows = min(seq_len, 512)
    full = lambda a: pl.BlockSpec(a.shape, lambda i: (0,) * a.ndim)
    ops = [w1, p["hy_f_b1"].reshape(1, -1), p["hy_f_freq"].reshape(1, -1), p["hy_f_w2"],
           p["hy_f_b2"].reshape(1, -1), p["hy_f_w3"], p["hy_decay"].reshape(1, -1)]
    return pl.pallas_call(
        _filter_kernel,
        grid=(seq_len // rows,),
        in_specs=[pl.BlockSpec((rows, 128), lambda i: (i, 0))] + [full(a) for a in ops],
        out_specs=pl.BlockSpec((4, rows, D_HYENA), lambda i: (0, i, 0)),
        out_shape=jax.ShapeDtypeStruct((4, seq_len, D_HYENA), out_dtype),
        compiler_params=_cparams(("arbitrary",)),
        name="hyena_filter",
    )(z, *ops)


def _stacked_dft(n_out, n_in, modulus, scale=1.0):
    k = np.arange(n_out, dtype=np.int64)[:, None]
    n = np.arange(n_in, dtype=np.int64)[None, :]
    th = ((k * n) % modulus) * (2.0 * math.pi / modulus)
    c, s = np.cos(th), np.sin(th)
    fwd = np.block([[c, s], [-s, c]]).astype(np.float32)
    inv = (np.block([[c.T, -s.T], [s.T, c.T]]) * scale).astype(np.float32)
    return fwd, inv


def _second_level_tables(n_total, n1, n2):
    k1 = np.arange(n1, dtype=np.int64)[:, None, None]
    k2 = np.arange(n2, dtype=np.int64)[None, :, None]
    m = np.arange(n2, dtype=np.int64)[None, None, :]
    th = ((m * (k1 + n1 * k2)) % n_total) * (2.0 * math.pi / n_total)
    c, s = np.cos(th), np.sin(th)
    g = np.concatenate([np.concatenate([c, s], axis=2), np.concatenate([-s, c], axis=2)], axis=1)
    return g.astype(np.float32), np.swapaxes(g, 1, 2).astype(np.float32)


def _skewed(n):
    return n + 1


def _fft1_kernel(h_ref, f_ref, a_ref, u_sc, t_sc):
    n2 = FFT_MINOR
    _, seq_len, cb = h_ref.shape
    h1 = seq_len // n2
    nk = a_ref.shape[1]
    up, tp = _skewed(n2), _skewed(nk)
    for b in range(2):
        for q in range(h1):
            u_sc[b, q * up:q * up + n2, :] = h_ref[b, q * n2:(q + 1) * n2, :].astype(F32)

    def fwd(gi, carry):
        m0 = gi * FFT_GROUP
        cols = []
        for d in range(FFT_GROUP):
            xr = u_sc[0, pl.ds(m0 + d, h1, stride=up), :]
            xi = u_sc[1, pl.ds(m0 + d, h1, stride=up), :]
            cols.append(jnp.concatenate([xr, xi], axis=0))
        x = jnp.concatenate(cols, axis=1).astype(BF16)
        res = jnp.dot(f_ref[...], x, preferred_element_type=F32)
        for d in range(FFT_GROUP):
            t_sc[pl.ds((m0 + d) * tp, nk), :] = res[:, d * cb:(d + 1) * cb]
        return carry

    lax.fori_loop(0, n2 // FFT_GROUP, fwd, 0, unroll=2)

    def transpose(k, carry):
        a_ref[0, k] = t_sc[pl.ds(k, n2, stride=tp), :].astype(a_ref.dtype)
        return carry

    lax.fori_loop(0, nk, transpose, 0, unroll=8)


def _fft_first_level(h, f1):
    b, seq_len, c = h.shape
    cb = 128
    return pl.pallas_call(
        _fft1_kernel,
        grid=(b // 2, c // cb),
        in_specs=[pl.BlockSpec((2, seq_len, cb), lambda i, j: (i, 0, j)),
                  pl.BlockSpec(f1.shape, lambda i, j: (0, 0))],
        out_specs=pl.BlockSpec((1, f1.shape[0], FFT_MINOR, cb), lambda i, j: (i, 0, 0, j)),
        out_shape=jax.ShapeDtypeStruct((b // 2, f1.shape[0], FFT_MINOR, c), BF16),
        scratch_shapes=[pltpu.VMEM((2, (seq_len // FFT_MINOR) * _skewed(FFT_MINOR), cb), F32),
                        pltpu.VMEM((FFT_MINOR * _skewed(f1.shape[0]), cb), F32)],
        compiler_params=_cparams(("parallel", "parallel")),
        name="hyena_dft_level1",
    )(h, f1)


def _fft_s2f_kernel(a_ref, g_ref, kr_ref, ki_ref):
    _, _, kb, n2, c = a_ref.shape
    for kk in range(kb):
        g = g_ref[kk]
        hf = jnp.dot(g, a_ref[0, :, kk].reshape(2 * n2, c), preferred_element_type=F32)
        hb = jnp.dot(g, a_ref[1, :, kk].reshape(2 * n2, c), preferred_element_type=F32)
        kr_ref[kk] = hf[:n2] + hb[:n2]
        ki_ref[kk] = hf[n2:] - hb[n2:]


def _fft_filter_second_level(a, g, kb=16):
    _, _, n1, n2, c = a.shape
    spec = pl.BlockSpec((kb, n2, c), lambda i: (i, 0, 0))
    return pl.pallas_call(
        _fft_s2f_kernel,
        grid=(n1 // kb,),
        in_specs=[pl.BlockSpec((2, 2, kb, n2, c), lambda i: (0, 0, i, 0, 0)),
                  pl.BlockSpec((kb, 2 * n2, 2 * n2), lambda i: (i, 0, 0))],
        out_specs=[spec, spec],
        out_shape=[jax.ShapeDtypeStruct((n1, n2, c), F32)] * 2,
        compiler_params=_cparams(("parallel",)),
        name="hyena_filter_spectrum",
    )(a, g)


def _fft_s2_kernel(a_hbm, g_ref, gt_ref, kr_ref, ki_ref, b_ref, r_sc, a_buf, a_sem):
    ring, npairs, _, kb, n2, cb = a_buf.shape
    nq = cb // 128
    rp = _skewed(2 * n2)
    nj = pl.num_programs(1)
    step = pl.program_id(0) * nj + pl.program_id(1)
    total = pl.num_programs(0) * nj

    def a_copy(s):
        src = a_hbm.at[:, :, pl.ds((s // nj) * kb, kb), :, pl.ds(pl.multiple_of((s % nj) * cb, cb), cb)]
        return pltpu.make_async_copy(src, a_buf.at[s % ring], a_sem.at[s % ring])

    @pl.when(step == 0)
    def _():
        for s in range(ring - 1):
            a_copy(s).start()

    @pl.when(step + ring - 1 < total)
    def _():
        a_copy(step + ring - 1).start()

    a_copy(step).wait()
    a_ref = a_buf.at[step % ring]
    for kk in range(kb):
        kr, ki = kr_ref[kk], ki_ref[kk]
        for p in range(npairs):
            x = jnp.dot(g_ref[kk], a_ref[p, :, kk].reshape(2 * n2, cb), preferred_element_type=F32)
            xr, xi = x[:n2], x[n2:]
            y = jnp.concatenate([xr * kr - xi * ki, xr * ki + xi * kr], axis=0).astype(BF16)
            res = jnp.dot(gt_ref[kk], y, preferred_element_type=F32)
            for q in range(nq):
                r_sc[p * nq + q, kk * rp:kk * rp + 2 * n2, :] = res[:, q * 128:(q + 1) * 128]

    for p in range(npairs):
        def gather(row):
            return jnp.concatenate([r_sc[p * nq + q, pl.ds(row, kb, stride=rp), :] for q in range(nq)], axis=1)

        def transpose(m, carry):
            b_ref[p, 0, m] = gather(m).astype(b_ref.dtype)
            b_ref[p, 1, m] = gather(n2 + m).astype(b_ref.dtype)
            return carry

        lax.fori_loop(0, n2, transpose, 0, unroll=8)


def _fft_second_level(a, g, gt, kr, ki):
    p, _, n1, n2, c = a.shape
    kb, cb = 16, 256
    gspec = pl.BlockSpec((kb, 2 * n2, 2 * n2), lambda i, j: (i, 0, 0))
    kspec = pl.BlockSpec((kb, n2, cb), lambda i, j: (i, 0, j))
    return pl.pallas_call(
        _fft_s2_kernel,
        grid=(n1 // kb, c // cb),
        in_specs=[pl.BlockSpec(memory_space=pl.ANY), gspec, gspec, kspec, kspec],
        out_specs=pl.BlockSpec((p, 2, n2, kb, cb), lambda i, j: (0, 0, 0, i, j)),
        out_shape=jax.ShapeDtypeStruct((p, 2, n2, n1, c), BF16),
        scratch_shapes=[pltpu.VMEM((p * cb // 128, kb * _skewed(2 * n2), 128), F32),
                        pltpu.VMEM((3, p, 2, kb, n2, cb), BF16),
                        pltpu.SemaphoreType.DMA((3,))],
        compiler_params=_cparams(("arbitrary", "arbitrary")),
        name="hyena_dft_level2",
    )(a, g, gt, kr, ki)


def _fft_s3_kernel(b_ref, f_ref, u_ref, x2_ref, bias_ref, o_ref, t_sc):
    _, _, n2, n1, cb = b_ref.shape
    nr = f_ref.shape[0]
    h1 = nr // 2
    tp = _skewed(nr)

    def inv(gi, carry):
        m0 = gi * FFT_GROUP
        x = jnp.concatenate([b_ref[0, :, m0 + d].reshape(2 * n1, cb) for d in range(FFT_GROUP)], axis=1)
        res = jnp.dot(f_ref[...], x, preferred_element_type=F32)
        for d in range(FFT_GROUP):
            t_sc[pl.ds((m0 + d) * tp, nr), :] = res[:, d * cb:(d + 1) * cb]
        return carry

    lax.fori_loop(0, n2 // FFT_GROUP, inv, 0, unroll=2)

    bias = bias_ref[...]
    for b in range(2):
        def finish(q, carry):
            rows = pl.ds(pl.multiple_of(q * n2, n2), n2)
            conv = t_sc[pl.ds(b * h1 + q, n2, stride=tp), :]
            u = u_ref[b, rows, :].astype(F32)
            o_ref[b, rows, :] = ((conv + u * bias) * x2_ref[b, rows, :].astype(F32)).astype(o_ref.dtype)
            return carry

        lax.fori_loop(0, h1, finish, 0, unroll=4)


def _fft_last_level(bt, f1inv, u, x2c, bias):
    p, _, n2, n1, c = bt.shape
    b, seq_len, _ = u.shape
    cb = 128
    uspec = pl.BlockSpec((2, seq_len, cb), lambda i, j: (i, 0, j))
    return pl.pallas_call(
        _fft_s3_kernel,
        grid=(p, c // cb),
        in_specs=[pl.BlockSpec((1, 2, n2, n1, cb), lambda i, j: (i, 0, 0, 0, j)),
                  pl.BlockSpec(f1inv.shape, lambda i, j: (0, 0)),
                  uspec, uspec,
                  pl.BlockSpec((1, cb), lambda i, j: (0, j))],
        out_specs=uspec,
        out_shape=jax.ShapeDtypeStruct(u.shape, BF16),
        scratch_shapes=[pltpu.VMEM((n2 * _skewed(f1inv.shape[0]), cb), F32)],
        compiler_params=_cparams(("parallel", "parallel")),
        name="hyena_dft_inverse",
    )(bt, f1inv, u, x2c, bias)


def _kf_direct_kernel(h_ref, f_ref, kr_ref, ki_ref):
    n = kr_ref.shape[0]
    dot = functools.partial(jnp.dot, preferred_element_type=F32, precision=HIGHEST)
    hf = dot(f_ref[...], h_ref[0])
    hb = dot(f_ref[...], h_ref[2])
    kr_ref[...] = hf[:n] + hb[:n]
    ki_ref[...] = hf[n:] - hb[n:]


def _hyena_direct_kernel(u_ref, x2_ref, f_ref, fi_ref, kr_ref, ki_ref, bias_ref, o_ref):
    _, two, seq_len, cb = u_ref.shape
    n = kr_ref.shape[0]
    ub = u_ref[0].reshape(two * seq_len, cb)
    x = jnp.dot(f_ref[...], ub, preferred_element_type=F32)
    xr, xi = x[:n], x[n:]
    kr, ki = kr_ref[...], ki_ref[...]
    y = jnp.concatenate([xr * kr - xi * ki, xr * ki + xi * kr], axis=0).astype(BF16)
    conv = jnp.dot(fi_ref[...], y, preferred_element_type=F32)
    x2 = x2_ref[0].reshape(two * seq_len, cb).astype(F32)
    o = (conv + ub.astype(F32) * bias_ref[...]) * x2
    o_ref[0] = o.reshape(two, seq_len, cb).astype(o_ref.dtype)


def _hyena_long_conv(u, x2c, p):
    b, seq_len, _ = u.shape
    c = D_HYENA
    n = 2 * seq_len
    bias = p["hy_bias"].reshape(1, c)
    hh = _hyena_filters(seq_len, p, F32)
    if seq_len <= 512:
        fwd, inv = _stacked_dft(n, seq_len, n, scale=1.0 / n)
        kr, ki = pl.pallas_call(
            _kf_direct_kernel,
            out_shape=[jax.ShapeDtypeStruct((n, c), F32)] * 2,
            compiler_params=_cparams(None),
            name="hyena_filter_spectrum_direct",
        )(hh, fwd[:, :seq_len])
        cb = 512
        pair = lambda a: a.reshape(b // 2, 2, seq_len, c)
        uspec = pl.BlockSpec((1, 2, seq_len, cb), lambda i, j: (i, 0, 0, j))
        kspec = pl.BlockSpec((n, cb), lambda i, j: (0, j))
        out = pl.pallas_call(
            _hyena_direct_kernel,
            grid=(b // 2, c // cb),
            in_specs=[uspec, uspec,
                      pl.BlockSpec(fwd.shape, lambda i, j: (0, 0)),
                      pl.BlockSpec(inv.shape, lambda i, j: (0, 0)),
                      kspec, kspec,
                      pl.BlockSpec((1, cb), lambda i, j: (0, j))],
            out_specs=uspec,
            out_shape=jax.ShapeDtypeStruct((b // 2, 2, seq_len, c), BF16),
            compiler_params=_cparams(("parallel", "parallel")),
            name="hyena_dft_direct",
        )(pair(u), pair(x2c), jnp.asarray(fwd).astype(BF16), jnp.asarray(inv).astype(BF16), kr, ki, bias)
        return out.reshape(b, seq_len, c)

    n2 = FFT_MINOR
    n1 = n // n2
    h1 = seq_len // n2
    f1, f1inv = _stacked_dft(n1, h1, n1, scale=1.0 / n)
    f1, f1inv = jnp.asarray(f1).astype(BF16), jnp.asarray(f1inv).astype(BF16)
    g, gt = _second_level_tables(n, n1, n2)
    g, gt = jnp.asarray(g).astype(BF16), jnp.asarray(gt).astype(BF16)
    ha = _fft_first_level(hh, f1)
    kr, ki = _fft_filter_second_level(ha.reshape(2, 2, n1, n2, c), g)
    a = _fft_first_level(u, f1)
    bt = _fft_second_level(a.reshape(b // 2, 2, n1, n2, c), g, gt, kr, ki)
    return _fft_last_level(bt, f1inv, u, x2c, bias)


def _rope_tables(seq_len):
    half = RET_HEAD_DIM // 2
    nf = half // 2
    t = np.arange(seq_len)
    inv = ROPE_BASE ** (-np.arange(nf, dtype=np.float64) / nf)
    ar = (t // GRID_W)[:, None] * inv[None, :]
    ac = (t % GRID_W)[:, None] * inv[None, :]
    cos = np.concatenate([np.cos(ar), np.cos(ar), np.cos(ac), np.cos(ac)], axis=-1)
    sin = np.concatenate([-np.sin(ar), np.sin(ar), -np.sin(ac), np.sin(ac)], axis=-1)
    return cos.astype(np.float32), sin.astype(np.float32)


def _log_sigmoid(x):
    return jnp.minimum(x, 0.0) - jnp.log1p(jnp.exp(-jnp.abs(x)))


def _retention_kernel(*refs, has_init, emit_state, cpb):
    refs = list(refs)
    q_ref, k_ref, v_ref, g_ref, dec_ref, gn_ref = refs[:6]
    del refs[:6]
    if has_init:
        s0f_ref, s0b_ref = refs[:2]
        del refs[:2]
    o_ref = refs.pop(0)
    if emit_state:
        sf_out, sb_out = refs[:2]
        del refs[:2]
    sf_ref, sb_ref, sball_ref, mask_sc, scale_sc = refs

    c = RET_CHUNK
    dh = RET_HEAD_DIM
    nh = N_RET_HEADS
    phase = pl.program_id(1)
    j = pl.program_id(2)
    nb = pl.num_programs(2)

    diff = (lax.broadcasted_iota(jnp.int32, (c, c), 0) - lax.broadcasted_iota(jnp.int32, (c, c), 1)).astype(F32)
    ri = lax.broadcasted_iota(jnp.int32, (c, dh), 0).astype(F32)

    def head_consts(h):
        lgf = _log_sigmoid(dec_ref[0, h])[0:1, :]
        lgb = _log_sigmoid(dec_ref[1, h])[0:1, :]
        return lgf, lgb

    def chunk_wide(lg):
        return jnp.concatenate([lg] * (c // dh), axis=1)

    bb = q_ref.shape[0]
    heads = [(bi, h) for bi in range(bb) for h in range(nh)]

    def load(ref, bi, r0, h):
        return ref[bi, r0:r0 + c, h * dh:(h + 1) * dh]

    dn_t = (((0,), (0,)), ((), ()))
    dn_nt = (((1,), (1,)), ((), ()))

    @pl.when(jnp.logical_and(pl.program_id(0) == 0, jnp.logical_and(phase == 0, j == 0)))
    def _decay_tables():
        for h in range(nh):
            lgf, lgb = head_consts(h)
            mask_sc[h] = (jnp.where(diff >= 0, jnp.exp(chunk_wide(lgf) * jnp.maximum(diff, 0.0)), 0.0)
                          + jnp.where(diff <= 0, jnp.exp(chunk_wide(lgb) * jnp.maximum(-diff, 0.0)), 0.0))
            scale_sc[h, 0] = jnp.exp(lgb * ri)
            scale_sc[h, 1] = jnp.exp(lgf * (ri + 1.0))
            scale_sc[h, 2] = jnp.exp(lgb * (float(c) - ri))
            scale_sc[h, 3] = jnp.exp(lgf * (float(c - 1) - ri))

    @pl.when(phase == 0)
    def _backward_sweep():
        @pl.when(j == 0)
        def _():
            for bi, h in heads:
                sb_ref[bi * nh + h] = s0b_ref[bi, h] if has_init else jnp.zeros((dh, dh), F32)

        blk = nb - 1 - j
        for bi, h in heads:
            hs = bi * nh + h
            _, lgb = head_consts(h)
            zeta_b = scale_sc[h, 0]
            cdec_b = jnp.exp(lgb * float(c))
            for cc in reversed(range(cpb)):
                r0 = cc * c
                n = blk * cpb + cc
                s = sb_ref[hs]
                sball_ref[n, hs] = s.astype(BF16)
                kz = (load(k_ref, bi, r0, h).astype(F32) * zeta_b).astype(BF16)
                vv = load(v_ref, bi, r0, h)
                sb_ref[hs] = cdec_b * s + lax.dot_general(kz, vv, dn_t, preferred_element_type=F32)

        if emit_state:
            @pl.when(j == nb - 1)
            def _():
                for bi, h in heads:
                    sb_out[bi, h] = sb_ref[bi * nh + h]

    @pl.when(phase == 1)
    def _forward_sweep():
        @pl.when(j == 0)
        def _():
            for bi, h in heads:
                sf_ref[bi * nh + h] = s0f_ref[bi, h] if has_init else jnp.zeros((dh, dh), F32)

        for bi, h in heads:
            hs = bi * nh + h
            lgf, _ = head_consts(h)
            mask = mask_sc[h]
            xi_f, xi_b, zeta_f = scale_sc[h, 1], scale_sc[h, 2], scale_sc[h, 3]
            cdec_f = jnp.exp(lgf * float(c))
            gn = gn_ref[:, h * dh:(h + 1) * dh]
            for cc in range(cpb):
                r0 = cc * c
                n = j * cpb + cc
                qb = load(q_ref, bi, r0, h)
                kb = load(k_ref, bi, r0, h)
                vv = load(v_ref, bi, r0, h)
                gate = load(g_ref, bi, r0, h).astype(F32)
                sc = lax.dot_general(qb, kb, dn_nt, preferred_element_type=F32)
                inner = jnp.dot((sc * mask).astype(BF16), vv, preferred_element_type=F32)
                s = sf_ref[hs]
                q = qb.astype(F32)
                lhs = jnp.concatenate([q * xi_f, q * xi_b], axis=1).astype(BF16)
                rhs = jnp.concatenate([s.astype(BF16), sball_ref[n, hs]], axis=0)
                o = inner + jnp.dot(lhs, rhs, preferred_element_type=F32)
                mu = jnp.mean(o, axis=-1, keepdims=True)
                d = o - mu
                var = jnp.mean(d * d, axis=-1, keepdims=True)
                y = d * lax.rsqrt(var + EPS) * gn * _silu(gate)
                o_ref[bi, r0:r0 + c, h * dh:(h + 1) * dh] = y.astype(o_ref.dtype)
                kz = (kb.astype(F32) * zeta_f).astype(BF16)
                sf_ref[hs] = cdec_f * s + lax.dot_general(kz, vv, dn_t, preferred_element_type=F32)

        if emit_state:
            @pl.when(j == nb - 1)
            def _():
                for bi, h in heads:
                    sf_out[bi, h] = sf_ref[bi * nh + h]


def _retention(qkvg, dec_f, dec_b, gn_g, s0_f, s0_b, emit_state):
    b, seq_len, _ = qkvg.shape
    nh, dh, c = N_RET_HEADS, RET_HEAD_DIM, RET_CHUNK
    rb = min(seq_len, 1024)
    nb = seq_len // rb
    cpb = rb // c
    bb = max(1, min(4, 1024 // seq_len))
    while b % bb:
        bb //= 2
    has_init = s0_f is not None
    dec = jnp.broadcast_to(jnp.stack([dec_f, dec_b])[:, :, None, None], (2, nh, 8, 128)).astype(F32)
    kv_blk = lambda i, p, j: jnp.where(p == 0, nb - 1 - j, j)
    q_blk = lambda i, p, j: jnp.where(p == 0, 0, j)
    in_specs = [pl.BlockSpec((bb, rb, D_RET), lambda i, p, j: (i, q_blk(i, p, j), 0)),
                pl.BlockSpec((bb, rb, D_RET), lambda i, p, j: (i, kv_blk(i, p, j), 1)),
                pl.BlockSpec((bb, rb, D_RET), lambda i, p, j: (i, kv_blk(i, p, j), 2)),
                pl.BlockSpec((bb, rb, D_RET), lambda i, p, j: (i, q_blk(i, p, j), 3)),
                pl.BlockSpec((2, nh, 8, 128), lambda i, p, j: (0, 0, 0, 0)),
                pl.BlockSpec((1, D_RET), lambda i, p, j: (0, 0))]
    args = [qkvg, qkvg, qkvg, qkvg, dec, gn_g.reshape(1, -1)]
    sspec = pl.BlockSpec((bb, nh, dh, dh), lambda i, p, j: (i, 0, 0, 0))
    if has_init:
        in_specs += [sspec, sspec]
        args += [s0_f, s0_b]
    out_specs = [pl.BlockSpec((bb, rb, D_RET), lambda i, p, j: (i, q_blk(i, p, j), 0))]
    out_shape = [jax.ShapeDtypeStruct((b, seq_len, D_RET), BF16)]
    if emit_state:
        out_specs += [sspec, sspec]
        out_shape += [jax.ShapeDtypeStruct((b, nh, dh, dh), F32)] * 2
    return pl.pallas_call(
        functools.partial(_retention_kernel, has_init=has_init, emit_state=emit_state, cpb=cpb),
        grid=(b // bb, 2, nb),
        in_specs=in_specs,
        out_specs=out_specs,
        out_shape=out_shape,
        scratch_shapes=[pltpu.VMEM((bb * nh, dh, dh), F32), pltpu.VMEM((bb * nh, dh, dh), F32),
                        pltpu.VMEM((nb * cpb, bb * nh, dh, dh), BF16),
                        pltpu.VMEM((nh, c, c), F32), pltpu.VMEM((nh, 4, c, dh), F32)],
        compiler_params=_cparams(("arbitrary", "arbitrary", "arbitrary")),
        name="retention",
    )(*args)


def _outproj_kernel(yhy_ref, yret_ref, gates_ref, x_ref, g1_ref, sc_ref, sh_ref, ng_ref,
                    why_ref, wret_ref, wout_ref, wr_ref, x1_ref, h2_ref, aff_ref):
    g, m, d = x_ref.shape
    rows = lambda ref: ref[...].reshape(g * m, ref.shape[2])
    a = jnp.dot(rows(yhy_ref), why_ref[...], preferred_element_type=F32)
    b = jnp.dot(rows(yret_ref), wret_ref[...], preferred_element_type=F32)
    gates = rows(gates_ref)
    merged = gates[:, :d].astype(F32) * a + gates[:, d:].astype(F32) * b
    out = jnp.dot(merged.astype(BF16), wout_ref[...], preferred_element_type=F32)
    x1 = x_ref[...] + g1_ref[...] * out.reshape(g, m, d)
    x1_ref[...] = x1
    ms = jnp.mean(x1 * x1, axis=-1, keepdims=True)
    h = x1 * lax.rsqrt(ms + EPS) * ng_ref[...]
    h = (h * (1.0 + sc_ref[...]) + sh_ref[...]).reshape(g * m, d)
    h_hi = h.astype(BF16)
    h2_ref[...] = h_hi.reshape(g, m, d)
    h_lo = (h - h_hi.astype(F32)).astype(BF16)
    t = jnp.dot(h_hi, wr_ref[...], preferred_element_type=F32)
    logits = t[:, :128] + t[:, 128:] + jnp.dot(h_lo, wr_ref[:, :128], preferred_element_type=F32)
    lane = lax.broadcasted_iota(jnp.int32, logits.shape, 1)
    logits = jnp.where(lane < N_EXPERTS, logits, -jnp.inf)
    e = jnp.exp(logits - jnp.max(logits, axis=-1, keepdims=True))
    aff = e / jnp.sum(e, axis=-1, keepdims=True)
    aff_ref[0] = aff.T[:N_EXPERTS, :]


def _out_projection(y_hy, y_ret, gates, x, g1, sc2, sh2, norm2_g, w_hy_o, w_ret_o, w_out, w_router, cond_block):
    g, s, d = x.shape
    tm = TOKEN_TILE
    m = tm // g
    tok = lambda c: pl.BlockSpec((g, m, c), lambda i: (0, i, 0))
    cond = pl.BlockSpec((g, 1, d), lambda i: (cond_block, 0, 0))
    full = lambda a: pl.BlockSpec(a.shape, lambda i: (0, 0))
    wr = jnp.pad(w_router, ((0, 0), (0, 128 - N_EXPERTS)))
    wr_hi = wr.astype(BF16)
    wr = jnp.concatenate([wr_hi, (wr - wr_hi.astype(F32)).astype(BF16)], axis=1)
    return pl.pallas_call(
        _outproj_kernel,
        grid=(s // m,),
        in_specs=[tok(D_HYENA), tok(D_RET), tok(2 * d), tok(d), cond, cond, cond,
                  pl.BlockSpec((1, d), lambda i: (0, 0)),
                  full(w_hy_o), full(w_ret_o), full(w_out), full(wr)],
        out_specs=[tok(d), tok(d), pl.BlockSpec((1, N_EXPERTS, tm), lambda i: (i, 0, 0))],
        out_shape=[jax.ShapeDtypeStruct((g, s, d), F32), jax.ShapeDtypeStruct((g, s, d), BF16),
                   jax.ShapeDtypeStruct((s // m, N_EXPERTS, tm), F32)],
        compiler_params=_cparams(("parallel",)),
        name="out_projection_router",
    )(y_hy, y_ret, gates, x, g1, sc2, sh2, norm2_g.reshape(1, d), w_hy_o, w_ret_o, w_out, wr)


def _select_kernel(aff_ref, tri_ref, rank_ref, cnt_ref, start_ref, tot_ref, *, cap, idx_bits, groups):
    nt, ne, tm = aff_ref.shape
    a = aff_ref[...]

    def count(m):
        return jnp.sum(jnp.sum(m, axis=0, keepdims=True), axis=2, keepdims=True)

    def thr_step(s, thr):
        cand = thr | (1 << (30 - s))
        cnt = count(jnp.where(a >= pltpu.bitcast(cand, F32), 1.0, 0.0))
        return jnp.where(cnt >= float(cap), cand, thr)

    thr = pltpu.bitcast(lax.fori_loop(0, 31, thr_step, jnp.zeros((1, ne, 1), jnp.int32)), F32)
    gt = a > thr
    eq = a == thr
    need = float(cap) - count(jnp.where(gt, 1.0, 0.0))
    m = tm // groups
    tile = lax.broadcasted_iota(jnp.int32, (nt, 1, tm), 0)
    lane = lax.broadcasted_iota(jnp.int32, (nt, 1, tm), 2)
    idx = (lane // m) * (nt * m) + tile * m + lane % m

    def idx_step(s, lim):
        cand = lim | (1 << (idx_bits - 1 - s))
        cnt = count(jnp.where(eq, jnp.where(idx < cand, 1.0, 0.0), 0.0))
        return jnp.where(cnt < need, cand, lim)

    lim = lax.fori_loop(0, idx_bits, idx_step, jnp.zeros((1, ne, 1), jnp.int32))
    sel = jnp.where(gt, 1.0, jnp.where(eq, jnp.where(idx <= lim, 1.0, 0.0), 0.0))
    sel2 = sel.reshape(nt * ne, tm).astype(BF16)
    prefix = jnp.dot(sel2, tri_ref[...], preferred_element_type=F32)
    rank = jnp.where(sel2 > 0, prefix, -1.0).astype(jnp.int32)
    rank_ref[...] = rank.reshape(nt, ne, tm)
    ones = jnp.ones((tm, 128), BF16)
    cnt = jnp.dot(sel2, ones, preferred_element_type=F32).astype(jnp.int32).reshape(nt, ne, 128)
    cnt_ref[...] = cnt
    acc = jnp.zeros((ne, 128), jnp.int32)
    for t in range(nt):
        start_ref[t] = acc
        acc = acc + ((cnt[t] + (ROW_ALIGN - 1)) & (-ROW_ALIGN))
    tot_ref[...] = acc


def _select(aff, cap, groups):
    nt, ne, tm = aff.shape
    idx_bits = max(1, int(math.ceil(math.log2(nt * tm))))
    r = lax.broadcasted_iota(jnp.int32, (tm, tm), 0)
    c = lax.broadcasted_iota(jnp.int32, (tm, tm), 1)
    tri = (r < c).astype(BF16)
    rank, cnt, start, tot = pl.pallas_call(
        functools.partial(_select_kernel, cap=cap, idx_bits=idx_bits, groups=groups),
        out_shape=[jax.ShapeDtypeStruct((nt, ne, tm), jnp.int32),
                   jax.ShapeDtypeStruct((nt, ne, 128), jnp.int32),
                   jax.ShapeDtypeStruct((nt, ne, 128), jnp.int32),
                   jax.ShapeDtypeStruct((ne, 128), jnp.int32)],
        compiler_params=_cparams(None),
        name="expert_choice_select",
    )(aff, tri)
    return rank, cnt[:, :, 0], start[:, :, 0], tot[:, 0]


def _expert_block(cap):
    return 512 if cap >= 2048 else 256


def _used_rows(cap, nt):
    eb = _expert_block(cap)
    return -(-(cap + ROW_ALIGN * nt) // eb) * eb


def _list_rows(cap, nt):
    max_rounds = -(-TOKEN_TILE // SLOT_ROWS)
    return _used_rows(cap, nt) + max(_expert_block(cap), max_rounds * SLOT_ROWS)


def _num_rounds(cnt_sm, i):
    m = cnt_sm[i, 0]
    for e in range(1, N_EXPERTS):
        m = jnp.maximum(m, cnt_sm[i, e])
    return jnp.maximum((m + SLOT_ROWS - 1) // SLOT_ROWS, 1)


def _gather_kernel(start_sm, cnt_sm, tot_sm, h_ref, rank_ref, xs_hbm, stage, zbuf, sem):
    i = pl.program_id(0)
    ne = N_EXPERTS
    slot = i % 2
    rank = rank_ref[0]
    sub = lax.broadcasted_iota(jnp.int32, (SLOT_ROWS, rank.shape[1]), 0)

    def copy(s, e, off):
        return pltpu.make_async_copy(stage.at[s, pl.ds(e * SLOT_ROWS, SLOT_ROWS)],
                                     xs_hbm.at[e, pl.ds(off, SLOT_ROWS)], sem.at[e])

    def fill(r):
        h = h_ref[...].reshape(rank.shape[1], h_ref.shape[2])
        for e in range(ne):
            onehot = jnp.where(rank[e:e + 1, :] == sub + r * SLOT_ROWS, 1.0, 0.0).astype(BF16)
            stage[slot, e * SLOT_ROWS:(e + 1) * SLOT_ROWS, :] = jnp.dot(
                onehot, h, preferred_element_type=F32).astype(BF16)

    def start_all(r):
        for e in range(ne):
            copy(slot, e, pl.multiple_of(start_sm[i, e] + r * SLOT_ROWS, ROW_ALIGN)).start()

    def wait_all(s):
        for e in range(ne):
            copy(s, e, 0).wait()

    fill(0)

    @pl.when(i > 0)
    def _():
        wait_all(1 - slot)

    start_all(0)

    def extra_round(r, carry):
        wait_all(slot)
        fill(r)
        start_all(r)
        return carry

    lax.fori_loop(1, _num_rounds(cnt_sm, i), extra_round, 0)

    @pl.when(i == pl.num_programs(0) - 1)
    def _zero_tail():
        wait_all(slot)
        zbuf[...] = jnp.zeros(zbuf.shape, zbuf.dtype)

        def zcopy(e):
            off = pl.multiple_of(tot_sm[e], ROW_ALIGN)
            return pltpu.make_async_copy(zbuf, xs_hbm.at[e, pl.ds(off, zbuf.shape[0])], sem.at[e])

        for e in range(ne):
            zcopy(e).start()
        for e in range(ne):
            zcopy(e).wait()


def _gather(h2, rank, start, cnt, tot, cap):
    g, _, d = h2.shape
    nt, ne, tm = rank.shape
    rl = _list_rows(cap, nt)
    return pl.pallas_call(
        _gather_kernel,
        grid_spec=pltpu.PrefetchScalarGridSpec(
            num_scalar_prefetch=3,
            grid=(nt,),
            in_specs=[pl.BlockSpec((g, tm // g, d), lambda i, *_: (0, i, 0)),
                      pl.BlockSpec((1, ne, tm), lambda i, *_: (i, 0, 0))],
            out_specs=pl.BlockSpec(memory_space=pl.ANY),
            scratch_shapes=[pltpu.VMEM((2, ne * SLOT_ROWS, d), BF16),
                            pltpu.VMEM((_expert_block(cap), d), BF16),
                            pltpu.SemaphoreType.DMA((ne,))]),
        out_shape=jax.ShapeDtypeStruct((ne, rl, d), BF16),
        compiler_params=_cparams(("arbitrary",)),
        name="expert_gather",
    )(start, cnt, tot, h2, rank)


def _expert_kernel(*refs, nblocks):
    ns = len(nblocks)
    tot_sms, xs_refs = refs[:ns], refs[ns:2 * ns]
    wg_ref, wu_ref, wd_ref = refs[2 * ns:2 * ns + 3]
    ye_refs = refs[2 * ns + 3:3 * ns + 3]
    wg_bf, wu_bf, wd_bf = refs[3 * ns + 3:]
    e = pl.program_id(0)
    j = pl.program_id(1)

    @pl.when(j == 0)
    def _():
        wg_bf[...] = wg_ref[0].astype(BF16)
        wu_bf[...] = wu_ref[0].astype(BF16)
        wd_bf[...] = wd_ref[0].astype(BF16)

    base = 0
    for s in range(ns):
        jj = j - base
        live = jnp.logical_and(jnp.logical_and(jj >= 0, jj < nblocks[s]), jj * xs_refs[s].shape[1] < tot_sms[s][e])

        @pl.when(live)
        def _(xs_ref=xs_refs[s], ye_ref=ye_refs[s]):
            x = xs_ref[0]
            g = jnp.dot(x, wg_bf[...], preferred_element_type=F32)
            u = jnp.dot(x, wu_bf[...], preferred_element_type=F32)
            hid = (_silu(g) * u).astype(BF16)
            ye_ref[0] = jnp.dot(hid, wd_bf[...], preferred_element_type=F32).astype(ye_ref.dtype)

        base += nblocks[s]


def _experts(sets, w_gate, w_up, w_down):
    ns = len(sets)
    ne, _, d = sets[0][0].shape
    f = w_gate.shape[2]
    nblocks = tuple(used // eb for _, _, used, eb in sets)

    def block_spec(s):
        eb = sets[s][3]
        base = sum(nblocks[:s])

        def index(e, j, *tots):
            return (e, jnp.clip(j - base, 0, (tots[s][e] - 1) // eb), 0)

        return pl.BlockSpec((1, eb, d), index)

    wspec = lambda shape: pl.BlockSpec(shape, lambda e, j, *tots: (e, 0, 0))
    return pl.pallas_call(
        functools.partial(_expert_kernel, nblocks=nblocks),
        grid_spec=pltpu.PrefetchScalarGridSpec(
            num_scalar_prefetch=ns,
            grid=(ne, sum(nblocks)),
            in_specs=[block_spec(s) for s in range(ns)] + [wspec((1, d, f)), wspec((1, d, f)), wspec((1, f, d))],
            out_specs=[block_spec(s) for s in range(ns)],
            scratch_shapes=[pltpu.VMEM((d, f), BF16), pltpu.VMEM((d, f), BF16), pltpu.VMEM((f, d), BF16)]),
        out_shape=[jax.ShapeDtypeStruct((ne, used, d), BF16) for _, _, used, _ in sets],
        compiler_params=_cparams(("arbitrary", "arbitrary")),
        name="expert_ffn",
    )(*[t for _, t, _, _ in sets], *[x for x, _, _, _ in sets], w_gate, w_up, w_down)


def _combine_kernel(start_sm, cnt_sm, tot_sm, x1_ref, rank_ref, aff_ref, g2_ref, nf_ref, ye_hbm, o_ref, buf, sem, *,
                    final_norm, eb):
    i = pl.program_id(0)
    nt = pl.num_programs(0)
    ne = N_EXPERTS
    slot = i % 2
    rank = rank_ref[0]
    aff = aff_ref[0]
    sub = lax.broadcasted_iota(jnp.int32, (SLOT_ROWS, rank.shape[1]), 0)
    dn_t = (((0,), (0,)), ((), ()))

    def window(t, e, r):
        want = start_sm[t, e] + r * SLOT_ROWS
        written = ((tot_sm[e] + eb - 1) // eb) * eb
        off = jnp.minimum(want, written - SLOT_ROWS)
        return pl.multiple_of(off, ROW_ALIGN), want - off

    def copy(t, s, e, r):
        off, _ = window(t, e, r)
        return pltpu.make_async_copy(ye_hbm.at[e, pl.ds(off, SLOT_ROWS)],
                                     buf.at[s, pl.ds(e * SLOT_ROWS, SLOT_ROWS)], sem.at[s, e])

    def weighted(r):
        parts = []
        for e in range(ne):
            _, shift = window(i, e, r)
            local = rank[e:e + 1, :] - r * SLOT_ROWS
            hit = jnp.where(local >= 0, local + shift, -1) == sub
            parts.append(jnp.where(hit, aff[e:e + 1, :], 0.0).astype(BF16))
        return lax.dot_general(jnp.concatenate(parts, axis=0), buf[slot], dn_t, preferred_element_type=F32)

    @pl.when(i == 0)
    def _():
        for e in range(ne):
            copy(0, 0, e, 0).start()

    @pl.when(i + 1 < nt)
    def _prefetch_next_tile():
        for e in range(ne):
            copy(i + 1, 1 - slot, e, 0).start()

    for e in range(ne):
        copy(i, slot, e, 0).wait()
    y0 = weighted(0)

    def extra_round(r, acc):
        for e in range(ne):
            @pl.when(cnt_sm[i, e] > r * SLOT_ROWS)
            def _():
                copy(i, slot, e, r).start()
        for e in range(ne):
            @pl.when(cnt_sm[i, e] > r * SLOT_ROWS)
            def _():
                copy(i, slot, e, r).wait()
        return acc + weighted(r)

    y = lax.fori_loop(1, _num_rounds(cnt_sm, i), extra_round, y0)
    x2 = x1_ref[...] + g2_ref[...] * y.reshape(x1_ref.shape)
    if final_norm:
        ms = jnp.mean(x2 * x2, axis=-1, keepdims=True)
        x2 = x2 * lax.rsqrt(ms + EPS) * nf_ref[...]
    o_ref[...] = x2


def _combine(x1, rank, aff, start, cnt, tot, g2, norm_f_g, ye, cond_block, eb):
    g, s, d = x1.shape
    nt, ne, tm = rank.shape
    tok = pl.BlockSpec((g, tm // g, d), lambda i, *_: (0, i, 0))
    final_norm = norm_f_g is not None
    if not final_norm:
        norm_f_g = jnp.ones((d,), F32)
    return pl.pallas_call(
        functools.partial(_combine_kernel, final_norm=final_norm, eb=eb),
        grid_spec=pltpu.PrefetchScalarGridSpec(
            num_scalar_prefetch=3,
            grid=(nt,),
            in_specs=[tok,
                      pl.BlockSpec((1, ne, tm), lambda i, *_: (i, 0, 0)),
                      pl.BlockSpec((1, ne, tm), lambda i, *_: (i, 0, 0)),
                      pl.BlockSpec((g, 1, d), lambda i, *_: (cond_block, 0, 0)),
                      pl.BlockSpec((1, d), lambda i, *_: (0, 0)),
                      pl.BlockSpec(memory_space=pl.ANY)],
            out_specs=tok,
            scratch_shapes=[pltpu.VMEM((2, ne * SLOT_ROWS, d), BF16),
                            pltpu.SemaphoreType.DMA((2, ne))]),
        out_shape=jax.ShapeDtypeStruct((g, s, d), F32),
        compiler_params=_cparams(("arbitrary",)),
        name="expert_combine",
    )(start, cnt, tot, x1, rank, aff, g2, norm_f_g.reshape(1, d), ye)


def _layer_front(x, batch, seq_len, mods, cond_of_tile, moe_groups, cond_block, p, s0_f, s0_b, rope, emit_state):
    t, d = x.shape
    sh1, sc1, g1, sh2, sc2, g2 = mods
    u, x2c, qkvg, gates = _in_projection(x, p["norm1_g"], sc1, sh1, p["w_in"], p["hy_conv_w"], p["hy_conv_b"],
                                         cond_of_tile, seq_len, rope)
    seq = lambda a: a.reshape(batch, seq_len, a.shape[-1])
    y_hy = _hyena_long_conv(seq(u), seq(x2c), p).reshape(t, -1)
    ret = _retention(qkvg.reshape(batch, seq_len, -1), p["ret_decay_fwd"], p["ret_decay_bwd"], p["ret_gn_g"],
                     s0_f, s0_b, emit_state)
    y_ret = ret[0].reshape(t, -1)
    view = lambda a: a.reshape(moe_groups, t // moe_groups, a.shape[-1])
    x1, h2, aff = _out_projection(view(y_hy), view(y_ret), view(gates), view(x), g1, sc2, sh2, p["norm2_g"],
                                  p["w_hy_o"], p["w_ret_o"], p["w_out"], p["w_router"], cond_block)
    cap = (EC_CAPACITY_FACTOR * t) // N_EXPERTS
    eb = _expert_block(cap)
    rank, cnt, start, tot = _select(aff, cap, moe_groups)
    xs = _gather(h2, rank, start, cnt, tot, cap)
    routed = (xs, tot, _used_rows(cap, rank.shape[0]), eb)
    back = dict(x1=x1, rank=rank, aff=aff, start=start, cnt=cnt, tot=tot, g2=g2, cond_block=cond_block, eb=eb)
    return routed, back, ret[1:]


def _layer_back(back, ye, final_g):
    out = _combine(back["x1"], back["rank"], back["aff"], back["start"], back["cnt"], back["tot"], back["g2"],
                   final_g, ye, back["cond_block"], back["eb"])
    return out.reshape(-1, out.shape[-1])


def kernel(x_prompt, x_sample, state_ret_fwd, state_ret_bwd, c, c_ctx, w_mod, b_mod, norm1_g, w_in, hy_conv_w, hy_conv_b, hy_f_w1, hy_f_b1, hy_f_freq, hy_f_w2, hy_f_b2, hy_f_w3, hy_decay, hy_bias, ret_decay_fwd, ret_decay_bwd, ret_gn_g, w_hy_o, w_ret_o, w_out, norm2_g, w_router, w_e_gate, w_e_up, w_e_down, norm_f_g):
    b, s, d = x_prompt.shape
    bd, sd, _ = x_sample.shape
    depth = w_mod.shape[0]
    assert (b * s) % TOKEN_TILE == 0 and sd % TOKEN_TILE == 0 and b % 2 == 0 and bd % 2 == 0
    ncond = -(-(bd + 1) // 8) * 8
    lat_groups = bd if (TOKEN_TILE % (16 * bd) == 0 and sd % (TOKEN_TILE // bd) == 0) else 1
    assert lat_groups == bd or bd == 1
    conds = jnp.concatenate([c, c_ctx[None], jnp.zeros((ncond - bd - 1, d), F32)], axis=0)
    ctx_cond = lambda i: bd
    lat_cond = lambda i: i // (sd // TOKEN_TILE)
    xp = x_prompt.reshape(b * s, d)
    xs = x_sample.reshape(bd * sd, d)
    new_f, new_b = [], []
    for l in range(depth):
        mod = _modulation(conds, w_mod[l], b_mod[l])
        mods = [mod[:, k * d:(k + 1) * d].reshape(ncond, 1, d) for k in range(6)]
        p = {"norm1_g": norm1_g[l], "w_in": w_in[l].astype(BF16), "hy_conv_w": hy_conv_w[l],
             "hy_conv_b": hy_conv_b[l], "hy_f_w1": hy_f_w1[l], "hy_f_b1": hy_f_b1[l], "hy_f_freq": hy_f_freq[l],
             "hy_f_w2": hy_f_w2[l], "hy_f_b2": hy_f_b2[l], "hy_f_w3": hy_f_w3[l], "hy_decay": hy_decay[l],
             "hy_bias": hy_bias[l], "ret_decay_fwd": ret_decay_fwd[l], "ret_decay_bwd": ret_decay_bwd[l],
             "ret_gn_g": ret_gn_g[l], "w_hy_o": w_hy_o[l].astype(BF16), "w_ret_o": w_ret_o[l].astype(BF16),
             "w_out": w_out[l].astype(BF16), "norm2_g": norm2_g[l], "w_router": w_router[l],
             "w_e_gate": w_e_gate[l], "w_e_up": w_e_up[l], "w_e_down": w_e_down[l]}
        final_g = norm_f_g if l == depth - 1 else None
        routed_p, back_p, (s_f, s_b) = _layer_front(xp, b, s, mods, ctx_cond, 1, bd, p, None, None, False, True)
        new_f.append(s_f)
        new_b.append(s_b)
        routed_s, back_s, _ = _layer_front(xs, bd, sd, mods, lat_cond, lat_groups, 0, p, state_ret_fwd[:, l],
                                           state_ret_bwd[:, l], True, False)
        ye_p, ye_s = _experts([routed_p, routed_s], p["w_e_gate"], p["w_e_up"], p["w_e_down"])
        xp = _layer_back(back_p, ye_p, final_g)
        xs = _layer_back(back_s, ye_s, final_g)
    y_prompt = xp.reshape(b, s, d)
    y_sample = xs.reshape(bd, sd, d)
    return (y_prompt, y_sample, jnp.stack(new_f, axis=1), jnp.stack(new_b, axis=1))
```

```python
import functools
import math

import numpy as np
import jax
import jax.numpy as jnp
from jax import lax
from jax.experimental import pallas as pl
from jax.experimental.pallas import tpu as pltpu

F32 = jnp.float32
BF16 = jnp.bfloat16
HIGHEST = lax.Precision.HIGHEST

EPS = 1e-6
D_HYENA = 512
D_RET = 512
N_RET_HEADS = 4
RET_HEAD_DIM = 128
RET_CHUNK = 256
GRID_W = 64
FILTER_EMB = 33
ROPE_BASE = 10000.0
N_EXPERTS = 16
EC_CAPACITY_FACTOR = 2

TOKEN_TILE = 512
SLOT_ROWS = 96
ROW_ALIGN = 16
HALO_ROWS = 16
FFT_MINOR = 64
FFT_GROUP = 16
VMEM_LIMIT = 56 * 1024 * 1024


def _cparams(sem):
    return pltpu.CompilerParams(dimension_semantics=sem, vmem_limit_bytes=VMEM_LIMIT)


def _silu(x):
    return x * jax.nn.sigmoid(x)


def _mod_kernel(c_ref, w_ref, b_ref, o_ref):
    s = _silu(c_ref[...])
    o_ref[...] = jnp.dot(s, w_ref[...], preferred_element_type=F32, precision=HIGHEST) + b_ref[...]


def _modulation(conds, w_mod, b_mod):
    nc, d = conds.shape
    return pl.pallas_call(
        _mod_kernel,
        grid=(6,),
        in_specs=[pl.BlockSpec((nc, d), lambda j: (0, 0)),
                  pl.BlockSpec((d, d), lambda j: (0, j)),
                  pl.BlockSpec((1, d), lambda j: (0, j))],
        out_specs=pl.BlockSpec((nc, d), lambda j: (0, j)),
        out_shape=jax.ShapeDtypeStruct((nc, 6 * d), F32),
        compiler_params=_cparams(("arbitrary",)),
        name="modulation",
    )(conds, w_mod, b_mod.reshape(1, -1))


def _inproj_kernel(*refs, splits, rope, seq_len, halo):
    refs = list(refs)
    x_ref = refs.pop(0)
    if halo:
        xp_ref, xn_ref = refs.pop(0), refs.pop(0)
    g_ref, sc_ref, sh_ref, w_ref, cw_ref, cb_ref = refs[:6]
    del refs[:6]
    if rope:
        cos_ref, sin_ref = refs.pop(0), refs.pop(0)
    u_ref, x2c_ref, qkvg_ref, gates_ref, cv_sc, cx_sc = refs

    def normed(ref):
        x = ref[...]
        ms = jnp.mean(x * x, axis=-1, keepdims=True)
        h = x * lax.rsqrt(ms + EPS) * g_ref[...]
        return (h * (1.0 + sc_ref[0]) + sh_ref[0]).astype(BF16)

    hb = normed(x_ref)
    tm = hb.shape[0]
    nz, nq, ng = splits
    cw = 512
    dh = RET_HEAD_DIM
    hr = HALO_ROWS
    if halo:
        t0 = pl.program_id(0) * tm
        hp = jnp.where(t0 % seq_len != 0, normed(xp_ref), jnp.zeros((hr, hb.shape[1]), BF16))
        hn = jnp.where((t0 + tm) % seq_len != 0, normed(xn_ref), jnp.zeros((hr, hb.shape[1]), BF16))
        hext = jnp.concatenate([hp, hb, hn], axis=0)

    piece = tm if halo else seq_len
    pitch = piece + hr
    if not halo:
        for s in range(tm // piece + 1):
            cx_sc[s * pitch:s * pitch + hr, :] = jnp.zeros((hr, cw), F32)

    def conv(c0):
        taps = cw_ref[:, c0:c0 + cw]
        bias = cb_ref[:, c0:c0 + cw]
        if halo:
            cx_sc[...] = jnp.dot(hext, w_ref[:, c0:c0 + cw], preferred_element_type=F32)
        else:
            acc = jnp.dot(hb, w_ref[:, c0:c0 + cw], preferred_element_type=F32)
            for s in range(tm // piece):
                cx_sc[hr + s * pitch:hr + s * pitch + piece, :] = acc[s * piece:(s + 1) * piece]
        outs = []
        for s in range(tm // piece):
            r0 = hr + s * pitch
            outs.append(bias + cx_sc[r0 - 1:r0 - 1 + piece, :] * taps[0:1] + cx_sc[r0:r0 + piece, :] * taps[1:2]
                        + cx_sc[r0 + 1:r0 + 1 + piece, :] * taps[2:3])
        return outs[0] if len(outs) == 1 else jnp.concatenate(outs, axis=0)

    cv_sc[...] = conv(0)
    u_ref[...] = (cv_sc[...] * conv(D_HYENA)).astype(BF16)
    x2c_ref[...] = conv(2 * D_HYENA).astype(BF16)

    for c0 in range(nz, nz + nq + ng, cw):
        acc = jnp.dot(hb, w_ref[:, c0:c0 + cw], preferred_element_type=F32)
        if c0 < nz + nq:
            part = (c0 - nz) // D_RET
            if part == 1:
                acc = acc * (dh ** -0.5)
            if rope and part < 2:
                lane = lax.broadcasted_iota(jnp.int32, (acc.shape[0], dh), 1)
                swap_hi = (lane % (dh // 2)) < (dh // 4)
                cs, sn = cos_ref[...], sin_ref[...]
                heads = []
                for hh in range(cw // dh):
                    xh = acc[:, hh * dh:(hh + 1) * dh]
                    rot = jnp.where(swap_hi, pltpu.roll(xh, dh - dh // 4, axis=1), pltpu.roll(xh, dh // 4, axis=1))
                    heads.append(xh * cs + rot * sn)
                acc = jnp.concatenate(heads, axis=1)
            qkvg_ref[:, c0 - nz:c0 - nz + cw] = acc.astype(BF16)
        else:
            gates_ref[:, c0 - nz - nq:c0 - nz - nq + cw] = jax.nn.sigmoid(acc).astype(BF16)


def _in_projection(x, norm_g, sc, sh, w_in_bf, conv_w, conv_b, cond_of_tile, seq_len, rope):
    t, d = x.shape
    splits = (3 * D_HYENA, 4 * D_RET, 2 * d)
    assert D_RET == 512 and D_HYENA == 512
    tm = TOKEN_TILE
    assert seq_len % tm == 0 or tm % seq_len == 0
    halo = seq_len > tm
    row = lambda i: (i, 0)
    cond = lambda i: (cond_of_tile(i), 0, 0)
    full = lambda a: pl.BlockSpec(a.shape, lambda i: (0, 0))
    args, specs = [x], [pl.BlockSpec((tm, d), row)]
    if halo:
        hb_per_tile = tm // HALO_ROWS
        last = t // HALO_ROWS - 1
        args += [x, x]
        specs += [pl.BlockSpec((HALO_ROWS, d), lambda i: (jnp.maximum(i * hb_per_tile - 1, 0), 0)),
                  pl.BlockSpec((HALO_ROWS, d), lambda i: (jnp.minimum((i + 1) * hb_per_tile, last), 0))]
    cbias = conv_b.reshape(1, -1)
    args += [norm_g.reshape(1, d), sc, sh, w_in_bf, conv_w, cbias]
    specs += [pl.BlockSpec((1, d), lambda i: (0, 0)), pl.BlockSpec((1, 1, d), cond), pl.BlockSpec((1, 1, d), cond),
              full(w_in_bf), full(conv_w), full(cbias)]
    if rope:
        tiles_per_seq = seq_len // tm
        args += list(_rope_tables(seq_len))
        specs += [pl.BlockSpec((tm, RET_HEAD_DIM), lambda i: (i % tiles_per_seq, 0))] * 2
    return pl.pallas_call(
        functools.partial(_inproj_kernel, splits=splits, rope=rope, seq_len=seq_len, halo=halo),
        grid=(t // tm,),
        in_specs=specs,
        out_specs=[pl.BlockSpec((tm, D_HYENA), row),
                   pl.BlockSpec((tm, D_HYENA), row),
                   pl.BlockSpec((tm, splits[1]), row),
                   pl.BlockSpec((tm, splits[2]), row)],
        out_shape=[jax.ShapeDtypeStruct((t, D_HYENA), BF16),
                   jax.ShapeDtypeStruct((t, D_HYENA), BF16),
                   jax.ShapeDtypeStruct((t, splits[1]), BF16),
                   jax.ShapeDtypeStruct((t, splits[2]), BF16)],
        scratch_shapes=[pltpu.VMEM((tm, 512), F32),
                        pltpu.VMEM((tm + 2 * HALO_ROWS if halo else (tm // seq_len) * (seq_len + HALO_ROWS) + HALO_ROWS,
                                    512), F32)],
        compiler_params=_cparams(("parallel",)),
        name="in_projection",
    )(*args)


def _filter_features(seq_len):
    t = np.linspace(0.0, 1.0, seq_len, dtype=np.float32)[:, None]
    bands = (FILTER_EMB - 1) // 2
    w = (np.float32(2.0 * math.pi) * np.arange(seq_len, dtype=np.float32)) / np.float32(seq_len)
    f = np.linspace(1e-4, bands - 1, bands, dtype=np.float32)
    ang = (w[:, None] * f[None, :]).astype(np.float64)
    z = np.concatenate([t, np.cos(ang), -np.sin(ang)], axis=-1).astype(np.float32)
    return np.pad(z, ((0, 0), (0, 128 - FILTER_EMB)))


def _filter_kernel(z_ref, w1_ref, b1_ref, fr_ref, w2_ref, b2_ref, w3_ref, dec_ref, o_ref):
    z = z_ref[...]
    fr = fr_ref[...]
    dot = functools.partial(jnp.dot, preferred_element_type=F32, precision=HIGHEST)
    h = jnp.sin(fr * (dot(z, w1_ref[...]) + b1_ref[...]))
    h = jnp.sin(fr * (dot(h, w2_ref[...]) + b2_ref[...]))
    split = lambda a: (a.astype(BF16), (a - a.astype(BF16).astype(F32)).astype(BF16))
    (h_hi, h_lo), (w_hi, w_lo) = split(h), split(w3_ref[...])
    bdot = functools.partial(jnp.dot, preferred_element_type=F32)
    h = (bdot(h_hi, w_hi) + bdot(h_hi, w_lo) + bdot(h_lo, w_hi)) * jnp.exp(-z[:, 0:1] * jnp.abs(dec_ref[...]))
    rows = h.shape[0]
    grow = pl.program_id(0) * rows + lax.broadcasted_iota(jnp.int32, (rows, 1), 0)
    o_ref[0] = h[:, :D_HYENA].astype(o_ref.dtype)
    o_ref[1] = jnp.zeros((rows, D_HYENA), o_ref.dtype)
    o_ref[2] = jnp.where(grow == 0, 0.0, h[:, D_HYENA:]).astype(o_ref.dtype)
    o_ref[3] = jnp.zeros((rows, D_HYENA), o_ref.dtype)


def _hyena_filters(seq_len, p, out_dtype):
    z = jnp.asarray(_filter_features(seq_len))
    w1 = jnp.pad(p["hy_f_w1"], ((0, 128 - FILTER_EMB), (0, 0)))
    rows = min(seq_len, 512)
    full = lambda a: pl.BlockSpec(a.shape, lambda i: (0,) * a.ndim)
    ops = [w1, p["hy_f_b1"].reshape(1, -1), p["hy_f_freq"].reshape(1, -1), p["hy_f_w2"],
           p["hy_f_b2"].reshape(1, -1), p["hy_f_w3"], p["hy_decay"].reshape(1, -1)]
    return pl.pallas_call(
        _filter_kernel,
        grid=(seq_len // rows,),
        in_specs=[pl.BlockSpec((rows, 128), lambda i: (i, 0))] + [full(a) for a in ops],
        out_specs=pl.BlockSpec((4, rows, D_HYENA), lambda i: (0, i, 0)),
        out_shape=jax.ShapeDtypeStruct((4, seq_len, D_HYENA), out_dtype),
        compiler_params=_cparams(("arbitrary",)),
        name="hyena_filter",
    )(z, *ops)


def _stacked_dft(n_out, n_in, modulus, scale=1.0):
    k = np.arange(n_out, dtype=np.int64)[:, None]
    n = np.arange(n_in, dtype=np.int64)[None, :]
    th = ((k * n) % modulus) * (2.0 * math.pi / modulus)
    c, s = np.cos(th), np.sin(th)
    fwd = np.block([[c, s], [-s, c]]).astype(np.float32)
    inv = (np.block([[c.T, -s.T], [s.T, c.T]]) * scale).astype(np.float32)
    return fwd, inv


def _second_level_tables(n_total, n1, n2):
    k1 = np.arange(n1, dtype=np.int64)[:, None, None]
    k2 = np.arange(n2, dtype=np.int64)[None, :, None]
    m = np.arange(n2, dtype=np.int64)[None, None, :]
    th = ((m * (k1 + n1 * k2)) % n_total) * (2.0 * math.pi / n_total)
    c, s = np.cos(th), np.sin(th)
    g = np.concatenate([np.concatenate([c, s], axis=2), np.concatenate([-s, c], axis=2)], axis=1)
    return g.astype(np.float32), np.swapaxes(g, 1, 2).astype(np.float32)


def _skewed(n):
    return n + 1


def _fft1_kernel(h_ref, f_ref, a_ref, u_sc, t_sc):
    n2 = FFT_MINOR
    _, seq_len, cb = h_ref.shape
    h1 = seq_len // n2
    nk = a_ref.shape[1]
    up, tp = _skewed(n2), _skewed(nk)
    for b in range(2):
        for q in range(h1):
            u_sc[b, q * up:q * up + n2, :] = h_ref[b, q * n2:(q + 1) * n2, :].astype(F32)

    def fwd(gi, carry):
        m0 = gi * FFT_GROUP
        cols = []
        for d in range(FFT_GROUP):
            xr = u_sc[0, pl.ds(m0 + d, h1, stride=up), :]
            xi = u_sc[1, pl.ds(m0 + d, h1, stride=up), :]
            cols.append(jnp.concatenate([xr, xi], axis=0))
        x = jnp.concatenate(cols, axis=1).astype(BF16)
        res = jnp.dot(f_ref[...], x, preferred_element_type=F32)
        for d in range(FFT_GROUP):
            t_sc[pl.ds((m0 + d) * tp, nk), :] = res[:, d * cb:(d + 1) * cb]
        return carry

    lax.fori_loop(0, n2 // FFT_GROUP, fwd, 0, unroll=2)

    def transpose(k, carry):
        a_ref[0, k] = t_sc[pl.ds(k, n2, stride=tp), :].astype(a_ref.dtype)
        return carry

    lax.fori_loop(0, nk, transpose, 0, unroll=8)


def _fft_first_level(h, f1):
    b, seq_len, c = h.shape
    cb = 128
    return pl.pallas_call(
        _fft1_kernel,
        grid=(b // 2, c // cb),
        in_specs=[pl.BlockSpec((2, seq_len, cb), lambda i, j: (i, 0, j)),
                  pl.BlockSpec(f1.shape, lambda i, j: (0, 0))],
        out_specs=pl.BlockSpec((1, f1.shape[0], FFT_MINOR, cb), lambda i, j: (i, 0, 0, j)),
        out_shape=jax.ShapeDtypeStruct((b // 2, f1.shape[0], FFT_MINOR, c), BF16),
        scratch_shapes=[pltpu.VMEM((2, (seq_len // FFT_MINOR) * _skewed(FFT_MINOR), cb), F32),
                        pltpu.VMEM((FFT_MINOR * _skewed(f1.shape[0]), cb), F32)],
        compiler_params=_cparams(("parallel", "parallel")),
        name="hyena_dft_level1",
    )(h, f1)


def _fft_s2f_kernel(a_ref, g_ref, kr_ref, ki_ref):
    _, _, kb, n2, c = a_ref.shape
    for kk in range(kb):
        g = g_ref[kk]
        hf = jnp.dot(g, a_ref[0, :, kk].reshape(2 * n2, c), preferred_element_type=F32)
        hb = jnp.dot(g, a_ref[1, :, kk].reshape(2 * n2, c), preferred_element_type=F32)
        kr_ref[kk] = hf[:n2] + hb[:n2]
        ki_ref[kk] = hf[n2:] - hb[n2:]


def _fft_filter_second_level(a, g, kb=16):
    _, _, n1, n2, c = a.shape
    spec = pl.BlockSpec((kb, n2, c), lambda i: (i, 0, 0))
    return pl.pallas_call(
        _fft_s2f_kernel,
        grid=(n1 // kb,),
        in_specs=[pl.BlockSpec((2, 2, kb, n2, c), lambda i: (0, 0, i, 0, 0)),
                  pl.BlockSpec((kb, 2 * n2, 2 * n2), lambda i: (i, 0, 0))],
        out_specs=[spec, spec],
        out_shape=[jax.ShapeDtypeStruct((n1, n2, c), F32)] * 2,
        compiler_params=_cparams(("parallel",)),
        name="hyena_filter_spectrum",
    )(a, g)


def _fft_s2_kernel(a_hbm, g_ref, gt_ref, kr_ref, ki_ref, b_ref, r_sc, a_buf, a_sem):
    ring, npairs, _, kb, n2, cb = a_buf.shape
    nq = cb // 128
    rp = _skewed(2 * n2)
    nj = pl.num_programs(1)
    step = pl.program_id(0) * nj + pl.program_id(1)
    total = pl.num_programs(0) * nj

    def a_copy(s):
        src = a_hbm.at[:, :, pl.ds((s // nj) * kb, kb), :, pl.ds(pl.multiple_of((s % nj) * cb, cb), cb)]
        return pltpu.make_async_copy(src, a_buf.at[s % ring], a_sem.at[s % ring])

    @pl.when(step == 0)
    def _():
        for s in range(ring - 1):
            a_copy(s).start()

    @pl.when(step + ring - 1 < total)
    def _():
        a_copy(step + ring - 1).start()

    a_copy(step).wait()
    a_ref = a_buf.at[step % ring]
    for kk in range(kb):
        kr, ki = kr_ref[kk], ki_ref[kk]
        for p in range(npairs):
            x = jnp.dot(g_ref[kk], a_ref[p, :, kk].reshape(2 * n2, cb), preferred_element_type=F32)
            xr, xi = x[:n2], x[n2:]
            y = jnp.concatenate([xr * kr - xi * ki, xr * ki + xi * kr], axis=0).astype(BF16)
            res = jnp.dot(gt_ref[kk], y, preferred_element_type=F32)
            for q in range(nq):
                r_sc[p * nq + q, kk * rp:kk * rp + 2 * n2, :] = res[:, q * 128:(q + 1) * 128]

    for p in range(npairs):
        def gather(row):
            return jnp.concatenate([r_sc[p * nq + q, pl.ds(row, kb, stride=rp), :] for q in range(nq)], axis=1)

        def transpose(m, carry):
            b_ref[p, 0, m] = gather(m).astype(b_ref.dtype)
            b_ref[p, 1, m] = gather(n2 + m).astype(b_ref.dtype)
            return carry

        lax.fori_loop(0, n2, transpose, 0, unroll=8)


def _fft_second_level(a, g, gt, kr, ki):
    p, _, n1, n2, c = a.shape
    kb, cb = 16, 256
    gspec = pl.BlockSpec((kb, 2 * n2, 2 * n2), lambda i, j: (i, 0, 0))
    kspec = pl.BlockSpec((kb, n2, cb), lambda i, j: (i, 0, j))
    return pl.pallas_call(
        _fft_s2_kernel,
        grid=(n1 // kb, c // cb),
        in_specs=[pl.BlockSpec(memory_space=pl.ANY), gspec, gspec, kspec, kspec],
        out_specs=pl.BlockSpec((p, 2, n2, kb, cb), lambda i, j: (0, 0, 0, i, j)),
        out_shape=jax.ShapeDtypeStruct((p, 2, n2, n1, c), BF16),
        scratch_shapes=[pltpu.VMEM((p * cb // 128, kb * _skewed(2 * n2), 128), F32),
                        pltpu.VMEM((3, p, 2, kb, n2, cb), BF16),
                        pltpu.SemaphoreType.DMA((3,))],
        compiler_params=_cparams(("arbitrary", "arbitrary")),
        name="hyena_dft_level2",
    )(a, g, gt, kr, ki)


def _fft_s3_kernel(b_hbm, f_ref, u_ref, x2_ref, bias_ref, o_ref, t_sc, b_buf, b_sem):
    ring, _, n2, n1, cb = b_buf.shape
    nj = pl.num_programs(1)
    step = pl.program_id(0) * nj + pl.program_id(1)
    total = pl.num_programs(0) * nj

    def b_copy(s):
        src = b_hbm.at[s // nj, :, :, :, pl.ds(pl.multiple_of((s % nj) * cb, cb), cb)]
        return pltpu.make_async_copy(src, b_buf.at[s % ring], b_sem.at[s % ring])

    @pl.when(step == 0)
    def _():
        for s in range(ring - 1):
            b_copy(s).start()

    @pl.when(step + ring - 1 < total)
    def _():
        b_copy(step + ring - 1).start()

    b_copy(step).wait()
    b_ref = b_buf.at[step % ring]
    nr = f_ref.shape[0]
    h1 = nr // 2
    tp = _skewed(nr)

    def inv(gi, carry):
        m0 = gi * FFT_GROUP
        x = jnp.concatenate([b_ref[:, m0 + d].reshape(2 * n1, cb) for d in range(FFT_GROUP)], axis=1)
        res = jnp.dot(f_ref[...], x, preferred_element_type=F32)
        for d in range(FFT_GROUP):
            t_sc[pl.ds((m0 + d) * tp, nr), :] = res[:, d * cb:(d + 1) * cb]
        return carry

    lax.fori_loop(0, n2 // FFT_GROUP, inv, 0, unroll=2)

    bias = bias_ref[...]
    for b in range(2):
        def finish(q, carry):
            rows = pl.ds(pl.multiple_of(q * n2, n2), n2)
            conv = t_sc[pl.ds(b * h1 + q, n2, stride=tp), :]
            u = u_ref[b, rows, :].astype(F32)
            o_ref[b, rows, :] = ((conv + u * bias) * x2_ref[b, rows, :].astype(F32)).astype(o_ref.dtype)
            return carry

        lax.fori_loop(0, h1, finish, 0, unroll=4)


def _fft_last_level(bt, f1inv, u, x2c, bias):
    p, _, n2, n1, c = bt.shape
    b, seq_len, _ = u.shape
    cb = 128
    uspec = pl.BlockSpec((2, seq_len, cb), lambda i, j: (i, 0, j))
    return pl.pallas_call(
        _fft_s3_kernel,
        grid=(p, c // cb),
        in_specs=[pl.BlockSpec(memory_space=pl.ANY),
                  pl.BlockSpec(f1inv.shape, lambda i, j: (0, 0)),
                  uspec, uspec,
                  pl.BlockSpec((1, cb), lambda i, j: (0, j))],
        out_specs=uspec,
        out_shape=jax.ShapeDtypeStruct(u.shape, BF16),
        scratch_shapes=[pltpu.VMEM((n2 * _skewed(f1inv.shape[0]), cb), F32),
                        pltpu.VMEM((3, 2, n2, n1, cb), BF16),
                        pltpu.SemaphoreType.DMA((3,))],
        compiler_params=_cparams(("arbitrary", "arbitrary")),
        name="hyena_dft_inverse",
    )(bt, f1inv, u, x2c, bias)


def _kf_direct_kernel(h_ref, f_ref, kr_ref, ki_ref):
    n = kr_ref.shape[0]
    dot = functools.partial(jnp.dot, preferred_element_type=F32, precision=HIGHEST)
    hf = dot(f_ref[...], h_ref[0])
    hb = dot(f_ref[...], h_ref[2])
    kr_ref[...] = hf[:n] + hb[:n]
    ki_ref[...] = hf[n:] - hb[n:]


def _hyena_direct_kernel(u_ref, x2_ref, f_ref, fi_ref, kr_ref, ki_ref, bias_ref, o_ref):
    _, two, seq_len, cb = u_ref.shape
    n = kr_ref.shape[0]
    ub = u_ref[0].reshape(two * seq_len, cb)
    x = jnp.dot(f_ref[...], ub, preferred_element_type=F32)
    xr, xi = x[:n], x[n:]
    kr, ki = kr_ref[...], ki_ref[...]
    y = jnp.concatenate([xr * kr - xi * ki, xr * ki + xi * kr], axis=0).astype(BF16)
    conv = jnp.dot(fi_ref[...], y, preferred_element_type=F32)
    x2 = x2_ref[0].reshape(two * seq_len, cb).astype(F32)
    o = (conv + ub.astype(F32) * bias_ref[...]) * x2
    o_ref[0] = o.reshape(two, seq_len, cb).astype(o_ref.dtype)


def _hyena_long_conv(u, x2c, p):
    b, seq_len, _ = u.shape
    c = D_HYENA
    n = 2 * seq_len
    bias = p["hy_bias"].reshape(1, c)
    hh = _hyena_filters(seq_len, p, F32)
    if seq_len <= 512:
        fwd, inv = _stacked_dft(n, seq_len, n, scale=1.0 / n)
        kr, ki = pl.pallas_call(
            _kf_direct_kernel,
            out_shape=[jax.ShapeDtypeStruct((n, c), F32)] * 2,
            compiler_params=_cparams(None),
            name="hyena_filter_spectrum_direct",
        )(hh, fwd[:, :seq_len])
        cb = 512
        pair = lambda a: a.reshape(b // 2, 2, seq_len, c)
        uspec = pl.BlockSpec((1, 2, seq_len, cb), lambda i, j: (i, 0, 0, j))
        kspec = pl.BlockSpec((n, cb), lambda i, j: (0, j))
        out = pl.pallas_call(
            _hyena_direct_kernel,
            grid=(b // 2, c // cb),
            in_specs=[uspec, uspec,
                      pl.BlockSpec(fwd.shape, lambda i, j: (0, 0)),
                      pl.BlockSpec(inv.shape, lambda i, j: (0, 0)),
                      kspec, kspec,
                      pl.BlockSpec((1, cb), lambda i, j: (0, j))],
            out_specs=uspec,
            out_shape=jax.ShapeDtypeStruct((b // 2, 2, seq_len, c), BF16),
            compiler_params=_cparams(("parallel", "parallel")),
            name="hyena_dft_direct",
        )(pair(u), pair(x2c), jnp.asarray(fwd).astype(BF16), jnp.asarray(inv).astype(BF16), kr, ki, bias)
        return out.reshape(b, seq_len, c)

    n2 = FFT_MINOR
    n1 = n // n2
    h1 = seq_len // n2
    f1, f1inv = _stacked_dft(n1, h1, n1, scale=1.0 / n)
    f1, f1inv = jnp.asarray(f1).astype(BF16), jnp.asarray(f1inv).astype(BF16)
    g, gt = _second_level_tables(n, n1, n2)
    g, gt = jnp.asarray(g).astype(BF16), jnp.asarray(gt).astype(BF16)
    ha = _fft_first_level(hh, f1)
    kr, ki = _fft_filter_second_level(ha.reshape(2, 2, n1, n2, c), g)
    a = _fft_first_level(u, f1)
    bt = _fft_second_level(a.reshape(b // 2, 2, n1, n2, c), g, gt, kr, ki)
    return _fft_last_level(bt, f1inv, u, x2c, bias)


def _rope_tables(seq_len):
    half = RET_HEAD_DIM // 2
    nf = half // 2
    t = np.arange(seq_len)
    inv = ROPE_BASE ** (-np.arange(nf, dtype=np.float64) / nf)
    ar = (t // GRID_W)[:, None] * inv[None, :]
    ac = (t % GRID_W)[:, None] * inv[None, :]
    cos = np.concatenate([np.cos(ar), np.cos(ar), np.cos(ac), np.cos(ac)], axis=-1)
    sin = np.concatenate([-np.sin(ar), np.sin(ar), -np.sin(ac), np.sin(ac)], axis=-1)
    return cos.astype(np.float32), sin.astype(np.float32)


def _log_sigmoid(x):
    return jnp.minimum(x, 0.0) - jnp.log1p(jnp.exp(-jnp.abs(x)))


def _retention_kernel(*refs, has_init, emit_state, cpb):
    refs = list(refs)
    q_ref, k_ref, v_ref, g_ref, dec_ref, gn_ref = refs[:6]
    del refs[:6]
    if has_init:
        s0f_ref, s0b_ref = refs[:2]
        del refs[:2]
    o_ref = refs.pop(0)
    if emit_state:
        sf_out, sb_out = refs[:2]
        del refs[:2]
    sf_ref, sb_ref, sball_ref, mask_sc, scale_sc = refs

    c = RET_CHUNK
    dh = RET_HEAD_DIM
    nh = N_RET_HEADS
    phase = pl.program_id(1)
    j = pl.program_id(2)
    nb = pl.num_programs(2)

    diff = (lax.broadcasted_iota(jnp.int32, (c, c), 0) - lax.broadcasted_iota(jnp.int32, (c, c), 1)).astype(F32)
    ri = lax.broadcasted_iota(jnp.int32, (c, dh), 0).astype(F32)

    def head_consts(h):
        lgf = _log_sigmoid(dec_ref[0, h])[0:1, :]
        lgb = _log_sigmoid(dec_ref[1, h])[0:1, :]
        return lgf, lgb

    def chunk_wide(lg):
        return jnp.concatenate([lg] * (c // dh), axis=1)

    bb = q_ref.shape[0]
    heads = [(bi, h) for bi in range(bb) for h in range(nh)]

    def load(ref, bi, r0, h):
        return ref[bi, r0:r0 + c, h * dh:(h + 1) * dh]

    dn_t = (((0,), (0,)), ((), ()))
    dn_nt = (((1,), (1,)), ((), ()))

    @pl.when(jnp.logical_and(pl.program_id(0) == 0, jnp.logical_and(phase == 0, j == 0)))
    def _decay_tables():
        for h in range(nh):
            lgf, lgb = head_consts(h)
            mask_sc[h] = (jnp.where(diff >= 0, jnp.exp(chunk_wide(lgf) * jnp.maximum(diff, 0.0)), 0.0)
                          + jnp.where(diff <= 0, jnp.exp(chunk_wide(lgb) * jnp.maximum(-diff, 0.0)), 0.0))
            scale_sc[h, 0] = jnp.exp(lgb * ri)
            scale_sc[h, 1] = jnp.exp(lgf * (ri + 1.0))
            scale_sc[h, 2] = jnp.exp(lgb * (float(c) - ri))
            scale_sc[h, 3] = jnp.exp(lgf * (float(c - 1) - ri))

    @pl.when(phase == 0)
    def _backward_sweep():
        @pl.when(j == 0)
        def _():
            for bi, h in heads:
                sb_ref[bi * nh + h] = s0b_ref[bi, h] if has_init else jnp.zeros((dh, dh), F32)

        blk = nb - 1 - j
        for bi, h in heads:
            hs = bi * nh + h
            _, lgb = head_consts(h)
            zeta_b = scale_sc[h, 0]
            cdec_b = jnp.exp(lgb * float(c))
            for cc in reversed(range(cpb)):
                r0 = cc * c
                n = blk * cpb + cc
                s = sb_ref[hs]
                sball_ref[n, hs] = s.astype(BF16)
                kz = (load(k_ref, bi, r0, h).astype(F32) * zeta_b).astype(BF16)
                vv = load(v_ref, bi, r0, h)
                sb_ref[hs] = cdec_b * s + lax.dot_general(kz, vv, dn_t, preferred_element_type=F32)

        if emit_state:
            @pl.when(j == nb - 1)
            def _():
                for bi, h in heads:
                    sb_out[bi, h] = sb_ref[bi * nh + h]

    @pl.when(phase == 1)
    def _forward_sweep():
        @pl.when(j == 0)
        def _():
            for bi, h in heads:
                sf_ref[bi * nh + h] = s0f_ref[bi, h] if has_init else jnp.zeros((dh, dh), F32)

        for bi, h in heads:
            hs = bi * nh + h
            lgf, _ = head_consts(h)
            mask = mask_sc[h]
            xi_f, xi_b, zeta_f = scale_sc[h, 1], scale_sc[h, 2], scale_sc[h, 3]
            cdec_f = jnp.exp(lgf * float(c))
            gn = gn_ref[:, h * dh:(h + 1) * dh]
            for cc in range(cpb):
                r0 = cc * c
                n = j * cpb + cc
                qb = load(q_ref, bi, r0, h)
                kb = load(k_ref, bi, r0, h)
                vv = load(v_ref, bi, r0, h)
                gate = load(g_ref, bi, r0, h).astype(F32)
                sc = lax.dot_general(qb, kb, dn_nt, preferred_element_type=F32)
                inner = jnp.dot((sc * mask).astype(BF16), vv, preferred_element_type=F32)
                s = sf_ref[hs]
                q = qb.astype(F32)
                lhs = jnp.concatenate([q * xi_f, q * xi_b], axis=1).astype(BF16)
                rhs = jnp.concatenate([s.astype(BF16), sball_ref[n, hs]], axis=0)
                o = inner + jnp.dot(lhs, rhs, preferred_element_type=F32)
                mu = jnp.mean(o, axis=-1, keepdims=True)
                d = o - mu
                var = jnp.mean(d * d, axis=-1, keepdims=True)
                y = d * lax.rsqrt(var + EPS) * gn * _silu(gate)
                o_ref[bi, r0:r0 + c, h * dh:(h + 1) * dh] = y.astype(o_ref.dtype)
                kz = (kb.astype(F32) * zeta_f).astype(BF16)
                sf_ref[hs] = cdec_f * s + lax.dot_general(kz, vv, dn_t, preferred_element_type=F32)

        if emit_state:
            @pl.when(j == nb - 1)
            def _():
                for bi, h in heads:
                    sf_out[bi, h] = sf_ref[bi * nh + h]


def _retention(qkvg, dec_f, dec_b, gn_g, s0_f, s0_b, emit_state):
    b, seq_len, _ = qkvg.shape
    nh, dh, c = N_RET_HEADS, RET_HEAD_DIM, RET_CHUNK
    rb = min(seq_len, 1024)
    nb = seq_len // rb
    cpb = rb // c
    bb = max(1, min(4, 1024 // seq_len))
    while b % bb:
        bb //= 2
    has_init = s0_f is not None
    dec = jnp.broadcast_to(jnp.stack([dec_f, dec_b])[:, :, None, None], (2, nh, 8, 128)).astype(F32)
    kv_blk = lambda i, p, j: jnp.where(p == 0, nb - 1 - j, j)
    q_blk = lambda i, p, j: jnp.where(p == 0, 0, j)
    in_specs = [pl.BlockSpec((bb, rb, D_RET), lambda i, p, j: (i, q_blk(i, p, j), 0)),
                pl.BlockSpec((bb, rb, D_RET), lambda i, p, j: (i, kv_blk(i, p, j), 1)),
                pl.BlockSpec((bb, rb, D_RET), lambda i, p, j: (i, kv_blk(i, p, j), 2)),
                pl.BlockSpec((bb, rb, D_RET), lambda i, p, j: (i, q_blk(i, p, j), 3)),
                pl.BlockSpec((2, nh, 8, 128), lambda i, p, j: (0, 0, 0, 0)),
                pl.BlockSpec((1, D_RET), lambda i, p, j: (0, 0))]
    args = [qkvg, qkvg, qkvg, qkvg, dec, gn_g.reshape(1, -1)]
    sspec = pl.BlockSpec((bb, nh, dh, dh), lambda i, p, j: (i, 0, 0, 0))
    if has_init:
        in_specs += [sspec, sspec]
        args += [s0_f, s0_b]
    out_specs = [pl.BlockSpec((bb, rb, D_RET), lambda i, p, j: (i, q_blk(i, p, j), 0))]
    out_shape = [jax.ShapeDtypeStruct((b, seq_len, D_RET), BF16)]
    if emit_state:
        out_specs += [sspec, sspec]
        out_shape += [jax.ShapeDtypeStruct((b, nh, dh, dh), F32)] * 2
    return pl.pallas_call(
        functools.partial(_retention_kernel, has_init=has_init, emit_state=emit_state, cpb=cpb),
        grid=(b // bb, 2, nb),
        in_specs=in_specs,
        out_specs=out_specs,
        out_shape=out_shape,
        scratch_shapes=[pltpu.VMEM((bb * nh, dh, dh), F32), pltpu.VMEM((bb * nh, dh, dh), F32),
                        pltpu.VMEM((nb * cpb, bb * nh, dh, dh), BF16),
                        pltpu.VMEM((nh, c, c), F32), pltpu.VMEM((nh, 4, c, dh), F32)],
        compiler_params=_cparams(("arbitrary", "arbitrary", "arbitrary")),
        name="retention",
    )(*args)


def _outproj_kernel(yhy_ref, yret_ref, gates_ref, x_ref, g1_ref, sc_ref, sh_ref, ng_ref,
                    why_ref, wret_ref, wout_ref, wr_ref, x1_ref, h2_ref, aff_ref):
    g, m, d = x_ref.shape
    rows = lambda ref: ref[...].reshape(g * m, ref.shape[2])
    a = jnp.dot(rows(yhy_ref), why_ref[...], preferred_element_type=F32)
    b = jnp.dot(rows(yret_ref), wret_ref[...], preferred_element_type=F32)
    gates = rows(gates_ref)
    merged = gates[:, :d].astype(F32) * a + gates[:, d:].astype(F32) * b
    out = jnp.dot(merged.astype(BF16), wout_ref[...], preferred_element_type=F32)
    x1 = x_ref[...] + g1_ref[...] * out.reshape(g, m, d)
    x1_ref[...] = x1
    ms = jnp.mean(x1 * x1, axis=-1, keepdims=True)
    h = x1 * lax.rsqrt(ms + EPS) * ng_ref[...]
    h = (h * (1.0 + sc_ref[...]) + sh_ref[...]).reshape(g * m, d)
    h_hi = h.astype(BF16)
    h2_ref[...] = h_hi.reshape(g, m, d)
    h_lo = (h - h_hi.astype(F32)).astype(BF16)
    t = jnp.dot(h_hi, wr_ref[...], preferred_element_type=F32)
    logits = t[:, :128] + t[:, 128:] + jnp.dot(h_lo, wr_ref[:, :128], preferred_element_type=F32)
    lane = lax.broadcasted_iota(jnp.int32, logits.shape, 1)
    logits = jnp.where(lane < N_EXPERTS, logits, -jnp.inf)
    e = jnp.exp(logits - jnp.max(logits, axis=-1, keepdims=True))
    aff = e / jnp.sum(e, axis=-1, keepdims=True)
    aff_ref[0] = aff.T[:N_EXPERTS, :]


def _out_projection(y_hy, y_ret, gates, x, g1, sc2, sh2, norm2_g, w_hy_o, w_ret_o, w_out, w_router, cond_block):
    g, s, d = x.shape
    tm = TOKEN_TILE
    m = tm // g
    tok = lambda c: pl.BlockSpec((g, m, c), lambda i: (0, i, 0))
    cond = pl.BlockSpec((g, 1, d), lambda i: (cond_block, 0, 0))
    full = lambda a: pl.BlockSpec(a.shape, lambda i: (0, 0))
    wr = jnp.pad(w_router, ((0, 0), (0, 128 - N_EXPERTS)))
    wr_hi = wr.astype(BF16)
    wr = jnp.concatenate([wr_hi, (wr - wr_hi.astype(F32)).astype(BF16)], axis=1)
    return pl.pallas_call(
        _outproj_kernel,
        grid=(s // m,),
        in_specs=[tok(D_HYENA), tok(D_RET), tok(2 * d), tok(d), cond, cond, cond,
                  pl.BlockSpec((1, d), lambda i: (0, 0)),
                  full(w_hy_o), full(w_ret_o), full(w_out), full(wr)],
        out_specs=[tok(d), tok(d), pl.BlockSpec((1, N_EXPERTS, tm), lambda i: (i, 0, 0))],
        out_shape=[jax.ShapeDtypeStruct((g, s, d), F32), jax.ShapeDtypeStruct((g, s, d), BF16),
                   jax.ShapeDtypeStruct((s // m, N_EXPERTS, tm), F32)],
        compiler_params=_cparams(("parallel",)),
        name="out_projection_router",
    )(y_hy, y_ret, gates, x, g1, sc2, sh2, norm2_g.reshape(1, d), w_hy_o, w_ret_o, w_out, wr)


def _select_kernel(aff_ref, tri_ref, rank_ref, cnt_ref, start_ref, tot_ref, *, cap, idx_bits, groups):
    nt, ne, tm = aff_ref.shape
    a = aff_ref[...]

    def count(m):
        return jnp.sum(jnp.sum(m, axis=0, keepdims=True), axis=2, keepdims=True)

    def thr_step(s, thr):
        cand = thr | (1 << (30 - s))
        cnt = count(jnp.where(a >= pltpu.bitcast(cand, F32), 1.0, 0.0))
        return jnp.where(cnt >= float(cap), cand, thr)

    thr = pltpu.bitcast(lax.fori_loop(0, 31, thr_step, jnp.zeros((1, ne, 1), jnp.int32)), F32)
    gt = a > thr
    eq = a == thr
    need = float(cap) - count(jnp.where(gt, 1.0, 0.0))
    m = tm // groups
    tile = lax.broadcasted_iota(jnp.int32, (nt, 1, tm), 0)
    lane = lax.broadcasted_iota(jnp.int32, (nt, 1, tm), 2)
    idx = (lane // m) * (nt * m) + tile * m + lane % m

    def idx_step(s, lim):
        cand = lim | (1 << (idx_bits - 1 - s))
        cnt = count(jnp.where(eq, jnp.where(idx < cand, 1.0, 0.0), 0.0))
        return jnp.where(cnt < need, cand, lim)

    lim = lax.fori_loop(0, idx_bits, idx_step, jnp.zeros((1, ne, 1), jnp.int32))
    sel = jnp.where(gt, 1.0, jnp.where(eq, jnp.where(idx <= lim, 1.0, 0.0), 0.0))
    sel2 = sel.reshape(nt * ne, tm).astype(BF16)
    prefix = jnp.dot(sel2, tri_ref[...], preferred_element_type=F32)
    rank = jnp.where(sel2 > 0, prefix, -1.0).astype(jnp.int32)
    rank_ref[...] = rank.reshape(nt, ne, tm)
    ones = jnp.ones((tm, 128), BF16)
    cnt = jnp.dot(sel2, ones, preferred_element_type=F32).astype(jnp.int32).reshape(nt, ne, 128)
    cnt_ref[...] = cnt
    acc = jnp.zeros((ne, 128), jnp.int32)
    for t in range(nt):
        start_ref[t] = acc
        acc = acc + ((cnt[t] + (ROW_ALIGN - 1)) & (-ROW_ALIGN))
    tot_ref[...] = acc


def _select(aff, cap, groups):
    nt, ne, tm = aff.shape
    idx_bits = max(1, int(math.ceil(math.log2(nt * tm))))
    r = lax.broadcasted_iota(jnp.int32, (tm, tm), 0)
    c = lax.broadcasted_iota(jnp.int32, (tm, tm), 1)
    tri = (r < c).astype(BF16)
    rank, cnt, start, tot = pl.pallas_call(
        functools.partial(_select_kernel, cap=cap, idx_bits=idx_bits, groups=groups),
        out_shape=[jax.ShapeDtypeStruct((nt, ne, tm), jnp.int32),
                   jax.ShapeDtypeStruct((nt, ne, 128), jnp.int32),
                   jax.ShapeDtypeStruct((nt, ne, 128), jnp.int32),
                   jax.ShapeDtypeStruct((ne, 128), jnp.int32)],
        compiler_params=_cparams(None),
        name="expert_choice_select",
    )(aff, tri)
    return rank, cnt[:, :, 0], start[:, :, 0], tot[:, 0]


def _expert_block(cap):
    return 512 if cap >= 2048 else 256


def _used_rows(cap, nt):
    eb = _expert_block(cap)
    return -(-(cap + ROW_ALIGN * nt) // eb) * eb


def _list_rows(cap, nt):
    max_rounds = -(-TOKEN_TILE // SLOT_ROWS)
    return _used_rows(cap, nt) + max(_expert_block(cap), max_rounds * SLOT_ROWS)


def _num_rounds(cnt_sm, i):
    m = cnt_sm[i, 0]
    for e in range(1, N_EXPERTS):
        m = jnp.maximum(m, cnt_sm[i, e])
    return jnp.maximum((m + SLOT_ROWS - 1) // SLOT_ROWS, 1)


def _gather_kernel(start_sm, cnt_sm, tot_sm, h_ref, rank_ref, xs_hbm, stage, zbuf, sem):
    i = pl.program_id(0)
    ne = N_EXPERTS
    slot = i % 2
    rank = rank_ref[0]
    sub = lax.broadcasted_iota(jnp.int32, (SLOT_ROWS, rank.shape[1]), 0)

    def copy(s, e, off):
        return pltpu.make_async_copy(stage.at[s, pl.ds(e * SLOT_ROWS, SLOT_ROWS)],
                                     xs_hbm.at[e, pl.ds(off, SLOT_ROWS)], sem.at[e])

    def fill(r):
        h = h_ref[...].reshape(rank.shape[1], h_ref.shape[2])
        for e in range(ne):
            onehot = jnp.where(rank[e:e + 1, :] == sub + r * SLOT_ROWS, 1.0, 0.0).astype(BF16)
            stage[slot, e * SLOT_ROWS:(e + 1) * SLOT_ROWS, :] = jnp.dot(
                onehot, h, preferred_element_type=F32).astype(BF16)

    def start_all(r):
        for e in range(ne):
            copy(slot, e, pl.multiple_of(start_sm[i, e] + r * SLOT_ROWS, ROW_ALIGN)).start()

    def wait_all(s):
        for e in range(ne):
            copy(s, e, 0).wait()

    fill(0)

    @pl.when(i > 0)
    def _():
        wait_all(1 - slot)

    start_all(0)

    def extra_round(r, carry):
        wait_all(slot)
        fill(r)
        start_all(r)
        return carry

    lax.fori_loop(1, _num_rounds(cnt_sm, i), extra_round, 0)

    @pl.when(i == pl.num_programs(0) - 1)
    def _zero_tail():
        wait_all(slot)
        zbuf[...] = jnp.zeros(zbuf.shape, zbuf.dtype)

        def zcopy(e):
            off = pl.multiple_of(tot_sm[e], ROW_ALIGN)
            return pltpu.make_async_copy(zbuf, xs_hbm.at[e, pl.ds(off, zbuf.shape[0])], sem.at[e])

        for e in range(ne):
            zcopy(e).start()
        for e in range(ne):
            zcopy(e).wait()


def _gather(h2, rank, start, cnt, tot, cap):
    g, _, d = h2.shape
    nt, ne, tm = rank.shape
    rl = _list_rows(cap, nt)
    return pl.pallas_call(
        _gather_kernel,
        grid_spec=pltpu.PrefetchScalarGridSpec(
            num_scalar_prefetch=3,
            grid=(nt,),
            in_specs=[pl.BlockSpec((g, tm // g, d), lambda i, *_: (0, i, 0)),
                      pl.BlockSpec((1, ne, tm), lambda i, *_: (i, 0, 0))],
            out_specs=pl.BlockSpec(memory_space=pl.ANY),
            scratch_shapes=[pltpu.VMEM((2, ne * SLOT_ROWS, d), BF16),
                            pltpu.VMEM((_expert_block(cap), d), BF16),
                            pltpu.SemaphoreType.DMA((ne,))]),
        out_shape=jax.ShapeDtypeStruct((ne, rl, d), BF16),
        compiler_params=_cparams(("arbitrary",)),
        name="expert_gather",
    )(start, cnt, tot, h2, rank)


def _expert_kernel(*refs, nblocks):
    ns = len(nblocks)
    tot_sms, xs_refs = refs[:ns], refs[ns:2 * ns]
    wg_ref, wu_ref, wd_ref = refs[2 * ns:2 * ns + 3]
    ye_refs = refs[2 * ns + 3:3 * ns + 3]
    wg_bf, wu_bf, wd_bf = refs[3 * ns + 3:]
    e = pl.program_id(0)
    j = pl.program_id(1)

    @pl.when(j == 0)
    def _():
        wg_bf[...] = wg_ref[0].astype(BF16)
        wu_bf[...] = wu_ref[0].astype(BF16)
        wd_bf[...] = wd_ref[0].astype(BF16)

    base = 0
    for s in range(ns):
        jj = j - base
        live = jnp.logical_and(jnp.logical_and(jj >= 0, jj < nblocks[s]), jj * xs_refs[s].shape[1] < tot_sms[s][e])

        @pl.when(live)
        def _(xs_ref=xs_refs[s], ye_ref=ye_refs[s]):
            x = xs_ref[0]
            g = jnp.dot(x, wg_bf[...], preferred_element_type=F32)
            u = jnp.dot(x, wu_bf[...], preferred_element_type=F32)
            hid = (_silu(g) * u).astype(BF16)
            ye_ref[0] = jnp.dot(hid, wd_bf[...], preferred_element_type=F32).astype(ye_ref.dtype)

        base += nblocks[s]


def _experts(sets, w_gate, w_up, w_down):
    ns = len(sets)
    ne, _, d = sets[0][0].shape
    f = w_gate.shape[2]
    nblocks = tuple(used // eb for _, _, used, eb in sets)

    def block_spec(s):
        eb = sets[s][3]
        base = sum(nblocks[:s])

        def index(e, j, *tots):
            return (e, jnp.clip(j - base, 0, (tots[s][e] - 1) // eb), 0)

        return pl.BlockSpec((1, eb, d), index)

    wspec = lambda shape: pl.BlockSpec(shape, lambda e, j, *tots: (e, 0, 0))
    return pl.pallas_call(
        functools.partial(_expert_kernel, nblocks=nblocks),
        grid_spec=pltpu.PrefetchScalarGridSpec(
            num_scalar_prefetch=ns,
            grid=(ne, sum(nblocks)),
            in_specs=[block_spec(s) for s in range(ns)] + [wspec((1, d, f)), wspec((1, d, f)), wspec((1, f, d))],
            out_specs=[block_spec(s) for s in range(ns)],
            scratch_shapes=[pltpu.VMEM((d, f), BF16), pltpu.VMEM((d, f), BF16), pltpu.VMEM((f, d), BF16)]),
        out_shape=[jax.ShapeDtypeStruct((ne, used, d), BF16) for _, _, used, _ in sets],
        compiler_params=_cparams(("arbitrary", "arbitrary")),
        name="expert_ffn",
    )(*[t for _, t, _, _ in sets], *[x for x, _, _, _ in sets], w_gate, w_up, w_down)


def _combine_kernel(start_sm, cnt_sm, tot_sm, x1_ref, rank_ref, aff_ref, g2_ref, nf_ref, ye_hbm, o_ref, buf, sem, *,
                    final_norm, eb):
    i = pl.program_id(0)
    nt = pl.num_programs(0)
    ne = N_EXPERTS
    slot = i % 2
    rank = rank_ref[0]
    aff = aff_ref[0]
    sub = lax.broadcasted_iota(jnp.int32, (SLOT_ROWS, rank.shape[1]), 0)
    dn_t = (((0,), (0,)), ((), ()))

    def window(t, e, r):
        want = start_sm[t, e] + r * SLOT_ROWS
        written = ((tot_sm[e] + eb - 1) // eb) * eb
        off = jnp.minimum(want, written - SLOT_ROWS)
        return pl.multiple_of(off, ROW_ALIGN), want - off

    def copy(t, s, e, r):
        off, _ = window(t, e, r)
        return pltpu.make_async_copy(ye_hbm.at[e, pl.ds(off, SLOT_ROWS)],
                                     buf.at[s, pl.ds(e * SLOT_ROWS, SLOT_ROWS)], sem.at[s, e])

    def weighted(r):
        parts = []
        for e in range(ne):
            _, shift = window(i, e, r)
            local = rank[e:e + 1, :] - r * SLOT_ROWS
            hit = jnp.where(local >= 0, local + shift, -1) == sub
            parts.append(jnp.where(hit, aff[e:e + 1, :], 0.0).astype(BF16))
        return lax.dot_general(jnp.concatenate(parts, axis=0), buf[slot], dn_t, preferred_element_type=F32)

    @pl.when(i == 0)
    def _():
        for e in range(ne):
            copy(0, 0, e, 0).start()

    @pl.when(i + 1 < nt)
    def _prefetch_next_tile():
        for e in range(ne):
            copy(i + 1, 1 - slot, e, 0).start()

    for e in range(ne):
        copy(i, slot, e, 0).wait()
    y0 = weighted(0)

    def extra_round(r, acc):
        for e in range(ne):
            @pl.when(cnt_sm[i, e] > r * SLOT_ROWS)
            def _():
                copy(i, slot, e, r).start()
        for e in range(ne):
            @pl.when(cnt_sm[i, e] > r * SLOT_ROWS)
            def _():
                copy(i, slot, e, r).wait()
        return acc + weighted(r)

    y = lax.fori_loop(1, _num_rounds(cnt_sm, i), extra_round, y0)
    x2 = x1_ref[...] + g2_ref[...] * y.reshape(x1_ref.shape)
    if final_norm:
        ms = jnp.mean(x2 * x2, axis=-1, keepdims=True)
        x2 = x2 * lax.rsqrt(ms + EPS) * nf_ref[...]
    o_ref[...] = x2


def _combine(x1, rank, aff, start, cnt, tot, g2, norm_f_g, ye, cond_block, eb):
    g, s, d = x1.shape
    nt, ne, tm = rank.shape
    tok = pl.BlockSpec((g, tm // g, d), lambda i, *_: (0, i, 0))
    final_norm = norm_f_g is not None
    if not final_norm:
        norm_f_g = jnp.ones((d,), F32)
    return pl.pallas_call(
        functools.partial(_combine_kernel, final_norm=final_norm, eb=eb),
        grid_spec=pltpu.PrefetchScalarGridSpec(
            num_scalar_prefetch=3,
            grid=(nt,),
            in_specs=[tok,
                      pl.BlockSpec((1, ne, tm), lambda i, *_: (i, 0, 0)),
                      pl.BlockSpec((1, ne, tm), lambda i, *_: (i, 0, 0)),
                      pl.BlockSpec((g, 1, d), lambda i, *_: (cond_block, 0, 0)),
                      pl.BlockSpec((1, d), lambda i, *_: (0, 0)),
                      pl.BlockSpec(memory_space=pl.ANY)],
            out_specs=tok,
            scratch_shapes=[pltpu.VMEM((2, ne * SLOT_ROWS, d), BF16),
                            pltpu.SemaphoreType.DMA((2, ne))]),
        out_shape=jax.ShapeDtypeStruct((g, s, d), F32),
        compiler_params=_cparams(("arbitrary",)),
        name="expert_combine",
    )(start, cnt, tot, x1, rank, aff, g2, norm_f_g.reshape(1, d), ye)


def _layer_front(x, batch, seq_len, mods, cond_of_tile, moe_groups, cond_block, p, s0_f, s0_b, rope, emit_state):
    t, d = x.shape
    sh1, sc1, g1, sh2, sc2, g2 = mods
    u, x2c, qkvg, gates = _in_projection(x, p["norm1_g"], sc1, sh1, p["w_in"], p["hy_conv_w"], p["hy_conv_b"],
                                         cond_of_tile, seq_len, rope)
    seq = lambda a: a.reshape(batch, seq_len, a.shape[-1])
    y_hy = _hyena_long_conv(seq(u), seq(x2c), p).reshape(t, -1)
    ret = _retention(qkvg.reshape(batch, seq_len, -1), p["ret_decay_fwd"], p["ret_decay_bwd"], p["ret_gn_g"],
                     s0_f, s0_b, emit_state)
    y_ret = ret[0].reshape(t, -1)
    view = lambda a: a.reshape(moe_groups, t // moe_groups, a.shape[-1])
    x1, h2, aff = _out_projection(view(y_hy), view(y_ret), view(gates), view(x), g1, sc2, sh2, p["norm2_g"],
                                  p["w_hy_o"], p["w_ret_o"], p["w_out"], p["w_router"], cond_block)
    cap = (EC_CAPACITY_FACTOR * t) // N_EXPERTS
    eb = _expert_block(cap)
    rank, cnt, start, tot = _select(aff, cap, moe_groups)
    xs = _gather(h2, rank, start, cnt, tot, cap)
    routed = (xs, tot, _used_rows(cap, rank.shape[0]), eb)
    back = dict(x1=x1, rank=rank, aff=aff, start=start, cnt=cnt, tot=tot, g2=g2, cond_block=cond_block, eb=eb)
    return routed, back, ret[1:]


def _layer_back(back, ye, final_g):
    out = _combine(back["x1"], back["rank"], back["aff"], back["start"], back["cnt"], back["tot"], back["g2"],
                   final_g, ye, back["cond_block"], back["eb"])
    return out.reshape(-1, out.shape[-1])


def kernel(x_prompt, x_sample, state_ret_fwd, state_ret_bwd, c, c_ctx, w_mod, b_mod, norm1_g, w_in, hy_conv_w, hy_conv_b, hy_f_w1, hy_f_b1, hy_f_freq, hy_f_w2, hy_f_b2, hy_f_w3, hy_decay, hy_bias, ret_decay_fwd, ret_decay_bwd, ret_gn_g, w_hy_o, w_ret_o, w_out, norm2_g, w_router, w_e_gate, w_e_up, w_e_down, norm_f_g):
    b, s, d = x_prompt.shape
    bd, sd, _ = x_sample.shape
    depth = w_mod.shape[0]
    assert (b * s) % TOKEN_TILE == 0 and sd % TOKEN_TILE == 0 and b % 2 == 0 and bd % 2 == 0
    ncond = -(-(bd + 1) // 8) * 8
    lat_groups = bd if (TOKEN_TILE % (16 * bd) == 0 and sd % (TOKEN_TILE // bd) == 0) else 1
    assert lat_groups == bd or bd == 1
    conds = jnp.concatenate([c, c_ctx[None], jnp.zeros((ncond - bd - 1, d), F32)], axis=0)
    ctx_cond = lambda i: bd
    lat_cond = lambda i: i // (sd // TOKEN_TILE)
    xp = x_prompt.reshape(b * s, d)
    xs = x_sample.reshape(bd * sd, d)
    new_f, new_b = [], []
    for l in range(depth):
        mod = _modulation(conds, w_mod[l], b_mod[l])
        mods = [mod[:, k * d:(k + 1) * d].reshape(ncond, 1, d) for k in range(6)]
        p = {"norm1_g": norm1_g[l], "w_in": w_in[l].astype(BF16), "hy_conv_w": hy_conv_w[l],
             "hy_conv_b": hy_conv_b[l], "hy_f_w1": hy_f_w1[l], "hy_f_b1": hy_f_b1[l], "hy_f_freq": hy_f_freq[l],
             "hy_f_w2": hy_f_w2[l], "hy_f_b2": hy_f_b2[l], "hy_f_w3": hy_f_w3[l], "hy_decay": hy_decay[l],
             "hy_bias": hy_bias[l], "ret_decay_fwd": ret_decay_fwd[l], "ret_decay_bwd": ret_decay_bwd[l],
             "ret_gn_g": ret_gn_g[l], "w_hy_o": w_hy_o[l].astype(BF16), "w_ret_o": w_ret_o[l].astype(BF16),
             "w_out": w_out[l].astype(BF16), "norm2_g": norm2_g[l], "w_router": w_router[l],
             "w_e_gate": w_e_gate[l], "w_e_up": w_e_up[l], "w_e_down": w_e_down[l]}
        final_g = norm_f_g if l == depth - 1 else None
        routed_p, back_p, (s_f, s_b) = _layer_front(xp, b, s, mods, ctx_cond, 1, bd, p, None, None, False, True)
        new_f.append(s_f)
        new_b.append(s_b)
        routed_s, back_s, _ = _layer_front(xs, bd, sd, mods, lat_cond, lat_groups, 0, p, state_ret_fwd[:, l],
                                           state_ret_bwd[:, l], True, False)
        ye_p, ye_s = _experts([routed_p, routed_s], p["w_e_gate"], p["w_e_up"], p["w_e_down"])
        xp = _layer_back(back_p, ye_p, final_g)
        xs = _layer_back(back_s, ye_s, final_g)
    y_prompt = xp.reshape(b, s, d)
    y_sample = xs.reshape(bd, sd, d)
    return (y_prompt, y_sample, jnp.stack(new_f, axis=1), jnp.stack(new_b, axis=1))
```
